```python
import jax, jax.numpy as jnp
from jax import lax
import numpy as np

D_MODEL = 1024
BATCH = 8
SEQ = 8192
DEPTH = 4

D_A = D_MODEL // 2
D_B = D_MODEL // 2
D_IN_CONV = 2 * D_A + 3 * D_B
CONV_A_WIDTH = 31
CONV_B_WIDTH = 3
HEAD_DIM = 64
N_HEADS = D_MODEL // HEAD_DIM
BLOCK_Q = 128
D_FF = ((8 * D_MODEL // 3 + 127) // 128) * 128
FFN_CONV_WIDTH = 3
N_EVEN = (DEPTH + 1) // 2
N_ODD = DEPTH // 2
EPS = 1e-6

kernel_name = "hybrid_conformer_shortconv_stickbreaking"


def rmsnorm(x, g):
    xf = x.astype(jnp.float32)
    y = xf * lax.rsqrt(jnp.mean(xf * xf, axis=-1, keepdims=True) + EPS)
    return (y * g.astype(jnp.float32)).astype(x.dtype)


def layernorm(x, g, b):
    xf = x.astype(jnp.float32)
    mu = jnp.mean(xf, axis=-1, keepdims=True)
    xc = xf - mu
    y = xc * lax.rsqrt(jnp.mean(xc * xc, axis=-1, keepdims=True) + EPS)
    return (y * g.astype(jnp.float32) + b.astype(jnp.float32)).astype(x.dtype)


def causal_dwconv(x, w, b=None):
    K, C = w.shape
    y = lax.conv_general_dilated(
        x, w[:, None, :].astype(x.dtype), window_strides=(1,),
        padding=[(K - 1, 0)], dimension_numbers=("NWC", "WIO", "NWC"),
        feature_group_count=C)
    if b is not None:
        y = y + b.astype(x.dtype)
    return y


def conv_mixer(h, w_in, a_dw_w, a_dw_b, a_ln_g, a_ln_b, b_dw_w, w_out):
    p = h @ w_in
    a_val, a_gate, b_gb, b_gc, b_h = jnp.split(
        p, [D_A, 2 * D_A, 2 * D_A + D_B, 2 * D_A + 2 * D_B], axis=-1)
    a = causal_dwconv(a_val * jax.nn.sigmoid(a_gate), a_dw_w, a_dw_b)
    a = jax.nn.silu(layernorm(a, a_ln_g, a_ln_b))
    b = b_gb * causal_dwconv(b_gc * b_h, b_dw_w)
    return jnp.concatenate([a, b], axis=-1) @ w_out


def stick_breaking_attention(q, k, v):
    B, H, S, dh = q.shape
    nb = S // BLOCK_Q
    scale = dh ** -0.5
    qb = q.astype(jnp.float32).reshape(B, H, nb, BLOCK_Q, dh).transpose(2, 0, 1, 3, 4)
    kf = k.astype(jnp.float32)
    starts = jnp.arange(nb, dtype=jnp.int32) * BLOCK_Q
    kpos = jnp.arange(S, dtype=jnp.int32)

    def block(args):
        q_blk, start = args
        z = jnp.einsum("bhqd,bhsd->bhqs", q_blk, kf) * scale
        qpos = start + jnp.arange(BLOCK_Q, dtype=jnp.int32)
        valid = kpos[None, :] < qpos[:, None]
        log_keep = jnp.where(valid, jax.nn.log_sigmoid(-z), 0.0)
        rc = lax.cumsum(log_keep, axis=3, reverse=True)
        after = jnp.concatenate([rc[..., 1:], jnp.zeros_like(rc[..., :1])], axis=-1)
        weight = jnp.where(valid, jnp.exp(jax.nn.log_sigmoid(z) + after), 0.0)
        return jnp.einsum("bhqs,bhsd->bhqd", weight.astype(v.dtype), v)

    out = lax.map(block, (qb, starts))
    return out.transpose(1, 2, 0, 3, 4).reshape(B, H, S, dh)


def attn_mixer(h, w_qkv, q_g, k_g, w_o):
    B, S, _ = h.shape
    qkv = (h @ w_qkv).reshape(B, S, 3, N_HEADS, HEAD_DIM)
    q = rmsnorm(qkv[:, :, 0], q_g).transpose(0, 2, 1, 3)
    k = rmsnorm(qkv[:, :, 1], k_g).transpose(0, 2, 1, 3)
    v = qkv[:, :, 2].transpose(0, 2, 1, 3)
    o = stick_breaking_attention(q, k, v)
    return o.transpose(0, 2, 1, 3).reshape(B, S, N_HEADS * HEAD_DIM) @ w_o


def conv_ffn(h, w_up, dw_w, dw_b, w_down):
    u = causal_dwconv(h @ w_up, dw_w, dw_b)
    gate, val = jnp.split(u, 2, axis=-1)
    return (jax.nn.silu(gate) * val) @ w_down


def _fwd_setup_inputs(seed: int = 0) -> dict:
    key = jax.random.key(seed)
    ks = jax.random.split(key, 18)
    f32 = jnp.float32

    def nrm(k, shape, scale):
        return jax.random.normal(k, shape, f32) * scale

    def gain(k, shape):
        return 1.0 + 0.02 * jax.random.normal(k, shape, f32)

    return {
        "x": nrm(ks[0], (BATCH, SEQ, D_MODEL), 1.0),
        "mix_norm_g": gain(ks[1], (DEPTH, D_MODEL)),
        "ffn_norm_g": gain(ks[2], (DEPTH, D_MODEL)),
        "conv_w_in": nrm(ks[3], (N_EVEN, D_MODEL, D_IN_CONV), D_MODEL ** -0.5),
        "conv_a_dw_w": nrm(ks[4], (N_EVEN, CONV_A_WIDTH, D_A), CONV_A_WIDTH ** -0.5),
        "conv_a_dw_b": nrm(ks[5], (N_EVEN, D_A), 0.02),
        "conv_a_ln_g": gain(ks[6], (N_EVEN, D_A)),
        "conv_a_ln_b": nrm(ks[7], (N_EVEN, D_A), 0.02),
        "conv_b_dw_w": nrm(ks[8], (N_EVEN, CONV_B_WIDTH, D_B), CONV_B_WIDTH ** -0.5),
        "conv_w_out": nrm(ks[9], (N_EVEN, D_A + D_B, D_MODEL), (D_A + D_B) ** -0.5),
        "attn_w_qkv": nrm(ks[10], (N_ODD, D_MODEL, 3 * N_HEADS * HEAD_DIM), D_MODEL ** -0.5),
        "attn_q_g": gain(ks[11], (N_ODD, HEAD_DIM)),
        "attn_k_g": gain(ks[12], (N_ODD, HEAD_DIM)),
        "attn_w_o": nrm(ks[13], (N_ODD, N_HEADS * HEAD_DIM, D_MODEL), (N_HEADS * HEAD_DIM) ** -0.5),
        "ffn_w_up": nrm(ks[14], (DEPTH, D_MODEL, 2 * D_FF), D_MODEL ** -0.5),
        "ffn_dw_w": nrm(ks[15], (DEPTH, FFN_CONV_WIDTH, 2 * D_FF), FFN_CONV_WIDTH ** -0.5),
        "ffn_dw_b": nrm(ks[16], (DEPTH, 2 * D_FF), 0.02),
        "ffn_w_down": nrm(ks[17], (DEPTH, D_FF, D_MODEL), D_FF ** -0.5),
    }


def _fwd_reference(x, mix_norm_g, ffn_norm_g, conv_w_in, conv_a_dw_w, conv_a_dw_b,
              conv_a_ln_g, conv_a_ln_b, conv_b_dw_w, conv_w_out, attn_w_qkv,
              attn_q_g, attn_k_g, attn_w_o, ffn_w_up, ffn_dw_w, ffn_dw_b,
              ffn_w_down):
    for layer in range(DEPTH):
        i = layer // 2
        h = rmsnorm(x, mix_norm_g[layer])
        if layer % 2 == 0:
            x = x + conv_mixer(h, conv_w_in[i], conv_a_dw_w[i], conv_a_dw_b[i],
                               conv_a_ln_g[i], conv_a_ln_b[i], conv_b_dw_w[i],
                               conv_w_out[i])
        else:
            x = x + attn_mixer(h, attn_w_qkv[i], attn_q_g[i], attn_k_g[i], attn_w_o[i])
        h = rmsnorm(x, ffn_norm_g[layer])
        x = x + conv_ffn(h, ffn_w_up[layer], ffn_dw_w[layer], ffn_dw_b[layer], ffn_w_down[layer])
    return x


import jax as _jax
import jax.numpy as _jnp

TWIN_FORMAT = 'train_step'
FWD_PARAMS = ['x', 'mix_norm_g', 'ffn_norm_g', 'conv_w_in', 'conv_a_dw_w', 'conv_a_dw_b', 'conv_a_ln_g', 'conv_a_ln_b', 'conv_b_dw_w', 'conv_w_out', 'attn_w_qkv', 'attn_q_g', 'attn_k_g', 'attn_w_o', 'ffn_w_up', 'ffn_dw_w', 'ffn_dw_b', 'ffn_w_down']
TWIN_WEIGHTS = ['mix_norm_g', 'ffn_norm_g', 'conv_w_in', 'conv_a_dw_w', 'conv_a_dw_b', 'conv_a_ln_g', 'conv_a_ln_b', 'conv_b_dw_w', 'conv_w_out', 'attn_w_qkv', 'attn_q_g', 'attn_k_g', 'attn_w_o', 'ffn_w_up', 'ffn_dw_w', 'ffn_dw_b', 'ffn_w_down']
TWIN_DIFF_INPUT = 'x'
TWIN_INPUTS = ['x', 'mix_norm_g', 'ffn_norm_g', 'conv_w_in', 'conv_a_dw_w', 'conv_a_dw_b', 'conv_a_ln_g', 'conv_a_ln_b', 'conv_b_dw_w', 'conv_w_out', 'attn_w_qkv', 'attn_q_g', 'attn_k_g', 'attn_w_o', 'ffn_w_up', 'ffn_dw_w', 'ffn_dw_b', 'ffn_w_down', 'loss_target', 'm_mix_norm_g', 'm_ffn_norm_g', 'm_conv_w_in', 'm_conv_a_dw_w', 'm_conv_a_dw_b', 'm_conv_a_ln_g', 'm_conv_a_ln_b', 'm_conv_b_dw_w', 'm_conv_w_out', 'm_attn_w_qkv', 'm_attn_q_g', 'm_attn_k_g', 'm_attn_w_o', 'm_ffn_w_up', 'm_ffn_dw_w', 'm_ffn_dw_b', 'm_ffn_w_down', 'v_mix_norm_g', 'v_ffn_norm_g', 'v_conv_w_in', 'v_conv_a_dw_w', 'v_conv_a_dw_b', 'v_conv_a_ln_g', 'v_conv_a_ln_b', 'v_conv_b_dw_w', 'v_conv_w_out', 'v_attn_w_qkv', 'v_attn_q_g', 'v_attn_k_g', 'v_attn_w_o', 'v_ffn_w_up', 'v_ffn_dw_w', 'v_ffn_dw_b', 'v_ffn_w_down']
TWIN_OUTPUTS = ['loss', 'grad_x', 'grad_mix_norm_g', 'grad_ffn_norm_g', 'grad_conv_w_in', 'grad_conv_a_dw_w', 'grad_conv_a_dw_b', 'grad_conv_a_ln_g', 'grad_conv_a_ln_b', 'grad_conv_b_dw_w', 'grad_conv_w_out', 'grad_attn_w_qkv', 'grad_attn_q_g', 'grad_attn_k_g', 'grad_attn_w_o', 'grad_ffn_w_up', 'grad_ffn_dw_w', 'grad_ffn_dw_b', 'grad_ffn_w_down', 'delta_mix_norm_g', 'delta_ffn_norm_g', 'delta_conv_w_in', 'delta_conv_a_dw_w', 'delta_conv_a_dw_b', 'delta_conv_a_ln_g', 'delta_conv_a_ln_b', 'delta_conv_b_dw_w', 'delta_conv_w_out', 'delta_attn_w_qkv', 'delta_attn_q_g', 'delta_attn_k_g', 'delta_attn_w_o', 'delta_ffn_w_up', 'delta_ffn_dw_w', 'delta_ffn_dw_b', 'delta_ffn_w_down', 'new_m_mix_norm_g', 'new_m_ffn_norm_g', 'new_m_conv_w_in', 'new_m_conv_a_dw_w', 'new_m_conv_a_dw_b', 'new_m_conv_a_ln_g', 'new_m_conv_a_ln_b', 'new_m_conv_b_dw_w', 'new_m_conv_w_out', 'new_m_attn_w_qkv', 'new_m_attn_q_g', 'new_m_attn_k_g', 'new_m_attn_w_o', 'new_m_ffn_w_up', 'new_m_ffn_dw_w', 'new_m_ffn_dw_b', 'new_m_ffn_w_down', 'new_v_mix_norm_g', 'new_v_ffn_norm_g', 'new_v_conv_w_in', 'new_v_conv_a_dw_w', 'new_v_conv_a_dw_b', 'new_v_conv_a_ln_g', 'new_v_conv_a_ln_b', 'new_v_conv_b_dw_w', 'new_v_conv_w_out', 'new_v_attn_w_qkv', 'new_v_attn_q_g', 'new_v_attn_k_g', 'new_v_attn_w_o', 'new_v_ffn_w_up', 'new_v_ffn_dw_w', 'new_v_ffn_dw_b', 'new_v_ffn_w_down']
TWIN_LEAF_KINDS = {'loss': 'loss', 'grad_x': 'grad_x', 'grad_mix_norm_g': 'grad_w', 'grad_ffn_norm_g': 'grad_w', 'grad_conv_w_in': 'grad_w', 'grad_conv_a_dw_w': 'grad_w', 'grad_conv_a_dw_b': 'grad_w', 'grad_conv_a_ln_g': 'grad_w', 'grad_conv_a_ln_b': 'grad_w', 'grad_conv_b_dw_w': 'grad_w', 'grad_conv_w_out': 'grad_w', 'grad_attn_w_qkv': 'grad_w', 'grad_attn_q_g': 'grad_w', 'grad_attn_k_g': 'grad_w', 'grad_attn_w_o': 'grad_w', 'grad_ffn_w_up': 'grad_w', 'grad_ffn_dw_w': 'grad_w', 'grad_ffn_dw_b': 'grad_w', 'grad_ffn_w_down': 'grad_w', 'delta_mix_norm_g': 'delta_w', 'delta_ffn_norm_g': 'delta_w', 'delta_conv_w_in': 'delta_w', 'delta_conv_a_dw_w': 'delta_w', 'delta_conv_a_dw_b': 'delta_w', 'delta_conv_a_ln_g': 'delta_w', 'delta_conv_a_ln_b': 'delta_w', 'delta_conv_b_dw_w': 'delta_w', 'delta_conv_w_out': 'delta_w', 'delta_attn_w_qkv': 'delta_w', 'delta_attn_q_g': 'delta_w', 'delta_attn_k_g': 'delta_w', 'delta_attn_w_o': 'delta_w', 'delta_ffn_w_up': 'delta_w', 'delta_ffn_dw_w': 'delta_w', 'delta_ffn_dw_b': 'delta_w', 'delta_ffn_w_down': 'delta_w', 'new_m_mix_norm_g': 'new_m', 'new_m_ffn_norm_g': 'new_m', 'new_m_conv_w_in': 'new_m', 'new_m_conv_a_dw_w': 'new_m', 'new_m_conv_a_dw_b': 'new_m', 'new_m_conv_a_ln_g': 'new_m', 'new_m_conv_a_ln_b': 'new_m', 'new_m_conv_b_dw_w': 'new_m', 'new_m_conv_w_out': 'new_m', 'new_m_attn_w_qkv': 'new_m', 'new_m_attn_q_g': 'new_m', 'new_m_attn_k_g': 'new_m', 'new_m_attn_w_o': 'new_m', 'new_m_ffn_w_up': 'new_m', 'new_m_ffn_dw_w': 'new_m', 'new_m_ffn_dw_b': 'new_m', 'new_m_ffn_w_down': 'new_m', 'new_v_mix_norm_g': 'new_v', 'new_v_ffn_norm_g': 'new_v', 'new_v_conv_w_in': 'new_v', 'new_v_conv_a_dw_w': 'new_v', 'new_v_conv_a_dw_b': 'new_v', 'new_v_conv_a_ln_g': 'new_v', 'new_v_conv_a_ln_b': 'new_v', 'new_v_conv_b_dw_w': 'new_v', 'new_v_conv_w_out': 'new_v', 'new_v_attn_w_qkv': 'new_v', 'new_v_attn_q_g': 'new_v', 'new_v_attn_k_g': 'new_v', 'new_v_attn_w_o': 'new_v', 'new_v_ffn_w_up': 'new_v', 'new_v_ffn_dw_w': 'new_v', 'new_v_ffn_dw_b': 'new_v', 'new_v_ffn_w_down': 'new_v'}


def _forward(args):
    return _fwd_reference(*[args[k] for k in FWD_PARAMS])


def _output_shape():
    def fwd():
        inp = _fwd_setup_inputs(0)
        return _fwd_reference(*[inp[k] for k in FWD_PARAMS])
    out = _jax.eval_shape(fwd)
    return out.shape, out.dtype

N_MICROBATCH = 1
ADAM_LR = 0.001
ADAM_B1 = 0.9
ADAM_B2 = 0.999
ADAM_EPS = 1e-08
ADAM_WD = 0.01
ADAM_STEP = 10
PER_EXAMPLE_BATCH_AXIS = {'x': 0, 'loss_target': 0}
SHARED_INPUTS = []
_WEIGHT_DTYPES = {'mix_norm_g': _jnp.float32, 'ffn_norm_g': _jnp.float32, 'conv_w_in': _jnp.float32, 'conv_a_dw_w': _jnp.float32, 'conv_a_dw_b': _jnp.float32, 'conv_a_ln_g': _jnp.float32, 'conv_a_ln_b': _jnp.float32, 'conv_b_dw_w': _jnp.float32, 'conv_w_out': _jnp.float32, 'attn_w_qkv': _jnp.float32, 'attn_q_g': _jnp.float32, 'attn_k_g': _jnp.float32, 'attn_w_o': _jnp.float32, 'ffn_w_up': _jnp.float32, 'ffn_dw_w': _jnp.float32, 'ffn_dw_b': _jnp.float32, 'ffn_w_down': _jnp.float32}
MOMENT_SCALE = {'mix_norm_g': 6.902416e+01, 'ffn_norm_g': 5.045301e+01, 'conv_w_in': 1.890221e+00, 'conv_a_dw_w': 1.830249e+00, 'conv_a_dw_b': 3.001522e+01, 'conv_a_ln_g': 3.116877e+01, 'conv_a_ln_b': 2.573316e+01, 'conv_b_dw_w': 3.564603e+01, 'conv_w_out': 4.537698e+00, 'attn_w_qkv': 1.506046e+00, 'attn_q_g': 6.132219e+01, 'attn_k_g': 6.110265e+01, 'attn_w_o': 2.489574e+00, 'ffn_w_up': 9.835059e-01, 'ffn_dw_w': 7.018854e+00, 'ffn_dw_b': 6.898767e+00, 'ffn_w_down': 1.081674e+00}


def _to_microbatches(a, axis):
    t = _jnp.moveaxis(a, axis, 0)
    t = t.reshape((N_MICROBATCH, t.shape[0] // N_MICROBATCH) + t.shape[1:])
    return _jnp.moveaxis(t, 1, axis + 1)


def setup_inputs(seed: int = 0) -> dict:
    inp = _fwd_setup_inputs(seed)
    key = _jax.random.fold_in(_jax.random.key(seed), 7919)
    shape, _ = _output_shape()
    out = dict(inp)
    out["loss_target"] = _jax.random.normal(_jax.random.fold_in(key, 0), shape, _jnp.float32)
    for i, name in enumerate(TWIN_WEIGHTS):
        w = inp[name].astype(_jnp.float32)
        if MOMENT_SCALE is None:
            s = _jnp.sqrt(_jnp.mean(_jnp.square(w)) + 1e-30)
        else:
            s = MOMENT_SCALE[name]
        km, kv = _jax.random.split(_jax.random.fold_in(key, i + 1))
        out[name] = w
        out["m_" + name] = s * _jax.random.normal(km, w.shape, _jnp.float32)
        out["v_" + name] = (s * s) * _jax.random.uniform(kv, w.shape, _jnp.float32, 0.5, 1.5)
    if N_MICROBATCH > 1:
        for name, axis in PER_EXAMPLE_BATCH_AXIS.items():
            out[name] = _to_microbatches(out[name], axis)
    return {'x': out['x'], 'mix_norm_g': out['mix_norm_g'], 'ffn_norm_g': out['ffn_norm_g'], 'conv_w_in': out['conv_w_in'], 'conv_a_dw_w': out['conv_a_dw_w'], 'conv_a_dw_b': out['conv_a_dw_b'], 'conv_a_ln_g': out['conv_a_ln_g'], 'conv_a_ln_b': out['conv_a_ln_b'], 'conv_b_dw_w': out['conv_b_dw_w'], 'conv_w_out': out['conv_w_out'], 'attn_w_qkv': out['attn_w_qkv'], 'attn_q_g': out['attn_q_g'], 'attn_k_g': out['attn_k_g'], 'attn_w_o': out['attn_w_o'], 'ffn_w_up': out['ffn_w_up'], 'ffn_dw_w': out['ffn_dw_w'], 'ffn_dw_b': out['ffn_dw_b'], 'ffn_w_down': out['ffn_w_down'], 'loss_target': out['loss_target'], 'm_mix_norm_g': out['m_mix_norm_g'], 'm_ffn_norm_g': out['m_ffn_norm_g'], 'm_conv_w_in': out['m_conv_w_in'], 'm_conv_a_dw_w': out['m_conv_a_dw_w'], 'm_conv_a_dw_b': out['m_conv_a_dw_b'], 'm_conv_a_ln_g': out['m_conv_a_ln_g'], 'm_conv_a_ln_b': out['m_conv_a_ln_b'], 'm_conv_b_dw_w': out['m_conv_b_dw_w'], 'm_conv_w_out': out['m_conv_w_out'], 'm_attn_w_qkv': out['m_attn_w_qkv'], 'm_attn_q_g': out['m_attn_q_g'], 'm_attn_k_g': out['m_attn_k_g'], 'm_attn_w_o': out['m_attn_w_o'], 'm_ffn_w_up': out['m_ffn_w_up'], 'm_ffn_dw_w': out['m_ffn_dw_w'], 'm_ffn_dw_b': out['m_ffn_dw_b'], 'm_ffn_w_down': out['m_ffn_w_down'], 'v_mix_norm_g': out['v_mix_norm_g'], 'v_ffn_norm_g': out['v_ffn_norm_g'], 'v_conv_w_in': out['v_conv_w_in'], 'v_conv_a_dw_w': out['v_conv_a_dw_w'], 'v_conv_a_dw_b': out['v_conv_a_dw_b'], 'v_conv_a_ln_g': out['v_conv_a_ln_g'], 'v_conv_a_ln_b': out['v_conv_a_ln_b'], 'v_conv_b_dw_w': out['v_conv_b_dw_w'], 'v_conv_w_out': out['v_conv_w_out'], 'v_attn_w_qkv': out['v_attn_w_qkv'], 'v_attn_q_g': out['v_attn_q_g'], 'v_attn_k_g': out['v_attn_k_g'], 'v_attn_w_o': out['v_attn_w_o'], 'v_ffn_w_up': out['v_ffn_w_up'], 'v_ffn_dw_w': out['v_ffn_dw_w'], 'v_ffn_dw_b': out['v_ffn_dw_b'], 'v_ffn_w_down': out['v_ffn_w_down']}


def _loss(weights, diff, rest, loss_target):
    with _jax.named_scope("forward"):
        args = {**rest, TWIN_DIFF_INPUT: diff, **{k: w.astype(_WEIGHT_DTYPES[k]) for k, w in weights.items()}}
        y = _forward(args)
    with _jax.named_scope("loss_head"):
        err = _jnp.square(y.astype(_jnp.float32) - loss_target)
        return 0.5 * _jnp.sum(_jnp.mean(err, axis=-1)) if err.ndim else 0.5 * err


def _adamw(w, g, m, v):
    m = ADAM_B1 * m + (1.0 - ADAM_B1) * g
    v = ADAM_B2 * v + (1.0 - ADAM_B2) * _jnp.square(g)
    m_hat = m / (1.0 - ADAM_B1 ** ADAM_STEP)
    v_hat = v / (1.0 - ADAM_B2 ** ADAM_STEP)
    delta = -ADAM_LR * (m_hat / (_jnp.sqrt(v_hat) + ADAM_EPS) + ADAM_WD * w)
    return delta, m, v


def reference(x, mix_norm_g, ffn_norm_g, conv_w_in, conv_a_dw_w, conv_a_dw_b, conv_a_ln_g, conv_a_ln_b, conv_b_dw_w, conv_w_out, attn_w_qkv, attn_q_g, attn_k_g, attn_w_o, ffn_w_up, ffn_dw_w, ffn_dw_b, ffn_w_down, loss_target, m_mix_norm_g, m_ffn_norm_g, m_conv_w_in, m_conv_a_dw_w, m_conv_a_dw_b, m_conv_a_ln_g, m_conv_a_ln_b, m_conv_b_dw_w, m_conv_w_out, m_attn_w_qkv, m_attn_q_g, m_attn_k_g, m_attn_w_o, m_ffn_w_up, m_ffn_dw_w, m_ffn_dw_b, m_ffn_w_down, v_mix_norm_g, v_ffn_norm_g, v_conv_w_in, v_conv_a_dw_w, v_conv_a_dw_b, v_conv_a_ln_g, v_conv_a_ln_b, v_conv_b_dw_w, v_conv_w_out, v_attn_w_qkv, v_attn_q_g, v_attn_k_g, v_attn_w_o, v_ffn_w_up, v_ffn_dw_w, v_ffn_dw_b, v_ffn_w_down):
    given = dict(x=x, mix_norm_g=mix_norm_g, ffn_norm_g=ffn_norm_g, conv_w_in=conv_w_in, conv_a_dw_w=conv_a_dw_w, conv_a_dw_b=conv_a_dw_b, conv_a_ln_g=conv_a_ln_g, conv_a_ln_b=conv_a_ln_b, conv_b_dw_w=conv_b_dw_w, conv_w_out=conv_w_out, attn_w_qkv=attn_w_qkv, attn_q_g=attn_q_g, attn_k_g=attn_k_g, attn_w_o=attn_w_o, ffn_w_up=ffn_w_up, ffn_dw_w=ffn_dw_w, ffn_dw_b=ffn_dw_b, ffn_w_down=ffn_w_down, loss_target=loss_target, m_mix_norm_g=m_mix_norm_g, m_ffn_norm_g=m_ffn_norm_g, m_conv_w_in=m_conv_w_in, m_conv_a_dw_w=m_conv_a_dw_w, m_conv_a_dw_b=m_conv_a_dw_b, m_conv_a_ln_g=m_conv_a_ln_g, m_conv_a_ln_b=m_conv_a_ln_b, m_conv_b_dw_w=m_conv_b_dw_w, m_conv_w_out=m_conv_w_out, m_attn_w_qkv=m_attn_w_qkv, m_attn_q_g=m_attn_q_g, m_attn_k_g=m_attn_k_g, m_attn_w_o=m_attn_w_o, m_ffn_w_up=m_ffn_w_up, m_ffn_dw_w=m_ffn_dw_w, m_ffn_dw_b=m_ffn_dw_b, m_ffn_w_down=m_ffn_w_down, v_mix_norm_g=v_mix_norm_g, v_ffn_norm_g=v_ffn_norm_g, v_conv_w_in=v_conv_w_in, v_conv_a_dw_w=v_conv_a_dw_w, v_conv_a_dw_b=v_conv_a_dw_b, v_conv_a_ln_g=v_conv_a_ln_g, v_conv_a_ln_b=v_conv_a_ln_b, v_conv_b_dw_w=v_conv_b_dw_w, v_conv_w_out=v_conv_w_out, v_attn_w_qkv=v_attn_w_qkv, v_attn_q_g=v_attn_q_g, v_attn_k_g=v_attn_k_g, v_attn_w_o=v_attn_w_o, v_ffn_w_up=v_ffn_w_up, v_ffn_dw_w=v_ffn_dw_w, v_ffn_dw_b=v_ffn_dw_b, v_ffn_w_down=v_ffn_w_down)
    weights = {n: given[n] for n in TWIN_WEIGHTS}
    shared = {n: given[n] for n in SHARED_INPUTS}
    per_example = {n: given[n] for n in ['x']}
    grad_fn = _jax.value_and_grad(_loss, argnums=(0, 1))

    def one_microbatch(ex, loss_target):
        ex = dict(ex)
        diff = ex.pop(TWIN_DIFF_INPUT)
        return grad_fn(weights, diff, {**shared, **ex}, loss_target)

    if N_MICROBATCH == 1:
        loss, (grad_w, grad_x) = one_microbatch(per_example, given["loss_target"])
    else:
        def body(carry, xs):
            loss_sum, grad_sum = carry
            l_k, (gw_k, gx_k) = one_microbatch(xs[0], xs[1])
            with _jax.named_scope("update"):
                return (loss_sum + l_k, _jax.tree.map(_jnp.add, grad_sum, gw_k)), gx_k

        init = (_jnp.zeros((), _jnp.float32), _jax.tree.map(_jnp.zeros_like, weights))
        (loss, grad_w), grad_x = _jax.lax.scan(body, init, (per_example, given["loss_target"]))
    with _jax.named_scope("update"):
        delta_w, new_m, new_v = {}, {}, {}
        for n in TWIN_WEIGHTS:
            delta_w[n], new_m[n], new_v[n] = _adamw(weights[n], grad_w[n], given["m_" + n], given["v_" + n])
    return (loss, grad_x, *[grad_w[n] for n in TWIN_WEIGHTS], *[delta_w[n] for n in TWIN_WEIGHTS],
            *[new_m[n] for n in TWIN_WEIGHTS], *[new_v[n] for n in TWIN_WEIGHTS])
```

```python
import jax
import jax.numpy as jnp
from jax import lax
from jax.experimental import pallas as pl
from jax.experimental.pallas import tpu as pltpu

F32 = jnp.float32
BF16 = jnp.bfloat16
EPS = 1e-6
CONV_A_WIDTH = 31
CONV_B_WIDTH = 3
FFN_CONV_WIDTH = 3
HEAD_DIM = 64
ADAM_LR = 0.001
ADAM_B1 = 0.9
ADAM_B2 = 0.999
ADAM_EPS = 1e-08
ADAM_WD = 0.01
ADAM_STEP = 10

LANES = 128
SUBLANES = 8
BF16_ROWS = 16
V7X_VMEM_BYTES = 64 * 1024 * 1024
VMEM_LIMIT_BYTES = V7X_VMEM_BYTES * 3 // 4
N_CHIPS = 4
N_DEV = 8
HALO_A = 32
HALO_S = 8
ATTN_BLOCK = 128
EXP_UNDERFLOW = -104.0
MESH = pl.DeviceIdType.MESH
ANY = pl.BlockSpec(memory_space=pl.ANY)
NT = (((1,), (1,)), ((), ()))
NN = (((1,), (0,)), ((), ()))
TN = (((0,), (0,)), ((), ()))


def _pcall(body, **kw):
    return pl.pallas_call(body, **kw)


def _cp(*sem):
    return pltpu.CompilerParams(dimension_semantics=sem, vmem_limit_bytes=VMEM_LIMIT_BYTES)


def _sds(shape, dtype):
    return jax.ShapeDtypeStruct(tuple(shape), dtype)


def _tile(n, cap, align=LANES):
    if n <= cap:
        return n
    for t in range(cap - cap % align, 0, -align):
        if n % t == 0:
            return t
    return n


def _sig(x):
    return 1.0 / (1.0 + jnp.exp(-x))


def _rowsum(x):
    return jnp.sum(x, axis=0, keepdims=True)


def _mm_call(name, dn, operands, in_specs, out_shape, out_spec, grid, nk, acc_shape, has_res, has_alias):
    def body(*refs):
        a_ref, b_ref = refs[0], refs[1]
        pos = 2
        res_ref = refs[pos] if has_res else None
        pos += int(has_res) + int(has_alias)
        o_ref = refs[pos]
        acc_ref = refs[pos + 1] if nk > 1 else None
        p = lax.dot_general(a_ref[...].astype(BF16), b_ref[...].astype(BF16), dn, preferred_element_type=F32)

        def finish(v):
            if has_res:
                v = v + res_ref[...]
            o_ref[...] = v.astype(o_ref.dtype)

        if nk == 1:
            finish(p)
        else:
            k = pl.program_id(2)

            @pl.when(k == 0)
            def _():
                acc_ref[...] = p

            @pl.when(k > 0)
            def _():
                acc_ref[...] += p

            @pl.when(k == nk - 1)
            def _():
                finish(acc_ref[...])

    aliases = {len(operands) - 1: 0} if has_alias else {}
    return _pcall(
        body, grid=grid, in_specs=in_specs, out_specs=out_spec, out_shape=out_shape,
        scratch_shapes=[pltpu.VMEM(acc_shape, F32)] if nk > 1 else [],
        input_output_aliases=aliases, compiler_params=_cp("parallel", "parallel", "arbitrary"), name=name,
    )(*operands)


def _mm_fwd(name, a, w, l, *, colshard, res=None, out_split=1):
    M, K = a.shape
    tm = _tile(M, 512, BF16_ROWS)
    if colshard:
        cs = w.shape[3]
        N, tn, tk = N_CHIPS * cs, cs, K
        b_spec = pl.BlockSpec((None, None, tk, tn), lambda j, i, k: (l, j, k, 0))
    else:
        N = w.shape[2]
        tn, tk = _tile(N, 1024), _tile(K, 1536)
        b_spec = pl.BlockSpec((None, tk, tn), lambda j, i, k: (l, k, j))
    nk = K // tk
    in_specs = [pl.BlockSpec((tm, tk), lambda j, i, k: (i, k)), b_spec]
    operands = [a, w]
    if res is not None:
        in_specs.append(pl.BlockSpec((tm, tn), lambda j, i, k: (i, j)))
        operands.append(res)
    if out_split == 1:
        out_shape = _sds((M, N), F32)
        out_spec = pl.BlockSpec((tm, tn), lambda j, i, k: (i, j))
    else:
        per = N // tn // out_split
        out_shape = _sds((out_split, M, N // out_split), F32)
        out_spec = pl.BlockSpec((None, tm, tn), lambda j, i, k: (j // per, i, j % per))
    return _mm_call(name, NN, operands, in_specs, out_shape, out_spec, (N // tn, M // tm, nk), nk, (tm, tn),
                    res is not None, False)


def _mm_dgrad(name, g, w, l, *, colshard):
    split = g.ndim == 3
    M = g.shape[-2]
    tm = _tile(M, 512, BF16_ROWS)
    if colshard:
        kw, cs = w.shape[2], w.shape[3]
        tn, tk, nk = _tile(kw, 1408), cs, N_CHIPS
        b_spec = pl.BlockSpec((None, None, tn, tk), lambda j, i, k: (l, k, j, 0))
    else:
        kw, ncon = w.shape[1], w.shape[2]
        tn, tk = _tile(kw, 1408), _tile(ncon, 1536)
        nk = ncon // tk
        b_spec = pl.BlockSpec((None, tn, tk), lambda j, i, k: (l, j, k))
    if split:
        per = nk // g.shape[0]
        a_spec = pl.BlockSpec((None, tm, tk), lambda j, i, k: (k // per, i, k % per))
    else:
        a_spec = pl.BlockSpec((tm, tk), lambda j, i, k: (i, k))
    out_shape = _sds((M, kw), F32)
    out_spec = pl.BlockSpec((tm, tn), lambda j, i, k: (i, j))
    return _mm_call(name, NT, [g, w], [a_spec, b_spec], out_shape, out_spec, (kw // tn, M // tm, nk), nk, (tm, tn),
                    False, False)


def _mm_wgrad(name, a, g, l, n_layers, buf, *, colshard):
    S, M = a.shape
    split = g.ndim == 3
    N = g.shape[-1] * (g.shape[0] if split else 1)
    tk = _tile(S, 512, BF16_ROWS)
    nk = S // tk
    tm = _tile(M, 1408)
    if colshard:
        tn = N // N_CHIPS
        out_shape = _sds((n_layers, N_CHIPS, M, tn), F32)
        out_spec = pl.BlockSpec((None, None, tm, tn), lambda j, i, k: (l, j, i, 0))
    else:
        tn = _tile(N, 1024)
        out_shape = _sds((n_layers, M, N), F32)
        out_spec = pl.BlockSpec((None, tm, tn), lambda j, i, k: (l, i, j))
    if split:
        per = N // tn // g.shape[0]
        b_spec = pl.BlockSpec((None, tk, tn), lambda j, i, k: (j // per, k, j % per))
    else:
        b_spec = pl.BlockSpec((tk, tn), lambda j, i, k: (k, j))
    in_specs = [pl.BlockSpec((tk, tm), lambda j, i, k: (k, i)), b_spec]
    operands = [a, g]
    if buf is not None:
        in_specs.append(ANY)
        operands.append(buf)
    return _mm_call(name, TN, operands, in_specs, out_shape, out_spec, (N // tn, M // tm, nk), nk, (tm, tn),
                    False, buf is not None)


def _rms_fwd(name, x, g, l):
    S, D = x.shape
    tm = _tile(S, 512, BF16_ROWS)

    def body(x_ref, g_ref, o_ref):
        xf = x_ref[...]
        r = lax.rsqrt(jnp.mean(xf * xf, axis=-1, keepdims=True) + EPS)
        o_ref[...] = (xf * r * g_ref[l:l + 1, :]).astype(BF16)

    return _pcall(
        body, grid=(S // tm,),
        in_specs=[pl.BlockSpec((tm, D), lambda i: (i, 0)), pl.BlockSpec(g.shape, lambda i: (0, 0))],
        out_specs=pl.BlockSpec((tm, D), lambda i: (i, 0)), out_shape=_sds((S, D), BF16),
        compiler_params=_cp("parallel"), name=name,
    )(x, g)


def _rms_bwd(name, x, g, l, dh, dres):
    S, D = x.shape
    tm = _tile(S, 512, SUBLANES)

    def body(x_ref, g_ref, dh_ref, dr_ref, dx_ref, dg_ref):
        xf = x_ref[...]
        r = lax.rsqrt(jnp.mean(xf * xf, axis=-1, keepdims=True) + EPS)
        xh = xf * r
        d = dh_ref[...]
        dxh = d * g_ref[l:l + 1, :]
        dx_ref[...] = dr_ref[...] + r * (dxh - xh * jnp.mean(dxh * xh, axis=-1, keepdims=True))

        @pl.when(pl.program_id(0) == 0)
        def _():
            dg_ref[...] = jnp.zeros_like(dg_ref)

        dg_ref[...] += _rowsum(d * xh)

    row = pl.BlockSpec((tm, D), lambda i: (i, 0))
    return _pcall(
        body, grid=(S // tm,),
        in_specs=[row, pl.BlockSpec(g.shape, lambda i: (0, 0)), row, row],
        out_specs=[row, pl.BlockSpec((1, D), lambda i: (0, 0))],
        out_shape=[_sds((S, D), F32), _sds((1, D), F32)],
        compiler_params=_cp("arbitrary"), name=name,
    )(x, g, dh, dres)


def _loss_fwd_bwd(name, y, t):
    S, D = y.shape
    tm = _tile(S, 512, SUBLANES)

    def body(y_ref, t_ref, dy_ref, l_ref):
        e = y_ref[...] - t_ref[...]
        dy_ref[...] = e * (1.0 / D)

        @pl.when(pl.program_id(0) == 0)
        def _():
            l_ref[...] = jnp.zeros_like(l_ref)

        l_ref[...] += 0.5 * jnp.sum(jnp.sum(e * e, axis=-1, keepdims=True) * (1.0 / D), axis=0, keepdims=True)

    row = pl.BlockSpec((tm, D), lambda i: (i, 0))
    return _pcall(
        body, grid=(S // tm,), in_specs=[row, row],
        out_specs=[row, pl.BlockSpec((SUBLANES, LANES), lambda i: (0, 0))],
        out_shape=[_sds((S, D), F32), _sds((SUBLANES, LANES), F32)],
        compiler_params=_cp("arbitrary"), name=name,
    )(y, t)


def _convmix_fwd(name, p, aw, ab, lg, lb, bw, l):
    S, W = p.shape
    dg = W // 5
    tm = _tile(S, 256, HALO_A)
    nb = tm // HALO_A
    ka, kb = CONV_A_WIDTH, CONV_B_WIDTH

    def body(p_ref, ph_ref, aw_ref, ab_ref, lg_ref, lb_ref, bw_ref, o_ref, uext, mext):
        first = pl.program_id(0) == 0
        ph = ph_ref[...]
        pc = p_ref[...]
        uext[pl.ds(0, HALO_A), :] = jnp.where(first, 0.0, ph[:, 0:dg] * _sig(ph[:, dg:2 * dg]))
        uext[pl.ds(HALO_A, tm), :] = pc[:, 0:dg] * _sig(pc[:, dg:2 * dg])
        mext[pl.ds(0, HALO_A), :] = jnp.where(first, 0.0, ph[:, 3 * dg:4 * dg] * ph[:, 4 * dg:5 * dg])
        mext[pl.ds(HALO_A, tm), :] = pc[:, 3 * dg:4 * dg] * pc[:, 4 * dg:5 * dg]
        acc = jnp.zeros((tm, dg), F32) + ab_ref[l:l + 1, :]
        for k in range(ka):
            acc = acc + aw_ref[l, k:k + 1, :] * uext[pl.ds(HALO_A - (ka - 1) + k, tm), :]
        mu = jnp.mean(acc, axis=-1, keepdims=True)
        xc = acc - mu
        ln = xc * lax.rsqrt(jnp.mean(xc * xc, axis=-1, keepdims=True) + EPS) * lg_ref[l:l + 1, :] + lb_ref[l:l + 1, :]
        o_ref[:, 0:dg] = (ln * _sig(ln)).astype(BF16)
        cb = jnp.zeros((tm, dg), F32)
        for k in range(kb):
            cb = cb + bw_ref[l, k:k + 1, :] * mext[pl.ds(HALO_A - (kb - 1) + k, tm), :]
        o_ref[:, dg:2 * dg] = (pc[:, 2 * dg:3 * dg] * cb).astype(BF16)

    full = lambda a: pl.BlockSpec(a.shape, lambda i: (0,) * a.ndim)
    return _pcall(
        body, grid=(S // tm,),
        in_specs=[pl.BlockSpec((tm, W), lambda i: (i, 0)),
                  pl.BlockSpec((HALO_A, W), lambda i: (jnp.maximum(i * nb - 1, 0), 0)),
                  full(aw), full(ab), full(lg), full(lb), full(bw)],
        out_specs=pl.BlockSpec((tm, 2 * dg), lambda i: (i, 0)), out_shape=_sds((S, 2 * dg), BF16),
        scratch_shapes=[pltpu.VMEM((HALO_A + tm, dg), F32), pltpu.VMEM((HALO_A + tm, dg), F32)],
        compiler_params=_cp("parallel"), name=name,
    )(p, p, aw, ab, lg, lb, bw)


def _convmix_bwd(name, p, dab, aw, ab, lg, lb, bw, l):
    S, W = p.shape
    dg = W // 5
    tm = _tile(S, 256, HALO_A)
    nb = tm // HALO_A
    n_i = S // tm
    ka, kb = CONV_A_WIDTH, CONV_B_WIDTH
    n = tm + HALO_A
    ext = HALO_A + n

    def body(p_ref, pp_ref, pn_ref, d_ref, dn_ref, aw_ref, ab_ref, lg_ref, lb_ref, bw_ref,
             dp_ref, daw_ref, dab_ref, dlg_ref, dlb_ref, dbw_ref, uext, mext, gext, dcext, dbext):
        i = pl.program_id(0)
        first, last = i == 0, i == n_i - 1

        @pl.when(first)
        def _():
            for r in (daw_ref, dab_ref, dlg_ref, dlb_ref, dbw_ref):
                r[...] = jnp.zeros_like(r)

        pp, pc, pn = pp_ref[...], p_ref[...], pn_ref[...]
        glu = lambda b: b[:, 0:dg] * _sig(b[:, dg:2 * dg])
        gch = lambda b: b[:, 3 * dg:4 * dg] * b[:, 4 * dg:5 * dg]
        uext[pl.ds(0, HALO_A), :] = jnp.where(first, 0.0, glu(pp))
        uext[pl.ds(HALO_A, tm), :] = glu(pc)
        uext[pl.ds(HALO_A + tm, HALO_A), :] = glu(pn)
        mext[pl.ds(0, HALO_A), :] = jnp.where(first, 0.0, gch(pp))
        mext[pl.ds(HALO_A, tm), :] = gch(pc)
        mext[pl.ds(HALO_A + tm, HALO_A), :] = gch(pn)

        c = jnp.zeros((n, dg), F32) + ab_ref[l:l + 1, :]
        for k in range(ka):
            c = c + aw_ref[l, k:k + 1, :] * uext[pl.ds(HALO_A - (ka - 1) + k, n), :]
        xc = c - jnp.mean(c, axis=-1, keepdims=True)
        rstd = lax.rsqrt(jnp.mean(xc * xc, axis=-1, keepdims=True) + EPS)
        chat = xc * rstd
        g_ln = lg_ref[l:l + 1, :]
        ln = chat * g_ln + lb_ref[l:l + 1, :]
        s = _sig(ln)
        gext[pl.ds(0, tm), :] = d_ref[:, 0:dg]
        gext[pl.ds(tm, HALO_A), :] = jnp.where(last, 0.0, dn_ref[:, 0:dg])
        dln = gext[...] * (s * (1.0 + ln * (1.0 - s)))
        dlnh = dln * g_ln
        dc = rstd * (dlnh - jnp.mean(dlnh, axis=-1, keepdims=True) - chat * jnp.mean(dlnh * chat, axis=-1, keepdims=True))
        dcext[...] = dc
        dlg_ref[...] += _rowsum((dln * chat)[0:tm])
        dlb_ref[...] += _rowsum(dln[0:tm])
        dab_ref[...] += _rowsum(dc[0:tm])
        du = jnp.zeros((tm, dg), F32)
        for k in range(ka):
            du = du + aw_ref[l, k:k + 1, :] * dcext[pl.ds(ka - 1 - k, tm), :]
            daw_ref[k:k + 1, :] += _rowsum(dcext[pl.ds(0, tm), :] * uext[pl.ds(HALO_A - (ka - 1) + k, tm), :])
        sg = _sig(pc[:, dg:2 * dg])
        dp_ref[:, 0:dg] = (du * sg).astype(BF16)
        dp_ref[:, dg:2 * dg] = (du * pc[:, 0:dg] * sg * (1.0 - sg)).astype(BF16)

        cb = jnp.zeros((tm, dg), F32)
        for k in range(kb):
            cb = cb + bw_ref[l, k:k + 1, :] * mext[pl.ds(HALO_A - (kb - 1) + k, tm), :]
        db = d_ref[:, dg:2 * dg]
        dp_ref[:, 2 * dg:3 * dg] = (db * cb).astype(BF16)
        dbext[pl.ds(0, tm), :] = db * pc[:, 2 * dg:3 * dg]
        dbext[pl.ds(tm, HALO_A), :] = jnp.where(last, 0.0, dn_ref[:, dg:2 * dg] * pn[:, 2 * dg:3 * dg])
        dm = jnp.zeros((tm, dg), F32)
        for k in range(kb):
            dm = dm + bw_ref[l, k:k + 1, :] * dbext[pl.ds(kb - 1 - k, tm), :]
            dbw_ref[k:k + 1, :] += _rowsum(dbext[pl.ds(0, tm), :] * mext[pl.ds(HALO_A - (kb - 1) + k, tm), :])
        dp_ref[:, 3 * dg:4 * dg] = (dm * pc[:, 4 * dg:5 * dg]).astype(BF16)
        dp_ref[:, 4 * dg:5 * dg] = (dm * pc[:, 3 * dg:4 * dg]).astype(BF16)

    full = lambda a: pl.BlockSpec(a.shape, lambda i: (0,) * a.ndim)
    prev = lambda i: (jnp.maximum(i * nb - 1, 0), 0)
    nxt = lambda i: (jnp.minimum((i + 1) * nb, S // HALO_A - 1), 0)
    acc = lambda r: pl.BlockSpec((r, dg), lambda i: (0, 0))
    return _pcall(
        body, grid=(n_i,),
        in_specs=[pl.BlockSpec((tm, W), lambda i: (i, 0)), pl.BlockSpec((HALO_A, W), prev), pl.BlockSpec((HALO_A, W), nxt),
                  pl.BlockSpec((tm, 2 * dg), lambda i: (i, 0)), pl.BlockSpec((HALO_A, 2 * dg), nxt),
                  full(aw), full(ab), full(lg), full(lb), full(bw)],
        out_specs=[pl.BlockSpec((tm, W), lambda i: (i, 0)), acc(ka), acc(1), acc(1), acc(1), acc(kb)],
        out_shape=[_sds((S, W), BF16), _sds((ka, dg), F32), _sds((1, dg), F32), _sds((1, dg), F32), _sds((1, dg), F32),
                   _sds((kb, dg), F32)],
        scratch_shapes=[pltpu.VMEM((ext, dg), F32), pltpu.VMEM((ext, dg), F32), pltpu.VMEM((n, dg), F32),
                        pltpu.VMEM((n, dg), F32), pltpu.VMEM((n, dg), F32)],
        compiler_params=_cp("arbitrary"), name=name,
    )(p, p, p, dab, dab, aw, ab, lg, lb, bw)


def _ffn_mid_fwd(name, u2, dww, dwb, l):
    _, S, F = u2.shape
    tm = _tile(S, 256, BF16_ROWS)
    tc = _tile(F, 1408)
    n_f = F // tc
    nb = tm // HALO_S
    kf = FFN_CONV_WIDTH

    def body(u_ref, uh_ref, wg_ref, wv_ref, bg_ref, bv_ref, o_ref, ext):
        first = pl.program_id(1) == 0
        ext[:, pl.ds(0, HALO_S), :] = jnp.where(first, 0.0, uh_ref[...])
        ext[:, pl.ds(HALO_S, tm), :] = u_ref[...]

        def conv(g, w_ref, b_ref):
            acc = jnp.zeros((tm, tc), F32) + b_ref[l:l + 1, :]
            for k in range(kf):
                acc = acc + w_ref[k:k + 1, :] * ext[g, pl.ds(HALO_S - (kf - 1) + k, tm), :]
            return acc

        cg = conv(0, wg_ref, bg_ref)
        cv = conv(1, wv_ref, bv_ref)
        o_ref[...] = (cg * _sig(cg) * cv).astype(BF16)

    n_l = dwb.shape[0]
    return _pcall(
        body, grid=(n_f, S // tm),
        in_specs=[pl.BlockSpec((2, tm, tc), lambda j, i: (0, i, j)),
                  pl.BlockSpec((2, HALO_S, tc), lambda j, i: (0, jnp.maximum(i * nb - 1, 0), j)),
                  pl.BlockSpec((None, kf, tc), lambda j, i: (l, 0, j)),
                  pl.BlockSpec((None, kf, tc), lambda j, i: (l, 0, j + n_f)),
                  pl.BlockSpec((n_l, tc), lambda j, i: (0, j)),
                  pl.BlockSpec((n_l, tc), lambda j, i: (0, j + n_f))],
        out_specs=pl.BlockSpec((tm, tc), lambda j, i: (i, j)), out_shape=_sds((S, F), BF16),
        scratch_shapes=[pltpu.VMEM((2, HALO_S + tm, tc), F32)],
        compiler_params=_cp("parallel", "parallel"), name=name,
    )(u2, u2, dww, dww, dwb, dwb)


def _ffn_mid_bwd(name, u2, df, dww, dwb, l):
    _, S, F = u2.shape
    tm = _tile(S, 256, BF16_ROWS)
    tc = _tile(F, 1408)
    n_f = F // tc
    nb = tm // HALO_S
    n_i = S // tm
    kf = FFN_CONV_WIDTH
    n = tm + HALO_S

    def body(u_ref, up_ref, un_ref, df_ref, dfn_ref, wg_ref, wv_ref, bg_ref, bv_ref,
             du_ref, dw_ref, db_ref, uext, dfext, dcext):
        i = pl.program_id(1)
        first, last = i == 0, i == n_i - 1

        @pl.when(first)
        def _():
            dw_ref[...] = jnp.zeros_like(dw_ref)
            db_ref[...] = jnp.zeros_like(db_ref)

        uext[:, pl.ds(0, HALO_S), :] = jnp.where(first, 0.0, up_ref[...])
        uext[:, pl.ds(HALO_S, tm), :] = u_ref[...]
        uext[:, pl.ds(HALO_S + tm, HALO_S), :] = un_ref[...]
        dfext[pl.ds(0, tm), :] = df_ref[...]
        dfext[pl.ds(tm, HALO_S), :] = jnp.where(last, 0.0, dfn_ref[...])

        def conv(g, w_ref, b_ref):
            acc = jnp.zeros((n, tc), F32) + b_ref[l:l + 1, :]
            for k in range(kf):
                acc = acc + w_ref[k:k + 1, :] * uext[g, pl.ds(HALO_S - (kf - 1) + k, n), :]
            return acc

        cg = conv(0, wg_ref, bg_ref)
        cv = conv(1, wv_ref, bv_ref)
        s = _sig(cg)
        dfe = dfext[...]
        dcext[0] = dfe * cv * (s * (1.0 + cg * (1.0 - s)))
        dcext[1] = dfe * (cg * s)
        for g, w_ref in ((0, wg_ref), (1, wv_ref)):
            du = jnp.zeros((tm, tc), F32)
            for k in range(kf):
                du = du + w_ref[k:k + 1, :] * dcext[g, pl.ds(kf - 1 - k, tm), :]
                dw_ref[g, k:k + 1, :] += _rowsum(dcext[g, pl.ds(0, tm), :] * uext[g, pl.ds(HALO_S - (kf - 1) + k, tm), :])
            du_ref[g] = du.astype(BF16)
            db_ref[g] += _rowsum(dcext[g, pl.ds(0, tm), :])

    n_l = dwb.shape[0]
    prev = lambda j, i: (0, jnp.maximum(i * nb - 1, 0), j)
    nxt = lambda j, i: (0, jnp.minimum((i + 1) * nb, S // HALO_S - 1), j)
    return _pcall(
        body, grid=(n_f, n_i),
        in_specs=[pl.BlockSpec((2, tm, tc), lambda j, i: (0, i, j)),
                  pl.BlockSpec((2, HALO_S, tc), prev), pl.BlockSpec((2, HALO_S, tc), nxt),
                  pl.BlockSpec((tm, tc), lambda j, i: (i, j)),
                  pl.BlockSpec((HALO_S, tc), lambda j, i: nxt(j, i)[1:]),
                  pl.BlockSpec((None, kf, tc), lambda j, i: (l, 0, j)),
                  pl.BlockSpec((None, kf, tc), lambda j, i: (l, 0, j + n_f)),
                  pl.BlockSpec((n_l, tc), lambda j, i: (0, j)),
                  pl.BlockSpec((n_l, tc), lambda j, i: (0, j + n_f))],
        out_specs=[pl.BlockSpec((2, tm, tc), lambda j, i: (0, i, j)),
                   pl.BlockSpec((2, kf, tc), lambda j, i: (0, 0, j)),
                   pl.BlockSpec((2, 1, tc), lambda j, i: (0, 0, j))],
        out_shape=[_sds((2, S, F), BF16), _sds((2, kf, F), F32), _sds((2, 1, F), F32)],
        scratch_shapes=[pltpu.VMEM((2, HALO_S + n, tc), F32), pltpu.VMEM((n, tc), F32), pltpu.VMEM((2, n, tc), F32)],
        compiler_params=_cp("parallel", "arbitrary"), name=name,
    )(u2, u2, u2, df, df, dww, dww, dwb, dwb)


def _head_sum_matrix():
    r = lax.broadcasted_iota(jnp.int32, (LANES, LANES), 0) // HEAD_DIM
    c = lax.broadcasted_iota(jnp.int32, (LANES, LANES), 1) // HEAD_DIM
    return (r == c).astype(F32)


def _head_mean(x, ones):
    return jnp.dot(x, ones, preferred_element_type=F32, precision=lax.Precision.HIGHEST) * (1.0 / HEAD_DIM)


def _qknorm_fwd(name, qkv, g2):
    S, D3 = qkv.shape
    D = D3 // 3
    tm = _tile(S, 256, BF16_ROWS)
    scale = HEAD_DIM ** -0.5

    def body(q_ref, k_ref, v_ref, g_ref, qo_ref, ko_ref, vo_ref):
        ones = _head_sum_matrix()
        for cc in range(D // LANES):
            sl = slice(cc * LANES, (cc + 1) * LANES)
            for x_ref, o_ref, row, mult in ((q_ref, qo_ref, 0, scale), (k_ref, ko_ref, 1, 1.0)):
                x = x_ref[:, sl]
                r = lax.rsqrt(_head_mean(x * x, ones) + EPS)
                o_ref[:, sl] = ((x * r * g_ref[row:row + 1, :]).astype(BF16) * mult).astype(BF16)
        vo_ref[...] = v_ref[...].astype(BF16)

    col = lambda c: pl.BlockSpec((tm, D), lambda i: (i, c))
    out = pl.BlockSpec((tm, D), lambda i: (i, 0))
    return _pcall(
        body, grid=(S // tm,),
        in_specs=[col(0), col(1), col(2), pl.BlockSpec(g2.shape, lambda i: (0, 0))],
        out_specs=[out, out, out], out_shape=[_sds((S, D), BF16)] * 3,
        compiler_params=_cp("parallel"), name=name,
    )(qkv, qkv, qkv, g2)


def _qknorm_bwd(name, qkv, dq, dk, dv, g2):
    S, D3 = qkv.shape
    D = D3 // 3
    tm = _tile(S, 256, BF16_ROWS)
    scale = HEAD_DIM ** -0.5

    def body(q_ref, k_ref, dq_ref, dk_ref, dv_ref, g_ref, o_ref, dg_ref):
        @pl.when(pl.program_id(0) == 0)
        def _():
            dg_ref[...] = jnp.zeros_like(dg_ref)

        ones = _head_sum_matrix()
        for cc in range(D // LANES):
            sl = slice(cc * LANES, (cc + 1) * LANES)
            for x_ref, d_ref, row, mult, base in ((q_ref, dq_ref, 0, scale, 0), (k_ref, dk_ref, 1, 1.0, D)):
                x = x_ref[:, sl]
                r = lax.rsqrt(_head_mean(x * x, ones) + EPS)
                xh = x * r
                dn = d_ref[:, sl] * mult
                dxh = dn * g_ref[row:row + 1, :]
                dx = r * (dxh - xh * _head_mean(dxh * xh, ones))
                o_ref[:, base + cc * LANES:base + (cc + 1) * LANES] = dx.astype(BF16)
                dg_ref[row:row + 1, :] += _rowsum(dn * xh)
        o_ref[:, 2 * D:3 * D] = dv_ref[...].astype(BF16)

    col = lambda c: pl.BlockSpec((tm, D), lambda i: (i, c))
    row = pl.BlockSpec((tm, D), lambda i: (i, 0))
    return _pcall(
        body, grid=(S // tm,),
        in_specs=[col(0), col(1), row, row, row, pl.BlockSpec(g2.shape, lambda i: (0, 0))],
        out_specs=[pl.BlockSpec((tm, D3), lambda i: (i, 0)), pl.BlockSpec((2, LANES), lambda i: (0, 0))],
        out_shape=[_sds((S, D3), BF16), _sds((2, LANES), F32)],
        compiler_params=_cp("arbitrary"), name=name,
    )(qkv, qkv, dq, dk, dv, g2)


def _attn_consts():
    t = ATTN_BLOCK
    row = lax.broadcasted_iota(jnp.int32, (t, t), 0)
    col = lax.broadcasted_iota(jnp.int32, (t, t), 1)
    lane = lax.broadcasted_iota(jnp.int32, (1, LANES), 1)
    heads = (lane < HEAD_DIM, lane >= HEAD_DIM)
    return row, col, heads


def _split_dot(x, m):
    hi = x.astype(BF16)
    lo = (x - hi.astype(F32)).astype(BF16)
    return jnp.dot(hi, m, preferred_element_type=F32) + jnp.dot(lo, m, preferred_element_type=F32)


def _log_keep(z):
    return -(jnp.maximum(z, 0.0) + jnp.log(1.0 + jnp.exp(-jnp.abs(z))))


def _attn_fwd(name, qs, kn, vb):
    S, D = qs.shape
    t = ATTN_BLOCK

    def body(q_ref, k_ref, v_ref, o_ref):
        i = pl.program_id(1)
        row, col, heads = _attn_consts()
        after_m = (row > col).astype(BF16)
        causal = col < row
        q = q_ref[...]
        qh = [jnp.where(h, q, jnp.zeros_like(q)) for h in heads]

        def block(j, r, acc, diagonal):
            off = pl.multiple_of(j * t, t)
            kb = k_ref[pl.ds(off, t), :]
            vblk = v_ref[pl.ds(off, t), :]
            r_new = []
            for h in range(2):
                z = lax.dot_general(qh[h], kb, NT, preferred_element_type=F32)
                lk = _log_keep(z)
                if diagonal:
                    lk = jnp.where(causal, lk, 0.0)
                w = jnp.exp(z + lk + _split_dot(lk, after_m) + r[h])
                if diagonal:
                    w = jnp.where(causal, w, 0.0)
                acc = acc + jnp.dot(w.astype(BF16), jnp.where(heads[h], vblk, jnp.zeros_like(vblk)),
                                    preferred_element_type=F32)
                r_new.append(r[h] + jnp.sum(lk, axis=1, keepdims=True))
            return r_new, acc

        zero = jnp.zeros((t, 1), F32)
        r, acc = block(i, [zero, zero], jnp.zeros((t, LANES), F32), True)

        def cond(c):
            j, r0, r1, _ = c
            return jnp.logical_and(j >= 0, jnp.max(jnp.maximum(r0, r1)) > EXP_UNDERFLOW)

        def step(c):
            j, r0, r1, a = c
            rr, a = block(j, [r0, r1], a, False)
            return j - 1, rr[0], rr[1], a

        _, _, _, acc = lax.while_loop(cond, step, (i - 1, r[0], r[1], acc))
        o_ref[...] = acc

    n_hp = D // LANES
    blk = pl.BlockSpec((t, LANES), lambda hp, i: (i, hp))
    seq = pl.BlockSpec((S, LANES), lambda hp, i: (0, hp))
    return _pcall(
        body, grid=(n_hp, S // t), in_specs=[blk, seq, seq], out_specs=blk, out_shape=_sds((S, D), F32),
        compiler_params=_cp("parallel", "arbitrary"), name=name,
    )(qs, kn, vb)


def _attn_bwd(name, qs, kn, vb, o, do):
    S, D = qs.shape
    t = ATTN_BLOCK

    def body(q_ref, k_ref, v_ref, o_ref, do_ref, dq_ref, dk_ref, dv_ref):
        i = pl.program_id(1)

        @pl.when(i == 0)
        def _():
            dk_ref[...] = jnp.zeros_like(dk_ref)
            dv_ref[...] = jnp.zeros_like(dv_ref)

        row, col, heads = _attn_consts()
        after_m = (row > col).astype(BF16)
        from_m = (row >= col).astype(BF16)
        causal = col < row
        q = q_ref[...]
        dob = do_ref[...].astype(BF16)
        qh = [jnp.where(h, q, jnp.zeros_like(q)) for h in heads]
        doh = [jnp.where(h, dob, jnp.zeros_like(dob)) for h in heads]
        prod = dob.astype(F32) * o_ref[...]
        dsum = [jnp.sum(jnp.where(h, prod, 0.0), axis=1, keepdims=True) for h in heads]

        def block(j, r, es, dq, diagonal):
            off = pl.multiple_of(j * t, t)
            kb = k_ref[pl.ds(off, t), :]
            vblk = v_ref[pl.ds(off, t), :]
            r_new, es_new = [], []
            dk = jnp.zeros((t, LANES), F32)
            dv = jnp.zeros((t, LANES), F32)
            for h in range(2):
                z = lax.dot_general(qh[h], kb, NT, preferred_element_type=F32)
                lk = _log_keep(z)
                if diagonal:
                    lk = jnp.where(causal, lk, 0.0)
                ls = z + lk
                w = jnp.exp(ls + _split_dot(lk, after_m) + r[h])
                if diagonal:
                    w = jnp.where(causal, w, 0.0)
                e = w * lax.dot_general(doh[h], vblk, NT, preferred_element_type=F32)
                before = dsum[h] - (es[h] + _split_dot(e, from_m))
                dz = e - (e + before) * jnp.exp(ls)
                if diagonal:
                    dz = jnp.where(causal, dz, 0.0)
                dzb = dz.astype(BF16)
                dq = dq + jnp.dot(dzb, jnp.where(heads[h], kb, jnp.zeros_like(kb)), preferred_element_type=F32)
                dk = dk + lax.dot_general(dzb, qh[h], TN, preferred_element_type=F32)
                dv = dv + lax.dot_general(w.astype(BF16), doh[h], TN, preferred_element_type=F32)
                r_new.append(r[h] + jnp.sum(lk, axis=1, keepdims=True))
                es_new.append(es[h] + jnp.sum(e, axis=1, keepdims=True))
            dk_ref[pl.ds(off, t), :] += dk
            dv_ref[pl.ds(off, t), :] += dv
            return r_new, es_new, dq

        zero = jnp.zeros((t, 1), F32)
        r, es, dq = block(i, [zero, zero], [zero, zero], jnp.zeros((t, LANES), F32), True)

        def cond(c):
            j, r0, r1 = c[0], c[1], c[2]
            return jnp.logical_and(j >= 0, jnp.max(jnp.maximum(r0, r1)) > EXP_UNDERFLOW)

        def step(c):
            j, r0, r1, e0, e1, a = c
            rr, ee, a = block(j, [r0, r1], [e0, e1], a, False)
            return j - 1, rr[0], rr[1], ee[0], ee[1], a

        c = lax.while_loop(cond, step, (i - 1, r[0], r[1], es[0], es[1], dq))
        dq_ref[...] = c[5]

    n_hp = D // LANES
    blk = pl.BlockSpec((t, LANES), lambda hp, i: (i, hp))
    seq = pl.BlockSpec((S, LANES), lambda hp, i: (0, hp))
    return _pcall(
        body, grid=(n_hp, S // t), in_specs=[blk, seq, seq, blk, blk], out_specs=[blk, seq, seq],
        out_shape=[_sds((S, D), F32)] * 3, compiler_params=_cp("parallel", "arbitrary"), name=name,
    )(qs, kn, vb, o, do)


def _adamw(name, w, g, m, v):
    L, R, C = w.shape
    tr = _tile(R, 256, SUBLANES)
    c1 = 1.0 - ADAM_B1 ** ADAM_STEP
    c2 = 1.0 - ADAM_B2 ** ADAM_STEP

    def body(w_ref, g_ref, m_ref, v_ref, d_ref, mo_ref, vo_ref):
        gg = g_ref[...]
        mn = ADAM_B1 * m_ref[...] + (1.0 - ADAM_B1) * gg
        vn = ADAM_B2 * v_ref[...] + (1.0 - ADAM_B2) * (gg * gg)
        d_ref[...] = -ADAM_LR * ((mn / c1) / (jnp.sqrt(vn / c2) + ADAM_EPS) + ADAM_WD * w_ref[...])
        mo_ref[...] = mn
        vo_ref[...] = vn

    blk = pl.BlockSpec((None, tr, C), lambda l, i: (l, i, 0))
    return _pcall(
        body, grid=(L, R // tr), in_specs=[blk] * 4, out_specs=[blk] * 3, out_shape=[_sds(w.shape, F32)] * 3,
        compiler_params=_cp("parallel", "parallel"), name=name,
    )(w, g, m, v)


def _place():
    x, y, c = lax.axis_index("x"), lax.axis_index("y"), lax.axis_index("c")
    chips = [(1 - x, y), (x, 1 - y), (1 - x, 1 - y)]
    return x, y, c, chips


def _all_gather_weights(col_ws, row_ws, small_ws):
    n_col, n_row, n_small = len(col_ws), len(row_ws), len(small_ws)
    n_big = n_col + n_row
    n_in = n_big + n_small

    def body(*refs):
        ins, outs = refs[:n_in], refs[n_in:2 * n_in]
        send_sems, recv_sems, local_sems = refs[2 * n_in:]
        x, y, c, chips = _place()
        j_me = 2 * x + y
        j_of = [2 * cx + cy for cx, cy in chips]
        sibling = (x, y, 1 - c)

        def half_rows(t):
            return (ins[t].shape[1] // 2)

        def piece(t, j, cc):
            rh = half_rows(t)
            if t < n_col:
                return outs[t].at[:, j, pl.ds(pl.multiple_of(cc * rh, BF16_ROWS), rh), :]
            return outs[t].at[:, pl.ds(pl.multiple_of(j * 2 * rh + cc * rh, BF16_ROWS), rh), :]

        def own_slot(t):
            if t < n_col or t >= n_big:
                return outs[t].at[:, j_me]
            rs = ins[t].shape[1]
            return outs[t].at[:, pl.ds(pl.multiple_of(j_me * rs, BF16_ROWS), rs), :]

        def remote(src, dst, s, to):
            return pltpu.make_async_remote_copy(src_ref=src, dst_ref=dst, send_sem=send_sems.at[s], recv_sem=recv_sems.at[s],
                                                device_id=to, device_id_type=MESH)

        started = []
        for t in range(n_in):
            loc = pltpu.make_async_copy(ins[t], own_slot(t), local_sems.at[t])
            loc.start()
            started.append(loc)
        first = []
        for t in range(n_big):
            rh = half_rows(t)
            src = ins[t].at[:, pl.ds(pl.multiple_of(c * rh, BF16_ROWS), rh), :]
            for k in range(3):
                first.append(remote(src, piece(t, j_me, c), 6 * t + k, (*chips[k], c)))
        for t in range(n_big, n_in):
            for k in range(3):
                first.append(remote(ins[t], outs[t].at[:, j_me], 6 * n_big + 3 * (t - n_big) + k, (*chips[k], c)))
        for cp in first:
            cp.start()
        passed = []
        for t in range(n_big):
            for k in range(3):
                landed = piece(t, j_of[k], c)
                remote(landed, landed, 6 * t + k, (*chips[k], c)).wait_recv()
                fwd = remote(landed, landed, 6 * t + 3 + k, sibling)
                fwd.start()
                passed.append(fwd)
        for t in range(n_big):
            for k in range(3):
                other = piece(t, j_of[k], 1 - c)
                remote(other, other, 6 * t + 3 + k, sibling).wait_recv()
        for t in range(n_big, n_in):
            for k in range(3):
                dst = outs[t].at[:, j_of[k]]
                remote(dst, dst, 6 * n_big + 3 * (t - n_big) + k, (*chips[k], c)).wait_recv()
        for cp in first + passed:
            cp.wait_send()
        for loc in started:
            loc.wait()

    out_shape = []
    for w in col_ws:
        out_shape.append(_sds((w.shape[0], N_CHIPS, w.shape[1], w.shape[2]), w.dtype))
    for w in row_ws:
        out_shape.append(_sds((w.shape[0], N_CHIPS * w.shape[1], w.shape[2]), w.dtype))
    for w in small_ws:
        out_shape.append(_sds((w.shape[0], N_CHIPS, w.shape[1], w.shape[2]), w.dtype))
    n_sem = 6 * n_big + 3 * n_small
    outs = _pcall(
        body, in_specs=[ANY] * n_in, out_specs=[ANY] * n_in, out_shape=out_shape,
        scratch_shapes=[pltpu.SemaphoreType.DMA((n_sem,)), pltpu.SemaphoreType.DMA((n_sem,)), pltpu.SemaphoreType.DMA((n_in,))],
        name="all_gather_weights",
    )(*col_ws, *row_ws, *small_ws)
    return outs[:n_col], outs[n_col:n_big], outs[n_big:]


def _exchange_core_halves(grads):
    n = len(grads)

    def body(*refs):
        ins, outs = refs[:n], refs[n:2 * n]
        send_sems, recv_sems = refs[2 * n:]
        x, y, c, _ = _place()
        cps = [pltpu.make_async_remote_copy(src_ref=ins[t].at[:, :, 1 - c], dst_ref=outs[t], send_sem=send_sems.at[t],
                                            recv_sem=recv_sems.at[t], device_id=(x, y, 1 - c), device_id_type=MESH)
               for t in range(n)]
        for cp in cps:
            cp.start()
        for cp in cps:
            cp.wait()

    return _pcall(
        body, in_specs=[ANY] * n, out_specs=[ANY] * n,
        out_shape=[_sds((g.shape[0], g.shape[1], g.shape[3], g.shape[4]), F32) for g in grads],
        scratch_shapes=[pltpu.SemaphoreType.DMA((n,)), pltpu.SemaphoreType.DMA((n,))],
        name="grad_exchange_core_halves",
    )(*grads)


def _add_core_halves(name, g, a, c_idx):
    L, nj, _, rh, X = g.shape
    tr = _tile(rh, 256, SUBLANES)

    def body(c_ref, g_ref, a_ref, o_ref):
        o_ref[...] = g_ref[...] + a_ref[...]

    blk = pl.BlockSpec((None, None, tr, X), lambda l, j, i, c_ref: (l, j, i, 0))
    return _pcall(
        body,
        grid_spec=pltpu.PrefetchScalarGridSpec(
            num_scalar_prefetch=1, grid=(L, nj, rh // tr),
            in_specs=[pl.BlockSpec((None, None, None, tr, X), lambda l, j, i, c_ref: (l, j, c_ref[0], i, 0)), blk],
            out_specs=blk),
        out_shape=_sds((L, nj, rh, X), F32), compiler_params=_cp("parallel", "parallel", "parallel"), name=name,
    )(c_idx, g, a)


def _exchange_chip_shards(parts):
    n = len(parts)

    def body(*refs):
        ins, outs = refs[:n], refs[n:2 * n]
        send_sems, recv_sems = refs[2 * n:]
        x, y, c, chips = _place()
        cps = []
        for t in range(n):
            for k, (cx, cy) in enumerate(chips):
                cps.append(pltpu.make_async_remote_copy(
                    src_ref=ins[t].at[:, 2 * cx + cy], dst_ref=outs[t].at[k], send_sem=send_sems.at[3 * t + k],
                    recv_sem=recv_sems.at[3 * t + k], device_id=(cx, cy, c), device_id_type=MESH))
        for cp in cps:
            cp.start()
        for cp in cps:
            cp.wait()

    return _pcall(
        body, in_specs=[ANY] * n, out_specs=[ANY] * n,
        out_shape=[_sds((3, p.shape[0], p.shape[2], p.shape[3]), F32) for p in parts],
        scratch_shapes=[pltpu.SemaphoreType.DMA((3 * n,)), pltpu.SemaphoreType.DMA((3 * n,))],
        name="grad_exchange_chip_shards",
    )(*parts)


def _add_chip_shards(name, p, b, j_idx):
    L, _, rh, X = p.shape
    tr = _tile(rh, 256, SUBLANES)

    def body(j_ref, p_ref, b_ref, o_ref):
        o_ref[...] = ((p_ref[...] + b_ref[0]) + b_ref[1]) + b_ref[2]

    return _pcall(
        body,
        grid_spec=pltpu.PrefetchScalarGridSpec(
            num_scalar_prefetch=1, grid=(L, rh // tr),
            in_specs=[pl.BlockSpec((None, None, tr, X), lambda l, i, j_ref: (l, j_ref[0], i, 0)),
                      pl.BlockSpec((3, None, tr, X), lambda l, i, j_ref: (0, l, i, 0))],
            out_specs=pl.BlockSpec((None, tr, X), lambda l, i, j_ref: (l, i, 0))),
        out_shape=_sds((L, rh, X), F32), compiler_params=_cp("parallel", "parallel"), name=name,
    )(j_idx, p, b)


def _join_core_halves(halves):
    n = len(halves)

    def body(*refs):
        ins, outs = refs[:n], refs[n:2 * n]
        send_sems, recv_sems, local_sems = refs[2 * n:]
        x, y, c, _ = _place()
        cps, locs = [], []
        for t in range(n):
            rh = ins[t].shape[1]
            mine = outs[t].at[:, pl.ds(pl.multiple_of(c * rh, SUBLANES), rh), :]
            locs.append(pltpu.make_async_copy(ins[t], mine, local_sems.at[t]))
            cps.append(pltpu.make_async_remote_copy(src_ref=ins[t], dst_ref=mine, send_sem=send_sems.at[t],
                                                    recv_sem=recv_sems.at[t], device_id=(x, y, 1 - c), device_id_type=MESH))
        for cp in locs + cps:
            cp.start()
        for t in range(n):
            rh = ins[t].shape[1]
            theirs = outs[t].at[:, pl.ds(pl.multiple_of((1 - c) * rh, SUBLANES), rh), :]
            pltpu.make_async_remote_copy(src_ref=ins[t], dst_ref=theirs, send_sem=send_sems.at[t], recv_sem=recv_sems.at[t],
                                         device_id=(x, y, 1 - c), device_id_type=MESH).wait()
        for cp in locs:
            cp.wait()

    return _pcall(
        body, in_specs=[ANY] * n, out_specs=[ANY] * n,
        out_shape=[_sds((h.shape[0], 2 * h.shape[1], h.shape[2]), F32) for h in halves],
        scratch_shapes=[pltpu.SemaphoreType.DMA((n,)), pltpu.SemaphoreType.DMA((n,)), pltpu.SemaphoreType.DMA((n,))],
        name="grad_join_core_halves",
    )(*halves)


def _all_reduce_small(packed):
    R, C = packed.shape

    def body(x_ref, o_ref, slots, send_sems, recv_sems):
        x, y, c, _ = _place()
        me = 4 * x + 2 * y + c
        slots[me] = x_ref[...]
        cps = []
        for d in range(N_DEV):
            to = (d // 4, (d // 2) % 2, d % 2)
            cp = pltpu.make_async_remote_copy(src_ref=x_ref, dst_ref=slots.at[me], send_sem=send_sems.at[d],
                                              recv_sem=recv_sems.at[me], device_id=to, device_id_type=MESH)
            cps.append(cp)

            @pl.when(d != me)
            def _():
                cp.start()

        for d in range(N_DEV):
            @pl.when(d != me)
            def _():
                pltpu.make_async_remote_copy(src_ref=x_ref, dst_ref=slots.at[d], send_sem=send_sems.at[d],
                                             recv_sem=recv_sems.at[d], device_id=(x, y, c), device_id_type=MESH).wait_recv()
                cps[d].wait_send()

        acc = slots[0]
        for d in range(1, N_DEV):
            acc = acc + slots[d]
        o_ref[...] = acc

    vm = pl.BlockSpec(memory_space=pltpu.VMEM)
    return _pcall(
        body, in_specs=[vm], out_specs=vm, out_shape=_sds((R, C), F32),
        scratch_shapes=[pltpu.VMEM((N_DEV, R, C), F32), pltpu.SemaphoreType.DMA((N_DEV,)), pltpu.SemaphoreType.DMA((N_DEV,))],
        compiler_params=pltpu.CompilerParams(vmem_limit_bytes=VMEM_LIMIT_BYTES), name="all_reduce_small",
    )(packed)


PACK = SUBLANES * LANES


def _pack(arrays):
    flat = []
    for a in arrays:
        v = a.reshape(-1)
        flat.append(jnp.pad(v, (0, (-v.shape[0]) % PACK)))
    return jnp.concatenate(flat).reshape(-1, LANES)


def _unpack(packed, shapes):
    flat = packed.reshape(-1)
    out, pos = [], 0
    for s in shapes:
        n = 1
        for d in s:
            n *= d
        out.append(flat[pos:pos + n].reshape(s))
        pos += n + (-n) % PACK
    return out


def kernel(x, mix_norm_g, ffn_norm_g, conv_w_in, conv_a_dw_w, conv_a_dw_b, conv_a_ln_g, conv_a_ln_b, conv_b_dw_w, conv_w_out, attn_w_qkv, attn_q_g, attn_k_g, attn_w_o, ffn_w_up, ffn_dw_w, ffn_dw_b, ffn_w_down, loss_target, m_mix_norm_g, m_ffn_norm_g, m_conv_w_in, m_conv_a_dw_w, m_conv_a_dw_b, m_conv_a_ln_g, m_conv_a_ln_b, m_conv_b_dw_w, m_conv_w_out, m_attn_w_qkv, m_attn_q_g, m_attn_k_g, m_attn_w_o, m_ffn_w_up, m_ffn_dw_w, m_ffn_dw_b, m_ffn_w_down, v_mix_norm_g, v_ffn_norm_g, v_conv_w_in, v_conv_a_dw_w, v_conv_a_dw_b, v_conv_a_ln_g, v_conv_a_ln_b, v_conv_b_dw_w, v_conv_w_out, v_attn_w_qkv, v_attn_q_g, v_attn_k_g, v_attn_w_o, v_ffn_w_up, v_ffn_dw_w, v_ffn_dw_b, v_ffn_w_down):
    depth = mix_norm_g.shape[0]
    n_even, n_odd = conv_w_in.shape[0], attn_w_qkv.shape[0]
    S, D = x.shape[1], x.shape[2]
    dg = D // 2
    x0 = x.reshape(S, D)
    target = loss_target.reshape(S, D)
    j_me = 2 * lax.axis_index("x") + lax.axis_index("y")
    c_me = lax.axis_index("c")
    j_idx = j_me.astype(jnp.int32).reshape(1)
    c_idx = c_me.astype(jnp.int32).reshape(1)

    col_names = ["conv_w_in", "attn_w_qkv", "ffn_w_up"]
    row_names = ["conv_w_out", "attn_w_o", "ffn_w_down"]
    (w_in, w_qkv, w_up), (w_out, w_o, w_down), (a_dw, b_dw, f_dw) = _all_gather_weights(
        [conv_w_in.astype(BF16), attn_w_qkv.astype(BF16), ffn_w_up.astype(BF16)],
        [conv_w_out.astype(BF16), attn_w_o.astype(BF16), ffn_w_down.astype(BF16)],
        [conv_a_dw_w, conv_b_dw_w, ffn_dw_w])
    unshard = lambda a: jnp.moveaxis(a, 1, 2).reshape(a.shape[0], a.shape[2], N_CHIPS * a.shape[3])
    a_dw, b_dw, f_dw = unshard(a_dw), unshard(b_dw), unshard(f_dw)
    qk_gain = [jnp.stack([jnp.tile(attn_q_g[i], LANES // HEAD_DIM), jnp.tile(attn_k_g[i], LANES // HEAD_DIM)])
               for i in range(n_odd)]

    saved = []
    xc = x0
    for layer in range(depth):
        i = layer // 2
        tag = f"l{layer}"
        s = {"x_in": xc}
        h = _rms_fwd(f"rms_mix_fwd_{tag}", xc, mix_norm_g, layer)
        s["h"] = h
        if layer % 2 == 0:
            p = _mm_fwd(f"conv_in_fwd_{tag}", h, w_in, i, colshard=True)
            ab = _convmix_fwd(f"convmix_fwd_{tag}", p, a_dw, conv_a_dw_b, conv_a_ln_g, conv_a_ln_b, b_dw, i)
            xm = _mm_fwd(f"conv_out_fwd_{tag}", ab, w_out, i, colshard=False, res=xc)
            s.update(p=p, ab=ab)
        else:
            qkv = _mm_fwd(f"attn_qkv_fwd_{tag}", h, w_qkv, i, colshard=True)
            qs, kn, vb = _qknorm_fwd(f"qknorm_fwd_{tag}", qkv, qk_gain[i])
            o = _attn_fwd(f"attn_fwd_{tag}", qs, kn, vb)
            xm = _mm_fwd(f"attn_out_fwd_{tag}", o, w_o, i, colshard=False, res=xc)
            s.update(qkv=qkv, qs=qs, kn=kn, vb=vb, o=o)
        s["x_mid"] = xm
        h2 = _rms_fwd(f"rms_ffn_fwd_{tag}", xm, ffn_norm_g, layer)
        u2 = _mm_fwd(f"ffn_up_fwd_{tag}", h2, w_up, layer, colshard=True, out_split=2)
        f = _ffn_mid_fwd(f"ffn_mid_fwd_{tag}", u2, f_dw, ffn_dw_b, layer)
        xc = _mm_fwd(f"ffn_down_fwd_{tag}", f, w_down, layer, colshard=False, res=xm)
        s.update(h2=h2, u2=u2, f=f)
        saved.append(s)

    dx, loss_tile = _loss_fwd_bwd("loss", xc, target)

    g_up = g_down = g_in = g_out = g_qkv = g_o = None
    d_mix_g, d_ffn_g = [None] * depth, [None] * depth
    d_ffn_dw_w, d_ffn_dw_b = [None] * depth, [None] * depth
    d_a_dw_w, d_a_dw_b, d_a_ln_g, d_a_ln_b, d_b_dw_w = ([None] * n_even for _ in range(5))
    d_q_g, d_k_g = [None] * n_odd, [None] * n_odd
    for layer in reversed(range(depth)):
        i = layer // 2
        tag = f"l{layer}"
        s = saved[layer]
        df = _mm_dgrad(f"ffn_down_dgrad_{tag}", dx, w_down, layer, colshard=False)
        g_down = _mm_wgrad(f"ffn_down_wgrad_{tag}", s["f"], dx, layer, depth, g_down, colshard=False)
        du2, dww, dwb = _ffn_mid_bwd(f"ffn_mid_bwd_{tag}", s["u2"], df, f_dw, ffn_dw_b, layer)
        d_ffn_dw_w[layer] = jnp.moveaxis(dww, 0, 1).reshape(FFN_CONV_WIDTH, -1)
        d_ffn_dw_b[layer] = dwb.reshape(-1)
        dh2 = _mm_dgrad(f"ffn_up_dgrad_{tag}", du2, w_up, layer, colshard=True)
        g_up = _mm_wgrad(f"ffn_up_wgrad_{tag}", s["h2"], du2, layer, depth, g_up, colshard=True)
        dx, dg_ = _rms_bwd(f"rms_ffn_bwd_{tag}", s["x_mid"], ffn_norm_g, layer, dh2, dx)
        d_ffn_g[layer] = dg_.reshape(-1)
        if layer % 2 == 0:
            dab = _mm_dgrad(f"conv_out_dgrad_{tag}", dx, w_out, i, colshard=False)
            g_out = _mm_wgrad(f"conv_out_wgrad_{tag}", s["ab"], dx, i, n_even, g_out, colshard=False)
            dp, daw, dab_b, dlg, dlb, dbw = _convmix_bwd(f"convmix_bwd_{tag}", s["p"], dab, a_dw, conv_a_dw_b, conv_a_ln_g,
                                                         conv_a_ln_b, b_dw, i)
            d_a_dw_w[i], d_a_dw_b[i], d_a_ln_g[i], d_a_ln_b[i], d_b_dw_w[i] = (
                daw, dab_b.reshape(-1), dlg.reshape(-1), dlb.reshape(-1), dbw)
            dh = _mm_dgrad(f"conv_in_dgrad_{tag}", dp, w_in, i, colshard=True)
            g_in = _mm_wgrad(f"conv_in_wgrad_{tag}", s["h"], dp, i, n_even, g_in, colshard=True)
        else:
            do = _mm_dgrad(f"attn_out_dgrad_{tag}", dx, w_o, i, colshard=False)
            g_o = _mm_wgrad(f"attn_out_wgrad_{tag}", s["o"], dx, i, n_odd, g_o, colshard=False)
            dq, dk, dv = _attn_bwd(f"attn_bwd_{tag}", s["qs"], s["kn"], s["vb"], s["o"], do)
            dqkv, dgain = _qknorm_bwd(f"qknorm_bwd_{tag}", s["qkv"], dq, dk, dv, qk_gain[i])
            d_q_g[i] = dgain[0, :HEAD_DIM] + dgain[0, HEAD_DIM:]
            d_k_g[i] = dgain[1, :HEAD_DIM] + dgain[1, HEAD_DIM:]
            dh = _mm_dgrad(f"attn_qkv_dgrad_{tag}", dqkv, w_qkv, i, colshard=True)
            g_qkv = _mm_wgrad(f"attn_qkv_wgrad_{tag}", s["h"], dqkv, i, n_odd, g_qkv, colshard=True)
        dx, dg_ = _rms_bwd(f"rms_mix_bwd_{tag}", s["x_in"], mix_norm_g, layer, dh, dx)
        d_mix_g[layer] = dg_.reshape(-1)
    grad_x = dx.reshape(1, S, D)

    small = {
        "mix_norm_g": jnp.stack(d_mix_g), "ffn_norm_g": jnp.stack(d_ffn_g),
        "conv_a_dw_w": jnp.stack(d_a_dw_w), "conv_a_dw_b": jnp.stack(d_a_dw_b),
        "conv_a_ln_g": jnp.stack(d_a_ln_g), "conv_a_ln_b": jnp.stack(d_a_ln_b),
        "conv_b_dw_w": jnp.stack(d_b_dw_w), "attn_q_g": jnp.stack(d_q_g), "attn_k_g": jnp.stack(d_k_g),
        "ffn_dw_w": jnp.stack(d_ffn_dw_w), "ffn_dw_b": jnp.stack(d_ffn_dw_b),
    }
    small_names = list(small)
    summed = _all_reduce_small(_pack([loss_tile] + [small[n] for n in small_names]))
    parts = _unpack(summed, [loss_tile.shape] + [small[n].shape for n in small_names])
    loss = parts[0][0, 0]
    small_g = dict(zip(small_names, parts[1:]))
    for n in ("conv_a_dw_w", "conv_b_dw_w", "ffn_dw_w"):
        cs = small_g[n].shape[2] // N_CHIPS
        small_g[n] = lax.dynamic_slice_in_dim(small_g[n], j_me * cs, cs, axis=2)

    big = {"conv_w_in": g_in, "attn_w_qkv": g_qkv, "ffn_w_up": g_up, "conv_w_out": g_out, "attn_w_o": g_o, "ffn_w_down": g_down}
    big_names = col_names + row_names
    five = []
    for n in big_names:
        g = big[n]
        if n in col_names:
            five.append(g.reshape(g.shape[0], N_CHIPS, 2, g.shape[2] // 2, g.shape[3]))
        else:
            five.append(g.reshape(g.shape[0], N_CHIPS, 2, g.shape[1] // (2 * N_CHIPS), g.shape[2]))
    from_sibling = _exchange_core_halves(five)
    chip_sums = [_add_core_halves(f"grad_add_core_{n}", g, a, c_idx) for n, g, a in zip(big_names, five, from_sibling)]
    from_chips = _exchange_chip_shards(chip_sums)
    totals = [_add_chip_shards(f"grad_add_chips_{n}", p, b, j_idx) for n, p, b in zip(big_names, chip_sums, from_chips)]
    big_g = dict(zip(big_names, _join_core_halves(totals)))

    weights = dict(mix_norm_g=mix_norm_g, ffn_norm_g=ffn_norm_g, conv_w_in=conv_w_in, conv_a_dw_w=conv_a_dw_w, conv_a_dw_b=conv_a_dw_b, conv_a_ln_g=conv_a_ln_g, conv_a_ln_b=conv_a_ln_b, conv_b_dw_w=conv_b_dw_w, conv_w_out=conv_w_out, attn_w_qkv=attn_w_qkv, attn_q_g=attn_q_g, attn_k_g=attn_k_g, attn_w_o=attn_w_o, ffn_w_up=ffn_w_up, ffn_dw_w=ffn_dw_w, ffn_dw_b=ffn_dw_b, ffn_w_down=ffn_w_down)
    m_in = dict(mix_norm_g=m_mix_norm_g, ffn_norm_g=m_ffn_norm_g, conv_w_in=m_conv_w_in, conv_a_dw_w=m_conv_a_dw_w, conv_a_dw_b=m_conv_a_dw_b, conv_a_ln_g=m_conv_a_ln_g, conv_a_ln_b=m_conv_a_ln_b, conv_b_dw_w=m_conv_b_dw_w, conv_w_out=m_conv_w_out, attn_w_qkv=m_attn_w_qkv, attn_q_g=m_attn_q_g, attn_k_g=m_attn_k_g, attn_w_o=m_attn_w_o, ffn_w_up=m_ffn_w_up, ffn_dw_w=m_ffn_dw_w, ffn_dw_b=m_ffn_dw_b, ffn_w_down=m_ffn_w_down)
    v_in = dict(mix_norm_g=v_mix_norm_g, ffn_norm_g=v_ffn_norm_g, conv_w_in=v_conv_w_in, conv_a_dw_w=v_conv_a_dw_w, conv_a_dw_b=v_conv_a_dw_b, conv_a_ln_g=v_conv_a_ln_g, conv_a_ln_b=v_conv_a_ln_b, conv_b_dw_w=v_conv_b_dw_w, conv_w_out=v_conv_w_out, attn_w_qkv=v_attn_w_qkv, attn_q_g=v_attn_q_g, attn_k_g=v_attn_k_g, attn_w_o=v_attn_w_o, ffn_w_up=v_ffn_w_up, ffn_dw_w=v_ffn_dw_w, ffn_dw_b=v_ffn_dw_b, ffn_w_down=v_ffn_w_down)
    order = list(weights)
    grads, delta, new_m, new_v = {}, {}, {}, {}
    for n in big_names:
        grads[n] = big_g[n]
        delta[n], new_m[n], new_v[n] = _adamw(f"adamw_{n}", weights[n], big_g[n], m_in[n], v_in[n])
    shapes = [weights[n].shape for n in small_names]
    packed = [_pack([d[n] for n in small_names]) for d in (weights, small_g, m_in, v_in)]
    upd = _adamw("adamw_small", *[p[None] for p in packed])
    for out, res in zip((delta, new_m, new_v), upd):
        out.update(zip(small_names, _unpack(res[0], shapes)))
    grads.update({n: small_g[n].reshape(weights[n].shape) for n in small_names})
    return (loss, grad_x, *[grads[n] for n in order], *[delta[n] for n in order], *[new_m[n] for n in order],
            *[new_v[n] for n in order])
```

```python
import jax
import jax.numpy as jnp
from jax import lax
from jax.experimental import pallas as pl
from jax.experimental.pallas import tpu as pltpu

F32 = jnp.float32
BF16 = jnp.bfloat16
EPS = 1e-6
CONV_A_WIDTH = 31
CONV_B_WIDTH = 3
FFN_CONV_WIDTH = 3
HEAD_DIM = 64
ADAM_LR = 0.001
ADAM_B1 = 0.9
ADAM_B2 = 0.999
ADAM_EPS = 1e-08
ADAM_WD = 0.01
ADAM_STEP = 10

LANES = 128
SUBLANES = 8
BF16_ROWS = 16
V7X_VMEM_BYTES = 64 * 1024 * 1024
VMEM_LIMIT_BYTES = V7X_VMEM_BYTES * 3 // 4
MM_VMEM_BUDGET = VMEM_LIMIT_BYTES * 4 // 5
MM_ROWS = 1024
N_CHIPS = 4
N_DEV = 8
HALO_A = 32
HALO_S = 8
ATTN_BLOCK = 128
ATTN_SUB = 2
EXP_UNDERFLOW = -104.0
MESH = pl.DeviceIdType.MESH
ANY = pl.BlockSpec(memory_space=pl.ANY)
NT = (((1,), (1,)), ((), ()))
NN = (((1,), (0,)), ((), ()))
TN = (((0,), (0,)), ((), ()))


def _pcall(body, **kw):
    return pl.pallas_call(body, **kw)


def _cp(*sem):
    return pltpu.CompilerParams(dimension_semantics=sem, vmem_limit_bytes=VMEM_LIMIT_BYTES)


def _sds(shape, dtype):
    return jax.ShapeDtypeStruct(tuple(shape), dtype)


def _tile(n, cap, align=LANES):
    if n <= cap:
        return n
    for t in range(cap - cap % align, 0, -align):
        if n % t == 0:
            return t
    return n


def _sig(x):
    return 1.0 / (1.0 + jnp.exp(-x))


def _rowsum(x):
    return jnp.sum(x, axis=0, keepdims=True)


def _mm_call(name, dn, operands, in_specs, out_shape, out_spec, grid, nk, acc_shape, has_res, has_alias):
    def body(*refs):
        a_ref, b_ref = refs[0], refs[1]
        pos = 2
        res_ref = refs[pos] if has_res else None
        pos += int(has_res) + int(has_alias)
        o_ref = refs[pos]
        acc_ref = refs[pos + 1] if nk > 1 else None
        p = lax.dot_general(a_ref[...].astype(BF16), b_ref[...].astype(BF16), dn, preferred_element_type=F32)

        def finish(v):
            if has_res:
                v = v + res_ref[...]
            o_ref[...] = v.astype(o_ref.dtype)

        if nk == 1:
            finish(p)
        else:
            k = pl.program_id(2)

            @pl.when(k == 0)
            def _():
                acc_ref[...] = p

            @pl.when(k > 0)
            def _():
                acc_ref[...] += p

            @pl.when(k == nk - 1)
            def _():
                finish(acc_ref[...])

    aliases = {len(operands) - 1: 0} if has_alias else {}
    return _pcall(
        body, grid=grid, in_specs=in_specs, out_specs=out_spec, out_shape=out_shape,
        scratch_shapes=[pltpu.VMEM(acc_shape, F32)] if nk > 1 else [],
        input_output_aliases=aliases, compiler_params=_cp("parallel", "parallel", "arbitrary"), name=name,
    )(*operands)


def _mm_fwd(name, a, w, l, *, colshard, res=None, out_split=1):
    M, K = a.shape
    tm = _tile(M, MM_ROWS, BF16_ROWS)
    if colshard:
        cs = w.shape[3]
        N, tn, tk = N_CHIPS * cs, cs, K
        b_spec = pl.BlockSpec((None, None, tk, tn), lambda j, i, k: (l, j, k, 0))
    else:
        N = w.shape[2]
        tn, tk = _tile(N, 1024), _tile(K, 1536)
        b_spec = pl.BlockSpec((None, tk, tn), lambda j, i, k: (l, k, j))
    nk = K // tk
    in_specs = [pl.BlockSpec((tm, tk), lambda j, i, k: (i, k)), b_spec]
    operands = [a, w]
    if res is not None:
        in_specs.append(pl.BlockSpec((tm, tn), lambda j, i, k: (i, j)))
        operands.append(res)
    if out_split == 1:
        out_shape = _sds((M, N), F32)
        out_spec = pl.BlockSpec((tm, tn), lambda j, i, k: (i, j))
    else:
        per = N // tn // out_split
        out_shape = _sds((out_split, M, N // out_split), F32)
        out_spec = pl.BlockSpec((None, tm, tn), lambda j, i, k: (j // per, i, j % per))
    return _mm_call(name, NN, operands, in_specs, out_shape, out_spec, (N // tn, M // tm, nk), nk, (tm, tn),
                    res is not None, False)


def _mm_dgrad(name, g, w, l, *, colshard):
    split = g.ndim == 3
    M = g.shape[-2]
    tm = _tile(M, MM_ROWS, BF16_ROWS)
    if colshard:
        kw, cs = w.shape[2], w.shape[3]
        tn, tk, nk = _tile(kw, 1408), cs, N_CHIPS
        b_spec = pl.BlockSpec((None, None, tn, tk), lambda j, i, k: (l, k, j, 0))
    else:
        kw, ncon = w.shape[1], w.shape[2]
        tn, tk = _tile(kw, 1408), _tile(ncon, 1536)
        nk = ncon // tk
        b_spec = pl.BlockSpec((None, tn, tk), lambda j, i, k: (l, j, k))
    if split:
        per = nk // g.shape[0]
        a_spec = pl.BlockSpec((None, tm, tk), lambda j, i, k: (k // per, i, k % per))
    else:
        a_spec = pl.BlockSpec((tm, tk), lambda j, i, k: (i, k))
    out_shape = _sds((M, kw), F32)
    out_spec = pl.BlockSpec((tm, tn), lambda j, i, k: (i, j))
    return _mm_call(name, NT, [g, w], [a_spec, b_spec], out_shape, out_spec, (kw // tn, M // tm, nk), nk, (tm, tn),
                    False, False)


def _mm_wgrad(name, a, g, l, n_layers, buf, *, colshard):
    S, M = a.shape
    split = g.ndim == 3
    N = g.shape[-1] * (g.shape[0] if split else 1)
    tm = _tile(M, 1408)
    tn = N // N_CHIPS if colshard else _tile(N, 1024)
    per_row = 2 * (tm * a.dtype.itemsize + tn * g.dtype.itemsize)
    tk = _tile(S, max(BF16_ROWS, min(2048, (MM_VMEM_BUDGET - 3 * tm * tn * 4) // per_row)), BF16_ROWS)
    nk = S // tk
    if colshard:
        out_shape = _sds((n_layers, N_CHIPS, M, tn), F32)
        out_spec = pl.BlockSpec((None, None, tm, tn), lambda j, i, k: (l, j, i, 0))
    else:
        out_shape = _sds((n_layers, M, N), F32)
        out_spec = pl.BlockSpec((None, tm, tn), lambda j, i, k: (l, i, j))
    if split:
        per = N // tn // g.shape[0]
        b_spec = pl.BlockSpec((None, tk, tn), lambda j, i, k: (j // per, k, j % per))
    else:
        b_spec = pl.BlockSpec((tk, tn), lambda j, i, k: (k, j))
    in_specs = [pl.BlockSpec((tk, tm), lambda j, i, k: (k, i)), b_spec]
    operands = [a, g]
    if buf is not None:
        in_specs.append(ANY)
        operands.append(buf)
    return _mm_call(name, TN, operands, in_specs, out_shape, out_spec, (N // tn, M // tm, nk), nk, (tm, tn),
                    False, buf is not None)


def _rms_fwd(name, x, g, l):
    S, D = x.shape
    tm = _tile(S, 512, BF16_ROWS)

    def body(x_ref, g_ref, o_ref):
        xf = x_ref[...]
        r = lax.rsqrt(jnp.mean(xf * xf, axis=-1, keepdims=True) + EPS)
        o_ref[...] = (xf * r * g_ref[l:l + 1, :]).astype(BF16)

    return _pcall(
        body, grid=(S // tm,),
        in_specs=[pl.BlockSpec((tm, D), lambda i: (i, 0)), pl.BlockSpec(g.shape, lambda i: (0, 0))],
        out_specs=pl.BlockSpec((tm, D), lambda i: (i, 0)), out_shape=_sds((S, D), BF16),
        compiler_params=_cp("parallel"), name=name,
    )(x, g)


def _rms_bwd(name, x, g, l, dh, dres):
    S, D = x.shape
    tm = _tile(S, 512, SUBLANES)

    def body(x_ref, g_ref, dh_ref, dr_ref, dx_ref, dg_ref):
        xf = x_ref[...]
        r = lax.rsqrt(jnp.mean(xf * xf, axis=-1, keepdims=True) + EPS)
        xh = xf * r
        d = dh_ref[...]
        dxh = d * g_ref[l:l + 1, :]
        dx_ref[...] = dr_ref[...] + r * (dxh - xh * jnp.mean(dxh * xh, axis=-1, keepdims=True))

        @pl.when(pl.program_id(0) == 0)
        def _():
            dg_ref[...] = jnp.zeros_like(dg_ref)

        dg_ref[...] += _rowsum(d * xh)

    row = pl.BlockSpec((tm, D), lambda i: (i, 0))
    return _pcall(
        body, grid=(S // tm,),
        in_specs=[row, pl.BlockSpec(g.shape, lambda i: (0, 0)), row, row],
        out_specs=[row, pl.BlockSpec((1, D), lambda i: (0, 0))],
        out_shape=[_sds((S, D), F32), _sds((1, D), F32)],
        compiler_params=_cp("arbitrary"), name=name,
    )(x, g, dh, dres)


def _loss_fwd_bwd(name, y, t):
    S, D = y.shape
    tm = _tile(S, 512, SUBLANES)

    def body(y_ref, t_ref, dy_ref, l_ref):
        e = y_ref[...] - t_ref[...]
        dy_ref[...] = e * (1.0 / D)

        @pl.when(pl.program_id(0) == 0)
        def _():
            l_ref[...] = jnp.zeros_like(l_ref)

        l_ref[...] += 0.5 * jnp.sum(jnp.sum(e * e, axis=-1, keepdims=True) * (1.0 / D), axis=0, keepdims=True)

    row = pl.BlockSpec((tm, D), lambda i: (i, 0))
    return _pcall(
        body, grid=(S // tm,), in_specs=[row, row],
        out_specs=[row, pl.BlockSpec((SUBLANES, LANES), lambda i: (0, 0))],
        out_shape=[_sds((S, D), F32), _sds((SUBLANES, LANES), F32)],
        compiler_params=_cp("arbitrary"), name=name,
    )(y, t)


def _convmix_fwd(name, p, aw, ab, lg, lb, bw, l):
    S, W = p.shape
    dg = W // 5
    tm = _tile(S, 256, HALO_A)
    nb = tm // HALO_A
    ka, kb = CONV_A_WIDTH, CONV_B_WIDTH

    def body(p_ref, ph_ref, aw_ref, ab_ref, lg_ref, lb_ref, bw_ref, o_ref, uext, mext):
        first = pl.program_id(0) == 0
        ph = ph_ref[...]
        pc = p_ref[...]
        uext[pl.ds(0, HALO_A), :] = jnp.where(first, 0.0, ph[:, 0:dg] * _sig(ph[:, dg:2 * dg]))
        uext[pl.ds(HALO_A, tm), :] = pc[:, 0:dg] * _sig(pc[:, dg:2 * dg])
        mext[pl.ds(0, HALO_A), :] = jnp.where(first, 0.0, ph[:, 3 * dg:4 * dg] * ph[:, 4 * dg:5 * dg])
        mext[pl.ds(HALO_A, tm), :] = pc[:, 3 * dg:4 * dg] * pc[:, 4 * dg:5 * dg]
        acc = jnp.zeros((tm, dg), F32) + ab_ref[l:l + 1, :]
        for k in range(ka):
            acc = acc + aw_ref[l, k:k + 1, :] * uext[pl.ds(HALO_A - (ka - 1) + k, tm), :]
        mu = jnp.mean(acc, axis=-1, keepdims=True)
        xc = acc - mu
        ln = xc * lax.rsqrt(jnp.mean(xc * xc, axis=-1, keepdims=True) + EPS) * lg_ref[l:l + 1, :] + lb_ref[l:l + 1, :]
        o_ref[:, 0:dg] = (ln * _sig(ln)).astype(BF16)
        cb = jnp.zeros((tm, dg), F32)
        for k in range(kb):
            cb = cb + bw_ref[l, k:k + 1, :] * mext[pl.ds(HALO_A - (kb - 1) + k, tm), :]
        o_ref[:, dg:2 * dg] = (pc[:, 2 * dg:3 * dg] * cb).astype(BF16)

    full = lambda a: pl.BlockSpec(a.shape, lambda i: (0,) * a.ndim)
    return _pcall(
        body, grid=(S // tm,),
        in_specs=[pl.BlockSpec((tm, W), lambda i: (i, 0)),
                  pl.BlockSpec((HALO_A, W), lambda i: (jnp.maximum(i * nb - 1, 0), 0)),
                  full(aw), full(ab), full(lg), full(lb), full(bw)],
        out_specs=pl.BlockSpec((tm, 2 * dg), lambda i: (i, 0)), out_shape=_sds((S, 2 * dg), BF16),
        scratch_shapes=[pltpu.VMEM((HALO_A + tm, dg), F32), pltpu.VMEM((HALO_A + tm, dg), F32)],
        compiler_params=_cp("parallel"), name=name,
    )(p, p, aw, ab, lg, lb, bw)


def _convmix_bwd(name, p, dab, aw, ab, lg, lb, bw, l):
    S, W = p.shape
    dg = W // 5
    tm = _tile(S, 256, HALO_A)
    nb = tm // HALO_A
    n_i = S // tm
    ka, kb = CONV_A_WIDTH, CONV_B_WIDTH
    n = tm + HALO_A
    ext = HALO_A + n

    def body(p_ref, pp_ref, pn_ref, d_ref, dn_ref, aw_ref, ab_ref, lg_ref, lb_ref, bw_ref,
             dp_ref, daw_ref, dab_ref, dlg_ref, dlb_ref, dbw_ref, uext, mext, gext, dcext, dbext):
        i = pl.program_id(0)
        first, last = i == 0, i == n_i - 1

        @pl.when(first)
        def _():
            for r in (daw_ref, dab_ref, dlg_ref, dlb_ref, dbw_ref):
                r[...] = jnp.zeros_like(r)

        pp, pc, pn = pp_ref[...], p_ref[...], pn_ref[...]
        glu = lambda b: b[:, 0:dg] * _sig(b[:, dg:2 * dg])
        gch = lambda b: b[:, 3 * dg:4 * dg] * b[:, 4 * dg:5 * dg]
        uext[pl.ds(0, HALO_A), :] = jnp.where(first, 0.0, glu(pp))
        uext[pl.ds(HALO_A, tm), :] = glu(pc)
        uext[pl.ds(HALO_A + tm, HALO_A), :] = glu(pn)
        mext[pl.ds(0, HALO_A), :] = jnp.where(first, 0.0, gch(pp))
        mext[pl.ds(HALO_A, tm), :] = gch(pc)
        mext[pl.ds(HALO_A + tm, HALO_A), :] = gch(pn)

        c = jnp.zeros((n, dg), F32) + ab_ref[l:l + 1, :]
        for k in range(ka):
            c = c + aw_ref[l, k:k + 1, :] * uext[pl.ds(HALO_A - (ka - 1) + k, n), :]
        xc = c - jnp.mean(c, axis=-1, keepdims=True)
        rstd = lax.rsqrt(jnp.mean(xc * xc, axis=-1, keepdims=True) + EPS)
        chat = xc * rstd
        g_ln = lg_ref[l:l + 1, :]
        ln = chat * g_ln + lb_ref[l:l + 1, :]
        s = _sig(ln)
        gext[pl.ds(0, tm), :] = d_ref[:, 0:dg]
        gext[pl.ds(tm, HALO_A), :] = jnp.where(last, 0.0, dn_ref[:, 0:dg])
        dln = gext[...] * (s * (1.0 + ln * (1.0 - s)))
        dlnh = dln * g_ln
        dc = rstd * (dlnh - jnp.mean(dlnh, axis=-1, keepdims=True) - chat * jnp.mean(dlnh * chat, axis=-1, keepdims=True))
        dcext[...] = dc
        dlg_ref[...] += _rowsum((dln * chat)[0:tm])
        dlb_ref[...] += _rowsum(dln[0:tm])
        dab_ref[...] += _rowsum(dc[0:tm])
        du = jnp.zeros((tm, dg), F32)
        for k in range(ka):
            du = du + aw_ref[l, k:k + 1, :] * dcext[pl.ds(ka - 1 - k, tm), :]
            daw_ref[k:k + 1, :] += _rowsum(dcext[pl.ds(0, tm), :] * uext[pl.ds(HALO_A - (ka - 1) + k, tm), :])
        sg = _sig(pc[:, dg:2 * dg])
        dp_ref[:, 0:dg] = (du * sg).astype(BF16)
        dp_ref[:, dg:2 * dg] = (du * pc[:, 0:dg] * sg * (1.0 - sg)).astype(BF16)

        cb = jnp.zeros((tm, dg), F32)
        for k in range(kb):
            cb = cb + bw_ref[l, k:k + 1, :] * mext[pl.ds(HALO_A - (kb - 1) + k, tm), :]
        db = d_ref[:, dg:2 * dg]
        dp_ref[:, 2 * dg:3 * dg] = (db * cb).astype(BF16)
        dbext[pl.ds(0, tm), :] = db * pc[:, 2 * dg:3 * dg]
        dbext[pl.ds(tm, HALO_A), :] = jnp.where(last, 0.0, dn_ref[:, dg:2 * dg] * pn[:, 2 * dg:3 * dg])
        dm = jnp.zeros((tm, dg), F32)
        for k in range(kb):
            dm = dm + bw_ref[l, k:k + 1, :] * dbext[pl.ds(kb - 1 - k, tm), :]
            dbw_ref[k:k + 1, :] += _rowsum(dbext[pl.ds(0, tm), :] * mext[pl.ds(HALO_A - (kb - 1) + k, tm), :])
        dp_ref[:, 3 * dg:4 * dg] = (dm * pc[:, 4 * dg:5 * dg]).astype(BF16)
        dp_ref[:, 4 * dg:5 * dg] = (dm * pc[:, 3 * dg:4 * dg]).astype(BF16)

    full = lambda a: pl.BlockSpec(a.shape, lambda i: (0,) * a.ndim)
    prev = lambda i: (jnp.maximum(i * nb - 1, 0), 0)
    nxt = lambda i: (jnp.minimum((i + 1) * nb, S // HALO_A - 1), 0)
    acc = lambda r: pl.BlockSpec((r, dg), lambda i: (0, 0))
    return _pcall(
        body, grid=(n_i,),
        in_specs=[pl.BlockSpec((tm, W), lambda i: (i, 0)), pl.BlockSpec((HALO_A, W), prev), pl.BlockSpec((HALO_A, W), nxt),
                  pl.BlockSpec((tm, 2 * dg), lambda i: (i, 0)), pl.BlockSpec((HALO_A, 2 * dg), nxt),
                  full(aw), full(ab), full(lg), full(lb), full(bw)],
        out_specs=[pl.BlockSpec((tm, W), lambda i: (i, 0)), acc(ka), acc(1), acc(1), acc(1), acc(kb)],
        out_shape=[_sds((S, W), BF16), _sds((ka, dg), F32), _sds((1, dg), F32), _sds((1, dg), F32), _sds((1, dg), F32),
                   _sds((kb, dg), F32)],
        scratch_shapes=[pltpu.VMEM((ext, dg), F32), pltpu.VMEM((ext, dg), F32), pltpu.VMEM((n, dg), F32),
                        pltpu.VMEM((n, dg), F32), pltpu.VMEM((n, dg), F32)],
        compiler_params=_cp("arbitrary"), name=name,
    )(p, p, p, dab, dab, aw, ab, lg, lb, bw)


def _ffn_mid_fwd(name, u2, dww, dwb, l):
    _, S, F = u2.shape
    tm = _tile(S, 256, BF16_ROWS)
    tc = _tile(F, 1408)
    n_f = F // tc
    nb = tm // HALO_S
    kf = FFN_CONV_WIDTH

    def body(u_ref, uh_ref, wg_ref, wv_ref, bg_ref, bv_ref, o_ref, ext):
        first = pl.program_id(1) == 0
        ext[:, pl.ds(0, HALO_S), :] = jnp.where(first, 0.0, uh_ref[...])
        ext[:, pl.ds(HALO_S, tm), :] = u_ref[...]

        def conv(g, w_ref, b_ref):
            acc = jnp.zeros((tm, tc), F32) + b_ref[l:l + 1, :]
            for k in range(kf):
                acc = acc + w_ref[k:k + 1, :] * ext[g, pl.ds(HALO_S - (kf - 1) + k, tm), :]
            return acc

        cg = conv(0, wg_ref, bg_ref)
        cv = conv(1, wv_ref, bv_ref)
        o_ref[...] = (cg * _sig(cg) * cv).astype(BF16)

    n_l = dwb.shape[0]
    return _pcall(
        body, grid=(n_f, S // tm),
        in_specs=[pl.BlockSpec((2, tm, tc), lambda j, i: (0, i, j)),
                  pl.BlockSpec((2, HALO_S, tc), lambda j, i: (0, jnp.maximum(i * nb - 1, 0), j)),
                  pl.BlockSpec((None, kf, tc), lambda j, i: (l, 0, j)),
                  pl.BlockSpec((None, kf, tc), lambda j, i: (l, 0, j + n_f)),
                  pl.BlockSpec((n_l, tc), lambda j, i: (0, j)),
                  pl.BlockSpec((n_l, tc), lambda j, i: (0, j + n_f))],
        out_specs=pl.BlockSpec((tm, tc), lambda j, i: (i, j)), out_shape=_sds((S, F), BF16),
        scratch_shapes=[pltpu.VMEM((2, HALO_S + tm, tc), F32)],
        compiler_params=_cp("parallel", "parallel"), name=name,
    )(u2, u2, dww, dww, dwb, dwb)


def _ffn_mid_bwd(name, u2, df, dww, dwb, l):
    _, S, F = u2.shape
    tm = _tile(S, 256, BF16_ROWS)
    tc = _tile(F, 1408)
    n_f = F // tc
    nb = tm // HALO_S
    n_i = S // tm
    kf = FFN_CONV_WIDTH
    n = tm + HALO_S

    def body(u_ref, up_ref, un_ref, df_ref, dfn_ref, wg_ref, wv_ref, bg_ref, bv_ref,
             du_ref, dw_ref, db_ref, uext, dfext, dcext):
        i = pl.program_id(1)
        first, last = i == 0, i == n_i - 1

        @pl.when(first)
        def _():
            dw_ref[...] = jnp.zeros_like(dw_ref)
            db_ref[...] = jnp.zeros_like(db_ref)

        uext[:, pl.ds(0, HALO_S), :] = jnp.where(first, 0.0, up_ref[...])
        uext[:, pl.ds(HALO_S, tm), :] = u_ref[...]
        uext[:, pl.ds(HALO_S + tm, HALO_S), :] = un_ref[...]
        dfext[pl.ds(0, tm), :] = df_ref[...]
        dfext[pl.ds(tm, HALO_S), :] = jnp.where(last, 0.0, dfn_ref[...])

        def conv(g, w_ref, b_ref):
            acc = jnp.zeros((n, tc), F32) + b_ref[l:l + 1, :]
            for k in range(kf):
                acc = acc + w_ref[k:k + 1, :] * uext[g, pl.ds(HALO_S - (kf - 1) + k, n), :]
            return acc

        cg = conv(0, wg_ref, bg_ref)
        cv = conv(1, wv_ref, bv_ref)
        s = _sig(cg)
        dfe = dfext[...]
        dcext[0] = dfe * cv * (s * (1.0 + cg * (1.0 - s)))
        dcext[1] = dfe * (cg * s)
        for g, w_ref in ((0, wg_ref), (1, wv_ref)):
            du = jnp.zeros((tm, tc), F32)
            for k in range(kf):
                du = du + w_ref[k:k + 1, :] * dcext[g, pl.ds(kf - 1 - k, tm), :]
                dw_ref[g, k:k + 1, :] += _rowsum(dcext[g, pl.ds(0, tm), :] * uext[g, pl.ds(HALO_S - (kf - 1) + k, tm), :])
            du_ref[g] = du.astype(BF16)
            db_ref[g] += _rowsum(dcext[g, pl.ds(0, tm), :])

    n_l = dwb.shape[0]
    prev = lambda j, i: (0, jnp.maximum(i * nb - 1, 0), j)
    nxt = lambda j, i: (0, jnp.minimum((i + 1) * nb, S // HALO_S - 1), j)
    return _pcall(
        body, grid=(n_f, n_i),
        in_specs=[pl.BlockSpec((2, tm, tc), lambda j, i: (0, i, j)),
                  pl.BlockSpec((2, HALO_S, tc), prev), pl.BlockSpec((2, HALO_S, tc), nxt),
                  pl.BlockSpec((tm, tc), lambda j, i: (i, j)),
                  pl.BlockSpec((HALO_S, tc), lambda j, i: nxt(j, i)[1:]),
                  pl.BlockSpec((None, kf, tc), lambda j, i: (l, 0, j)),
                  pl.BlockSpec((None, kf, tc), lambda j, i: (l, 0, j + n_f)),
                  pl.BlockSpec((n_l, tc), lambda j, i: (0, j)),
                  pl.BlockSpec((n_l, tc), lambda j, i: (0, j + n_f))],
        out_specs=[pl.BlockSpec((2, tm, tc), lambda j, i: (0, i, j)),
                   pl.BlockSpec((2, kf, tc), lambda j, i: (0, 0, j)),
                   pl.BlockSpec((2, 1, tc), lambda j, i: (0, 0, j))],
        out_shape=[_sds((2, S, F), BF16), _sds((2, kf, F), F32), _sds((2, 1, F), F32)],
        scratch_shapes=[pltpu.VMEM((2, HALO_S + n, tc), F32), pltpu.VMEM((n, tc), F32), pltpu.VMEM((2, n, tc), F32)],
        compiler_params=_cp("parallel", "arbitrary"), name=name,
    )(u2, u2, u2, df, df, dww, dww, dwb, dwb)


def _head_sum_matrix():
    r = lax.broadcasted_iota(jnp.int32, (LANES, LANES), 0) // HEAD_DIM
    c = lax.broadcasted_iota(jnp.int32, (LANES, LANES), 1) // HEAD_DIM
    return (r == c).astype(F32)


def _head_mean(x, ones):
    return jnp.dot(x, ones, preferred_element_type=F32, precision=lax.Precision.HIGHEST) * (1.0 / HEAD_DIM)


def _qknorm_fwd(name, qkv, g2):
    S, D3 = qkv.shape
    D = D3 // 3
    tm = _tile(S, 256, BF16_ROWS)
    scale = HEAD_DIM ** -0.5

    def body(q_ref, k_ref, v_ref, g_ref, qo_ref, ko_ref, vo_ref):
        ones = _head_sum_matrix()
        for cc in range(D // LANES):
            sl = slice(cc * LANES, (cc + 1) * LANES)
            for x_ref, o_ref, row, mult in ((q_ref, qo_ref, 0, scale), (k_ref, ko_ref, 1, 1.0)):
                x = x_ref[:, sl]
                r = lax.rsqrt(_head_mean(x * x, ones) + EPS)
                o_ref[:, sl] = ((x * r * g_ref[row:row + 1, :]).astype(BF16) * mult).astype(BF16)
        vo_ref[...] = v_ref[...].astype(BF16)

    col = lambda c: pl.BlockSpec((tm, D), lambda i: (i, c))
    out = pl.BlockSpec((tm, D), lambda i: (i, 0))
    return _pcall(
        body, grid=(S // tm,),
        in_specs=[col(0), col(1), col(2), pl.BlockSpec(g2.shape, lambda i: (0, 0))],
        out_specs=[out, out, out], out_shape=[_sds((S, D), BF16)] * 3,
        compiler_params=_cp("parallel"), name=name,
    )(qkv, qkv, qkv, g2)


def _qknorm_bwd(name, qkv, dq, dk, dv, g2):
    S, D3 = qkv.shape
    D = D3 // 3
    tm = _tile(S, 256, BF16_ROWS)
    scale = HEAD_DIM ** -0.5

    def body(q_ref, k_ref, dq_ref, dk_ref, dv_ref, g_ref, o_ref, dg_ref):
        @pl.when(pl.program_id(0) == 0)
        def _():
            dg_ref[...] = jnp.zeros_like(dg_ref)

        ones = _head_sum_matrix()
        for cc in range(D // LANES):
            sl = slice(cc * LANES, (cc + 1) * LANES)
            for x_ref, d_ref, row, mult, base in ((q_ref, dq_ref, 0, scale, 0), (k_ref, dk_ref, 1, 1.0, D)):
                x = x_ref[:, sl]
                r = lax.rsqrt(_head_mean(x * x, ones) + EPS)
                xh = x * r
                dn = d_ref[:, sl] * mult
                dxh = dn * g_ref[row:row + 1, :]
                dx = r * (dxh - xh * _head_mean(dxh * xh, ones))
                o_ref[:, base + cc * LANES:base + (cc + 1) * LANES] = dx.astype(BF16)
                dg_ref[row:row + 1, :] += _rowsum(dn * xh)
        o_ref[:, 2 * D:3 * D] = dv_ref[...].astype(BF16)

    col = lambda c: pl.BlockSpec((tm, D), lambda i: (i, c))
    row = pl.BlockSpec((tm, D), lambda i: (i, 0))
    return _pcall(
        body, grid=(S // tm,),
        in_specs=[col(0), col(1), row, row, row, pl.BlockSpec(g2.shape, lambda i: (0, 0))],
        out_specs=[pl.BlockSpec((tm, D3), lambda i: (i, 0)), pl.BlockSpec((2, LANES), lambda i: (0, 0))],
        out_shape=[_sds((S, D3), BF16), _sds((2, LANES), F32)],
        compiler_params=_cp("arbitrary"), name=name,
    )(qkv, qkv, dq, dk, dv, g2)


def _attn_consts():
    t = ATTN_BLOCK
    row = lax.broadcasted_iota(jnp.int32, (t, t), 0)
    col = lax.broadcasted_iota(jnp.int32, (t, t), 1)
    lane = lax.broadcasted_iota(jnp.int32, (1, LANES), 1)
    heads = (lane < HEAD_DIM, lane >= HEAD_DIM)
    return row, col, heads


def _split_dot(x, m):
    n = x.shape[0]
    hi = x.astype(BF16)
    lo = (x - hi.astype(F32)).astype(BF16)
    both = jnp.dot(jnp.concatenate([hi, lo], axis=0), m, preferred_element_type=F32)
    return both[:n] + both[n:]


def _log_keep(z):
    return -(jnp.maximum(z, 0.0) + jnp.log(1.0 + jnp.exp(-jnp.abs(z))))


def _stack_heads(a, heads):
    t = ATTN_BLOCK
    zero = jnp.zeros((t, LANES), a.dtype)
    return jnp.concatenate([jnp.where(h, a[s * t:(s + 1) * t], zero) for s in range(a.shape[0] // t) for h in heads], axis=0)


def _side_by_side(a):
    t = ATTN_BLOCK
    return jnp.concatenate([jnp.concatenate([a[2 * s * t:(2 * s + 1) * t], a[(2 * s + 1) * t:(2 * s + 2) * t]], axis=1)
                            for s in range(a.shape[0] // (2 * t))], axis=0)


def _grow(a, rows, cols):
    z = jnp.zeros((rows, cols), F32)
    return z if a is None else jnp.concatenate([z, a], axis=0)


def _attn_fwd(name, qs, kn, vb):
    S, D = qs.shape
    t = ATTN_BLOCK
    tq = ATTN_SUB * t

    def body(q_ref, k_ref, v_ref, o_ref):
        i = pl.program_id(1)
        row, col, heads = _attn_consts()
        after_m = (row > col).astype(BF16)
        causal = col < row
        q_all = _stack_heads(q_ref[...], heads)

        def block(j, q, r, acc, mask):
            off = pl.multiple_of(j * t, t)
            kb = k_ref[pl.ds(off, t), :]
            v2 = _stack_heads(v_ref[pl.ds(off, t), :], heads)
            z = lax.dot_general(q, kb, NT, preferred_element_type=F32)
            lk = _log_keep(z)
            if mask is not None:
                lk = jnp.where(mask, lk, 0.0)
            w = jnp.exp(z + lk + _split_dot(lk, after_m) + r)
            if mask is not None:
                w = jnp.where(mask, w, 0.0)
            acc = acc + jnp.dot(_side_by_side(w.astype(BF16)), v2, preferred_element_type=F32)
            return r + jnp.sum(lk, axis=1, keepdims=True), acc

        r = acc = None
        for s in reversed(range(ATTN_SUB)):
            mask = jnp.concatenate([causal, causal] + [jnp.ones_like(causal)] * (2 * (ATTN_SUB - 1 - s)), axis=0)
            r, acc = block(ATTN_SUB * i + s, q_all[2 * s * t:], _grow(r, 2 * t, 1), _grow(acc, t, LANES), mask)

        def cond(c):
            return jnp.logical_and(c[0] >= 0, jnp.max(c[1]) > EXP_UNDERFLOW)

        def step(c):
            r, a = block(c[0], q_all, c[1], c[2], None)
            return c[0] - 1, r, a

        o_ref[...] = lax.while_loop(cond, step, (ATTN_SUB * i - 1, r, acc))[2]

    n_hp = D // LANES
    blk = pl.BlockSpec((tq, LANES), lambda hp, i: (i, hp))
    seq = pl.BlockSpec((S, LANES), lambda hp, i: (0, hp))
    return _pcall(
        body, grid=(n_hp, S // tq), in_specs=[blk, seq, seq], out_specs=blk, out_shape=_sds((S, D), F32),
        compiler_params=_cp("parallel", "arbitrary"), name=name,
    )(qs, kn, vb)


def _attn_bwd(name, qs, kn, vb, o, do):
    S, D = qs.shape
    t = ATTN_BLOCK
    tq = ATTN_SUB * t

    def body(q_ref, k_ref, v_ref, o_ref, do_ref, dq_ref, dk_ref, dv_ref):
        i = pl.program_id(1)

        @pl.when(i == 0)
        def _():
            dk_ref[...] = jnp.zeros_like(dk_ref)
            dv_ref[...] = jnp.zeros_like(dv_ref)

        row, col, heads = _attn_consts()
        after_m = (row > col).astype(BF16)
        from_m = (row >= col).astype(BF16)
        causal = col < row
        q_all = _stack_heads(q_ref[...], heads)
        dob = do_ref[...].astype(BF16)
        do_all = _stack_heads(dob, heads)
        dsum_all = jnp.sum(_stack_heads(dob.astype(F32) * o_ref[...], heads), axis=1, keepdims=True)

        def block(j, q, dor, dsum, r, es, dq, mask):
            off = pl.multiple_of(j * t, t)
            kb = k_ref[pl.ds(off, t), :]
            vblk = v_ref[pl.ds(off, t), :]
            z = lax.dot_general(q, kb, NT, preferred_element_type=F32)
            lk = _log_keep(z)
            if mask is not None:
                lk = jnp.where(mask, lk, 0.0)
            ls = z + lk
            w = jnp.exp(ls + _split_dot(lk, after_m) + r)
            if mask is not None:
                w = jnp.where(mask, w, 0.0)
            e = w * lax.dot_general(dor, vblk, NT, preferred_element_type=F32)
            before = dsum - (es + _split_dot(e, from_m))
            dz = e - (e + before) * jnp.exp(ls)
            if mask is not None:
                dz = jnp.where(mask, dz, 0.0)
            dzb = dz.astype(BF16)
            dq = dq + jnp.dot(_side_by_side(dzb), _stack_heads(kb, heads), preferred_element_type=F32)
            dk_ref[pl.ds(off, t), :] += lax.dot_general(dzb, q, TN, preferred_element_type=F32)
            dv_ref[pl.ds(off, t), :] += lax.dot_general(w.astype(BF16), dor, TN, preferred_element_type=F32)
            return r + jnp.sum(lk, axis=1, keepdims=True), es + jnp.sum(e, axis=1, keepdims=True), dq

        r = es = dq = None
        for s in reversed(range(ATTN_SUB)):
            mask = jnp.concatenate([causal, causal] + [jnp.ones_like(causal)] * (2 * (ATTN_SUB - 1 - s)), axis=0)
            lo = 2 * s * t
            r, es, dq = block(ATTN_SUB * i + s, q_all[lo:], do_all[lo:], dsum_all[lo:], _grow(r, 2 * t, 1),
                              _grow(es, 2 * t, 1), _grow(dq, t, LANES), mask)

        def cond(c):
            return jnp.logical_and(c[0] >= 0, jnp.max(c[1]) > EXP_UNDERFLOW)

        def step(c):
            r, es, a = block(c[0], q_all, do_all, dsum_all, c[1], c[2], c[3], None)
            return c[0] - 1, r, es, a

        dq_ref[...] = lax.while_loop(cond, step, (ATTN_SUB * i - 1, r, es, dq))[3]

    n_hp = D // LANES
    blk = pl.BlockSpec((tq, LANES), lambda hp, i: (i, hp))
    seq = pl.BlockSpec((S, LANES), lambda hp, i: (0, hp))
    return _pcall(
        body, grid=(n_hp, S // tq), in_specs=[blk, seq, seq, blk, blk], out_specs=[blk, seq, seq],
        out_shape=[_sds((S, D), F32)] * 3, compiler_params=_cp("parallel", "arbitrary"), name=name,
    )(qs, kn, vb, o, do)


def _adamw(name, w, g, m, v):
    L, R, C = w.shape
    tr = _tile(R, 256, SUBLANES)
    c1 = 1.0 - ADAM_B1 ** ADAM_STEP
    c2 = 1.0 - ADAM_B2 ** ADAM_STEP

    def body(w_ref, g_ref, m_ref, v_ref, d_ref, mo_ref, vo_ref):
        gg = g_ref[...]
        mn = ADAM_B1 * m_ref[...] + (1.0 - ADAM_B1) * gg
        vn = ADAM_B2 * v_ref[...] + (1.0 - ADAM_B2) * (gg * gg)
        d_ref[...] = -ADAM_LR * ((mn / c1) / (jnp.sqrt(vn / c2) + ADAM_EPS) + ADAM_WD * w_ref[...])
        mo_ref[...] = mn
        vo_ref[...] = vn

    blk = pl.BlockSpec((None, tr, C), lambda l, i: (l, i, 0))
    return _pcall(
        body, grid=(L, R // tr), in_specs=[blk] * 4, out_specs=[blk] * 3, out_shape=[_sds(w.shape, F32)] * 3,
        compiler_params=_cp("parallel", "parallel"), name=name,
    )(w, g, m, v)


def _place():
    x, y, c = lax.axis_index("x"), lax.axis_index("y"), lax.axis_index("c")
    chips = [(1 - x, y), (x, 1 - y), (1 - x, 1 - y)]
    return x, y, c, chips


def _all_gather_weights(col_ws, row_ws, small_ws):
    n_col, n_row, n_small = len(col_ws), len(row_ws), len(small_ws)
    n_big = n_col + n_row
    n_in = n_big + n_small

    def body(*refs):
        ins, outs = refs[:n_in], refs[n_in:2 * n_in]
        send_sems, recv_sems, local_sems = refs[2 * n_in:]
        x, y, c, chips = _place()
        j_me = 2 * x + y
        j_of = [2 * cx + cy for cx, cy in chips]
        sibling = (x, y, 1 - c)

        def piece(t, j, cc):
            return outs[t].at[:, j, cc]

        def own_slot(t):
            return outs[t].at[:, j_me]

        def remote(src, dst, s, to):
            return pltpu.make_async_remote_copy(src_ref=src, dst_ref=dst, send_sem=send_sems.at[s], recv_sem=recv_sems.at[s],
                                                device_id=to, device_id_type=MESH)

        started = []
        for t in range(n_in):
            loc = pltpu.make_async_copy(ins[t], own_slot(t), local_sems.at[t])
            loc.start()
            started.append(loc)
        first = []
        for t in range(n_big):
            for k in range(3):
                first.append(remote(ins[t].at[:, c], piece(t, j_me, c), 6 * t + k, (*chips[k], c)))
        for t in range(n_big, n_in):
            for k in range(3):
                first.append(remote(ins[t], outs[t].at[:, j_me], 6 * n_big + 3 * (t - n_big) + k, (*chips[k], c)))
        for cp in first:
            cp.start()
        passed = []
        for t in range(n_big):
            for k in range(3):
                landed = piece(t, j_of[k], c)
                remote(landed, landed, 6 * t + k, (*chips[k], c)).wait_recv()
                fwd = remote(landed, landed, 6 * t + 3 + k, sibling)
                fwd.start()
                passed.append(fwd)
        for t in range(n_big):
            for k in range(3):
                other = piece(t, j_of[k], 1 - c)
                remote(other, other, 6 * t + 3 + k, sibling).wait_recv()
        for t in range(n_big, n_in):
            for k in range(3):
                dst = outs[t].at[:, j_of[k]]
                remote(dst, dst, 6 * n_big + 3 * (t - n_big) + k, (*chips[k], c)).wait_recv()
        for cp in first + passed:
            cp.wait_send()
        for loc in started:
            loc.wait()

    halves = lambda w: w.reshape(w.shape[0], 2, w.shape[1] // 2, w.shape[2])
    operands = [halves(w) for w in col_ws + row_ws] + list(small_ws)
    out_shape = [_sds((w.shape[0], N_CHIPS) + w.shape[1:], w.dtype) for w in operands]
    n_sem = 6 * n_big + 3 * n_small
    outs = _pcall(
        body, in_specs=[ANY] * n_in, out_specs=[ANY] * n_in, out_shape=out_shape,
        scratch_shapes=[pltpu.SemaphoreType.DMA((n_sem,)), pltpu.SemaphoreType.DMA((n_sem,)), pltpu.SemaphoreType.DMA((n_in,))],
        name="all_gather_weights",
    )(*operands)
    cols = [o.reshape(o.shape[0], N_CHIPS, -1, o.shape[4]) for o in outs[:n_col]]
    rows = [o.reshape(o.shape[0], -1, o.shape[4]) for o in outs[n_col:n_big]]
    return cols, rows, outs[n_big:]


def _exchange_core_halves(grads):
    n = len(grads)

    def body(*refs):
        ins, outs = refs[:n], refs[n:2 * n]
        send_sems, recv_sems = refs[2 * n:]
        x, y, c, _ = _place()
        cps = [pltpu.make_async_remote_copy(src_ref=ins[t].at[:, :, 1 - c], dst_ref=outs[t], send_sem=send_sems.at[t],
                                            recv_sem=recv_sems.at[t], device_id=(x, y, 1 - c), device_id_type=MESH)
               for t in range(n)]
        for cp in cps:
            cp.start()
        for cp in cps:
            cp.wait()

    return _pcall(
        body, in_specs=[ANY] * n, out_specs=[ANY] * n,
        out_shape=[_sds((g.shape[0], g.shape[1], g.shape[3], g.shape[4]), F32) for g in grads],
        scratch_shapes=[pltpu.SemaphoreType.DMA((n,)), pltpu.SemaphoreType.DMA((n,))],
        name="grad_exchange_core_halves",
    )(*grads)


def _add_core_halves(name, g, a, c_idx):
    L, nj, _, rh, X = g.shape
    tr = _tile(rh, 256, SUBLANES)

    def body(c_ref, g_ref, a_ref, o_ref):
        o_ref[...] = g_ref[...] + a_ref[...]

    blk = pl.BlockSpec((None, None, tr, X), lambda l, j, i, c_ref: (l, j, i, 0))
    return _pcall(
        body,
        grid_spec=pltpu.PrefetchScalarGridSpec(
            num_scalar_prefetch=1, grid=(L, nj, rh // tr),
            in_specs=[pl.BlockSpec((None, None, None, tr, X), lambda l, j, i, c_ref: (l, j, c_ref[0], i, 0)), blk],
            out_specs=blk),
        out_shape=_sds((L, nj, rh, X), F32), compiler_params=_cp("parallel", "parallel", "parallel"), name=name,
    )(c_idx, g, a)


def _exchange_chip_shards(parts):
    n = len(parts)

    def body(*refs):
        ins, outs = refs[:n], refs[n:2 * n]
        send_sems, recv_sems = refs[2 * n:]
        x, y, c, chips = _place()
        cps = []
        for t in range(n):
            for k, (cx, cy) in enumerate(chips):
                cps.append(pltpu.make_async_remote_copy(
                    src_ref=ins[t].at[:, 2 * cx + cy], dst_ref=outs[t].at[k], send_sem=send_sems.at[3 * t + k],
                    recv_sem=recv_sems.at[3 * t + k], device_id=(cx, cy, c), device_id_type=MESH))
        for cp in cps:
            cp.start()
        for cp in cps:
            cp.wait()

    return _pcall(
        body, in_specs=[ANY] * n, out_specs=[ANY] * n,
        out_shape=[_sds((3, p.shape[0], p.shape[2], p.shape[3]), F32) for p in parts],
        scratch_shapes=[pltpu.SemaphoreType.DMA((3 * n,)), pltpu.SemaphoreType.DMA((3 * n,))],
        name="grad_exchange_chip_shards",
    )(*parts)


def _add_chip_shards(name, p, b, j_idx):
    L, _, rh, X = p.shape
    tr = _tile(rh, 256, SUBLANES)

    def body(j_ref, p_ref, b_ref, o_ref):
        o_ref[...] = ((p_ref[...] + b_ref[0]) + b_ref[1]) + b_ref[2]

    return _pcall(
        body,
        grid_spec=pltpu.PrefetchScalarGridSpec(
            num_scalar_prefetch=1, grid=(L, rh // tr),
            in_specs=[pl.BlockSpec((None, None, tr, X), lambda l, i, j_ref: (l, j_ref[0], i, 0)),
                      pl.BlockSpec((3, None, tr, X), lambda l, i, j_ref: (0, l, i, 0))],
            out_specs=pl.BlockSpec((None, tr, X), lambda l, i, j_ref: (l, i, 0))),
        out_shape=_sds((L, rh, X), F32), compiler_params=_cp("parallel", "parallel"), name=name,
    )(j_idx, p, b)


def _join_core_halves(halves):
    n = len(halves)

    def body(*refs):
        ins, outs = refs[:n], refs[n:2 * n]
        send_sems, recv_sems, local_sems = refs[2 * n:]
        x, y, c, _ = _place()
        cps, locs = [], []
        for t in range(n):
            mine = outs[t].at[:, c]
            locs.append(pltpu.make_async_copy(ins[t], mine, local_sems.at[t]))
            cps.append(pltpu.make_async_remote_copy(src_ref=ins[t], dst_ref=mine, send_sem=send_sems.at[t],
                                                    recv_sem=recv_sems.at[t], device_id=(x, y, 1 - c), device_id_type=MESH))
        for cp in locs + cps:
            cp.start()
        for t in range(n):
            pltpu.make_async_remote_copy(src_ref=ins[t], dst_ref=outs[t].at[:, 1 - c], send_sem=send_sems.at[t],
                                         recv_sem=recv_sems.at[t], device_id=(x, y, 1 - c), device_id_type=MESH).wait()
        for cp in locs:
            cp.wait()

    outs = _pcall(
        body, in_specs=[ANY] * n, out_specs=[ANY] * n,
        out_shape=[_sds((h.shape[0], 2, h.shape[1], h.shape[2]), F32) for h in halves],
        scratch_shapes=[pltpu.SemaphoreType.DMA((n,)), pltpu.SemaphoreType.DMA((n,)), pltpu.SemaphoreType.DMA((n,))],
        name="grad_join_core_halves",
    )(*halves)
    return [o.reshape(o.shape[0], 2 * o.shape[2], o.shape[3]) for o in outs]


def _all_reduce_small(packed):
    R, C = packed.shape

    def body(x_ref, o_ref, slots, send_sems, recv_sems):
        x, y, c, _ = _place()
        me = 4 * x + 2 * y + c
        slots[me] = x_ref[...]
        cps = []
        for d in range(N_DEV):
            to = (d // 4, (d // 2) % 2, d % 2)
            cp = pltpu.make_async_remote_copy(src_ref=x_ref, dst_ref=slots.at[me], send_sem=send_sems.at[d],
                                              recv_sem=recv_sems.at[me], device_id=to, device_id_type=MESH)
            cps.append(cp)

            @pl.when(d != me)
            def _():
                cp.start()

        for d in range(N_DEV):
            @pl.when(d != me)
            def _():
                pltpu.make_async_remote_copy(src_ref=x_ref, dst_ref=slots.at[d], send_sem=send_sems.at[d],
                                             recv_sem=recv_sems.at[d], device_id=(x, y, c), device_id_type=MESH).wait_recv()
                cps[d].wait_send()

        acc = slots[0]
        for d in range(1, N_DEV):
            acc = acc + slots[d]
        o_ref[...] = acc

    vm = pl.BlockSpec(memory_space=pltpu.VMEM)
    return _pcall(
        body, in_specs=[vm], out_specs=vm, out_shape=_sds((R, C), F32),
        scratch_shapes=[pltpu.VMEM((N_DEV, R, C), F32), pltpu.SemaphoreType.DMA((N_DEV,)), pltpu.SemaphoreType.DMA((N_DEV,))],
        compiler_params=pltpu.CompilerParams(vmem_limit_bytes=VMEM_LIMIT_BYTES), name="all_reduce_small",
    )(packed)


PACK = SUBLANES * LANES


def _pack(arrays):
    flat = []
    for a in arrays:
        v = a.reshape(-1)
        flat.append(jnp.pad(v, (0, (-v.shape[0]) % PACK)))
    return jnp.concatenate(flat).reshape(-1, LANES)


def _unpack(packed, shapes):
    flat = packed.reshape(-1)
    out, pos = [], 0
    for s in shapes:
        n = 1
        for d in s:
            n *= d
        out.append(flat[pos:pos + n].reshape(s))
        pos += n + (-n) % PACK
    return out


def kernel(x, mix_norm_g, ffn_norm_g, conv_w_in, conv_a_dw_w, conv_a_dw_b, conv_a_ln_g, conv_a_ln_b, conv_b_dw_w, conv_w_out, attn_w_qkv, attn_q_g, attn_k_g, attn_w_o, ffn_w_up, ffn_dw_w, ffn_dw_b, ffn_w_down, loss_target, m_mix_norm_g, m_ffn_norm_g, m_conv_w_in, m_conv_a_dw_w, m_conv_a_dw_b, m_conv_a_ln_g, m_conv_a_ln_b, m_conv_b_dw_w, m_conv_w_out, m_attn_w_qkv, m_attn_q_g, m_attn_k_g, m_attn_w_o, m_ffn_w_up, m_ffn_dw_w, m_ffn_dw_b, m_ffn_w_down, v_mix_norm_g, v_ffn_norm_g, v_conv_w_in, v_conv_a_dw_w, v_conv_a_dw_b, v_conv_a_ln_g, v_conv_a_ln_b, v_conv_b_dw_w, v_conv_w_out, v_attn_w_qkv, v_attn_q_g, v_attn_k_g, v_attn_w_o, v_ffn_w_up, v_ffn_dw_w, v_ffn_dw_b, v_ffn_w_down):
    depth = mix_norm_g.shape[0]
    n_even, n_odd = conv_w_in.shape[0], attn_w_qkv.shape[0]
    S, D = x.shape[1], x.shape[2]
    dg = D // 2
    x0 = x.reshape(S, D)
    target = loss_target.reshape(S, D)
    j_me = 2 * lax.axis_index("x") + lax.axis_index("y")
    c_me = lax.axis_index("c")
    j_idx = j_me.astype(jnp.int32).reshape(1)
    c_idx = c_me.astype(jnp.int32).reshape(1)

    col_names = ["conv_w_in", "attn_w_qkv", "ffn_w_up"]
    row_names = ["conv_w_out", "attn_w_o", "ffn_w_down"]
    (w_in, w_qkv, w_up), (w_out, w_o, w_down), (a_dw, b_dw, f_dw) = _all_gather_weights(
        [conv_w_in.astype(BF16), attn_w_qkv.astype(BF16), ffn_w_up.astype(BF16)],
        [conv_w_out.astype(BF16), attn_w_o.astype(BF16), ffn_w_down.astype(BF16)],
        [conv_a_dw_w, conv_b_dw_w, ffn_dw_w])
    unshard = lambda a: jnp.moveaxis(a, 1, 2).reshape(a.shape[0], a.shape[2], N_CHIPS * a.shape[3])
    a_dw, b_dw, f_dw = unshard(a_dw), unshard(b_dw), unshard(f_dw)
    qk_gain = [jnp.stack([jnp.tile(attn_q_g[i], LANES // HEAD_DIM), jnp.tile(attn_k_g[i], LANES // HEAD_DIM)])
               for i in range(n_odd)]

    saved = []
    xc = x0
    for layer in range(depth):
        i = layer // 2
        tag = f"l{layer}"
        s = {"x_in": xc}
        h = _rms_fwd(f"rms_mix_fwd_{tag}", xc, mix_norm_g, layer)
        s["h"] = h
        if layer % 2 == 0:
            p = _mm_fwd(f"conv_in_fwd_{tag}", h, w_in, i, colshard=True)
            ab = _convmix_fwd(f"convmix_fwd_{tag}", p, a_dw, conv_a_dw_b, conv_a_ln_g, conv_a_ln_b, b_dw, i)
            xm = _mm_fwd(f"conv_out_fwd_{tag}", ab, w_out, i, colshard=False, res=xc)
            s.update(p=p, ab=ab)
        else:
            qkv = _mm_fwd(f"attn_qkv_fwd_{tag}", h, w_qkv, i, colshard=True)
            qs, kn, vb = _qknorm_fwd(f"qknorm_fwd_{tag}", qkv, qk_gain[i])
            o = _attn_fwd(f"attn_fwd_{tag}", qs, kn, vb)
            xm = _mm_fwd(f"attn_out_fwd_{tag}", o, w_o, i, colshard=False, res=xc)
            s.update(qkv=qkv, qs=qs, kn=kn, vb=vb, o=o)
        s["x_mid"] = xm
        h2 = _rms_fwd(f"rms_ffn_fwd_{tag}", xm, ffn_norm_g, layer)
        u2 = _mm_fwd(f"ffn_up_fwd_{tag}", h2, w_up, layer, colshard=True, out_split=2)
        f = _ffn_mid_fwd(f"ffn_mid_fwd_{tag}", u2, f_dw, ffn_dw_b, layer)
        xc = _mm_fwd(f"ffn_down_fwd_{tag}", f, w_down, layer, colshard=False, res=xm)
        s.update(h2=h2, u2=u2, f=f)
        saved.append(s)

    dx, loss_tile = _loss_fwd_bwd("loss", xc, target)

    g_up = g_down = g_in = g_out = g_qkv = g_o = None
    d_mix_g, d_ffn_g = [None] * depth, [None] * depth
    d_ffn_dw_w, d_ffn_dw_b = [None] * depth, [None] * depth
    d_a_dw_w, d_a_dw_b, d_a_ln_g, d_a_ln_b, d_b_dw_w = ([None] * n_even for _ in range(5))
    d_q_g, d_k_g = [None] * n_odd, [None] * n_odd
    for layer in reversed(range(depth)):
        i = layer // 2
        tag = f"l{layer}"
        s = saved[layer]
        df = _mm_dgrad(f"ffn_down_dgrad_{tag}", dx, w_down, layer, colshard=False)
        g_down = _mm_wgrad(f"ffn_down_wgrad_{tag}", s["f"], dx, layer, depth, g_down, colshard=False)
        du2, dww, dwb = _ffn_mid_bwd(f"ffn_mid_bwd_{tag}", s["u2"], df, f_dw, ffn_dw_b, layer)
        d_ffn_dw_w[layer] = jnp.moveaxis(dww, 0, 1).reshape(FFN_CONV_WIDTH, -1)
        d_ffn_dw_b[layer] = dwb.reshape(-1)
        dh2 = _mm_dgrad(f"ffn_up_dgrad_{tag}", du2, w_up, layer, colshard=True)
        g_up = _mm_wgrad(f"ffn_up_wgrad_{tag}", s["h2"], du2, layer, depth, g_up, colshard=True)
        dx, dg_ = _rms_bwd(f"rms_ffn_bwd_{tag}", s["x_mid"], ffn_norm_g, layer, dh2, dx)
        d_ffn_g[layer] = dg_.reshape(-1)
        if layer % 2 == 0:
            dab = _mm_dgrad(f"conv_out_dgrad_{tag}", dx, w_out, i, colshard=False)
            g_out = _mm_wgrad(f"conv_out_wgrad_{tag}", s["ab"], dx, i, n_even, g_out, colshard=False)
            dp, daw, dab_b, dlg, dlb, dbw = _convmix_bwd(f"convmix_bwd_{tag}", s["p"], dab, a_dw, conv_a_dw_b, conv_a_ln_g,
                                                         conv_a_ln_b, b_dw, i)
            d_a_dw_w[i], d_a_dw_b[i], d_a_ln_g[i], d_a_ln_b[i], d_b_dw_w[i] = (
                daw, dab_b.reshape(-1), dlg.reshape(-1), dlb.reshape(-1), dbw)
            dh = _mm_dgrad(f"conv_in_dgrad_{tag}", dp, w_in, i, colshard=True)
            g_in = _mm_wgrad(f"conv_in_wgrad_{tag}", s["h"], dp, i, n_even, g_in, colshard=True)
        else:
            do = _mm_dgrad(f"attn_out_dgrad_{tag}", dx, w_o, i, colshard=False)
            g_o = _mm_wgrad(f"attn_out_wgrad_{tag}", s["o"], dx, i, n_odd, g_o, colshard=False)
            dq, dk, dv = _attn_bwd(f"attn_bwd_{tag}", s["qs"], s["kn"], s["vb"], s["o"], do)
            dqkv, dgain = _qknorm_bwd(f"qknorm_bwd_{tag}", s["qkv"], dq, dk, dv, qk_gain[i])
            d_q_g[i] = dgain[0, :HEAD_DIM] + dgain[0, HEAD_DIM:]
            d_k_g[i] = dgain[1, :HEAD_DIM] + dgain[1, HEAD_DIM:]
            dh = _mm_dgrad(f"attn_qkv_dgrad_{tag}", dqkv, w_qkv, i, colshard=True)
            g_qkv = _mm_wgrad(f"attn_qkv_wgrad_{tag}", s["h"], dqkv, i, n_odd, g_qkv, colshard=True)
        dx, dg_ = _rms_bwd(f"rms_mix_bwd_{tag}", s["x_in"], mix_norm_g, layer, dh, dx)
        d_mix_g[layer] = dg_.reshape(-1)
    grad_x = dx.reshape(1, S, D)

    small = {
        "mix_norm_g": jnp.stack(d_mix_g), "ffn_norm_g": jnp.stack(d_ffn_g),
        "conv_a_dw_w": jnp.stack(d_a_dw_w), "conv_a_dw_b": jnp.stack(d_a_dw_b),
        "conv_a_ln_g": jnp.stack(d_a_ln_g), "conv_a_ln_b": jnp.stack(d_a_ln_b),
        "conv_b_dw_w": jnp.stack(d_b_dw_w), "attn_q_g": jnp.stack(d_q_g), "attn_k_g": jnp.stack(d_k_g),
        "ffn_dw_w": jnp.stack(d_ffn_dw_w), "ffn_dw_b": jnp.stack(d_ffn_dw_b),
    }
    small_names = list(small)
    summed = _all_reduce_small(_pack([loss_tile] + [small[n] for n in small_names]))
    parts = _unpack(summed, [loss_tile.shape] + [small[n].shape for n in small_names])
    loss = parts[0][0, 0]
    small_g = dict(zip(small_names, parts[1:]))
    for n in ("conv_a_dw_w", "conv_b_dw_w", "ffn_dw_w"):
        cs = small_g[n].shape[2] // N_CHIPS
        small_g[n] = lax.dynamic_slice_in_dim(small_g[n], j_me * cs, cs, axis=2)

    big = {"conv_w_in": g_in, "attn_w_qkv": g_qkv, "ffn_w_up": g_up, "conv_w_out": g_out, "attn_w_o": g_o, "ffn_w_down": g_down}
    big_names = col_names + row_names
    five = []
    for n in big_names:
        g = big[n]
        if n in col_names:
            five.append(g.reshape(g.shape[0], N_CHIPS, 2, g.shape[2] // 2, g.shape[3]))
        else:
            five.append(g.reshape(g.shape[0], N_CHIPS, 2, g.shape[1] // (2 * N_CHIPS), g.shape[2]))
    from_sibling = _exchange_core_halves(five)
    chip_sums = [_add_core_halves(f"grad_add_core_{n}", g, a, c_idx) for n, g, a in zip(big_names, five, from_sibling)]
    from_chips = _exchange_chip_shards(chip_sums)
    totals = [_add_chip_shards(f"grad_add_chips_{n}", p, b, j_idx) for n, p, b in zip(big_names, chip_sums, from_chips)]
    big_g = dict(zip(big_names, _join_core_halves(totals)))

    weights = dict(mix_norm_g=mix_norm_g, ffn_norm_g=ffn_norm_g, conv_w_in=conv_w_in, conv_a_dw_w=conv_a_dw_w, conv_a_dw_b=conv_a_dw_b, conv_a_ln_g=conv_a_ln_g, conv_a_ln_b=conv_a_ln_b, conv_b_dw_w=conv_b_dw_w, conv_w_out=conv_w_out, attn_w_qkv=attn_w_qkv, attn_q_g=attn_q_g, attn_k_g=attn_k_g, attn_w_o=attn_w_o, ffn_w_up=ffn_w_up, ffn_dw_w=ffn_dw_w, ffn_dw_b=ffn_dw_b, ffn_w_down=ffn_w_down)
    m_in = dict(mix_norm_g=m_mix_norm_g, ffn_norm_g=m_ffn_norm_g, conv_w_in=m_conv_w_in, conv_a_dw_w=m_conv_a_dw_w, conv_a_dw_b=m_conv_a_dw_b, conv_a_ln_g=m_conv_a_ln_g, conv_a_ln_b=m_conv_a_ln_b, conv_b_dw_w=m_conv_b_dw_w, conv_w_out=m_conv_w_out, attn_w_qkv=m_attn_w_qkv, attn_q_g=m_attn_q_g, attn_k_g=m_attn_k_g, attn_w_o=m_attn_w_o, ffn_w_up=m_ffn_w_up, ffn_dw_w=m_ffn_dw_w, ffn_dw_b=m_ffn_dw_b, ffn_w_down=m_ffn_w_down)
    v_in = dict(mix_norm_g=v_mix_norm_g, ffn_norm_g=v_ffn_norm_g, conv_w_in=v_conv_w_in, conv_a_dw_w=v_conv_a_dw_w, conv_a_dw_b=v_conv_a_dw_b, conv_a_ln_g=v_conv_a_ln_g, conv_a_ln_b=v_conv_a_ln_b, conv_b_dw_w=v_conv_b_dw_w, conv_w_out=v_conv_w_out, attn_w_qkv=v_attn_w_qkv, attn_q_g=v_attn_q_g, attn_k_g=v_attn_k_g, attn_w_o=v_attn_w_o, ffn_w_up=v_ffn_w_up, ffn_dw_w=v_ffn_dw_w, ffn_dw_b=v_ffn_dw_b, ffn_w_down=v_ffn_w_down)
    order = list(weights)
    grads, delta, new_m, new_v = {}, {}, {}, {}
    for n in big_names:
        grads[n] = big_g[n]
        delta[n], new_m[n], new_v[n] = _adamw(f"adamw_{n}", weights[n], big_g[n], m_in[n], v_in[n])
    shapes = [weights[n].shape for n in small_names]
    packed = [_pack([d[n] for n in small_names]) for d in (weights, small_g, m_in, v_in)]
    upd = _adamw("adamw_small", *[p[None] for p in packed])
    for out, res in zip((delta, new_m, new_v), upd):
        out.update(zip(small_names, _unpack(res[0], shapes)))
    grads.update({n: small_g[n].reshape(weights[n].shape) for n in small_names})
    return (loss, grad_x, *[grads[n] for n in order], *[delta[n] for n in order], *[new_m[n] for n in order],
            *[new_v[n] for n in order])
```

```python
import jax
import jax.numpy as jnp
from jax import lax
from jax.experimental import pallas as pl
from jax.experimental.pallas import tpu as pltpu

F32 = jnp.float32
BF16 = jnp.bfloat16
EPS = 1e-6
CONV_A_WIDTH = 31
CONV_B_WIDTH = 3
FFN_CONV_WIDTH = 3
HEAD_DIM = 64
ADAM_LR = 0.001
ADAM_B1 = 0.9
ADAM_B2 = 0.999
ADAM_EPS = 1e-08
ADAM_WD = 0.01
ADAM_STEP = 10

LANES = 128
SUBLANES = 8
BF16_ROWS = 16
V7X_VMEM_BYTES = 64 * 1024 * 1024
VMEM_LIMIT_BYTES = V7X_VMEM_BYTES * 3 // 4
MM_VMEM_BUDGET = VMEM_LIMIT_BYTES * 4 // 5
MM_ROWS = 1024
N_CHIPS = 4
N_DEV = 8
HALO_A = 32
HALO_S = 8
ELT_ROWS = 64
ATTN_BLOCK = 128
ATTN_SUB = 2
EXP_UNDERFLOW = -104.0
MESH = pl.DeviceIdType.MESH
ANY = pl.BlockSpec(memory_space=pl.ANY)
NT = (((1,), (1,)), ((), ()))
NN = (((1,), (0,)), ((), ()))
TN = (((0,), (0,)), ((), ()))


def _pcall(body, **kw):
    return pl.pallas_call(body, **kw)


def _cp(*sem):
    return pltpu.CompilerParams(dimension_semantics=sem, vmem_limit_bytes=VMEM_LIMIT_BYTES)


def _sds(shape, dtype):
    return jax.ShapeDtypeStruct(tuple(shape), dtype)


def _tile(n, cap, align=LANES):
    if n <= cap:
        return n
    for t in range(cap - cap % align, 0, -align):
        if n % t == 0:
            return t
    return n


def _sig(x):
    return 0.5 * jnp.tanh(0.5 * x) + 0.5


def _rowsum(x):
    return jnp.sum(x, axis=0, keepdims=True)


def _fold(x):
    acc = x[0:SUBLANES]
    for r in range(SUBLANES, x.shape[0], SUBLANES):
        acc = acc + x[r:r + SUBLANES]
    return acc


def _mm_call(name, dn, operands, in_specs, out_shape, out_spec, grid, nk, acc_shape, has_res, has_alias):
    def body(*refs):
        a_ref, b_ref = refs[0], refs[1]
        pos = 2
        res_ref = refs[pos] if has_res else None
        pos += int(has_res) + int(has_alias)
        o_ref = refs[pos]
        acc_ref = refs[pos + 1] if nk > 1 else None
        p = lax.dot_general(a_ref[...].astype(BF16), b_ref[...].astype(BF16), dn, preferred_element_type=F32)

        def finish(v):
            if has_res:
                v = v + res_ref[...]
            o_ref[...] = v.astype(o_ref.dtype)

        if nk == 1:
            finish(p)
        else:
            k = pl.program_id(2)

            @pl.when(k == 0)
            def _():
                acc_ref[...] = p

            @pl.when(k > 0)
            def _():
                acc_ref[...] += p

            @pl.when(k == nk - 1)
            def _():
                finish(acc_ref[...])

    aliases = {len(operands) - 1: 0} if has_alias else {}
    return _pcall(
        body, grid=grid, in_specs=in_specs, out_specs=out_spec, out_shape=out_shape,
        scratch_shapes=[pltpu.VMEM(acc_shape, F32)] if nk > 1 else [],
        input_output_aliases=aliases, compiler_params=_cp("parallel", "parallel", "arbitrary"), name=name,
    )(*operands)


def _mm_fwd(name, a, w, l, *, colshard, res=None, out_split=1):
    M, K = a.shape
    tm = _tile(M, MM_ROWS, BF16_ROWS)
    if colshard:
        cs = w.shape[3]
        N, tn, tk = N_CHIPS * cs, cs, K
        b_spec = pl.BlockSpec((None, None, tk, tn), lambda j, i, k: (l, j, k, 0))
    else:
        N = w.shape[2]
        tn, tk = _tile(N, 1024), _tile(K, 1536)
        b_spec = pl.BlockSpec((None, tk, tn), lambda j, i, k: (l, k, j))
    nk = K // tk
    in_specs = [pl.BlockSpec((tm, tk), lambda j, i, k: (i, k)), b_spec]
    operands = [a, w]
    if res is not None:
        in_specs.append(pl.BlockSpec((tm, tn), lambda j, i, k: (i, j)))
        operands.append(res)
    if out_split == 1:
        out_shape = _sds((M, N), F32)
        out_spec = pl.BlockSpec((tm, tn), lambda j, i, k: (i, j))
    else:
        per = N // tn // out_split
        out_shape = _sds((out_split, M, N // out_split), F32)
        out_spec = pl.BlockSpec((None, tm, tn), lambda j, i, k: (j // per, i, j % per))
    return _mm_call(name, NN, operands, in_specs, out_shape, out_spec, (N // tn, M // tm, nk), nk, (tm, tn),
                    res is not None, False)


def _mm_dgrad(name, g, w, l, *, colshard):
    split = g.ndim == 3
    M = g.shape[-2]
    tm = _tile(M, MM_ROWS, BF16_ROWS)
    if colshard:
        kw, cs = w.shape[2], w.shape[3]
        tn, tk, nk = _tile(kw, 1408), cs, N_CHIPS
        b_spec = pl.BlockSpec((None, None, tn, tk), lambda j, i, k: (l, k, j, 0))
    else:
        kw, ncon = w.shape[1], w.shape[2]
        tn, tk = _tile(kw, 1408), _tile(ncon, 1536)
        nk = ncon // tk
        b_spec = pl.BlockSpec((None, tn, tk), lambda j, i, k: (l, j, k))
    if split:
        per = nk // g.shape[0]
        a_spec = pl.BlockSpec((None, tm, tk), lambda j, i, k: (k // per, i, k % per))
    else:
        a_spec = pl.BlockSpec((tm, tk), lambda j, i, k: (i, k))
    out_shape = _sds((M, kw), F32)
    out_spec = pl.BlockSpec((tm, tn), lambda j, i, k: (i, j))
    return _mm_call(name, NT, [g, w], [a_spec, b_spec], out_shape, out_spec, (kw // tn, M // tm, nk), nk, (tm, tn),
                    False, False)


def _mm_wgrad(name, a, g, l, n_layers, buf, *, colshard):
    S, M = a.shape
    split = g.ndim == 3
    N = g.shape[-1] * (g.shape[0] if split else 1)
    tm = _tile(M, 1408)
    tn = N // N_CHIPS if colshard else _tile(N, 1024)
    per_row = 2 * (tm * a.dtype.itemsize + tn * g.dtype.itemsize)
    tk = _tile(S, max(BF16_ROWS, min(2048, (MM_VMEM_BUDGET - 3 * tm * tn * 4) // per_row)), BF16_ROWS)
    nk = S // tk
    if colshard:
        out_shape = _sds((n_layers, N_CHIPS, M, tn), F32)
        out_spec = pl.BlockSpec((None, None, tm, tn), lambda j, i, k: (l, j, i, 0))
    else:
        out_shape = _sds((n_layers, M, N), F32)
        out_spec = pl.BlockSpec((None, tm, tn), lambda j, i, k: (l, i, j))
    if split:
        per = N // tn // g.shape[0]
        b_spec = pl.BlockSpec((None, tk, tn), lambda j, i, k: (j // per, k, j % per))
    else:
        b_spec = pl.BlockSpec((tk, tn), lambda j, i, k: (k, j))
    in_specs = [pl.BlockSpec((tk, tm), lambda j, i, k: (k, i)), b_spec]
    operands = [a, g]
    if buf is not None:
        in_specs.append(ANY)
        operands.append(buf)
    return _mm_call(name, TN, operands, in_specs, out_shape, out_spec, (N // tn, M // tm, nk), nk, (tm, tn),
                    False, buf is not None)


def _rms_fwd(name, x, g, l):
    S, D = x.shape
    tm = _tile(S, 512, BF16_ROWS)

    def body(x_ref, g_ref, o_ref):
        xf = x_ref[...]
        r = lax.rsqrt(jnp.mean(xf * xf, axis=-1, keepdims=True) + EPS)
        o_ref[...] = (xf * r * g_ref[l:l + 1, :]).astype(BF16)

    return _pcall(
        body, grid=(S // tm,),
        in_specs=[pl.BlockSpec((tm, D), lambda i: (i, 0)), pl.BlockSpec(g.shape, lambda i: (0, 0))],
        out_specs=pl.BlockSpec((tm, D), lambda i: (i, 0)), out_shape=_sds((S, D), BF16),
        compiler_params=_cp("parallel"), name=name,
    )(x, g)


def _rms_bwd(name, x, g, l, dh, dres):
    S, D = x.shape
    tm = _tile(S, 512, SUBLANES)

    def body(x_ref, g_ref, dh_ref, dr_ref, dx_ref, dg_ref):
        xf = x_ref[...]
        r = lax.rsqrt(jnp.mean(xf * xf, axis=-1, keepdims=True) + EPS)
        xh = xf * r
        d = dh_ref[...]
        dxh = d * g_ref[l:l + 1, :]
        dx_ref[...] = dr_ref[...] + r * (dxh - xh * jnp.mean(dxh * xh, axis=-1, keepdims=True))

        @pl.when(pl.program_id(0) == 0)
        def _():
            dg_ref[...] = jnp.zeros_like(dg_ref)

        dg_ref[...] += _rowsum(d * xh)

    row = pl.BlockSpec((tm, D), lambda i: (i, 0))
    return _pcall(
        body, grid=(S // tm,),
        in_specs=[row, pl.BlockSpec(g.shape, lambda i: (0, 0)), row, row],
        out_specs=[row, pl.BlockSpec((1, D), lambda i: (0, 0))],
        out_shape=[_sds((S, D), F32), _sds((1, D), F32)],
        compiler_params=_cp("arbitrary"), name=name,
    )(x, g, dh, dres)


def _loss_fwd_bwd(name, y, t):
    S, D = y.shape
    tm = _tile(S, 512, SUBLANES)

    def body(y_ref, t_ref, dy_ref, l_ref):
        e = y_ref[...] - t_ref[...]
        dy_ref[...] = e * (1.0 / D)

        @pl.when(pl.program_id(0) == 0)
        def _():
            l_ref[...] = jnp.zeros_like(l_ref)

        l_ref[...] += 0.5 * jnp.sum(jnp.sum(e * e, axis=-1, keepdims=True) * (1.0 / D), axis=0, keepdims=True)

    row = pl.BlockSpec((tm, D), lambda i: (i, 0))
    return _pcall(
        body, grid=(S // tm,), in_specs=[row, row],
        out_specs=[row, pl.BlockSpec((SUBLANES, LANES), lambda i: (0, 0))],
        out_shape=[_sds((S, D), F32), _sds((SUBLANES, LANES), F32)],
        compiler_params=_cp("arbitrary"), name=name,
    )(y, t)


def _convmix_fwd(name, p, aw, ab, lg, lb, bw, l):
    S, W = p.shape
    dg = W // 5
    tm = _tile(S, 256, HALO_A)
    nb = tm // HALO_A
    ka, kb = CONV_A_WIDTH, CONV_B_WIDTH

    def body(p_ref, ph_ref, aw_ref, ab_ref, lg_ref, lb_ref, bw_ref, o_ref, uext, mext):
        first = pl.program_id(0) == 0
        ph = ph_ref[...]
        pc = p_ref[...]
        uext[pl.ds(0, HALO_A), :] = jnp.where(first, 0.0, ph[:, 0:dg] * _sig(ph[:, dg:2 * dg]))
        uext[pl.ds(HALO_A, tm), :] = pc[:, 0:dg] * _sig(pc[:, dg:2 * dg])
        mext[pl.ds(0, HALO_A), :] = jnp.where(first, 0.0, ph[:, 3 * dg:4 * dg] * ph[:, 4 * dg:5 * dg])
        mext[pl.ds(HALO_A, tm), :] = pc[:, 3 * dg:4 * dg] * pc[:, 4 * dg:5 * dg]
        acc = jnp.zeros((tm, dg), F32) + ab_ref[l:l + 1, :]
        for k in range(ka):
            acc = acc + aw_ref[l, k:k + 1, :] * uext[pl.ds(HALO_A - (ka - 1) + k, tm), :]
        mu = jnp.mean(acc, axis=-1, keepdims=True)
        xc = acc - mu
        ln = xc * lax.rsqrt(jnp.mean(xc * xc, axis=-1, keepdims=True) + EPS) * lg_ref[l:l + 1, :] + lb_ref[l:l + 1, :]
        o_ref[:, 0:dg] = (ln * _sig(ln)).astype(BF16)
        cb = jnp.zeros((tm, dg), F32)
        for k in range(kb):
            cb = cb + bw_ref[l, k:k + 1, :] * mext[pl.ds(HALO_A - (kb - 1) + k, tm), :]
        o_ref[:, dg:2 * dg] = (pc[:, 2 * dg:3 * dg] * cb).astype(BF16)

    full = lambda a: pl.BlockSpec(a.shape, lambda i: (0,) * a.ndim)
    return _pcall(
        body, grid=(S // tm,),
        in_specs=[pl.BlockSpec((tm, W), lambda i: (i, 0)),
                  pl.BlockSpec((HALO_A, W), lambda i: (jnp.maximum(i * nb - 1, 0), 0)),
                  full(aw), full(ab), full(lg), full(lb), full(bw)],
        out_specs=pl.BlockSpec((tm, 2 * dg), lambda i: (i, 0)), out_shape=_sds((S, 2 * dg), BF16),
        scratch_shapes=[pltpu.VMEM((HALO_A + tm, dg), F32), pltpu.VMEM((HALO_A + tm, dg), F32)],
        compiler_params=_cp("parallel"), name=name,
    )(p, p, aw, ab, lg, lb, bw)


def _convmix_bwd(name, p, dab, aw, ab, lg, lb, bw, l):
    S, W = p.shape
    dg = W // 5
    tm = _tile(S, 256, HALO_A)
    nb = tm // HALO_A
    n_i = S // tm
    ka, kb = CONV_A_WIDTH, CONV_B_WIDTH
    n = tm + HALO_A
    ext = HALO_A + n

    def body(p_ref, pp_ref, pn_ref, d_ref, dn_ref, aw_ref, ab_ref, lg_ref, lb_ref, bw_ref,
             dp_ref, daw_ref, dab_ref, dlg_ref, dlb_ref, dbw_ref, uext, mext, gext, dcext, dbext):
        i = pl.program_id(0)
        first, last = i == 0, i == n_i - 1

        @pl.when(first)
        def _():
            for r in (daw_ref, dab_ref, dlg_ref, dlb_ref, dbw_ref):
                r[...] = jnp.zeros_like(r)

        pp, pc, pn = pp_ref[...], p_ref[...], pn_ref[...]
        glu = lambda b: b[:, 0:dg] * _sig(b[:, dg:2 * dg])
        gch = lambda b: b[:, 3 * dg:4 * dg] * b[:, 4 * dg:5 * dg]
        uext[pl.ds(0, HALO_A), :] = jnp.where(first, 0.0, glu(pp))
        uext[pl.ds(HALO_A, tm), :] = glu(pc)
        uext[pl.ds(HALO_A + tm, HALO_A), :] = glu(pn)
        mext[pl.ds(0, HALO_A), :] = jnp.where(first, 0.0, gch(pp))
        mext[pl.ds(HALO_A, tm), :] = gch(pc)
        mext[pl.ds(HALO_A + tm, HALO_A), :] = gch(pn)

        c = jnp.zeros((n, dg), F32) + ab_ref[l:l + 1, :]
        for k in range(ka):
            c = c + aw_ref[l, k:k + 1, :] * uext[pl.ds(HALO_A - (ka - 1) + k, n), :]
        xc = c - jnp.mean(c, axis=-1, keepdims=True)
        rstd = lax.rsqrt(jnp.mean(xc * xc, axis=-1, keepdims=True) + EPS)
        chat = xc * rstd
        g_ln = lg_ref[l:l + 1, :]
        ln = chat * g_ln + lb_ref[l:l + 1, :]
        s = _sig(ln)
        gext[pl.ds(0, tm), :] = d_ref[:, 0:dg]
        gext[pl.ds(tm, HALO_A), :] = jnp.where(last, 0.0, dn_ref[:, 0:dg])
        dln = gext[...] * (s * (1.0 + ln * (1.0 - s)))
        dlnh = dln * g_ln
        dc = rstd * (dlnh - jnp.mean(dlnh, axis=-1, keepdims=True) - chat * jnp.mean(dlnh * chat, axis=-1, keepdims=True))
        dcext[...] = dc
        dlg_ref[...] += _rowsum((dln * chat)[0:tm])
        dlb_ref[...] += _rowsum(dln[0:tm])
        dab_ref[...] += _rowsum(dc[0:tm])
        du = jnp.zeros((tm, dg), F32)
        for k in range(ka):
            du = du + aw_ref[l, k:k + 1, :] * dcext[pl.ds(ka - 1 - k, tm), :]
            daw_ref[k:k + 1, :] += _rowsum(dcext[pl.ds(0, tm), :] * uext[pl.ds(HALO_A - (ka - 1) + k, tm), :])
        sg = _sig(pc[:, dg:2 * dg])
        dp_ref[:, 0:dg] = (du * sg).astype(BF16)
        dp_ref[:, dg:2 * dg] = (du * pc[:, 0:dg] * sg * (1.0 - sg)).astype(BF16)

        cb = jnp.zeros((tm, dg), F32)
        for k in range(kb):
            cb = cb + bw_ref[l, k:k + 1, :] * mext[pl.ds(HALO_A - (kb - 1) + k, tm), :]
        db = d_ref[:, dg:2 * dg]
        dp_ref[:, 2 * dg:3 * dg] = (db * cb).astype(BF16)
        dbext[pl.ds(0, tm), :] = db * pc[:, 2 * dg:3 * dg]
        dbext[pl.ds(tm, HALO_A), :] = jnp.where(last, 0.0, dn_ref[:, dg:2 * dg] * pn[:, 2 * dg:3 * dg])
        dm = jnp.zeros((tm, dg), F32)
        for k in range(kb):
            dm = dm + bw_ref[l, k:k + 1, :] * dbext[pl.ds(kb - 1 - k, tm), :]
            dbw_ref[k:k + 1, :] += _rowsum(dbext[pl.ds(0, tm), :] * mext[pl.ds(HALO_A - (kb - 1) + k, tm), :])
        dp_ref[:, 3 * dg:4 * dg] = (dm * pc[:, 4 * dg:5 * dg]).astype(BF16)
        dp_ref[:, 4 * dg:5 * dg] = (dm * pc[:, 3 * dg:4 * dg]).astype(BF16)

    full = lambda a: pl.BlockSpec(a.shape, lambda i: (0,) * a.ndim)
    prev = lambda i: (jnp.maximum(i * nb - 1, 0), 0)
    nxt = lambda i: (jnp.minimum((i + 1) * nb, S // HALO_A - 1), 0)
    acc = lambda r: pl.BlockSpec((r, dg), lambda i: (0, 0))
    return _pcall(
        body, grid=(n_i,),
        in_specs=[pl.BlockSpec((tm, W), lambda i: (i, 0)), pl.BlockSpec((HALO_A, W), prev), pl.BlockSpec((HALO_A, W), nxt),
                  pl.BlockSpec((tm, 2 * dg), lambda i: (i, 0)), pl.BlockSpec((HALO_A, 2 * dg), nxt),
                  full(aw), full(ab), full(lg), full(lb), full(bw)],
        out_specs=[pl.BlockSpec((tm, W), lambda i: (i, 0)), acc(ka), acc(1), acc(1), acc(1), acc(kb)],
        out_shape=[_sds((S, W), BF16), _sds((ka, dg), F32), _sds((1, dg), F32), _sds((1, dg), F32), _sds((1, dg), F32),
                   _sds((kb, dg), F32)],
        scratch_shapes=[pltpu.VMEM((ext, dg), F32), pltpu.VMEM((ext, dg), F32), pltpu.VMEM((n, dg), F32),
                        pltpu.VMEM((n, dg), F32), pltpu.VMEM((n, dg), F32)],
        compiler_params=_cp("arbitrary"), name=name,
    )(p, p, p, dab, dab, aw, ab, lg, lb, bw)


def _ffn_mid_fwd(name, u2, dww, dwb, l):
    _, S, F = u2.shape
    tm = _tile(S, 256, BF16_ROWS)
    tc = _tile(F, 1408)
    n_f = F // tc
    nb = tm // HALO_S
    kf = FFN_CONV_WIDTH

    def body(u_ref, uh_ref, wg_ref, wv_ref, bg_ref, bv_ref, o_ref, ext):
        first = pl.program_id(1) == 0
        ext[:, pl.ds(0, HALO_S), :] = jnp.where(first, 0.0, uh_ref[...])
        ext[:, pl.ds(HALO_S, tm), :] = u_ref[...]
        rc = _tile(tm, ELT_ROWS, BF16_ROWS)

        def lane_chunk(ci, carry):
            lanes = pl.ds(pl.multiple_of(ci * LANES, LANES), LANES)
            taps = [[w_ref[k:k + 1, lanes] for k in range(kf)] for w_ref in (wg_ref, wv_ref)]
            bias = [b_ref[l:l + 1, lanes] for b_ref in (bg_ref, bv_ref)]
            for r0 in range(0, tm, rc):
                c = []
                for g in range(2):
                    acc = bias[g]
                    for k in range(kf):
                        acc = acc + taps[g][k] * ext[g, pl.ds(HALO_S - (kf - 1) + k + r0, rc), lanes]
                    c.append(acc)
                o_ref[pl.ds(r0, rc), lanes] = (c[0] * _sig(c[0]) * c[1]).astype(BF16)
            return carry

        lax.fori_loop(0, tc // LANES, lane_chunk, 0)

    n_l = dwb.shape[0]
    return _pcall(
        body, grid=(n_f, S // tm),
        in_specs=[pl.BlockSpec((2, tm, tc), lambda j, i: (0, i, j)),
                  pl.BlockSpec((2, HALO_S, tc), lambda j, i: (0, jnp.maximum(i * nb - 1, 0), j)),
                  pl.BlockSpec((None, kf, tc), lambda j, i: (l, 0, j)),
                  pl.BlockSpec((None, kf, tc), lambda j, i: (l, 0, j + n_f)),
                  pl.BlockSpec((n_l, tc), lambda j, i: (0, j)),
                  pl.BlockSpec((n_l, tc), lambda j, i: (0, j + n_f))],
        out_specs=pl.BlockSpec((tm, tc), lambda j, i: (i, j)), out_shape=_sds((S, F), BF16),
        scratch_shapes=[pltpu.VMEM((2, HALO_S + tm, tc), F32)],
        compiler_params=_cp("parallel", "parallel"), name=name,
    )(u2, u2, dww, dww, dwb, dwb)


def _ffn_mid_bwd(name, u2, df, dww, dwb, l):
    _, S, F = u2.shape
    tm = _tile(S, 256, BF16_ROWS)
    tc = _tile(F, 1408)
    n_f = F // tc
    nb = tm // HALO_S
    n_i = S // tm
    kf = FFN_CONV_WIDTH
    n = tm + HALO_S

    def body(u_ref, up_ref, un_ref, df_ref, dfn_ref, wg_ref, wv_ref, bg_ref, bv_ref,
             du_ref, dw_ref, db_ref, uext, dcext):
        i = pl.program_id(1)
        first, last = i == 0, i == n_i - 1

        @pl.when(first)
        def _():
            dw_ref[...] = jnp.zeros_like(dw_ref)
            db_ref[...] = jnp.zeros_like(db_ref)

        uext[:, pl.ds(0, HALO_S), :] = jnp.where(first, 0.0, up_ref[...])
        uext[:, pl.ds(HALO_S, tm), :] = u_ref[...]
        uext[:, pl.ds(HALO_S + tm, HALO_S), :] = un_ref[...]
        rc = _tile(tm, ELT_ROWS, BF16_ROWS)

        def lane_chunk(ci, carry):
            lanes = pl.ds(pl.multiple_of(ci * LANES, LANES), LANES)
            taps = [[w_ref[k:k + 1, lanes] for k in range(kf)] for w_ref in (wg_ref, wv_ref)]
            bias = [b_ref[l:l + 1, lanes] for b_ref in (bg_ref, bv_ref)]
            acc_w = [[jnp.zeros((SUBLANES, LANES), F32) for _ in range(kf)] for _ in range(2)]
            acc_b = [jnp.zeros((SUBLANES, LANES), F32) for _ in range(2)]
            for r0, rows in [(r, rc) for r in range(0, tm, rc)] + [(tm, HALO_S)]:
                shifted = [[uext[g, pl.ds(HALO_S - (kf - 1) + k + r0, rows), lanes] for k in range(kf)] for g in range(2)]
                conv = []
                for g in range(2):
                    acc = bias[g]
                    for k in range(kf):
                        acc = acc + taps[g][k] * shifted[g][k]
                    conv.append(acc)
                cg, cv = conv
                s = _sig(cg)
                dfe = df_ref[pl.ds(r0, rows), lanes] if r0 < tm else jnp.where(last, 0.0, dfn_ref[:, lanes])
                dc = [dfe * cv * (s * (1.0 + cg * (1.0 - s))), dfe * (cg * s)]
                for g in range(2):
                    dcext[g, pl.ds(r0, rows), lanes] = dc[g]
                    if r0 < tm:
                        acc_b[g] = acc_b[g] + _fold(dc[g])
                        for k in range(kf):
                            acc_w[g][k] = acc_w[g][k] + _fold(dc[g] * shifted[g][k])
            for r0 in range(0, tm, rc):
                for g in range(2):
                    du = taps[g][0] * dcext[g, pl.ds(r0 + kf - 1, rc), lanes]
                    for k in range(1, kf):
                        du = du + taps[g][k] * dcext[g, pl.ds(r0 + kf - 1 - k, rc), lanes]
                    du_ref[g, pl.ds(r0, rc), lanes] = du.astype(BF16)
            for g in range(2):
                db_ref[g, :, lanes] += _rowsum(acc_b[g])
                for k in range(kf):
                    dw_ref[g, k:k + 1, lanes] += _rowsum(acc_w[g][k])
            return carry

        lax.fori_loop(0, tc // LANES, lane_chunk, 0)

    n_l = dwb.shape[0]
    prev = lambda j, i: (0, jnp.maximum(i * nb - 1, 0), j)
    nxt = lambda j, i: (0, jnp.minimum((i + 1) * nb, S // HALO_S - 1), j)
    return _pcall(
        body, grid=(n_f, n_i),
        in_specs=[pl.BlockSpec((2, tm, tc), lambda j, i: (0, i, j)),
                  pl.BlockSpec((2, HALO_S, tc), prev), pl.BlockSpec((2, HALO_S, tc), nxt),
                  pl.BlockSpec((tm, tc), lambda j, i: (i, j)),
                  pl.BlockSpec((HALO_S, tc), lambda j, i: nxt(j, i)[1:]),
                  pl.BlockSpec((None, kf, tc), lambda j, i: (l, 0, j)),
                  pl.BlockSpec((None, kf, tc), lambda j, i: (l, 0, j + n_f)),
                  pl.BlockSpec((n_l, tc), lambda j, i: (0, j)),
                  pl.BlockSpec((n_l, tc), lambda j, i: (0, j + n_f))],
        out_specs=[pl.BlockSpec((2, tm, tc), lambda j, i: (0, i, j)),
                   pl.BlockSpec((2, kf, tc), lambda j, i: (0, 0, j)),
                   pl.BlockSpec((2, 1, tc), lambda j, i: (0, 0, j))],
        out_shape=[_sds((2, S, F), BF16), _sds((2, kf, F), F32), _sds((2, 1, F), F32)],
        scratch_shapes=[pltpu.VMEM((2, HALO_S + n, tc), F32), pltpu.VMEM((2, n, tc), F32)],
        compiler_params=_cp("parallel", "arbitrary"), name=name,
    )(u2, u2, u2, df, df, dww, dww, dwb, dwb)


def _head_sum_matrix():
    r = lax.broadcasted_iota(jnp.int32, (LANES, LANES), 0) // HEAD_DIM
    c = lax.broadcasted_iota(jnp.int32, (LANES, LANES), 1) // HEAD_DIM
    return (r == c).astype(F32)


def _head_mean(x, ones):
    return jnp.dot(x, ones, preferred_element_type=F32, precision=lax.Precision.HIGHEST) * (1.0 / HEAD_DIM)


def _qknorm_fwd(name, qkv, g2):
    S, D3 = qkv.shape
    D = D3 // 3
    tm = _tile(S, 256, BF16_ROWS)
    scale = HEAD_DIM ** -0.5

    def body(q_ref, k_ref, v_ref, g_ref, qo_ref, ko_ref, vo_ref):
        ones = _head_sum_matrix()
        for cc in range(D // LANES):
            sl = slice(cc * LANES, (cc + 1) * LANES)
            for x_ref, o_ref, row, mult in ((q_ref, qo_ref, 0, scale), (k_ref, ko_ref, 1, 1.0)):
                x = x_ref[:, sl]
                r = lax.rsqrt(_head_mean(x * x, ones) + EPS)
                o_ref[:, sl] = ((x * r * g_ref[row:row + 1, :]).astype(BF16) * mult).astype(BF16)
        vo_ref[...] = v_ref[...].astype(BF16)

    col = lambda c: pl.BlockSpec((tm, D), lambda i: (i, c))
    out = pl.BlockSpec((tm, D), lambda i: (i, 0))
    return _pcall(
        body, grid=(S // tm,),
        in_specs=[col(0), col(1), col(2), pl.BlockSpec(g2.shape, lambda i: (0, 0))],
        out_specs=[out, out, out], out_shape=[_sds((S, D), BF16)] * 3,
        compiler_params=_cp("parallel"), name=name,
    )(qkv, qkv, qkv, g2)


def _qknorm_bwd(name, qkv, dq, dk, dv, g2):
    S, D3 = qkv.shape
    D = D3 // 3
    tm = _tile(S, 256, BF16_ROWS)
    scale = HEAD_DIM ** -0.5

    def body(q_ref, k_ref, dq_ref, dk_ref, dv_ref, g_ref, o_ref, dg_ref):
        @pl.when(pl.program_id(0) == 0)
        def _():
            dg_ref[...] = jnp.zeros_like(dg_ref)

        ones = _head_sum_matrix()
        for cc in range(D // LANES):
            sl = slice(cc * LANES, (cc + 1) * LANES)
            for x_ref, d_ref, row, mult, base in ((q_ref, dq_ref, 0, scale, 0), (k_ref, dk_ref, 1, 1.0, D)):
                x = x_ref[:, sl]
                r = lax.rsqrt(_head_mean(x * x, ones) + EPS)
                xh = x * r
                dn = d_ref[:, sl] * mult
                dxh = dn * g_ref[row:row + 1, :]
                dx = r * (dxh - xh * _head_mean(dxh * xh, ones))
                o_ref[:, base + cc * LANES:base + (cc + 1) * LANES] = dx.astype(BF16)
                dg_ref[row:row + 1, :] += _rowsum(dn * xh)
        o_ref[:, 2 * D:3 * D] = dv_ref[...].astype(BF16)

    col = lambda c: pl.BlockSpec((tm, D), lambda i: (i, c))
    row = pl.BlockSpec((tm, D), lambda i: (i, 0))
    return _pcall(
        body, grid=(S // tm,),
        in_specs=[col(0), col(1), row, row, row, pl.BlockSpec(g2.shape, lambda i: (0, 0))],
        out_specs=[pl.BlockSpec((tm, D3), lambda i: (i, 0)), pl.BlockSpec((2, LANES), lambda i: (0, 0))],
        out_shape=[_sds((S, D3), BF16), _sds((2, LANES), F32)],
        compiler_params=_cp("arbitrary"), name=name,
    )(qkv, qkv, dq, dk, dv, g2)


def _attn_consts():
    t = ATTN_BLOCK
    row = lax.broadcasted_iota(jnp.int32, (t, t), 0)
    col = lax.broadcasted_iota(jnp.int32, (t, t), 1)
    lane = lax.broadcasted_iota(jnp.int32, (1, LANES), 1)
    heads = (lane < HEAD_DIM, lane >= HEAD_DIM)
    return row, col, heads


def _split_dot(x, m):
    n = x.shape[0]
    hi = x.astype(BF16)
    lo = (x - hi.astype(F32)).astype(BF16)
    both = jnp.dot(jnp.concatenate([hi, lo], axis=0), m, preferred_element_type=F32)
    return both[:n] + both[n:]


def _log_keep(z):
    return -(jnp.maximum(z, 0.0) + jnp.log(1.0 + jnp.exp(-jnp.abs(z))))


def _stack_heads(a, heads):
    t = ATTN_BLOCK
    zero = jnp.zeros((t, LANES), a.dtype)
    return jnp.concatenate([jnp.where(h, a[s * t:(s + 1) * t], zero) for s in range(a.shape[0] // t) for h in heads], axis=0)


def _side_by_side(a):
    t = ATTN_BLOCK
    return jnp.concatenate([jnp.concatenate([a[2 * s * t:(2 * s + 1) * t], a[(2 * s + 1) * t:(2 * s + 2) * t]], axis=1)
                            for s in range(a.shape[0] // (2 * t))], axis=0)


def _grow(a, rows, cols):
    z = jnp.zeros((rows, cols), F32)
    return z if a is None else jnp.concatenate([z, a], axis=0)


def _attn_fwd(name, qs, kn, vb):
    S, D = qs.shape
    t = ATTN_BLOCK
    tq = ATTN_SUB * t

    def body(q_ref, k_ref, v_ref, o_ref):
        i = pl.program_id(1)
        row, col, heads = _attn_consts()
        after_m = (row > col).astype(BF16)
        causal = col < row
        q_all = _stack_heads(q_ref[...], heads)

        def block(j, q, r, acc, mask):
            off = pl.multiple_of(j * t, t)
            kb = k_ref[pl.ds(off, t), :]
            v2 = _stack_heads(v_ref[pl.ds(off, t), :], heads)
            z = lax.dot_general(q, kb, NT, preferred_element_type=F32)
            lk = _log_keep(z)
            if mask is not None:
                lk = jnp.where(mask, lk, 0.0)
            w = jnp.exp(z + lk + _split_dot(lk, after_m) + r)
            if mask is not None:
                w = jnp.where(mask, w, 0.0)
            acc = acc + jnp.dot(_side_by_side(w.astype(BF16)), v2, preferred_element_type=F32)
            return r + jnp.sum(lk, axis=1, keepdims=True), acc

        r = acc = None
        for s in reversed(range(ATTN_SUB)):
            mask = jnp.concatenate([causal, causal] + [jnp.ones_like(causal)] * (2 * (ATTN_SUB - 1 - s)), axis=0)
            r, acc = block(ATTN_SUB * i + s, q_all[2 * s * t:], _grow(r, 2 * t, 1), _grow(acc, t, LANES), mask)

        def cond(c):
            return jnp.logical_and(c[0] >= 0, jnp.max(c[1]) > EXP_UNDERFLOW)

        def step(c):
            r, a = block(c[0], q_all, c[1], c[2], None)
            return c[0] - 1, r, a

        o_ref[...] = lax.while_loop(cond, step, (ATTN_SUB * i - 1, r, acc))[2]

    n_hp = D // LANES
    blk = pl.BlockSpec((tq, LANES), lambda hp, i: (i, hp))
    seq = pl.BlockSpec((S, LANES), lambda hp, i: (0, hp))
    return _pcall(
        body, grid=(n_hp, S // tq), in_specs=[blk, seq, seq], out_specs=blk, out_shape=_sds((S, D), F32),
        compiler_params=_cp("parallel", "arbitrary"), name=name,
    )(qs, kn, vb)


def _attn_bwd(name, qs, kn, vb, o, do):
    S, D = qs.shape
    t = ATTN_BLOCK
    tq = ATTN_SUB * t

    def body(q_ref, k_ref, v_ref, o_ref, do_ref, dq_ref, dk_ref, dv_ref):
        i = pl.program_id(1)

        @pl.when(i == 0)
        def _():
            dk_ref[...] = jnp.zeros_like(dk_ref)
            dv_ref[...] = jnp.zeros_like(dv_ref)

        row, col, heads = _attn_consts()
        after_m = (row > col).astype(BF16)
        from_m = (row >= col).astype(BF16)
        causal = col < row
        q_all = _stack_heads(q_ref[...], heads)
        dob = do_ref[...].astype(BF16)
        do_all = _stack_heads(dob, heads)
        dsum_all = jnp.sum(_stack_heads(dob.astype(F32) * o_ref[...], heads), axis=1, keepdims=True)

        def block(j, q, dor, dsum, r, es, dq, mask):
            off = pl.multiple_of(j * t, t)
            kb = k_ref[pl.ds(off, t), :]
            vblk = v_ref[pl.ds(off, t), :]
            z = lax.dot_general(q, kb, NT, preferred_element_type=F32)
            lk = _log_keep(z)
            if mask is not None:
                lk = jnp.where(mask, lk, 0.0)
            ls = z + lk
            w = jnp.exp(ls + _split_dot(lk, after_m) + r)
            if mask is not None:
                w = jnp.where(mask, w, 0.0)
            e = w * lax.dot_general(dor, vblk, NT, preferred_element_type=F32)
            before = dsum - (es + _split_dot(e, from_m))
            dz = e - (e + before) * jnp.exp(ls)
            if mask is not None:
                dz = jnp.where(mask, dz, 0.0)
            dzb = dz.astype(BF16)
            dq = dq + jnp.dot(_side_by_side(dzb), _stack_heads(kb, heads), preferred_element_type=F32)
            dk_ref[pl.ds(off, t), :] += lax.dot_general(dzb, q, TN, preferred_element_type=F32)
            dv_ref[pl.ds(off, t), :] += lax.dot_general(w.astype(BF16), dor, TN, preferred_element_type=F32)
            return r + jnp.sum(lk, axis=1, keepdims=True), es + jnp.sum(e, axis=1, keepdims=True), dq

        r = es = dq = None
        for s in reversed(range(ATTN_SUB)):
            mask = jnp.concatenate([causal, causal] + [jnp.ones_like(causal)] * (2 * (ATTN_SUB - 1 - s)), axis=0)
            lo = 2 * s * t
            r, es, dq = block(ATTN_SUB * i + s, q_all[lo:], do_all[lo:], dsum_all[lo:], _grow(r, 2 * t, 1),
                              _grow(es, 2 * t, 1), _grow(dq, t, LANES), mask)

        def cond(c):
            return jnp.logical_and(c[0] >= 0, jnp.max(c[1]) > EXP_UNDERFLOW)

        def step(c):
            r, es, a = block(c[0], q_all, do_all, dsum_all, c[1], c[2], c[3], None)
            return c[0] - 1, r, es, a

        dq_ref[...] = lax.while_loop(cond, step, (ATTN_SUB * i - 1, r, es, dq))[3]

    n_hp = D // LANES
    blk = pl.BlockSpec((tq, LANES), lambda hp, i: (i, hp))
    seq = pl.BlockSpec((S, LANES), lambda hp, i: (0, hp))
    return _pcall(
        body, grid=(n_hp, S // tq), in_specs=[blk, seq, seq, blk, blk], out_specs=[blk, seq, seq],
        out_shape=[_sds((S, D), F32)] * 3, compiler_params=_cp("parallel", "arbitrary"), name=name,
    )(qs, kn, vb, o, do)


def _adamw(name, w, g, m, v):
    L, R, C = w.shape
    tr = _tile(R, 256, SUBLANES)
    c1 = 1.0 - ADAM_B1 ** ADAM_STEP
    c2 = 1.0 - ADAM_B2 ** ADAM_STEP

    def body(w_ref, g_ref, m_ref, v_ref, d_ref, mo_ref, vo_ref):
        gg = g_ref[...]
        mn = ADAM_B1 * m_ref[...] + (1.0 - ADAM_B1) * gg
        vn = ADAM_B2 * v_ref[...] + (1.0 - ADAM_B2) * (gg * gg)
        d_ref[...] = -ADAM_LR * ((mn / c1) / (jnp.sqrt(vn / c2) + ADAM_EPS) + ADAM_WD * w_ref[...])
        mo_ref[...] = mn
        vo_ref[...] = vn

    blk = pl.BlockSpec((None, tr, C), lambda l, i: (l, i, 0))
    return _pcall(
        body, grid=(L, R // tr), in_specs=[blk] * 4, out_specs=[blk] * 3, out_shape=[_sds(w.shape, F32)] * 3,
        compiler_params=_cp("parallel", "parallel"), name=name,
    )(w, g, m, v)


def _place():
    x, y, c = lax.axis_index("x"), lax.axis_index("y"), lax.axis_index("c")
    chips = [(1 - x, y), (x, 1 - y), (1 - x, 1 - y)]
    return x, y, c, chips


def _place_shard(name, w, j_idx):
    L, R, X = w.shape
    rh = R // 2
    tr = _tile(rh, 256, BF16_ROWS)

    def body(j_ref, w_ref, o_ref):
        o_ref[...] = w_ref[...].astype(BF16)

    return _pcall(
        body,
        grid_spec=pltpu.PrefetchScalarGridSpec(
            num_scalar_prefetch=1, grid=(L, 2, rh // tr),
            in_specs=[pl.BlockSpec((None, None, tr, X), lambda l, h, i, j_ref: (l, h, i, 0))],
            out_specs=pl.BlockSpec((None, None, None, tr, X), lambda l, h, i, j_ref: (l, j_ref[0], h, i, 0))),
        out_shape=_sds((L, N_CHIPS, 2, rh, X), BF16), compiler_params=_cp("parallel", "parallel", "parallel"), name=name,
    )(j_idx, w.reshape(L, 2, rh, X))


def _all_gather_weights(bufs, small_ws):
    n_big, n_small = len(bufs), len(small_ws)
    n_in = n_big + n_small

    def body(*refs):
        ins, outs = refs[:n_in], refs[n_in:2 * n_in]
        send_sems, recv_sems, local_sems = refs[2 * n_in:]
        x, y, c, chips = _place()
        j_me = 2 * x + y
        j_of = [2 * cx + cy for cx, cy in chips]
        sibling = (x, y, 1 - c)

        def remote(src, dst, s, to):
            return pltpu.make_async_remote_copy(src_ref=src, dst_ref=dst, send_sem=send_sems.at[s], recv_sem=recv_sems.at[s],
                                                device_id=to, device_id_type=MESH)

        started = []
        for t in range(n_big, n_in):
            loc = pltpu.make_async_copy(ins[t], outs[t].at[:, j_me], local_sems.at[t - n_big])
            loc.start()
            started.append(loc)
        first = []
        for t in range(n_big):
            mine = outs[t].at[:, j_me, c]
            for k in range(3):
                first.append(remote(mine, mine, 6 * t + k, (*chips[k], c)))
        for t in range(n_big, n_in):
            for k in range(3):
                first.append(remote(ins[t], outs[t].at[:, j_me], 6 * n_big + 3 * (t - n_big) + k, (*chips[k], c)))
        for cp in first:
            cp.start()
        passed = []
        for t in range(n_big):
            for k in range(3):
                landed = outs[t].at[:, j_of[k], c]
                remote(landed, landed, 6 * t + k, (*chips[k], c)).wait_recv()
                fwd = remote(landed, landed, 6 * t + 3 + k, sibling)
                fwd.start()
                passed.append(fwd)
        for t in range(n_big):
            for k in range(3):
                other = outs[t].at[:, j_of[k], 1 - c]
                remote(other, other, 6 * t + 3 + k, sibling).wait_recv()
        for t in range(n_big, n_in):
            for k in range(3):
                dst = outs[t].at[:, j_of[k]]
                remote(dst, dst, 6 * n_big + 3 * (t - n_big) + k, (*chips[k], c)).wait_recv()
        for cp in first + passed:
            cp.wait_send()
        for loc in started:
            loc.wait()

    out_shape = [_sds(b.shape, b.dtype) for b in bufs]
    out_shape += [_sds((w.shape[0], N_CHIPS) + w.shape[1:], w.dtype) for w in small_ws]
    n_sem = 6 * n_big + 3 * n_small
    outs = _pcall(
        body, in_specs=[ANY] * n_in, out_specs=[ANY] * n_in, out_shape=out_shape,
        input_output_aliases={t: t for t in range(n_big)},
        scratch_shapes=[pltpu.SemaphoreType.DMA((n_sem,)), pltpu.SemaphoreType.DMA((n_sem,)), pltpu.SemaphoreType.DMA((n_small,))],
        name="all_gather_weights",
    )(*bufs, *small_ws)
    return outs[:n_big], outs[n_big:]


def _exchange_core_halves(grads):
    n = len(grads)

    def body(*refs):
        ins, outs = refs[:n], refs[n:2 * n]
        send_sems, recv_sems = refs[2 * n:]
        x, y, c, _ = _place()
        cps = [pltpu.make_async_remote_copy(src_ref=ins[t].at[:, :, 1 - c], dst_ref=outs[t], send_sem=send_sems.at[t],
                                            recv_sem=recv_sems.at[t], device_id=(x, y, 1 - c), device_id_type=MESH)
               for t in range(n)]
        for cp in cps:
            cp.start()
        for cp in cps:
            cp.wait()

    return _pcall(
        body, in_specs=[ANY] * n, out_specs=[ANY] * n,
        out_shape=[_sds((g.shape[0], g.shape[1], g.shape[3], g.shape[4]), F32) for g in grads],
        scratch_shapes=[pltpu.SemaphoreType.DMA((n,)), pltpu.SemaphoreType.DMA((n,))],
        name="grad_exchange_core_halves",
    )(*grads)


def _add_core_halves(name, g, a, c_idx):
    L, nj, _, rh, X = g.shape
    tr = _tile(rh, 256, BF16_ROWS)

    def body(c_ref, g_ref, a_ref, o_ref, ob_ref):
        s = g_ref[...] + a_ref[...]
        o_ref[...] = s
        ob_ref[...] = s.astype(BF16)

    blk = pl.BlockSpec((None, None, tr, X), lambda l, j, i, c_ref: (l, j, i, 0))
    return _pcall(
        body,
        grid_spec=pltpu.PrefetchScalarGridSpec(
            num_scalar_prefetch=1, grid=(L, nj, rh // tr),
            in_specs=[pl.BlockSpec((None, None, None, tr, X), lambda l, j, i, c_ref: (l, j, c_ref[0], i, 0)), blk],
            out_specs=[blk, blk]),
        out_shape=[_sds((L, nj, rh, X), F32), _sds((L, nj, rh, X), BF16)],
        compiler_params=_cp("parallel", "parallel", "parallel"), name=name,
    )(c_idx, g, a)


def _exchange_chip_shards(parts):
    n = len(parts)

    def body(*refs):
        ins, outs = refs[:n], refs[n:2 * n]
        send_sems, recv_sems = refs[2 * n:]
        x, y, c, chips = _place()
        cps = []
        for t in range(n):
            for k, (cx, cy) in enumerate(chips):
                cps.append(pltpu.make_async_remote_copy(
                    src_ref=ins[t].at[:, 2 * cx + cy], dst_ref=outs[t].at[k], send_sem=send_sems.at[3 * t + k],
                    recv_sem=recv_sems.at[3 * t + k], device_id=(cx, cy, c), device_id_type=MESH))
        for cp in cps:
            cp.start()
        for cp in cps:
            cp.wait()

    return _pcall(
        body, in_specs=[ANY] * n, out_specs=[ANY] * n,
        out_shape=[_sds((3, p.shape[0], p.shape[2], p.shape[3]), p.dtype) for p in parts],
        scratch_shapes=[pltpu.SemaphoreType.DMA((3 * n,)), pltpu.SemaphoreType.DMA((3 * n,))],
        name="grad_exchange_chip_shards",
    )(*parts)


def _add_chip_shards(name, p, b, jc_idx):
    L, _, rh, X = p.shape
    tr = _tile(rh, 256, BF16_ROWS)

    def body(jc_ref, p_ref, b_ref, o_ref):
        o_ref[...] = ((p_ref[...] + b_ref[0].astype(F32)) + b_ref[1].astype(F32)) + b_ref[2].astype(F32)

    return _pcall(
        body,
        grid_spec=pltpu.PrefetchScalarGridSpec(
            num_scalar_prefetch=1, grid=(L, rh // tr),
            in_specs=[pl.BlockSpec((None, None, tr, X), lambda l, i, jc: (l, jc[0], i, 0)),
                      pl.BlockSpec((3, None, tr, X), lambda l, i, jc: (0, l, i, 0))],
            out_specs=pl.BlockSpec((None, None, tr, X), lambda l, i, jc: (l, jc[1], i, 0))),
        out_shape=_sds((L, 2, rh, X), F32), compiler_params=_cp("parallel", "parallel"), name=name,
    )(jc_idx, p, b)


def _join_core_halves(bufs):
    n = len(bufs)

    def body(*refs):
        outs = refs[n:2 * n]
        send_sems, recv_sems = refs[2 * n:]
        x, y, c, _ = _place()
        cps = [pltpu.make_async_remote_copy(src_ref=outs[t].at[:, c], dst_ref=outs[t].at[:, c], send_sem=send_sems.at[t],
                                            recv_sem=recv_sems.at[t], device_id=(x, y, 1 - c), device_id_type=MESH)
               for t in range(n)]
        for cp in cps:
            cp.start()
        for t in range(n):
            pltpu.make_async_remote_copy(src_ref=outs[t].at[:, c], dst_ref=outs[t].at[:, 1 - c], send_sem=send_sems.at[t],
                                         recv_sem=recv_sems.at[t], device_id=(x, y, 1 - c), device_id_type=MESH).wait()

    outs = _pcall(
        body, in_specs=[ANY] * n, out_specs=[ANY] * n, out_shape=[_sds(b.shape, F32) for b in bufs],
        input_output_aliases={t: t for t in range(n)},
        scratch_shapes=[pltpu.SemaphoreType.DMA((n,)), pltpu.SemaphoreType.DMA((n,))],
        name="grad_join_core_halves",
    )(*bufs)
    return [o.reshape(o.shape[0], 2 * o.shape[2], o.shape[3]) for o in outs]


def _all_reduce_small(packed):
    R, C = packed.shape

    def body(x_ref, o_ref, slots, send_sems, recv_sems):
        x, y, c, _ = _place()
        me = 4 * x + 2 * y + c
        slots[me] = x_ref[...]
        cps = []
        for d in range(N_DEV):
            to = (d // 4, (d // 2) % 2, d % 2)
            cp = pltpu.make_async_remote_copy(src_ref=x_ref, dst_ref=slots.at[me], send_sem=send_sems.at[d],
                                              recv_sem=recv_sems.at[me], device_id=to, device_id_type=MESH)
            cps.append(cp)

            @pl.when(d != me)
            def _():
                cp.start()

        for d in range(N_DEV):
            @pl.when(d != me)
            def _():
                pltpu.make_async_remote_copy(src_ref=x_ref, dst_ref=slots.at[d], send_sem=send_sems.at[d],
                                             recv_sem=recv_sems.at[d], device_id=(x, y, c), device_id_type=MESH).wait_recv()
                cps[d].wait_send()

        acc = slots[0]
        for d in range(1, N_DEV):
            acc = acc + slots[d]
        o_ref[...] = acc

    vm = pl.BlockSpec(memory_space=pltpu.VMEM)
    return _pcall(
        body, in_specs=[vm], out_specs=vm, out_shape=_sds((R, C), F32),
        scratch_shapes=[pltpu.VMEM((N_DEV, R, C), F32), pltpu.SemaphoreType.DMA((N_DEV,)), pltpu.SemaphoreType.DMA((N_DEV,))],
        compiler_params=pltpu.CompilerParams(vmem_limit_bytes=VMEM_LIMIT_BYTES), name="all_reduce_small",
    )(packed)


PACK = SUBLANES * LANES


def _pack(arrays):
    flat = []
    for a in arrays:
        v = a.reshape(-1)
        flat.append(jnp.pad(v, (0, (-v.shape[0]) % PACK)))
    return jnp.concatenate(flat).reshape(-1, LANES)


def _unpack(packed, shapes):
    flat = packed.reshape(-1)
    out, pos = [], 0
    for s in shapes:
        n = 1
        for d in s:
            n *= d
        out.append(flat[pos:pos + n].reshape(s))
        pos += n + (-n) % PACK
    return out


def kernel(x, mix_norm_g, ffn_norm_g, conv_w_in, conv_a_dw_w, conv_a_dw_b, conv_a_ln_g, conv_a_ln_b, conv_b_dw_w, conv_w_out, attn_w_qkv, attn_q_g, attn_k_g, attn_w_o, ffn_w_up, ffn_dw_w, ffn_dw_b, ffn_w_down, loss_target, m_mix_norm_g, m_ffn_norm_g, m_conv_w_in, m_conv_a_dw_w, m_conv_a_dw_b, m_conv_a_ln_g, m_conv_a_ln_b, m_conv_b_dw_w, m_conv_w_out, m_attn_w_qkv, m_attn_q_g, m_attn_k_g, m_attn_w_o, m_ffn_w_up, m_ffn_dw_w, m_ffn_dw_b, m_ffn_w_down, v_mix_norm_g, v_ffn_norm_g, v_conv_w_in, v_conv_a_dw_w, v_conv_a_dw_b, v_conv_a_ln_g, v_conv_a_ln_b, v_conv_b_dw_w, v_conv_w_out, v_attn_w_qkv, v_attn_q_g, v_attn_k_g, v_attn_w_o, v_ffn_w_up, v_ffn_dw_w, v_ffn_dw_b, v_ffn_w_down):
    depth = mix_norm_g.shape[0]
    n_even, n_odd = conv_w_in.shape[0], attn_w_qkv.shape[0]
    S, D = x.shape[1], x.shape[2]
    dg = D // 2
    x0 = x.reshape(S, D)
    target = loss_target.reshape(S, D)
    j_me = 2 * lax.axis_index("x") + lax.axis_index("y")
    c_me = lax.axis_index("c")
    j_idx = j_me.astype(jnp.int32).reshape(1)
    c_idx = c_me.astype(jnp.int32).reshape(1)

    col_names = ["conv_w_in", "attn_w_qkv", "ffn_w_up"]
    row_names = ["conv_w_out", "attn_w_o", "ffn_w_down"]
    local = dict(conv_w_in=conv_w_in, attn_w_qkv=attn_w_qkv, ffn_w_up=ffn_w_up, conv_w_out=conv_w_out, attn_w_o=attn_w_o,
                 ffn_w_down=ffn_w_down)
    gathered, (a_dw, b_dw, f_dw) = _all_gather_weights(
        [_place_shard(f"place_{n}", local[n], j_idx) for n in col_names + row_names], [conv_a_dw_w, conv_b_dw_w, ffn_dw_w])
    w_in, w_qkv, w_up = (g.reshape(g.shape[0], N_CHIPS, -1, g.shape[4]) for g in gathered[:3])
    w_out, w_o, w_down = (g.reshape(g.shape[0], -1, g.shape[4]) for g in gathered[3:])
    unshard = lambda a: jnp.moveaxis(a, 1, 2).reshape(a.shape[0], a.shape[2], N_CHIPS * a.shape[3])
    a_dw, b_dw, f_dw = unshard(a_dw), unshard(b_dw), unshard(f_dw)
    qk_gain = [jnp.stack([jnp.tile(attn_q_g[i], LANES // HEAD_DIM), jnp.tile(attn_k_g[i], LANES // HEAD_DIM)])
               for i in range(n_odd)]

    saved = []
    xc = x0
    for layer in range(depth):
        i = layer // 2
        tag = f"l{layer}"
        s = {"x_in": xc}
        h = _rms_fwd(f"rms_mix_fwd_{tag}", xc, mix_norm_g, layer)
        s["h"] = h
        if layer % 2 == 0:
            p = _mm_fwd(f"conv_in_fwd_{tag}", h, w_in, i, colshard=True)
            ab = _convmix_fwd(f"convmix_fwd_{tag}", p, a_dw, conv_a_dw_b, conv_a_ln_g, conv_a_ln_b, b_dw, i)
            xm = _mm_fwd(f"conv_out_fwd_{tag}", ab, w_out, i, colshard=False, res=xc)
            s.update(p=p, ab=ab)
        else:
            qkv = _mm_fwd(f"attn_qkv_fwd_{tag}", h, w_qkv, i, colshard=True)
            qs, kn, vb = _qknorm_fwd(f"qknorm_fwd_{tag}", qkv, qk_gain[i])
            o = _attn_fwd(f"attn_fwd_{tag}", qs, kn, vb)
            xm = _mm_fwd(f"attn_out_fwd_{tag}", o, w_o, i, colshard=False, res=xc)
            s.update(qkv=qkv, qs=qs, kn=kn, vb=vb, o=o)
        s["x_mid"] = xm
        h2 = _rms_fwd(f"rms_ffn_fwd_{tag}", xm, ffn_norm_g, layer)
        u2 = _mm_fwd(f"ffn_up_fwd_{tag}", h2, w_up, layer, colshard=True, out_split=2)
        f = _ffn_mid_fwd(f"ffn_mid_fwd_{tag}", u2, f_dw, ffn_dw_b, layer)
        xc = _mm_fwd(f"ffn_down_fwd_{tag}", f, w_down, layer, colshard=False, res=xm)
        s.update(h2=h2, u2=u2, f=f)
        saved.append(s)

    dx, loss_tile = _loss_fwd_bwd("loss", xc, target)

    g_up = g_down = g_in = g_out = g_qkv = g_o = None
    d_mix_g, d_ffn_g = [None] * depth, [None] * depth
    d_ffn_dw_w, d_ffn_dw_b = [None] * depth, [None] * depth
    d_a_dw_w, d_a_dw_b, d_a_ln_g, d_a_ln_b, d_b_dw_w = ([None] * n_even for _ in range(5))
    d_q_g, d_k_g = [None] * n_odd, [None] * n_odd
    for layer in reversed(range(depth)):
        i = layer // 2
        tag = f"l{layer}"
        s = saved[layer]
        df = _mm_dgrad(f"ffn_down_dgrad_{tag}", dx, w_down, layer, colshard=False)
        g_down = _mm_wgrad(f"ffn_down_wgrad_{tag}", s["f"], dx, layer, depth, g_down, colshard=False)
        du2, dww, dwb = _ffn_mid_bwd(f"ffn_mid_bwd_{tag}", s["u2"], df, f_dw, ffn_dw_b, layer)
        d_ffn_dw_w[layer] = jnp.moveaxis(dww, 0, 1).reshape(FFN_CONV_WIDTH, -1)
        d_ffn_dw_b[layer] = dwb.reshape(-1)
        dh2 = _mm_dgrad(f"ffn_up_dgrad_{tag}", du2, w_up, layer, colshard=True)
        g_up = _mm_wgrad(f"ffn_up_wgrad_{tag}", s["h2"], du2, layer, depth, g_up, colshard=True)
        dx, dg_ = _rms_bwd(f"rms_ffn_bwd_{tag}", s["x_mid"], ffn_norm_g, layer, dh2, dx)
        d_ffn_g[layer] = dg_.reshape(-1)
        if layer % 2 == 0:
            dab = _mm_dgrad(f"conv_out_dgrad_{tag}", dx, w_out, i, colshard=False)
            g_out = _mm_wgrad(f"conv_out_wgrad_{tag}", s["ab"], dx, i, n_even, g_out, colshard=False)
            dp, daw, dab_b, dlg, dlb, dbw = _convmix_bwd(f"convmix_bwd_{tag}", s["p"], dab, a_dw, conv_a_dw_b, conv_a_ln_g,
                                                         conv_a_ln_b, b_dw, i)
            d_a_dw_w[i], d_a_dw_b[i], d_a_ln_g[i], d_a_ln_b[i], d_b_dw_w[i] = (
                daw, dab_b.reshape(-1), dlg.reshape(-1), dlb.reshape(-1), dbw)
            dh = _mm_dgrad(f"conv_in_dgrad_{tag}", dp, w_in, i, colshard=True)
            g_in = _mm_wgrad(f"conv_in_wgrad_{tag}", s["h"], dp, i, n_even, g_in, colshard=True)
        else:
            do = _mm_dgrad(f"attn_out_dgrad_{tag}", dx, w_o, i, colshard=False)
            g_o = _mm_wgrad(f"attn_out_wgrad_{tag}", s["o"], dx, i, n_odd, g_o, colshard=False)
            dq, dk, dv = _attn_bwd(f"attn_bwd_{tag}", s["qs"], s["kn"], s["vb"], s["o"], do)
            dqkv, dgain = _qknorm_bwd(f"qknorm_bwd_{tag}", s["qkv"], dq, dk, dv, qk_gain[i])
            d_q_g[i] = dgain[0, :HEAD_DIM] + dgain[0, HEAD_DIM:]
            d_k_g[i] = dgain[1, :HEAD_DIM] + dgain[1, HEAD_DIM:]
            dh = _mm_dgrad(f"attn_qkv_dgrad_{tag}", dqkv, w_qkv, i, colshard=True)
            g_qkv = _mm_wgrad(f"attn_qkv_wgrad_{tag}", s["h"], dqkv, i, n_odd, g_qkv, colshard=True)
        dx, dg_ = _rms_bwd(f"rms_mix_bwd_{tag}", s["x_in"], mix_norm_g, layer, dh, dx)
        d_mix_g[layer] = dg_.reshape(-1)
    grad_x = dx.reshape(1, S, D)

    small = {
        "mix_norm_g": jnp.stack(d_mix_g), "ffn_norm_g": jnp.stack(d_ffn_g),
        "conv_a_dw_w": jnp.stack(d_a_dw_w), "conv_a_dw_b": jnp.stack(d_a_dw_b),
        "conv_a_ln_g": jnp.stack(d_a_ln_g), "conv_a_ln_b": jnp.stack(d_a_ln_b),
        "conv_b_dw_w": jnp.stack(d_b_dw_w), "attn_q_g": jnp.stack(d_q_g), "attn_k_g": jnp.stack(d_k_g),
        "ffn_dw_w": jnp.stack(d_ffn_dw_w), "ffn_dw_b": jnp.stack(d_ffn_dw_b),
    }
    small_names = list(small)
    summed = _all_reduce_small(_pack([loss_tile] + [small[n] for n in small_names]))
    parts = _unpack(summed, [loss_tile.shape] + [small[n].shape for n in small_names])
    loss = parts[0][0, 0]
    small_g = dict(zip(small_names, parts[1:]))
    for n in ("conv_a_dw_w", "conv_b_dw_w", "ffn_dw_w"):
        cs = small_g[n].shape[2] // N_CHIPS
        small_g[n] = lax.dynamic_slice_in_dim(small_g[n], j_me * cs, cs, axis=2)

    big = {"conv_w_in": g_in, "attn_w_qkv": g_qkv, "ffn_w_up": g_up, "conv_w_out": g_out, "attn_w_o": g_o, "ffn_w_down": g_down}
    big_names = col_names + row_names
    five = []
    for n in big_names:
        g = big[n]
        if n in col_names:
            five.append(g.reshape(g.shape[0], N_CHIPS, 2, g.shape[2] // 2, g.shape[3]))
        else:
            five.append(g.reshape(g.shape[0], N_CHIPS, 2, g.shape[1] // (2 * N_CHIPS), g.shape[2]))
    from_sibling = _exchange_core_halves(five)
    both = [_add_core_halves(f"grad_add_core_{n}", g, a, c_idx) for n, g, a in zip(big_names, five, from_sibling)]
    chip_sums = [b[0] for b in both]
    from_chips = _exchange_chip_shards([b[1] for b in both])
    jc_idx = jnp.concatenate([j_idx, c_idx])
    totals = [_add_chip_shards(f"grad_add_chips_{n}", p, b, jc_idx) for n, p, b in zip(big_names, chip_sums, from_chips)]
    big_g = dict(zip(big_names, _join_core_halves(totals)))

    weights = dict(mix_norm_g=mix_norm_g, ffn_norm_g=ffn_norm_g, conv_w_in=conv_w_in, conv_a_dw_w=conv_a_dw_w, conv_a_dw_b=conv_a_dw_b, conv_a_ln_g=conv_a_ln_g, conv_a_ln_b=conv_a_ln_b, conv_b_dw_w=conv_b_dw_w, conv_w_out=conv_w_out, attn_w_qkv=attn_w_qkv, attn_q_g=attn_q_g, attn_k_g=attn_k_g, attn_w_o=attn_w_o, ffn_w_up=ffn_w_up, ffn_dw_w=ffn_dw_w, ffn_dw_b=ffn_dw_b, ffn_w_down=ffn_w_down)
    m_in = dict(mix_norm_g=m_mix_norm_g, ffn_norm_g=m_ffn_norm_g, conv_w_in=m_conv_w_in, conv_a_dw_w=m_conv_a_dw_w, conv_a_dw_b=m_conv_a_dw_b, conv_a_ln_g=m_conv_a_ln_g, conv_a_ln_b=m_conv_a_ln_b, conv_b_dw_w=m_conv_b_dw_w, conv_w_out=m_conv_w_out, attn_w_qkv=m_attn_w_qkv, attn_q_g=m_attn_q_g, attn_k_g=m_attn_k_g, attn_w_o=m_attn_w_o, ffn_w_up=m_ffn_w_up, ffn_dw_w=m_ffn_dw_w, ffn_dw_b=m_ffn_dw_b, ffn_w_down=m_ffn_w_down)
    v_in = dict(mix_norm_g=v_mix_norm_g, ffn_norm_g=v_ffn_norm_g, conv_w_in=v_conv_w_in, conv_a_dw_w=v_conv_a_dw_w, conv_a_dw_b=v_conv_a_dw_b, conv_a_ln_g=v_conv_a_ln_g, conv_a_ln_b=v_conv_a_ln_b, conv_b_dw_w=v_conv_b_dw_w, conv_w_out=v_conv_w_out, attn_w_qkv=v_attn_w_qkv, attn_q_g=v_attn_q_g, attn_k_g=v_attn_k_g, attn_w_o=v_attn_w_o, ffn_w_up=v_ffn_w_up, ffn_dw_w=v_ffn_dw_w, ffn_dw_b=v_ffn_dw_b, ffn_w_down=v_ffn_w_down)
    order = list(weights)
    grads, delta, new_m, new_v = {}, {}, {}, {}
    for n in big_names:
        grads[n] = big_g[n]
        delta[n], new_m[n], new_v[n] = _adamw(f"adamw_{n}", weights[n], big_g[n], m_in[n], v_in[n])
    shapes = [weights[n].shape for n in small_names]
    packed = [_pack([d[n] for n in small_names]) for d in (weights, small_g, m_in, v_in)]
    upd = _adamw("adamw_small", *[p[None] for p in packed])
    for out, res in zip((delta, new_m, new_v), upd):
        out.update(zip(small_names, _unpack(res[0], shapes)))
    grads.update({n: small_g[n].reshape(weights[n].shape) for n in small_names})
    return (loss, grad_x, *[grads[n] for n in order], *[delta[n] for n in order], *[new_m[n] for n in order],
            *[new_v[n] for n in order])
```

```python
import jax
import jax.numpy as jnp
from jax import lax
from jax.experimental import pallas as pl
from jax.experimental.pallas import tpu as pltpu

F32 = jnp.float32
BF16 = jnp.bfloat16
EPS = 1e-6
CONV_A_WIDTH = 31
CONV_B_WIDTH = 3
FFN_CONV_WIDTH = 3
HEAD_DIM = 64
ADAM_LR = 0.001
ADAM_B1 = 0.9
ADAM_B2 = 0.999
ADAM_EPS = 1e-08
ADAM_WD = 0.01
ADAM_STEP = 10

LANES = 128
SUBLANES = 8
BF16_ROWS = 16
V7X_VMEM_BYTES = 64 * 1024 * 1024
VMEM_LIMIT_BYTES = V7X_VMEM_BYTES * 3 // 4
MM_VMEM_BUDGET = VMEM_LIMIT_BYTES * 4 // 5
MM_ROWS = 1024
N_CHIPS = 4
N_DEV = 8
HALO_A = 32
HALO_S = 8
ELT_ROWS = 64
ATTN_BLOCK = 128
ATTN_SUB = 2
ATTN_MORE = 2
EXP_UNDERFLOW = -104.0
MESH = pl.DeviceIdType.MESH
ANY = pl.BlockSpec(memory_space=pl.ANY)
NT = (((1,), (1,)), ((), ()))
NN = (((1,), (0,)), ((), ()))
TN = (((0,), (0,)), ((), ()))


def _pcall(body, **kw):
    return pl.pallas_call(body, **kw)


def _cp(*sem):
    return pltpu.CompilerParams(dimension_semantics=sem, vmem_limit_bytes=VMEM_LIMIT_BYTES)


def _sds(shape, dtype):
    return jax.ShapeDtypeStruct(tuple(shape), dtype)


def _tile(n, cap, align=LANES):
    if n <= cap:
        return n
    for t in range(cap - cap % align, 0, -align):
        if n % t == 0:
            return t
    return n


def _sig(x):
    return 0.5 * jnp.tanh(0.5 * x) + 0.5


def _rowsum(x):
    return jnp.sum(x, axis=0, keepdims=True)


def _fold(x):
    acc = x[0:SUBLANES]
    for r in range(SUBLANES, x.shape[0], SUBLANES):
        acc = acc + x[r:r + SUBLANES]
    return acc


def _mm_call(name, dn, operands, in_specs, out_shape, out_spec, grid, nk, acc_shape, has_res, has_alias):
    def body(*refs):
        a_ref, b_ref = refs[0], refs[1]
        pos = 2
        res_ref = refs[pos] if has_res else None
        pos += int(has_res) + int(has_alias)
        o_ref = refs[pos]
        acc_ref = refs[pos + 1] if nk > 1 else None
        p = lax.dot_general(a_ref[...].astype(BF16), b_ref[...].astype(BF16), dn, preferred_element_type=F32)

        def finish(v):
            if has_res:
                v = v + res_ref[...]
            o_ref[...] = v.astype(o_ref.dtype)

        if nk == 1:
            finish(p)
        else:
            k = pl.program_id(2)

            @pl.when(k == 0)
            def _():
                acc_ref[...] = p

            @pl.when(k > 0)
            def _():
                acc_ref[...] += p

            @pl.when(k == nk - 1)
            def _():
                finish(acc_ref[...])

    aliases = {len(operands) - 1: 0} if has_alias else {}
    return _pcall(
        body, grid=grid, in_specs=in_specs, out_specs=out_spec, out_shape=out_shape,
        scratch_shapes=[pltpu.VMEM(acc_shape, F32)] if nk > 1 else [],
        input_output_aliases=aliases, compiler_params=_cp("parallel", "parallel", "arbitrary"), name=name,
    )(*operands)


def _mm_fwd(name, a, w, l, *, colshard, res=None, out_split=1):
    M, K = a.shape
    tm = _tile(M, MM_ROWS, BF16_ROWS)
    if colshard:
        cs = w.shape[3]
        N, tn, tk = N_CHIPS * cs, cs, K
        b_spec = pl.BlockSpec((None, None, tk, tn), lambda j, i, k: (l, j, k, 0))
    else:
        N = w.shape[2]
        tn, tk = _tile(N, 1024), _tile(K, 1536)
        b_spec = pl.BlockSpec((None, tk, tn), lambda j, i, k: (l, k, j))
    nk = K // tk
    in_specs = [pl.BlockSpec((tm, tk), lambda j, i, k: (i, k)), b_spec]
    operands = [a, w]
    if res is not None:
        in_specs.append(pl.BlockSpec((tm, tn), lambda j, i, k: (i, j)))
        operands.append(res)
    if out_split == 1:
        out_shape = _sds((M, N), F32)
        out_spec = pl.BlockSpec((tm, tn), lambda j, i, k: (i, j))
    else:
        per = N // tn // out_split
        out_shape = _sds((out_split, M, N // out_split), F32)
        out_spec = pl.BlockSpec((None, tm, tn), lambda j, i, k: (j // per, i, j % per))
    return _mm_call(name, NN, operands, in_specs, out_shape, out_spec, (N // tn, M // tm, nk), nk, (tm, tn),
                    res is not None, False)


def _mm_dgrad(name, g, w, l, *, colshard):
    split = g.ndim == 3
    M = g.shape[-2]
    tm = _tile(M, MM_ROWS, BF16_ROWS)
    if colshard:
        kw, cs = w.shape[2], w.shape[3]
        tn, tk, nk = _tile(kw, 1408), cs, N_CHIPS
        b_spec = pl.BlockSpec((None, None, tn, tk), lambda j, i, k: (l, k, j, 0))
    else:
        kw, ncon = w.shape[1], w.shape[2]
        tn, tk = _tile(kw, 1408), _tile(ncon, 1536)
        nk = ncon // tk
        b_spec = pl.BlockSpec((None, tn, tk), lambda j, i, k: (l, j, k))
    if split:
        per = nk // g.shape[0]
        a_spec = pl.BlockSpec((None, tm, tk), lambda j, i, k: (k // per, i, k % per))
    else:
        a_spec = pl.BlockSpec((tm, tk), lambda j, i, k: (i, k))
    out_shape = _sds((M, kw), F32)
    out_spec = pl.BlockSpec((tm, tn), lambda j, i, k: (i, j))
    return _mm_call(name, NT, [g, w], [a_spec, b_spec], out_shape, out_spec, (kw // tn, M // tm, nk), nk, (tm, tn),
                    False, False)


def _mm_wgrad(name, a, g, l, n_layers, buf, *, colshard):
    S, M = a.shape
    split = g.ndim == 3
    N = g.shape[-1] * (g.shape[0] if split else 1)
    tm = _tile(M, 1408)
    tn = N // N_CHIPS if colshard else _tile(N, 1024)
    per_row = 2 * (tm * a.dtype.itemsize + tn * g.dtype.itemsize)
    tk = _tile(S, max(BF16_ROWS, min(2048, (MM_VMEM_BUDGET - 3 * tm * tn * 4) // per_row)), BF16_ROWS)
    nk = S // tk
    if colshard:
        out_shape = _sds((n_layers, N_CHIPS, M, tn), F32)
        out_spec = pl.BlockSpec((None, None, tm, tn), lambda j, i, k: (l, j, i, 0))
    else:
        out_shape = _sds((n_layers, M, N), F32)
        out_spec = pl.BlockSpec((None, tm, tn), lambda j, i, k: (l, i, j))
    if split:
        per = N // tn // g.shape[0]
        b_spec = pl.BlockSpec((None, tk, tn), lambda j, i, k: (j // per, k, j % per))
    else:
        b_spec = pl.BlockSpec((tk, tn), lambda j, i, k: (k, j))
    in_specs = [pl.BlockSpec((tk, tm), lambda j, i, k: (k, i)), b_spec]
    operands = [a, g]
    if buf is not None:
        in_specs.append(ANY)
        operands.append(buf)
    return _mm_call(name, TN, operands, in_specs, out_shape, out_spec, (N // tn, M // tm, nk), nk, (tm, tn),
                    False, buf is not None)


def _rms_fwd(name, x, g, l):
    S, D = x.shape
    tm = _tile(S, 512, BF16_ROWS)

    def body(x_ref, g_ref, o_ref):
        xf = x_ref[...]
        r = lax.rsqrt(jnp.mean(xf * xf, axis=-1, keepdims=True) + EPS)
        o_ref[...] = (xf * r * g_ref[l:l + 1, :]).astype(BF16)

    return _pcall(
        body, grid=(S // tm,),
        in_specs=[pl.BlockSpec((tm, D), lambda i: (i, 0)), pl.BlockSpec(g.shape, lambda i: (0, 0))],
        out_specs=pl.BlockSpec((tm, D), lambda i: (i, 0)), out_shape=_sds((S, D), BF16),
        compiler_params=_cp("parallel"), name=name,
    )(x, g)


def _rms_bwd(name, x, g, l, dh, dres):
    S, D = x.shape
    tm = _tile(S, 512, SUBLANES)

    def body(x_ref, g_ref, dh_ref, dr_ref, dx_ref, dg_ref):
        xf = x_ref[...]
        r = lax.rsqrt(jnp.mean(xf * xf, axis=-1, keepdims=True) + EPS)
        xh = xf * r
        d = dh_ref[...]
        dxh = d * g_ref[l:l + 1, :]
        dx_ref[...] = dr_ref[...] + r * (dxh - xh * jnp.mean(dxh * xh, axis=-1, keepdims=True))

        @pl.when(pl.program_id(0) == 0)
        def _():
            dg_ref[...] = jnp.zeros_like(dg_ref)

        dg_ref[...] += _rowsum(d * xh)

    row = pl.BlockSpec((tm, D), lambda i: (i, 0))
    return _pcall(
        body, grid=(S // tm,),
        in_specs=[row, pl.BlockSpec(g.shape, lambda i: (0, 0)), row, row],
        out_specs=[row, pl.BlockSpec((1, D), lambda i: (0, 0))],
        out_shape=[_sds((S, D), F32), _sds((1, D), F32)],
        compiler_params=_cp("arbitrary"), name=name,
    )(x, g, dh, dres)


def _loss_fwd_bwd(name, y, t):
    S, D = y.shape
    tm = _tile(S, 512, SUBLANES)

    def body(y_ref, t_ref, dy_ref, l_ref):
        e = y_ref[...] - t_ref[...]
        dy_ref[...] = e * (1.0 / D)

        @pl.when(pl.program_id(0) == 0)
        def _():
            l_ref[...] = jnp.zeros_like(l_ref)

        l_ref[...] += 0.5 * jnp.sum(jnp.sum(e * e, axis=-1, keepdims=True) * (1.0 / D), axis=0, keepdims=True)

    row = pl.BlockSpec((tm, D), lambda i: (i, 0))
    return _pcall(
        body, grid=(S // tm,), in_specs=[row, row],
        out_specs=[row, pl.BlockSpec((SUBLANES, LANES), lambda i: (0, 0))],
        out_shape=[_sds((S, D), F32), _sds((SUBLANES, LANES), F32)],
        compiler_params=_cp("arbitrary"), name=name,
    )(y, t)


def _convmix_fwd(name, p, aw, ab, lg, lb, bw, l):
    S, W = p.shape
    dg = W // 5
    tm = _tile(S, 256, HALO_A)
    nb = tm // HALO_A
    ka, kb = CONV_A_WIDTH, CONV_B_WIDTH

    def body(p_ref, ph_ref, aw_ref, ab_ref, lg_ref, lb_ref, bw_ref, o_ref, uext, mext):
        first = pl.program_id(0) == 0
        ph = ph_ref[...]
        pc = p_ref[...]
        uext[pl.ds(0, HALO_A), :] = jnp.where(first, 0.0, ph[:, 0:dg] * _sig(ph[:, dg:2 * dg]))
        uext[pl.ds(HALO_A, tm), :] = pc[:, 0:dg] * _sig(pc[:, dg:2 * dg])
        mext[pl.ds(0, HALO_A), :] = jnp.where(first, 0.0, ph[:, 3 * dg:4 * dg] * ph[:, 4 * dg:5 * dg])
        mext[pl.ds(HALO_A, tm), :] = pc[:, 3 * dg:4 * dg] * pc[:, 4 * dg:5 * dg]
        acc = jnp.zeros((tm, dg), F32) + ab_ref[l:l + 1, :]
        for k in range(ka):
            acc = acc + aw_ref[l, k:k + 1, :] * uext[pl.ds(HALO_A - (ka - 1) + k, tm), :]
        mu = jnp.mean(acc, axis=-1, keepdims=True)
        xc = acc - mu
        ln = xc * lax.rsqrt(jnp.mean(xc * xc, axis=-1, keepdims=True) + EPS) * lg_ref[l:l + 1, :] + lb_ref[l:l + 1, :]
        o_ref[:, 0:dg] = (ln * _sig(ln)).astype(BF16)
        cb = jnp.zeros((tm, dg), F32)
        for k in range(kb):
            cb = cb + bw_ref[l, k:k + 1, :] * mext[pl.ds(HALO_A - (kb - 1) + k, tm), :]
        o_ref[:, dg:2 * dg] = (pc[:, 2 * dg:3 * dg] * cb).astype(BF16)

    full = lambda a: pl.BlockSpec(a.shape, lambda i: (0,) * a.ndim)
    return _pcall(
        body, grid=(S // tm,),
        in_specs=[pl.BlockSpec((tm, W), lambda i: (i, 0)),
                  pl.BlockSpec((HALO_A, W), lambda i: (jnp.maximum(i * nb - 1, 0), 0)),
                  full(aw), full(ab), full(lg), full(lb), full(bw)],
        out_specs=pl.BlockSpec((tm, 2 * dg), lambda i: (i, 0)), out_shape=_sds((S, 2 * dg), BF16),
        scratch_shapes=[pltpu.VMEM((HALO_A + tm, dg), F32), pltpu.VMEM((HALO_A + tm, dg), F32)],
        compiler_params=_cp("parallel"), name=name,
    )(p, p, aw, ab, lg, lb, bw)


def _convmix_bwd(name, p, dab, aw, ab, lg, lb, bw, l):
    S, W = p.shape
    dg = W // 5
    tm = _tile(S, 256, HALO_A)
    nb = tm // HALO_A
    n_i = S // tm
    ka, kb = CONV_A_WIDTH, CONV_B_WIDTH
    n = tm + HALO_A
    ext = HALO_A + n

    def body(p_ref, pp_ref, pn_ref, d_ref, dn_ref, aw_ref, ab_ref, lg_ref, lb_ref, bw_ref,
             dp_ref, daw_ref, dab_ref, dlg_ref, dlb_ref, dbw_ref, uext, mext, gext, dcext, dbext):
        i = pl.program_id(0)
        first, last = i == 0, i == n_i - 1

        @pl.when(first)
        def _():
            for r in (daw_ref, dab_ref, dlg_ref, dlb_ref, dbw_ref):
                r[...] = jnp.zeros_like(r)

        pp, pc, pn = pp_ref[...], p_ref[...], pn_ref[...]
        glu = lambda b: b[:, 0:dg] * _sig(b[:, dg:2 * dg])
        gch = lambda b: b[:, 3 * dg:4 * dg] * b[:, 4 * dg:5 * dg]
        uext[pl.ds(0, HALO_A), :] = jnp.where(first, 0.0, glu(pp))
        uext[pl.ds(HALO_A, tm), :] = glu(pc)
        uext[pl.ds(HALO_A + tm, HALO_A), :] = glu(pn)
        mext[pl.ds(0, HALO_A), :] = jnp.where(first, 0.0, gch(pp))
        mext[pl.ds(HALO_A, tm), :] = gch(pc)
        mext[pl.ds(HALO_A + tm, HALO_A), :] = gch(pn)

        c = jnp.zeros((n, dg), F32) + ab_ref[l:l + 1, :]
        for k in range(ka):
            c = c + aw_ref[l, k:k + 1, :] * uext[pl.ds(HALO_A - (ka - 1) + k, n), :]
        xc = c - jnp.mean(c, axis=-1, keepdims=True)
        rstd = lax.rsqrt(jnp.mean(xc * xc, axis=-1, keepdims=True) + EPS)
        chat = xc * rstd
        g_ln = lg_ref[l:l + 1, :]
        ln = chat * g_ln + lb_ref[l:l + 1, :]
        s = _sig(ln)
        gext[pl.ds(0, tm), :] = d_ref[:, 0:dg]
        gext[pl.ds(tm, HALO_A), :] = jnp.where(last, 0.0, dn_ref[:, 0:dg])
        dln = gext[...] * (s * (1.0 + ln * (1.0 - s)))
        dlnh = dln * g_ln
        dc = rstd * (dlnh - jnp.mean(dlnh, axis=-1, keepdims=True) - chat * jnp.mean(dlnh * chat, axis=-1, keepdims=True))
        dcext[...] = dc
        dlg_ref[...] += _rowsum((dln * chat)[0:tm])
        dlb_ref[...] += _rowsum(dln[0:tm])
        dab_ref[...] += _rowsum(dc[0:tm])
        du = jnp.zeros((tm, dg), F32)
        for k in range(ka):
            du = du + aw_ref[l, k:k + 1, :] * dcext[pl.ds(ka - 1 - k, tm), :]
            daw_ref[k:k + 1, :] += _rowsum(dcext[pl.ds(0, tm), :] * uext[pl.ds(HALO_A - (ka - 1) + k, tm), :])
        sg = _sig(pc[:, dg:2 * dg])
        dp_ref[:, 0:dg] = (du * sg).astype(BF16)
        dp_ref[:, dg:2 * dg] = (du * pc[:, 0:dg] * sg * (1.0 - sg)).astype(BF16)

        cb = jnp.zeros((tm, dg), F32)
        for k in range(kb):
            cb = cb + bw_ref[l, k:k + 1, :] * mext[pl.ds(HALO_A - (kb - 1) + k, tm), :]
        db = d_ref[:, dg:2 * dg]
        dp_ref[:, 2 * dg:3 * dg] = (db * cb).astype(BF16)
        dbext[pl.ds(0, tm), :] = db * pc[:, 2 * dg:3 * dg]
        dbext[pl.ds(tm, HALO_A), :] = jnp.where(last, 0.0, dn_ref[:, dg:2 * dg] * pn[:, 2 * dg:3 * dg])
        dm = jnp.zeros((tm, dg), F32)
        for k in range(kb):
            dm = dm + bw_ref[l, k:k + 1, :] * dbext[pl.ds(kb - 1 - k, tm), :]
            dbw_ref[k:k + 1, :] += _rowsum(dbext[pl.ds(0, tm), :] * mext[pl.ds(HALO_A - (kb - 1) + k, tm), :])
        dp_ref[:, 3 * dg:4 * dg] = (dm * pc[:, 4 * dg:5 * dg]).astype(BF16)
        dp_ref[:, 4 * dg:5 * dg] = (dm * pc[:, 3 * dg:4 * dg]).astype(BF16)

    full = lambda a: pl.BlockSpec(a.shape, lambda i: (0,) * a.ndim)
    prev = lambda i: (jnp.maximum(i * nb - 1, 0), 0)
    nxt = lambda i: (jnp.minimum((i + 1) * nb, S // HALO_A - 1), 0)
    acc = lambda r: pl.BlockSpec((r, dg), lambda i: (0, 0))
    return _pcall(
        body, grid=(n_i,),
        in_specs=[pl.BlockSpec((tm, W), lambda i: (i, 0)), pl.BlockSpec((HALO_A, W), prev), pl.BlockSpec((HALO_A, W), nxt),
                  pl.BlockSpec((tm, 2 * dg), lambda i: (i, 0)), pl.BlockSpec((HALO_A, 2 * dg), nxt),
                  full(aw), full(ab), full(lg), full(lb), full(bw)],
        out_specs=[pl.BlockSpec((tm, W), lambda i: (i, 0)), acc(ka), acc(1), acc(1), acc(1), acc(kb)],
        out_shape=[_sds((S, W), BF16), _sds((ka, dg), F32), _sds((1, dg), F32), _sds((1, dg), F32), _sds((1, dg), F32),
                   _sds((kb, dg), F32)],
        scratch_shapes=[pltpu.VMEM((ext, dg), F32), pltpu.VMEM((ext, dg), F32), pltpu.VMEM((n, dg), F32),
                        pltpu.VMEM((n, dg), F32), pltpu.VMEM((n, dg), F32)],
        compiler_params=_cp("arbitrary"), name=name,
    )(p, p, p, dab, dab, aw, ab, lg, lb, bw)


def _ffn_mid_fwd(name, u2, dww, dwb, l):
    _, S, F = u2.shape
    tm = _tile(S, 256, BF16_ROWS)
    tc = _tile(F, 1408)
    n_f = F // tc
    nb = tm // HALO_S
    kf = FFN_CONV_WIDTH

    def body(u_ref, uh_ref, wg_ref, wv_ref, bg_ref, bv_ref, o_ref, ext):
        first = pl.program_id(1) == 0
        ext[:, pl.ds(0, HALO_S), :] = jnp.where(first, 0.0, uh_ref[...])
        ext[:, pl.ds(HALO_S, tm), :] = u_ref[...]
        rc = _tile(tm, ELT_ROWS, BF16_ROWS)

        def lane_chunk(ci, carry):
            lanes = pl.ds(pl.multiple_of(ci * LANES, LANES), LANES)
            taps = [[w_ref[k:k + 1, lanes] for k in range(kf)] for w_ref in (wg_ref, wv_ref)]
            bias = [b_ref[l:l + 1, lanes] for b_ref in (bg_ref, bv_ref)]
            for r0 in range(0, tm, rc):
                c = []
                for g in range(2):
                    acc = bias[g]
                    for k in range(kf):
                        acc = acc + taps[g][k] * ext[g, pl.ds(HALO_S - (kf - 1) + k + r0, rc), lanes]
                    c.append(acc)
                o_ref[pl.ds(r0, rc), lanes] = (c[0] * _sig(c[0]) * c[1]).astype(BF16)
            return carry

        lax.fori_loop(0, tc // LANES, lane_chunk, 0)

    n_l = dwb.shape[0]
    return _pcall(
        body, grid=(n_f, S // tm),
        in_specs=[pl.BlockSpec((2, tm, tc), lambda j, i: (0, i, j)),
                  pl.BlockSpec((2, HALO_S, tc), lambda j, i: (0, jnp.maximum(i * nb - 1, 0), j)),
                  pl.BlockSpec((None, kf, tc), lambda j, i: (l, 0, j)),
                  pl.BlockSpec((None, kf, tc), lambda j, i: (l, 0, j + n_f)),
                  pl.BlockSpec((n_l, tc), lambda j, i: (0, j)),
                  pl.BlockSpec((n_l, tc), lambda j, i: (0, j + n_f))],
        out_specs=pl.BlockSpec((tm, tc), lambda j, i: (i, j)), out_shape=_sds((S, F), BF16),
        scratch_shapes=[pltpu.VMEM((2, HALO_S + tm, tc), F32)],
        compiler_params=_cp("parallel", "parallel"), name=name,
    )(u2, u2, dww, dww, dwb, dwb)


def _ffn_mid_bwd(name, u2, df, dww, dwb, l):
    _, S, F = u2.shape
    tm = _tile(S, 256, BF16_ROWS)
    tc = _tile(F, 1408)
    n_f = F // tc
    nb = tm // HALO_S
    n_i = S // tm
    kf = FFN_CONV_WIDTH
    n = tm + HALO_S

    def body(u_ref, up_ref, un_ref, df_ref, dfn_ref, wg_ref, wv_ref, bg_ref, bv_ref,
             du_ref, dw_ref, db_ref, uext, dcext):
        i = pl.program_id(1)
        first, last = i == 0, i == n_i - 1

        @pl.when(first)
        def _():
            dw_ref[...] = jnp.zeros_like(dw_ref)
            db_ref[...] = jnp.zeros_like(db_ref)

        uext[:, pl.ds(0, HALO_S), :] = jnp.where(first, 0.0, up_ref[...])
        uext[:, pl.ds(HALO_S, tm), :] = u_ref[...]
        uext[:, pl.ds(HALO_S + tm, HALO_S), :] = un_ref[...]
        rc = _tile(tm, ELT_ROWS, BF16_ROWS)

        def lane_chunk(ci, carry):
            lanes = pl.ds(pl.multiple_of(ci * LANES, LANES), LANES)
            taps = [[w_ref[k:k + 1, lanes] for k in range(kf)] for w_ref in (wg_ref, wv_ref)]
            bias = [b_ref[l:l + 1, lanes] for b_ref in (bg_ref, bv_ref)]
            acc_w = [[jnp.zeros((SUBLANES, LANES), F32) for _ in range(kf)] for _ in range(2)]
            acc_b = [jnp.zeros((SUBLANES, LANES), F32) for _ in range(2)]
            for r0, rows in [(r, rc) for r in range(0, tm, rc)] + [(tm, HALO_S)]:
                shifted = [[uext[g, pl.ds(HALO_S - (kf - 1) + k + r0, rows), lanes] for k in range(kf)] for g in range(2)]
                conv = []
                for g in range(2):
                    acc = bias[g]
                    for k in range(kf):
                        acc = acc + taps[g][k] * shifted[g][k]
                    conv.append(acc)
                cg, cv = conv
                s = _sig(cg)
                dfe = df_ref[pl.ds(r0, rows), lanes] if r0 < tm else jnp.where(last, 0.0, dfn_ref[:, lanes])
                dc = [dfe * cv * (s * (1.0 + cg * (1.0 - s))), dfe * (cg * s)]
                for g in range(2):
                    dcext[g, pl.ds(r0, rows), lanes] = dc[g]
                    if r0 < tm:
                        acc_b[g] = acc_b[g] + _fold(dc[g])
                        for k in range(kf):
                            acc_w[g][k] = acc_w[g][k] + _fold(dc[g] * shifted[g][k])
            for r0 in range(0, tm, rc):
                for g in range(2):
                    du = taps[g][0] * dcext[g, pl.ds(r0 + kf - 1, rc), lanes]
                    for k in range(1, kf):
                        du = du + taps[g][k] * dcext[g, pl.ds(r0 + kf - 1 - k, rc), lanes]
                    du_ref[g, pl.ds(r0, rc), lanes] = du.astype(BF16)
            for g in range(2):
                db_ref[g, :, lanes] += _rowsum(acc_b[g])
                for k in range(kf):
                    dw_ref[g, k:k + 1, lanes] += _rowsum(acc_w[g][k])
            return carry

        lax.fori_loop(0, tc // LANES, lane_chunk, 0)

    n_l = dwb.shape[0]
    prev = lambda j, i: (0, jnp.maximum(i * nb - 1, 0), j)
    nxt = lambda j, i: (0, jnp.minimum((i + 1) * nb, S // HALO_S - 1), j)
    return _pcall(
        body, grid=(n_f, n_i),
        in_specs=[pl.BlockSpec((2, tm, tc), lambda j, i: (0, i, j)),
                  pl.BlockSpec((2, HALO_S, tc), prev), pl.BlockSpec((2, HALO_S, tc), nxt),
                  pl.BlockSpec((tm, tc), lambda j, i: (i, j)),
                  pl.BlockSpec((HALO_S, tc), lambda j, i: nxt(j, i)[1:]),
                  pl.BlockSpec((None, kf, tc), lambda j, i: (l, 0, j)),
                  pl.BlockSpec((None, kf, tc), lambda j, i: (l, 0, j + n_f)),
                  pl.BlockSpec((n_l, tc), lambda j, i: (0, j)),
                  pl.BlockSpec((n_l, tc), lambda j, i: (0, j + n_f))],
        out_specs=[pl.BlockSpec((2, tm, tc), lambda j, i: (0, i, j)),
                   pl.BlockSpec((2, kf, tc), lambda j, i: (0, 0, j)),
                   pl.BlockSpec((2, 1, tc), lambda j, i: (0, 0, j))],
        out_shape=[_sds((2, S, F), BF16), _sds((2, kf, F), F32), _sds((2, 1, F), F32)],
        scratch_shapes=[pltpu.VMEM((2, HALO_S + n, tc), F32), pltpu.VMEM((2, n, tc), F32)],
        compiler_params=_cp("parallel", "arbitrary"), name=name,
    )(u2, u2, u2, df, df, dww, dww, dwb, dwb)


def _head_sum_matrix():
    r = lax.broadcasted_iota(jnp.int32, (LANES, LANES), 0) // HEAD_DIM
    c = lax.broadcasted_iota(jnp.int32, (LANES, LANES), 1) // HEAD_DIM
    return (r == c).astype(BF16)


def _head_mean(x, ones):
    return _split_dot(x, ones) * (1.0 / HEAD_DIM)


def _qknorm_fwd(name, qkv, g2):
    S, D3 = qkv.shape
    D = D3 // 3
    tm = _tile(S, 256, BF16_ROWS)
    scale = HEAD_DIM ** -0.5

    def body(q_ref, k_ref, v_ref, g_ref, qo_ref, ko_ref, vo_ref):
        ones = _head_sum_matrix()
        for cc in range(D // LANES):
            sl = slice(cc * LANES, (cc + 1) * LANES)
            for x_ref, o_ref, row, mult in ((q_ref, qo_ref, 0, scale), (k_ref, ko_ref, 1, 1.0)):
                x = x_ref[:, sl]
                r = lax.rsqrt(_head_mean(x * x, ones) + EPS)
                o_ref[:, sl] = ((x * r * g_ref[row:row + 1, :]).astype(BF16) * mult).astype(BF16)
        vo_ref[...] = v_ref[...].astype(BF16)

    col = lambda c: pl.BlockSpec((tm, D), lambda i: (i, c))
    out = pl.BlockSpec((tm, D), lambda i: (i, 0))
    return _pcall(
        body, grid=(S // tm,),
        in_specs=[col(0), col(1), col(2), pl.BlockSpec(g2.shape, lambda i: (0, 0))],
        out_specs=[out, out, out], out_shape=[_sds((S, D), BF16)] * 3,
        compiler_params=_cp("parallel"), name=name,
    )(qkv, qkv, qkv, g2)


def _qknorm_bwd(name, qkv, dq, dk, dv, g2):
    S, D3 = qkv.shape
    D = D3 // 3
    tm = _tile(S, 256, BF16_ROWS)
    scale = HEAD_DIM ** -0.5

    def body(q_ref, k_ref, dq_ref, dk_ref, dv_ref, g_ref, o_ref, dg_ref):
        @pl.when(pl.program_id(0) == 0)
        def _():
            dg_ref[...] = jnp.zeros_like(dg_ref)

        ones = _head_sum_matrix()
        for cc in range(D // LANES):
            sl = slice(cc * LANES, (cc + 1) * LANES)
            for x_ref, d_ref, row, mult, base in ((q_ref, dq_ref, 0, scale, 0), (k_ref, dk_ref, 1, 1.0, D)):
                x = x_ref[:, sl]
                r = lax.rsqrt(_head_mean(x * x, ones) + EPS)
                xh = x * r
                dn = d_ref[:, sl] * mult
                dxh = dn * g_ref[row:row + 1, :]
                dx = r * (dxh - xh * _head_mean(dxh * xh, ones))
                o_ref[:, base + cc * LANES:base + (cc + 1) * LANES] = dx.astype(BF16)
                dg_ref[row:row + 1, :] += _rowsum(dn * xh)
        o_ref[:, 2 * D:3 * D] = dv_ref[...].astype(BF16)

    col = lambda c: pl.BlockSpec((tm, D), lambda i: (i, c))
    row = pl.BlockSpec((tm, D), lambda i: (i, 0))
    return _pcall(
        body, grid=(S // tm,),
        in_specs=[col(0), col(1), row, row, row, pl.BlockSpec(g2.shape, lambda i: (0, 0))],
        out_specs=[pl.BlockSpec((tm, D3), lambda i: (i, 0)), pl.BlockSpec((2, LANES), lambda i: (0, 0))],
        out_shape=[_sds((S, D3), BF16), _sds((2, LANES), F32)],
        compiler_params=_cp("arbitrary"), name=name,
    )(qkv, qkv, dq, dk, dv, g2)


def _attn_consts():
    t = ATTN_BLOCK
    row = lax.broadcasted_iota(jnp.int32, (t, t), 0)
    col = lax.broadcasted_iota(jnp.int32, (t, t), 1)
    lane = lax.broadcasted_iota(jnp.int32, (1, LANES), 1)
    heads = (lane < HEAD_DIM, lane >= HEAD_DIM)
    return row, col, heads


def _split_dot(x, m):
    n = x.shape[0]
    hi = x.astype(BF16)
    lo = (x - hi.astype(F32)).astype(BF16)
    both = jnp.dot(jnp.concatenate([hi, lo], axis=0), m, preferred_element_type=F32)
    return both[:n] + both[n:]


def _log_keep(z):
    return -(jnp.maximum(z, 0.0) + jnp.log(1.0 + jnp.exp(-jnp.abs(z))))


def _stack_heads(a, heads):
    t = ATTN_BLOCK
    zero = jnp.zeros((t, LANES), a.dtype)
    return jnp.concatenate([jnp.where(h, a[s * t:(s + 1) * t], zero) for s in range(a.shape[0] // t) for h in heads], axis=0)


def _side_by_side(a):
    t = ATTN_BLOCK
    return jnp.concatenate([jnp.concatenate([a[2 * s * t:(2 * s + 1) * t], a[(2 * s + 1) * t:(2 * s + 2) * t]], axis=1)
                            for s in range(a.shape[0] // (2 * t))], axis=0)


def _grow(a, rows, cols):
    z = jnp.zeros((rows, cols), F32)
    return z if a is None else jnp.concatenate([z, a], axis=0)


def _attn_fwd(name, qs, kn, vb):
    S, D = qs.shape
    t = ATTN_BLOCK
    tq = ATTN_SUB * t

    def body(q_ref, k_ref, v_ref, o_ref):
        i = pl.program_id(1)
        row, col, heads = _attn_consts()
        after_m = (row > col).astype(BF16)
        causal = col < row
        q_all = _stack_heads(q_ref[...], heads)

        def block(j, q, r, acc, mask):
            off = pl.multiple_of(j * t, t)
            kb = k_ref[pl.ds(off, t), :]
            v2 = _stack_heads(v_ref[pl.ds(off, t), :], heads)
            z = lax.dot_general(q, kb, NT, preferred_element_type=F32)
            lk = _log_keep(z)
            if mask is not None:
                lk = jnp.where(mask, lk, 0.0)
            w = jnp.exp(z + lk + _split_dot(lk, after_m) + r)
            if mask is not None:
                w = jnp.where(mask, w, 0.0)
            acc = acc + jnp.dot(_side_by_side(w.astype(BF16)), v2, preferred_element_type=F32)
            return r + jnp.sum(lk, axis=1, keepdims=True), acc

        def head(n_more):
            r = acc = None
            for s in reversed(range(ATTN_SUB)):
                mask = jnp.concatenate([causal, causal] + [jnp.ones_like(causal)] * (2 * (ATTN_SUB - 1 - s)), axis=0)
                r, acc = block(ATTN_SUB * i + s, q_all[2 * s * t:], _grow(r, 2 * t, 1), _grow(acc, t, LANES), mask)
            for b in range(n_more):
                r, acc = block(ATTN_SUB * i - 1 - b, q_all, r, acc, None)
            return r, acc

        r, acc = lax.cond(ATTN_SUB * i >= ATTN_MORE, lambda: head(ATTN_MORE), lambda: head(0))

        def cond(c):
            return jnp.logical_and(c[0] >= 0, jnp.max(c[1]) > EXP_UNDERFLOW)

        def step(c):
            r, a = block(c[0], q_all, c[1], c[2], None)
            return c[0] - 1, r, a

        first = jnp.where(ATTN_SUB * i >= ATTN_MORE, ATTN_SUB * i - 1 - ATTN_MORE, ATTN_SUB * i - 1)
        o_ref[...] = lax.while_loop(cond, step, (first, r, acc))[2]

    n_hp = D // LANES
    blk = pl.BlockSpec((tq, LANES), lambda hp, i: (i, hp))
    seq = pl.BlockSpec((S, LANES), lambda hp, i: (0, hp))
    return _pcall(
        body, grid=(n_hp, S // tq), in_specs=[blk, seq, seq], out_specs=blk, out_shape=_sds((S, D), F32),
        compiler_params=_cp("parallel", "arbitrary"), name=name,
    )(qs, kn, vb)


def _attn_bwd(name, qs, kn, vb, o, do):
    S, D = qs.shape
    t = ATTN_BLOCK
    tq = ATTN_SUB * t

    def body(q_ref, k_ref, v_ref, o_ref, do_ref, dq_ref, dk_ref, dv_ref):
        i = pl.program_id(1)

        @pl.when(i == 0)
        def _():
            dk_ref[...] = jnp.zeros_like(dk_ref)
            dv_ref[...] = jnp.zeros_like(dv_ref)

        row, col, heads = _attn_consts()
        after_m = (row > col).astype(BF16)
        from_m = (row >= col).astype(BF16)
        causal = col < row
        q_all = _stack_heads(q_ref[...], heads)
        dob = do_ref[...].astype(BF16)
        do_all = _stack_heads(dob, heads)
        dsum_all = jnp.sum(_stack_heads(dob.astype(F32) * o_ref[...], heads), axis=1, keepdims=True)

        def block(j, q, dor, dsum, r, es, dq, mask):
            off = pl.multiple_of(j * t, t)
            kb = k_ref[pl.ds(off, t), :]
            vblk = v_ref[pl.ds(off, t), :]
            z = lax.dot_general(q, kb, NT, preferred_element_type=F32)
            lk = _log_keep(z)
            if mask is not None:
                lk = jnp.where(mask, lk, 0.0)
            ls = z + lk
            w = jnp.exp(ls + _split_dot(lk, after_m) + r)
            if mask is not None:
                w = jnp.where(mask, w, 0.0)
            e = w * lax.dot_general(dor, vblk, NT, preferred_element_type=F32)
            before = dsum - (es + _split_dot(e, from_m))
            dz = e - (e + before) * jnp.exp(ls)
            if mask is not None:
                dz = jnp.where(mask, dz, 0.0)
            dzb = dz.astype(BF16)
            dq = dq + jnp.dot(_side_by_side(dzb), _stack_heads(kb, heads), preferred_element_type=F32)
            dk_ref[pl.ds(off, t), :] += lax.dot_general(dzb, q, TN, preferred_element_type=F32)
            dv_ref[pl.ds(off, t), :] += lax.dot_general(w.astype(BF16), dor, TN, preferred_element_type=F32)
            return r + jnp.sum(lk, axis=1, keepdims=True), es + jnp.sum(e, axis=1, keepdims=True), dq

        def head(n_more):
            r = es = dq = None
            for s in reversed(range(ATTN_SUB)):
                mask = jnp.concatenate([causal, causal] + [jnp.ones_like(causal)] * (2 * (ATTN_SUB - 1 - s)), axis=0)
                lo = 2 * s * t
                r, es, dq = block(ATTN_SUB * i + s, q_all[lo:], do_all[lo:], dsum_all[lo:], _grow(r, 2 * t, 1),
                                  _grow(es, 2 * t, 1), _grow(dq, t, LANES), mask)
            for b in range(n_more):
                r, es, dq = block(ATTN_SUB * i - 1 - b, q_all, do_all, dsum_all, r, es, dq, None)
            return r, es, dq

        r, es, dq = lax.cond(ATTN_SUB * i >= ATTN_MORE, lambda: head(ATTN_MORE), lambda: head(0))

        def cond(c):
            return jnp.logical_and(c[0] >= 0, jnp.max(c[1]) > EXP_UNDERFLOW)

        def step(c):
            r, es, a = block(c[0], q_all, do_all, dsum_all, c[1], c[2], c[3], None)
            return c[0] - 1, r, es, a

        first = jnp.where(ATTN_SUB * i >= ATTN_MORE, ATTN_SUB * i - 1 - ATTN_MORE, ATTN_SUB * i - 1)
        dq_ref[...] = lax.while_loop(cond, step, (first, r, es, dq))[3]

    n_hp = D // LANES
    blk = pl.BlockSpec((tq, LANES), lambda hp, i: (i, hp))
    seq = pl.BlockSpec((S, LANES), lambda hp, i: (0, hp))
    return _pcall(
        body, grid=(n_hp, S // tq), in_specs=[blk, seq, seq, blk, blk], out_specs=[blk, seq, seq],
        out_shape=[_sds((S, D), F32)] * 3, compiler_params=_cp("parallel", "arbitrary"), name=name,
    )(qs, kn, vb, o, do)


def _adamw(name, w, g, m, v):
    L, R, C = w.shape
    tr = _tile(R, 256, SUBLANES)
    c1 = 1.0 - ADAM_B1 ** ADAM_STEP
    c2 = 1.0 - ADAM_B2 ** ADAM_STEP

    def body(w_ref, g_ref, m_ref, v_ref, d_ref, mo_ref, vo_ref):
        gg = g_ref[...]
        mn = ADAM_B1 * m_ref[...] + (1.0 - ADAM_B1) * gg
        vn = ADAM_B2 * v_ref[...] + (1.0 - ADAM_B2) * (gg * gg)
        d_ref[...] = -ADAM_LR * ((mn / c1) / (jnp.sqrt(vn / c2) + ADAM_EPS) + ADAM_WD * w_ref[...])
        mo_ref[...] = mn
        vo_ref[...] = vn

    blk = pl.BlockSpec((None, tr, C), lambda l, i: (l, i, 0))
    return _pcall(
        body, grid=(L, R // tr), in_specs=[blk] * 4, out_specs=[blk] * 3, out_shape=[_sds(w.shape, F32)] * 3,
        compiler_params=_cp("parallel", "parallel"), name=name,
    )(w, g, m, v)


def _place():
    x, y, c = lax.axis_index("x"), lax.axis_index("y"), lax.axis_index("c")
    chips = [(1 - x, y), (x, 1 - y), (1 - x, 1 - y)]
    return x, y, c, chips


def _place_shard(name, w, j_idx):
    L, R, X = w.shape
    rh = R // 2
    tr = _tile(rh, 256, BF16_ROWS)

    def body(j_ref, w_ref, o_ref):
        o_ref[...] = w_ref[...].astype(BF16)

    return _pcall(
        body,
        grid_spec=pltpu.PrefetchScalarGridSpec(
            num_scalar_prefetch=1, grid=(L, 2, rh // tr),
            in_specs=[pl.BlockSpec((None, None, tr, X), lambda l, h, i, j_ref: (l, h, i, 0))],
            out_specs=pl.BlockSpec((None, None, None, tr, X), lambda l, h, i, j_ref: (l, j_ref[0], h, i, 0))),
        out_shape=_sds((L, N_CHIPS, 2, rh, X), BF16), compiler_params=_cp("parallel", "parallel", "parallel"), name=name,
    )(j_idx, w.reshape(L, 2, rh, X))


def _all_gather_weights(bufs, small_ws):
    n_big, n_small = len(bufs), len(small_ws)
    n_in = n_big + n_small

    def body(*refs):
        ins, outs = refs[:n_in], refs[n_in:2 * n_in]
        send_sems, recv_sems, local_sems = refs[2 * n_in:]
        x, y, c, chips = _place()
        j_me = 2 * x + y
        j_of = [2 * cx + cy for cx, cy in chips]
        sibling = (x, y, 1 - c)

        def remote(src, dst, s, to):
            return pltpu.make_async_remote_copy(src_ref=src, dst_ref=dst, send_sem=send_sems.at[s], recv_sem=recv_sems.at[s],
                                                device_id=to, device_id_type=MESH)

        started = []
        for t in range(n_big, n_in):
            loc = pltpu.make_async_copy(ins[t], outs[t].at[:, j_me], local_sems.at[t - n_big])
            loc.start()
            started.append(loc)
        first = []
        for t in range(n_big):
            mine = outs[t].at[:, j_me, c]
            for k in range(3):
                first.append(remote(mine, mine, 6 * t + k, (*chips[k], c)))
        for t in range(n_big, n_in):
            for k in range(3):
                first.append(remote(ins[t], outs[t].at[:, j_me], 6 * n_big + 3 * (t - n_big) + k, (*chips[k], c)))
        for cp in first:
            cp.start()
        passed = []
        for t in range(n_big):
            for k in range(3):
                landed = outs[t].at[:, j_of[k], c]
                remote(landed, landed, 6 * t + k, (*chips[k], c)).wait_recv()
                fwd = remote(landed, landed, 6 * t + 3 + k, sibling)
                fwd.start()
                passed.append(fwd)
        for t in range(n_big):
            for k in range(3):
                other = outs[t].at[:, j_of[k], 1 - c]
                remote(other, other, 6 * t + 3 + k, sibling).wait_recv()
        for t in range(n_big, n_in):
            for k in range(3):
                dst = outs[t].at[:, j_of[k]]
                remote(dst, dst, 6 * n_big + 3 * (t - n_big) + k, (*chips[k], c)).wait_recv()
        for cp in first + passed:
            cp.wait_send()
        for loc in started:
            loc.wait()

    out_shape = [_sds(b.shape, b.dtype) for b in bufs]
    out_shape += [_sds((w.shape[0], N_CHIPS) + w.shape[1:], w.dtype) for w in small_ws]
    n_sem = 6 * n_big + 3 * n_small
    outs = _pcall(
        body, in_specs=[ANY] * n_in, out_specs=[ANY] * n_in, out_shape=out_shape,
        input_output_aliases={t: t for t in range(n_big)},
        scratch_shapes=[pltpu.SemaphoreType.DMA((n_sem,)), pltpu.SemaphoreType.DMA((n_sem,)), pltpu.SemaphoreType.DMA((n_small,))],
        name="all_gather_weights",
    )(*bufs, *small_ws)
    return outs[:n_big], outs[n_big:]


def _exchange_core_halves(grads):
    n = len(grads)

    def body(*refs):
        ins, outs = refs[:n], refs[n:2 * n]
        send_sems, recv_sems = refs[2 * n:]
        x, y, c, _ = _place()
        cps = [pltpu.make_async_remote_copy(src_ref=ins[t].at[:, :, 1 - c], dst_ref=outs[t], send_sem=send_sems.at[t],
                                            recv_sem=recv_sems.at[t], device_id=(x, y, 1 - c), device_id_type=MESH)
               for t in range(n)]
        for cp in cps:
            cp.start()
        for cp in cps:
            cp.wait()

    return _pcall(
        body, in_specs=[ANY] * n, out_specs=[ANY] * n,
        out_shape=[_sds((g.shape[0], g.shape[1], g.shape[3], g.shape[4]), F32) for g in grads],
        scratch_shapes=[pltpu.SemaphoreType.DMA((n,)), pltpu.SemaphoreType.DMA((n,))],
        name="grad_exchange_core_halves",
    )(*grads)


def _add_core_halves(name, g, a, c_idx):
    L, nj, _, rh, X = g.shape
    tr = _tile(rh, 256, BF16_ROWS)

    def body(c_ref, g_ref, a_ref, o_ref, ob_ref):
        s = g_ref[...] + a_ref[...]
        o_ref[...] = s
        ob_ref[...] = s.astype(BF16)

    blk = pl.BlockSpec((None, None, tr, X), lambda l, j, i, c_ref: (l, j, i, 0))
    return _pcall(
        body,
        grid_spec=pltpu.PrefetchScalarGridSpec(
            num_scalar_prefetch=1, grid=(L, nj, rh // tr),
            in_specs=[pl.BlockSpec((None, None, None, tr, X), lambda l, j, i, c_ref: (l, j, c_ref[0], i, 0)), blk],
            out_specs=[blk, blk]),
        out_shape=[_sds((L, nj, rh, X), F32), _sds((L, nj, rh, X), BF16)],
        compiler_params=_cp("parallel", "parallel", "parallel"), name=name,
    )(c_idx, g, a)


def _exchange_chip_shards(parts):
    n = len(parts)

    def body(*refs):
        ins, outs = refs[:n], refs[n:2 * n]
        send_sems, recv_sems = refs[2 * n:]
        x, y, c, chips = _place()
        cps = []
        for t in range(n):
            for k, (cx, cy) in enumerate(chips):
                cps.append(pltpu.make_async_remote_copy(
                    src_ref=ins[t].at[:, 2 * cx + cy], dst_ref=outs[t].at[k], send_sem=send_sems.at[3 * t + k],
                    recv_sem=recv_sems.at[3 * t + k], device_id=(cx, cy, c), device_id_type=MESH))
        for cp in cps:
            cp.start()
        for cp in cps:
            cp.wait()

    return _pcall(
        body, in_specs=[ANY] * n, out_specs=[ANY] * n,
        out_shape=[_sds((3, p.shape[0], p.shape[2], p.shape[3]), p.dtype) for p in parts],
        scratch_shapes=[pltpu.SemaphoreType.DMA((3 * n,)), pltpu.SemaphoreType.DMA((3 * n,))],
        name="grad_exchange_chip_shards",
    )(*parts)


def _add_chip_shards(name, p, b, jc_idx):
    L, _, rh, X = p.shape
    tr = _tile(rh, 256, BF16_ROWS)

    def body(jc_ref, p_ref, b_ref, o_ref):
        o_ref[...] = ((p_ref[...] + b_ref[0].astype(F32)) + b_ref[1].astype(F32)) + b_ref[2].astype(F32)

    return _pcall(
        body,
        grid_spec=pltpu.PrefetchScalarGridSpec(
            num_scalar_prefetch=1, grid=(L, rh // tr),
            in_specs=[pl.BlockSpec((None, None, tr, X), lambda l, i, jc: (l, jc[0], i, 0)),
                      pl.BlockSpec((3, None, tr, X), lambda l, i, jc: (0, l, i, 0))],
            out_specs=pl.BlockSpec((None, None, tr, X), lambda l, i, jc: (l, jc[1], i, 0))),
        out_shape=_sds((L, 2, rh, X), F32), compiler_params=_cp("parallel", "parallel"), name=name,
    )(jc_idx, p, b)


def _join_core_halves(bufs):
    n = len(bufs)

    def body(*refs):
        outs = refs[n:2 * n]
        send_sems, recv_sems = refs[2 * n:]
        x, y, c, _ = _place()
        cps = [pltpu.make_async_remote_copy(src_ref=outs[t].at[:, c], dst_ref=outs[t].at[:, c], send_sem=send_sems.at[t],
                                            recv_sem=recv_sems.at[t], device_id=(x, y, 1 - c), device_id_type=MESH)
               for t in range(n)]
        for cp in cps:
            cp.start()
        for t in range(n):
            pltpu.make_async_remote_copy(src_ref=outs[t].at[:, c], dst_ref=outs[t].at[:, 1 - c], send_sem=send_sems.at[t],
                                         recv_sem=recv_sems.at[t], device_id=(x, y, 1 - c), device_id_type=MESH).wait()

    outs = _pcall(
        body, in_specs=[ANY] * n, out_specs=[ANY] * n, out_shape=[_sds(b.shape, F32) for b in bufs],
        input_output_aliases={t: t for t in range(n)},
        scratch_shapes=[pltpu.SemaphoreType.DMA((n,)), pltpu.SemaphoreType.DMA((n,))],
        name="grad_join_core_halves",
    )(*bufs)
    return [o.reshape(o.shape[0], 2 * o.shape[2], o.shape[3]) for o in outs]


def _all_reduce_small(packed):
    R, C = packed.shape

    def body(x_ref, o_ref, slots, send_sems, recv_sems):
        x, y, c, _ = _place()
        me = 4 * x + 2 * y + c
        slots[me] = x_ref[...]
        cps = []
        for d in range(N_DEV):
            to = (d // 4, (d // 2) % 2, d % 2)
            cp = pltpu.make_async_remote_copy(src_ref=x_ref, dst_ref=slots.at[me], send_sem=send_sems.at[d],
                                              recv_sem=recv_sems.at[me], device_id=to, device_id_type=MESH)
            cps.append(cp)

            @pl.when(d != me)
            def _():
                cp.start()

        for d in range(N_DEV):
            @pl.when(d != me)
            def _():
                pltpu.make_async_remote_copy(src_ref=x_ref, dst_ref=slots.at[d], send_sem=send_sems.at[d],
                                             recv_sem=recv_sems.at[d], device_id=(x, y, c), device_id_type=MESH).wait_recv()
                cps[d].wait_send()

        acc = slots[0]
        for d in range(1, N_DEV):
            acc = acc + slots[d]
        o_ref[...] = acc

    vm = pl.BlockSpec(memory_space=pltpu.VMEM)
    return _pcall(
        body, in_specs=[vm], out_specs=vm, out_shape=_sds((R, C), F32),
        scratch_shapes=[pltpu.VMEM((N_DEV, R, C), F32), pltpu.SemaphoreType.DMA((N_DEV,)), pltpu.SemaphoreType.DMA((N_DEV,))],
        compiler_params=pltpu.CompilerParams(vmem_limit_bytes=VMEM_LIMIT_BYTES), name="all_reduce_small",
    )(packed)


PACK = SUBLANES * LANES


def _pack(arrays):
    flat = []
    for a in arrays:
        v = a.reshape(-1)
        flat.append(jnp.pad(v, (0, (-v.shape[0]) % PACK)))
    return jnp.concatenate(flat).reshape(-1, LANES)


def _unpack(packed, shapes):
    flat = packed.reshape(-1)
    out, pos = [], 0
    for s in shapes:
        n = 1
        for d in s:
            n *= d
        out.append(flat[pos:pos + n].reshape(s))
        pos += n + (-n) % PACK
    return out


def kernel(x, mix_norm_g, ffn_norm_g, conv_w_in, conv_a_dw_w, conv_a_dw_b, conv_a_ln_g, conv_a_ln_b, conv_b_dw_w, conv_w_out, attn_w_qkv, attn_q_g, attn_k_g, attn_w_o, ffn_w_up, ffn_dw_w, ffn_dw_b, ffn_w_down, loss_target, m_mix_norm_g, m_ffn_norm_g, m_conv_w_in, m_conv_a_dw_w, m_conv_a_dw_b, m_conv_a_ln_g, m_conv_a_ln_b, m_conv_b_dw_w, m_conv_w_out, m_attn_w_qkv, m_attn_q_g, m_attn_k_g, m_attn_w_o, m_ffn_w_up, m_ffn_dw_w, m_ffn_dw_b, m_ffn_w_down, v_mix_norm_g, v_ffn_norm_g, v_conv_w_in, v_conv_a_dw_w, v_conv_a_dw_b, v_conv_a_ln_g, v_conv_a_ln_b, v_conv_b_dw_w, v_conv_w_out, v_attn_w_qkv, v_attn_q_g, v_attn_k_g, v_attn_w_o, v_ffn_w_up, v_ffn_dw_w, v_ffn_dw_b, v_ffn_w_down):
    depth = mix_norm_g.shape[0]
    n_even, n_odd = conv_w_in.shape[0], attn_w_qkv.shape[0]
    S, D = x.shape[1], x.shape[2]
    dg = D // 2
    x0 = x.reshape(S, D)
    target = loss_target.reshape(S, D)
    j_me = 2 * lax.axis_index("x") + lax.axis_index("y")
    c_me = lax.axis_index("c")
    j_idx = j_me.astype(jnp.int32).reshape(1)
    c_idx = c_me.astype(jnp.int32).reshape(1)

    col_names = ["conv_w_in", "attn_w_qkv", "ffn_w_up"]
    row_names = ["conv_w_out", "attn_w_o", "ffn_w_down"]
    local = dict(conv_w_in=conv_w_in, attn_w_qkv=attn_w_qkv, ffn_w_up=ffn_w_up, conv_w_out=conv_w_out, attn_w_o=attn_w_o,
                 ffn_w_down=ffn_w_down)
    gathered, (a_dw, b_dw, f_dw) = _all_gather_weights(
        [_place_shard(f"place_{n}", local[n], j_idx) for n in col_names + row_names], [conv_a_dw_w, conv_b_dw_w, ffn_dw_w])
    w_in, w_qkv, w_up = (g.reshape(g.shape[0], N_CHIPS, -1, g.shape[4]) for g in gathered[:3])
    w_out, w_o, w_down = (g.reshape(g.shape[0], -1, g.shape[4]) for g in gathered[3:])
    unshard = lambda a: jnp.moveaxis(a, 1, 2).reshape(a.shape[0], a.shape[2], N_CHIPS * a.shape[3])
    a_dw, b_dw, f_dw = unshard(a_dw), unshard(b_dw), unshard(f_dw)
    qk_gain = [jnp.stack([jnp.tile(attn_q_g[i], LANES // HEAD_DIM), jnp.tile(attn_k_g[i], LANES // HEAD_DIM)])
               for i in range(n_odd)]

    saved = []
    xc = x0
    for layer in range(depth):
        i = layer // 2
        tag = f"l{layer}"
        s = {"x_in": xc}
        h = _rms_fwd(f"rms_mix_fwd_{tag}", xc, mix_norm_g, layer)
        s["h"] = h
        if layer % 2 == 0:
            p = _mm_fwd(f"conv_in_fwd_{tag}", h, w_in, i, colshard=True)
            ab = _convmix_fwd(f"convmix_fwd_{tag}", p, a_dw, conv_a_dw_b, conv_a_ln_g, conv_a_ln_b, b_dw, i)
            xm = _mm_fwd(f"conv_out_fwd_{tag}", ab, w_out, i, colshard=False, res=xc)
            s.update(p=p, ab=ab)
        else:
            qkv = _mm_fwd(f"attn_qkv_fwd_{tag}", h, w_qkv, i, colshard=True)
            qs, kn, vb = _qknorm_fwd(f"qknorm_fwd_{tag}", qkv, qk_gain[i])
            o = _attn_fwd(f"attn_fwd_{tag}", qs, kn, vb)
            xm = _mm_fwd(f"attn_out_fwd_{tag}", o, w_o, i, colshard=False, res=xc)
            s.update(qkv=qkv, qs=qs, kn=kn, vb=vb, o=o)
        s["x_mid"] = xm
        h2 = _rms_fwd(f"rms_ffn_fwd_{tag}", xm, ffn_norm_g, layer)
        u2 = _mm_fwd(f"ffn_up_fwd_{tag}", h2, w_up, layer, colshard=True, out_split=2)
        f = _ffn_mid_fwd(f"ffn_mid_fwd_{tag}", u2, f_dw, ffn_dw_b, layer)
        xc = _mm_fwd(f"ffn_down_fwd_{tag}", f, w_down, layer, colshard=False, res=xm)
        s.update(h2=h2, u2=u2, f=f)
        saved.append(s)

    dx, loss_tile = _loss_fwd_bwd("loss", xc, target)

    g_up = g_down = g_in = g_out = g_qkv = g_o = None
    d_mix_g, d_ffn_g = [None] * depth, [None] * depth
    d_ffn_dw_w, d_ffn_dw_b = [None] * depth, [None] * depth
    d_a_dw_w, d_a_dw_b, d_a_ln_g, d_a_ln_b, d_b_dw_w = ([None] * n_even for _ in range(5))
    d_q_g, d_k_g = [None] * n_odd, [None] * n_odd
    for layer in reversed(range(depth)):
        i = layer // 2
        tag = f"l{layer}"
        s = saved[layer]
        df = _mm_dgrad(f"ffn_down_dgrad_{tag}", dx, w_down, layer, colshard=False)
        g_down = _mm_wgrad(f"ffn_down_wgrad_{tag}", s["f"], dx, layer, depth, g_down, colshard=False)
        du2, dww, dwb = _ffn_mid_bwd(f"ffn_mid_bwd_{tag}", s["u2"], df, f_dw, ffn_dw_b, layer)
        d_ffn_dw_w[layer] = jnp.moveaxis(dww, 0, 1).reshape(FFN_CONV_WIDTH, -1)
        d_ffn_dw_b[layer] = dwb.reshape(-1)
        dh2 = _mm_dgrad(f"ffn_up_dgrad_{tag}", du2, w_up, layer, colshard=True)
        g_up = _mm_wgrad(f"ffn_up_wgrad_{tag}", s["h2"], du2, layer, depth, g_up, colshard=True)
        dx, dg_ = _rms_bwd(f"rms_ffn_bwd_{tag}", s["x_mid"], ffn_norm_g, layer, dh2, dx)
        d_ffn_g[layer] = dg_.reshape(-1)
        if layer % 2 == 0:
            dab = _mm_dgrad(f"conv_out_dgrad_{tag}", dx, w_out, i, colshard=False)
            g_out = _mm_wgrad(f"conv_out_wgrad_{tag}", s["ab"], dx, i, n_even, g_out, colshard=False)
            dp, daw, dab_b, dlg, dlb, dbw = _convmix_bwd(f"convmix_bwd_{tag}", s["p"], dab, a_dw, conv_a_dw_b, conv_a_ln_g,
                                                         conv_a_ln_b, b_dw, i)
            d_a_dw_w[i], d_a_dw_b[i], d_a_ln_g[i], d_a_ln_b[i], d_b_dw_w[i] = (
                daw, dab_b.reshape(-1), dlg.reshape(-1), dlb.reshape(-1), dbw)
            dh = _mm_dgrad(f"conv_in_dgrad_{tag}", dp, w_in, i, colshard=True)
            g_in = _mm_wgrad(f"conv_in_wgrad_{tag}", s["h"], dp, i, n_even, g_in, colshard=True)
        else:
            do = _mm_dgrad(f"attn_out_dgrad_{tag}", dx, w_o, i, colshard=False)
            g_o = _mm_wgrad(f"attn_out_wgrad_{tag}", s["o"], dx, i, n_odd, g_o, colshard=False)
            dq, dk, dv = _attn_bwd(f"attn_bwd_{tag}", s["qs"], s["kn"], s["vb"], s["o"], do)
            dqkv, dgain = _qknorm_bwd(f"qknorm_bwd_{tag}", s["qkv"], dq, dk, dv, qk_gain[i])
            d_q_g[i] = dgain[0, :HEAD_DIM] + dgain[0, HEAD_DIM:]
            d_k_g[i] = dgain[1, :HEAD_DIM] + dgain[1, HEAD_DIM:]
            dh = _mm_dgrad(f"attn_qkv_dgrad_{tag}", dqkv, w_qkv, i, colshard=True)
            g_qkv = _mm_wgrad(f"attn_qkv_wgrad_{tag}", s["h"], dqkv, i, n_odd, g_qkv, colshard=True)
        dx, dg_ = _rms_bwd(f"rms_mix_bwd_{tag}", s["x_in"], mix_norm_g, layer, dh, dx)
        d_mix_g[layer] = dg_.reshape(-1)
    grad_x = dx.reshape(1, S, D)

    small = {
        "mix_norm_g": jnp.stack(d_mix_g), "ffn_norm_g": jnp.stack(d_ffn_g),
        "conv_a_dw_w": jnp.stack(d_a_dw_w), "conv_a_dw_b": jnp.stack(d_a_dw_b),
        "conv_a_ln_g": jnp.stack(d_a_ln_g), "conv_a_ln_b": jnp.stack(d_a_ln_b),
        "conv_b_dw_w": jnp.stack(d_b_dw_w), "attn_q_g": jnp.stack(d_q_g), "attn_k_g": jnp.stack(d_k_g),
        "ffn_dw_w": jnp.stack(d_ffn_dw_w), "ffn_dw_b": jnp.stack(d_ffn_dw_b),
    }
    small_names = list(small)
    summed = _all_reduce_small(_pack([loss_tile] + [small[n] for n in small_names]))
    parts = _unpack(summed, [loss_tile.shape] + [small[n].shape for n in small_names])
    loss = parts[0][0, 0]
    small_g = dict(zip(small_names, parts[1:]))
    for n in ("conv_a_dw_w", "conv_b_dw_w", "ffn_dw_w"):
        cs = small_g[n].shape[2] // N_CHIPS
        small_g[n] = lax.dynamic_slice_in_dim(small_g[n], j_me * cs, cs, axis=2)

    big = {"conv_w_in": g_in, "attn_w_qkv": g_qkv, "ffn_w_up": g_up, "conv_w_out": g_out, "attn_w_o": g_o, "ffn_w_down": g_down}
    big_names = col_names + row_names
    five = []
    for n in big_names:
        g = big[n]
        if n in col_names:
            five.append(g.reshape(g.shape[0], N_CHIPS, 2, g.shape[2] // 2, g.shape[3]))
        else:
            five.append(g.reshape(g.shape[0], N_CHIPS, 2, g.shape[1] // (2 * N_CHIPS), g.shape[2]))
    from_sibling = _exchange_core_halves(five)
    both = [_add_core_halves(f"grad_add_core_{n}", g, a, c_idx) for n, g, a in zip(big_names, five, from_sibling)]
    chip_sums = [b[0] for b in both]
    from_chips = _exchange_chip_shards([b[1] for b in both])
    jc_idx = jnp.concatenate([j_idx, c_idx])
    totals = [_add_chip_shards(f"grad_add_chips_{n}", p, b, jc_idx) for n, p, b in zip(big_names, chip_sums, from_chips)]
    big_g = dict(zip(big_names, _join_core_halves(totals)))

    weights = dict(mix_norm_g=mix_norm_g, ffn_norm_g=ffn_norm_g, conv_w_in=conv_w_in, conv_a_dw_w=conv_a_dw_w, conv_a_dw_b=conv_a_dw_b, conv_a_ln_g=conv_a_ln_g, conv_a_ln_b=conv_a_ln_b, conv_b_dw_w=conv_b_dw_w, conv_w_out=conv_w_out, attn_w_qkv=attn_w_qkv, attn_q_g=attn_q_g, attn_k_g=attn_k_g, attn_w_o=attn_w_o, ffn_w_up=ffn_w_up, ffn_dw_w=ffn_dw_w, ffn_dw_b=ffn_dw_b, ffn_w_down=ffn_w_down)
    m_in = dict(mix_norm_g=m_mix_norm_g, ffn_norm_g=m_ffn_norm_g, conv_w_in=m_conv_w_in, conv_a_dw_w=m_conv_a_dw_w, conv_a_dw_b=m_conv_a_dw_b, conv_a_ln_g=m_conv_a_ln_g, conv_a_ln_b=m_conv_a_ln_b, conv_b_dw_w=m_conv_b_dw_w, conv_w_out=m_conv_w_out, attn_w_qkv=m_attn_w_qkv, attn_q_g=m_attn_q_g, attn_k_g=m_attn_k_g, attn_w_o=m_attn_w_o, ffn_w_up=m_ffn_w_up, ffn_dw_w=m_ffn_dw_w, ffn_dw_b=m_ffn_dw_b, ffn_w_down=m_ffn_w_down)
    v_in = dict(mix_norm_g=v_mix_norm_g, ffn_norm_g=v_ffn_norm_g, conv_w_in=v_conv_w_in, conv_a_dw_w=v_conv_a_dw_w, conv_a_dw_b=v_conv_a_dw_b, conv_a_ln_g=v_conv_a_ln_g, conv_a_ln_b=v_conv_a_ln_b, conv_b_dw_w=v_conv_b_dw_w, conv_w_out=v_conv_w_out, attn_w_qkv=v_attn_w_qkv, attn_q_g=v_attn_q_g, attn_k_g=v_attn_k_g, attn_w_o=v_attn_w_o, ffn_w_up=v_ffn_w_up, ffn_dw_w=v_ffn_dw_w, ffn_dw_b=v_ffn_dw_b, ffn_w_down=v_ffn_w_down)
    order = list(weights)
    grads, delta, new_m, new_v = {}, {}, {}, {}
    for n in big_names:
        grads[n] = big_g[n]
        delta[n], new_m[n], new_v[n] = _adamw(f"adamw_{n}", weights[n], big_g[n], m_in[n], v_in[n])
    shapes = [weights[n].shape for n in small_names]
    packed = [_pack([d[n] for n in small_names]) for d in (weights, small_g, m_in, v_in)]
    upd = _adamw("adamw_small", *[p[None] for p in packed])
    for out, res in zip((delta, new_m, new_v), upd):
        out.update(zip(small_names, _unpack(res[0], shapes)))
    grads.update({n: small_g[n].reshape(weights[n].shape) for n in small_names})
    return (loss, grad_x, *[grads[n] for n in order], *[delta[n] for n in order], *[new_m[n] for n in order],
            *[new_v[n] for n in order])
```

```python
import jax
import jax.numpy as jnp
from jax import lax
from jax.experimental import pallas as pl
from jax.experimental.pallas import tpu as pltpu

F32 = jnp.float32
BF16 = jnp.bfloat16
EPS = 1e-6
CONV_A_WIDTH = 31
CONV_B_WIDTH = 3
FFN_CONV_WIDTH = 3
HEAD_DIM = 64
ADAM_LR = 0.001
ADAM_B1 = 0.9
ADAM_B2 = 0.999
ADAM_EPS = 1e-08
ADAM_WD = 0.01
ADAM_STEP = 10

LANES = 128
SUBLANES = 8
BF16_ROWS = 16
V7X_VMEM_BYTES = 64 * 1024 * 1024
VMEM_LIMIT_BYTES = V7X_VMEM_BYTES * 3 // 4
MM_VMEM_BUDGET = VMEM_LIMIT_BYTES * 4 // 5
MM_ROWS = 1024
N_CHIPS = 4
N_DEV = 8
HALO_A = 32
HALO_S = 8
ELT_ROWS = 64
ATTN_BLOCK = 128
ATTN_SUB = 2
ATTN_MORE = 2
EXP_UNDERFLOW = -104.0
MESH = pl.DeviceIdType.MESH
ANY = pl.BlockSpec(memory_space=pl.ANY)
NT = (((1,), (1,)), ((), ()))
NN = (((1,), (0,)), ((), ()))
TN = (((0,), (0,)), ((), ()))


def _pcall(body, **kw):
    return pl.pallas_call(body, **kw)


def _cp(*sem):
    return pltpu.CompilerParams(dimension_semantics=sem, vmem_limit_bytes=VMEM_LIMIT_BYTES)


def _sds(shape, dtype):
    return jax.ShapeDtypeStruct(tuple(shape), dtype)


def _tile(n, cap, align=LANES):
    if n <= cap:
        return n
    for t in range(cap - cap % align, 0, -align):
        if n % t == 0:
            return t
    return n


def _sig(x):
    return 0.5 * jnp.tanh(0.5 * x) + 0.5


def _rowsum(x):
    return jnp.sum(x, axis=0, keepdims=True)


def _fold(x):
    acc = x[0:SUBLANES]
    for r in range(SUBLANES, x.shape[0], SUBLANES):
        acc = acc + x[r:r + SUBLANES]
    return acc


def _mm_call(name, dn, operands, in_specs, out_shape, out_spec, grid, nk, acc_shape, has_res, has_alias):
    def body(*refs):
        a_ref, b_ref = refs[0], refs[1]
        pos = 2
        res_ref = refs[pos] if has_res else None
        pos += int(has_res) + int(has_alias)
        o_ref = refs[pos]
        acc_ref = refs[pos + 1] if nk > 1 else None
        p = lax.dot_general(a_ref[...].astype(BF16), b_ref[...].astype(BF16), dn, preferred_element_type=F32)

        def finish(v):
            if has_res:
                v = v + res_ref[...]
            o_ref[...] = v.astype(o_ref.dtype)

        if nk == 1:
            finish(p)
        else:
            k = pl.program_id(2)

            @pl.when(k == 0)
            def _():
                acc_ref[...] = p

            @pl.when(k > 0)
            def _():
                acc_ref[...] += p

            @pl.when(k == nk - 1)
            def _():
                finish(acc_ref[...])

    aliases = {len(operands) - 1: 0} if has_alias else {}
    return _pcall(
        body, grid=grid, in_specs=in_specs, out_specs=out_spec, out_shape=out_shape,
        scratch_shapes=[pltpu.VMEM(acc_shape, F32)] if nk > 1 else [],
        input_output_aliases=aliases, compiler_params=_cp("parallel", "parallel", "arbitrary"), name=name,
    )(*operands)


def _mm_fwd(name, a, w, l, *, colshard, res=None, out_split=1):
    M, K = a.shape
    tm = _tile(M, MM_ROWS, BF16_ROWS)
    if colshard:
        cs = w.shape[3]
        N, tn, tk = N_CHIPS * cs, cs, K
        b_spec = pl.BlockSpec((None, None, tk, tn), lambda j, i, k: (l, j, k, 0))
    else:
        N = w.shape[2]
        tn, tk = _tile(N, 1024), _tile(K, 1536)
        b_spec = pl.BlockSpec((None, tk, tn), lambda j, i, k: (l, k, j))
    nk = K // tk
    in_specs = [pl.BlockSpec((tm, tk), lambda j, i, k: (i, k)), b_spec]
    operands = [a, w]
    if res is not None:
        in_specs.append(pl.BlockSpec((tm, tn), lambda j, i, k: (i, j)))
        operands.append(res)
    if out_split == 1:
        out_shape = _sds((M, N), F32)
        out_spec = pl.BlockSpec((tm, tn), lambda j, i, k: (i, j))
    else:
        per = N // tn // out_split
        out_shape = _sds((out_split, M, N // out_split), F32)
        out_spec = pl.BlockSpec((None, tm, tn), lambda j, i, k: (j // per, i, j % per))
    return _mm_call(name, NN, operands, in_specs, out_shape, out_spec, (N // tn, M // tm, nk), nk, (tm, tn),
                    res is not None, False)


def _mm_dgrad(name, g, w, l, *, colshard):
    split = g.ndim == 3
    M = g.shape[-2]
    tm = _tile(M, MM_ROWS, BF16_ROWS)
    if colshard:
        kw, cs = w.shape[2], w.shape[3]
        tn, tk, nk = _tile(kw, 1408), cs, N_CHIPS
        b_spec = pl.BlockSpec((None, None, tn, tk), lambda j, i, k: (l, k, j, 0))
    else:
        kw, ncon = w.shape[1], w.shape[2]
        tn, tk = _tile(kw, 1408), _tile(ncon, 1536)
        nk = ncon // tk
        b_spec = pl.BlockSpec((None, tn, tk), lambda j, i, k: (l, j, k))
    if split:
        per = nk // g.shape[0]
        a_spec = pl.BlockSpec((None, tm, tk), lambda j, i, k: (k // per, i, k % per))
    else:
        a_spec = pl.BlockSpec((tm, tk), lambda j, i, k: (i, k))
    out_shape = _sds((M, kw), F32)
    out_spec = pl.BlockSpec((tm, tn), lambda j, i, k: (i, j))
    return _mm_call(name, NT, [g, w], [a_spec, b_spec], out_shape, out_spec, (kw // tn, M // tm, nk), nk, (tm, tn),
                    False, False)


def _mm_wgrad(name, a, g, l, n_layers, buf, *, colshard):
    S, M = a.shape
    split = g.ndim == 3
    N = g.shape[-1] * (g.shape[0] if split else 1)
    tm = _tile(M, 1408)
    tn = N // N_CHIPS if colshard else _tile(N, 1024)
    per_row = 2 * (tm * a.dtype.itemsize + tn * g.dtype.itemsize)
    tk = _tile(S, max(BF16_ROWS, min(2048, (MM_VMEM_BUDGET - 3 * tm * tn * 4) // per_row)), BF16_ROWS)
    nk = S // tk
    if colshard:
        out_shape = _sds((n_layers, N_CHIPS, M, tn), F32)
        out_spec = pl.BlockSpec((None, None, tm, tn), lambda j, i, k: (l, j, i, 0))
    else:
        out_shape = _sds((n_layers, M, N), F32)
        out_spec = pl.BlockSpec((None, tm, tn), lambda j, i, k: (l, i, j))
    if split:
        per = N // tn // g.shape[0]
        b_spec = pl.BlockSpec((None, tk, tn), lambda j, i, k: (j // per, k, j % per))
    else:
        b_spec = pl.BlockSpec((tk, tn), lambda j, i, k: (k, j))
    in_specs = [pl.BlockSpec((tk, tm), lambda j, i, k: (k, i)), b_spec]
    operands = [a, g]
    if buf is not None:
        in_specs.append(ANY)
        operands.append(buf)
    return _mm_call(name, TN, operands, in_specs, out_shape, out_spec, (N // tn, M // tm, nk), nk, (tm, tn),
                    False, buf is not None)


def _rms_fwd(name, x, g, l):
    S, D = x.shape
    tm = _tile(S, 512, BF16_ROWS)

    def body(x_ref, g_ref, o_ref):
        xf = x_ref[...]
        r = lax.rsqrt(jnp.mean(xf * xf, axis=-1, keepdims=True) + EPS)
        o_ref[...] = (xf * r * g_ref[l:l + 1, :]).astype(BF16)

    return _pcall(
        body, grid=(S // tm,),
        in_specs=[pl.BlockSpec((tm, D), lambda i: (i, 0)), pl.BlockSpec(g.shape, lambda i: (0, 0))],
        out_specs=pl.BlockSpec((tm, D), lambda i: (i, 0)), out_shape=_sds((S, D), BF16),
        compiler_params=_cp("parallel"), name=name,
    )(x, g)


def _rms_bwd(name, x, g, l, dh, dres):
    S, D = x.shape
    tm = _tile(S, 512, SUBLANES)

    def body(x_ref, g_ref, dh_ref, dr_ref, dx_ref, dg_ref):
        xf = x_ref[...]
        r = lax.rsqrt(jnp.mean(xf * xf, axis=-1, keepdims=True) + EPS)
        xh = xf * r
        d = dh_ref[...]
        dxh = d * g_ref[l:l + 1, :]
        dx_ref[...] = dr_ref[...] + r * (dxh - xh * jnp.mean(dxh * xh, axis=-1, keepdims=True))

        @pl.when(pl.program_id(0) == 0)
        def _():
            dg_ref[...] = jnp.zeros_like(dg_ref)

        dg_ref[...] += _rowsum(d * xh)

    row = pl.BlockSpec((tm, D), lambda i: (i, 0))
    return _pcall(
        body, grid=(S // tm,),
        in_specs=[row, pl.BlockSpec(g.shape, lambda i: (0, 0)), row, row],
        out_specs=[row, pl.BlockSpec((1, D), lambda i: (0, 0))],
        out_shape=[_sds((S, D), F32), _sds((1, D), F32)],
        compiler_params=_cp("arbitrary"), name=name,
    )(x, g, dh, dres)


def _loss_fwd_bwd(name, y, t):
    S, D = y.shape
    tm = _tile(S, 512, SUBLANES)

    def body(y_ref, t_ref, dy_ref, l_ref):
        e = y_ref[...] - t_ref[...]
        dy_ref[...] = e * (1.0 / D)

        @pl.when(pl.program_id(0) == 0)
        def _():
            l_ref[...] = jnp.zeros_like(l_ref)

        l_ref[...] += 0.5 * jnp.sum(jnp.sum(e * e, axis=-1, keepdims=True) * (1.0 / D), axis=0, keepdims=True)

    row = pl.BlockSpec((tm, D), lambda i: (i, 0))
    return _pcall(
        body, grid=(S // tm,), in_specs=[row, row],
        out_specs=[row, pl.BlockSpec((SUBLANES, LANES), lambda i: (0, 0))],
        out_shape=[_sds((S, D), F32), _sds((SUBLANES, LANES), F32)],
        compiler_params=_cp("arbitrary"), name=name,
    )(y, t)


def _delayed_copies(us, n_rows):
    for s in range(1, SUBLANES):
        us[s, pl.ds(SUBLANES, n_rows - SUBLANES), :] = us[0, pl.ds(SUBLANES - s, n_rows - SUBLANES), :]


def _conv_a(aw_ref, ab_ref, l, us, row0, rows, dg):
    ka = CONV_A_WIDTH
    out = []
    for c0 in range(0, dg, LANES):
        lanes = slice(c0, c0 + LANES)
        acc = ab_ref[l:l + 1, lanes]
        for d in range(ka):
            a, s = divmod(d, SUBLANES)
            acc = acc + aw_ref[l, ka - 1 - d:ka - d, lanes] * us[s, pl.ds(row0 - SUBLANES * a, rows), lanes]
        out.append(acc)
    return jnp.concatenate(out, axis=1)


def _convmix_fwd(name, p, aw, ab, lg, lb, bw, l):
    S, W = p.shape
    dg = W // 5
    tm = _tile(S, 256, HALO_A)
    nb = tm // HALO_A
    ka, kb = CONV_A_WIDTH, CONV_B_WIDTH

    ext = HALO_A + tm
    rc = _tile(tm, ELT_ROWS, BF16_ROWS)

    def body(p_ref, ph_ref, aw_ref, ab_ref, lg_ref, lb_ref, bw_ref, o_ref, us, mext):
        first = pl.program_id(0) == 0
        ph = ph_ref[...]
        pc = p_ref[...]
        us[0, pl.ds(0, HALO_A), :] = jnp.where(first, 0.0, ph[:, 0:dg] * _sig(ph[:, dg:2 * dg]))
        us[0, pl.ds(HALO_A, tm), :] = pc[:, 0:dg] * _sig(pc[:, dg:2 * dg])
        mext[pl.ds(0, HALO_A), :] = jnp.where(first, 0.0, ph[:, 3 * dg:4 * dg] * ph[:, 4 * dg:5 * dg])
        mext[pl.ds(HALO_A, tm), :] = pc[:, 3 * dg:4 * dg] * pc[:, 4 * dg:5 * dg]
        _delayed_copies(us, ext)
        for r0 in range(0, tm, rc):
            rows = pl.ds(r0, rc)
            c = _conv_a(aw_ref, ab_ref, l, us, HALO_A + r0, rc, dg)
            xc = c - jnp.mean(c, axis=-1, keepdims=True)
            ln = xc * lax.rsqrt(jnp.mean(xc * xc, axis=-1, keepdims=True) + EPS) * lg_ref[l:l + 1, :] + lb_ref[l:l + 1, :]
            o_ref[rows, 0:dg] = (ln * _sig(ln)).astype(BF16)
            cb = bw_ref[l, 0:1, :] * mext[pl.ds(HALO_A - (kb - 1) + r0, rc), :]
            for k in range(1, kb):
                cb = cb + bw_ref[l, k:k + 1, :] * mext[pl.ds(HALO_A - (kb - 1) + k + r0, rc), :]
            o_ref[rows, dg:2 * dg] = (p_ref[rows, 2 * dg:3 * dg] * cb).astype(BF16)

    full = lambda a: pl.BlockSpec(a.shape, lambda i: (0,) * a.ndim)
    return _pcall(
        body, grid=(S // tm,),
        in_specs=[pl.BlockSpec((tm, W), lambda i: (i, 0)),
                  pl.BlockSpec((HALO_A, W), lambda i: (jnp.maximum(i * nb - 1, 0), 0)),
                  full(aw), full(ab), full(lg), full(lb), full(bw)],
        out_specs=pl.BlockSpec((tm, 2 * dg), lambda i: (i, 0)), out_shape=_sds((S, 2 * dg), BF16),
        scratch_shapes=[pltpu.VMEM((SUBLANES, ext, dg), F32), pltpu.VMEM((ext, dg), F32)],
        compiler_params=_cp("parallel"), name=name,
    )(p, p, aw, ab, lg, lb, bw)


def _convmix_bwd(name, p, dab, aw, ab, lg, lb, bw, l):
    S, W = p.shape
    dg = W // 5
    tm = _tile(S, 256, HALO_A)
    nb = tm // HALO_A
    n_i = S // tm
    ka, kb = CONV_A_WIDTH, CONV_B_WIDTH
    n = tm + HALO_A
    ext = HALO_A + n
    rc = _tile(tm, ELT_ROWS, BF16_ROWS)

    def body(p_ref, pp_ref, pn_ref, d_ref, dn_ref, aw_ref, ab_ref, lg_ref, lb_ref, bw_ref,
             dp_ref, daw_ref, dab_ref, dlg_ref, dlb_ref, dbw_ref, us, mext, dcs, dbext, accw):
        i = pl.program_id(0)
        first, last = i == 0, i == n_i - 1

        @pl.when(first)
        def _():
            for r in (daw_ref, dab_ref, dlg_ref, dlb_ref, dbw_ref):
                r[...] = jnp.zeros_like(r)

        accw[...] = jnp.zeros_like(accw)
        pp, pc, pn = pp_ref[...], p_ref[...], pn_ref[...]
        glu = lambda b: b[:, 0:dg] * _sig(b[:, dg:2 * dg])
        gch = lambda b: b[:, 3 * dg:4 * dg] * b[:, 4 * dg:5 * dg]
        us[0, pl.ds(0, HALO_A), :] = jnp.where(first, 0.0, glu(pp))
        us[0, pl.ds(HALO_A, tm), :] = glu(pc)
        us[0, pl.ds(HALO_A + tm, HALO_A), :] = glu(pn)
        mext[pl.ds(0, HALO_A), :] = jnp.where(first, 0.0, gch(pp))
        mext[pl.ds(HALO_A, tm), :] = gch(pc)
        mext[pl.ds(HALO_A + tm, HALO_A), :] = gch(pn)
        _delayed_copies(us, ext)
        chunks = [(r, rc) for r in range(0, tm, rc)] + [(tm, HALO_A)]
        g_ln = lg_ref[l:l + 1, :]
        zero8 = jnp.zeros((SUBLANES, dg), F32)

        acc_lg = acc_lb = acc_ab = zero8
        for r0, rows in chunks:
            c = _conv_a(aw_ref, ab_ref, l, us, HALO_A + r0, rows, dg)
            xc = c - jnp.mean(c, axis=-1, keepdims=True)
            rstd = lax.rsqrt(jnp.mean(xc * xc, axis=-1, keepdims=True) + EPS)
            chat = xc * rstd
            ln = chat * g_ln + lb_ref[l:l + 1, :]
            s = _sig(ln)
            da = d_ref[pl.ds(r0, rows), 0:dg] if r0 < tm else jnp.where(last, 0.0, dn_ref[:, 0:dg])
            dln = da * (s * (1.0 + ln * (1.0 - s)))
            dlnh = dln * g_ln
            dc = rstd * (dlnh - jnp.mean(dlnh, axis=-1, keepdims=True)
                         - chat * jnp.mean(dlnh * chat, axis=-1, keepdims=True))
            dcs[0, pl.ds(r0, rows), :] = dc
            if r0 < tm:
                acc_lg = acc_lg + _fold(dln * chat)
                acc_lb = acc_lb + _fold(dln)
                acc_ab = acc_ab + _fold(dc)
                for c0 in range(0, dg, LANES):
                    lanes = slice(c0, c0 + LANES)
                    for d in range(ka):
                        a, sh = divmod(d, SUBLANES)
                        k = ka - 1 - d
                        accw[pl.ds(SUBLANES * k, SUBLANES), lanes] += _fold(
                            dc[:, lanes] * us[sh, pl.ds(HALO_A + r0 - SUBLANES * a, rows), lanes])
        dlg_ref[...] += _rowsum(acc_lg)
        dlb_ref[...] += _rowsum(acc_lb)
        dab_ref[...] += _rowsum(acc_ab)
        for k in range(ka):
            daw_ref[k:k + 1, :] += _rowsum(accw[pl.ds(SUBLANES * k, SUBLANES), :])
        for s in range(1, SUBLANES):
            dcs[s, pl.ds(0, n - SUBLANES), :] = dcs[0, pl.ds(s, n - SUBLANES), :]
        for r0 in range(0, tm, rc):
            rows = pl.ds(r0, rc)
            parts = []
            for c0 in range(0, dg, LANES):
                lanes = slice(c0, c0 + LANES)
                acc = aw_ref[l, ka - 1:ka, lanes] * dcs[0, rows, lanes]
                for e in range(1, ka):
                    a, sh = divmod(e, SUBLANES)
                    acc = acc + aw_ref[l, ka - 1 - e:ka - e, lanes] * dcs[sh, pl.ds(r0 + SUBLANES * a, rc), lanes]
                parts.append(acc)
            du = jnp.concatenate(parts, axis=1)
            sg = _sig(p_ref[rows, dg:2 * dg])
            dp_ref[rows, 0:dg] = (du * sg).astype(BF16)
            dp_ref[rows, dg:2 * dg] = (du * p_ref[rows, 0:dg] * sg * (1.0 - sg)).astype(BF16)

        for r0, rows in chunks:
            if r0 < tm:
                dbext[pl.ds(r0, rows), :] = d_ref[pl.ds(r0, rows), dg:2 * dg] * p_ref[pl.ds(r0, rows), 2 * dg:3 * dg]
            else:
                dbext[pl.ds(r0, rows), :] = jnp.where(last, 0.0, dn_ref[:, dg:2 * dg] * pn[:, 2 * dg:3 * dg])
        acc_bw = [zero8] * kb
        for r0 in range(0, tm, rc):
            rows = pl.ds(r0, rc)
            m_k = [mext[pl.ds(HALO_A - (kb - 1) + k + r0, rc), :] for k in range(kb)]
            cb = bw_ref[l, 0:1, :] * m_k[0]
            dm = bw_ref[l, 0:1, :] * dbext[pl.ds(r0 + kb - 1, rc), :]
            for k in range(1, kb):
                cb = cb + bw_ref[l, k:k + 1, :] * m_k[k]
                dm = dm + bw_ref[l, k:k + 1, :] * dbext[pl.ds(r0 + kb - 1 - k, rc), :]
            dcb = dbext[rows, :]
            acc_bw = [acc_bw[k] + _fold(dcb * m_k[k]) for k in range(kb)]
            dp_ref[rows, 2 * dg:3 * dg] = (d_ref[rows, dg:2 * dg] * cb).astype(BF16)
            dp_ref[rows, 3 * dg:4 * dg] = (dm * p_ref[rows, 4 * dg:5 * dg]).astype(BF16)
            dp_ref[rows, 4 * dg:5 * dg] = (dm * p_ref[rows, 3 * dg:4 * dg]).astype(BF16)
        for k in range(kb):
            dbw_ref[k:k + 1, :] += _rowsum(acc_bw[k])

    full = lambda a: pl.BlockSpec(a.shape, lambda i: (0,) * a.ndim)
    prev = lambda i: (jnp.maximum(i * nb - 1, 0), 0)
    nxt = lambda i: (jnp.minimum((i + 1) * nb, S // HALO_A - 1), 0)
    acc = lambda r: pl.BlockSpec((r, dg), lambda i: (0, 0))
    return _pcall(
        body, grid=(n_i,),
        in_specs=[pl.BlockSpec((tm, W), lambda i: (i, 0)), pl.BlockSpec((HALO_A, W), prev), pl.BlockSpec((HALO_A, W), nxt),
                  pl.BlockSpec((tm, 2 * dg), lambda i: (i, 0)), pl.BlockSpec((HALO_A, 2 * dg), nxt),
                  full(aw), full(ab), full(lg), full(lb), full(bw)],
        out_specs=[pl.BlockSpec((tm, W), lambda i: (i, 0)), acc(ka), acc(1), acc(1), acc(1), acc(kb)],
        out_shape=[_sds((S, W), BF16), _sds((ka, dg), F32), _sds((1, dg), F32), _sds((1, dg), F32), _sds((1, dg), F32),
                   _sds((kb, dg), F32)],
        scratch_shapes=[pltpu.VMEM((SUBLANES, ext, dg), F32), pltpu.VMEM((ext, dg), F32), pltpu.VMEM((SUBLANES, n, dg), F32),
                        pltpu.VMEM((n, dg), F32), pltpu.VMEM((SUBLANES * ka, dg), F32)],
        compiler_params=_cp("arbitrary"), name=name,
    )(p, p, p, dab, dab, aw, ab, lg, lb, bw)


def _ffn_mid_fwd(name, u2, dww, dwb, l):
    _, S, F = u2.shape
    tm = _tile(S, 256, BF16_ROWS)
    tc = _tile(F, 1408)
    n_f = F // tc
    nb = tm // HALO_S
    kf = FFN_CONV_WIDTH

    def body(u_ref, uh_ref, wg_ref, wv_ref, bg_ref, bv_ref, o_ref, ext):
        first = pl.program_id(1) == 0
        ext[:, pl.ds(0, HALO_S), :] = jnp.where(first, 0.0, uh_ref[...])
        ext[:, pl.ds(HALO_S, tm), :] = u_ref[...]
        rc = _tile(tm, ELT_ROWS, BF16_ROWS)

        def lane_chunk(ci, carry):
            lanes = pl.ds(pl.multiple_of(ci * LANES, LANES), LANES)
            taps = [[w_ref[k:k + 1, lanes] for k in range(kf)] for w_ref in (wg_ref, wv_ref)]
            bias = [b_ref[l:l + 1, lanes] for b_ref in (bg_ref, bv_ref)]
            for r0 in range(0, tm, rc):
                c = []
                for g in range(2):
                    acc = bias[g]
                    for k in range(kf):
                        acc = acc + taps[g][k] * ext[g, pl.ds(HALO_S - (kf - 1) + k + r0, rc), lanes]
                    c.append(acc)
                o_ref[pl.ds(r0, rc), lanes] = (c[0] * _sig(c[0]) * c[1]).astype(BF16)
            return carry

        lax.fori_loop(0, tc // LANES, lane_chunk, 0)

    n_l = dwb.shape[0]
    return _pcall(
        body, grid=(n_f, S // tm),
        in_specs=[pl.BlockSpec((2, tm, tc), lambda j, i: (0, i, j)),
                  pl.BlockSpec((2, HALO_S, tc), lambda j, i: (0, jnp.maximum(i * nb - 1, 0), j)),
                  pl.BlockSpec((None, kf, tc), lambda j, i: (l, 0, j)),
                  pl.BlockSpec((None, kf, tc), lambda j, i: (l, 0, j + n_f)),
                  pl.BlockSpec((n_l, tc), lambda j, i: (0, j)),
                  pl.BlockSpec((n_l, tc), lambda j, i: (0, j + n_f))],
        out_specs=pl.BlockSpec((tm, tc), lambda j, i: (i, j)), out_shape=_sds((S, F), BF16),
        scratch_shapes=[pltpu.VMEM((2, HALO_S + tm, tc), F32)],
        compiler_params=_cp("parallel", "parallel"), name=name,
    )(u2, u2, dww, dww, dwb, dwb)


def _ffn_mid_bwd(name, u2, df, dww, dwb, l):
    _, S, F = u2.shape
    tm = _tile(S, 256, BF16_ROWS)
    tc = _tile(F, 1408)
    n_f = F // tc
    nb = tm // HALO_S
    n_i = S // tm
    kf = FFN_CONV_WIDTH
    n = tm + HALO_S

    def body(u_ref, up_ref, un_ref, df_ref, dfn_ref, wg_ref, wv_ref, bg_ref, bv_ref,
             du_ref, dw_ref, db_ref, uext, dcext):
        i = pl.program_id(1)
        first, last = i == 0, i == n_i - 1

        @pl.when(first)
        def _():
            dw_ref[...] = jnp.zeros_like(dw_ref)
            db_ref[...] = jnp.zeros_like(db_ref)

        uext[:, pl.ds(0, HALO_S), :] = jnp.where(first, 0.0, up_ref[...])
        uext[:, pl.ds(HALO_S, tm), :] = u_ref[...]
        uext[:, pl.ds(HALO_S + tm, HALO_S), :] = un_ref[...]
        rc = _tile(tm, ELT_ROWS, BF16_ROWS)

        def lane_chunk(ci, carry):
            lanes = pl.ds(pl.multiple_of(ci * LANES, LANES), LANES)
            taps = [[w_ref[k:k + 1, lanes] for k in range(kf)] for w_ref in (wg_ref, wv_ref)]
            bias = [b_ref[l:l + 1, lanes] for b_ref in (bg_ref, bv_ref)]
            acc_w = [[jnp.zeros((SUBLANES, LANES), F32) for _ in range(kf)] for _ in range(2)]
            acc_b = [jnp.zeros((SUBLANES, LANES), F32) for _ in range(2)]
            for r0, rows in [(r, rc) for r in range(0, tm, rc)] + [(tm, HALO_S)]:
                shifted = [[uext[g, pl.ds(HALO_S - (kf - 1) + k + r0, rows), lanes] for k in range(kf)] for g in range(2)]
                conv = []
                for g in range(2):
                    acc = bias[g]
                    for k in range(kf):
                        acc = acc + taps[g][k] * shifted[g][k]
                    conv.append(acc)
                cg, cv = conv
                s = _sig(cg)
                dfe = df_ref[pl.ds(r0, rows), lanes] if r0 < tm else jnp.where(last, 0.0, dfn_ref[:, lanes])
                dc = [dfe * cv * (s * (1.0 + cg * (1.0 - s))), dfe * (cg * s)]
                for g in range(2):
                    dcext[g, pl.ds(r0, rows), lanes] = dc[g]
                    if r0 < tm:
                        acc_b[g] = acc_b[g] + _fold(dc[g])
                        for k in range(kf):
                            acc_w[g][k] = acc_w[g][k] + _fold(dc[g] * shifted[g][k])
            for r0 in range(0, tm, rc):
                for g in range(2):
                    du = taps[g][0] * dcext[g, pl.ds(r0 + kf - 1, rc), lanes]
                    for k in range(1, kf):
                        du = du + taps[g][k] * dcext[g, pl.ds(r0 + kf - 1 - k, rc), lanes]
                    du_ref[g, pl.ds(r0, rc), lanes] = du.astype(BF16)
            for g in range(2):
                db_ref[g, :, lanes] += _rowsum(acc_b[g])
                for k in range(kf):
                    dw_ref[g, k:k + 1, lanes] += _rowsum(acc_w[g][k])
            return carry

        lax.fori_loop(0, tc // LANES, lane_chunk, 0)

    n_l = dwb.shape[0]
    prev = lambda j, i: (0, jnp.maximum(i * nb - 1, 0), j)
    nxt = lambda j, i: (0, jnp.minimum((i + 1) * nb, S // HALO_S - 1), j)
    return _pcall(
        body, grid=(n_f, n_i),
        in_specs=[pl.BlockSpec((2, tm, tc), lambda j, i: (0, i, j)),
                  pl.BlockSpec((2, HALO_S, tc), prev), pl.BlockSpec((2, HALO_S, tc), nxt),
                  pl.BlockSpec((tm, tc), lambda j, i: (i, j)),
                  pl.BlockSpec((HALO_S, tc), lambda j, i: nxt(j, i)[1:]),
                  pl.BlockSpec((None, kf, tc), lambda j, i: (l, 0, j)),
                  pl.BlockSpec((None, kf, tc), lambda j, i: (l, 0, j + n_f)),
                  pl.BlockSpec((n_l, tc), lambda j, i: (0, j)),
                  pl.BlockSpec((n_l, tc), lambda j, i: (0, j + n_f))],
        out_specs=[pl.BlockSpec((2, tm, tc), lambda j, i: (0, i, j)),
                   pl.BlockSpec((2, kf, tc), lambda j, i: (0, 0, j)),
                   pl.BlockSpec((2, 1, tc), lambda j, i: (0, 0, j))],
        out_shape=[_sds((2, S, F), BF16), _sds((2, kf, F), F32), _sds((2, 1, F), F32)],
        scratch_shapes=[pltpu.VMEM((2, HALO_S + n, tc), F32), pltpu.VMEM((2, n, tc), F32)],
        compiler_params=_cp("parallel", "arbitrary"), name=name,
    )(u2, u2, u2, df, df, dww, dww, dwb, dwb)


def _head_sum_matrix():
    r = lax.broadcasted_iota(jnp.int32, (LANES, LANES), 0) // HEAD_DIM
    c = lax.broadcasted_iota(jnp.int32, (LANES, LANES), 1) // HEAD_DIM
    return (r == c).astype(BF16)


def _head_mean(x, ones):
    return _split_dot(x, ones) * (1.0 / HEAD_DIM)


def _qknorm_fwd(name, qkv, g2):
    S, D3 = qkv.shape
    D = D3 // 3
    tm = _tile(S, 256, BF16_ROWS)
    scale = HEAD_DIM ** -0.5

    def body(q_ref, k_ref, v_ref, g_ref, qo_ref, ko_ref, vo_ref):
        ones = _head_sum_matrix()
        for cc in range(D // LANES):
            sl = slice(cc * LANES, (cc + 1) * LANES)
            for x_ref, o_ref, row, mult in ((q_ref, qo_ref, 0, scale), (k_ref, ko_ref, 1, 1.0)):
                x = x_ref[:, sl]
                r = lax.rsqrt(_head_mean(x * x, ones) + EPS)
                o_ref[:, sl] = ((x * r * g_ref[row:row + 1, :]).astype(BF16) * mult).astype(BF16)
        vo_ref[...] = v_ref[...].astype(BF16)

    col = lambda c: pl.BlockSpec((tm, D), lambda i: (i, c))
    out = pl.BlockSpec((tm, D), lambda i: (i, 0))
    return _pcall(
        body, grid=(S // tm,),
        in_specs=[col(0), col(1), col(2), pl.BlockSpec(g2.shape, lambda i: (0, 0))],
        out_specs=[out, out, out], out_shape=[_sds((S, D), BF16)] * 3,
        compiler_params=_cp("parallel"), name=name,
    )(qkv, qkv, qkv, g2)


def _qknorm_bwd(name, qkv, dq, dk, dv, g2):
    S, D3 = qkv.shape
    D = D3 // 3
    tm = _tile(S, 256, BF16_ROWS)
    scale = HEAD_DIM ** -0.5

    def body(q_ref, k_ref, dq_ref, dk_ref, dv_ref, g_ref, o_ref, dg_ref):
        @pl.when(pl.program_id(0) == 0)
        def _():
            dg_ref[...] = jnp.zeros_like(dg_ref)

        ones = _head_sum_matrix()
        for cc in range(D // LANES):
            sl = slice(cc * LANES, (cc + 1) * LANES)
            for x_ref, d_ref, row, mult, base in ((q_ref, dq_ref, 0, scale, 0), (k_ref, dk_ref, 1, 1.0, D)):
                x = x_ref[:, sl]
                r = lax.rsqrt(_head_mean(x * x, ones) + EPS)
                xh = x * r
                dn = d_ref[:, sl] * mult
                dxh = dn * g_ref[row:row + 1, :]
                dx = r * (dxh - xh * _head_mean(dxh * xh, ones))
                o_ref[:, base + cc * LANES:base + (cc + 1) * LANES] = dx.astype(BF16)
                dg_ref[row:row + 1, :] += _rowsum(dn * xh)
        o_ref[:, 2 * D:3 * D] = dv_ref[...].astype(BF16)

    col = lambda c: pl.BlockSpec((tm, D), lambda i: (i, c))
    row = pl.BlockSpec((tm, D), lambda i: (i, 0))
    return _pcall(
        body, grid=(S // tm,),
        in_specs=[col(0), col(1), row, row, row, pl.BlockSpec(g2.shape, lambda i: (0, 0))],
        out_specs=[pl.BlockSpec((tm, D3), lambda i: (i, 0)), pl.BlockSpec((2, LANES), lambda i: (0, 0))],
        out_shape=[_sds((S, D3), BF16), _sds((2, LANES), F32)],
        compiler_params=_cp("arbitrary"), name=name,
    )(qkv, qkv, dq, dk, dv, g2)


def _attn_consts():
    t = ATTN_BLOCK
    row = lax.broadcasted_iota(jnp.int32, (t, t), 0)
    col = lax.broadcasted_iota(jnp.int32, (t, t), 1)
    lane = lax.broadcasted_iota(jnp.int32, (1, LANES), 1)
    heads = (lane < HEAD_DIM, lane >= HEAD_DIM)
    return row, col, heads


def _split_dot(x, m):
    n = x.shape[0]
    hi = x.astype(BF16)
    lo = (x - hi.astype(F32)).astype(BF16)
    both = jnp.dot(jnp.concatenate([hi, lo], axis=0), m, preferred_element_type=F32)
    return both[:n] + both[n:]


def _log_keep(z):
    return -(jnp.maximum(z, 0.0) + jnp.log(1.0 + jnp.exp(-jnp.abs(z))))


def _stack_heads(a, heads):
    t = ATTN_BLOCK
    zero = jnp.zeros((t, LANES), a.dtype)
    return jnp.concatenate([jnp.where(h, a[s * t:(s + 1) * t], zero) for s in range(a.shape[0] // t) for h in heads], axis=0)


def _side_by_side(a):
    t = ATTN_BLOCK
    return jnp.concatenate([jnp.concatenate([a[2 * s * t:(2 * s + 1) * t], a[(2 * s + 1) * t:(2 * s + 2) * t]], axis=1)
                            for s in range(a.shape[0] // (2 * t))], axis=0)


def _grow(a, rows, cols):
    z = jnp.zeros((rows, cols), F32)
    return z if a is None else jnp.concatenate([z, a], axis=0)


def _attn_fwd(name, qs, kn, vb):
    S, D = qs.shape
    t = ATTN_BLOCK
    tq = ATTN_SUB * t

    def body(q_ref, k_ref, v_ref, o_ref):
        i = pl.program_id(1)
        row, col, heads = _attn_consts()
        after_m = (row > col).astype(BF16)
        causal = col < row
        q_all = _stack_heads(q_ref[...], heads)

        def block(j, q, r, acc, mask):
            off = pl.multiple_of(j * t, t)
            kb = k_ref[pl.ds(off, t), :]
            v2 = _stack_heads(v_ref[pl.ds(off, t), :], heads)
            z = lax.dot_general(q, kb, NT, preferred_element_type=F32)
            lk = _log_keep(z)
            if mask is not None:
                lk = jnp.where(mask, lk, 0.0)
            w = jnp.exp(z + lk + _split_dot(lk, after_m) + r)
            if mask is not None:
                w = jnp.where(mask, w, 0.0)
            acc = acc + jnp.dot(_side_by_side(w.astype(BF16)), v2, preferred_element_type=F32)
            return r + jnp.sum(lk, axis=1, keepdims=True), acc

        def head(n_more):
            r = acc = None
            for s in reversed(range(ATTN_SUB)):
                mask = jnp.concatenate([causal, causal] + [jnp.ones_like(causal)] * (2 * (ATTN_SUB - 1 - s)), axis=0)
                r, acc = block(ATTN_SUB * i + s, q_all[2 * s * t:], _grow(r, 2 * t, 1), _grow(acc, t, LANES), mask)
            for b in range(n_more):
                r, acc = block(ATTN_SUB * i - 1 - b, q_all, r, acc, None)
            return r, acc

        r, acc = lax.cond(ATTN_SUB * i >= ATTN_MORE, lambda: head(ATTN_MORE), lambda: head(0))

        def cond(c):
            return jnp.logical_and(c[0] >= 0, jnp.max(c[1]) > EXP_UNDERFLOW)

        def step(c):
            r, a = block(c[0], q_all, c[1], c[2], None)
            return c[0] - 1, r, a

        first = jnp.where(ATTN_SUB * i >= ATTN_MORE, ATTN_SUB * i - 1 - ATTN_MORE, ATTN_SUB * i - 1)
        o_ref[...] = lax.while_loop(cond, step, (first, r, acc))[2]

    n_hp = D // LANES
    blk = pl.BlockSpec((tq, LANES), lambda hp, i: (i, hp))
    seq = pl.BlockSpec((S, LANES), lambda hp, i: (0, hp))
    return _pcall(
        body, grid=(n_hp, S // tq), in_specs=[blk, seq, seq], out_specs=blk, out_shape=_sds((S, D), F32),
        compiler_params=_cp("parallel", "arbitrary"), name=name,
    )(qs, kn, vb)


def _attn_bwd(name, qs, kn, vb, o, do):
    S, D = qs.shape
    t = ATTN_BLOCK
    tq = ATTN_SUB * t

    def body(q_ref, k_ref, v_ref, o_ref, do_ref, dq_ref, dk_ref, dv_ref):
        i = pl.program_id(1)

        @pl.when(i == 0)
        def _():
            dk_ref[...] = jnp.zeros_like(dk_ref)
            dv_ref[...] = jnp.zeros_like(dv_ref)

        row, col, heads = _attn_consts()
        after_m = (row > col).astype(BF16)
        from_m = (row >= col).astype(BF16)
        causal = col < row
        q_all = _stack_heads(q_ref[...], heads)
        dob = do_ref[...].astype(BF16)
        do_all = _stack_heads(dob, heads)
        dsum_all = jnp.sum(_stack_heads(dob.astype(F32) * o_ref[...], heads), axis=1, keepdims=True)

        def block(j, q, dor, dsum, r, es, dq, mask):
            off = pl.multiple_of(j * t, t)
            kb = k_ref[pl.ds(off, t), :]
            vblk = v_ref[pl.ds(off, t), :]
            z = lax.dot_general(q, kb, NT, preferred_element_type=F32)
            lk = _log_keep(z)
            if mask is not None:
                lk = jnp.where(mask, lk, 0.0)
            ls = z + lk
            w = jnp.exp(ls + _split_dot(lk, after_m) + r)
            if mask is not None:
                w = jnp.where(mask, w, 0.0)
            e = w * lax.dot_general(dor, vblk, NT, preferred_element_type=F32)
            before = dsum - (es + _split_dot(e, from_m))
            dz = e - (e + before) * jnp.exp(ls)
            if mask is not None:
                dz = jnp.where(mask, dz, 0.0)
            dzb = dz.astype(BF16)
            dq = dq + jnp.dot(_side_by_side(dzb), _stack_heads(kb, heads), preferred_element_type=F32)
            dk_ref[pl.ds(off, t), :] += lax.dot_general(dzb, q, TN, preferred_element_type=F32)
            dv_ref[pl.ds(off, t), :] += lax.dot_general(w.astype(BF16), dor, TN, preferred_element_type=F32)
            return r + jnp.sum(lk, axis=1, keepdims=True), es + jnp.sum(e, axis=1, keepdims=True), dq

        def head(n_more):
            r = es = dq = None
            for s in reversed(range(ATTN_SUB)):
                mask = jnp.concatenate([causal, causal] + [jnp.ones_like(causal)] * (2 * (ATTN_SUB - 1 - s)), axis=0)
                lo = 2 * s * t
                r, es, dq = block(ATTN_SUB * i + s, q_all[lo:], do_all[lo:], dsum_all[lo:], _grow(r, 2 * t, 1),
                                  _grow(es, 2 * t, 1), _grow(dq, t, LANES), mask)
            for b in range(n_more):
                r, es, dq = block(ATTN_SUB * i - 1 - b, q_all, do_all, dsum_all, r, es, dq, None)
            return r, es, dq

        r, es, dq = lax.cond(ATTN_SUB * i >= ATTN_MORE, lambda: head(ATTN_MORE), lambda: head(0))

        def cond(c):
            return jnp.logical_and(c[0] >= 0, jnp.max(c[1]) > EXP_UNDERFLOW)

        def step(c):
            r, es, a = block(c[0], q_all, do_all, dsum_all, c[1], c[2], c[3], None)
            return c[0] - 1, r, es, a

        first = jnp.where(ATTN_SUB * i >= ATTN_MORE, ATTN_SUB * i - 1 - ATTN_MORE, ATTN_SUB * i - 1)
        dq_ref[...] = lax.while_loop(cond, step, (first, r, es, dq))[3]

    n_hp = D // LANES
    blk = pl.BlockSpec((tq, LANES), lambda hp, i: (i, hp))
    seq = pl.BlockSpec((S, LANES), lambda hp, i: (0, hp))
    return _pcall(
        body, grid=(n_hp, S // tq), in_specs=[blk, seq, seq, blk, blk], out_specs=[blk, seq, seq],
        out_shape=[_sds((S, D), F32)] * 3, compiler_params=_cp("parallel", "arbitrary"), name=name,
    )(qs, kn, vb, o, do)


def _adamw(name, w, g, m, v):
    L, R, C = w.shape
    tr = _tile(R, 256, SUBLANES)
    c1 = 1.0 - ADAM_B1 ** ADAM_STEP
    c2 = 1.0 - ADAM_B2 ** ADAM_STEP

    def body(w_ref, g_ref, m_ref, v_ref, d_ref, mo_ref, vo_ref):
        gg = g_ref[...]
        mn = ADAM_B1 * m_ref[...] + (1.0 - ADAM_B1) * gg
        vn = ADAM_B2 * v_ref[...] + (1.0 - ADAM_B2) * (gg * gg)
        d_ref[...] = -ADAM_LR * ((mn / c1) / (jnp.sqrt(vn / c2) + ADAM_EPS) + ADAM_WD * w_ref[...])
        mo_ref[...] = mn
        vo_ref[...] = vn

    blk = pl.BlockSpec((None, tr, C), lambda l, i: (l, i, 0))
    return _pcall(
        body, grid=(L, R // tr), in_specs=[blk] * 4, out_specs=[blk] * 3, out_shape=[_sds(w.shape, F32)] * 3,
        compiler_params=_cp("parallel", "parallel"), name=name,
    )(w, g, m, v)


def _place():
    x, y, c = lax.axis_index("x"), lax.axis_index("y"), lax.axis_index("c")
    chips = [(1 - x, y), (x, 1 - y), (1 - x, 1 - y)]
    return x, y, c, chips


def _place_shard(name, w, j_idx):
    L, R, X = w.shape
    rh = R // 2
    tr = _tile(rh, 256, BF16_ROWS)

    def body(j_ref, w_ref, o_ref):
        o_ref[...] = w_ref[...].astype(BF16)

    return _pcall(
        body,
        grid_spec=pltpu.PrefetchScalarGridSpec(
            num_scalar_prefetch=1, grid=(L, 2, rh // tr),
            in_specs=[pl.BlockSpec((None, None, tr, X), lambda l, h, i, j_ref: (l, h, i, 0))],
            out_specs=pl.BlockSpec((None, None, None, tr, X), lambda l, h, i, j_ref: (l, j_ref[0], h, i, 0))),
        out_shape=_sds((L, N_CHIPS, 2, rh, X), BF16), compiler_params=_cp("parallel", "parallel", "parallel"), name=name,
    )(j_idx, w.reshape(L, 2, rh, X))


def _all_gather_weights(bufs, small_ws):
    n_big, n_small = len(bufs), len(small_ws)
    n_in = n_big + n_small

    def body(*refs):
        ins, outs = refs[:n_in], refs[n_in:2 * n_in]
        send_sems, recv_sems, local_sems = refs[2 * n_in:]
        x, y, c, chips = _place()
        j_me = 2 * x + y
        j_of = [2 * cx + cy for cx, cy in chips]
        sibling = (x, y, 1 - c)

        def remote(src, dst, s, to):
            return pltpu.make_async_remote_copy(src_ref=src, dst_ref=dst, send_sem=send_sems.at[s], recv_sem=recv_sems.at[s],
                                                device_id=to, device_id_type=MESH)

        started = []
        for t in range(n_big, n_in):
            loc = pltpu.make_async_copy(ins[t], outs[t].at[:, j_me], local_sems.at[t - n_big])
            loc.start()
            started.append(loc)
        first = []
        for t in range(n_big):
            mine = outs[t].at[:, j_me, c]
            for k in range(3):
                first.append(remote(mine, mine, 6 * t + k, (*chips[k], c)))
        for t in range(n_big, n_in):
            for k in range(3):
                first.append(remote(ins[t], outs[t].at[:, j_me], 6 * n_big + 3 * (t - n_big) + k, (*chips[k], c)))
        for cp in first:
            cp.start()
        passed = []
        for t in range(n_big):
            for k in range(3):
                landed = outs[t].at[:, j_of[k], c]
                remote(landed, landed, 6 * t + k, (*chips[k], c)).wait_recv()
                fwd = remote(landed, landed, 6 * t + 3 + k, sibling)
                fwd.start()
                passed.append(fwd)
        for t in range(n_big):
            for k in range(3):
                other = outs[t].at[:, j_of[k], 1 - c]
                remote(other, other, 6 * t + 3 + k, sibling).wait_recv()
        for t in range(n_big, n_in):
            for k in range(3):
                dst = outs[t].at[:, j_of[k]]
                remote(dst, dst, 6 * n_big + 3 * (t - n_big) + k, (*chips[k], c)).wait_recv()
        for cp in first + passed:
            cp.wait_send()
        for loc in started:
            loc.wait()

    out_shape = [_sds(b.shape, b.dtype) for b in bufs]
    out_shape += [_sds((w.shape[0], N_CHIPS) + w.shape[1:], w.dtype) for w in small_ws]
    n_sem = 6 * n_big + 3 * n_small
    outs = _pcall(
        body, in_specs=[ANY] * n_in, out_specs=[ANY] * n_in, out_shape=out_shape,
        input_output_aliases={t: t for t in range(n_big)},
        scratch_shapes=[pltpu.SemaphoreType.DMA((n_sem,)), pltpu.SemaphoreType.DMA((n_sem,)), pltpu.SemaphoreType.DMA((n_small,))],
        name="all_gather_weights",
    )(*bufs, *small_ws)
    return outs[:n_big], outs[n_big:]


def _exchange_core_halves(grads):
    n = len(grads)

    def body(*refs):
        ins, outs = refs[:n], refs[n:2 * n]
        send_sems, recv_sems = refs[2 * n:]
        x, y, c, _ = _place()
        cps = [pltpu.make_async_remote_copy(src_ref=ins[t].at[:, :, 1 - c], dst_ref=outs[t], send_sem=send_sems.at[t],
                                            recv_sem=recv_sems.at[t], device_id=(x, y, 1 - c), device_id_type=MESH)
               for t in range(n)]
        for cp in cps:
            cp.start()
        for cp in cps:
            cp.wait()

    return _pcall(
        body, in_specs=[ANY] * n, out_specs=[ANY] * n,
        out_shape=[_sds((g.shape[0], g.shape[1], g.shape[3], g.shape[4]), F32) for g in grads],
        scratch_shapes=[pltpu.SemaphoreType.DMA((n,)), pltpu.SemaphoreType.DMA((n,))],
        name="grad_exchange_core_halves",
    )(*grads)


def _add_core_halves(name, g, a, c_idx):
    L, nj, _, rh, X = g.shape
    tr = _tile(rh, 256, BF16_ROWS)

    def body(c_ref, g_ref, a_ref, o_ref, ob_ref):
        s = g_ref[...] + a_ref[...]
        o_ref[...] = s
        ob_ref[...] = s.astype(BF16)

    blk = pl.BlockSpec((None, None, tr, X), lambda l, j, i, c_ref: (l, j, i, 0))
    return _pcall(
        body,
        grid_spec=pltpu.PrefetchScalarGridSpec(
            num_scalar_prefetch=1, grid=(L, nj, rh // tr),
            in_specs=[pl.BlockSpec((None, None, None, tr, X), lambda l, j, i, c_ref: (l, j, c_ref[0], i, 0)), blk],
            out_specs=[blk, blk]),
        out_shape=[_sds((L, nj, rh, X), F32), _sds((L, nj, rh, X), BF16)],
        compiler_params=_cp("parallel", "parallel", "parallel"), name=name,
    )(c_idx, g, a)


def _exchange_chip_shards(parts):
    n = len(parts)

    def body(*refs):
        ins, outs = refs[:n], refs[n:2 * n]
        send_sems, recv_sems = refs[2 * n:]
        x, y, c, chips = _place()
        cps = []
        for t in range(n):
            for k, (cx, cy) in enumerate(chips):
                cps.append(pltpu.make_async_remote_copy(
                    src_ref=ins[t].at[:, 2 * cx + cy], dst_ref=outs[t].at[k], send_sem=send_sems.at[3 * t + k],
                    recv_sem=recv_sems.at[3 * t + k], device_id=(cx, cy, c), device_id_type=MESH))
        for cp in cps:
            cp.start()
        for cp in cps:
            cp.wait()

    return _pcall(
        body, in_specs=[ANY] * n, out_specs=[ANY] * n,
        out_shape=[_sds((3, p.shape[0], p.shape[2], p.shape[3]), p.dtype) for p in parts],
        scratch_shapes=[pltpu.SemaphoreType.DMA((3 * n,)), pltpu.SemaphoreType.DMA((3 * n,))],
        name="grad_exchange_chip_shards",
    )(*parts)


def _add_chip_shards(name, p, b, jc_idx):
    L, _, rh, X = p.shape
    tr = _tile(rh, 256, BF16_ROWS)

    def body(jc_ref, p_ref, b_ref, o_ref):
        o_ref[...] = ((p_ref[...] + b_ref[0].astype(F32)) + b_ref[1].astype(F32)) + b_ref[2].astype(F32)

    return _pcall(
        body,
        grid_spec=pltpu.PrefetchScalarGridSpec(
            num_scalar_prefetch=1, grid=(L, rh // tr),
            in_specs=[pl.BlockSpec((None, None, tr, X), lambda l, i, jc: (l, jc[0], i, 0)),
                      pl.BlockSpec((3, None, tr, X), lambda l, i, jc: (0, l, i, 0))],
            out_specs=pl.BlockSpec((None, None, tr, X), lambda l, i, jc: (l, jc[1], i, 0))),
        out_shape=_sds((L, 2, rh, X), F32), compiler_params=_cp("parallel", "parallel"), name=name,
    )(jc_idx, p, b)


def _join_core_halves(bufs):
    n = len(bufs)

    def body(*refs):
        outs = refs[n:2 * n]
        send_sems, recv_sems = refs[2 * n:]
        x, y, c, _ = _place()
        cps = [pltpu.make_async_remote_copy(src_ref=outs[t].at[:, c], dst_ref=outs[t].at[:, c], send_sem=send_sems.at[t],
                                            recv_sem=recv_sems.at[t], device_id=(x, y, 1 - c), device_id_type=MESH)
               for t in range(n)]
        for cp in cps:
            cp.start()
        for t in range(n):
            pltpu.make_async_remote_copy(src_ref=outs[t].at[:, c], dst_ref=outs[t].at[:, 1 - c], send_sem=send_sems.at[t],
                                         recv_sem=recv_sems.at[t], device_id=(x, y, 1 - c), device_id_type=MESH).wait()

    outs = _pcall(
        body, in_specs=[ANY] * n, out_specs=[ANY] * n, out_shape=[_sds(b.shape, F32) for b in bufs],
        input_output_aliases={t: t for t in range(n)},
        scratch_shapes=[pltpu.SemaphoreType.DMA((n,)), pltpu.SemaphoreType.DMA((n,))],
        name="grad_join_core_halves",
    )(*bufs)
    return [o.reshape(o.shape[0], 2 * o.shape[2], o.shape[3]) for o in outs]


def _all_reduce_small(packed):
    R, C = packed.shape

    def body(x_ref, o_ref, slots, send_sems, recv_sems):
        x, y, c, _ = _place()
        me = 4 * x + 2 * y + c
        slots[me] = x_ref[...]
        cps = []
        for d in range(N_DEV):
            to = (d // 4, (d // 2) % 2, d % 2)
            cp = pltpu.make_async_remote_copy(src_ref=x_ref, dst_ref=slots.at[me], send_sem=send_sems.at[d],
                                              recv_sem=recv_sems.at[me], device_id=to, device_id_type=MESH)
            cps.append(cp)

            @pl.when(d != me)
            def _():
                cp.start()

        for d in range(N_DEV):
            @pl.when(d != me)
            def _():
                pltpu.make_async_remote_copy(src_ref=x_ref, dst_ref=slots.at[d], send_sem=send_sems.at[d],
                                             recv_sem=recv_sems.at[d], device_id=(x, y, c), device_id_type=MESH).wait_recv()
                cps[d].wait_send()

        acc = slots[0]
        for d in range(1, N_DEV):
            acc = acc + slots[d]
        o_ref[...] = acc

    vm = pl.BlockSpec(memory_space=pltpu.VMEM)
    return _pcall(
        body, in_specs=[vm], out_specs=vm, out_shape=_sds((R, C), F32),
        scratch_shapes=[pltpu.VMEM((N_DEV, R, C), F32), pltpu.SemaphoreType.DMA((N_DEV,)), pltpu.SemaphoreType.DMA((N_DEV,))],
        compiler_params=pltpu.CompilerParams(vmem_limit_bytes=VMEM_LIMIT_BYTES), name="all_reduce_small",
    )(packed)


PACK = SUBLANES * LANES


def _pack(arrays):
    flat = []
    for a in arrays:
        v = a.reshape(-1)
        flat.append(jnp.pad(v, (0, (-v.shape[0]) % PACK)))
    return jnp.concatenate(flat).reshape(-1, LANES)


def _unpack(packed, shapes):
    flat = packed.reshape(-1)
    out, pos = [], 0
    for s in shapes:
        n = 1
        for d in s:
            n *= d
        out.append(flat[pos:pos + n].reshape(s))
        pos += n + (-n) % PACK
    return out


def kernel(x, mix_norm_g, ffn_norm_g, conv_w_in, conv_a_dw_w, conv_a_dw_b, conv_a_ln_g, conv_a_ln_b, conv_b_dw_w, conv_w_out, attn_w_qkv, attn_q_g, attn_k_g, attn_w_o, ffn_w_up, ffn_dw_w, ffn_dw_b, ffn_w_down, loss_target, m_mix_norm_g, m_ffn_norm_g, m_conv_w_in, m_conv_a_dw_w, m_conv_a_dw_b, m_conv_a_ln_g, m_conv_a_ln_b, m_conv_b_dw_w, m_conv_w_out, m_attn_w_qkv, m_attn_q_g, m_attn_k_g, m_attn_w_o, m_ffn_w_up, m_ffn_dw_w, m_ffn_dw_b, m_ffn_w_down, v_mix_norm_g, v_ffn_norm_g, v_conv_w_in, v_conv_a_dw_w, v_conv_a_dw_b, v_conv_a_ln_g, v_conv_a_ln_b, v_conv_b_dw_w, v_conv_w_out, v_attn_w_qkv, v_attn_q_g, v_attn_k_g, v_attn_w_o, v_ffn_w_up, v_ffn_dw_w, v_ffn_dw_b, v_ffn_w_down):
    depth = mix_norm_g.shape[0]
    n_even, n_odd = conv_w_in.shape[0], attn_w_qkv.shape[0]
    S, D = x.shape[1], x.shape[2]
    dg = D // 2
    x0 = x.reshape(S, D)
    target = loss_target.reshape(S, D)
    j_me = 2 * lax.axis_index("x") + lax.axis_index("y")
    c_me = lax.axis_index("c")
    j_idx = j_me.astype(jnp.int32).reshape(1)
    c_idx = c_me.astype(jnp.int32).reshape(1)

    col_names = ["conv_w_in", "attn_w_qkv", "ffn_w_up"]
    row_names = ["conv_w_out", "attn_w_o", "ffn_w_down"]
    local = dict(conv_w_in=conv_w_in, attn_w_qkv=attn_w_qkv, ffn_w_up=ffn_w_up, conv_w_out=conv_w_out, attn_w_o=attn_w_o,
                 ffn_w_down=ffn_w_down)
    gathered, (a_dw, b_dw, f_dw) = _all_gather_weights(
        [_place_shard(f"place_{n}", local[n], j_idx) for n in col_names + row_names], [conv_a_dw_w, conv_b_dw_w, ffn_dw_w])
    w_in, w_qkv, w_up = (g.reshape(g.shape[0], N_CHIPS, -1, g.shape[4]) for g in gathered[:3])
    w_out, w_o, w_down = (g.reshape(g.shape[0], -1, g.shape[4]) for g in gathered[3:])
    unshard = lambda a: jnp.moveaxis(a, 1, 2).reshape(a.shape[0], a.shape[2], N_CHIPS * a.shape[3])
    a_dw, b_dw, f_dw = unshard(a_dw), unshard(b_dw), unshard(f_dw)
    qk_gain = [jnp.stack([jnp.tile(attn_q_g[i], LANES // HEAD_DIM), jnp.tile(attn_k_g[i], LANES // HEAD_DIM)])
               for i in range(n_odd)]

    saved = []
    xc = x0
    for layer in range(depth):
        i = layer // 2
        tag = f"l{layer}"
        s = {"x_in": xc}
        h = _rms_fwd(f"rms_mix_fwd_{tag}", xc, mix_norm_g, layer)
        s["h"] = h
        if layer % 2 == 0:
            p = _mm_fwd(f"conv_in_fwd_{tag}", h, w_in, i, colshard=True)
            ab = _convmix_fwd(f"convmix_fwd_{tag}", p, a_dw, conv_a_dw_b, conv_a_ln_g, conv_a_ln_b, b_dw, i)
            xm = _mm_fwd(f"conv_out_fwd_{tag}", ab, w_out, i, colshard=False, res=xc)
            s.update(p=p, ab=ab)
        else:
            qkv = _mm_fwd(f"attn_qkv_fwd_{tag}", h, w_qkv, i, colshard=True)
            qs, kn, vb = _qknorm_fwd(f"qknorm_fwd_{tag}", qkv, qk_gain[i])
            o = _attn_fwd(f"attn_fwd_{tag}", qs, kn, vb)
            xm = _mm_fwd(f"attn_out_fwd_{tag}", o, w_o, i, colshard=False, res=xc)
            s.update(qkv=qkv, qs=qs, kn=kn, vb=vb, o=o)
        s["x_mid"] = xm
        h2 = _rms_fwd(f"rms_ffn_fwd_{tag}", xm, ffn_norm_g, layer)
        u2 = _mm_fwd(f"ffn_up_fwd_{tag}", h2, w_up, layer, colshard=True, out_split=2)
        f = _ffn_mid_fwd(f"ffn_mid_fwd_{tag}", u2, f_dw, ffn_dw_b, layer)
        xc = _mm_fwd(f"ffn_down_fwd_{tag}", f, w_down, layer, colshard=False, res=xm)
        s.update(h2=h2, u2=u2, f=f)
        saved.append(s)

    dx, loss_tile = _loss_fwd_bwd("loss", xc, target)

    g_up = g_down = g_in = g_out = g_qkv = g_o = None
    d_mix_g, d_ffn_g = [None] * depth, [None] * depth
    d_ffn_dw_w, d_ffn_dw_b = [None] * depth, [None] * depth
    d_a_dw_w, d_a_dw_b, d_a_ln_g, d_a_ln_b, d_b_dw_w = ([None] * n_even for _ in range(5))
    d_q_g, d_k_g = [None] * n_odd, [None] * n_odd
    for layer in reversed(range(depth)):
        i = layer // 2
        tag = f"l{layer}"
        s = saved[layer]
        df = _mm_dgrad(f"ffn_down_dgrad_{tag}", dx, w_down, layer, colshard=False)
        g_down = _mm_wgrad(f"ffn_down_wgrad_{tag}", s["f"], dx, layer, depth, g_down, colshard=False)
        du2, dww, dwb = _ffn_mid_bwd(f"ffn_mid_bwd_{tag}", s["u2"], df, f_dw, ffn_dw_b, layer)
        d_ffn_dw_w[layer] = jnp.moveaxis(dww, 0, 1).reshape(FFN_CONV_WIDTH, -1)
        d_ffn_dw_b[layer] = dwb.reshape(-1)
        dh2 = _mm_dgrad(f"ffn_up_dgrad_{tag}", du2, w_up, layer, colshard=True)
        g_up = _mm_wgrad(f"ffn_up_wgrad_{tag}", s["h2"], du2, layer, depth, g_up, colshard=True)
        dx, dg_ = _rms_bwd(f"rms_ffn_bwd_{tag}", s["x_mid"], ffn_norm_g, layer, dh2, dx)
        d_ffn_g[layer] = dg_.reshape(-1)
        if layer % 2 == 0:
            dab = _mm_dgrad(f"conv_out_dgrad_{tag}", dx, w_out, i, colshard=False)
            g_out = _mm_wgrad(f"conv_out_wgrad_{tag}", s["ab"], dx, i, n_even, g_out, colshard=False)
            dp, daw, dab_b, dlg, dlb, dbw = _convmix_bwd(f"convmix_bwd_{tag}", s["p"], dab, a_dw, conv_a_dw_b, conv_a_ln_g,
                                                         conv_a_ln_b, b_dw, i)
            d_a_dw_w[i], d_a_dw_b[i], d_a_ln_g[i], d_a_ln_b[i], d_b_dw_w[i] = (
                daw, dab_b.reshape(-1), dlg.reshape(-1), dlb.reshape(-1), dbw)
            dh = _mm_dgrad(f"conv_in_dgrad_{tag}", dp, w_in, i, colshard=True)
            g_in = _mm_wgrad(f"conv_in_wgrad_{tag}", s["h"], dp, i, n_even, g_in, colshard=True)
        else:
            do = _mm_dgrad(f"attn_out_dgrad_{tag}", dx, w_o, i, colshard=False)
            g_o = _mm_wgrad(f"attn_out_wgrad_{tag}", s["o"], dx, i, n_odd, g_o, colshard=False)
            dq, dk, dv = _attn_bwd(f"attn_bwd_{tag}", s["qs"], s["kn"], s["vb"], s["o"], do)
            dqkv, dgain = _qknorm_bwd(f"qknorm_bwd_{tag}", s["qkv"], dq, dk, dv, qk_gain[i])
            d_q_g[i] = dgain[0, :HEAD_DIM] + dgain[0, HEAD_DIM:]
            d_k_g[i] = dgain[1, :HEAD_DIM] + dgain[1, HEAD_DIM:]
            dh = _mm_dgrad(f"attn_qkv_dgrad_{tag}", dqkv, w_qkv, i, colshard=True)
            g_qkv = _mm_wgrad(f"attn_qkv_wgrad_{tag}", s["h"], dqkv, i, n_odd, g_qkv, colshard=True)
        dx, dg_ = _rms_bwd(f"rms_mix_bwd_{tag}", s["x_in"], mix_norm_g, layer, dh, dx)
        d_mix_g[layer] = dg_.reshape(-1)
    grad_x = dx.reshape(1, S, D)

    small = {
        "mix_norm_g": jnp.stack(d_mix_g), "ffn_norm_g": jnp.stack(d_ffn_g),
        "conv_a_dw_w": jnp.stack(d_a_dw_w), "conv_a_dw_b": jnp.stack(d_a_dw_b),
        "conv_a_ln_g": jnp.stack(d_a_ln_g), "conv_a_ln_b": jnp.stack(d_a_ln_b),
        "conv_b_dw_w": jnp.stack(d_b_dw_w), "attn_q_g": jnp.stack(d_q_g), "attn_k_g": jnp.stack(d_k_g),
        "ffn_dw_w": jnp.stack(d_ffn_dw_w), "ffn_dw_b": jnp.stack(d_ffn_dw_b),
    }
    small_names = list(small)
    summed = _all_reduce_small(_pack([loss_tile] + [small[n] for n in small_names]))
    parts = _unpack(summed, [loss_tile.shape] + [small[n].shape for n in small_names])
    loss = parts[0][0, 0]
    small_g = dict(zip(small_names, parts[1:]))
    for n in ("conv_a_dw_w", "conv_b_dw_w", "ffn_dw_w"):
        cs = small_g[n].shape[2] // N_CHIPS
        small_g[n] = lax.dynamic_slice_in_dim(small_g[n], j_me * cs, cs, axis=2)

    big = {"conv_w_in": g_in, "attn_w_qkv": g_qkv, "ffn_w_up": g_up, "conv_w_out": g_out, "attn_w_o": g_o, "ffn_w_down": g_down}
    big_names = col_names + row_names
    five = []
    for n in big_names:
        g = big[n]
        if n in col_names:
            five.append(g.reshape(g.shape[0], N_CHIPS, 2, g.shape[2] // 2, g.shape[3]))
        else:
            five.append(g.reshape(g.shape[0], N_CHIPS, 2, g.shape[1] // (2 * N_CHIPS), g.shape[2]))
    from_sibling = _exchange_core_halves(five)
    both = [_add_core_halves(f"grad_add_core_{n}", g, a, c_idx) for n, g, a in zip(big_names, five, from_sibling)]
    chip_sums = [b[0] for b in both]
    from_chips = _exchange_chip_shards([b[1] for b in both])
    jc_idx = jnp.concatenate([j_idx, c_idx])
    totals = [_add_chip_shards(f"grad_add_chips_{n}", p, b, jc_idx) for n, p, b in zip(big_names, chip_sums, from_chips)]
    big_g = dict(zip(big_names, _join_core_halves(totals)))

    weights = dict(mix_norm_g=mix_norm_g, ffn_norm_g=ffn_norm_g, conv_w_in=conv_w_in, conv_a_dw_w=conv_a_dw_w, conv_a_dw_b=conv_a_dw_b, conv_a_ln_g=conv_a_ln_g, conv_a_ln_b=conv_a_ln_b, conv_b_dw_w=conv_b_dw_w, conv_w_out=conv_w_out, attn_w_qkv=attn_w_qkv, attn_q_g=attn_q_g, attn_k_g=attn_k_g, attn_w_o=attn_w_o, ffn_w_up=ffn_w_up, ffn_dw_w=ffn_dw_w, ffn_dw_b=ffn_dw_b, ffn_w_down=ffn_w_down)
    m_in = dict(mix_norm_g=m_mix_norm_g, ffn_norm_g=m_ffn_norm_g, conv_w_in=m_conv_w_in, conv_a_dw_w=m_conv_a_dw_w, conv_a_dw_b=m_conv_a_dw_b, conv_a_ln_g=m_conv_a_ln_g, conv_a_ln_b=m_conv_a_ln_b, conv_b_dw_w=m_conv_b_dw_w, conv_w_out=m_conv_w_out, attn_w_qkv=m_attn_w_qkv, attn_q_g=m_attn_q_g, attn_k_g=m_attn_k_g, attn_w_o=m_attn_w_o, ffn_w_up=m_ffn_w_up, ffn_dw_w=m_ffn_dw_w, ffn_dw_b=m_ffn_dw_b, ffn_w_down=m_ffn_w_down)
    v_in = dict(mix_norm_g=v_mix_norm_g, ffn_norm_g=v_ffn_norm_g, conv_w_in=v_conv_w_in, conv_a_dw_w=v_conv_a_dw_w, conv_a_dw_b=v_conv_a_dw_b, conv_a_ln_g=v_conv_a_ln_g, conv_a_ln_b=v_conv_a_ln_b, conv_b_dw_w=v_conv_b_dw_w, conv_w_out=v_conv_w_out, attn_w_qkv=v_attn_w_qkv, attn_q_g=v_attn_q_g, attn_k_g=v_attn_k_g, attn_w_o=v_attn_w_o, ffn_w_up=v_ffn_w_up, ffn_dw_w=v_ffn_dw_w, ffn_dw_b=v_ffn_dw_b, ffn_w_down=v_ffn_w_down)
    order = list(weights)
    grads, delta, new_m, new_v = {}, {}, {}, {}
    for n in big_names:
        grads[n] = big_g[n]
        delta[n], new_m[n], new_v[n] = _adamw(f"adamw_{n}", weights[n], big_g[n], m_in[n], v_in[n])
    shapes = [weights[n].shape for n in small_names]
    packed = [_pack([d[n] for n in small_names]) for d in (weights, small_g, m_in, v_in)]
    upd = _adamw("adamw_small", *[p[None] for p in packed])
    for out, res in zip((delta, new_m, new_v), upd):
        out.update(zip(small_names, _unpack(res[0], shapes)))
    grads.update({n: small_g[n].reshape(weights[n].shape) for n in small_names})
    return (loss, grad_x, *[grads[n] for n in order], *[delta[n] for n in order], *[new_m[n] for n in order],
            *[new_v[n] for n in order])
```

```python
import jax
import jax.numpy as jnp
from jax import lax
from jax.experimental import pallas as pl
from jax.experimental.pallas import tpu as pltpu

F32 = jnp.float32
BF16 = jnp.bfloat16
EPS = 1e-6
CONV_A_WIDTH = 31
CONV_B_WIDTH = 3
FFN_CONV_WIDTH = 3
HEAD_DIM = 64
ADAM_LR = 0.001
ADAM_B1 = 0.9
ADAM_B2 = 0.999
ADAM_EPS = 1e-08
ADAM_WD = 0.01
ADAM_STEP = 10

LANES = 128
SUBLANES = 8
BF16_ROWS = 16
V7X_VMEM_BYTES = 64 * 1024 * 1024
VMEM_LIMIT_BYTES = V7X_VMEM_BYTES * 3 // 4
MM_VMEM_BUDGET = VMEM_LIMIT_BYTES * 4 // 5
MM_ROWS = 1024
N_CHIPS = 4
N_DEV = 8
HALO_A = 32
HALO_S = 8
ELT_ROWS = 64
ATTN_BLOCK = 128
ATTN_SUB = 2
ATTN_MORE = 2
EXP_UNDERFLOW = -104.0
MESH = pl.DeviceIdType.MESH
ANY = pl.BlockSpec(memory_space=pl.ANY)
NT = (((1,), (1,)), ((), ()))
NN = (((1,), (0,)), ((), ()))
TN = (((0,), (0,)), ((), ()))


def _pcall(body, **kw):
    return pl.pallas_call(body, **kw)


def _cp(*sem):
    return pltpu.CompilerParams(dimension_semantics=sem, vmem_limit_bytes=VMEM_LIMIT_BYTES)


def _sds(shape, dtype):
    return jax.ShapeDtypeStruct(tuple(shape), dtype)


def _tile(n, cap, align=LANES):
    if n <= cap:
        return n
    for t in range(cap - cap % align, 0, -align):
        if n % t == 0:
            return t
    return n


def _sig(x):
    return 0.5 * jnp.tanh(0.5 * x) + 0.5


def _rowsum(x):
    return jnp.sum(x, axis=0, keepdims=True)


def _fold(x):
    acc = x[0:SUBLANES]
    for r in range(SUBLANES, x.shape[0], SUBLANES):
        acc = acc + x[r:r + SUBLANES]
    return acc


def _with_exchange(ex, body, in_specs, out_specs, out_shape, scratch, operands, first, last):
    if ex is None:
        return body, in_specs, out_specs, out_shape, scratch, operands
    n_in, n_out, n_scr = len(in_specs), len(out_specs), len(scratch)
    e_in, e_out = len(ex.operands), len(ex.out_shapes)

    def hosted(*refs):
        refs = list(refs)
        ins, refs = refs[:n_in], refs[n_in:]
        e_ins, refs = refs[:e_in], refs[e_in:]
        outs, refs = refs[:n_out], refs[n_out:]
        e_outs, refs = refs[:e_out], refs[e_out:]
        scr, sems = refs[:n_scr], refs[n_scr:]

        @pl.when(first())
        def _():
            ex.start(e_ins, e_outs, sems)

        body(*ins, *outs, *scr)

        @pl.when(last())
        def _():
            ex.wait(e_ins, e_outs, sems)

    return (hosted, in_specs + [ANY] * e_in, out_specs + [ANY] * e_out, out_shape + ex.out_shapes, scratch + ex.scratch,
            operands + ex.operands)


def _mm_call(name, dn, operands, in_specs, out_shape, out_spec, grid, nk, acc_shape, has_res, has_alias):
    def body(*refs):
        a_ref, b_ref = refs[0], refs[1]
        pos = 2
        res_ref = refs[pos] if has_res else None
        pos += int(has_res) + int(has_alias)
        o_ref = refs[pos]
        acc_ref = refs[pos + 1] if nk > 1 else None
        p = lax.dot_general(a_ref[...].astype(BF16), b_ref[...].astype(BF16), dn, preferred_element_type=F32)

        def finish(v):
            if has_res:
                v = v + res_ref[...]
            o_ref[...] = v.astype(o_ref.dtype)

        if nk == 1:
            finish(p)
        else:
            k = pl.program_id(2)

            @pl.when(k == 0)
            def _():
                acc_ref[...] = p

            @pl.when(k > 0)
            def _():
                acc_ref[...] += p

            @pl.when(k == nk - 1)
            def _():
                finish(acc_ref[...])

    aliases = {len(operands) - 1: 0} if has_alias else {}
    return _pcall(
        body, grid=grid, in_specs=in_specs, out_specs=out_spec, out_shape=out_shape,
        scratch_shapes=[pltpu.VMEM(acc_shape, F32)] if nk > 1 else [],
        input_output_aliases=aliases, compiler_params=_cp("parallel", "parallel", "arbitrary"), name=name,
    )(*operands)


def _mm_fwd(name, a, w, l, *, colshard, res=None, out_split=1):
    M, K = a.shape
    tm = _tile(M, MM_ROWS, BF16_ROWS)
    if colshard:
        cs = w.shape[3]
        N, tn, tk = N_CHIPS * cs, cs, K
        b_spec = pl.BlockSpec((None, None, tk, tn), lambda j, i, k: (l, j, k, 0))
    else:
        N = w.shape[2]
        tn, tk = _tile(N, 1024), _tile(K, 1536)
        b_spec = pl.BlockSpec((None, tk, tn), lambda j, i, k: (l, k, j))
    nk = K // tk
    in_specs = [pl.BlockSpec((tm, tk), lambda j, i, k: (i, k)), b_spec]
    operands = [a, w]
    if res is not None:
        in_specs.append(pl.BlockSpec((tm, tn), lambda j, i, k: (i, j)))
        operands.append(res)
    if out_split == 1:
        out_shape = _sds((M, N), F32)
        out_spec = pl.BlockSpec((tm, tn), lambda j, i, k: (i, j))
    else:
        per = N // tn // out_split
        out_shape = _sds((out_split, M, N // out_split), F32)
        out_spec = pl.BlockSpec((None, tm, tn), lambda j, i, k: (j // per, i, j % per))
    return _mm_call(name, NN, operands, in_specs, out_shape, out_spec, (N // tn, M // tm, nk), nk, (tm, tn),
                    res is not None, False)


def _mm_dgrad(name, g, w, l, *, colshard):
    split = g.ndim == 3
    M = g.shape[-2]
    tm = _tile(M, MM_ROWS, BF16_ROWS)
    if colshard:
        kw, cs = w.shape[2], w.shape[3]
        tn, tk, nk = _tile(kw, 1408), cs, N_CHIPS
        b_spec = pl.BlockSpec((None, None, tn, tk), lambda j, i, k: (l, k, j, 0))
    else:
        kw, ncon = w.shape[1], w.shape[2]
        tn, tk = _tile(kw, 1408), _tile(ncon, 1536)
        nk = ncon // tk
        b_spec = pl.BlockSpec((None, tn, tk), lambda j, i, k: (l, j, k))
    if split:
        per = nk // g.shape[0]
        a_spec = pl.BlockSpec((None, tm, tk), lambda j, i, k: (k // per, i, k % per))
    else:
        a_spec = pl.BlockSpec((tm, tk), lambda j, i, k: (i, k))
    out_shape = _sds((M, kw), F32)
    out_spec = pl.BlockSpec((tm, tn), lambda j, i, k: (i, j))
    return _mm_call(name, NT, [g, w], [a_spec, b_spec], out_shape, out_spec, (kw // tn, M // tm, nk), nk, (tm, tn),
                    False, False)


def _mm_wgrad(name, a, g, l, n_layers, buf, *, colshard):
    S, M = a.shape
    split = g.ndim == 3
    N = g.shape[-1] * (g.shape[0] if split else 1)
    tm = _tile(M, 1408)
    tn = N // N_CHIPS if colshard else _tile(N, 1024)
    per_row = 2 * (tm * a.dtype.itemsize + tn * g.dtype.itemsize)
    tk = _tile(S, max(BF16_ROWS, min(2048, (MM_VMEM_BUDGET - 3 * tm * tn * 4) // per_row)), BF16_ROWS)
    nk = S // tk
    if colshard:
        out_shape = _sds((n_layers, N_CHIPS, M, tn), F32)
        out_spec = pl.BlockSpec((None, None, tm, tn), lambda j, i, k: (l, j, i, 0))
    else:
        out_shape = _sds((n_layers, M, N), F32)
        out_spec = pl.BlockSpec((None, tm, tn), lambda j, i, k: (l, i, j))
    if split:
        per = N // tn // g.shape[0]
        b_spec = pl.BlockSpec((None, tk, tn), lambda j, i, k: (j // per, k, j % per))
    else:
        b_spec = pl.BlockSpec((tk, tn), lambda j, i, k: (k, j))
    in_specs = [pl.BlockSpec((tk, tm), lambda j, i, k: (k, i)), b_spec]
    operands = [a, g]
    if buf is not None:
        in_specs.append(ANY)
        operands.append(buf)
    return _mm_call(name, TN, operands, in_specs, out_shape, out_spec, (N // tn, M // tm, nk), nk, (tm, tn),
                    False, buf is not None)


def _rms_fwd(name, x, g, l):
    S, D = x.shape
    tm = _tile(S, 512, BF16_ROWS)

    def body(x_ref, g_ref, o_ref):
        xf = x_ref[...]
        r = lax.rsqrt(jnp.mean(xf * xf, axis=-1, keepdims=True) + EPS)
        o_ref[...] = (xf * r * g_ref[l:l + 1, :]).astype(BF16)

    return _pcall(
        body, grid=(S // tm,),
        in_specs=[pl.BlockSpec((tm, D), lambda i: (i, 0)), pl.BlockSpec(g.shape, lambda i: (0, 0))],
        out_specs=pl.BlockSpec((tm, D), lambda i: (i, 0)), out_shape=_sds((S, D), BF16),
        compiler_params=_cp("parallel"), name=name,
    )(x, g)


def _rms_bwd(name, x, g, l, dh, dres):
    S, D = x.shape
    tm = _tile(S, 512, SUBLANES)

    def body(x_ref, g_ref, dh_ref, dr_ref, dx_ref, dg_ref):
        xf = x_ref[...]
        r = lax.rsqrt(jnp.mean(xf * xf, axis=-1, keepdims=True) + EPS)
        xh = xf * r
        d = dh_ref[...]
        dxh = d * g_ref[l:l + 1, :]
        dx_ref[...] = dr_ref[...] + r * (dxh - xh * jnp.mean(dxh * xh, axis=-1, keepdims=True))

        @pl.when(pl.program_id(0) == 0)
        def _():
            dg_ref[...] = jnp.zeros_like(dg_ref)

        dg_ref[...] += _rowsum(d * xh)

    row = pl.BlockSpec((tm, D), lambda i: (i, 0))
    return _pcall(
        body, grid=(S // tm,),
        in_specs=[row, pl.BlockSpec(g.shape, lambda i: (0, 0)), row, row],
        out_specs=[row, pl.BlockSpec((1, D), lambda i: (0, 0))],
        out_shape=[_sds((S, D), F32), _sds((1, D), F32)],
        compiler_params=_cp("arbitrary"), name=name,
    )(x, g, dh, dres)


def _loss_fwd_bwd(name, y, t):
    S, D = y.shape
    tm = _tile(S, 512, SUBLANES)

    def body(y_ref, t_ref, dy_ref, l_ref):
        e = y_ref[...] - t_ref[...]
        dy_ref[...] = e * (1.0 / D)

        @pl.when(pl.program_id(0) == 0)
        def _():
            l_ref[...] = jnp.zeros_like(l_ref)

        l_ref[...] += 0.5 * jnp.sum(jnp.sum(e * e, axis=-1, keepdims=True) * (1.0 / D), axis=0, keepdims=True)

    row = pl.BlockSpec((tm, D), lambda i: (i, 0))
    return _pcall(
        body, grid=(S // tm,), in_specs=[row, row],
        out_specs=[row, pl.BlockSpec((SUBLANES, LANES), lambda i: (0, 0))],
        out_shape=[_sds((S, D), F32), _sds((SUBLANES, LANES), F32)],
        compiler_params=_cp("arbitrary"), name=name,
    )(y, t)


def _delayed_copies(us, n_rows):
    for s in range(1, SUBLANES):
        us[s, pl.ds(SUBLANES, n_rows - SUBLANES), :] = us[0, pl.ds(SUBLANES - s, n_rows - SUBLANES), :]


def _conv_a(aw_ref, ab_ref, l, us, row0, rows, dg):
    ka = CONV_A_WIDTH
    out = []
    for c0 in range(0, dg, LANES):
        lanes = slice(c0, c0 + LANES)
        acc = ab_ref[l:l + 1, lanes]
        for d in range(ka):
            a, s = divmod(d, SUBLANES)
            acc = acc + aw_ref[l, ka - 1 - d:ka - d, lanes] * us[s, pl.ds(row0 - SUBLANES * a, rows), lanes]
        out.append(acc)
    return jnp.concatenate(out, axis=1)


def _convmix_fwd(name, p, aw, ab, lg, lb, bw, l):
    S, W = p.shape
    dg = W // 5
    tm = _tile(S, 256, HALO_A)
    nb = tm // HALO_A
    ka, kb = CONV_A_WIDTH, CONV_B_WIDTH

    ext = HALO_A + tm
    rc = _tile(tm, ELT_ROWS, BF16_ROWS)

    def body(p_ref, ph_ref, aw_ref, ab_ref, lg_ref, lb_ref, bw_ref, o_ref, us, mext):
        first = pl.program_id(0) == 0
        ph = ph_ref[...]
        pc = p_ref[...]
        us[0, pl.ds(0, HALO_A), :] = jnp.where(first, 0.0, ph[:, 0:dg] * _sig(ph[:, dg:2 * dg]))
        us[0, pl.ds(HALO_A, tm), :] = pc[:, 0:dg] * _sig(pc[:, dg:2 * dg])
        mext[pl.ds(0, HALO_A), :] = jnp.where(first, 0.0, ph[:, 3 * dg:4 * dg] * ph[:, 4 * dg:5 * dg])
        mext[pl.ds(HALO_A, tm), :] = pc[:, 3 * dg:4 * dg] * pc[:, 4 * dg:5 * dg]
        _delayed_copies(us, ext)
        for r0 in range(0, tm, rc):
            rows = pl.ds(r0, rc)
            c = _conv_a(aw_ref, ab_ref, l, us, HALO_A + r0, rc, dg)
            xc = c - jnp.mean(c, axis=-1, keepdims=True)
            ln = xc * lax.rsqrt(jnp.mean(xc * xc, axis=-1, keepdims=True) + EPS) * lg_ref[l:l + 1, :] + lb_ref[l:l + 1, :]
            o_ref[rows, 0:dg] = (ln * _sig(ln)).astype(BF16)
            cb = bw_ref[l, 0:1, :] * mext[pl.ds(HALO_A - (kb - 1) + r0, rc), :]
            for k in range(1, kb):
                cb = cb + bw_ref[l, k:k + 1, :] * mext[pl.ds(HALO_A - (kb - 1) + k + r0, rc), :]
            o_ref[rows, dg:2 * dg] = (p_ref[rows, 2 * dg:3 * dg] * cb).astype(BF16)

    full = lambda a: pl.BlockSpec(a.shape, lambda i: (0,) * a.ndim)
    return _pcall(
        body, grid=(S // tm,),
        in_specs=[pl.BlockSpec((tm, W), lambda i: (i, 0)),
                  pl.BlockSpec((HALO_A, W), lambda i: (jnp.maximum(i * nb - 1, 0), 0)),
                  full(aw), full(ab), full(lg), full(lb), full(bw)],
        out_specs=pl.BlockSpec((tm, 2 * dg), lambda i: (i, 0)), out_shape=_sds((S, 2 * dg), BF16),
        scratch_shapes=[pltpu.VMEM((SUBLANES, ext, dg), F32), pltpu.VMEM((ext, dg), F32)],
        compiler_params=_cp("parallel"), name=name,
    )(p, p, aw, ab, lg, lb, bw)


def _convmix_bwd(name, p, dab, aw, ab, lg, lb, bw, l, exchange=None):
    S, W = p.shape
    dg = W // 5
    tm = _tile(S, 256, HALO_A)
    nb = tm // HALO_A
    n_i = S // tm
    ka, kb = CONV_A_WIDTH, CONV_B_WIDTH
    n = tm + HALO_A
    ext = HALO_A + n
    rc = _tile(tm, ELT_ROWS, BF16_ROWS)

    def body(p_ref, pp_ref, pn_ref, d_ref, dn_ref, aw_ref, ab_ref, lg_ref, lb_ref, bw_ref,
             dp_ref, daw_ref, dab_ref, dlg_ref, dlb_ref, dbw_ref, us, mext, dcs, dbext, accw):
        i = pl.program_id(0)
        first, last = i == 0, i == n_i - 1

        @pl.when(first)
        def _():
            for r in (daw_ref, dab_ref, dlg_ref, dlb_ref, dbw_ref):
                r[...] = jnp.zeros_like(r)

        accw[...] = jnp.zeros_like(accw)
        pp, pc, pn = pp_ref[...], p_ref[...], pn_ref[...]
        glu = lambda b: b[:, 0:dg] * _sig(b[:, dg:2 * dg])
        gch = lambda b: b[:, 3 * dg:4 * dg] * b[:, 4 * dg:5 * dg]
        us[0, pl.ds(0, HALO_A), :] = jnp.where(first, 0.0, glu(pp))
        us[0, pl.ds(HALO_A, tm), :] = glu(pc)
        us[0, pl.ds(HALO_A + tm, HALO_A), :] = glu(pn)
        mext[pl.ds(0, HALO_A), :] = jnp.where(first, 0.0, gch(pp))
        mext[pl.ds(HALO_A, tm), :] = gch(pc)
        mext[pl.ds(HALO_A + tm, HALO_A), :] = gch(pn)
        _delayed_copies(us, ext)
        chunks = [(r, rc) for r in range(0, tm, rc)] + [(tm, HALO_A)]
        g_ln = lg_ref[l:l + 1, :]
        zero8 = jnp.zeros((SUBLANES, dg), F32)

        acc_lg = acc_lb = acc_ab = zero8
        for r0, rows in chunks:
            c = _conv_a(aw_ref, ab_ref, l, us, HALO_A + r0, rows, dg)
            xc = c - jnp.mean(c, axis=-1, keepdims=True)
            rstd = lax.rsqrt(jnp.mean(xc * xc, axis=-1, keepdims=True) + EPS)
            chat = xc * rstd
            ln = chat * g_ln + lb_ref[l:l + 1, :]
            s = _sig(ln)
            da = d_ref[pl.ds(r0, rows), 0:dg] if r0 < tm else jnp.where(last, 0.0, dn_ref[:, 0:dg])
            dln = da * (s * (1.0 + ln * (1.0 - s)))
            dlnh = dln * g_ln
            dc = rstd * (dlnh - jnp.mean(dlnh, axis=-1, keepdims=True)
                         - chat * jnp.mean(dlnh * chat, axis=-1, keepdims=True))
            dcs[0, pl.ds(r0, rows), :] = dc
            if r0 < tm:
                acc_lg = acc_lg + _fold(dln * chat)
                acc_lb = acc_lb + _fold(dln)
                acc_ab = acc_ab + _fold(dc)
                for c0 in range(0, dg, LANES):
                    lanes = slice(c0, c0 + LANES)
                    for d in range(ka):
                        a, sh = divmod(d, SUBLANES)
                        k = ka - 1 - d
                        accw[pl.ds(SUBLANES * k, SUBLANES), lanes] += _fold(
                            dc[:, lanes] * us[sh, pl.ds(HALO_A + r0 - SUBLANES * a, rows), lanes])
        dlg_ref[...] += _rowsum(acc_lg)
        dlb_ref[...] += _rowsum(acc_lb)
        dab_ref[...] += _rowsum(acc_ab)
        for k in range(ka):
            daw_ref[k:k + 1, :] += _rowsum(accw[pl.ds(SUBLANES * k, SUBLANES), :])
        for s in range(1, SUBLANES):
            dcs[s, pl.ds(0, n - SUBLANES), :] = dcs[0, pl.ds(s, n - SUBLANES), :]
        for r0 in range(0, tm, rc):
            rows = pl.ds(r0, rc)
            parts = []
            for c0 in range(0, dg, LANES):
                lanes = slice(c0, c0 + LANES)
                acc = aw_ref[l, ka - 1:ka, lanes] * dcs[0, rows, lanes]
                for e in range(1, ka):
                    a, sh = divmod(e, SUBLANES)
                    acc = acc + aw_ref[l, ka - 1 - e:ka - e, lanes] * dcs[sh, pl.ds(r0 + SUBLANES * a, rc), lanes]
                parts.append(acc)
            du = jnp.concatenate(parts, axis=1)
            sg = _sig(p_ref[rows, dg:2 * dg])
            dp_ref[rows, 0:dg] = (du * sg).astype(BF16)
            dp_ref[rows, dg:2 * dg] = (du * p_ref[rows, 0:dg] * sg * (1.0 - sg)).astype(BF16)

        for r0, rows in chunks:
            if r0 < tm:
                dbext[pl.ds(r0, rows), :] = d_ref[pl.ds(r0, rows), dg:2 * dg] * p_ref[pl.ds(r0, rows), 2 * dg:3 * dg]
            else:
                dbext[pl.ds(r0, rows), :] = jnp.where(last, 0.0, dn_ref[:, dg:2 * dg] * pn[:, 2 * dg:3 * dg])
        acc_bw = [zero8] * kb
        for r0 in range(0, tm, rc):
            rows = pl.ds(r0, rc)
            m_k = [mext[pl.ds(HALO_A - (kb - 1) + k + r0, rc), :] for k in range(kb)]
            cb = bw_ref[l, 0:1, :] * m_k[0]
            dm = bw_ref[l, 0:1, :] * dbext[pl.ds(r0 + kb - 1, rc), :]
            for k in range(1, kb):
                cb = cb + bw_ref[l, k:k + 1, :] * m_k[k]
                dm = dm + bw_ref[l, k:k + 1, :] * dbext[pl.ds(r0 + kb - 1 - k, rc), :]
            dcb = dbext[rows, :]
            acc_bw = [acc_bw[k] + _fold(dcb * m_k[k]) for k in range(kb)]
            dp_ref[rows, 2 * dg:3 * dg] = (d_ref[rows, dg:2 * dg] * cb).astype(BF16)
            dp_ref[rows, 3 * dg:4 * dg] = (dm * p_ref[rows, 4 * dg:5 * dg]).astype(BF16)
            dp_ref[rows, 4 * dg:5 * dg] = (dm * p_ref[rows, 3 * dg:4 * dg]).astype(BF16)
        for k in range(kb):
            dbw_ref[k:k + 1, :] += _rowsum(acc_bw[k])

    full = lambda a: pl.BlockSpec(a.shape, lambda i: (0,) * a.ndim)
    prev = lambda i: (jnp.maximum(i * nb - 1, 0), 0)
    nxt = lambda i: (jnp.minimum((i + 1) * nb, S // HALO_A - 1), 0)
    acc = lambda r: pl.BlockSpec((r, dg), lambda i: (0, 0))
    body, in_specs, out_specs, out_shape, scratch, operands = _with_exchange(
        exchange, body,
        [pl.BlockSpec((tm, W), lambda i: (i, 0)), pl.BlockSpec((HALO_A, W), prev), pl.BlockSpec((HALO_A, W), nxt),
         pl.BlockSpec((tm, 2 * dg), lambda i: (i, 0)), pl.BlockSpec((HALO_A, 2 * dg), nxt),
         full(aw), full(ab), full(lg), full(lb), full(bw)],
        [pl.BlockSpec((tm, W), lambda i: (i, 0)), acc(ka), acc(1), acc(1), acc(1), acc(kb)],
        [_sds((S, W), BF16), _sds((ka, dg), F32), _sds((1, dg), F32), _sds((1, dg), F32), _sds((1, dg), F32),
         _sds((kb, dg), F32)],
        [pltpu.VMEM((SUBLANES, ext, dg), F32), pltpu.VMEM((ext, dg), F32), pltpu.VMEM((SUBLANES, n, dg), F32),
         pltpu.VMEM((n, dg), F32), pltpu.VMEM((SUBLANES * ka, dg), F32)],
        [p, p, p, dab, dab, aw, ab, lg, lb, bw],
        lambda: pl.program_id(0) == 0, lambda: pl.program_id(0) == n_i - 1)
    return _pcall(
        body, grid=(n_i,), in_specs=in_specs, out_specs=out_specs, out_shape=out_shape, scratch_shapes=scratch,
        compiler_params=_cp("arbitrary"), name=name,
    )(*operands)


def _ffn_mid_fwd(name, u2, dww, dwb, l):
    _, S, F = u2.shape
    tm = _tile(S, 256, BF16_ROWS)
    tc = _tile(F, 1408)
    n_f = F // tc
    nb = tm // HALO_S
    kf = FFN_CONV_WIDTH

    def body(u_ref, uh_ref, wg_ref, wv_ref, bg_ref, bv_ref, o_ref, ext):
        first = pl.program_id(1) == 0
        ext[:, pl.ds(0, HALO_S), :] = jnp.where(first, 0.0, uh_ref[...])
        ext[:, pl.ds(HALO_S, tm), :] = u_ref[...]
        rc = _tile(tm, ELT_ROWS, BF16_ROWS)

        def lane_chunk(ci, carry):
            lanes = pl.ds(pl.multiple_of(ci * LANES, LANES), LANES)
            taps = [[w_ref[k:k + 1, lanes] for k in range(kf)] for w_ref in (wg_ref, wv_ref)]
            bias = [b_ref[l:l + 1, lanes] for b_ref in (bg_ref, bv_ref)]
            for r0 in range(0, tm, rc):
                c = []
                for g in range(2):
                    acc = bias[g]
                    for k in range(kf):
                        acc = acc + taps[g][k] * ext[g, pl.ds(HALO_S - (kf - 1) + k + r0, rc), lanes]
                    c.append(acc)
                o_ref[pl.ds(r0, rc), lanes] = (c[0] * _sig(c[0]) * c[1]).astype(BF16)
            return carry

        lax.fori_loop(0, tc // LANES, lane_chunk, 0)

    n_l = dwb.shape[0]
    return _pcall(
        body, grid=(n_f, S // tm),
        in_specs=[pl.BlockSpec((2, tm, tc), lambda j, i: (0, i, j)),
                  pl.BlockSpec((2, HALO_S, tc), lambda j, i: (0, jnp.maximum(i * nb - 1, 0), j)),
                  pl.BlockSpec((None, kf, tc), lambda j, i: (l, 0, j)),
                  pl.BlockSpec((None, kf, tc), lambda j, i: (l, 0, j + n_f)),
                  pl.BlockSpec((n_l, tc), lambda j, i: (0, j)),
                  pl.BlockSpec((n_l, tc), lambda j, i: (0, j + n_f))],
        out_specs=pl.BlockSpec((tm, tc), lambda j, i: (i, j)), out_shape=_sds((S, F), BF16),
        scratch_shapes=[pltpu.VMEM((2, HALO_S + tm, tc), F32)],
        compiler_params=_cp("parallel", "parallel"), name=name,
    )(u2, u2, dww, dww, dwb, dwb)


def _ffn_mid_bwd(name, u2, df, dww, dwb, l, exchange=None):
    _, S, F = u2.shape
    tm = _tile(S, 256, BF16_ROWS)
    tc = _tile(F, 1408)
    n_f = F // tc
    nb = tm // HALO_S
    n_i = S // tm
    kf = FFN_CONV_WIDTH
    n = tm + HALO_S

    def body(u_ref, up_ref, un_ref, df_ref, dfn_ref, wg_ref, wv_ref, bg_ref, bv_ref,
             du_ref, dw_ref, db_ref, uext, dcext):
        i = pl.program_id(1)
        first, last = i == 0, i == n_i - 1

        @pl.when(first)
        def _():
            dw_ref[...] = jnp.zeros_like(dw_ref)
            db_ref[...] = jnp.zeros_like(db_ref)

        uext[:, pl.ds(0, HALO_S), :] = jnp.where(first, 0.0, up_ref[...])
        uext[:, pl.ds(HALO_S, tm), :] = u_ref[...]
        uext[:, pl.ds(HALO_S + tm, HALO_S), :] = un_ref[...]
        rc = _tile(tm, ELT_ROWS, BF16_ROWS)

        def lane_chunk(ci, carry):
            lanes = pl.ds(pl.multiple_of(ci * LANES, LANES), LANES)
            taps = [[w_ref[k:k + 1, lanes] for k in range(kf)] for w_ref in (wg_ref, wv_ref)]
            bias = [b_ref[l:l + 1, lanes] for b_ref in (bg_ref, bv_ref)]
            acc_w = [[jnp.zeros((SUBLANES, LANES), F32) for _ in range(kf)] for _ in range(2)]
            acc_b = [jnp.zeros((SUBLANES, LANES), F32) for _ in range(2)]
            for r0, rows in [(r, rc) for r in range(0, tm, rc)] + [(tm, HALO_S)]:
                shifted = [[uext[g, pl.ds(HALO_S - (kf - 1) + k + r0, rows), lanes] for k in range(kf)] for g in range(2)]
                conv = []
                for g in range(2):
                    acc = bias[g]
                    for k in range(kf):
                        acc = acc + taps[g][k] * shifted[g][k]
                    conv.append(acc)
                cg, cv = conv
                s = _sig(cg)
                dfe = df_ref[pl.ds(r0, rows), lanes] if r0 < tm else jnp.where(last, 0.0, dfn_ref[:, lanes])
                dc = [dfe * cv * (s * (1.0 + cg * (1.0 - s))), dfe * (cg * s)]
                for g in range(2):
                    dcext[g, pl.ds(r0, rows), lanes] = dc[g]
                    if r0 < tm:
                        acc_b[g] = acc_b[g] + _fold(dc[g])
                        for k in range(kf):
                            acc_w[g][k] = acc_w[g][k] + _fold(dc[g] * shifted[g][k])
            for r0 in range(0, tm, rc):
                for g in range(2):
                    du = taps[g][0] * dcext[g, pl.ds(r0 + kf - 1, rc), lanes]
                    for k in range(1, kf):
                        du = du + taps[g][k] * dcext[g, pl.ds(r0 + kf - 1 - k, rc), lanes]
                    du_ref[g, pl.ds(r0, rc), lanes] = du.astype(BF16)
            for g in range(2):
                db_ref[g, :, lanes] += _rowsum(acc_b[g])
                for k in range(kf):
                    dw_ref[g, k:k + 1, lanes] += _rowsum(acc_w[g][k])
            return carry

        lax.fori_loop(0, tc // LANES, lane_chunk, 0)

    n_l = dwb.shape[0]
    prev = lambda j, i: (0, jnp.maximum(i * nb - 1, 0), j)
    nxt = lambda j, i: (0, jnp.minimum((i + 1) * nb, S // HALO_S - 1), j)
    body, in_specs, out_specs, out_shape, scratch, operands = _with_exchange(
        exchange, body,
        [pl.BlockSpec((2, tm, tc), lambda j, i: (0, i, j)),
         pl.BlockSpec((2, HALO_S, tc), prev), pl.BlockSpec((2, HALO_S, tc), nxt),
         pl.BlockSpec((tm, tc), lambda j, i: (i, j)),
         pl.BlockSpec((HALO_S, tc), lambda j, i: nxt(j, i)[1:]),
         pl.BlockSpec((None, kf, tc), lambda j, i: (l, 0, j)),
         pl.BlockSpec((None, kf, tc), lambda j, i: (l, 0, j + n_f)),
         pl.BlockSpec((n_l, tc), lambda j, i: (0, j)),
         pl.BlockSpec((n_l, tc), lambda j, i: (0, j + n_f))],
        [pl.BlockSpec((2, tm, tc), lambda j, i: (0, i, j)),
         pl.BlockSpec((2, kf, tc), lambda j, i: (0, 0, j)),
         pl.BlockSpec((2, 1, tc), lambda j, i: (0, 0, j))],
        [_sds((2, S, F), BF16), _sds((2, kf, F), F32), _sds((2, 1, F), F32)],
        [pltpu.VMEM((2, HALO_S + n, tc), F32), pltpu.VMEM((2, n, tc), F32)],
        [u2, u2, u2, df, df, dww, dww, dwb, dwb],
        lambda: jnp.logical_and(pl.program_id(0) == 0, pl.program_id(1) == 0),
        lambda: jnp.logical_and(pl.program_id(0) == n_f - 1, pl.program_id(1) == n_i - 1))
    return _pcall(
        body, grid=(n_f, n_i), in_specs=in_specs, out_specs=out_specs, out_shape=out_shape, scratch_shapes=scratch,
        compiler_params=_cp("arbitrary" if exchange else "parallel", "arbitrary"), name=name,
    )(*operands)


def _head_sum_matrix():
    r = lax.broadcasted_iota(jnp.int32, (LANES, LANES), 0) // HEAD_DIM
    c = lax.broadcasted_iota(jnp.int32, (LANES, LANES), 1) // HEAD_DIM
    return (r == c).astype(BF16)


def _head_mean(x, ones):
    return _split_dot(x, ones) * (1.0 / HEAD_DIM)


def _qknorm_fwd(name, qkv, g2):
    S, D3 = qkv.shape
    D = D3 // 3
    tm = _tile(S, 256, BF16_ROWS)
    scale = HEAD_DIM ** -0.5

    def body(q_ref, k_ref, v_ref, g_ref, qo_ref, ko_ref, vo_ref):
        ones = _head_sum_matrix()
        for cc in range(D // LANES):
            sl = slice(cc * LANES, (cc + 1) * LANES)
            for x_ref, o_ref, row, mult in ((q_ref, qo_ref, 0, scale), (k_ref, ko_ref, 1, 1.0)):
                x = x_ref[:, sl]
                r = lax.rsqrt(_head_mean(x * x, ones) + EPS)
                o_ref[:, sl] = ((x * r * g_ref[row:row + 1, :]).astype(BF16) * mult).astype(BF16)
        vo_ref[...] = v_ref[...].astype(BF16)

    col = lambda c: pl.BlockSpec((tm, D), lambda i: (i, c))
    out = pl.BlockSpec((tm, D), lambda i: (i, 0))
    return _pcall(
        body, grid=(S // tm,),
        in_specs=[col(0), col(1), col(2), pl.BlockSpec(g2.shape, lambda i: (0, 0))],
        out_specs=[out, out, out], out_shape=[_sds((S, D), BF16)] * 3,
        compiler_params=_cp("parallel"), name=name,
    )(qkv, qkv, qkv, g2)


def _qknorm_bwd(name, qkv, dq, dk, dv, g2):
    S, D3 = qkv.shape
    D = D3 // 3
    tm = _tile(S, 256, BF16_ROWS)
    scale = HEAD_DIM ** -0.5

    def body(q_ref, k_ref, dq_ref, dk_ref, dv_ref, g_ref, o_ref, dg_ref):
        @pl.when(pl.program_id(0) == 0)
        def _():
            dg_ref[...] = jnp.zeros_like(dg_ref)

        ones = _head_sum_matrix()
        for cc in range(D // LANES):
            sl = slice(cc * LANES, (cc + 1) * LANES)
            for x_ref, d_ref, row, mult, base in ((q_ref, dq_ref, 0, scale, 0), (k_ref, dk_ref, 1, 1.0, D)):
                x = x_ref[:, sl]
                r = lax.rsqrt(_head_mean(x * x, ones) + EPS)
                xh = x * r
                dn = d_ref[:, sl] * mult
                dxh = dn * g_ref[row:row + 1, :]
                dx = r * (dxh - xh * _head_mean(dxh * xh, ones))
                o_ref[:, base + cc * LANES:base + (cc + 1) * LANES] = dx.astype(BF16)
                dg_ref[row:row + 1, :] += _rowsum(dn * xh)
        o_ref[:, 2 * D:3 * D] = dv_ref[...].astype(BF16)

    col = lambda c: pl.BlockSpec((tm, D), lambda i: (i, c))
    row = pl.BlockSpec((tm, D), lambda i: (i, 0))
    return _pcall(
        body, grid=(S // tm,),
        in_specs=[col(0), col(1), row, row, row, pl.BlockSpec(g2.shape, lambda i: (0, 0))],
        out_specs=[pl.BlockSpec((tm, D3), lambda i: (i, 0)), pl.BlockSpec((2, LANES), lambda i: (0, 0))],
        out_shape=[_sds((S, D3), BF16), _sds((2, LANES), F32)],
        compiler_params=_cp("arbitrary"), name=name,
    )(qkv, qkv, dq, dk, dv, g2)


def _attn_consts():
    t = ATTN_BLOCK
    row = lax.broadcasted_iota(jnp.int32, (t, t), 0)
    col = lax.broadcasted_iota(jnp.int32, (t, t), 1)
    lane = lax.broadcasted_iota(jnp.int32, (1, LANES), 1)
    heads = (lane < HEAD_DIM, lane >= HEAD_DIM)
    return row, col, heads


def _split_dot(x, m):
    n = x.shape[0]
    hi = x.astype(BF16)
    lo = (x - hi.astype(F32)).astype(BF16)
    both = jnp.dot(jnp.concatenate([hi, lo], axis=0), m, preferred_element_type=F32)
    return both[:n] + both[n:]


def _log_keep(z):
    return -(jnp.maximum(z, 0.0) + jnp.log(1.0 + jnp.exp(-jnp.abs(z))))


def _stack_heads(a, heads):
    t = ATTN_BLOCK
    zero = jnp.zeros((t, LANES), a.dtype)
    return jnp.concatenate([jnp.where(h, a[s * t:(s + 1) * t], zero) for s in range(a.shape[0] // t) for h in heads], axis=0)


def _side_by_side(a):
    t = ATTN_BLOCK
    return jnp.concatenate([jnp.concatenate([a[2 * s * t:(2 * s + 1) * t], a[(2 * s + 1) * t:(2 * s + 2) * t]], axis=1)
                            for s in range(a.shape[0] // (2 * t))], axis=0)


def _grow(a, rows, cols):
    z = jnp.zeros((rows, cols), F32)
    return z if a is None else jnp.concatenate([z, a], axis=0)


def _attn_fwd(name, qs, kn, vb):
    S, D = qs.shape
    t = ATTN_BLOCK
    tq = ATTN_SUB * t

    def body(q_ref, k_ref, v_ref, o_ref):
        i = pl.program_id(1)
        row, col, heads = _attn_consts()
        after_m = (row > col).astype(BF16)
        causal = col < row
        q_all = _stack_heads(q_ref[...], heads)

        def block(j, q, r, acc, mask):
            off = pl.multiple_of(j * t, t)
            kb = k_ref[pl.ds(off, t), :]
            v2 = _stack_heads(v_ref[pl.ds(off, t), :], heads)
            z = lax.dot_general(q, kb, NT, preferred_element_type=F32)
            lk = _log_keep(z)
            if mask is not None:
                lk = jnp.where(mask, lk, 0.0)
            w = jnp.exp(z + lk + _split_dot(lk, after_m) + r)
            if mask is not None:
                w = jnp.where(mask, w, 0.0)
            acc = acc + jnp.dot(_side_by_side(w.astype(BF16)), v2, preferred_element_type=F32)
            return r + jnp.sum(lk, axis=1, keepdims=True), acc

        def head(n_more):
            r = acc = None
            for s in reversed(range(ATTN_SUB)):
                mask = jnp.concatenate([causal, causal] + [jnp.ones_like(causal)] * (2 * (ATTN_SUB - 1 - s)), axis=0)
                r, acc = block(ATTN_SUB * i + s, q_all[2 * s * t:], _grow(r, 2 * t, 1), _grow(acc, t, LANES), mask)
            for b in range(n_more):
                r, acc = block(ATTN_SUB * i - 1 - b, q_all, r, acc, None)
            return r, acc

        r, acc = lax.cond(ATTN_SUB * i >= ATTN_MORE, lambda: head(ATTN_MORE), lambda: head(0))

        def cond(c):
            return jnp.logical_and(c[0] >= 0, jnp.max(c[1]) > EXP_UNDERFLOW)

        def step(c):
            r, a = block(c[0], q_all, c[1], c[2], None)
            return c[0] - 1, r, a

        first = jnp.where(ATTN_SUB * i >= ATTN_MORE, ATTN_SUB * i - 1 - ATTN_MORE, ATTN_SUB * i - 1)
        o_ref[...] = lax.while_loop(cond, step, (first, r, acc))[2]

    n_hp = D // LANES
    blk = pl.BlockSpec((tq, LANES), lambda hp, i: (i, hp))
    seq = pl.BlockSpec((S, LANES), lambda hp, i: (0, hp))
    return _pcall(
        body, grid=(n_hp, S // tq), in_specs=[blk, seq, seq], out_specs=blk, out_shape=_sds((S, D), F32),
        compiler_params=_cp("parallel", "arbitrary"), name=name,
    )(qs, kn, vb)


def _attn_bwd(name, qs, kn, vb, o, do):
    S, D = qs.shape
    t = ATTN_BLOCK
    tq = ATTN_SUB * t

    def body(q_ref, k_ref, v_ref, o_ref, do_ref, dq_ref, dk_ref, dv_ref):
        i = pl.program_id(1)

        @pl.when(i == 0)
        def _():
            dk_ref[...] = jnp.zeros_like(dk_ref)
            dv_ref[...] = jnp.zeros_like(dv_ref)

        row, col, heads = _attn_consts()
        after_m = (row > col).astype(BF16)
        from_m = (row >= col).astype(BF16)
        causal = col < row
        q_all = _stack_heads(q_ref[...], heads)
        dob = do_ref[...].astype(BF16)
        do_all = _stack_heads(dob, heads)
        dsum_all = jnp.sum(_stack_heads(dob.astype(F32) * o_ref[...], heads), axis=1, keepdims=True)

        def block(j, q, dor, dsum, r, es, dq, mask):
            off = pl.multiple_of(j * t, t)
            kb = k_ref[pl.ds(off, t), :]
            vblk = v_ref[pl.ds(off, t), :]
            z = lax.dot_general(q, kb, NT, preferred_element_type=F32)
            lk = _log_keep(z)
            if mask is not None:
                lk = jnp.where(mask, lk, 0.0)
            ls = z + lk
            w = jnp.exp(ls + _split_dot(lk, after_m) + r)
            if mask is not None:
                w = jnp.where(mask, w, 0.0)
            e = w * lax.dot_general(dor, vblk, NT, preferred_element_type=F32)
            before = dsum - (es + _split_dot(e, from_m))
            dz = e - (e + before) * jnp.exp(ls)
            if mask is not None:
                dz = jnp.where(mask, dz, 0.0)
            dzb = dz.astype(BF16)
            dq = dq + jnp.dot(_side_by_side(dzb), _stack_heads(kb, heads), preferred_element_type=F32)
            dk_ref[pl.ds(off, t), :] += lax.dot_general(dzb, q, TN, preferred_element_type=F32)
            dv_ref[pl.ds(off, t), :] += lax.dot_general(w.astype(BF16), dor, TN, preferred_element_type=F32)
            return r + jnp.sum(lk, axis=1, keepdims=True), es + jnp.sum(e, axis=1, keepdims=True), dq

        def head(n_more):
            r = es = dq = None
            for s in reversed(range(ATTN_SUB)):
                mask = jnp.concatenate([causal, causal] + [jnp.ones_like(causal)] * (2 * (ATTN_SUB - 1 - s)), axis=0)
                lo = 2 * s * t
                r, es, dq = block(ATTN_SUB * i + s, q_all[lo:], do_all[lo:], dsum_all[lo:], _grow(r, 2 * t, 1),
                                  _grow(es, 2 * t, 1), _grow(dq, t, LANES), mask)
            for b in range(n_more):
                r, es, dq = block(ATTN_SUB * i - 1 - b, q_all, do_all, dsum_all, r, es, dq, None)
            return r, es, dq

        r, es, dq = lax.cond(ATTN_SUB * i >= ATTN_MORE, lambda: head(ATTN_MORE), lambda: head(0))

        def cond(c):
            return jnp.logical_and(c[0] >= 0, jnp.max(c[1]) > EXP_UNDERFLOW)

        def step(c):
            r, es, a = block(c[0], q_all, do_all, dsum_all, c[1], c[2], c[3], None)
            return c[0] - 1, r, es, a

        first = jnp.where(ATTN_SUB * i >= ATTN_MORE, ATTN_SUB * i - 1 - ATTN_MORE, ATTN_SUB * i - 1)
        dq_ref[...] = lax.while_loop(cond, step, (first, r, es, dq))[3]

    n_hp = D // LANES
    blk = pl.BlockSpec((tq, LANES), lambda hp, i: (i, hp))
    seq = pl.BlockSpec((S, LANES), lambda hp, i: (0, hp))
    return _pcall(
        body, grid=(n_hp, S // tq), in_specs=[blk, seq, seq, blk, blk], out_specs=[blk, seq, seq],
        out_shape=[_sds((S, D), F32)] * 3, compiler_params=_cp("parallel", "arbitrary"), name=name,
    )(qs, kn, vb, o, do)


def _adamw(name, w, g, m, v):
    L, R, C = w.shape
    tr = _tile(R, 256, SUBLANES)
    c1 = 1.0 - ADAM_B1 ** ADAM_STEP
    c2 = 1.0 - ADAM_B2 ** ADAM_STEP

    def body(w_ref, g_ref, m_ref, v_ref, d_ref, mo_ref, vo_ref):
        gg = g_ref[...]
        mn = ADAM_B1 * m_ref[...] + (1.0 - ADAM_B1) * gg
        vn = ADAM_B2 * v_ref[...] + (1.0 - ADAM_B2) * (gg * gg)
        d_ref[...] = -ADAM_LR * ((mn / c1) / (jnp.sqrt(vn / c2) + ADAM_EPS) + ADAM_WD * w_ref[...])
        mo_ref[...] = mn
        vo_ref[...] = vn

    blk = pl.BlockSpec((None, tr, C), lambda l, i: (l, i, 0))
    return _pcall(
        body, grid=(L, R // tr), in_specs=[blk] * 4, out_specs=[blk] * 3, out_shape=[_sds(w.shape, F32)] * 3,
        compiler_params=_cp("parallel", "parallel"), name=name,
    )(w, g, m, v)


def _place():
    x, y, c = lax.axis_index("x"), lax.axis_index("y"), lax.axis_index("c")
    chips = [(1 - x, y), (x, 1 - y), (1 - x, 1 - y)]
    return x, y, c, chips


def _place_shard(name, w, j_idx):
    L, R, X = w.shape
    rh = R // 2
    tr = _tile(rh, 256, BF16_ROWS)

    def body(j_ref, w_ref, o_ref):
        o_ref[...] = w_ref[...].astype(BF16)

    return _pcall(
        body,
        grid_spec=pltpu.PrefetchScalarGridSpec(
            num_scalar_prefetch=1, grid=(L, 2, rh // tr),
            in_specs=[pl.BlockSpec((None, None, tr, X), lambda l, h, i, j_ref: (l, h, i, 0))],
            out_specs=pl.BlockSpec((None, None, None, tr, X), lambda l, h, i, j_ref: (l, j_ref[0], h, i, 0))),
        out_shape=_sds((L, N_CHIPS, 2, rh, X), BF16), compiler_params=_cp("parallel", "parallel", "parallel"), name=name,
    )(j_idx, w.reshape(L, 2, rh, X))


def _all_gather_weights(bufs, small_ws):
    n_big, n_small = len(bufs), len(small_ws)
    n_in = n_big + n_small

    def body(*refs):
        ins, outs = refs[:n_in], refs[n_in:2 * n_in]
        send_sems, recv_sems, local_sems = refs[2 * n_in:]
        x, y, c, chips = _place()
        j_me = 2 * x + y
        j_of = [2 * cx + cy for cx, cy in chips]
        sibling = (x, y, 1 - c)

        def remote(src, dst, s, to):
            return pltpu.make_async_remote_copy(src_ref=src, dst_ref=dst, send_sem=send_sems.at[s], recv_sem=recv_sems.at[s],
                                                device_id=to, device_id_type=MESH)

        started = []
        for t in range(n_big, n_in):
            loc = pltpu.make_async_copy(ins[t], outs[t].at[:, j_me], local_sems.at[t - n_big])
            loc.start()
            started.append(loc)
        first = []
        for t in range(n_big):
            mine = outs[t].at[:, j_me, c]
            for k in range(3):
                first.append(remote(mine, mine, 6 * t + k, (*chips[k], c)))
        for t in range(n_big, n_in):
            for k in range(3):
                first.append(remote(ins[t], outs[t].at[:, j_me], 6 * n_big + 3 * (t - n_big) + k, (*chips[k], c)))
        for cp in first:
            cp.start()
        passed = []
        for t in range(n_big):
            for k in range(3):
                landed = outs[t].at[:, j_of[k], c]
                remote(landed, landed, 6 * t + k, (*chips[k], c)).wait_recv()
                fwd = remote(landed, landed, 6 * t + 3 + k, sibling)
                fwd.start()
                passed.append(fwd)
        for t in range(n_big):
            for k in range(3):
                other = outs[t].at[:, j_of[k], 1 - c]
                remote(other, other, 6 * t + 3 + k, sibling).wait_recv()
        for t in range(n_big, n_in):
            for k in range(3):
                dst = outs[t].at[:, j_of[k]]
                remote(dst, dst, 6 * n_big + 3 * (t - n_big) + k, (*chips[k], c)).wait_recv()
        for cp in first + passed:
            cp.wait_send()
        for loc in started:
            loc.wait()

    out_shape = [_sds(b.shape, b.dtype) for b in bufs]
    out_shape += [_sds((w.shape[0], N_CHIPS) + w.shape[1:], w.dtype) for w in small_ws]
    n_sem = 6 * n_big + 3 * n_small
    outs = _pcall(
        body, in_specs=[ANY] * n_in, out_specs=[ANY] * n_in, out_shape=out_shape,
        input_output_aliases={t: t for t in range(n_big)},
        scratch_shapes=[pltpu.SemaphoreType.DMA((n_sem,)), pltpu.SemaphoreType.DMA((n_sem,)), pltpu.SemaphoreType.DMA((n_small,))],
        name="all_gather_weights",
    )(*bufs, *small_ws)
    return outs[:n_big], outs[n_big:]


class _Exchange:
    def __init__(self, operands, out_shapes, n_sems, copies):
        self.operands, self.out_shapes, self.n_sems, self.copies = list(operands), list(out_shapes), n_sems, copies

    @property
    def scratch(self):
        return [pltpu.SemaphoreType.DMA((self.n_sems,)), pltpu.SemaphoreType.DMA((self.n_sems,))]

    def split(self, refs):
        n_in, n_out = len(self.operands), len(self.out_shapes)
        return refs[:n_in], refs[n_in:n_in + n_out]

    def start(self, ins, outs, sems):
        for cp in self.copies(ins, outs, *sems):
            cp.start()

    def wait(self, ins, outs, sems):
        for cp in self.copies(ins, outs, *sems):
            cp.wait()


def _run_exchange(name, ex):
    n_in, n_out = len(ex.operands), len(ex.out_shapes)

    def body(*refs):
        ins, outs, sems = refs[:n_in], refs[n_in:n_in + n_out], refs[n_in + n_out:]
        ex.start(ins, outs, sems)
        ex.wait(ins, outs, sems)

    return _pcall(body, in_specs=[ANY] * n_in, out_specs=[ANY] * n_out, out_shape=ex.out_shapes, scratch_shapes=ex.scratch,
                  name=name)(*ex.operands)


def _core_halves_exchange(grads, spans):
    def copies(ins, outs, send_sems, recv_sems):
        x, y, c, _ = _place()
        return [pltpu.make_async_remote_copy(src_ref=ins[t].at[pl.ds(l0, n), :, 1 - c], dst_ref=outs[t],
                                             send_sem=send_sems.at[t], recv_sem=recv_sems.at[t], device_id=(x, y, 1 - c),
                                             device_id_type=MESH) for t, (l0, n) in enumerate(spans)]

    shapes = [_sds((n, g.shape[1], g.shape[3], g.shape[4]), F32) for g, (_, n) in zip(grads, spans)]
    return _Exchange(grads, shapes, len(grads), copies)


def _add_core_halves(name, g, a, c_idx, l0):
    _, nj, _, rh, X = g.shape
    L = a.shape[0]
    tr = _tile(rh, 256, BF16_ROWS)

    def body(c_ref, g_ref, a_ref, o_ref, ob_ref):
        s = g_ref[...] + a_ref[...]
        o_ref[...] = s
        ob_ref[...] = s.astype(BF16)

    blk = pl.BlockSpec((None, None, tr, X), lambda l, j, i, c_ref: (l, j, i, 0))
    return _pcall(
        body,
        grid_spec=pltpu.PrefetchScalarGridSpec(
            num_scalar_prefetch=1, grid=(L, nj, rh // tr),
            in_specs=[pl.BlockSpec((None, None, None, tr, X), lambda l, j, i, c_ref: (l + l0, j, c_ref[0], i, 0)), blk],
            out_specs=[blk, blk]),
        out_shape=[_sds((L, nj, rh, X), F32), _sds((L, nj, rh, X), BF16)],
        compiler_params=_cp("parallel", "parallel", "parallel"), name=name,
    )(c_idx, g, a)


def _chip_shards_exchange(parts):
    def copies(ins, outs, send_sems, recv_sems):
        x, y, c, chips = _place()
        return [pltpu.make_async_remote_copy(
            src_ref=ins[t].at[:, 2 * cx + cy], dst_ref=outs[t].at[k], send_sem=send_sems.at[3 * t + k],
            recv_sem=recv_sems.at[3 * t + k], device_id=(cx, cy, c), device_id_type=MESH)
            for t in range(len(parts)) for k, (cx, cy) in enumerate(chips)]

    shapes = [_sds((3, p.shape[0], p.shape[2], p.shape[3]), p.dtype) for p in parts]
    return _Exchange(parts, shapes, 3 * len(parts), copies)


def _add_chip_shards(name, p, b, jc_idx, l0, n_layers, buf):
    n, _, rh, X = p.shape
    tr = _tile(rh, 256, BF16_ROWS)

    def body(jc_ref, p_ref, b_ref, *rest):
        rest[-1][...] = ((p_ref[...] + b_ref[0].astype(F32)) + b_ref[1].astype(F32)) + b_ref[2].astype(F32)

    in_specs = [pl.BlockSpec((None, None, tr, X), lambda l, i, jc: (l, jc[0], i, 0)),
                pl.BlockSpec((3, None, tr, X), lambda l, i, jc: (0, l, i, 0))]
    operands = [jc_idx, p, b]
    if buf is not None:
        in_specs.append(ANY)
        operands.append(buf)
    return _pcall(
        body,
        grid_spec=pltpu.PrefetchScalarGridSpec(
            num_scalar_prefetch=1, grid=(n, rh // tr), in_specs=in_specs,
            out_specs=pl.BlockSpec((None, None, tr, X), lambda l, i, jc: (l + l0, jc[1], i, 0))),
        out_shape=_sds((n_layers, 2, rh, X), F32), input_output_aliases={3: 0} if buf is not None else {},
        compiler_params=_cp("parallel", "parallel"), name=name,
    )(*operands)


def _join_core_halves(bufs):
    n = len(bufs)

    def body(*refs):
        outs = refs[n:2 * n]
        send_sems, recv_sems = refs[2 * n:]
        x, y, c, _ = _place()
        cps = [pltpu.make_async_remote_copy(src_ref=outs[t].at[:, c], dst_ref=outs[t].at[:, c], send_sem=send_sems.at[t],
                                            recv_sem=recv_sems.at[t], device_id=(x, y, 1 - c), device_id_type=MESH)
               for t in range(n)]
        for cp in cps:
            cp.start()
        for t in range(n):
            pltpu.make_async_remote_copy(src_ref=outs[t].at[:, c], dst_ref=outs[t].at[:, 1 - c], send_sem=send_sems.at[t],
                                         recv_sem=recv_sems.at[t], device_id=(x, y, 1 - c), device_id_type=MESH).wait()

    outs = _pcall(
        body, in_specs=[ANY] * n, out_specs=[ANY] * n, out_shape=[_sds(b.shape, F32) for b in bufs],
        input_output_aliases={t: t for t in range(n)},
        scratch_shapes=[pltpu.SemaphoreType.DMA((n,)), pltpu.SemaphoreType.DMA((n,))],
        name="grad_join_core_halves",
    )(*bufs)
    return [o.reshape(o.shape[0], 2 * o.shape[2], o.shape[3]) for o in outs]


def _all_reduce_small(packed):
    R, C = packed.shape

    def body(x_ref, o_ref, slots, send_sems, recv_sems):
        x, y, c, _ = _place()
        me = 4 * x + 2 * y + c
        slots[me] = x_ref[...]
        cps = []
        for d in range(N_DEV):
            to = (d // 4, (d // 2) % 2, d % 2)
            cp = pltpu.make_async_remote_copy(src_ref=x_ref, dst_ref=slots.at[me], send_sem=send_sems.at[d],
                                              recv_sem=recv_sems.at[me], device_id=to, device_id_type=MESH)
            cps.append(cp)

            @pl.when(d != me)
            def _():
                cp.start()

        for d in range(N_DEV):
            @pl.when(d != me)
            def _():
                pltpu.make_async_remote_copy(src_ref=x_ref, dst_ref=slots.at[d], send_sem=send_sems.at[d],
                                             recv_sem=recv_sems.at[d], device_id=(x, y, c), device_id_type=MESH).wait_recv()
                cps[d].wait_send()

        acc = slots[0]
        for d in range(1, N_DEV):
            acc = acc + slots[d]
        o_ref[...] = acc

    vm = pl.BlockSpec(memory_space=pltpu.VMEM)
    return _pcall(
        body, in_specs=[vm], out_specs=vm, out_shape=_sds((R, C), F32),
        scratch_shapes=[pltpu.VMEM((N_DEV, R, C), F32), pltpu.SemaphoreType.DMA((N_DEV,)), pltpu.SemaphoreType.DMA((N_DEV,))],
        compiler_params=pltpu.CompilerParams(vmem_limit_bytes=VMEM_LIMIT_BYTES), name="all_reduce_small",
    )(packed)


PACK = SUBLANES * LANES


def _pack(arrays):
    flat = []
    for a in arrays:
        v = a.reshape(-1)
        flat.append(jnp.pad(v, (0, (-v.shape[0]) % PACK)))
    return jnp.concatenate(flat).reshape(-1, LANES)


def _unpack(packed, shapes):
    flat = packed.reshape(-1)
    out, pos = [], 0
    for s in shapes:
        n = 1
        for d in s:
            n *= d
        out.append(flat[pos:pos + n].reshape(s))
        pos += n + (-n) % PACK
    return out


def kernel(x, mix_norm_g, ffn_norm_g, conv_w_in, conv_a_dw_w, conv_a_dw_b, conv_a_ln_g, conv_a_ln_b, conv_b_dw_w, conv_w_out, attn_w_qkv, attn_q_g, attn_k_g, attn_w_o, ffn_w_up, ffn_dw_w, ffn_dw_b, ffn_w_down, loss_target, m_mix_norm_g, m_ffn_norm_g, m_conv_w_in, m_conv_a_dw_w, m_conv_a_dw_b, m_conv_a_ln_g, m_conv_a_ln_b, m_conv_b_dw_w, m_conv_w_out, m_attn_w_qkv, m_attn_q_g, m_attn_k_g, m_attn_w_o, m_ffn_w_up, m_ffn_dw_w, m_ffn_dw_b, m_ffn_w_down, v_mix_norm_g, v_ffn_norm_g, v_conv_w_in, v_conv_a_dw_w, v_conv_a_dw_b, v_conv_a_ln_g, v_conv_a_ln_b, v_conv_b_dw_w, v_conv_w_out, v_attn_w_qkv, v_attn_q_g, v_attn_k_g, v_attn_w_o, v_ffn_w_up, v_ffn_dw_w, v_ffn_dw_b, v_ffn_w_down):
    depth = mix_norm_g.shape[0]
    n_even, n_odd = conv_w_in.shape[0], attn_w_qkv.shape[0]
    S, D = x.shape[1], x.shape[2]
    dg = D // 2
    x0 = x.reshape(S, D)
    target = loss_target.reshape(S, D)
    j_me = 2 * lax.axis_index("x") + lax.axis_index("y")
    c_me = lax.axis_index("c")
    j_idx = j_me.astype(jnp.int32).reshape(1)
    c_idx = c_me.astype(jnp.int32).reshape(1)

    col_names = ["conv_w_in", "attn_w_qkv", "ffn_w_up"]
    row_names = ["conv_w_out", "attn_w_o", "ffn_w_down"]
    local = dict(conv_w_in=conv_w_in, attn_w_qkv=attn_w_qkv, ffn_w_up=ffn_w_up, conv_w_out=conv_w_out, attn_w_o=attn_w_o,
                 ffn_w_down=ffn_w_down)
    gathered, (a_dw, b_dw, f_dw) = _all_gather_weights(
        [_place_shard(f"place_{n}", local[n], j_idx) for n in col_names + row_names], [conv_a_dw_w, conv_b_dw_w, ffn_dw_w])
    w_in, w_qkv, w_up = (g.reshape(g.shape[0], N_CHIPS, -1, g.shape[4]) for g in gathered[:3])
    w_out, w_o, w_down = (g.reshape(g.shape[0], -1, g.shape[4]) for g in gathered[3:])
    unshard = lambda a: jnp.moveaxis(a, 1, 2).reshape(a.shape[0], a.shape[2], N_CHIPS * a.shape[3])
    a_dw, b_dw, f_dw = unshard(a_dw), unshard(b_dw), unshard(f_dw)
    qk_gain = [jnp.stack([jnp.tile(attn_q_g[i], LANES // HEAD_DIM), jnp.tile(attn_k_g[i], LANES // HEAD_DIM)])
               for i in range(n_odd)]

    saved = []
    xc = x0
    for layer in range(depth):
        i = layer // 2
        tag = f"l{layer}"
        s = {"x_in": xc}
        h = _rms_fwd(f"rms_mix_fwd_{tag}", xc, mix_norm_g, layer)
        s["h"] = h
        if layer % 2 == 0:
            p = _mm_fwd(f"conv_in_fwd_{tag}", h, w_in, i, colshard=True)
            ab = _convmix_fwd(f"convmix_fwd_{tag}", p, a_dw, conv_a_dw_b, conv_a_ln_g, conv_a_ln_b, b_dw, i)
            xm = _mm_fwd(f"conv_out_fwd_{tag}", ab, w_out, i, colshard=False, res=xc)
            s.update(p=p, ab=ab)
        else:
            qkv = _mm_fwd(f"attn_qkv_fwd_{tag}", h, w_qkv, i, colshard=True)
            qs, kn, vb = _qknorm_fwd(f"qknorm_fwd_{tag}", qkv, qk_gain[i])
            o = _attn_fwd(f"attn_fwd_{tag}", qs, kn, vb)
            xm = _mm_fwd(f"attn_out_fwd_{tag}", o, w_o, i, colshard=False, res=xc)
            s.update(qkv=qkv, qs=qs, kn=kn, vb=vb, o=o)
        s["x_mid"] = xm
        h2 = _rms_fwd(f"rms_ffn_fwd_{tag}", xm, ffn_norm_g, layer)
        u2 = _mm_fwd(f"ffn_up_fwd_{tag}", h2, w_up, layer, colshard=True, out_split=2)
        f = _ffn_mid_fwd(f"ffn_mid_fwd_{tag}", u2, f_dw, ffn_dw_b, layer)
        xc = _mm_fwd(f"ffn_down_fwd_{tag}", f, w_down, layer, colshard=False, res=xm)
        s.update(h2=h2, u2=u2, f=f)
        saved.append(s)

    dx, loss_tile = _loss_fwd_bwd("loss", xc, target)

    g_up = g_down = g_in = g_out = g_qkv = g_o = None
    big_names = col_names + row_names

    def halves_view(n, g):
        if n in col_names:
            return g.reshape(g.shape[0], N_CHIPS, 2, g.shape[2] // 2, g.shape[3])
        return g.reshape(g.shape[0], N_CHIPS, 2, g.shape[1] // (2 * N_CHIPS), g.shape[2])

    early = {"conv_w_in": (1, n_even - 1), "attn_w_qkv": (0, n_odd), "ffn_w_up": (1, depth - 1),
             "conv_w_out": (1, n_even - 1), "attn_w_o": (0, n_odd), "ffn_w_down": (1, depth - 1)}
    late = {n: (0, 1) for n in ("conv_w_in", "ffn_w_up", "conv_w_out", "ffn_w_down")}
    early_sums = early_from_chips = None
    d_mix_g, d_ffn_g = [None] * depth, [None] * depth
    d_ffn_dw_w, d_ffn_dw_b = [None] * depth, [None] * depth
    d_a_dw_w, d_a_dw_b, d_a_ln_g, d_a_ln_b, d_b_dw_w = ([None] * n_even for _ in range(5))
    d_q_g, d_k_g = [None] * n_odd, [None] * n_odd
    for layer in reversed(range(depth)):
        i = layer // 2
        tag = f"l{layer}"
        s = saved[layer]
        df = _mm_dgrad(f"ffn_down_dgrad_{tag}", dx, w_down, layer, colshard=False)
        g_down = _mm_wgrad(f"ffn_down_wgrad_{tag}", s["f"], dx, layer, depth, g_down, colshard=False)
        big = {"conv_w_in": g_in, "attn_w_qkv": g_qkv, "ffn_w_up": g_up, "conv_w_out": g_out, "attn_w_o": g_o,
               "ffn_w_down": g_down}
        core_ex = None
        if layer == 0:
            core_ex = _core_halves_exchange([halves_view(n, big[n]) for n in early], list(early.values()))
        du2, dww, dwb, *early_from_sibling = _ffn_mid_bwd(f"ffn_mid_bwd_{tag}", s["u2"], df, f_dw, ffn_dw_b, layer, core_ex)
        d_ffn_dw_w[layer] = jnp.moveaxis(dww, 0, 1).reshape(FFN_CONV_WIDTH, -1)
        d_ffn_dw_b[layer] = dwb.reshape(-1)
        dh2 = _mm_dgrad(f"ffn_up_dgrad_{tag}", du2, w_up, layer, colshard=True)
        g_up = _mm_wgrad(f"ffn_up_wgrad_{tag}", s["h2"], du2, layer, depth, g_up, colshard=True)
        dx, dg_ = _rms_bwd(f"rms_ffn_bwd_{tag}", s["x_mid"], ffn_norm_g, layer, dh2, dx)
        d_ffn_g[layer] = dg_.reshape(-1)
        if layer % 2 == 0:
            dab = _mm_dgrad(f"conv_out_dgrad_{tag}", dx, w_out, i, colshard=False)
            g_out = _mm_wgrad(f"conv_out_wgrad_{tag}", s["ab"], dx, i, n_even, g_out, colshard=False)
            chip_ex = None
            if layer == 0:
                big = {"conv_w_in": g_in, "attn_w_qkv": g_qkv, "ffn_w_up": g_up, "conv_w_out": g_out, "attn_w_o": g_o,
                       "ffn_w_down": g_down}
                both = [_add_core_halves(f"grad_add_core_early_{n}", halves_view(n, big[n]), a, c_idx, early[n][0])
                        for n, a in zip(early, early_from_sibling)]
                early_sums = [b[0] for b in both]
                chip_ex = _chip_shards_exchange([b[1] for b in both])
            dp, daw, dab_b, dlg, dlb, dbw, *early_from_chips = _convmix_bwd(
                f"convmix_bwd_{tag}", s["p"], dab, a_dw, conv_a_dw_b, conv_a_ln_g, conv_a_ln_b, b_dw, i, chip_ex)
            d_a_dw_w[i], d_a_dw_b[i], d_a_ln_g[i], d_a_ln_b[i], d_b_dw_w[i] = (
                daw, dab_b.reshape(-1), dlg.reshape(-1), dlb.reshape(-1), dbw)
            dh = _mm_dgrad(f"conv_in_dgrad_{tag}", dp, w_in, i, colshard=True)
            g_in = _mm_wgrad(f"conv_in_wgrad_{tag}", s["h"], dp, i, n_even, g_in, colshard=True)
        else:
            do = _mm_dgrad(f"attn_out_dgrad_{tag}", dx, w_o, i, colshard=False)
            g_o = _mm_wgrad(f"attn_out_wgrad_{tag}", s["o"], dx, i, n_odd, g_o, colshard=False)
            dq, dk, dv = _attn_bwd(f"attn_bwd_{tag}", s["qs"], s["kn"], s["vb"], s["o"], do)
            dqkv, dgain = _qknorm_bwd(f"qknorm_bwd_{tag}", s["qkv"], dq, dk, dv, qk_gain[i])
            d_q_g[i] = dgain[0, :HEAD_DIM] + dgain[0, HEAD_DIM:]
            d_k_g[i] = dgain[1, :HEAD_DIM] + dgain[1, HEAD_DIM:]
            dh = _mm_dgrad(f"attn_qkv_dgrad_{tag}", dqkv, w_qkv, i, colshard=True)
            g_qkv = _mm_wgrad(f"attn_qkv_wgrad_{tag}", s["h"], dqkv, i, n_odd, g_qkv, colshard=True)
        dx, dg_ = _rms_bwd(f"rms_mix_bwd_{tag}", s["x_in"], mix_norm_g, layer, dh, dx)
        d_mix_g[layer] = dg_.reshape(-1)
    grad_x = dx.reshape(1, S, D)

    small = {
        "mix_norm_g": jnp.stack(d_mix_g), "ffn_norm_g": jnp.stack(d_ffn_g),
        "conv_a_dw_w": jnp.stack(d_a_dw_w), "conv_a_dw_b": jnp.stack(d_a_dw_b),
        "conv_a_ln_g": jnp.stack(d_a_ln_g), "conv_a_ln_b": jnp.stack(d_a_ln_b),
        "conv_b_dw_w": jnp.stack(d_b_dw_w), "attn_q_g": jnp.stack(d_q_g), "attn_k_g": jnp.stack(d_k_g),
        "ffn_dw_w": jnp.stack(d_ffn_dw_w), "ffn_dw_b": jnp.stack(d_ffn_dw_b),
    }
    small_names = list(small)
    summed = _all_reduce_small(_pack([loss_tile] + [small[n] for n in small_names]))
    parts = _unpack(summed, [loss_tile.shape] + [small[n].shape for n in small_names])
    loss = parts[0][0, 0]
    small_g = dict(zip(small_names, parts[1:]))
    for n in ("conv_a_dw_w", "conv_b_dw_w", "ffn_dw_w"):
        cs = small_g[n].shape[2] // N_CHIPS
        small_g[n] = lax.dynamic_slice_in_dim(small_g[n], j_me * cs, cs, axis=2)

    big = {"conv_w_in": g_in, "attn_w_qkv": g_qkv, "ffn_w_up": g_up, "conv_w_out": g_out, "attn_w_o": g_o, "ffn_w_down": g_down}
    views = {n: halves_view(n, big[n]) for n in big_names}
    late_from_sibling = _run_exchange("grad_exchange_core_halves",
                                      _core_halves_exchange([views[n] for n in late], list(late.values())))
    both = [_add_core_halves(f"grad_add_core_{n}", views[n], a, c_idx, late[n][0]) for n, a in zip(late, late_from_sibling)]
    late_sums = [b[0] for b in both]
    late_from_chips = _run_exchange("grad_exchange_chip_shards", _chip_shards_exchange([b[1] for b in both]))
    jc_idx = jnp.concatenate([j_idx, c_idx])
    totals = {}
    for n, p, b in zip(early, early_sums, early_from_chips):
        totals[n] = _add_chip_shards(f"grad_add_chips_early_{n}", p, b, jc_idx, early[n][0], big[n].shape[0], None)
    for n, p, b in zip(late, late_sums, late_from_chips):
        totals[n] = _add_chip_shards(f"grad_add_chips_{n}", p, b, jc_idx, late[n][0], big[n].shape[0], totals[n])
    big_g = dict(zip(big_names, _join_core_halves([totals[n] for n in big_names])))

    weights = dict(mix_norm_g=mix_norm_g, ffn_norm_g=ffn_norm_g, conv_w_in=conv_w_in, conv_a_dw_w=conv_a_dw_w, conv_a_dw_b=conv_a_dw_b, conv_a_ln_g=conv_a_ln_g, conv_a_ln_b=conv_a_ln_b, conv_b_dw_w=conv_b_dw_w, conv_w_out=conv_w_out, attn_w_qkv=attn_w_qkv, attn_q_g=attn_q_g, attn_k_g=attn_k_g, attn_w_o=attn_w_o, ffn_w_up=ffn_w_up, ffn_dw_w=ffn_dw_w, ffn_dw_b=ffn_dw_b, ffn_w_down=ffn_w_down)
    m_in = dict(mix_norm_g=m_mix_norm_g, ffn_norm_g=m_ffn_norm_g, conv_w_in=m_conv_w_in, conv_a_dw_w=m_conv_a_dw_w, conv_a_dw_b=m_conv_a_dw_b, conv_a_ln_g=m_conv_a_ln_g, conv_a_ln_b=m_conv_a_ln_b, conv_b_dw_w=m_conv_b_dw_w, conv_w_out=m_conv_w_out, attn_w_qkv=m_attn_w_qkv, attn_q_g=m_attn_q_g, attn_k_g=m_attn_k_g, attn_w_o=m_attn_w_o, ffn_w_up=m_ffn_w_up, ffn_dw_w=m_ffn_dw_w, ffn_dw_b=m_ffn_dw_b, ffn_w_down=m_ffn_w_down)
    v_in = dict(mix_norm_g=v_mix_norm_g, ffn_norm_g=v_ffn_norm_g, conv_w_in=v_conv_w_in, conv_a_dw_w=v_conv_a_dw_w, conv_a_dw_b=v_conv_a_dw_b, conv_a_ln_g=v_conv_a_ln_g, conv_a_ln_b=v_conv_a_ln_b, conv_b_dw_w=v_conv_b_dw_w, conv_w_out=v_conv_w_out, attn_w_qkv=v_attn_w_qkv, attn_q_g=v_attn_q_g, attn_k_g=v_attn_k_g, attn_w_o=v_attn_w_o, ffn_w_up=v_ffn_w_up, ffn_dw_w=v_ffn_dw_w, ffn_dw_b=v_ffn_dw_b, ffn_w_down=v_ffn_w_down)
    order = list(weights)
    grads, delta, new_m, new_v = {}, {}, {}, {}
    for n in big_names:
        grads[n] = big_g[n]
        delta[n], new_m[n], new_v[n] = _adamw(f"adamw_{n}", weights[n], big_g[n], m_in[n], v_in[n])
    shapes = [weights[n].shape for n in small_names]
    packed = [_pack([d[n] for n in small_names]) for d in (weights, small_g, m_in, v_in)]
    upd = _adamw("adamw_small", *[p[None] for p in packed])
    for out, res in zip((delta, new_m, new_v), upd):
        out.update(zip(small_names, _unpack(res[0], shapes)))
    grads.update({n: small_g[n].reshape(weights[n].shape) for n in small_names})
    return (loss, grad_x, *[grads[n] for n in order], *[delta[n] for n in order], *[new_m[n] for n in order],
            *[new_v[n] for n in order])
```

```python
import jax
import jax.numpy as jnp
from jax import lax
from jax.experimental import pallas as pl
from jax.experimental.pallas import tpu as pltpu

F32 = jnp.float32
BF16 = jnp.bfloat16
EPS = 1e-6
CONV_A_WIDTH = 31
CONV_B_WIDTH = 3
FFN_CONV_WIDTH = 3
HEAD_DIM = 64
ADAM_LR = 0.001
ADAM_B1 = 0.9
ADAM_B2 = 0.999
ADAM_EPS = 1e-08
ADAM_WD = 0.01
ADAM_STEP = 10

LANES = 128
SUBLANES = 8
BF16_ROWS = 16
V7X_VMEM_BYTES = 64 * 1024 * 1024
VMEM_LIMIT_BYTES = V7X_VMEM_BYTES * 3 // 4
MM_VMEM_BUDGET = VMEM_LIMIT_BYTES * 4 // 5
MM_ROWS = 1024
N_CHIPS = 4
N_DEV = 8
HALO_A = 32
HALO_S = 8
ELT_ROWS = 64
ATTN_BLOCK = 128
ATTN_SUB = 2
ATTN_MORE = 2
EXP_UNDERFLOW = -104.0
MESH = pl.DeviceIdType.MESH
ANY = pl.BlockSpec(memory_space=pl.ANY)
NT = (((1,), (1,)), ((), ()))
NN = (((1,), (0,)), ((), ()))
TN = (((0,), (0,)), ((), ()))


def _pcall(body, **kw):
    return pl.pallas_call(body, **kw)


def _cp(*sem):
    return pltpu.CompilerParams(dimension_semantics=sem, vmem_limit_bytes=VMEM_LIMIT_BYTES)


def _sds(shape, dtype):
    return jax.ShapeDtypeStruct(tuple(shape), dtype)


def _tile(n, cap, align=LANES):
    if n <= cap:
        return n
    for t in range(cap - cap % align, 0, -align):
        if n % t == 0:
            return t
    return n


def _sig(x):
    return 0.5 * jnp.tanh(0.5 * x) + 0.5


def _rowsum(x):
    return jnp.sum(x, axis=0, keepdims=True)


def _fold(x):
    acc = x[0:SUBLANES]
    for r in range(SUBLANES, x.shape[0], SUBLANES):
        acc = acc + x[r:r + SUBLANES]
    return acc


def _with_exchange(ex, body, in_specs, out_specs, out_shape, scratch, operands, first, last):
    if ex is None:
        return body, in_specs, out_specs, out_shape, scratch, operands, {}
    n_in, n_out, n_scr = len(in_specs), len(out_specs), len(scratch)
    e_in, e_out = len(ex.operands), len(ex.out_shapes)

    def hosted(*refs):
        refs = list(refs)
        ins, refs = refs[:n_in], refs[n_in:]
        e_ins, refs = refs[:e_in], refs[e_in:]
        outs, refs = refs[:n_out], refs[n_out:]
        e_outs, refs = refs[:e_out], refs[e_out:]
        scr, sems = refs[:n_scr], refs[n_scr:]

        @pl.when(first())
        def _():
            ex.start(e_ins, e_outs, sems)

        body(*ins, *outs, *scr)

        @pl.when(last())
        def _():
            ex.wait(e_ins, e_outs, sems)

    return (hosted, in_specs + [ANY] * e_in, out_specs + [ANY] * e_out, out_shape + ex.out_shapes, scratch + ex.scratch,
            operands + ex.operands, {n_in + i: n_out + o for i, o in ex.aliases.items()})


def _mm_call(name, dn, operands, in_specs, out_shape, out_spec, grid, nk, acc_shape, has_res, has_alias):
    def body(*refs):
        a_ref, b_ref = refs[0], refs[1]
        pos = 2
        res_ref = refs[pos] if has_res else None
        pos += int(has_res) + int(has_alias)
        o_ref = refs[pos]
        acc_ref = refs[pos + 1] if nk > 1 else None
        p = lax.dot_general(a_ref[...].astype(BF16), b_ref[...].astype(BF16), dn, preferred_element_type=F32)

        def finish(v):
            if has_res:
                v = v + res_ref[...]
            o_ref[...] = v.astype(o_ref.dtype)

        if nk == 1:
            finish(p)
        else:
            k = pl.program_id(2)

            @pl.when(k == 0)
            def _():
                acc_ref[...] = p

            @pl.when(k > 0)
            def _():
                acc_ref[...] += p

            @pl.when(k == nk - 1)
            def _():
                finish(acc_ref[...])

    aliases = {len(operands) - 1: 0} if has_alias else {}
    return _pcall(
        body, grid=grid, in_specs=in_specs, out_specs=out_spec, out_shape=out_shape,
        scratch_shapes=[pltpu.VMEM(acc_shape, F32)] if nk > 1 else [],
        input_output_aliases=aliases, compiler_params=_cp("parallel", "parallel", "arbitrary"), name=name,
    )(*operands)


def _mm_fwd(name, a, w, l, *, colshard, res=None, out_split=1):
    M, K = a.shape
    tm = _tile(M, MM_ROWS, BF16_ROWS)
    if colshard:
        cs = w.shape[3]
        N, tn, tk = N_CHIPS * cs, cs, K
        b_spec = pl.BlockSpec((None, None, tk, tn), lambda j, i, k: (l, j, k, 0))
    else:
        N = w.shape[2]
        tn, tk = _tile(N, 1024), _tile(K, 1536)
        b_spec = pl.BlockSpec((None, tk, tn), lambda j, i, k: (l, k, j))
    nk = K // tk
    in_specs = [pl.BlockSpec((tm, tk), lambda j, i, k: (i, k)), b_spec]
    operands = [a, w]
    if res is not None:
        in_specs.append(pl.BlockSpec((tm, tn), lambda j, i, k: (i, j)))
        operands.append(res)
    if out_split == 1:
        out_shape = _sds((M, N), F32)
        out_spec = pl.BlockSpec((tm, tn), lambda j, i, k: (i, j))
    else:
        per = N // tn // out_split
        out_shape = _sds((out_split, M, N // out_split), F32)
        out_spec = pl.BlockSpec((None, tm, tn), lambda j, i, k: (j // per, i, j % per))
    return _mm_call(name, NN, operands, in_specs, out_shape, out_spec, (N // tn, M // tm, nk), nk, (tm, tn),
                    res is not None, False)


def _mm_dgrad(name, g, w, l, *, colshard):
    split = g.ndim == 3
    M = g.shape[-2]
    tm = _tile(M, MM_ROWS, BF16_ROWS)
    if colshard:
        kw, cs = w.shape[2], w.shape[3]
        tn, tk, nk = _tile(kw, 1408), cs, N_CHIPS
        b_spec = pl.BlockSpec((None, None, tn, tk), lambda j, i, k: (l, k, j, 0))
    else:
        kw, ncon = w.shape[1], w.shape[2]
        tn, tk = _tile(kw, 1408), _tile(ncon, 1536)
        nk = ncon // tk
        b_spec = pl.BlockSpec((None, tn, tk), lambda j, i, k: (l, j, k))
    if split:
        per = nk // g.shape[0]
        a_spec = pl.BlockSpec((None, tm, tk), lambda j, i, k: (k // per, i, k % per))
    else:
        a_spec = pl.BlockSpec((tm, tk), lambda j, i, k: (i, k))
    out_shape = _sds((M, kw), F32)
    out_spec = pl.BlockSpec((tm, tn), lambda j, i, k: (i, j))
    return _mm_call(name, NT, [g, w], [a_spec, b_spec], out_shape, out_spec, (kw // tn, M // tm, nk), nk, (tm, tn),
                    False, False)


def _mm_wgrad(name, a, g, l, n_layers, buf, *, colshard):
    S, M = a.shape
    split = g.ndim == 3
    N = g.shape[-1] * (g.shape[0] if split else 1)
    tm = _tile(M, 1408)
    tn = N // N_CHIPS if colshard else _tile(N, 1024)
    per_row = 2 * (tm * a.dtype.itemsize + tn * g.dtype.itemsize)
    tk = _tile(S, max(BF16_ROWS, min(2048, (MM_VMEM_BUDGET - 3 * tm * tn * 4) // per_row)), BF16_ROWS)
    nk = S // tk
    if colshard:
        out_shape = _sds((n_layers, N_CHIPS, M, tn), F32)
        out_spec = pl.BlockSpec((None, None, tm, tn), lambda j, i, k: (l, j, i, 0))
    else:
        out_shape = _sds((n_layers, M, N), F32)
        out_spec = pl.BlockSpec((None, tm, tn), lambda j, i, k: (l, i, j))
    if split:
        per = N // tn // g.shape[0]
        b_spec = pl.BlockSpec((None, tk, tn), lambda j, i, k: (j // per, k, j % per))
    else:
        b_spec = pl.BlockSpec((tk, tn), lambda j, i, k: (k, j))
    in_specs = [pl.BlockSpec((tk, tm), lambda j, i, k: (k, i)), b_spec]
    operands = [a, g]
    if buf is not None:
        in_specs.append(ANY)
        operands.append(buf)
    return _mm_call(name, TN, operands, in_specs, out_shape, out_spec, (N // tn, M // tm, nk), nk, (tm, tn),
                    False, buf is not None)


def _rms_fwd(name, x, g, l):
    S, D = x.shape
    tm = _tile(S, 512, BF16_ROWS)

    def body(x_ref, g_ref, o_ref):
        xf = x_ref[...]
        r = lax.rsqrt(jnp.mean(xf * xf, axis=-1, keepdims=True) + EPS)
        o_ref[...] = (xf * r * g_ref[l:l + 1, :]).astype(BF16)

    return _pcall(
        body, grid=(S // tm,),
        in_specs=[pl.BlockSpec((tm, D), lambda i: (i, 0)), pl.BlockSpec(g.shape, lambda i: (0, 0))],
        out_specs=pl.BlockSpec((tm, D), lambda i: (i, 0)), out_shape=_sds((S, D), BF16),
        compiler_params=_cp("parallel"), name=name,
    )(x, g)


def _rms_bwd(name, x, g, l, dh, dres):
    S, D = x.shape
    tm = _tile(S, 512, SUBLANES)

    def body(x_ref, g_ref, dh_ref, dr_ref, dx_ref, dg_ref):
        xf = x_ref[...]
        r = lax.rsqrt(jnp.mean(xf * xf, axis=-1, keepdims=True) + EPS)
        xh = xf * r
        d = dh_ref[...]
        dxh = d * g_ref[l:l + 1, :]
        dx_ref[...] = dr_ref[...] + r * (dxh - xh * jnp.mean(dxh * xh, axis=-1, keepdims=True))

        @pl.when(pl.program_id(0) == 0)
        def _():
            dg_ref[...] = jnp.zeros_like(dg_ref)

        dg_ref[...] += _rowsum(d * xh)

    row = pl.BlockSpec((tm, D), lambda i: (i, 0))
    return _pcall(
        body, grid=(S // tm,),
        in_specs=[row, pl.BlockSpec(g.shape, lambda i: (0, 0)), row, row],
        out_specs=[row, pl.BlockSpec((1, D), lambda i: (0, 0))],
        out_shape=[_sds((S, D), F32), _sds((1, D), F32)],
        compiler_params=_cp("arbitrary"), name=name,
    )(x, g, dh, dres)


def _loss_fwd_bwd(name, y, t):
    S, D = y.shape
    tm = _tile(S, 512, SUBLANES)

    def body(y_ref, t_ref, dy_ref, l_ref):
        e = y_ref[...] - t_ref[...]
        dy_ref[...] = e * (1.0 / D)

        @pl.when(pl.program_id(0) == 0)
        def _():
            l_ref[...] = jnp.zeros_like(l_ref)

        l_ref[...] += 0.5 * jnp.sum(jnp.sum(e * e, axis=-1, keepdims=True) * (1.0 / D), axis=0, keepdims=True)

    row = pl.BlockSpec((tm, D), lambda i: (i, 0))
    return _pcall(
        body, grid=(S // tm,), in_specs=[row, row],
        out_specs=[row, pl.BlockSpec((SUBLANES, LANES), lambda i: (0, 0))],
        out_shape=[_sds((S, D), F32), _sds((SUBLANES, LANES), F32)],
        compiler_params=_cp("arbitrary"), name=name,
    )(y, t)


def _delayed_copies(us, n_rows):
    for s in range(1, SUBLANES):
        us[s, pl.ds(SUBLANES, n_rows - SUBLANES), :] = us[0, pl.ds(SUBLANES - s, n_rows - SUBLANES), :]


def _conv_a(aw_ref, ab_ref, l, us, row0, rows, dg):
    ka = CONV_A_WIDTH
    out = []
    for c0 in range(0, dg, LANES):
        lanes = slice(c0, c0 + LANES)
        acc = ab_ref[l:l + 1, lanes]
        for d in range(ka):
            a, s = divmod(d, SUBLANES)
            acc = acc + aw_ref[l, ka - 1 - d:ka - d, lanes] * us[s, pl.ds(row0 - SUBLANES * a, rows), lanes]
        out.append(acc)
    return jnp.concatenate(out, axis=1)


def _convmix_fwd(name, p, aw, ab, lg, lb, bw, l, exchange=None):
    S, W = p.shape
    dg = W // 5
    tm = _tile(S, 256, HALO_A)
    nb = tm // HALO_A
    ka, kb = CONV_A_WIDTH, CONV_B_WIDTH

    ext = HALO_A + tm
    rc = _tile(tm, ELT_ROWS, BF16_ROWS)

    def body(p_ref, ph_ref, aw_ref, ab_ref, lg_ref, lb_ref, bw_ref, o_ref, us, mext):
        first = pl.program_id(0) == 0
        ph = ph_ref[...]
        pc = p_ref[...]
        us[0, pl.ds(0, HALO_A), :] = jnp.where(first, 0.0, ph[:, 0:dg] * _sig(ph[:, dg:2 * dg]))
        us[0, pl.ds(HALO_A, tm), :] = pc[:, 0:dg] * _sig(pc[:, dg:2 * dg])
        mext[pl.ds(0, HALO_A), :] = jnp.where(first, 0.0, ph[:, 3 * dg:4 * dg] * ph[:, 4 * dg:5 * dg])
        mext[pl.ds(HALO_A, tm), :] = pc[:, 3 * dg:4 * dg] * pc[:, 4 * dg:5 * dg]
        _delayed_copies(us, ext)
        for r0 in range(0, tm, rc):
            rows = pl.ds(r0, rc)
            c = _conv_a(aw_ref, ab_ref, l, us, HALO_A + r0, rc, dg)
            xc = c - jnp.mean(c, axis=-1, keepdims=True)
            ln = xc * lax.rsqrt(jnp.mean(xc * xc, axis=-1, keepdims=True) + EPS) * lg_ref[l:l + 1, :] + lb_ref[l:l + 1, :]
            o_ref[rows, 0:dg] = (ln * _sig(ln)).astype(BF16)
            cb = bw_ref[l, 0:1, :] * mext[pl.ds(HALO_A - (kb - 1) + r0, rc), :]
            for k in range(1, kb):
                cb = cb + bw_ref[l, k:k + 1, :] * mext[pl.ds(HALO_A - (kb - 1) + k + r0, rc), :]
            o_ref[rows, dg:2 * dg] = (p_ref[rows, 2 * dg:3 * dg] * cb).astype(BF16)

    full = lambda a: pl.BlockSpec(a.shape, lambda i: (0,) * a.ndim)
    n_i = S // tm
    body, in_specs, out_specs, out_shape, scratch, operands, aliases = _with_exchange(
        exchange, body,
        [pl.BlockSpec((tm, W), lambda i: (i, 0)), pl.BlockSpec((HALO_A, W), lambda i: (jnp.maximum(i * nb - 1, 0), 0)),
         full(aw), full(ab), full(lg), full(lb), full(bw)],
        [pl.BlockSpec((tm, 2 * dg), lambda i: (i, 0))], [_sds((S, 2 * dg), BF16)],
        [pltpu.VMEM((SUBLANES, ext, dg), F32), pltpu.VMEM((ext, dg), F32)], [p, p, aw, ab, lg, lb, bw],
        lambda: pl.program_id(0) == 0, lambda: pl.program_id(0) == n_i - 1)
    outs = _pcall(
        body, grid=(n_i,), in_specs=in_specs, out_specs=out_specs, out_shape=out_shape, scratch_shapes=scratch,
        input_output_aliases=aliases, compiler_params=_cp("arbitrary" if exchange else "parallel"), name=name,
    )(*operands)
    return outs if exchange else outs[0]


def _convmix_bwd(name, p, dab, aw, ab, lg, lb, bw, l, exchange=None):
    S, W = p.shape
    dg = W // 5
    tm = _tile(S, 256, HALO_A)
    nb = tm // HALO_A
    n_i = S // tm
    ka, kb = CONV_A_WIDTH, CONV_B_WIDTH
    n = tm + HALO_A
    ext = HALO_A + n
    rc = _tile(tm, ELT_ROWS, BF16_ROWS)

    def body(p_ref, pp_ref, pn_ref, d_ref, dn_ref, aw_ref, ab_ref, lg_ref, lb_ref, bw_ref,
             dp_ref, daw_ref, dab_ref, dlg_ref, dlb_ref, dbw_ref, us, mext, dcs, dbext, accw):
        i = pl.program_id(0)
        first, last = i == 0, i == n_i - 1

        @pl.when(first)
        def _():
            for r in (daw_ref, dab_ref, dlg_ref, dlb_ref, dbw_ref):
                r[...] = jnp.zeros_like(r)

        accw[...] = jnp.zeros_like(accw)
        pp, pc, pn = pp_ref[...], p_ref[...], pn_ref[...]
        glu = lambda b: b[:, 0:dg] * _sig(b[:, dg:2 * dg])
        gch = lambda b: b[:, 3 * dg:4 * dg] * b[:, 4 * dg:5 * dg]
        us[0, pl.ds(0, HALO_A), :] = jnp.where(first, 0.0, glu(pp))
        us[0, pl.ds(HALO_A, tm), :] = glu(pc)
        us[0, pl.ds(HALO_A + tm, HALO_A), :] = glu(pn)
        mext[pl.ds(0, HALO_A), :] = jnp.where(first, 0.0, gch(pp))
        mext[pl.ds(HALO_A, tm), :] = gch(pc)
        mext[pl.ds(HALO_A + tm, HALO_A), :] = gch(pn)
        _delayed_copies(us, ext)
        chunks = [(r, rc) for r in range(0, tm, rc)] + [(tm, HALO_A)]
        g_ln = lg_ref[l:l + 1, :]
        zero8 = jnp.zeros((SUBLANES, dg), F32)

        acc_lg = acc_lb = acc_ab = zero8
        for r0, rows in chunks:
            c = _conv_a(aw_ref, ab_ref, l, us, HALO_A + r0, rows, dg)
            xc = c - jnp.mean(c, axis=-1, keepdims=True)
            rstd = lax.rsqrt(jnp.mean(xc * xc, axis=-1, keepdims=True) + EPS)
            chat = xc * rstd
            ln = chat * g_ln + lb_ref[l:l + 1, :]
            s = _sig(ln)
            da = d_ref[pl.ds(r0, rows), 0:dg] if r0 < tm else jnp.where(last, 0.0, dn_ref[:, 0:dg])
            dln = da * (s * (1.0 + ln * (1.0 - s)))
            dlnh = dln * g_ln
            dc = rstd * (dlnh - jnp.mean(dlnh, axis=-1, keepdims=True)
                         - chat * jnp.mean(dlnh * chat, axis=-1, keepdims=True))
            dcs[0, pl.ds(r0, rows), :] = dc
            if r0 < tm:
                acc_lg = acc_lg + _fold(dln * chat)
                acc_lb = acc_lb + _fold(dln)
                acc_ab = acc_ab + _fold(dc)
                for c0 in range(0, dg, LANES):
                    lanes = slice(c0, c0 + LANES)
                    for d in range(ka):
                        a, sh = divmod(d, SUBLANES)
                        k = ka - 1 - d
                        accw[pl.ds(SUBLANES * k, SUBLANES), lanes] += _fold(
                            dc[:, lanes] * us[sh, pl.ds(HALO_A + r0 - SUBLANES * a, rows), lanes])
        dlg_ref[...] += _rowsum(acc_lg)
        dlb_ref[...] += _rowsum(acc_lb)
        dab_ref[...] += _rowsum(acc_ab)
        for k in range(ka):
            daw_ref[k:k + 1, :] += _rowsum(accw[pl.ds(SUBLANES * k, SUBLANES), :])
        for s in range(1, SUBLANES):
            dcs[s, pl.ds(0, n - SUBLANES), :] = dcs[0, pl.ds(s, n - SUBLANES), :]
        for r0 in range(0, tm, rc):
            rows = pl.ds(r0, rc)
            parts = []
            for c0 in range(0, dg, LANES):
                lanes = slice(c0, c0 + LANES)
                acc = aw_ref[l, ka - 1:ka, lanes] * dcs[0, rows, lanes]
                for e in range(1, ka):
                    a, sh = divmod(e, SUBLANES)
                    acc = acc + aw_ref[l, ka - 1 - e:ka - e, lanes] * dcs[sh, pl.ds(r0 + SUBLANES * a, rc), lanes]
                parts.append(acc)
            du = jnp.concatenate(parts, axis=1)
            sg = _sig(p_ref[rows, dg:2 * dg])
            dp_ref[rows, 0:dg] = (du * sg).astype(BF16)
            dp_ref[rows, dg:2 * dg] = (du * p_ref[rows, 0:dg] * sg * (1.0 - sg)).astype(BF16)

        for r0, rows in chunks:
            if r0 < tm:
                dbext[pl.ds(r0, rows), :] = d_ref[pl.ds(r0, rows), dg:2 * dg] * p_ref[pl.ds(r0, rows), 2 * dg:3 * dg]
            else:
                dbext[pl.ds(r0, rows), :] = jnp.where(last, 0.0, dn_ref[:, dg:2 * dg] * pn[:, 2 * dg:3 * dg])
        acc_bw = [zero8] * kb
        for r0 in range(0, tm, rc):
            rows = pl.ds(r0, rc)
            m_k = [mext[pl.ds(HALO_A - (kb - 1) + k + r0, rc), :] for k in range(kb)]
            cb = bw_ref[l, 0:1, :] * m_k[0]
            dm = bw_ref[l, 0:1, :] * dbext[pl.ds(r0 + kb - 1, rc), :]
            for k in range(1, kb):
                cb = cb + bw_ref[l, k:k + 1, :] * m_k[k]
                dm = dm + bw_ref[l, k:k + 1, :] * dbext[pl.ds(r0 + kb - 1 - k, rc), :]
            dcb = dbext[rows, :]
            acc_bw = [acc_bw[k] + _fold(dcb * m_k[k]) for k in range(kb)]
            dp_ref[rows, 2 * dg:3 * dg] = (d_ref[rows, dg:2 * dg] * cb).astype(BF16)
            dp_ref[rows, 3 * dg:4 * dg] = (dm * p_ref[rows, 4 * dg:5 * dg]).astype(BF16)
            dp_ref[rows, 4 * dg:5 * dg] = (dm * p_ref[rows, 3 * dg:4 * dg]).astype(BF16)
        for k in range(kb):
            dbw_ref[k:k + 1, :] += _rowsum(acc_bw[k])

    full = lambda a: pl.BlockSpec(a.shape, lambda i: (0,) * a.ndim)
    prev = lambda i: (jnp.maximum(i * nb - 1, 0), 0)
    nxt = lambda i: (jnp.minimum((i + 1) * nb, S // HALO_A - 1), 0)
    acc = lambda r: pl.BlockSpec((r, dg), lambda i: (0, 0))
    body, in_specs, out_specs, out_shape, scratch, operands, aliases = _with_exchange(
        exchange, body,
        [pl.BlockSpec((tm, W), lambda i: (i, 0)), pl.BlockSpec((HALO_A, W), prev), pl.BlockSpec((HALO_A, W), nxt),
         pl.BlockSpec((tm, 2 * dg), lambda i: (i, 0)), pl.BlockSpec((HALO_A, 2 * dg), nxt),
         full(aw), full(ab), full(lg), full(lb), full(bw)],
        [pl.BlockSpec((tm, W), lambda i: (i, 0)), acc(ka), acc(1), acc(1), acc(1), acc(kb)],
        [_sds((S, W), BF16), _sds((ka, dg), F32), _sds((1, dg), F32), _sds((1, dg), F32), _sds((1, dg), F32),
         _sds((kb, dg), F32)],
        [pltpu.VMEM((SUBLANES, ext, dg), F32), pltpu.VMEM((ext, dg), F32), pltpu.VMEM((SUBLANES, n, dg), F32),
         pltpu.VMEM((n, dg), F32), pltpu.VMEM((SUBLANES * ka, dg), F32)],
        [p, p, p, dab, dab, aw, ab, lg, lb, bw],
        lambda: pl.program_id(0) == 0, lambda: pl.program_id(0) == n_i - 1)
    return _pcall(
        body, grid=(n_i,), in_specs=in_specs, out_specs=out_specs, out_shape=out_shape, scratch_shapes=scratch,
        input_output_aliases=aliases, compiler_params=_cp("arbitrary"), name=name,
    )(*operands)


def _ffn_mid_fwd(name, u2, dww, dwb, l, exchange=None):
    _, S, F = u2.shape
    tm = _tile(S, 256, BF16_ROWS)
    tc = _tile(F, 1408)
    n_f = F // tc
    nb = tm // HALO_S
    kf = FFN_CONV_WIDTH

    def body(u_ref, uh_ref, wg_ref, wv_ref, bg_ref, bv_ref, o_ref, ext):
        first = pl.program_id(1) == 0
        ext[:, pl.ds(0, HALO_S), :] = jnp.where(first, 0.0, uh_ref[...])
        ext[:, pl.ds(HALO_S, tm), :] = u_ref[...]
        rc = _tile(tm, ELT_ROWS, BF16_ROWS)

        def lane_chunk(ci, carry):
            lanes = pl.ds(pl.multiple_of(ci * LANES, LANES), LANES)
            taps = [[w_ref[k:k + 1, lanes] for k in range(kf)] for w_ref in (wg_ref, wv_ref)]
            bias = [b_ref[l:l + 1, lanes] for b_ref in (bg_ref, bv_ref)]
            for r0 in range(0, tm, rc):
                c = []
                for g in range(2):
                    acc = bias[g]
                    for k in range(kf):
                        acc = acc + taps[g][k] * ext[g, pl.ds(HALO_S - (kf - 1) + k + r0, rc), lanes]
                    c.append(acc)
                o_ref[pl.ds(r0, rc), lanes] = (c[0] * _sig(c[0]) * c[1]).astype(BF16)
            return carry

        lax.fori_loop(0, tc // LANES, lane_chunk, 0)

    n_l = dwb.shape[0]
    n_i = S // tm
    body, in_specs, out_specs, out_shape, scratch, operands, aliases = _with_exchange(
        exchange, body,
        [pl.BlockSpec((2, tm, tc), lambda j, i: (0, i, j)),
         pl.BlockSpec((2, HALO_S, tc), lambda j, i: (0, jnp.maximum(i * nb - 1, 0), j)),
         pl.BlockSpec((None, kf, tc), lambda j, i: (l, 0, j)),
         pl.BlockSpec((None, kf, tc), lambda j, i: (l, 0, j + n_f)),
         pl.BlockSpec((n_l, tc), lambda j, i: (0, j)),
         pl.BlockSpec((n_l, tc), lambda j, i: (0, j + n_f))],
        [pl.BlockSpec((tm, tc), lambda j, i: (i, j))], [_sds((S, F), BF16)],
        [pltpu.VMEM((2, HALO_S + tm, tc), F32)], [u2, u2, dww, dww, dwb, dwb],
        lambda: jnp.logical_and(pl.program_id(0) == 0, pl.program_id(1) == 0),
        lambda: jnp.logical_and(pl.program_id(0) == n_f - 1, pl.program_id(1) == n_i - 1))
    sem = "arbitrary" if exchange else "parallel"
    outs = _pcall(
        body, grid=(n_f, n_i), in_specs=in_specs, out_specs=out_specs, out_shape=out_shape, scratch_shapes=scratch,
        input_output_aliases=aliases, compiler_params=_cp(sem, sem), name=name,
    )(*operands)
    return outs if exchange else outs[0]


def _ffn_mid_bwd(name, u2, df, dww, dwb, l, exchange=None):
    _, S, F = u2.shape
    tm = _tile(S, 256, BF16_ROWS)
    tc = _tile(F, 1408)
    n_f = F // tc
    nb = tm // HALO_S
    n_i = S // tm
    kf = FFN_CONV_WIDTH
    n = tm + HALO_S

    def body(u_ref, up_ref, un_ref, df_ref, dfn_ref, wg_ref, wv_ref, bg_ref, bv_ref,
             du_ref, dw_ref, db_ref, uext, dcext):
        i = pl.program_id(1)
        first, last = i == 0, i == n_i - 1

        @pl.when(first)
        def _():
            dw_ref[...] = jnp.zeros_like(dw_ref)
            db_ref[...] = jnp.zeros_like(db_ref)

        uext[:, pl.ds(0, HALO_S), :] = jnp.where(first, 0.0, up_ref[...])
        uext[:, pl.ds(HALO_S, tm), :] = u_ref[...]
        uext[:, pl.ds(HALO_S + tm, HALO_S), :] = un_ref[...]
        rc = _tile(tm, ELT_ROWS, BF16_ROWS)

        def lane_chunk(ci, carry):
            lanes = pl.ds(pl.multiple_of(ci * LANES, LANES), LANES)
            taps = [[w_ref[k:k + 1, lanes] for k in range(kf)] for w_ref in (wg_ref, wv_ref)]
            bias = [b_ref[l:l + 1, lanes] for b_ref in (bg_ref, bv_ref)]
            acc_w = [[jnp.zeros((SUBLANES, LANES), F32) for _ in range(kf)] for _ in range(2)]
            acc_b = [jnp.zeros((SUBLANES, LANES), F32) for _ in range(2)]
            for r0, rows in [(r, rc) for r in range(0, tm, rc)] + [(tm, HALO_S)]:
                shifted = [[uext[g, pl.ds(HALO_S - (kf - 1) + k + r0, rows), lanes] for k in range(kf)] for g in range(2)]
                conv = []
                for g in range(2):
                    acc = bias[g]
                    for k in range(kf):
                        acc = acc + taps[g][k] * shifted[g][k]
                    conv.append(acc)
                cg, cv = conv
                s = _sig(cg)
                dfe = df_ref[pl.ds(r0, rows), lanes] if r0 < tm else jnp.where(last, 0.0, dfn_ref[:, lanes])
                dc = [dfe * cv * (s * (1.0 + cg * (1.0 - s))), dfe * (cg * s)]
                for g in range(2):
                    dcext[g, pl.ds(r0, rows), lanes] = dc[g]
                    if r0 < tm:
                        acc_b[g] = acc_b[g] + _fold(dc[g])
                        for k in range(kf):
                            acc_w[g][k] = acc_w[g][k] + _fold(dc[g] * shifted[g][k])
            for r0 in range(0, tm, rc):
                for g in range(2):
                    du = taps[g][0] * dcext[g, pl.ds(r0 + kf - 1, rc), lanes]
                    for k in range(1, kf):
                        du = du + taps[g][k] * dcext[g, pl.ds(r0 + kf - 1 - k, rc), lanes]
                    du_ref[g, pl.ds(r0, rc), lanes] = du.astype(BF16)
            for g in range(2):
                db_ref[g, :, lanes] += _rowsum(acc_b[g])
                for k in range(kf):
                    dw_ref[g, k:k + 1, lanes] += _rowsum(acc_w[g][k])
            return carry

        lax.fori_loop(0, tc // LANES, lane_chunk, 0)

    n_l = dwb.shape[0]
    prev = lambda j, i: (0, jnp.maximum(i * nb - 1, 0), j)
    nxt = lambda j, i: (0, jnp.minimum((i + 1) * nb, S // HALO_S - 1), j)
    body, in_specs, out_specs, out_shape, scratch, operands, aliases = _with_exchange(
        exchange, body,
        [pl.BlockSpec((2, tm, tc), lambda j, i: (0, i, j)),
         pl.BlockSpec((2, HALO_S, tc), prev), pl.BlockSpec((2, HALO_S, tc), nxt),
         pl.BlockSpec((tm, tc), lambda j, i: (i, j)),
         pl.BlockSpec((HALO_S, tc), lambda j, i: nxt(j, i)[1:]),
         pl.BlockSpec((None, kf, tc), lambda j, i: (l, 0, j)),
         pl.BlockSpec((None, kf, tc), lambda j, i: (l, 0, j + n_f)),
         pl.BlockSpec((n_l, tc), lambda j, i: (0, j)),
         pl.BlockSpec((n_l, tc), lambda j, i: (0, j + n_f))],
        [pl.BlockSpec((2, tm, tc), lambda j, i: (0, i, j)),
         pl.BlockSpec((2, kf, tc), lambda j, i: (0, 0, j)),
         pl.BlockSpec((2, 1, tc), lambda j, i: (0, 0, j))],
        [_sds((2, S, F), BF16), _sds((2, kf, F), F32), _sds((2, 1, F), F32)],
        [pltpu.VMEM((2, HALO_S + n, tc), F32), pltpu.VMEM((2, n, tc), F32)],
        [u2, u2, u2, df, df, dww, dww, dwb, dwb],
        lambda: jnp.logical_and(pl.program_id(0) == 0, pl.program_id(1) == 0),
        lambda: jnp.logical_and(pl.program_id(0) == n_f - 1, pl.program_id(1) == n_i - 1))
    return _pcall(
        body, grid=(n_f, n_i), in_specs=in_specs, out_specs=out_specs, out_shape=out_shape, scratch_shapes=scratch,
        input_output_aliases=aliases, compiler_params=_cp("arbitrary" if exchange else "parallel", "arbitrary"), name=name,
    )(*operands)


def _head_sum_matrix():
    r = lax.broadcasted_iota(jnp.int32, (LANES, LANES), 0) // HEAD_DIM
    c = lax.broadcasted_iota(jnp.int32, (LANES, LANES), 1) // HEAD_DIM
    return (r == c).astype(BF16)


def _head_mean(x, ones):
    return _split_dot(x, ones) * (1.0 / HEAD_DIM)


def _qknorm_fwd(name, qkv, g2):
    S, D3 = qkv.shape
    D = D3 // 3
    tm = _tile(S, 256, BF16_ROWS)
    scale = HEAD_DIM ** -0.5

    def body(q_ref, k_ref, v_ref, g_ref, qo_ref, ko_ref, vo_ref):
        ones = _head_sum_matrix()
        for cc in range(D // LANES):
            sl = slice(cc * LANES, (cc + 1) * LANES)
            for x_ref, o_ref, row, mult in ((q_ref, qo_ref, 0, scale), (k_ref, ko_ref, 1, 1.0)):
                x = x_ref[:, sl]
                r = lax.rsqrt(_head_mean(x * x, ones) + EPS)
                o_ref[:, sl] = ((x * r * g_ref[row:row + 1, :]).astype(BF16) * mult).astype(BF16)
        vo_ref[...] = v_ref[...].astype(BF16)

    col = lambda c: pl.BlockSpec((tm, D), lambda i: (i, c))
    out = pl.BlockSpec((tm, D), lambda i: (i, 0))
    return _pcall(
        body, grid=(S // tm,),
        in_specs=[col(0), col(1), col(2), pl.BlockSpec(g2.shape, lambda i: (0, 0))],
        out_specs=[out, out, out], out_shape=[_sds((S, D), BF16)] * 3,
        compiler_params=_cp("parallel"), name=name,
    )(qkv, qkv, qkv, g2)


def _qknorm_bwd(name, qkv, dq, dk, dv, g2):
    S, D3 = qkv.shape
    D = D3 // 3
    tm = _tile(S, 256, BF16_ROWS)
    scale = HEAD_DIM ** -0.5

    def body(q_ref, k_ref, dq_ref, dk_ref, dv_ref, g_ref, o_ref, dg_ref):
        @pl.when(pl.program_id(0) == 0)
        def _():
            dg_ref[...] = jnp.zeros_like(dg_ref)

        ones = _head_sum_matrix()
        for cc in range(D // LANES):
            sl = slice(cc * LANES, (cc + 1) * LANES)
            for x_ref, d_ref, row, mult, base in ((q_ref, dq_ref, 0, scale, 0), (k_ref, dk_ref, 1, 1.0, D)):
                x = x_ref[:, sl]
                r = lax.rsqrt(_head_mean(x * x, ones) + EPS)
                xh = x * r
                dn = d_ref[:, sl] * mult
                dxh = dn * g_ref[row:row + 1, :]
                dx = r * (dxh - xh * _head_mean(dxh * xh, ones))
                o_ref[:, base + cc * LANES:base + (cc + 1) * LANES] = dx.astype(BF16)
                dg_ref[row:row + 1, :] += _rowsum(dn * xh)
        o_ref[:, 2 * D:3 * D] = dv_ref[...].astype(BF16)

    col = lambda c: pl.BlockSpec((tm, D), lambda i: (i, c))
    row = pl.BlockSpec((tm, D), lambda i: (i, 0))
    return _pcall(
        body, grid=(S // tm,),
        in_specs=[col(0), col(1), row, row, row, pl.BlockSpec(g2.shape, lambda i: (0, 0))],
        out_specs=[pl.BlockSpec((tm, D3), lambda i: (i, 0)), pl.BlockSpec((2, LANES), lambda i: (0, 0))],
        out_shape=[_sds((S, D3), BF16), _sds((2, LANES), F32)],
        compiler_params=_cp("arbitrary"), name=name,
    )(qkv, qkv, dq, dk, dv, g2)


def _attn_consts():
    t = ATTN_BLOCK
    row = lax.broadcasted_iota(jnp.int32, (t, t), 0)
    col = lax.broadcasted_iota(jnp.int32, (t, t), 1)
    lane = lax.broadcasted_iota(jnp.int32, (1, LANES), 1)
    heads = (lane < HEAD_DIM, lane >= HEAD_DIM)
    return row, col, heads


def _split_dot(x, m):
    n = x.shape[0]
    hi = x.astype(BF16)
    lo = (x - hi.astype(F32)).astype(BF16)
    both = jnp.dot(jnp.concatenate([hi, lo], axis=0), m, preferred_element_type=F32)
    return both[:n] + both[n:]


def _log_keep(z):
    return -(jnp.maximum(z, 0.0) + jnp.log(1.0 + jnp.exp(-jnp.abs(z))))


def _stack_heads(a, heads):
    t = ATTN_BLOCK
    zero = jnp.zeros((t, LANES), a.dtype)
    return jnp.concatenate([jnp.where(h, a[s * t:(s + 1) * t], zero) for s in range(a.shape[0] // t) for h in heads], axis=0)


def _side_by_side(a):
    t = ATTN_BLOCK
    return jnp.concatenate([jnp.concatenate([a[2 * s * t:(2 * s + 1) * t], a[(2 * s + 1) * t:(2 * s + 2) * t]], axis=1)
                            for s in range(a.shape[0] // (2 * t))], axis=0)


def _grow(a, rows, cols):
    z = jnp.zeros((rows, cols), F32)
    return z if a is None else jnp.concatenate([z, a], axis=0)


def _attn_fwd(name, qs, kn, vb, exchange=None):
    S, D = qs.shape
    t = ATTN_BLOCK
    tq = ATTN_SUB * t

    def body(q_ref, k_ref, v_ref, o_ref):
        i = pl.program_id(1)
        row, col, heads = _attn_consts()
        after_m = (row > col).astype(BF16)
        causal = col < row
        q_all = _stack_heads(q_ref[...], heads)

        def block(j, q, r, acc, mask):
            off = pl.multiple_of(j * t, t)
            kb = k_ref[pl.ds(off, t), :]
            v2 = _stack_heads(v_ref[pl.ds(off, t), :], heads)
            z = lax.dot_general(q, kb, NT, preferred_element_type=F32)
            lk = _log_keep(z)
            if mask is not None:
                lk = jnp.where(mask, lk, 0.0)
            w = jnp.exp(z + lk + _split_dot(lk, after_m) + r)
            if mask is not None:
                w = jnp.where(mask, w, 0.0)
            acc = acc + jnp.dot(_side_by_side(w.astype(BF16)), v2, preferred_element_type=F32)
            return r + jnp.sum(lk, axis=1, keepdims=True), acc

        def head(n_more):
            r = acc = None
            for s in reversed(range(ATTN_SUB)):
                mask = jnp.concatenate([causal, causal] + [jnp.ones_like(causal)] * (2 * (ATTN_SUB - 1 - s)), axis=0)
                r, acc = block(ATTN_SUB * i + s, q_all[2 * s * t:], _grow(r, 2 * t, 1), _grow(acc, t, LANES), mask)
            for b in range(n_more):
                r, acc = block(ATTN_SUB * i - 1 - b, q_all, r, acc, None)
            return r, acc

        r, acc = lax.cond(ATTN_SUB * i >= ATTN_MORE, lambda: head(ATTN_MORE), lambda: head(0))

        def cond(c):
            return jnp.logical_and(c[0] >= 0, jnp.max(c[1]) > EXP_UNDERFLOW)

        def step(c):
            r, a = block(c[0], q_all, c[1], c[2], None)
            return c[0] - 1, r, a

        first = jnp.where(ATTN_SUB * i >= ATTN_MORE, ATTN_SUB * i - 1 - ATTN_MORE, ATTN_SUB * i - 1)
        o_ref[...] = lax.while_loop(cond, step, (first, r, acc))[2]

    n_hp = D // LANES
    blk = pl.BlockSpec((tq, LANES), lambda hp, i: (i, hp))
    seq = pl.BlockSpec((S, LANES), lambda hp, i: (0, hp))
    n_i = S // tq
    body, in_specs, out_specs, out_shape, scratch, operands, aliases = _with_exchange(
        exchange, body, [blk, seq, seq], [blk], [_sds((S, D), F32)], [], [qs, kn, vb],
        lambda: jnp.logical_and(pl.program_id(0) == 0, pl.program_id(1) == 0),
        lambda: jnp.logical_and(pl.program_id(0) == n_hp - 1, pl.program_id(1) == n_i - 1))
    outs = _pcall(
        body, grid=(n_hp, n_i), in_specs=in_specs, out_specs=out_specs, out_shape=out_shape, scratch_shapes=scratch,
        input_output_aliases=aliases, compiler_params=_cp("arbitrary" if exchange else "parallel", "arbitrary"), name=name,
    )(*operands)
    return outs if exchange else outs[0]


def _attn_bwd(name, qs, kn, vb, o, do):
    S, D = qs.shape
    t = ATTN_BLOCK
    tq = ATTN_SUB * t

    def body(q_ref, k_ref, v_ref, o_ref, do_ref, dq_ref, dk_ref, dv_ref):
        i = pl.program_id(1)

        @pl.when(i == 0)
        def _():
            dk_ref[...] = jnp.zeros_like(dk_ref)
            dv_ref[...] = jnp.zeros_like(dv_ref)

        row, col, heads = _attn_consts()
        after_m = (row > col).astype(BF16)
        from_m = (row >= col).astype(BF16)
        causal = col < row
        q_all = _stack_heads(q_ref[...], heads)
        dob = do_ref[...].astype(BF16)
        do_all = _stack_heads(dob, heads)
        dsum_all = jnp.sum(_stack_heads(dob.astype(F32) * o_ref[...], heads), axis=1, keepdims=True)

        def block(j, q, dor, dsum, r, es, dq, mask):
            off = pl.multiple_of(j * t, t)
            kb = k_ref[pl.ds(off, t), :]
            vblk = v_ref[pl.ds(off, t), :]
            z = lax.dot_general(q, kb, NT, preferred_element_type=F32)
            lk = _log_keep(z)
            if mask is not None:
                lk = jnp.where(mask, lk, 0.0)
            ls = z + lk
            w = jnp.exp(ls + _split_dot(lk, after_m) + r)
            if mask is not None:
                w = jnp.where(mask, w, 0.0)
            e = w * lax.dot_general(dor, vblk, NT, preferred_element_type=F32)
            before = dsum - (es + _split_dot(e, from_m))
            dz = e - (e + before) * jnp.exp(ls)
            if mask is not None:
                dz = jnp.where(mask, dz, 0.0)
            dzb = dz.astype(BF16)
            dq = dq + jnp.dot(_side_by_side(dzb), _stack_heads(kb, heads), preferred_element_type=F32)
            dk_ref[pl.ds(off, t), :] += lax.dot_general(dzb, q, TN, preferred_element_type=F32)
            dv_ref[pl.ds(off, t), :] += lax.dot_general(w.astype(BF16), dor, TN, preferred_element_type=F32)
            return r + jnp.sum(lk, axis=1, keepdims=True), es + jnp.sum(e, axis=1, keepdims=True), dq

        def head(n_more):
            r = es = dq = None
            for s in reversed(range(ATTN_SUB)):
                mask = jnp.concatenate([causal, causal] + [jnp.ones_like(causal)] * (2 * (ATTN_SUB - 1 - s)), axis=0)
                lo = 2 * s * t
                r, es, dq = block(ATTN_SUB * i + s, q_all[lo:], do_all[lo:], dsum_all[lo:], _grow(r, 2 * t, 1),
                                  _grow(es, 2 * t, 1), _grow(dq, t, LANES), mask)
            for b in range(n_more):
                r, es, dq = block(ATTN_SUB * i - 1 - b, q_all, do_all, dsum_all, r, es, dq, None)
            return r, es, dq

        r, es, dq = lax.cond(ATTN_SUB * i >= ATTN_MORE, lambda: head(ATTN_MORE), lambda: head(0))

        def cond(c):
            return jnp.logical_and(c[0] >= 0, jnp.max(c[1]) > EXP_UNDERFLOW)

        def step(c):
            r, es, a = block(c[0], q_all, do_all, dsum_all, c[1], c[2], c[3], None)
            return c[0] - 1, r, es, a

        first = jnp.where(ATTN_SUB * i >= ATTN_MORE, ATTN_SUB * i - 1 - ATTN_MORE, ATTN_SUB * i - 1)
        dq_ref[...] = lax.while_loop(cond, step, (first, r, es, dq))[3]

    n_hp = D // LANES
    blk = pl.BlockSpec((tq, LANES), lambda hp, i: (i, hp))
    seq = pl.BlockSpec((S, LANES), lambda hp, i: (0, hp))
    return _pcall(
        body, grid=(n_hp, S // tq), in_specs=[blk, seq, seq, blk, blk], out_specs=[blk, seq, seq],
        out_shape=[_sds((S, D), F32)] * 3, compiler_params=_cp("parallel", "arbitrary"), name=name,
    )(qs, kn, vb, o, do)


def _adamw(name, w, g, m, v):
    L, R, C = w.shape
    tr = _tile(R, 256, SUBLANES)
    c1 = 1.0 - ADAM_B1 ** ADAM_STEP
    c2 = 1.0 - ADAM_B2 ** ADAM_STEP

    def body(w_ref, g_ref, m_ref, v_ref, d_ref, mo_ref, vo_ref):
        gg = g_ref[...]
        mn = ADAM_B1 * m_ref[...] + (1.0 - ADAM_B1) * gg
        vn = ADAM_B2 * v_ref[...] + (1.0 - ADAM_B2) * (gg * gg)
        d_ref[...] = -ADAM_LR * ((mn / c1) / (jnp.sqrt(vn / c2) + ADAM_EPS) + ADAM_WD * w_ref[...])
        mo_ref[...] = mn
        vo_ref[...] = vn

    blk = pl.BlockSpec((None, tr, C), lambda l, i: (l, i, 0))
    return _pcall(
        body, grid=(L, R // tr), in_specs=[blk] * 4, out_specs=[blk] * 3, out_shape=[_sds(w.shape, F32)] * 3,
        compiler_params=_cp("parallel", "parallel"), name=name,
    )(w, g, m, v)


def _place():
    x, y, c = lax.axis_index("x"), lax.axis_index("y"), lax.axis_index("c")
    chips = [(1 - x, y), (x, 1 - y), (1 - x, 1 - y)]
    return x, y, c, chips


def _place_shard(name, w, j_idx):
    L, R, X = w.shape
    rh = R // 2
    tr = _tile(rh, 256, BF16_ROWS)

    def body(j_ref, w_ref, o_ref):
        o_ref[...] = w_ref[...].astype(BF16)

    return _pcall(
        body,
        grid_spec=pltpu.PrefetchScalarGridSpec(
            num_scalar_prefetch=1, grid=(L, 2, rh // tr),
            in_specs=[pl.BlockSpec((None, None, tr, X), lambda l, h, i, j_ref: (l, h, i, 0))],
            out_specs=pl.BlockSpec((None, None, None, tr, X), lambda l, h, i, j_ref: (l, j_ref[0], h, i, 0))),
        out_shape=_sds((L, N_CHIPS, 2, rh, X), BF16), compiler_params=_cp("parallel", "parallel", "parallel"), name=name,
    )(j_idx, w.reshape(L, 2, rh, X))


def _all_gather_weights(bufs, spans, small_ws):
    n_big, n_small = len(bufs), len(small_ws)
    n_in = n_big + n_small
    layers = [pl.ds(l0, n) for l0, n in spans]

    def body(*refs):
        ins, outs = refs[:n_in], refs[n_in:2 * n_in]
        send_sems, recv_sems, local_sems = refs[2 * n_in:]
        x, y, c, chips = _place()
        j_me = 2 * x + y
        j_of = [2 * cx + cy for cx, cy in chips]
        sibling = (x, y, 1 - c)

        def remote(src, dst, s, to):
            return pltpu.make_async_remote_copy(src_ref=src, dst_ref=dst, send_sem=send_sems.at[s], recv_sem=recv_sems.at[s],
                                                device_id=to, device_id_type=MESH)

        started = []
        for t in range(n_big, n_in):
            loc = pltpu.make_async_copy(ins[t], outs[t].at[:, j_me], local_sems.at[t - n_big])
            loc.start()
            started.append(loc)
        first = []
        for t in range(n_big):
            mine = outs[t].at[layers[t], j_me, c]
            for k in range(3):
                first.append(remote(mine, mine, 6 * t + k, (*chips[k], c)))
        for t in range(n_big, n_in):
            for k in range(3):
                first.append(remote(ins[t], outs[t].at[:, j_me], 6 * n_big + 3 * (t - n_big) + k, (*chips[k], c)))
        for cp in first:
            cp.start()
        passed = []
        for t in range(n_big):
            for k in range(3):
                landed = outs[t].at[layers[t], j_of[k], c]
                remote(landed, landed, 6 * t + k, (*chips[k], c)).wait_recv()
                fwd = remote(landed, landed, 6 * t + 3 + k, sibling)
                fwd.start()
                passed.append(fwd)
        for t in range(n_big):
            for k in range(3):
                other = outs[t].at[layers[t], j_of[k], 1 - c]
                remote(other, other, 6 * t + 3 + k, sibling).wait_recv()
        for t in range(n_big, n_in):
            for k in range(3):
                dst = outs[t].at[:, j_of[k]]
                remote(dst, dst, 6 * n_big + 3 * (t - n_big) + k, (*chips[k], c)).wait_recv()
        for cp in first + passed:
            cp.wait_send()
        for loc in started:
            loc.wait()

    out_shape = [_sds(b.shape, b.dtype) for b in bufs]
    out_shape += [_sds((w.shape[0], N_CHIPS) + w.shape[1:], w.dtype) for w in small_ws]
    n_sem = 6 * n_big + 3 * n_small
    outs = _pcall(
        body, in_specs=[ANY] * n_in, out_specs=[ANY] * n_in, out_shape=out_shape,
        input_output_aliases={t: t for t in range(n_big)},
        scratch_shapes=[pltpu.SemaphoreType.DMA((n_sem,)), pltpu.SemaphoreType.DMA((n_sem,)), pltpu.SemaphoreType.DMA((n_small,))],
        name="all_gather_weights",
    )(*bufs, *small_ws)
    return outs[:n_big], outs[n_big:]


class _Exchange:
    def __init__(self, operands, out_shapes, n_sems, copies, in_place=False):
        self.operands, self.out_shapes, self.n_sems, self.copies = list(operands), list(out_shapes), n_sems, copies
        self.aliases = {t: t for t in range(len(self.operands))} if in_place else {}

    @property
    def scratch(self):
        return [pltpu.SemaphoreType.DMA((self.n_sems,)), pltpu.SemaphoreType.DMA((self.n_sems,))]

    def split(self, refs):
        n_in, n_out = len(self.operands), len(self.out_shapes)
        return refs[:n_in], refs[n_in:n_in + n_out]

    def start(self, ins, outs, sems):
        for cp in self.copies(ins, outs, *sems):
            cp.start()

    def wait(self, ins, outs, sems):
        for cp in self.copies(ins, outs, *sems):
            cp.wait()


def _run_exchange(name, ex):
    n_in, n_out = len(ex.operands), len(ex.out_shapes)

    def body(*refs):
        ins, outs, sems = refs[:n_in], refs[n_in:n_in + n_out], refs[n_in + n_out:]
        ex.start(ins, outs, sems)
        ex.wait(ins, outs, sems)

    return _pcall(body, in_specs=[ANY] * n_in, out_specs=[ANY] * n_out, out_shape=ex.out_shapes, scratch_shapes=ex.scratch,
                  input_output_aliases=ex.aliases, name=name)(*ex.operands)


def _gather_chips_exchange(bufs, spans):
    def copies(ins, outs, send_sems, recv_sems):
        x, y, c, chips = _place()
        cps = []
        for t, (l0, n) in enumerate(spans):
            mine = outs[t].at[pl.ds(l0, n), 2 * x + y, c]
            cps += [pltpu.make_async_remote_copy(src_ref=mine, dst_ref=mine, send_sem=send_sems.at[3 * t + k],
                                                 recv_sem=recv_sems.at[3 * t + k], device_id=(cx, cy, c), device_id_type=MESH)
                    for k, (cx, cy) in enumerate(chips)]
        return cps

    return _Exchange(bufs, [_sds(b.shape, b.dtype) for b in bufs], 3 * len(bufs), copies, in_place=True)


def _gather_cores_exchange(bufs, spans):
    def copies(ins, outs, send_sems, recv_sems):
        x, y, c, chips = _place()
        cps = []
        for t, (l0, n) in enumerate(spans):
            for k, (cx, cy) in enumerate(chips):
                part = outs[t].at[pl.ds(l0, n), 2 * cx + cy, c]
                cps.append(pltpu.make_async_remote_copy(src_ref=part, dst_ref=part, send_sem=send_sems.at[3 * t + k],
                                                        recv_sem=recv_sems.at[3 * t + k], device_id=(x, y, 1 - c),
                                                        device_id_type=MESH))
        return cps

    return _Exchange(bufs, [_sds(b.shape, b.dtype) for b in bufs], 3 * len(bufs), copies, in_place=True)


def _core_halves_exchange(grads, spans):
    def copies(ins, outs, send_sems, recv_sems):
        x, y, c, _ = _place()
        return [pltpu.make_async_remote_copy(src_ref=ins[t].at[pl.ds(l0, n), :, 1 - c], dst_ref=outs[t],
                                             send_sem=send_sems.at[t], recv_sem=recv_sems.at[t], device_id=(x, y, 1 - c),
                                             device_id_type=MESH) for t, (l0, n) in enumerate(spans)]

    shapes = [_sds((n, g.shape[1], g.shape[3], g.shape[4]), F32) for g, (_, n) in zip(grads, spans)]
    return _Exchange(grads, shapes, len(grads), copies)


def _add_core_halves(name, g, a, c_idx, l0):
    _, nj, _, rh, X = g.shape
    L = a.shape[0]
    tr = _tile(rh, 256, BF16_ROWS)

    def body(c_ref, g_ref, a_ref, o_ref, ob_ref):
        s = g_ref[...] + a_ref[...]
        o_ref[...] = s
        ob_ref[...] = s.astype(BF16)

    blk = pl.BlockSpec((None, None, tr, X), lambda l, j, i, c_ref: (l, j, i, 0))
    return _pcall(
        body,
        grid_spec=pltpu.PrefetchScalarGridSpec(
            num_scalar_prefetch=1, grid=(L, nj, rh // tr),
            in_specs=[pl.BlockSpec((None, None, None, tr, X), lambda l, j, i, c_ref: (l + l0, j, c_ref[0], i, 0)), blk],
            out_specs=[blk, blk]),
        out_shape=[_sds((L, nj, rh, X), F32), _sds((L, nj, rh, X), BF16)],
        compiler_params=_cp("parallel", "parallel", "parallel"), name=name,
    )(c_idx, g, a)


def _chip_shards_exchange(parts):
    def copies(ins, outs, send_sems, recv_sems):
        x, y, c, chips = _place()
        return [pltpu.make_async_remote_copy(
            src_ref=ins[t].at[:, 2 * cx + cy], dst_ref=outs[t].at[k], send_sem=send_sems.at[3 * t + k],
            recv_sem=recv_sems.at[3 * t + k], device_id=(cx, cy, c), device_id_type=MESH)
            for t in range(len(parts)) for k, (cx, cy) in enumerate(chips)]

    shapes = [_sds((3, p.shape[0], p.shape[2], p.shape[3]), p.dtype) for p in parts]
    return _Exchange(parts, shapes, 3 * len(parts), copies)


def _add_chip_shards(name, p, b, jc_idx, l0, n_layers, buf):
    n, _, rh, X = p.shape
    tr = _tile(rh, 256, BF16_ROWS)

    def body(jc_ref, p_ref, b_ref, *rest):
        rest[-1][...] = ((p_ref[...] + b_ref[0].astype(F32)) + b_ref[1].astype(F32)) + b_ref[2].astype(F32)

    in_specs = [pl.BlockSpec((None, None, tr, X), lambda l, i, jc: (l, jc[0], i, 0)),
                pl.BlockSpec((3, None, tr, X), lambda l, i, jc: (0, l, i, 0))]
    operands = [jc_idx, p, b]
    if buf is not None:
        in_specs.append(ANY)
        operands.append(buf)
    return _pcall(
        body,
        grid_spec=pltpu.PrefetchScalarGridSpec(
            num_scalar_prefetch=1, grid=(n, rh // tr), in_specs=in_specs,
            out_specs=pl.BlockSpec((None, None, tr, X), lambda l, i, jc: (l + l0, jc[1], i, 0))),
        out_shape=_sds((n_layers, 2, rh, X), F32), input_output_aliases={3: 0} if buf is not None else {},
        compiler_params=_cp("parallel", "parallel"), name=name,
    )(*operands)


def _join_core_halves(bufs):
    n = len(bufs)

    def body(*refs):
        outs = refs[n:2 * n]
        send_sems, recv_sems = refs[2 * n:]
        x, y, c, _ = _place()
        cps = [pltpu.make_async_remote_copy(src_ref=outs[t].at[:, c], dst_ref=outs[t].at[:, c], send_sem=send_sems.at[t],
                                            recv_sem=recv_sems.at[t], device_id=(x, y, 1 - c), device_id_type=MESH)
               for t in range(n)]
        for cp in cps:
            cp.start()
        for t in range(n):
            pltpu.make_async_remote_copy(src_ref=outs[t].at[:, c], dst_ref=outs[t].at[:, 1 - c], send_sem=send_sems.at[t],
                                         recv_sem=recv_sems.at[t], device_id=(x, y, 1 - c), device_id_type=MESH).wait()

    outs = _pcall(
        body, in_specs=[ANY] * n, out_specs=[ANY] * n, out_shape=[_sds(b.shape, F32) for b in bufs],
        input_output_aliases={t: t for t in range(n)},
        scratch_shapes=[pltpu.SemaphoreType.DMA((n,)), pltpu.SemaphoreType.DMA((n,))],
        name="grad_join_core_halves",
    )(*bufs)
    return [o.reshape(o.shape[0], 2 * o.shape[2], o.shape[3]) for o in outs]


def _all_reduce_small(packed):
    R, C = packed.shape

    def body(x_ref, o_ref, slots, send_sems, recv_sems):
        x, y, c, _ = _place()
        me = 4 * x + 2 * y + c
        slots[me] = x_ref[...]
        cps = []
        for d in range(N_DEV):
            to = (d // 4, (d // 2) % 2, d % 2)
            cp = pltpu.make_async_remote_copy(src_ref=x_ref, dst_ref=slots.at[me], send_sem=send_sems.at[d],
                                              recv_sem=recv_sems.at[me], device_id=to, device_id_type=MESH)
            cps.append(cp)

            @pl.when(d != me)
            def _():
                cp.start()

        for d in range(N_DEV):
            @pl.when(d != me)
            def _():
                pltpu.make_async_remote_copy(src_ref=x_ref, dst_ref=slots.at[d], send_sem=send_sems.at[d],
                                             recv_sem=recv_sems.at[d], device_id=(x, y, c), device_id_type=MESH).wait_recv()
                cps[d].wait_send()

        acc = slots[0]
        for d in range(1, N_DEV):
            acc = acc + slots[d]
        o_ref[...] = acc

    vm = pl.BlockSpec(memory_space=pltpu.VMEM)
    return _pcall(
        body, in_specs=[vm], out_specs=vm, out_shape=_sds((R, C), F32),
        scratch_shapes=[pltpu.VMEM((N_DEV, R, C), F32), pltpu.SemaphoreType.DMA((N_DEV,)), pltpu.SemaphoreType.DMA((N_DEV,))],
        compiler_params=pltpu.CompilerParams(vmem_limit_bytes=VMEM_LIMIT_BYTES), name="all_reduce_small",
    )(packed)


PACK = SUBLANES * LANES


def _pack(arrays):
    flat = []
    for a in arrays:
        v = a.reshape(-1)
        flat.append(jnp.pad(v, (0, (-v.shape[0]) % PACK)))
    return jnp.concatenate(flat).reshape(-1, LANES)


def _unpack(packed, shapes):
    flat = packed.reshape(-1)
    out, pos = [], 0
    for s in shapes:
        n = 1
        for d in s:
            n *= d
        out.append(flat[pos:pos + n].reshape(s))
        pos += n + (-n) % PACK
    return out


def kernel(x, mix_norm_g, ffn_norm_g, conv_w_in, conv_a_dw_w, conv_a_dw_b, conv_a_ln_g, conv_a_ln_b, conv_b_dw_w, conv_w_out, attn_w_qkv, attn_q_g, attn_k_g, attn_w_o, ffn_w_up, ffn_dw_w, ffn_dw_b, ffn_w_down, loss_target, m_mix_norm_g, m_ffn_norm_g, m_conv_w_in, m_conv_a_dw_w, m_conv_a_dw_b, m_conv_a_ln_g, m_conv_a_ln_b, m_conv_b_dw_w, m_conv_w_out, m_attn_w_qkv, m_attn_q_g, m_attn_k_g, m_attn_w_o, m_ffn_w_up, m_ffn_dw_w, m_ffn_dw_b, m_ffn_w_down, v_mix_norm_g, v_ffn_norm_g, v_conv_w_in, v_conv_a_dw_w, v_conv_a_dw_b, v_conv_a_ln_g, v_conv_a_ln_b, v_conv_b_dw_w, v_conv_w_out, v_attn_w_qkv, v_attn_q_g, v_attn_k_g, v_attn_w_o, v_ffn_w_up, v_ffn_dw_w, v_ffn_dw_b, v_ffn_w_down):
    depth = mix_norm_g.shape[0]
    n_even, n_odd = conv_w_in.shape[0], attn_w_qkv.shape[0]
    S, D = x.shape[1], x.shape[2]
    dg = D // 2
    x0 = x.reshape(S, D)
    target = loss_target.reshape(S, D)
    j_me = 2 * lax.axis_index("x") + lax.axis_index("y")
    c_me = lax.axis_index("c")
    j_idx = j_me.astype(jnp.int32).reshape(1)
    c_idx = c_me.astype(jnp.int32).reshape(1)

    col_names = ["conv_w_in", "attn_w_qkv", "ffn_w_up"]
    row_names = ["conv_w_out", "attn_w_o", "ffn_w_down"]
    local = dict(conv_w_in=conv_w_in, attn_w_qkv=attn_w_qkv, ffn_w_up=ffn_w_up, conv_w_out=conv_w_out, attn_w_o=attn_w_o,
                 ffn_w_down=ffn_w_down)
    gbuf = {n: _place_shard(f"place_{n}", local[n], j_idx) for n in col_names + row_names}

    def weights_of(layer):
        mixer = ("conv_w_in", "conv_w_out") if layer % 2 == 0 else ("attn_w_qkv", "attn_w_o")
        return {mixer[0]: (layer // 2, 1), mixer[1]: (layer // 2, 1), "ffn_w_up": (layer, 1), "ffn_w_down": (layer, 1)}

    def w_col(n):
        return gbuf[n].reshape(gbuf[n].shape[0], N_CHIPS, -1, gbuf[n].shape[4])

    def w_row(n):
        return gbuf[n].reshape(gbuf[n].shape[0], -1, gbuf[n].shape[4])

    def carry(make_exchange, layer):
        if layer + 1 == depth:
            return None, []
        names = list(weights_of(layer + 1))
        return make_exchange([gbuf[n] for n in names], list(weights_of(layer + 1).values())), names

    first = weights_of(0)
    outs, (a_dw, b_dw, f_dw) = _all_gather_weights([gbuf[n] for n in first], list(first.values()),
                                                   [conv_a_dw_w, conv_b_dw_w, ffn_dw_w])
    gbuf.update(zip(first, outs))
    unshard = lambda a: jnp.moveaxis(a, 1, 2).reshape(a.shape[0], a.shape[2], N_CHIPS * a.shape[3])
    a_dw, b_dw, f_dw = unshard(a_dw), unshard(b_dw), unshard(f_dw)
    qk_gain = [jnp.stack([jnp.tile(attn_q_g[i], LANES // HEAD_DIM), jnp.tile(attn_k_g[i], LANES // HEAD_DIM)])
               for i in range(n_odd)]

    saved = []
    xc = x0
    for layer in range(depth):
        i = layer // 2
        tag = f"l{layer}"
        s = {"x_in": xc}
        h = _rms_fwd(f"rms_mix_fwd_{tag}", xc, mix_norm_g, layer)
        s["h"] = h
        ex, names = carry(_gather_chips_exchange, layer)
        if layer % 2 == 0:
            p = _mm_fwd(f"conv_in_fwd_{tag}", h, w_col("conv_w_in"), i, colshard=True)
            ab = _convmix_fwd(f"convmix_fwd_{tag}", p, a_dw, conv_a_dw_b, conv_a_ln_g, conv_a_ln_b, b_dw, i, ex)
            if ex:
                ab, *new = ab
                gbuf.update(zip(names, new))
            xm = _mm_fwd(f"conv_out_fwd_{tag}", ab, w_row("conv_w_out"), i, colshard=False, res=xc)
            s.update(p=p, ab=ab)
        else:
            qkv = _mm_fwd(f"attn_qkv_fwd_{tag}", h, w_col("attn_w_qkv"), i, colshard=True)
            qs, kn, vb = _qknorm_fwd(f"qknorm_fwd_{tag}", qkv, qk_gain[i])
            o = _attn_fwd(f"attn_fwd_{tag}", qs, kn, vb, ex)
            if ex:
                o, *new = o
                gbuf.update(zip(names, new))
            xm = _mm_fwd(f"attn_out_fwd_{tag}", o, w_row("attn_w_o"), i, colshard=False, res=xc)
            s.update(qkv=qkv, qs=qs, kn=kn, vb=vb, o=o)
        s["x_mid"] = xm
        h2 = _rms_fwd(f"rms_ffn_fwd_{tag}", xm, ffn_norm_g, layer)
        u2 = _mm_fwd(f"ffn_up_fwd_{tag}", h2, w_col("ffn_w_up"), layer, colshard=True, out_split=2)
        ex, names = carry(_gather_cores_exchange, layer)
        f = _ffn_mid_fwd(f"ffn_mid_fwd_{tag}", u2, f_dw, ffn_dw_b, layer, ex)
        if ex:
            f, *new = f
            gbuf.update(zip(names, new))
        xc = _mm_fwd(f"ffn_down_fwd_{tag}", f, w_row("ffn_w_down"), layer, colshard=False, res=xm)
        s.update(h2=h2, u2=u2, f=f)
        saved.append(s)

    dx, loss_tile = _loss_fwd_bwd("loss", xc, target)

    w_in, w_qkv, w_up = w_col("conv_w_in"), w_col("attn_w_qkv"), w_col("ffn_w_up")
    w_out, w_o, w_down = w_row("conv_w_out"), w_row("attn_w_o"), w_row("ffn_w_down")
    g_up = g_down = g_in = g_out = g_qkv = g_o = None
    big_names = col_names + row_names

    def halves_view(n, g):
        if n in col_names:
            return g.reshape(g.shape[0], N_CHIPS, 2, g.shape[2] // 2, g.shape[3])
        return g.reshape(g.shape[0], N_CHIPS, 2, g.shape[1] // (2 * N_CHIPS), g.shape[2])

    early = {"conv_w_in": (1, n_even - 1), "attn_w_qkv": (0, n_odd), "ffn_w_up": (1, depth - 1),
             "conv_w_out": (1, n_even - 1), "attn_w_o": (0, n_odd), "ffn_w_down": (1, depth - 1)}
    late = {n: (0, 1) for n in ("conv_w_in", "ffn_w_up", "conv_w_out", "ffn_w_down")}
    early_sums = early_from_chips = None
    d_mix_g, d_ffn_g = [None] * depth, [None] * depth
    d_ffn_dw_w, d_ffn_dw_b = [None] * depth, [None] * depth
    d_a_dw_w, d_a_dw_b, d_a_ln_g, d_a_ln_b, d_b_dw_w = ([None] * n_even for _ in range(5))
    d_q_g, d_k_g = [None] * n_odd, [None] * n_odd
    for layer in reversed(range(depth)):
        i = layer // 2
        tag = f"l{layer}"
        s = saved[layer]
        df = _mm_dgrad(f"ffn_down_dgrad_{tag}", dx, w_down, layer, colshard=False)
        g_down = _mm_wgrad(f"ffn_down_wgrad_{tag}", s["f"], dx, layer, depth, g_down, colshard=False)
        big = {"conv_w_in": g_in, "attn_w_qkv": g_qkv, "ffn_w_up": g_up, "conv_w_out": g_out, "attn_w_o": g_o,
               "ffn_w_down": g_down}
        core_ex = None
        if layer == 0:
            core_ex = _core_halves_exchange([halves_view(n, big[n]) for n in early], list(early.values()))
        du2, dww, dwb, *early_from_sibling = _ffn_mid_bwd(f"ffn_mid_bwd_{tag}", s["u2"], df, f_dw, ffn_dw_b, layer, core_ex)
        d_ffn_dw_w[layer] = jnp.moveaxis(dww, 0, 1).reshape(FFN_CONV_WIDTH, -1)
        d_ffn_dw_b[layer] = dwb.reshape(-1)
        dh2 = _mm_dgrad(f"ffn_up_dgrad_{tag}", du2, w_up, layer, colshard=True)
        g_up = _mm_wgrad(f"ffn_up_wgrad_{tag}", s["h2"], du2, layer, depth, g_up, colshard=True)
        dx, dg_ = _rms_bwd(f"rms_ffn_bwd_{tag}", s["x_mid"], ffn_norm_g, layer, dh2, dx)
        d_ffn_g[layer] = dg_.reshape(-1)
        if layer % 2 == 0:
            dab = _mm_dgrad(f"conv_out_dgrad_{tag}", dx, w_out, i, colshard=False)
            g_out = _mm_wgrad(f"conv_out_wgrad_{tag}", s["ab"], dx, i, n_even, g_out, colshard=False)
            chip_ex = None
            if layer == 0:
                big = {"conv_w_in": g_in, "attn_w_qkv": g_qkv, "ffn_w_up": g_up, "conv_w_out": g_out, "attn_w_o": g_o,
                       "ffn_w_down": g_down}
                both = [_add_core_halves(f"grad_add_core_early_{n}", halves_view(n, big[n]), a, c_idx, early[n][0])
                        for n, a in zip(early, early_from_sibling)]
                early_sums = [b[0] for b in both]
                chip_ex = _chip_shards_exchange([b[1] for b in both])
            dp, daw, dab_b, dlg, dlb, dbw, *early_from_chips = _convmix_bwd(
                f"convmix_bwd_{tag}", s["p"], dab, a_dw, conv_a_dw_b, conv_a_ln_g, conv_a_ln_b, b_dw, i, chip_ex)
            d_a_dw_w[i], d_a_dw_b[i], d_a_ln_g[i], d_a_ln_b[i], d_b_dw_w[i] = (
                daw, dab_b.reshape(-1), dlg.reshape(-1), dlb.reshape(-1), dbw)
            dh = _mm_dgrad(f"conv_in_dgrad_{tag}", dp, w_in, i, colshard=True)
            g_in = _mm_wgrad(f"conv_in_wgrad_{tag}", s["h"], dp, i, n_even, g_in, colshard=True)
        else:
            do = _mm_dgrad(f"attn_out_dgrad_{tag}", dx, w_o, i, colshard=False)
            g_o = _mm_wgrad(f"attn_out_wgrad_{tag}", s["o"], dx, i, n_odd, g_o, colshard=False)
            dq, dk, dv = _attn_bwd(f"attn_bwd_{tag}", s["qs"], s["kn"], s["vb"], s["o"], do)
            dqkv, dgain = _qknorm_bwd(f"qknorm_bwd_{tag}", s["qkv"], dq, dk, dv, qk_gain[i])
            d_q_g[i] = dgain[0, :HEAD_DIM] + dgain[0, HEAD_DIM:]
            d_k_g[i] = dgain[1, :HEAD_DIM] + dgain[1, HEAD_DIM:]
            dh = _mm_dgrad(f"attn_qkv_dgrad_{tag}", dqkv, w_qkv, i, colshard=True)
            g_qkv = _mm_wgrad(f"attn_qkv_wgrad_{tag}", s["h"], dqkv, i, n_odd, g_qkv, colshard=True)
        dx, dg_ = _rms_bwd(f"rms_mix_bwd_{tag}", s["x_in"], mix_norm_g, layer, dh, dx)
        d_mix_g[layer] = dg_.reshape(-1)
    grad_x = dx.reshape(1, S, D)

    small = {
        "mix_norm_g": jnp.stack(d_mix_g), "ffn_norm_g": jnp.stack(d_ffn_g),
        "conv_a_dw_w": jnp.stack(d_a_dw_w), "conv_a_dw_b": jnp.stack(d_a_dw_b),
        "conv_a_ln_g": jnp.stack(d_a_ln_g), "conv_a_ln_b": jnp.stack(d_a_ln_b),
        "conv_b_dw_w": jnp.stack(d_b_dw_w), "attn_q_g": jnp.stack(d_q_g), "attn_k_g": jnp.stack(d_k_g),
        "ffn_dw_w": jnp.stack(d_ffn_dw_w), "ffn_dw_b": jnp.stack(d_ffn_dw_b),
    }
    small_names = list(small)
    summed = _all_reduce_small(_pack([loss_tile] + [small[n] for n in small_names]))
    parts = _unpack(summed, [loss_tile.shape] + [small[n].shape for n in small_names])
    loss = parts[0][0, 0]
    small_g = dict(zip(small_names, parts[1:]))
    for n in ("conv_a_dw_w", "conv_b_dw_w", "ffn_dw_w"):
        cs = small_g[n].shape[2] // N_CHIPS
        small_g[n] = lax.dynamic_slice_in_dim(small_g[n], j_me * cs, cs, axis=2)

    big = {"conv_w_in": g_in, "attn_w_qkv": g_qkv, "ffn_w_up": g_up, "conv_w_out": g_out, "attn_w_o": g_o, "ffn_w_down": g_down}
    views = {n: halves_view(n, big[n]) for n in big_names}
    late_from_sibling = _run_exchange("grad_exchange_core_halves",
                                      _core_halves_exchange([views[n] for n in late], list(late.values())))
    both = [_add_core_halves(f"grad_add_core_{n}", views[n], a, c_idx, late[n][0]) for n, a in zip(late, late_from_sibling)]
    late_sums = [b[0] for b in both]
    late_from_chips = _run_exchange("grad_exchange_chip_shards", _chip_shards_exchange([b[1] for b in both]))
    jc_idx = jnp.concatenate([j_idx, c_idx])
    totals = {}
    for n, p, b in zip(early, early_sums, early_from_chips):
        totals[n] = _add_chip_shards(f"grad_add_chips_early_{n}", p, b, jc_idx, early[n][0], big[n].shape[0], None)
    for n, p, b in zip(late, late_sums, late_from_chips):
        totals[n] = _add_chip_shards(f"grad_add_chips_{n}", p, b, jc_idx, late[n][0], big[n].shape[0], totals[n])
    big_g = dict(zip(big_names, _join_core_halves([totals[n] for n in big_names])))

    weights = dict(mix_norm_g=mix_norm_g, ffn_norm_g=ffn_norm_g, conv_w_in=conv_w_in, conv_a_dw_w=conv_a_dw_w, conv_a_dw_b=conv_a_dw_b, conv_a_ln_g=conv_a_ln_g, conv_a_ln_b=conv_a_ln_b, conv_b_dw_w=conv_b_dw_w, conv_w_out=conv_w_out, attn_w_qkv=attn_w_qkv, attn_q_g=attn_q_g, attn_k_g=attn_k_g, attn_w_o=attn_w_o, ffn_w_up=ffn_w_up, ffn_dw_w=ffn_dw_w, ffn_dw_b=ffn_dw_b, ffn_w_down=ffn_w_down)
    m_in = dict(mix_norm_g=m_mix_norm_g, ffn_norm_g=m_ffn_norm_g, conv_w_in=m_conv_w_in, conv_a_dw_w=m_conv_a_dw_w, conv_a_dw_b=m_conv_a_dw_b, conv_a_ln_g=m_conv_a_ln_g, conv_a_ln_b=m_conv_a_ln_b, conv_b_dw_w=m_conv_b_dw_w, conv_w_out=m_conv_w_out, attn_w_qkv=m_attn_w_qkv, attn_q_g=m_attn_q_g, attn_k_g=m_attn_k_g, attn_w_o=m_attn_w_o, ffn_w_up=m_ffn_w_up, ffn_dw_w=m_ffn_dw_w, ffn_dw_b=m_ffn_dw_b, ffn_w_down=m_ffn_w_down)
    v_in = dict(mix_norm_g=v_mix_norm_g, ffn_norm_g=v_ffn_norm_g, conv_w_in=v_conv_w_in, conv_a_dw_w=v_conv_a_dw_w, conv_a_dw_b=v_conv_a_dw_b, conv_a_ln_g=v_conv_a_ln_g, conv_a_ln_b=v_conv_a_ln_b, conv_b_dw_w=v_conv_b_dw_w, conv_w_out=v_conv_w_out, attn_w_qkv=v_attn_w_qkv, attn_q_g=v_attn_q_g, attn_k_g=v_attn_k_g, attn_w_o=v_attn_w_o, ffn_w_up=v_ffn_w_up, ffn_dw_w=v_ffn_dw_w, ffn_dw_b=v_ffn_dw_b, ffn_w_down=v_ffn_w_down)
    order = list(weights)
    grads, delta, new_m, new_v = {}, {}, {}, {}
    for n in big_names:
        grads[n] = big_g[n]
        delta[n], new_m[n], new_v[n] = _adamw(f"adamw_{n}", weights[n], big_g[n], m_in[n], v_in[n])
    shapes = [weights[n].shape for n in small_names]
    packed = [_pack([d[n] for n in small_names]) for d in (weights, small_g, m_in, v_in)]
    upd = _adamw("adamw_small", *[p[None] for p in packed])
    for out, res in zip((delta, new_m, new_v), upd):
        out.update(zip(small_names, _unpack(res[0], shapes)))
    grads.update({n: small_g[n].reshape(weights[n].shape) for n in small_names})
    return (loss, grad_x, *[grads[n] for n in order], *[delta[n] for n in order], *[new_m[n] for n in order],
            *[new_v[n] for n in order])
```

```python
import jax
import jax.numpy as jnp
from jax import lax
from jax.experimental import pallas as pl
from jax.experimental.pallas import tpu as pltpu

F32 = jnp.float32
BF16 = jnp.bfloat16
EPS = 1e-6
CONV_A_WIDTH = 31
CONV_B_WIDTH = 3
FFN_CONV_WIDTH = 3
HEAD_DIM = 64
ADAM_LR = 0.001
ADAM_B1 = 0.9
ADAM_B2 = 0.999
ADAM_EPS = 1e-08
ADAM_WD = 0.01
ADAM_STEP = 10

LANES = 128
SUBLANES = 8
BF16_ROWS = 16
V7X_VMEM_BYTES = 64 * 1024 * 1024
VMEM_LIMIT_BYTES = V7X_VMEM_BYTES * 3 // 4
MM_VMEM_BUDGET = VMEM_LIMIT_BYTES * 4 // 5
MM_ROWS = 1024
N_CHIPS = 4
N_DEV = 8
HALO_A = 32
HALO_S = 8
ELT_ROWS = 64
ATTN_BLOCK = 128
ATTN_SUB = 2
ATTN_MORE = 2
EXP_UNDERFLOW = -104.0
MESH = pl.DeviceIdType.MESH
ANY = pl.BlockSpec(memory_space=pl.ANY)
NT = (((1,), (1,)), ((), ()))
NN = (((1,), (0,)), ((), ()))
TN = (((0,), (0,)), ((), ()))


def _pcall(body, **kw):
    return pl.pallas_call(body, **kw)


def _cp(*sem):
    return pltpu.CompilerParams(dimension_semantics=sem, vmem_limit_bytes=VMEM_LIMIT_BYTES)


def _sds(shape, dtype):
    return jax.ShapeDtypeStruct(tuple(shape), dtype)


def _tile(n, cap, align=LANES):
    if n <= cap:
        return n
    for t in range(cap - cap % align, 0, -align):
        if n % t == 0:
            return t
    return n


def _sig(x):
    return 0.5 * jnp.tanh(0.5 * x) + 0.5


def _rowsum(x):
    return jnp.sum(x, axis=0, keepdims=True)


def _fold(x):
    acc = x[0:SUBLANES]
    for r in range(SUBLANES, x.shape[0], SUBLANES):
        acc = acc + x[r:r + SUBLANES]
    return acc


def _with_exchange(ex, body, in_specs, out_specs, out_shape, scratch, operands, first, last):
    if ex is None:
        return body, in_specs, out_specs, out_shape, scratch, operands, {}
    n_in, n_out, n_scr = len(in_specs), len(out_specs), len(scratch)
    e_in, e_out = len(ex.operands), len(ex.out_shapes)

    def hosted(*refs):
        refs = list(refs)
        ins, refs = refs[:n_in], refs[n_in:]
        e_ins, refs = refs[:e_in], refs[e_in:]
        outs, refs = refs[:n_out], refs[n_out:]
        e_outs, refs = refs[:e_out], refs[e_out:]
        scr, sems = refs[:n_scr], refs[n_scr:]

        @pl.when(first())
        def _():
            ex.start(e_ins, e_outs, sems)

        body(*ins, *outs, *scr)

        @pl.when(last())
        def _():
            ex.wait(e_ins, e_outs, sems)

    return (hosted, in_specs + [ANY] * e_in, out_specs + [ANY] * e_out, out_shape + ex.out_shapes, scratch + ex.scratch,
            operands + ex.operands, {n_in + i: n_out + o for i, o in ex.aliases.items()})


def _mm_call(name, dn, operands, in_specs, out_shape, out_spec, grid, nk, acc_shape, has_res, has_alias):
    def body(*refs):
        a_ref, b_ref = refs[0], refs[1]
        pos = 2
        res_ref = refs[pos] if has_res else None
        pos += int(has_res) + int(has_alias)
        o_ref = refs[pos]
        acc_ref = refs[pos + 1] if nk > 1 else None
        p = lax.dot_general(a_ref[...].astype(BF16), b_ref[...].astype(BF16), dn, preferred_element_type=F32)

        def finish(v):
            if has_res:
                v = v + res_ref[...]
            o_ref[...] = v.astype(o_ref.dtype)

        if nk == 1:
            finish(p)
        else:
            k = pl.program_id(2)

            @pl.when(k == 0)
            def _():
                acc_ref[...] = p

            @pl.when(k > 0)
            def _():
                acc_ref[...] += p

            @pl.when(k == nk - 1)
            def _():
                finish(acc_ref[...])

    aliases = {len(operands) - 1: 0} if has_alias else {}
    return _pcall(
        body, grid=grid, in_specs=in_specs, out_specs=out_spec, out_shape=out_shape,
        scratch_shapes=[pltpu.VMEM(acc_shape, F32)] if nk > 1 else [],
        input_output_aliases=aliases, compiler_params=_cp("parallel", "parallel", "arbitrary"), name=name,
    )(*operands)


def _mm_fwd(name, a, w, l, *, colshard, res=None, out_split=1):
    M, K = a.shape
    tm = _tile(M, MM_ROWS, BF16_ROWS)
    if colshard:
        cs = w.shape[3]
        N, tn, tk = N_CHIPS * cs, cs, K
        b_spec = pl.BlockSpec((None, None, tk, tn), lambda j, i, k: (l, j, k, 0))
    else:
        N = w.shape[2]
        tn, tk = _tile(N, 1024), _tile(K, 1536)
        b_spec = pl.BlockSpec((None, tk, tn), lambda j, i, k: (l, k, j))
    nk = K // tk
    in_specs = [pl.BlockSpec((tm, tk), lambda j, i, k: (i, k)), b_spec]
    operands = [a, w]
    if res is not None:
        in_specs.append(pl.BlockSpec((tm, tn), lambda j, i, k: (i, j)))
        operands.append(res)
    if out_split == 1:
        out_shape = _sds((M, N), F32)
        out_spec = pl.BlockSpec((tm, tn), lambda j, i, k: (i, j))
    else:
        per = N // tn // out_split
        out_shape = _sds((out_split, M, N // out_split), F32)
        out_spec = pl.BlockSpec((None, tm, tn), lambda j, i, k: (j // per, i, j % per))
    return _mm_call(name, NN, operands, in_specs, out_shape, out_spec, (N // tn, M // tm, nk), nk, (tm, tn),
                    res is not None, False)


def _mm_dgrad(name, g, w, l, *, colshard):
    split = g.ndim == 3
    M = g.shape[-2]
    tm = _tile(M, MM_ROWS, BF16_ROWS)
    if colshard:
        kw, cs = w.shape[2], w.shape[3]
        tn, tk, nk = _tile(kw, 1408), cs, N_CHIPS
        b_spec = pl.BlockSpec((None, None, tn, tk), lambda j, i, k: (l, k, j, 0))
    else:
        kw, ncon = w.shape[1], w.shape[2]
        tn, tk = _tile(kw, 1408), _tile(ncon, 1536)
        nk = ncon // tk
        b_spec = pl.BlockSpec((None, tn, tk), lambda j, i, k: (l, j, k))
    if split:
        per = nk // g.shape[0]
        a_spec = pl.BlockSpec((None, tm, tk), lambda j, i, k: (k // per, i, k % per))
    else:
        a_spec = pl.BlockSpec((tm, tk), lambda j, i, k: (i, k))
    out_shape = _sds((M, kw), F32)
    out_spec = pl.BlockSpec((tm, tn), lambda j, i, k: (i, j))
    return _mm_call(name, NT, [g, w], [a_spec, b_spec], out_shape, out_spec, (kw // tn, M // tm, nk), nk, (tm, tn),
                    False, False)


def _mm_wgrad(name, a, g, l, n_layers, buf, *, colshard):
    S, M = a.shape
    split = g.ndim == 3
    N = g.shape[-1] * (g.shape[0] if split else 1)
    tm = _tile(M, 1408)
    tn = N // N_CHIPS if colshard else _tile(N, 1024)
    per_row = 2 * (tm * a.dtype.itemsize + tn * g.dtype.itemsize)
    tk = _tile(S, max(BF16_ROWS, min(2048, (MM_VMEM_BUDGET - 3 * tm * tn * 4) // per_row)), BF16_ROWS)
    nk = S // tk
    if colshard:
        out_shape = _sds((n_layers, N_CHIPS, M, tn), F32)
        out_spec = pl.BlockSpec((None, None, tm, tn), lambda j, i, k: (l, j, i, 0))
    else:
        out_shape = _sds((n_layers, M, N), F32)
        out_spec = pl.BlockSpec((None, tm, tn), lambda j, i, k: (l, i, j))
    if split:
        per = N // tn // g.shape[0]
        b_spec = pl.BlockSpec((None, tk, tn), lambda j, i, k: (j // per, k, j % per))
    else:
        b_spec = pl.BlockSpec((tk, tn), lambda j, i, k: (k, j))
    in_specs = [pl.BlockSpec((tk, tm), lambda j, i, k: (k, i)), b_spec]
    operands = [a, g]
    if buf is not None:
        in_specs.append(ANY)
        operands.append(buf)
    return _mm_call(name, TN, operands, in_specs, out_shape, out_spec, (N // tn, M // tm, nk), nk, (tm, tn),
                    False, buf is not None)


def _rms_fwd(name, x, g, l):
    S, D = x.shape
    tm = _tile(S, 512, BF16_ROWS)

    def body(x_ref, g_ref, o_ref):
        xf = x_ref[...]
        r = lax.rsqrt(jnp.mean(xf * xf, axis=-1, keepdims=True) + EPS)
        o_ref[...] = (xf * r * g_ref[l:l + 1, :]).astype(BF16)

    return _pcall(
        body, grid=(S // tm,),
        in_specs=[pl.BlockSpec((tm, D), lambda i: (i, 0)), pl.BlockSpec(g.shape, lambda i: (0, 0))],
        out_specs=pl.BlockSpec((tm, D), lambda i: (i, 0)), out_shape=_sds((S, D), BF16),
        compiler_params=_cp("parallel"), name=name,
    )(x, g)


def _rms_bwd(name, x, g, l, dh, dres):
    S, D = x.shape
    tm = _tile(S, 512, SUBLANES)

    def body(x_ref, g_ref, dh_ref, dr_ref, dx_ref, dg_ref):
        xf = x_ref[...]
        r = lax.rsqrt(jnp.mean(xf * xf, axis=-1, keepdims=True) + EPS)
        xh = xf * r
        d = dh_ref[...]
        dxh = d * g_ref[l:l + 1, :]
        dx_ref[...] = dr_ref[...] + r * (dxh - xh * jnp.mean(dxh * xh, axis=-1, keepdims=True))

        @pl.when(pl.program_id(0) == 0)
        def _():
            dg_ref[...] = jnp.zeros_like(dg_ref)

        dg_ref[...] += _rowsum(d * xh)

    row = pl.BlockSpec((tm, D), lambda i: (i, 0))
    return _pcall(
        body, grid=(S // tm,),
        in_specs=[row, pl.BlockSpec(g.shape, lambda i: (0, 0)), row, row],
        out_specs=[row, pl.BlockSpec((1, D), lambda i: (0, 0))],
        out_shape=[_sds((S, D), F32), _sds((1, D), F32)],
        compiler_params=_cp("arbitrary"), name=name,
    )(x, g, dh, dres)


def _loss_fwd_bwd(name, y, t):
    S, D = y.shape
    tm = _tile(S, 512, SUBLANES)

    def body(y_ref, t_ref, dy_ref, l_ref):
        e = y_ref[...] - t_ref[...]
        dy_ref[...] = e * (1.0 / D)

        @pl.when(pl.program_id(0) == 0)
        def _():
            l_ref[...] = jnp.zeros_like(l_ref)

        l_ref[...] += 0.5 * jnp.sum(jnp.sum(e * e, axis=-1, keepdims=True) * (1.0 / D), axis=0, keepdims=True)

    row = pl.BlockSpec((tm, D), lambda i: (i, 0))
    return _pcall(
        body, grid=(S // tm,), in_specs=[row, row],
        out_specs=[row, pl.BlockSpec((SUBLANES, LANES), lambda i: (0, 0))],
        out_shape=[_sds((S, D), F32), _sds((SUBLANES, LANES), F32)],
        compiler_params=_cp("arbitrary"), name=name,
    )(y, t)


def _delayed_copies(us, n_rows):
    for s in range(1, SUBLANES):
        us[s, pl.ds(SUBLANES, n_rows - SUBLANES), :] = us[0, pl.ds(SUBLANES - s, n_rows - SUBLANES), :]


def _conv_a(aw_ref, ab_ref, l, us, row0, rows, dg):
    ka = CONV_A_WIDTH
    out = []
    for c0 in range(0, dg, LANES):
        lanes = slice(c0, c0 + LANES)
        acc = ab_ref[l:l + 1, lanes]
        for d in range(ka):
            a, s = divmod(d, SUBLANES)
            acc = acc + aw_ref[l, ka - 1 - d:ka - d, lanes] * us[s, pl.ds(row0 - SUBLANES * a, rows), lanes]
        out.append(acc)
    return jnp.concatenate(out, axis=1)


def _convmix_fwd(name, p, aw, ab, lg, lb, bw, l, exchange=None):
    S, W = p.shape
    dg = W // 5
    tm = _tile(S, 256, HALO_A)
    nb = tm // HALO_A
    ka, kb = CONV_A_WIDTH, CONV_B_WIDTH

    ext = HALO_A + tm
    rc = _tile(tm, ELT_ROWS, BF16_ROWS)

    def body(p_ref, ph_ref, aw_ref, ab_ref, lg_ref, lb_ref, bw_ref, o_ref, us, mext):
        first = pl.program_id(0) == 0
        ph = ph_ref[...]
        pc = p_ref[...]
        us[0, pl.ds(0, HALO_A), :] = jnp.where(first, 0.0, ph[:, 0:dg] * _sig(ph[:, dg:2 * dg]))
        us[0, pl.ds(HALO_A, tm), :] = pc[:, 0:dg] * _sig(pc[:, dg:2 * dg])
        mext[pl.ds(0, HALO_A), :] = jnp.where(first, 0.0, ph[:, 3 * dg:4 * dg] * ph[:, 4 * dg:5 * dg])
        mext[pl.ds(HALO_A, tm), :] = pc[:, 3 * dg:4 * dg] * pc[:, 4 * dg:5 * dg]
        _delayed_copies(us, ext)
        for r0 in range(0, tm, rc):
            rows = pl.ds(r0, rc)
            c = _conv_a(aw_ref, ab_ref, l, us, HALO_A + r0, rc, dg)
            xc = c - jnp.mean(c, axis=-1, keepdims=True)
            ln = xc * lax.rsqrt(jnp.mean(xc * xc, axis=-1, keepdims=True) + EPS) * lg_ref[l:l + 1, :] + lb_ref[l:l + 1, :]
            o_ref[rows, 0:dg] = (ln * _sig(ln)).astype(BF16)
            cb = bw_ref[l, 0:1, :] * mext[pl.ds(HALO_A - (kb - 1) + r0, rc), :]
            for k in range(1, kb):
                cb = cb + bw_ref[l, k:k + 1, :] * mext[pl.ds(HALO_A - (kb - 1) + k + r0, rc), :]
            o_ref[rows, dg:2 * dg] = (p_ref[rows, 2 * dg:3 * dg] * cb).astype(BF16)

    full = lambda a: pl.BlockSpec(a.shape, lambda i: (0,) * a.ndim)
    n_i = S // tm
    body, in_specs, out_specs, out_shape, scratch, operands, aliases = _with_exchange(
        exchange, body,
        [pl.BlockSpec((tm, W), lambda i: (i, 0)), pl.BlockSpec((HALO_A, W), lambda i: (jnp.maximum(i * nb - 1, 0), 0)),
         full(aw), full(ab), full(lg), full(lb), full(bw)],
        [pl.BlockSpec((tm, 2 * dg), lambda i: (i, 0))], [_sds((S, 2 * dg), BF16)],
        [pltpu.VMEM((SUBLANES, ext, dg), F32), pltpu.VMEM((ext, dg), F32)], [p, p, aw, ab, lg, lb, bw],
        lambda: pl.program_id(0) == 0, lambda: pl.program_id(0) == n_i - 1)
    outs = _pcall(
        body, grid=(n_i,), in_specs=in_specs, out_specs=out_specs, out_shape=out_shape, scratch_shapes=scratch,
        input_output_aliases=aliases, compiler_params=_cp("arbitrary" if exchange else "parallel"), name=name,
    )(*operands)
    return outs if exchange else outs[0]


def _convmix_bwd(name, p, dab, aw, ab, lg, lb, bw, l, exchange=None):
    S, W = p.shape
    dg = W // 5
    tm = _tile(S, 256, HALO_A)
    nb = tm // HALO_A
    n_i = S // tm
    ka, kb = CONV_A_WIDTH, CONV_B_WIDTH
    n = tm + HALO_A
    ext = HALO_A + n
    rc = _tile(tm, ELT_ROWS, BF16_ROWS)

    def body(p_ref, pp_ref, pn_ref, d_ref, dn_ref, aw_ref, ab_ref, lg_ref, lb_ref, bw_ref,
             dp_ref, daw_ref, dab_ref, dlg_ref, dlb_ref, dbw_ref, us, mext, dcs, dbext, accw):
        i = pl.program_id(0)
        first, last = i == 0, i == n_i - 1

        @pl.when(first)
        def _():
            for r in (daw_ref, dab_ref, dlg_ref, dlb_ref, dbw_ref):
                r[...] = jnp.zeros_like(r)

        accw[...] = jnp.zeros_like(accw)
        pp, pc, pn = pp_ref[...], p_ref[...], pn_ref[...]
        glu = lambda b: b[:, 0:dg] * _sig(b[:, dg:2 * dg])
        gch = lambda b: b[:, 3 * dg:4 * dg] * b[:, 4 * dg:5 * dg]
        us[0, pl.ds(0, HALO_A), :] = jnp.where(first, 0.0, glu(pp))
        us[0, pl.ds(HALO_A, tm), :] = glu(pc)
        us[0, pl.ds(HALO_A + tm, HALO_A), :] = glu(pn)
        mext[pl.ds(0, HALO_A), :] = jnp.where(first, 0.0, gch(pp))
        mext[pl.ds(HALO_A, tm), :] = gch(pc)
        mext[pl.ds(HALO_A + tm, HALO_A), :] = gch(pn)
        _delayed_copies(us, ext)
        chunks = [(r, rc) for r in range(0, tm, rc)] + [(tm, HALO_A)]
        g_ln = lg_ref[l:l + 1, :]
        zero8 = jnp.zeros((SUBLANES, dg), F32)

        acc_lg = acc_lb = acc_ab = zero8
        for r0, rows in chunks:
            c = _conv_a(aw_ref, ab_ref, l, us, HALO_A + r0, rows, dg)
            xc = c - jnp.mean(c, axis=-1, keepdims=True)
            rstd = lax.rsqrt(jnp.mean(xc * xc, axis=-1, keepdims=True) + EPS)
            chat = xc * rstd
            ln = chat * g_ln + lb_ref[l:l + 1, :]
            s = _sig(ln)
            da = d_ref[pl.ds(r0, rows), 0:dg] if r0 < tm else jnp.where(last, 0.0, dn_ref[:, 0:dg])
            dln = da * (s * (1.0 + ln * (1.0 - s)))
            dlnh = dln * g_ln
            dc = rstd * (dlnh - jnp.mean(dlnh, axis=-1, keepdims=True)
                         - chat * jnp.mean(dlnh * chat, axis=-1, keepdims=True))
            dcs[0, pl.ds(r0, rows), :] = dc
            if r0 < tm:
                acc_lg = acc_lg + _fold(dln * chat)
                acc_lb = acc_lb + _fold(dln)
                acc_ab = acc_ab + _fold(dc)
                for c0 in range(0, dg, LANES):
                    lanes = slice(c0, c0 + LANES)
                    for d in range(ka):
                        a, sh = divmod(d, SUBLANES)
                        k = ka - 1 - d
                        accw[pl.ds(SUBLANES * k, SUBLANES), lanes] += _fold(
                            dc[:, lanes] * us[sh, pl.ds(HALO_A + r0 - SUBLANES * a, rows), lanes])
        dlg_ref[...] += _rowsum(acc_lg)
        dlb_ref[...] += _rowsum(acc_lb)
        dab_ref[...] += _rowsum(acc_ab)
        for k in range(ka):
            daw_ref[k:k + 1, :] += _rowsum(accw[pl.ds(SUBLANES * k, SUBLANES), :])
        for s in range(1, SUBLANES):
            dcs[s, pl.ds(0, n - SUBLANES), :] = dcs[0, pl.ds(s, n - SUBLANES), :]
        for r0 in range(0, tm, rc):
            rows = pl.ds(r0, rc)
            parts = []
            for c0 in range(0, dg, LANES):
                lanes = slice(c0, c0 + LANES)
                acc = aw_ref[l, ka - 1:ka, lanes] * dcs[0, rows, lanes]
                for e in range(1, ka):
                    a, sh = divmod(e, SUBLANES)
                    acc = acc + aw_ref[l, ka - 1 - e:ka - e, lanes] * dcs[sh, pl.ds(r0 + SUBLANES * a, rc), lanes]
                parts.append(acc)
            du = jnp.concatenate(parts, axis=1)
            sg = _sig(p_ref[rows, dg:2 * dg])
            dp_ref[rows, 0:dg] = (du * sg).astype(BF16)
            dp_ref[rows, dg:2 * dg] = (du * p_ref[rows, 0:dg] * sg * (1.0 - sg)).astype(BF16)

        for r0, rows in chunks:
            if r0 < tm:
                dbext[pl.ds(r0, rows), :] = d_ref[pl.ds(r0, rows), dg:2 * dg] * p_ref[pl.ds(r0, rows), 2 * dg:3 * dg]
            else:
                dbext[pl.ds(r0, rows), :] = jnp.where(last, 0.0, dn_ref[:, dg:2 * dg] * pn[:, 2 * dg:3 * dg])
        acc_bw = [zero8] * kb
        for r0 in range(0, tm, rc):
            rows = pl.ds(r0, rc)
            m_k = [mext[pl.ds(HALO_A - (kb - 1) + k + r0, rc), :] for k in range(kb)]
            cb = bw_ref[l, 0:1, :] * m_k[0]
            dm = bw_ref[l, 0:1, :] * dbext[pl.ds(r0 + kb - 1, rc), :]
            for k in range(1, kb):
                cb = cb + bw_ref[l, k:k + 1, :] * m_k[k]
                dm = dm + bw_ref[l, k:k + 1, :] * dbext[pl.ds(r0 + kb - 1 - k, rc), :]
            dcb = dbext[rows, :]
            acc_bw = [acc_bw[k] + _fold(dcb * m_k[k]) for k in range(kb)]
            dp_ref[rows, 2 * dg:3 * dg] = (d_ref[rows, dg:2 * dg] * cb).astype(BF16)
            dp_ref[rows, 3 * dg:4 * dg] = (dm * p_ref[rows, 4 * dg:5 * dg]).astype(BF16)
            dp_ref[rows, 4 * dg:5 * dg] = (dm * p_ref[rows, 3 * dg:4 * dg]).astype(BF16)
        for k in range(kb):
            dbw_ref[k:k + 1, :] += _rowsum(acc_bw[k])

    full = lambda a: pl.BlockSpec(a.shape, lambda i: (0,) * a.ndim)
    prev = lambda i: (jnp.maximum(i * nb - 1, 0), 0)
    nxt = lambda i: (jnp.minimum((i + 1) * nb, S // HALO_A - 1), 0)
    acc = lambda r: pl.BlockSpec((r, dg), lambda i: (0, 0))
    body, in_specs, out_specs, out_shape, scratch, operands, aliases = _with_exchange(
        exchange, body,
        [pl.BlockSpec((tm, W), lambda i: (i, 0)), pl.BlockSpec((HALO_A, W), prev), pl.BlockSpec((HALO_A, W), nxt),
         pl.BlockSpec((tm, 2 * dg), lambda i: (i, 0)), pl.BlockSpec((HALO_A, 2 * dg), nxt),
         full(aw), full(ab), full(lg), full(lb), full(bw)],
        [pl.BlockSpec((tm, W), lambda i: (i, 0)), acc(ka), acc(1), acc(1), acc(1), acc(kb)],
        [_sds((S, W), BF16), _sds((ka, dg), F32), _sds((1, dg), F32), _sds((1, dg), F32), _sds((1, dg), F32),
         _sds((kb, dg), F32)],
        [pltpu.VMEM((SUBLANES, ext, dg), F32), pltpu.VMEM((ext, dg), F32), pltpu.VMEM((SUBLANES, n, dg), F32),
         pltpu.VMEM((n, dg), F32), pltpu.VMEM((SUBLANES * ka, dg), F32)],
        [p, p, p, dab, dab, aw, ab, lg, lb, bw],
        lambda: pl.program_id(0) == 0, lambda: pl.program_id(0) == n_i - 1)
    return _pcall(
        body, grid=(n_i,), in_specs=in_specs, out_specs=out_specs, out_shape=out_shape, scratch_shapes=scratch,
        input_output_aliases=aliases, compiler_params=_cp("arbitrary"), name=name,
    )(*operands)


def _ffn_mid_fwd(name, u2, dww, dwb, l, exchange=None):
    _, S, F = u2.shape
    tm = _tile(S, 256, BF16_ROWS)
    tc = _tile(F, 1408)
    n_f = F // tc
    nb = tm // HALO_S
    kf = FFN_CONV_WIDTH

    def body(u_ref, uh_ref, wg_ref, wv_ref, bg_ref, bv_ref, o_ref, ext):
        first = pl.program_id(1) == 0
        ext[:, pl.ds(0, HALO_S), :] = jnp.where(first, 0.0, uh_ref[...])
        ext[:, pl.ds(HALO_S, tm), :] = u_ref[...]
        rc = _tile(tm, ELT_ROWS, BF16_ROWS)

        def lane_chunk(ci, carry):
            lanes = pl.ds(pl.multiple_of(ci * LANES, LANES), LANES)
            taps = [[w_ref[k:k + 1, lanes] for k in range(kf)] for w_ref in (wg_ref, wv_ref)]
            bias = [b_ref[l:l + 1, lanes] for b_ref in (bg_ref, bv_ref)]
            for r0 in range(0, tm, rc):
                c = []
                for g in range(2):
                    acc = bias[g]
                    for k in range(kf):
                        acc = acc + taps[g][k] * ext[g, pl.ds(HALO_S - (kf - 1) + k + r0, rc), lanes]
                    c.append(acc)
                o_ref[pl.ds(r0, rc), lanes] = (c[0] * _sig(c[0]) * c[1]).astype(BF16)
            return carry

        lax.fori_loop(0, tc // LANES, lane_chunk, 0)

    n_l = dwb.shape[0]
    n_i = S // tm
    body, in_specs, out_specs, out_shape, scratch, operands, aliases = _with_exchange(
        exchange, body,
        [pl.BlockSpec((2, tm, tc), lambda j, i: (0, i, j)),
         pl.BlockSpec((2, HALO_S, tc), lambda j, i: (0, jnp.maximum(i * nb - 1, 0), j)),
         pl.BlockSpec((None, kf, tc), lambda j, i: (l, 0, j)),
         pl.BlockSpec((None, kf, tc), lambda j, i: (l, 0, j + n_f)),
         pl.BlockSpec((n_l, tc), lambda j, i: (0, j)),
         pl.BlockSpec((n_l, tc), lambda j, i: (0, j + n_f))],
        [pl.BlockSpec((tm, tc), lambda j, i: (i, j))], [_sds((S, F), BF16)],
        [pltpu.VMEM((2, HALO_S + tm, tc), F32)], [u2, u2, dww, dww, dwb, dwb],
        lambda: jnp.logical_and(pl.program_id(0) == 0, pl.program_id(1) == 0),
        lambda: jnp.logical_and(pl.program_id(0) == n_f - 1, pl.program_id(1) == n_i - 1))
    sem = "arbitrary" if exchange else "parallel"
    outs = _pcall(
        body, grid=(n_f, n_i), in_specs=in_specs, out_specs=out_specs, out_shape=out_shape, scratch_shapes=scratch,
        input_output_aliases=aliases, compiler_params=_cp(sem, sem), name=name,
    )(*operands)
    return outs if exchange else outs[0]


def _ffn_mid_bwd(name, u2, df, dww, dwb, l, exchange=None):
    _, S, F = u2.shape
    tm = _tile(S, 256, BF16_ROWS)
    tc = _tile(F, 1408)
    n_f = F // tc
    nb = tm // HALO_S
    n_i = S // tm
    kf = FFN_CONV_WIDTH
    n = tm + HALO_S

    def body(u_ref, up_ref, un_ref, df_ref, dfn_ref, wg_ref, wv_ref, bg_ref, bv_ref,
             du_ref, dw_ref, db_ref, uext, dcext):
        i = pl.program_id(1)
        first, last = i == 0, i == n_i - 1

        @pl.when(first)
        def _():
            dw_ref[...] = jnp.zeros_like(dw_ref)
            db_ref[...] = jnp.zeros_like(db_ref)

        uext[:, pl.ds(0, HALO_S), :] = jnp.where(first, 0.0, up_ref[...])
        uext[:, pl.ds(HALO_S, tm), :] = u_ref[...]
        uext[:, pl.ds(HALO_S + tm, HALO_S), :] = un_ref[...]
        rc = _tile(tm, ELT_ROWS, BF16_ROWS)

        def lane_chunk(ci, carry):
            lanes = pl.ds(pl.multiple_of(ci * LANES, LANES), LANES)
            taps = [[w_ref[k:k + 1, lanes] for k in range(kf)] for w_ref in (wg_ref, wv_ref)]
            bias = [b_ref[l:l + 1, lanes] for b_ref in (bg_ref, bv_ref)]
            acc_w = [[jnp.zeros((SUBLANES, LANES), F32) for _ in range(kf)] for _ in range(2)]
            acc_b = [jnp.zeros((SUBLANES, LANES), F32) for _ in range(2)]
            for r0, rows in [(r, rc) for r in range(0, tm, rc)] + [(tm, HALO_S)]:
                shifted = [[uext[g, pl.ds(HALO_S - (kf - 1) + k + r0, rows), lanes] for k in range(kf)] for g in range(2)]
                conv = []
                for g in range(2):
                    acc = bias[g]
                    for k in range(kf):
                        acc = acc + taps[g][k] * shifted[g][k]
                    conv.append(acc)
                cg, cv = conv
                s = _sig(cg)
                dfe = df_ref[pl.ds(r0, rows), lanes] if r0 < tm else jnp.where(last, 0.0, dfn_ref[:, lanes])
                dc = [dfe * cv * (s * (1.0 + cg * (1.0 - s))), dfe * (cg * s)]
                for g in range(2):
                    dcext[g, pl.ds(r0, rows), lanes] = dc[g]
                    if r0 < tm:
                        acc_b[g] = acc_b[g] + _fold(dc[g])
                        for k in range(kf):
                            acc_w[g][k] = acc_w[g][k] + _fold(dc[g] * shifted[g][k])
            for r0 in range(0, tm, rc):
                for g in range(2):
                    du = taps[g][0] * dcext[g, pl.ds(r0 + kf - 1, rc), lanes]
                    for k in range(1, kf):
                        du = du + taps[g][k] * dcext[g, pl.ds(r0 + kf - 1 - k, rc), lanes]
                    du_ref[g, pl.ds(r0, rc), lanes] = du.astype(BF16)
            for g in range(2):
                db_ref[g, :, lanes] += _rowsum(acc_b[g])
                for k in range(kf):
                    dw_ref[g, k:k + 1, lanes] += _rowsum(acc_w[g][k])
            return carry

        lax.fori_loop(0, tc // LANES, lane_chunk, 0)

    n_l = dwb.shape[0]
    prev = lambda j, i: (0, jnp.maximum(i * nb - 1, 0), j)
    nxt = lambda j, i: (0, jnp.minimum((i + 1) * nb, S // HALO_S - 1), j)
    body, in_specs, out_specs, out_shape, scratch, operands, aliases = _with_exchange(
        exchange, body,
        [pl.BlockSpec((2, tm, tc), lambda j, i: (0, i, j)),
         pl.BlockSpec((2, HALO_S, tc), prev), pl.BlockSpec((2, HALO_S, tc), nxt),
         pl.BlockSpec((tm, tc), lambda j, i: (i, j)),
         pl.BlockSpec((HALO_S, tc), lambda j, i: nxt(j, i)[1:]),
         pl.BlockSpec((None, kf, tc), lambda j, i: (l, 0, j)),
         pl.BlockSpec((None, kf, tc), lambda j, i: (l, 0, j + n_f)),
         pl.BlockSpec((n_l, tc), lambda j, i: (0, j)),
         pl.BlockSpec((n_l, tc), lambda j, i: (0, j + n_f))],
        [pl.BlockSpec((2, tm, tc), lambda j, i: (0, i, j)),
         pl.BlockSpec((2, kf, tc), lambda j, i: (0, 0, j)),
         pl.BlockSpec((2, 1, tc), lambda j, i: (0, 0, j))],
        [_sds((2, S, F), BF16), _sds((2, kf, F), F32), _sds((2, 1, F), F32)],
        [pltpu.VMEM((2, HALO_S + n, tc), F32), pltpu.VMEM((2, n, tc), F32)],
        [u2, u2, u2, df, df, dww, dww, dwb, dwb],
        lambda: jnp.logical_and(pl.program_id(0) == 0, pl.program_id(1) == 0),
        lambda: jnp.logical_and(pl.program_id(0) == n_f - 1, pl.program_id(1) == n_i - 1))
    return _pcall(
        body, grid=(n_f, n_i), in_specs=in_specs, out_specs=out_specs, out_shape=out_shape, scratch_shapes=scratch,
        input_output_aliases=aliases, compiler_params=_cp("arbitrary" if exchange else "parallel", "arbitrary"), name=name,
    )(*operands)


def _head_sum_matrix():
    r = lax.broadcasted_iota(jnp.int32, (LANES, LANES), 0) // HEAD_DIM
    c = lax.broadcasted_iota(jnp.int32, (LANES, LANES), 1) // HEAD_DIM
    return (r == c).astype(BF16)


def _head_mean(x, ones):
    return _split_dot(x, ones) * (1.0 / HEAD_DIM)


def _qknorm_fwd(name, qkv, g2):
    S, D3 = qkv.shape
    D = D3 // 3
    tm = _tile(S, 256, BF16_ROWS)
    scale = HEAD_DIM ** -0.5

    def body(q_ref, k_ref, v_ref, g_ref, qo_ref, ko_ref, vo_ref):
        ones = _head_sum_matrix()
        for cc in range(D // LANES):
            sl = slice(cc * LANES, (cc + 1) * LANES)
            for x_ref, o_ref, row, mult in ((q_ref, qo_ref, 0, scale), (k_ref, ko_ref, 1, 1.0)):
                x = x_ref[:, sl]
                r = lax.rsqrt(_head_mean(x * x, ones) + EPS)
                o_ref[:, sl] = ((x * r * g_ref[row:row + 1, :]).astype(BF16) * mult).astype(BF16)
        vo_ref[...] = v_ref[...].astype(BF16)

    col = lambda c: pl.BlockSpec((tm, D), lambda i: (i, c))
    out = pl.BlockSpec((tm, D), lambda i: (i, 0))
    return _pcall(
        body, grid=(S // tm,),
        in_specs=[col(0), col(1), col(2), pl.BlockSpec(g2.shape, lambda i: (0, 0))],
        out_specs=[out, out, out], out_shape=[_sds((S, D), BF16)] * 3,
        compiler_params=_cp("parallel"), name=name,
    )(qkv, qkv, qkv, g2)


def _qknorm_bwd(name, qkv, dq, dk, dv, g2):
    S, D3 = qkv.shape
    D = D3 // 3
    tm = _tile(S, 256, BF16_ROWS)
    scale = HEAD_DIM ** -0.5

    def body(q_ref, k_ref, dq_ref, dk_ref, dv_ref, g_ref, o_ref, dg_ref):
        @pl.when(pl.program_id(0) == 0)
        def _():
            dg_ref[...] = jnp.zeros_like(dg_ref)

        ones = _head_sum_matrix()
        for cc in range(D // LANES):
            sl = slice(cc * LANES, (cc + 1) * LANES)
            for x_ref, d_ref, row, mult, base in ((q_ref, dq_ref, 0, scale, 0), (k_ref, dk_ref, 1, 1.0, D)):
                x = x_ref[:, sl]
                r = lax.rsqrt(_head_mean(x * x, ones) + EPS)
                xh = x * r
                dn = d_ref[:, sl] * mult
                dxh = dn * g_ref[row:row + 1, :]
                dx = r * (dxh - xh * _head_mean(dxh * xh, ones))
                o_ref[:, base + cc * LANES:base + (cc + 1) * LANES] = dx.astype(BF16)
                dg_ref[row:row + 1, :] += _rowsum(dn * xh)
        o_ref[:, 2 * D:3 * D] = dv_ref[...].astype(BF16)

    col = lambda c: pl.BlockSpec((tm, D), lambda i: (i, c))
    row = pl.BlockSpec((tm, D), lambda i: (i, 0))
    return _pcall(
        body, grid=(S // tm,),
        in_specs=[col(0), col(1), row, row, row, pl.BlockSpec(g2.shape, lambda i: (0, 0))],
        out_specs=[pl.BlockSpec((tm, D3), lambda i: (i, 0)), pl.BlockSpec((2, LANES), lambda i: (0, 0))],
        out_shape=[_sds((S, D3), BF16), _sds((2, LANES), F32)],
        compiler_params=_cp("arbitrary"), name=name,
    )(qkv, qkv, dq, dk, dv, g2)


def _attn_consts():
    t = ATTN_BLOCK
    row = lax.broadcasted_iota(jnp.int32, (t, t), 0)
    col = lax.broadcasted_iota(jnp.int32, (t, t), 1)
    lane = lax.broadcasted_iota(jnp.int32, (1, LANES), 1)
    heads = (lane < HEAD_DIM, lane >= HEAD_DIM)
    return row, col, heads


def _split_dot(x, m):
    n = x.shape[0]
    hi = x.astype(BF16)
    lo = (x - hi.astype(F32)).astype(BF16)
    both = jnp.dot(jnp.concatenate([hi, lo], axis=0), m, preferred_element_type=F32)
    return both[:n] + both[n:]


def _log_keep(z):
    return -(jnp.maximum(z, 0.0) + jnp.log(1.0 + jnp.exp(-jnp.abs(z))))


def _stack_heads(a, heads):
    t = ATTN_BLOCK
    zero = jnp.zeros((t, LANES), a.dtype)
    return jnp.concatenate([jnp.where(h, a[s * t:(s + 1) * t], zero) for s in range(a.shape[0] // t) for h in heads], axis=0)


def _side_by_side(a):
    t = ATTN_BLOCK
    return jnp.concatenate([jnp.concatenate([a[2 * s * t:(2 * s + 1) * t], a[(2 * s + 1) * t:(2 * s + 2) * t]], axis=1)
                            for s in range(a.shape[0] // (2 * t))], axis=0)


def _grow(a, rows, cols):
    z = jnp.zeros((rows, cols), F32)
    return z if a is None else jnp.concatenate([z, a], axis=0)


def _attn_fwd(name, qs, kn, vb, exchange=None):
    S, D = qs.shape
    t = ATTN_BLOCK
    tq = ATTN_SUB * t

    def body(q_ref, k_ref, v_ref, o_ref):
        i = pl.program_id(1)
        row, col, heads = _attn_consts()
        after_m = (row > col).astype(BF16)
        causal = col < row
        q_all = _stack_heads(q_ref[...], heads)

        def blocks(specs, r, acc):
            n_rows = q_all.shape[0]
            offs = [pl.multiple_of(j * t, t) for j, _, _ in specs]
            zs = [lax.dot_general(q_all[lo:], k_ref[pl.ds(off, t), :], NT, preferred_element_type=F32)
                  for off, (_, lo, _) in zip(offs, specs)]
            lks = []
            for z, (_, _, mask) in zip(zs, specs):
                lk = _log_keep(z)
                lks.append(lk if mask is None else jnp.where(mask, lk, 0.0))
            cums = [_split_dot(lk, after_m) for lk in lks]
            ws = []
            for z, lk, cum, (_, lo, mask) in zip(zs, lks, cums, specs):
                rows = n_rows - lo
                r = _grow(r, rows - (0 if r is None else r.shape[0]), 1) if r is None or r.shape[0] < rows else r
                w = jnp.exp(z + lk + cum + r)
                ws.append((w if mask is None else jnp.where(mask, w, 0.0)).astype(BF16))
                r = r + jnp.sum(lk, axis=1, keepdims=True)
            acc = jnp.zeros((n_rows // 2, LANES), F32) if acc is None else acc
            for w, off, (_, lo, _) in zip(ws, offs, specs):
                part = jnp.dot(_side_by_side(w), _stack_heads(v_ref[pl.ds(off, t), :], heads), preferred_element_type=F32)
                acc = acc + (part if lo == 0 else _grow(part, lo // 2, LANES))
            return r, acc

        def head(n_more):
            specs = [(ATTN_SUB * i + s, 2 * s * t,
                      jnp.concatenate([causal, causal] + [jnp.ones_like(causal)] * (2 * (ATTN_SUB - 1 - s)), axis=0))
                     for s in reversed(range(ATTN_SUB))]
            specs += [(ATTN_SUB * i - 1 - b, 0, None) for b in range(n_more)]
            return blocks(specs, None, None)

        r, acc = lax.cond(ATTN_SUB * i >= ATTN_MORE, lambda: head(ATTN_MORE), lambda: head(0))

        def cond(c):
            return jnp.logical_and(c[0] >= 0, jnp.max(c[1]) > EXP_UNDERFLOW)

        def step(c):
            r, a = blocks([(c[0], 0, None)], c[1], c[2])
            return c[0] - 1, r, a

        first = jnp.where(ATTN_SUB * i >= ATTN_MORE, ATTN_SUB * i - 1 - ATTN_MORE, ATTN_SUB * i - 1)
        o_ref[...] = lax.while_loop(cond, step, (first, r, acc))[2]

    n_hp = D // LANES
    blk = pl.BlockSpec((tq, LANES), lambda hp, i: (i, hp))
    seq = pl.BlockSpec((S, LANES), lambda hp, i: (0, hp))
    n_i = S // tq
    body, in_specs, out_specs, out_shape, scratch, operands, aliases = _with_exchange(
        exchange, body, [blk, seq, seq], [blk], [_sds((S, D), F32)], [], [qs, kn, vb],
        lambda: jnp.logical_and(pl.program_id(0) == 0, pl.program_id(1) == 0),
        lambda: jnp.logical_and(pl.program_id(0) == n_hp - 1, pl.program_id(1) == n_i - 1))
    outs = _pcall(
        body, grid=(n_hp, n_i), in_specs=in_specs, out_specs=out_specs, out_shape=out_shape, scratch_shapes=scratch,
        input_output_aliases=aliases, compiler_params=_cp("arbitrary" if exchange else "parallel", "arbitrary"), name=name,
    )(*operands)
    return outs if exchange else outs[0]


def _attn_bwd(name, qs, kn, vb, o, do):
    S, D = qs.shape
    t = ATTN_BLOCK
    tq = ATTN_SUB * t

    def body(q_ref, k_ref, v_ref, o_ref, do_ref, dq_ref, dk_ref, dv_ref):
        i = pl.program_id(1)

        @pl.when(i == 0)
        def _():
            dk_ref[...] = jnp.zeros_like(dk_ref)
            dv_ref[...] = jnp.zeros_like(dv_ref)

        row, col, heads = _attn_consts()
        after_m = (row > col).astype(BF16)
        from_m = (row >= col).astype(BF16)
        causal = col < row
        q_all = _stack_heads(q_ref[...], heads)
        dob = do_ref[...].astype(BF16)
        do_all = _stack_heads(dob, heads)
        dsum_all = jnp.sum(_stack_heads(dob.astype(F32) * o_ref[...], heads), axis=1, keepdims=True)

        def blocks(specs, r, es, dq):
            n_rows = q_all.shape[0]
            offs = [pl.multiple_of(j * t, t) for j, _, _ in specs]
            masked = lambda x, mask: x if mask is None else jnp.where(mask, x, 0.0)
            top = lambda a, rows: a if a is not None and a.shape[0] == rows else _grow(a, rows - (0 if a is None else a.shape[0]), 1)
            zs = [lax.dot_general(q_all[lo:], k_ref[pl.ds(off, t), :], NT, preferred_element_type=F32)
                  for off, (_, lo, _) in zip(offs, specs)]
            gs = [lax.dot_general(do_all[lo:], v_ref[pl.ds(off, t), :], NT, preferred_element_type=F32)
                  for off, (_, lo, _) in zip(offs, specs)]
            lks = [masked(_log_keep(z), mask) for z, (_, _, mask) in zip(zs, specs)]
            cums = [_split_dot(lk, after_m) for lk in lks]
            ws, es_blk, sgs = [], [], []
            for z, g, lk, cum, (_, lo, mask) in zip(zs, gs, lks, cums, specs):
                r = top(r, n_rows - lo)
                ls = z + lk
                w = masked(jnp.exp(ls + cum + r), mask)
                ws.append(w.astype(BF16))
                es_blk.append(w * g)
                sgs.append(jnp.exp(ls))
                r = r + jnp.sum(lk, axis=1, keepdims=True)
            cum_es = [_split_dot(e, from_m) for e in es_blk]
            dzs = []
            for e, cum_e, sg, (_, lo, mask) in zip(es_blk, cum_es, sgs, specs):
                es = top(es, n_rows - lo)
                before = dsum_all[lo:] - (es + cum_e)
                dzs.append(masked(e - (e + before) * sg, mask).astype(BF16))
                es = es + jnp.sum(e, axis=1, keepdims=True)
            dq = jnp.zeros((n_rows // 2, LANES), F32) if dq is None else dq
            for dzb, w, off, (_, lo, _) in zip(dzs, ws, offs, specs):
                part = jnp.dot(_side_by_side(dzb), _stack_heads(k_ref[pl.ds(off, t), :], heads), preferred_element_type=F32)
                dq = dq + (part if lo == 0 else _grow(part, lo // 2, LANES))
                dk_ref[pl.ds(off, t), :] += lax.dot_general(dzb, q_all[lo:], TN, preferred_element_type=F32)
                dv_ref[pl.ds(off, t), :] += lax.dot_general(w, do_all[lo:], TN, preferred_element_type=F32)
            return r, es, dq

        def head(n_more):
            specs = [(ATTN_SUB * i + s, 2 * s * t,
                      jnp.concatenate([causal, causal] + [jnp.ones_like(causal)] * (2 * (ATTN_SUB - 1 - s)), axis=0))
                     for s in reversed(range(ATTN_SUB))]
            specs += [(ATTN_SUB * i - 1 - b, 0, None) for b in range(n_more)]
            return blocks(specs, None, None, None)

        r, es, dq = lax.cond(ATTN_SUB * i >= ATTN_MORE, lambda: head(ATTN_MORE), lambda: head(0))

        def cond(c):
            return jnp.logical_and(c[0] >= 0, jnp.max(c[1]) > EXP_UNDERFLOW)

        def step(c):
            r, es, a = blocks([(c[0], 0, None)], c[1], c[2], c[3])
            return c[0] - 1, r, es, a

        first = jnp.where(ATTN_SUB * i >= ATTN_MORE, ATTN_SUB * i - 1 - ATTN_MORE, ATTN_SUB * i - 1)
        dq_ref[...] = lax.while_loop(cond, step, (first, r, es, dq))[3]

    n_hp = D // LANES
    blk = pl.BlockSpec((tq, LANES), lambda hp, i: (i, hp))
    seq = pl.BlockSpec((S, LANES), lambda hp, i: (0, hp))
    return _pcall(
        body, grid=(n_hp, S // tq), in_specs=[blk, seq, seq, blk, blk], out_specs=[blk, seq, seq],
        out_shape=[_sds((S, D), F32)] * 3, compiler_params=_cp("parallel", "arbitrary"), name=name,
    )(qs, kn, vb, o, do)


def _adamw(name, w, g, m, v):
    L, R, C = w.shape
    tr = _tile(R, 256, SUBLANES)
    c1 = 1.0 - ADAM_B1 ** ADAM_STEP
    c2 = 1.0 - ADAM_B2 ** ADAM_STEP

    def body(w_ref, g_ref, m_ref, v_ref, d_ref, mo_ref, vo_ref):
        gg = g_ref[...]
        mn = ADAM_B1 * m_ref[...] + (1.0 - ADAM_B1) * gg
        vn = ADAM_B2 * v_ref[...] + (1.0 - ADAM_B2) * (gg * gg)
        d_ref[...] = -ADAM_LR * ((mn / c1) / (jnp.sqrt(vn / c2) + ADAM_EPS) + ADAM_WD * w_ref[...])
        mo_ref[...] = mn
        vo_ref[...] = vn

    blk = pl.BlockSpec((None, tr, C), lambda l, i: (l, i, 0))
    return _pcall(
        body, grid=(L, R // tr), in_specs=[blk] * 4, out_specs=[blk] * 3, out_shape=[_sds(w.shape, F32)] * 3,
        compiler_params=_cp("parallel", "parallel"), name=name,
    )(w, g, m, v)


def _place():
    x, y, c = lax.axis_index("x"), lax.axis_index("y"), lax.axis_index("c")
    chips = [(1 - x, y), (x, 1 - y), (1 - x, 1 - y)]
    return x, y, c, chips


def _place_shard(name, w, j_idx):
    L, R, X = w.shape
    rh = R // 2
    tr = _tile(rh, 256, BF16_ROWS)

    def body(j_ref, w_ref, o_ref):
        o_ref[...] = w_ref[...].astype(BF16)

    return _pcall(
        body,
        grid_spec=pltpu.PrefetchScalarGridSpec(
            num_scalar_prefetch=1, grid=(L, 2, rh // tr),
            in_specs=[pl.BlockSpec((None, None, tr, X), lambda l, h, i, j_ref: (l, h, i, 0))],
            out_specs=pl.BlockSpec((None, None, None, tr, X), lambda l, h, i, j_ref: (l, j_ref[0], h, i, 0))),
        out_shape=_sds((L, N_CHIPS, 2, rh, X), BF16), compiler_params=_cp("parallel", "parallel", "parallel"), name=name,
    )(j_idx, w.reshape(L, 2, rh, X))


def _all_gather_weights(bufs, spans, small_ws):
    n_big, n_small = len(bufs), len(small_ws)
    n_in = n_big + n_small
    layers = [pl.ds(l0, n) for l0, n in spans]

    def body(*refs):
        ins, outs = refs[:n_in], refs[n_in:2 * n_in]
        send_sems, recv_sems, local_sems = refs[2 * n_in:]
        x, y, c, chips = _place()
        j_me = 2 * x + y
        j_of = [2 * cx + cy for cx, cy in chips]
        sibling = (x, y, 1 - c)

        def remote(src, dst, s, to):
            return pltpu.make_async_remote_copy(src_ref=src, dst_ref=dst, send_sem=send_sems.at[s], recv_sem=recv_sems.at[s],
                                                device_id=to, device_id_type=MESH)

        started = []
        for t in range(n_big, n_in):
            loc = pltpu.make_async_copy(ins[t], outs[t].at[:, j_me], local_sems.at[t - n_big])
            loc.start()
            started.append(loc)
        first = []
        for t in range(n_big):
            mine = outs[t].at[layers[t], j_me, c]
            for k in range(3):
                first.append(remote(mine, mine, 6 * t + k, (*chips[k], c)))
        for t in range(n_big, n_in):
            for k in range(3):
                first.append(remote(ins[t], outs[t].at[:, j_me], 6 * n_big + 3 * (t - n_big) + k, (*chips[k], c)))
        for cp in first:
            cp.start()
        passed = []
        for t in range(n_big):
            for k in range(3):
                landed = outs[t].at[layers[t], j_of[k], c]
                remote(landed, landed, 6 * t + k, (*chips[k], c)).wait_recv()
                fwd = remote(landed, landed, 6 * t + 3 + k, sibling)
                fwd.start()
                passed.append(fwd)
        for t in range(n_big):
            for k in range(3):
                other = outs[t].at[layers[t], j_of[k], 1 - c]
                remote(other, other, 6 * t + 3 + k, sibling).wait_recv()
        for t in range(n_big, n_in):
            for k in range(3):
                dst = outs[t].at[:, j_of[k]]
                remote(dst, dst, 6 * n_big + 3 * (t - n_big) + k, (*chips[k], c)).wait_recv()
        for cp in first + passed:
            cp.wait_send()
        for loc in started:
            loc.wait()

    out_shape = [_sds(b.shape, b.dtype) for b in bufs]
    out_shape += [_sds((w.shape[0], N_CHIPS) + w.shape[1:], w.dtype) for w in small_ws]
    n_sem = 6 * n_big + 3 * n_small
    outs = _pcall(
        body, in_specs=[ANY] * n_in, out_specs=[ANY] * n_in, out_shape=out_shape,
        input_output_aliases={t: t for t in range(n_big)},
        scratch_shapes=[pltpu.SemaphoreType.DMA((n_sem,)), pltpu.SemaphoreType.DMA((n_sem,)), pltpu.SemaphoreType.DMA((n_small,))],
        name="all_gather_weights",
    )(*bufs, *small_ws)
    return outs[:n_big], outs[n_big:]


class _Exchange:
    def __init__(self, operands, out_shapes, n_sems, copies, in_place=False):
        self.operands, self.out_shapes, self.n_sems, self.copies = list(operands), list(out_shapes), n_sems, copies
        self.aliases = {t: t for t in range(len(self.operands))} if in_place else {}

    @property
    def scratch(self):
        return [pltpu.SemaphoreType.DMA((self.n_sems,)), pltpu.SemaphoreType.DMA((self.n_sems,))]

    def split(self, refs):
        n_in, n_out = len(self.operands), len(self.out_shapes)
        return refs[:n_in], refs[n_in:n_in + n_out]

    def start(self, ins, outs, sems):
        for cp in self.copies(ins, outs, *sems):
            cp.start()

    def wait(self, ins, outs, sems):
        for cp in self.copies(ins, outs, *sems):
            cp.wait()


def _run_exchange(name, ex):
    n_in, n_out = len(ex.operands), len(ex.out_shapes)

    def body(*refs):
        ins, outs, sems = refs[:n_in], refs[n_in:n_in + n_out], refs[n_in + n_out:]
        ex.start(ins, outs, sems)
        ex.wait(ins, outs, sems)

    return _pcall(body, in_specs=[ANY] * n_in, out_specs=[ANY] * n_out, out_shape=ex.out_shapes, scratch_shapes=ex.scratch,
                  input_output_aliases=ex.aliases, name=name)(*ex.operands)


def _gather_chips_exchange(bufs, spans):
    def copies(ins, outs, send_sems, recv_sems):
        x, y, c, chips = _place()
        cps = []
        for t, (l0, n) in enumerate(spans):
            mine = outs[t].at[pl.ds(l0, n), 2 * x + y, c]
            cps += [pltpu.make_async_remote_copy(src_ref=mine, dst_ref=mine, send_sem=send_sems.at[3 * t + k],
                                                 recv_sem=recv_sems.at[3 * t + k], device_id=(cx, cy, c), device_id_type=MESH)
                    for k, (cx, cy) in enumerate(chips)]
        return cps

    return _Exchange(bufs, [_sds(b.shape, b.dtype) for b in bufs], 3 * len(bufs), copies, in_place=True)


def _gather_cores_exchange(bufs, spans):
    def copies(ins, outs, send_sems, recv_sems):
        x, y, c, chips = _place()
        cps = []
        for t, (l0, n) in enumerate(spans):
            for k, (cx, cy) in enumerate(chips):
                part = outs[t].at[pl.ds(l0, n), 2 * cx + cy, c]
                cps.append(pltpu.make_async_remote_copy(src_ref=part, dst_ref=part, send_sem=send_sems.at[3 * t + k],
                                                        recv_sem=recv_sems.at[3 * t + k], device_id=(x, y, 1 - c),
                                                        device_id_type=MESH))
        return cps

    return _Exchange(bufs, [_sds(b.shape, b.dtype) for b in bufs], 3 * len(bufs), copies, in_place=True)


def _core_halves_exchange(grads, spans):
    def copies(ins, outs, send_sems, recv_sems):
        x, y, c, _ = _place()
        return [pltpu.make_async_remote_copy(src_ref=ins[t].at[pl.ds(l0, n), :, 1 - c], dst_ref=outs[t],
                                             send_sem=send_sems.at[t], recv_sem=recv_sems.at[t], device_id=(x, y, 1 - c),
                                             device_id_type=MESH) for t, (l0, n) in enumerate(spans)]

    shapes = [_sds((n, g.shape[1], g.shape[3], g.shape[4]), F32) for g, (_, n) in zip(grads, spans)]
    return _Exchange(grads, shapes, len(grads), copies)


def _add_core_halves(name, g, a, c_idx, l0):
    _, nj, _, rh, X = g.shape
    L = a.shape[0]
    tr = _tile(rh, 256, BF16_ROWS)

    def body(c_ref, g_ref, a_ref, o_ref, ob_ref):
        s = g_ref[...] + a_ref[...]
        o_ref[...] = s
        ob_ref[...] = s.astype(BF16)

    blk = pl.BlockSpec((None, None, tr, X), lambda l, j, i, c_ref: (l, j, i, 0))
    return _pcall(
        body,
        grid_spec=pltpu.PrefetchScalarGridSpec(
            num_scalar_prefetch=1, grid=(L, nj, rh // tr),
            in_specs=[pl.BlockSpec((None, None, None, tr, X), lambda l, j, i, c_ref: (l + l0, j, c_ref[0], i, 0)), blk],
            out_specs=[blk, blk]),
        out_shape=[_sds((L, nj, rh, X), F32), _sds((L, nj, rh, X), BF16)],
        compiler_params=_cp("parallel", "parallel", "parallel"), name=name,
    )(c_idx, g, a)


def _chip_shards_exchange(parts):
    def copies(ins, outs, send_sems, recv_sems):
        x, y, c, chips = _place()
        return [pltpu.make_async_remote_copy(
            src_ref=ins[t].at[:, 2 * cx + cy], dst_ref=outs[t].at[k], send_sem=send_sems.at[3 * t + k],
            recv_sem=recv_sems.at[3 * t + k], device_id=(cx, cy, c), device_id_type=MESH)
            for t in range(len(parts)) for k, (cx, cy) in enumerate(chips)]

    shapes = [_sds((3, p.shape[0], p.shape[2], p.shape[3]), p.dtype) for p in parts]
    return _Exchange(parts, shapes, 3 * len(parts), copies)


def _add_chip_shards(name, p, b, jc_idx, l0, n_layers, buf):
    n, _, rh, X = p.shape
    tr = _tile(rh, 256, BF16_ROWS)

    def body(jc_ref, p_ref, b_ref, *rest):
        rest[-1][...] = ((p_ref[...] + b_ref[0].astype(F32)) + b_ref[1].astype(F32)) + b_ref[2].astype(F32)

    in_specs = [pl.BlockSpec((None, None, tr, X), lambda l, i, jc: (l, jc[0], i, 0)),
                pl.BlockSpec((3, None, tr, X), lambda l, i, jc: (0, l, i, 0))]
    operands = [jc_idx, p, b]
    if buf is not None:
        in_specs.append(ANY)
        operands.append(buf)
    return _pcall(
        body,
        grid_spec=pltpu.PrefetchScalarGridSpec(
            num_scalar_prefetch=1, grid=(n, rh // tr), in_specs=in_specs,
            out_specs=pl.BlockSpec((None, None, tr, X), lambda l, i, jc: (l + l0, jc[1], i, 0))),
        out_shape=_sds((n_layers, 2, rh, X), F32), input_output_aliases={3: 0} if buf is not None else {},
        compiler_params=_cp("parallel", "parallel"), name=name,
    )(*operands)


def _join_core_halves(bufs):
    n = len(bufs)

    def body(*refs):
        outs = refs[n:2 * n]
        send_sems, recv_sems = refs[2 * n:]
        x, y, c, _ = _place()
        cps = [pltpu.make_async_remote_copy(src_ref=outs[t].at[:, c], dst_ref=outs[t].at[:, c], send_sem=send_sems.at[t],
                                            recv_sem=recv_sems.at[t], device_id=(x, y, 1 - c), device_id_type=MESH)
               for t in range(n)]
        for cp in cps:
            cp.start()
        for t in range(n):
            pltpu.make_async_remote_copy(src_ref=outs[t].at[:, c], dst_ref=outs[t].at[:, 1 - c], send_sem=send_sems.at[t],
                                         recv_sem=recv_sems.at[t], device_id=(x, y, 1 - c), device_id_type=MESH).wait()

    outs = _pcall(
        body, in_specs=[ANY] * n, out_specs=[ANY] * n, out_shape=[_sds(b.shape, F32) for b in bufs],
        input_output_aliases={t: t for t in range(n)},
        scratch_shapes=[pltpu.SemaphoreType.DMA((n,)), pltpu.SemaphoreType.DMA((n,))],
        name="grad_join_core_halves",
    )(*bufs)
    return [o.reshape(o.shape[0], 2 * o.shape[2], o.shape[3]) for o in outs]


def _all_reduce_small(packed):
    R, C = packed.shape

    def body(x_ref, o_ref, slots, send_sems, recv_sems):
        x, y, c, _ = _place()
        me = 4 * x + 2 * y + c
        slots[me] = x_ref[...]
        cps = []
        for d in range(N_DEV):
            to = (d // 4, (d // 2) % 2, d % 2)
            cp = pltpu.make_async_remote_copy(src_ref=x_ref, dst_ref=slots.at[me], send_sem=send_sems.at[d],
                                              recv_sem=recv_sems.at[me], device_id=to, device_id_type=MESH)
            cps.append(cp)

            @pl.when(d != me)
            def _():
                cp.start()

        for d in range(N_DEV):
            @pl.when(d != me)
            def _():
                pltpu.make_async_remote_copy(src_ref=x_ref, dst_ref=slots.at[d], send_sem=send_sems.at[d],
                                             recv_sem=recv_sems.at[d], device_id=(x, y, c), device_id_type=MESH).wait_recv()
                cps[d].wait_send()

        acc = slots[0]
        for d in range(1, N_DEV):
            acc = acc + slots[d]
        o_ref[...] = acc

    vm = pl.BlockSpec(memory_space=pltpu.VMEM)
    return _pcall(
        body, in_specs=[vm], out_specs=vm, out_shape=_sds((R, C), F32),
        scratch_shapes=[pltpu.VMEM((N_DEV, R, C), F32), pltpu.SemaphoreType.DMA((N_DEV,)), pltpu.SemaphoreType.DMA((N_DEV,))],
        compiler_params=pltpu.CompilerParams(vmem_limit_bytes=VMEM_LIMIT_BYTES), name="all_reduce_small",
    )(packed)


PACK = SUBLANES * LANES


def _pack(arrays):
    flat = []
    for a in arrays:
        v = a.reshape(-1)
        flat.append(jnp.pad(v, (0, (-v.shape[0]) % PACK)))
    return jnp.concatenate(flat).reshape(-1, LANES)


def _unpack(packed, shapes):
    flat = packed.reshape(-1)
    out, pos = [], 0
    for s in shapes:
        n = 1
        for d in s:
            n *= d
        out.append(flat[pos:pos + n].reshape(s))
        pos += n + (-n) % PACK
    return out


def kernel(x, mix_norm_g, ffn_norm_g, conv_w_in, conv_a_dw_w, conv_a_dw_b, conv_a_ln_g, conv_a_ln_b, conv_b_dw_w, conv_w_out, attn_w_qkv, attn_q_g, attn_k_g, attn_w_o, ffn_w_up, ffn_dw_w, ffn_dw_b, ffn_w_down, loss_target, m_mix_norm_g, m_ffn_norm_g, m_conv_w_in, m_conv_a_dw_w, m_conv_a_dw_b, m_conv_a_ln_g, m_conv_a_ln_b, m_conv_b_dw_w, m_conv_w_out, m_attn_w_qkv, m_attn_q_g, m_attn_k_g, m_attn_w_o, m_ffn_w_up, m_ffn_dw_w, m_ffn_dw_b, m_ffn_w_down, v_mix_norm_g, v_ffn_norm_g, v_conv_w_in, v_conv_a_dw_w, v_conv_a_dw_b, v_conv_a_ln_g, v_conv_a_ln_b, v_conv_b_dw_w, v_conv_w_out, v_attn_w_qkv, v_attn_q_g, v_attn_k_g, v_attn_w_o, v_ffn_w_up, v_ffn_dw_w, v_ffn_dw_b, v_ffn_w_down):
    depth = mix_norm_g.shape[0]
    n_even, n_odd = conv_w_in.shape[0], attn_w_qkv.shape[0]
    S, D = x.shape[1], x.shape[2]
    dg = D // 2
    x0 = x.reshape(S, D)
    target = loss_target.reshape(S, D)
    j_me = 2 * lax.axis_index("x") + lax.axis_index("y")
    c_me = lax.axis_index("c")
    j_idx = j_me.astype(jnp.int32).reshape(1)
    c_idx = c_me.astype(jnp.int32).reshape(1)

    col_names = ["conv_w_in", "attn_w_qkv", "ffn_w_up"]
    row_names = ["conv_w_out", "attn_w_o", "ffn_w_down"]
    local = dict(conv_w_in=conv_w_in, attn_w_qkv=attn_w_qkv, ffn_w_up=ffn_w_up, conv_w_out=conv_w_out, attn_w_o=attn_w_o,
                 ffn_w_down=ffn_w_down)
    gbuf = {n: _place_shard(f"place_{n}", local[n], j_idx) for n in col_names + row_names}

    def weights_of(layer):
        mixer = ("conv_w_in", "conv_w_out") if layer % 2 == 0 else ("attn_w_qkv", "attn_w_o")
        return {mixer[0]: (layer // 2, 1), mixer[1]: (layer // 2, 1), "ffn_w_up": (layer, 1), "ffn_w_down": (layer, 1)}

    def w_col(n):
        return gbuf[n].reshape(gbuf[n].shape[0], N_CHIPS, -1, gbuf[n].shape[4])

    def w_row(n):
        return gbuf[n].reshape(gbuf[n].shape[0], -1, gbuf[n].shape[4])

    def carry(make_exchange, layer):
        if layer + 1 == depth:
            return None, []
        names = list(weights_of(layer + 1))
        return make_exchange([gbuf[n] for n in names], list(weights_of(layer + 1).values())), names

    first = weights_of(0)
    outs, (a_dw, b_dw, f_dw) = _all_gather_weights([gbuf[n] for n in first], list(first.values()),
                                                   [conv_a_dw_w, conv_b_dw_w, ffn_dw_w])
    gbuf.update(zip(first, outs))
    unshard = lambda a: jnp.moveaxis(a, 1, 2).reshape(a.shape[0], a.shape[2], N_CHIPS * a.shape[3])
    a_dw, b_dw, f_dw = unshard(a_dw), unshard(b_dw), unshard(f_dw)
    qk_gain = [jnp.stack([jnp.tile(attn_q_g[i], LANES // HEAD_DIM), jnp.tile(attn_k_g[i], LANES // HEAD_DIM)])
               for i in range(n_odd)]

    saved = []
    xc = x0
    for layer in range(depth):
        i = layer // 2
        tag = f"l{layer}"
        s = {"x_in": xc}
        h = _rms_fwd(f"rms_mix_fwd_{tag}", xc, mix_norm_g, layer)
        s["h"] = h
        ex, names = carry(_gather_chips_exchange, layer)
        if layer % 2 == 0:
            p = _mm_fwd(f"conv_in_fwd_{tag}", h, w_col("conv_w_in"), i, colshard=True)
            ab = _convmix_fwd(f"convmix_fwd_{tag}", p, a_dw, conv_a_dw_b, conv_a_ln_g, conv_a_ln_b, b_dw, i, ex)
            if ex:
                ab, *new = ab
                gbuf.update(zip(names, new))
            xm = _mm_fwd(f"conv_out_fwd_{tag}", ab, w_row("conv_w_out"), i, colshard=False, res=xc)
            s.update(p=p, ab=ab)
        else:
            qkv = _mm_fwd(f"attn_qkv_fwd_{tag}", h, w_col("attn_w_qkv"), i, colshard=True)
            qs, kn, vb = _qknorm_fwd(f"qknorm_fwd_{tag}", qkv, qk_gain[i])
            o = _attn_fwd(f"attn_fwd_{tag}", qs, kn, vb, ex)
            if ex:
                o, *new = o
                gbuf.update(zip(names, new))
            xm = _mm_fwd(f"attn_out_fwd_{tag}", o, w_row("attn_w_o"), i, colshard=False, res=xc)
            s.update(qkv=qkv, qs=qs, kn=kn, vb=vb, o=o)
        s["x_mid"] = xm
        h2 = _rms_fwd(f"rms_ffn_fwd_{tag}", xm, ffn_norm_g, layer)
        u2 = _mm_fwd(f"ffn_up_fwd_{tag}", h2, w_col("ffn_w_up"), layer, colshard=True, out_split=2)
        ex, names = carry(_gather_cores_exchange, layer)
        f = _ffn_mid_fwd(f"ffn_mid_fwd_{tag}", u2, f_dw, ffn_dw_b, layer, ex)
        if ex:
            f, *new = f
            gbuf.update(zip(names, new))
        xc = _mm_fwd(f"ffn_down_fwd_{tag}", f, w_row("ffn_w_down"), layer, colshard=False, res=xm)
        s.update(h2=h2, u2=u2, f=f)
        saved.append(s)

    dx, loss_tile = _loss_fwd_bwd("loss", xc, target)

    w_in, w_qkv, w_up = w_col("conv_w_in"), w_col("attn_w_qkv"), w_col("ffn_w_up")
    w_out, w_o, w_down = w_row("conv_w_out"), w_row("attn_w_o"), w_row("ffn_w_down")
    g_up = g_down = g_in = g_out = g_qkv = g_o = None
    big_names = col_names + row_names

    def halves_view(n, g):
        if n in col_names:
            return g.reshape(g.shape[0], N_CHIPS, 2, g.shape[2] // 2, g.shape[3])
        return g.reshape(g.shape[0], N_CHIPS, 2, g.shape[1] // (2 * N_CHIPS), g.shape[2])

    early = {"conv_w_in": (1, n_even - 1), "attn_w_qkv": (0, n_odd), "ffn_w_up": (1, depth - 1),
             "conv_w_out": (1, n_even - 1), "attn_w_o": (0, n_odd), "ffn_w_down": (1, depth - 1)}
    late = {n: (0, 1) for n in ("conv_w_in", "ffn_w_up", "conv_w_out", "ffn_w_down")}
    early_sums = early_from_chips = None
    d_mix_g, d_ffn_g = [None] * depth, [None] * depth
    d_ffn_dw_w, d_ffn_dw_b = [None] * depth, [None] * depth
    d_a_dw_w, d_a_dw_b, d_a_ln_g, d_a_ln_b, d_b_dw_w = ([None] * n_even for _ in range(5))
    d_q_g, d_k_g = [None] * n_odd, [None] * n_odd
    for layer in reversed(range(depth)):
        i = layer // 2
        tag = f"l{layer}"
        s = saved[layer]
        df = _mm_dgrad(f"ffn_down_dgrad_{tag}", dx, w_down, layer, colshard=False)
        g_down = _mm_wgrad(f"ffn_down_wgrad_{tag}", s["f"], dx, layer, depth, g_down, colshard=False)
        big = {"conv_w_in": g_in, "attn_w_qkv": g_qkv, "ffn_w_up": g_up, "conv_w_out": g_out, "attn_w_o": g_o,
               "ffn_w_down": g_down}
        core_ex = None
        if layer == 0:
            core_ex = _core_halves_exchange([halves_view(n, big[n]) for n in early], list(early.values()))
        du2, dww, dwb, *early_from_sibling = _ffn_mid_bwd(f"ffn_mid_bwd_{tag}", s["u2"], df, f_dw, ffn_dw_b, layer, core_ex)
        d_ffn_dw_w[layer] = jnp.moveaxis(dww, 0, 1).reshape(FFN_CONV_WIDTH, -1)
        d_ffn_dw_b[layer] = dwb.reshape(-1)
        dh2 = _mm_dgrad(f"ffn_up_dgrad_{tag}", du2, w_up, layer, colshard=True)
        g_up = _mm_wgrad(f"ffn_up_wgrad_{tag}", s["h2"], du2, layer, depth, g_up, colshard=True)
        dx, dg_ = _rms_bwd(f"rms_ffn_bwd_{tag}", s["x_mid"], ffn_norm_g, layer, dh2, dx)
        d_ffn_g[layer] = dg_.reshape(-1)
        if layer % 2 == 0:
            dab = _mm_dgrad(f"conv_out_dgrad_{tag}", dx, w_out, i, colshard=False)
            g_out = _mm_wgrad(f"conv_out_wgrad_{tag}", s["ab"], dx, i, n_even, g_out, colshard=False)
            chip_ex = None
            if layer == 0:
                big = {"conv_w_in": g_in, "attn_w_qkv": g_qkv, "ffn_w_up": g_up, "conv_w_out": g_out, "attn_w_o": g_o,
                       "ffn_w_down": g_down}
                both = [_add_core_halves(f"grad_add_core_early_{n}", halves_view(n, big[n]), a, c_idx, early[n][0])
                        for n, a in zip(early, early_from_sibling)]
                early_sums = [b[0] for b in both]
                chip_ex = _chip_shards_exchange([b[1] for b in both])
            dp, daw, dab_b, dlg, dlb, dbw, *early_from_chips = _convmix_bwd(
                f"convmix_bwd_{tag}", s["p"], dab, a_dw, conv_a_dw_b, conv_a_ln_g, conv_a_ln_b, b_dw, i, chip_ex)
            d_a_dw_w[i], d_a_dw_b[i], d_a_ln_g[i], d_a_ln_b[i], d_b_dw_w[i] = (
                daw, dab_b.reshape(-1), dlg.reshape(-1), dlb.reshape(-1), dbw)
            dh = _mm_dgrad(f"conv_in_dgrad_{tag}", dp, w_in, i, colshard=True)
            g_in = _mm_wgrad(f"conv_in_wgrad_{tag}", s["h"], dp, i, n_even, g_in, colshard=True)
        else:
            do = _mm_dgrad(f"attn_out_dgrad_{tag}", dx, w_o, i, colshard=False)
            g_o = _mm_wgrad(f"attn_out_wgrad_{tag}", s["o"], dx, i, n_odd, g_o, colshard=False)
            dq, dk, dv = _attn_bwd(f"attn_bwd_{tag}", s["qs"], s["kn"], s["vb"], s["o"], do)
            dqkv, dgain = _qknorm_bwd(f"qknorm_bwd_{tag}", s["qkv"], dq, dk, dv, qk_gain[i])
            d_q_g[i] = dgain[0, :HEAD_DIM] + dgain[0, HEAD_DIM:]
            d_k_g[i] = dgain[1, :HEAD_DIM] + dgain[1, HEAD_DIM:]
            dh = _mm_dgrad(f"attn_qkv_dgrad_{tag}", dqkv, w_qkv, i, colshard=True)
            g_qkv = _mm_wgrad(f"attn_qkv_wgrad_{tag}", s["h"], dqkv, i, n_odd, g_qkv, colshard=True)
        dx, dg_ = _rms_bwd(f"rms_mix_bwd_{tag}", s["x_in"], mix_norm_g, layer, dh, dx)
        d_mix_g[layer] = dg_.reshape(-1)
    grad_x = dx.reshape(1, S, D)

    small = {
        "mix_norm_g": jnp.stack(d_mix_g), "ffn_norm_g": jnp.stack(d_ffn_g),
        "conv_a_dw_w": jnp.stack(d_a_dw_w), "conv_a_dw_b": jnp.stack(d_a_dw_b),
        "conv_a_ln_g": jnp.stack(d_a_ln_g), "conv_a_ln_b": jnp.stack(d_a_ln_b),
        "conv_b_dw_w": jnp.stack(d_b_dw_w), "attn_q_g": jnp.stack(d_q_g), "attn_k_g": jnp.stack(d_k_g),
        "ffn_dw_w": jnp.stack(d_ffn_dw_w), "ffn_dw_b": jnp.stack(d_ffn_dw_b),
    }
    small_names = list(small)
    summed = _all_reduce_small(_pack([loss_tile] + [small[n] for n in small_names]))
    parts = _unpack(summed, [loss_tile.shape] + [small[n].shape for n in small_names])
    loss = parts[0][0, 0]
    small_g = dict(zip(small_names, parts[1:]))
    for n in ("conv_a_dw_w", "conv_b_dw_w", "ffn_dw_w"):
        cs = small_g[n].shape[2] // N_CHIPS
        small_g[n] = lax.dynamic_slice_in_dim(small_g[n], j_me * cs, cs, axis=2)

    big = {"conv_w_in": g_in, "attn_w_qkv": g_qkv, "ffn_w_up": g_up, "conv_w_out": g_out, "attn_w_o": g_o, "ffn_w_down": g_down}
    views = {n: halves_view(n, big[n]) for n in big_names}
    late_from_sibling = _run_exchange("grad_exchange_core_halves",
                                      _core_halves_exchange([views[n] for n in late], list(late.values())))
    both = [_add_core_halves(f"grad_add_core_{n}", views[n], a, c_idx, late[n][0]) for n, a in zip(late, late_from_sibling)]
    late_sums = [b[0] for b in both]
    late_from_chips = _run_exchange("grad_exchange_chip_shards", _chip_shards_exchange([b[1] for b in both]))
    jc_idx = jnp.concatenate([j_idx, c_idx])
    totals = {}
    for n, p, b in zip(early, early_sums, early_from_chips):
        totals[n] = _add_chip_shards(f"grad_add_chips_early_{n}", p, b, jc_idx, early[n][0], big[n].shape[0], None)
    for n, p, b in zip(late, late_sums, late_from_chips):
        totals[n] = _add_chip_shards(f"grad_add_chips_{n}", p, b, jc_idx, late[n][0], big[n].shape[0], totals[n])
    big_g = dict(zip(big_names, _join_core_halves([totals[n] for n in big_names])))

    weights = dict(mix_norm_g=mix_norm_g, ffn_norm_g=ffn_norm_g, conv_w_in=conv_w_in, conv_a_dw_w=conv_a_dw_w, conv_a_dw_b=conv_a_dw_b, conv_a_ln_g=conv_a_ln_g, conv_a_ln_b=conv_a_ln_b, conv_b_dw_w=conv_b_dw_w, conv_w_out=conv_w_out, attn_w_qkv=attn_w_qkv, attn_q_g=attn_q_g, attn_k_g=attn_k_g, attn_w_o=attn_w_o, ffn_w_up=ffn_w_up, ffn_dw_w=ffn_dw_w, ffn_dw_b=ffn_dw_b, ffn_w_down=ffn_w_down)
    m_in = dict(mix_norm_g=m_mix_norm_g, ffn_norm_g=m_ffn_norm_g, conv_w_in=m_conv_w_in, conv_a_dw_w=m_conv_a_dw_w, conv_a_dw_b=m_conv_a_dw_b, conv_a_ln_g=m_conv_a_ln_g, conv_a_ln_b=m_conv_a_ln_b, conv_b_dw_w=m_conv_b_dw_w, conv_w_out=m_conv_w_out, attn_w_qkv=m_attn_w_qkv, attn_q_g=m_attn_q_g, attn_k_g=m_attn_k_g, attn_w_o=m_attn_w_o, ffn_w_up=m_ffn_w_up, ffn_dw_w=m_ffn_dw_w, ffn_dw_b=m_ffn_dw_b, ffn_w_down=m_ffn_w_down)
    v_in = dict(mix_norm_g=v_mix_norm_g, ffn_norm_g=v_ffn_norm_g, conv_w_in=v_conv_w_in, conv_a_dw_w=v_conv_a_dw_w, conv_a_dw_b=v_conv_a_dw_b, conv_a_ln_g=v_conv_a_ln_g, conv_a_ln_b=v_conv_a_ln_b, conv_b_dw_w=v_conv_b_dw_w, conv_w_out=v_conv_w_out, attn_w_qkv=v_attn_w_qkv, attn_q_g=v_attn_q_g, attn_k_g=v_attn_k_g, attn_w_o=v_attn_w_o, ffn_w_up=v_ffn_w_up, ffn_dw_w=v_ffn_dw_w, ffn_dw_b=v_ffn_dw_b, ffn_w_down=v_ffn_w_down)
    order = list(weights)
    grads, delta, new_m, new_v = {}, {}, {}, {}
    for n in big_names:
        grads[n] = big_g[n]
        delta[n], new_m[n], new_v[n] = _adamw(f"adamw_{n}", weights[n], big_g[n], m_in[n], v_in[n])
    shapes = [weights[n].shape for n in small_names]
    packed = [_pack([d[n] for n in small_names]) for d in (weights, small_g, m_in, v_in)]
    upd = _adamw("adamw_small", *[p[None] for p in packed])
    for out, res in zip((delta, new_m, new_v), upd):
        out.update(zip(small_names, _unpack(res[0], shapes)))
    grads.update({n: small_g[n].reshape(weights[n].shape) for n in small_names})
    return (loss, grad_x, *[grads[n] for n in order], *[delta[n] for n in order], *[new_m[n] for n in order],
            *[new_v[n] for n in order])
```

```python
import jax
import jax.numpy as jnp
from jax import lax
from jax.experimental import pallas as pl
from jax.experimental.pallas import tpu as pltpu

F32 = jnp.float32
BF16 = jnp.bfloat16
EPS = 1e-6
CONV_A_WIDTH = 31
CONV_B_WIDTH = 3
FFN_CONV_WIDTH = 3
HEAD_DIM = 64
ADAM_LR = 0.001
ADAM_B1 = 0.9
ADAM_B2 = 0.999
ADAM_EPS = 1e-08
ADAM_WD = 0.01
ADAM_STEP = 10

LANES = 128
SUBLANES = 8
BF16_ROWS = 16
V7X_VMEM_BYTES = 64 * 1024 * 1024
VMEM_LIMIT_BYTES = V7X_VMEM_BYTES * 3 // 4
MM_VMEM_BUDGET = VMEM_LIMIT_BYTES * 4 // 5
MM_ROWS = 1024
N_CHIPS = 4
N_DEV = 8
HALO_A = 32
HALO_S = 8
ELT_ROWS = 64
ATTN_BLOCK = 128
ATTN_SUB = 2
ATTN_MORE = 2
EXP_UNDERFLOW = -104.0
MESH = pl.DeviceIdType.MESH
ANY = pl.BlockSpec(memory_space=pl.ANY)
NT = (((1,), (1,)), ((), ()))
NN = (((1,), (0,)), ((), ()))
TN = (((0,), (0,)), ((), ()))


def _pcall(body, **kw):
    return pl.pallas_call(body, **kw)


def _cp(*sem):
    return pltpu.CompilerParams(dimension_semantics=sem, vmem_limit_bytes=VMEM_LIMIT_BYTES)


def _sds(shape, dtype):
    return jax.ShapeDtypeStruct(tuple(shape), dtype)


def _tile(n, cap, align=LANES):
    if n <= cap:
        return n
    for t in range(cap - cap % align, 0, -align):
        if n % t == 0:
            return t
    return n


def _sig(x):
    return 0.5 * jnp.tanh(0.5 * x) + 0.5


def _rowsum(x):
    return jnp.sum(x, axis=0, keepdims=True)


def _fold(x):
    acc = x[0:SUBLANES]
    for r in range(SUBLANES, x.shape[0], SUBLANES):
        acc = acc + x[r:r + SUBLANES]
    return acc


def _with_exchange(ex, body, in_specs, out_specs, out_shape, scratch, operands, first, last):
    if ex is None:
        return body, in_specs, out_specs, out_shape, scratch, operands, {}
    n_in, n_out, n_scr = len(in_specs), len(out_specs), len(scratch)
    e_in, e_out = len(ex.operands), len(ex.out_shapes)

    def hosted(*refs):
        refs = list(refs)
        ins, refs = refs[:n_in], refs[n_in:]
        e_ins, refs = refs[:e_in], refs[e_in:]
        outs, refs = refs[:n_out], refs[n_out:]
        e_outs, refs = refs[:e_out], refs[e_out:]
        scr, sems = refs[:n_scr], refs[n_scr:]

        @pl.when(first())
        def _():
            ex.start(e_ins, e_outs, sems)

        body(*ins, *outs, *scr)

        @pl.when(last())
        def _():
            ex.wait(e_ins, e_outs, sems)

    return (hosted, in_specs + [ANY] * e_in, out_specs + [ANY] * e_out, out_shape + ex.out_shapes, scratch + ex.scratch,
            operands + ex.operands, {n_in + i: n_out + o for i, o in ex.aliases.items()})


def _mm_call(name, dn, operands, in_specs, out_shape, out_spec, grid, nk, acc_shape, has_res, has_alias):
    def body(*refs):
        a_ref, b_ref = refs[0], refs[1]
        pos = 2
        res_ref = refs[pos] if has_res else None
        pos += int(has_res) + int(has_alias)
        o_ref = refs[pos]
        acc_ref = refs[pos + 1] if nk > 1 else None
        p = lax.dot_general(a_ref[...].astype(BF16), b_ref[...].astype(BF16), dn, preferred_element_type=F32)

        def finish(v):
            if has_res:
                v = v + res_ref[...]
            o_ref[...] = v.astype(o_ref.dtype)

        if nk == 1:
            finish(p)
        else:
            k = pl.program_id(2)

            @pl.when(k == 0)
            def _():
                acc_ref[...] = p

            @pl.when(k > 0)
            def _():
                acc_ref[...] += p

            @pl.when(k == nk - 1)
            def _():
                finish(acc_ref[...])

    aliases = {len(operands) - 1: 0} if has_alias else {}
    return _pcall(
        body, grid=grid, in_specs=in_specs, out_specs=out_spec, out_shape=out_shape,
        scratch_shapes=[pltpu.VMEM(acc_shape, F32)] if nk > 1 else [],
        input_output_aliases=aliases, compiler_params=_cp("parallel", "parallel", "arbitrary"), name=name,
    )(*operands)


def _mm_fwd(name, a, w, l, *, colshard, res=None, out_split=1):
    M, K = a.shape
    tm = _tile(M, MM_ROWS, BF16_ROWS)
    if colshard:
        cs = w.shape[3]
        N, tn, tk = N_CHIPS * cs, cs, K
        b_spec = pl.BlockSpec((None, None, tk, tn), lambda j, i, k: (l, j, k, 0))
    else:
        N = w.shape[2]
        tn, tk = _tile(N, 1024), _tile(K, 1536)
        b_spec = pl.BlockSpec((None, tk, tn), lambda j, i, k: (l, k, j))
    nk = K // tk
    in_specs = [pl.BlockSpec((tm, tk), lambda j, i, k: (i, k)), b_spec]
    operands = [a, w]
    if res is not None:
        in_specs.append(pl.BlockSpec((tm, tn), lambda j, i, k: (i, j)))
        operands.append(res)
    if out_split == 1:
        out_shape = _sds((M, N), F32)
        out_spec = pl.BlockSpec((tm, tn), lambda j, i, k: (i, j))
    else:
        per = N // tn // out_split
        out_shape = _sds((out_split, M, N // out_split), F32)
        out_spec = pl.BlockSpec((None, tm, tn), lambda j, i, k: (j // per, i, j % per))
    return _mm_call(name, NN, operands, in_specs, out_shape, out_spec, (N // tn, M // tm, nk), nk, (tm, tn),
                    res is not None, False)


def _mm_dgrad(name, g, w, l, *, colshard):
    split = g.ndim == 3
    M = g.shape[-2]
    tm = _tile(M, MM_ROWS, BF16_ROWS)
    if colshard:
        kw, cs = w.shape[2], w.shape[3]
        tn, tk, nk = _tile(kw, 1408), cs, N_CHIPS
        b_spec = pl.BlockSpec((None, None, tn, tk), lambda j, i, k: (l, k, j, 0))
    else:
        kw, ncon = w.shape[1], w.shape[2]
        tn, tk = _tile(kw, 1408), _tile(ncon, 1536)
        nk = ncon // tk
        b_spec = pl.BlockSpec((None, tn, tk), lambda j, i, k: (l, j, k))
    if split:
        per = nk // g.shape[0]
        a_spec = pl.BlockSpec((None, tm, tk), lambda j, i, k: (k // per, i, k % per))
    else:
        a_spec = pl.BlockSpec((tm, tk), lambda j, i, k: (i, k))
    out_shape = _sds((M, kw), F32)
    out_spec = pl.BlockSpec((tm, tn), lambda j, i, k: (i, j))
    return _mm_call(name, NT, [g, w], [a_spec, b_spec], out_shape, out_spec, (kw // tn, M // tm, nk), nk, (tm, tn),
                    False, False)


def _mm_wgrad(name, a, g, l, n_layers, buf, *, colshard):
    S, M = a.shape
    split = g.ndim == 3
    N = g.shape[-1] * (g.shape[0] if split else 1)
    tm = _tile(M, 1408)
    tn = N // N_CHIPS if colshard else _tile(N, 1024)
    per_row = 2 * (tm * a.dtype.itemsize + tn * g.dtype.itemsize)
    tk = _tile(S, max(BF16_ROWS, min(2048, (MM_VMEM_BUDGET - 3 * tm * tn * 4) // per_row)), BF16_ROWS)
    nk = S // tk
    if colshard:
        out_shape = _sds((n_layers, N_CHIPS, M, tn), F32)
        out_spec = pl.BlockSpec((None, None, tm, tn), lambda j, i, k: (l, j, i, 0))
    else:
        out_shape = _sds((n_layers, M, N), F32)
        out_spec = pl.BlockSpec((None, tm, tn), lambda j, i, k: (l, i, j))
    if split:
        per = N // tn // g.shape[0]
        b_spec = pl.BlockSpec((None, tk, tn), lambda j, i, k: (j // per, k, j % per))
    else:
        b_spec = pl.BlockSpec((tk, tn), lambda j, i, k: (k, j))
    in_specs = [pl.BlockSpec((tk, tm), lambda j, i, k: (k, i)), b_spec]
    operands = [a, g]
    if buf is not None:
        in_specs.append(ANY)
        operands.append(buf)
    return _mm_call(name, TN, operands, in_specs, out_shape, out_spec, (N // tn, M // tm, nk), nk, (tm, tn),
                    False, buf is not None)


def _rms_fwd(name, x, g, l):
    S, D = x.shape
    tm = _tile(S, 512, BF16_ROWS)

    def body(x_ref, g_ref, o_ref):
        xf = x_ref[...]
        r = lax.rsqrt(jnp.mean(xf * xf, axis=-1, keepdims=True) + EPS)
        o_ref[...] = (xf * r * g_ref[l:l + 1, :]).astype(BF16)

    return _pcall(
        body, grid=(S // tm,),
        in_specs=[pl.BlockSpec((tm, D), lambda i: (i, 0)), pl.BlockSpec(g.shape, lambda i: (0, 0))],
        out_specs=pl.BlockSpec((tm, D), lambda i: (i, 0)), out_shape=_sds((S, D), BF16),
        compiler_params=_cp("parallel"), name=name,
    )(x, g)


def _rms_bwd(name, x, g, l, dh, dres, exchange=None):
    S, D = x.shape
    tm = _tile(S, 512, SUBLANES)

    def body(x_ref, g_ref, dh_ref, dr_ref, dx_ref, dg_ref):
        xf = x_ref[...]
        r = lax.rsqrt(jnp.mean(xf * xf, axis=-1, keepdims=True) + EPS)
        xh = xf * r
        d = dh_ref[...]
        dxh = d * g_ref[l:l + 1, :]
        dx_ref[...] = dr_ref[...] + r * (dxh - xh * jnp.mean(dxh * xh, axis=-1, keepdims=True))

        @pl.when(pl.program_id(0) == 0)
        def _():
            dg_ref[...] = jnp.zeros_like(dg_ref)

        dg_ref[...] += _rowsum(d * xh)

    row = pl.BlockSpec((tm, D), lambda i: (i, 0))
    n_i = S // tm
    body, in_specs, out_specs, out_shape, scratch, operands, aliases = _with_exchange(
        exchange, body, [row, pl.BlockSpec(g.shape, lambda i: (0, 0)), row, row],
        [row, pl.BlockSpec((1, D), lambda i: (0, 0))], [_sds((S, D), F32), _sds((1, D), F32)], [], [x, g, dh, dres],
        lambda: pl.program_id(0) == 0, lambda: pl.program_id(0) == n_i - 1)
    return _pcall(
        body, grid=(n_i,), in_specs=in_specs, out_specs=out_specs, out_shape=out_shape, scratch_shapes=scratch,
        input_output_aliases=aliases, compiler_params=_cp("arbitrary"), name=name,
    )(*operands)


def _loss_fwd_bwd(name, y, t):
    S, D = y.shape
    tm = _tile(S, 512, SUBLANES)

    def body(y_ref, t_ref, dy_ref, l_ref):
        e = y_ref[...] - t_ref[...]
        dy_ref[...] = e * (1.0 / D)

        @pl.when(pl.program_id(0) == 0)
        def _():
            l_ref[...] = jnp.zeros_like(l_ref)

        l_ref[...] += 0.5 * jnp.sum(jnp.sum(e * e, axis=-1, keepdims=True) * (1.0 / D), axis=0, keepdims=True)

    row = pl.BlockSpec((tm, D), lambda i: (i, 0))
    return _pcall(
        body, grid=(S // tm,), in_specs=[row, row],
        out_specs=[row, pl.BlockSpec((SUBLANES, LANES), lambda i: (0, 0))],
        out_shape=[_sds((S, D), F32), _sds((SUBLANES, LANES), F32)],
        compiler_params=_cp("arbitrary"), name=name,
    )(y, t)


def _delayed_copies(us, n_rows):
    for s in range(1, SUBLANES):
        us[s, pl.ds(SUBLANES, n_rows - SUBLANES), :] = us[0, pl.ds(SUBLANES - s, n_rows - SUBLANES), :]


def _conv_a(aw_ref, ab_ref, l, us, row0, rows, dg):
    ka = CONV_A_WIDTH
    out = []
    for c0 in range(0, dg, LANES):
        lanes = slice(c0, c0 + LANES)
        acc = ab_ref[l:l + 1, lanes]
        for d in range(ka):
            a, s = divmod(d, SUBLANES)
            acc = acc + aw_ref[l, ka - 1 - d:ka - d, lanes] * us[s, pl.ds(row0 - SUBLANES * a, rows), lanes]
        out.append(acc)
    return jnp.concatenate(out, axis=1)


def _convmix_fwd(name, p, aw, ab, lg, lb, bw, l, exchange=None):
    S, W = p.shape
    dg = W // 5
    tm = _tile(S, 256, HALO_A)
    nb = tm // HALO_A
    ka, kb = CONV_A_WIDTH, CONV_B_WIDTH

    ext = HALO_A + tm
    rc = _tile(tm, ELT_ROWS, BF16_ROWS)

    def body(p_ref, ph_ref, aw_ref, ab_ref, lg_ref, lb_ref, bw_ref, o_ref, us, mext):
        first = pl.program_id(0) == 0
        ph = ph_ref[...]
        pc = p_ref[...]
        us[0, pl.ds(0, HALO_A), :] = jnp.where(first, 0.0, ph[:, 0:dg] * _sig(ph[:, dg:2 * dg]))
        us[0, pl.ds(HALO_A, tm), :] = pc[:, 0:dg] * _sig(pc[:, dg:2 * dg])
        mext[pl.ds(0, HALO_A), :] = jnp.where(first, 0.0, ph[:, 3 * dg:4 * dg] * ph[:, 4 * dg:5 * dg])
        mext[pl.ds(HALO_A, tm), :] = pc[:, 3 * dg:4 * dg] * pc[:, 4 * dg:5 * dg]
        _delayed_copies(us, ext)
        for r0 in range(0, tm, rc):
            rows = pl.ds(r0, rc)
            c = _conv_a(aw_ref, ab_ref, l, us, HALO_A + r0, rc, dg)
            xc = c - jnp.mean(c, axis=-1, keepdims=True)
            ln = xc * lax.rsqrt(jnp.mean(xc * xc, axis=-1, keepdims=True) + EPS) * lg_ref[l:l + 1, :] + lb_ref[l:l + 1, :]
            o_ref[rows, 0:dg] = (ln * _sig(ln)).astype(BF16)
            cb = bw_ref[l, 0:1, :] * mext[pl.ds(HALO_A - (kb - 1) + r0, rc), :]
            for k in range(1, kb):
                cb = cb + bw_ref[l, k:k + 1, :] * mext[pl.ds(HALO_A - (kb - 1) + k + r0, rc), :]
            o_ref[rows, dg:2 * dg] = (p_ref[rows, 2 * dg:3 * dg] * cb).astype(BF16)

    full = lambda a: pl.BlockSpec(a.shape, lambda i: (0,) * a.ndim)
    n_i = S // tm
    body, in_specs, out_specs, out_shape, scratch, operands, aliases = _with_exchange(
        exchange, body,
        [pl.BlockSpec((tm, W), lambda i: (i, 0)), pl.BlockSpec((HALO_A, W), lambda i: (jnp.maximum(i * nb - 1, 0), 0)),
         full(aw), full(ab), full(lg), full(lb), full(bw)],
        [pl.BlockSpec((tm, 2 * dg), lambda i: (i, 0))], [_sds((S, 2 * dg), BF16)],
        [pltpu.VMEM((SUBLANES, ext, dg), F32), pltpu.VMEM((ext, dg), F32)], [p, p, aw, ab, lg, lb, bw],
        lambda: pl.program_id(0) == 0, lambda: pl.program_id(0) == n_i - 1)
    outs = _pcall(
        body, grid=(n_i,), in_specs=in_specs, out_specs=out_specs, out_shape=out_shape, scratch_shapes=scratch,
        input_output_aliases=aliases, compiler_params=_cp("arbitrary" if exchange else "parallel"), name=name,
    )(*operands)
    return outs if exchange else outs[0]


def _convmix_bwd(name, p, dab, aw, ab, lg, lb, bw, l, exchange=None):
    S, W = p.shape
    dg = W // 5
    tm = _tile(S, 256, HALO_A)
    nb = tm // HALO_A
    n_i = S // tm
    ka, kb = CONV_A_WIDTH, CONV_B_WIDTH
    n = tm + HALO_A
    ext = HALO_A + n
    rc = _tile(tm, ELT_ROWS, BF16_ROWS)

    def body(p_ref, pp_ref, pn_ref, d_ref, dn_ref, aw_ref, ab_ref, lg_ref, lb_ref, bw_ref,
             dp_ref, daw_ref, dab_ref, dlg_ref, dlb_ref, dbw_ref, us, mext, dcs, dbext, accw):
        i = pl.program_id(0)
        first, last = i == 0, i == n_i - 1

        @pl.when(first)
        def _():
            for r in (daw_ref, dab_ref, dlg_ref, dlb_ref, dbw_ref):
                r[...] = jnp.zeros_like(r)

        accw[...] = jnp.zeros_like(accw)
        pp, pc, pn = pp_ref[...], p_ref[...], pn_ref[...]
        glu = lambda b: b[:, 0:dg] * _sig(b[:, dg:2 * dg])
        gch = lambda b: b[:, 3 * dg:4 * dg] * b[:, 4 * dg:5 * dg]
        us[0, pl.ds(0, HALO_A), :] = jnp.where(first, 0.0, glu(pp))
        us[0, pl.ds(HALO_A, tm), :] = glu(pc)
        us[0, pl.ds(HALO_A + tm, HALO_A), :] = glu(pn)
        mext[pl.ds(0, HALO_A), :] = jnp.where(first, 0.0, gch(pp))
        mext[pl.ds(HALO_A, tm), :] = gch(pc)
        mext[pl.ds(HALO_A + tm, HALO_A), :] = gch(pn)
        _delayed_copies(us, ext)
        chunks = [(r, rc) for r in range(0, tm, rc)] + [(tm, HALO_A)]
        g_ln = lg_ref[l:l + 1, :]
        zero8 = jnp.zeros((SUBLANES, dg), F32)

        acc_lg = acc_lb = acc_ab = zero8
        for r0, rows in chunks:
            c = _conv_a(aw_ref, ab_ref, l, us, HALO_A + r0, rows, dg)
            xc = c - jnp.mean(c, axis=-1, keepdims=True)
            rstd = lax.rsqrt(jnp.mean(xc * xc, axis=-1, keepdims=True) + EPS)
            chat = xc * rstd
            ln = chat * g_ln + lb_ref[l:l + 1, :]
            s = _sig(ln)
            da = d_ref[pl.ds(r0, rows), 0:dg] if r0 < tm else jnp.where(last, 0.0, dn_ref[:, 0:dg])
            dln = da * (s * (1.0 + ln * (1.0 - s)))
            dlnh = dln * g_ln
            dc = rstd * (dlnh - jnp.mean(dlnh, axis=-1, keepdims=True)
                         - chat * jnp.mean(dlnh * chat, axis=-1, keepdims=True))
            dcs[0, pl.ds(r0, rows), :] = dc
            if r0 < tm:
                acc_lg = acc_lg + _fold(dln * chat)
                acc_lb = acc_lb + _fold(dln)
                acc_ab = acc_ab + _fold(dc)
                for c0 in range(0, dg, LANES):
                    lanes = slice(c0, c0 + LANES)
                    for d in range(ka):
                        a, sh = divmod(d, SUBLANES)
                        k = ka - 1 - d
                        accw[pl.ds(SUBLANES * k, SUBLANES), lanes] += _fold(
                            dc[:, lanes] * us[sh, pl.ds(HALO_A + r0 - SUBLANES * a, rows), lanes])
        dlg_ref[...] += _rowsum(acc_lg)
        dlb_ref[...] += _rowsum(acc_lb)
        dab_ref[...] += _rowsum(acc_ab)
        for k in range(ka):
            daw_ref[k:k + 1, :] += _rowsum(accw[pl.ds(SUBLANES * k, SUBLANES), :])
        for s in range(1, SUBLANES):
            dcs[s, pl.ds(0, n - SUBLANES), :] = dcs[0, pl.ds(s, n - SUBLANES), :]
        for r0 in range(0, tm, rc):
            rows = pl.ds(r0, rc)
            parts = []
            for c0 in range(0, dg, LANES):
                lanes = slice(c0, c0 + LANES)
                acc = aw_ref[l, ka - 1:ka, lanes] * dcs[0, rows, lanes]
                for e in range(1, ka):
                    a, sh = divmod(e, SUBLANES)
                    acc = acc + aw_ref[l, ka - 1 - e:ka - e, lanes] * dcs[sh, pl.ds(r0 + SUBLANES * a, rc), lanes]
                parts.append(acc)
            du = jnp.concatenate(parts, axis=1)
            sg = _sig(p_ref[rows, dg:2 * dg])
            dp_ref[rows, 0:dg] = (du * sg).astype(BF16)
            dp_ref[rows, dg:2 * dg] = (du * p_ref[rows, 0:dg] * sg * (1.0 - sg)).astype(BF16)

        for r0, rows in chunks:
            if r0 < tm:
                dbext[pl.ds(r0, rows), :] = d_ref[pl.ds(r0, rows), dg:2 * dg] * p_ref[pl.ds(r0, rows), 2 * dg:3 * dg]
            else:
                dbext[pl.ds(r0, rows), :] = jnp.where(last, 0.0, dn_ref[:, dg:2 * dg] * pn[:, 2 * dg:3 * dg])
        acc_bw = [zero8] * kb
        for r0 in range(0, tm, rc):
            rows = pl.ds(r0, rc)
            m_k = [mext[pl.ds(HALO_A - (kb - 1) + k + r0, rc), :] for k in range(kb)]
            cb = bw_ref[l, 0:1, :] * m_k[0]
            dm = bw_ref[l, 0:1, :] * dbext[pl.ds(r0 + kb - 1, rc), :]
            for k in range(1, kb):
                cb = cb + bw_ref[l, k:k + 1, :] * m_k[k]
                dm = dm + bw_ref[l, k:k + 1, :] * dbext[pl.ds(r0 + kb - 1 - k, rc), :]
            dcb = dbext[rows, :]
            acc_bw = [acc_bw[k] + _fold(dcb * m_k[k]) for k in range(kb)]
            dp_ref[rows, 2 * dg:3 * dg] = (d_ref[rows, dg:2 * dg] * cb).astype(BF16)
            dp_ref[rows, 3 * dg:4 * dg] = (dm * p_ref[rows, 4 * dg:5 * dg]).astype(BF16)
            dp_ref[rows, 4 * dg:5 * dg] = (dm * p_ref[rows, 3 * dg:4 * dg]).astype(BF16)
        for k in range(kb):
            dbw_ref[k:k + 1, :] += _rowsum(acc_bw[k])

    full = lambda a: pl.BlockSpec(a.shape, lambda i: (0,) * a.ndim)
    prev = lambda i: (jnp.maximum(i * nb - 1, 0), 0)
    nxt = lambda i: (jnp.minimum((i + 1) * nb, S // HALO_A - 1), 0)
    acc = lambda r: pl.BlockSpec((r, dg), lambda i: (0, 0))
    body, in_specs, out_specs, out_shape, scratch, operands, aliases = _with_exchange(
        exchange, body,
        [pl.BlockSpec((tm, W), lambda i: (i, 0)), pl.BlockSpec((HALO_A, W), prev), pl.BlockSpec((HALO_A, W), nxt),
         pl.BlockSpec((tm, 2 * dg), lambda i: (i, 0)), pl.BlockSpec((HALO_A, 2 * dg), nxt),
         full(aw), full(ab), full(lg), full(lb), full(bw)],
        [pl.BlockSpec((tm, W), lambda i: (i, 0)), acc(ka), acc(1), acc(1), acc(1), acc(kb)],
        [_sds((S, W), BF16), _sds((ka, dg), F32), _sds((1, dg), F32), _sds((1, dg), F32), _sds((1, dg), F32),
         _sds((kb, dg), F32)],
        [pltpu.VMEM((SUBLANES, ext, dg), F32), pltpu.VMEM((ext, dg), F32), pltpu.VMEM((SUBLANES, n, dg), F32),
         pltpu.VMEM((n, dg), F32), pltpu.VMEM((SUBLANES * ka, dg), F32)],
        [p, p, p, dab, dab, aw, ab, lg, lb, bw],
        lambda: pl.program_id(0) == 0, lambda: pl.program_id(0) == n_i - 1)
    return _pcall(
        body, grid=(n_i,), in_specs=in_specs, out_specs=out_specs, out_shape=out_shape, scratch_shapes=scratch,
        input_output_aliases=aliases, compiler_params=_cp("arbitrary"), name=name,
    )(*operands)


def _ffn_mid_fwd(name, u2, dww, dwb, l, exchange=None):
    _, S, F = u2.shape
    tm = _tile(S, 256, BF16_ROWS)
    tc = _tile(F, 1408)
    n_f = F // tc
    nb = tm // HALO_S
    kf = FFN_CONV_WIDTH

    def body(u_ref, uh_ref, wg_ref, wv_ref, bg_ref, bv_ref, o_ref, ext):
        first = pl.program_id(1) == 0
        ext[:, pl.ds(0, HALO_S), :] = jnp.where(first, 0.0, uh_ref[...])
        ext[:, pl.ds(HALO_S, tm), :] = u_ref[...]
        rc = _tile(tm, ELT_ROWS, BF16_ROWS)

        def lane_chunk(ci, carry):
            lanes = pl.ds(pl.multiple_of(ci * LANES, LANES), LANES)
            taps = [[w_ref[k:k + 1, lanes] for k in range(kf)] for w_ref in (wg_ref, wv_ref)]
            bias = [b_ref[l:l + 1, lanes] for b_ref in (bg_ref, bv_ref)]
            for r0 in range(0, tm, rc):
                c = []
                for g in range(2):
                    acc = bias[g]
                    for k in range(kf):
                        acc = acc + taps[g][k] * ext[g, pl.ds(HALO_S - (kf - 1) + k + r0, rc), lanes]
                    c.append(acc)
                o_ref[pl.ds(r0, rc), lanes] = (c[0] * _sig(c[0]) * c[1]).astype(BF16)
            return carry

        lax.fori_loop(0, tc // LANES, lane_chunk, 0)

    n_l = dwb.shape[0]
    n_i = S // tm
    body, in_specs, out_specs, out_shape, scratch, operands, aliases = _with_exchange(
        exchange, body,
        [pl.BlockSpec((2, tm, tc), lambda j, i: (0, i, j)),
         pl.BlockSpec((2, HALO_S, tc), lambda j, i: (0, jnp.maximum(i * nb - 1, 0), j)),
         pl.BlockSpec((None, kf, tc), lambda j, i: (l, 0, j)),
         pl.BlockSpec((None, kf, tc), lambda j, i: (l, 0, j + n_f)),
         pl.BlockSpec((n_l, tc), lambda j, i: (0, j)),
         pl.BlockSpec((n_l, tc), lambda j, i: (0, j + n_f))],
        [pl.BlockSpec((tm, tc), lambda j, i: (i, j))], [_sds((S, F), BF16)],
        [pltpu.VMEM((2, HALO_S + tm, tc), F32)], [u2, u2, dww, dww, dwb, dwb],
        lambda: jnp.logical_and(pl.program_id(0) == 0, pl.program_id(1) == 0),
        lambda: jnp.logical_and(pl.program_id(0) == n_f - 1, pl.program_id(1) == n_i - 1))
    sem = "arbitrary" if exchange else "parallel"
    outs = _pcall(
        body, grid=(n_f, n_i), in_specs=in_specs, out_specs=out_specs, out_shape=out_shape, scratch_shapes=scratch,
        input_output_aliases=aliases, compiler_params=_cp(sem, sem), name=name,
    )(*operands)
    return outs if exchange else outs[0]


def _ffn_mid_bwd(name, u2, df, dww, dwb, l, exchange=None):
    _, S, F = u2.shape
    tm = _tile(S, 256, BF16_ROWS)
    tc = _tile(F, 1408)
    n_f = F // tc
    nb = tm // HALO_S
    n_i = S // tm
    kf = FFN_CONV_WIDTH
    n = tm + HALO_S

    def body(u_ref, up_ref, un_ref, df_ref, dfn_ref, wg_ref, wv_ref, bg_ref, bv_ref,
             du_ref, dw_ref, db_ref, uext, dcext):
        i = pl.program_id(1)
        first, last = i == 0, i == n_i - 1

        @pl.when(first)
        def _():
            dw_ref[...] = jnp.zeros_like(dw_ref)
            db_ref[...] = jnp.zeros_like(db_ref)

        uext[:, pl.ds(0, HALO_S), :] = jnp.where(first, 0.0, up_ref[...])
        uext[:, pl.ds(HALO_S, tm), :] = u_ref[...]
        uext[:, pl.ds(HALO_S + tm, HALO_S), :] = un_ref[...]
        rc = _tile(tm, ELT_ROWS, BF16_ROWS)

        def lane_chunk(ci, carry):
            lanes = pl.ds(pl.multiple_of(ci * LANES, LANES), LANES)
            taps = [[w_ref[k:k + 1, lanes] for k in range(kf)] for w_ref in (wg_ref, wv_ref)]
            bias = [b_ref[l:l + 1, lanes] for b_ref in (bg_ref, bv_ref)]
            acc_w = [[jnp.zeros((SUBLANES, LANES), F32) for _ in range(kf)] for _ in range(2)]
            acc_b = [jnp.zeros((SUBLANES, LANES), F32) for _ in range(2)]
            for r0, rows in [(r, rc) for r in range(0, tm, rc)] + [(tm, HALO_S)]:
                shifted = [[uext[g, pl.ds(HALO_S - (kf - 1) + k + r0, rows), lanes] for k in range(kf)] for g in range(2)]
                conv = []
                for g in range(2):
                    acc = bias[g]
                    for k in range(kf):
                        acc = acc + taps[g][k] * shifted[g][k]
                    conv.append(acc)
                cg, cv = conv
                s = _sig(cg)
                dfe = df_ref[pl.ds(r0, rows), lanes] if r0 < tm else jnp.where(last, 0.0, dfn_ref[:, lanes])
                dc = [dfe * cv * (s * (1.0 + cg * (1.0 - s))), dfe * (cg * s)]
                for g in range(2):
                    dcext[g, pl.ds(r0, rows), lanes] = dc[g]
                    if r0 < tm:
                        acc_b[g] = acc_b[g] + _fold(dc[g])
                        for k in range(kf):
                            acc_w[g][k] = acc_w[g][k] + _fold(dc[g] * shifted[g][k])
            for r0 in range(0, tm, rc):
                for g in range(2):
                    du = taps[g][0] * dcext[g, pl.ds(r0 + kf - 1, rc), lanes]
                    for k in range(1, kf):
                        du = du + taps[g][k] * dcext[g, pl.ds(r0 + kf - 1 - k, rc), lanes]
                    du_ref[g, pl.ds(r0, rc), lanes] = du.astype(BF16)
            for g in range(2):
                db_ref[g, :, lanes] += _rowsum(acc_b[g])
                for k in range(kf):
                    dw_ref[g, k:k + 1, lanes] += _rowsum(acc_w[g][k])
            return carry

        lax.fori_loop(0, tc // LANES, lane_chunk, 0)

    n_l = dwb.shape[0]
    prev = lambda j, i: (0, jnp.maximum(i * nb - 1, 0), j)
    nxt = lambda j, i: (0, jnp.minimum((i + 1) * nb, S // HALO_S - 1), j)
    body, in_specs, out_specs, out_shape, scratch, operands, aliases = _with_exchange(
        exchange, body,
        [pl.BlockSpec((2, tm, tc), lambda j, i: (0, i, j)),
         pl.BlockSpec((2, HALO_S, tc), prev), pl.BlockSpec((2, HALO_S, tc), nxt),
         pl.BlockSpec((tm, tc), lambda j, i: (i, j)),
         pl.BlockSpec((HALO_S, tc), lambda j, i: nxt(j, i)[1:]),
         pl.BlockSpec((None, kf, tc), lambda j, i: (l, 0, j)),
         pl.BlockSpec((None, kf, tc), lambda j, i: (l, 0, j + n_f)),
         pl.BlockSpec((n_l, tc), lambda j, i: (0, j)),
         pl.BlockSpec((n_l, tc), lambda j, i: (0, j + n_f))],
        [pl.BlockSpec((2, tm, tc), lambda j, i: (0, i, j)),
         pl.BlockSpec((2, kf, tc), lambda j, i: (0, 0, j)),
         pl.BlockSpec((2, 1, tc), lambda j, i: (0, 0, j))],
        [_sds((2, S, F), BF16), _sds((2, kf, F), F32), _sds((2, 1, F), F32)],
        [pltpu.VMEM((2, HALO_S + n, tc), F32), pltpu.VMEM((2, n, tc), F32)],
        [u2, u2, u2, df, df, dww, dww, dwb, dwb],
        lambda: jnp.logical_and(pl.program_id(0) == 0, pl.program_id(1) == 0),
        lambda: jnp.logical_and(pl.program_id(0) == n_f - 1, pl.program_id(1) == n_i - 1))
    return _pcall(
        body, grid=(n_f, n_i), in_specs=in_specs, out_specs=out_specs, out_shape=out_shape, scratch_shapes=scratch,
        input_output_aliases=aliases, compiler_params=_cp("arbitrary" if exchange else "parallel", "arbitrary"), name=name,
    )(*operands)


def _head_sum_matrix():
    r = lax.broadcasted_iota(jnp.int32, (LANES, LANES), 0) // HEAD_DIM
    c = lax.broadcasted_iota(jnp.int32, (LANES, LANES), 1) // HEAD_DIM
    return (r == c).astype(BF16)


def _head_mean(x, ones):
    return _split_dot(x, ones) * (1.0 / HEAD_DIM)


def _qknorm_fwd(name, qkv, g2):
    S, D3 = qkv.shape
    D = D3 // 3
    tm = _tile(S, 256, BF16_ROWS)
    scale = HEAD_DIM ** -0.5

    def body(q_ref, k_ref, v_ref, g_ref, qo_ref, ko_ref, vo_ref):
        ones = _head_sum_matrix()
        for cc in range(D // LANES):
            sl = slice(cc * LANES, (cc + 1) * LANES)
            for x_ref, o_ref, row, mult in ((q_ref, qo_ref, 0, scale), (k_ref, ko_ref, 1, 1.0)):
                x = x_ref[:, sl]
                r = lax.rsqrt(_head_mean(x * x, ones) + EPS)
                o_ref[:, sl] = ((x * r * g_ref[row:row + 1, :]).astype(BF16) * mult).astype(BF16)
        vo_ref[...] = v_ref[...].astype(BF16)

    col = lambda c: pl.BlockSpec((tm, D), lambda i: (i, c))
    out = pl.BlockSpec((tm, D), lambda i: (i, 0))
    return _pcall(
        body, grid=(S // tm,),
        in_specs=[col(0), col(1), col(2), pl.BlockSpec(g2.shape, lambda i: (0, 0))],
        out_specs=[out, out, out], out_shape=[_sds((S, D), BF16)] * 3,
        compiler_params=_cp("parallel"), name=name,
    )(qkv, qkv, qkv, g2)


def _qknorm_bwd(name, qkv, dq, dk, dv, g2):
    S, D3 = qkv.shape
    D = D3 // 3
    tm = _tile(S, 256, BF16_ROWS)
    scale = HEAD_DIM ** -0.5

    def body(q_ref, k_ref, dq_ref, dk_ref, dv_ref, g_ref, o_ref, dg_ref):
        @pl.when(pl.program_id(0) == 0)
        def _():
            dg_ref[...] = jnp.zeros_like(dg_ref)

        ones = _head_sum_matrix()
        for cc in range(D // LANES):
            sl = slice(cc * LANES, (cc + 1) * LANES)
            for x_ref, d_ref, row, mult, base in ((q_ref, dq_ref, 0, scale, 0), (k_ref, dk_ref, 1, 1.0, D)):
                x = x_ref[:, sl]
                r = lax.rsqrt(_head_mean(x * x, ones) + EPS)
                xh = x * r
                dn = d_ref[:, sl] * mult
                dxh = dn * g_ref[row:row + 1, :]
                dx = r * (dxh - xh * _head_mean(dxh * xh, ones))
                o_ref[:, base + cc * LANES:base + (cc + 1) * LANES] = dx.astype(BF16)
                dg_ref[row:row + 1, :] += _rowsum(dn * xh)
        o_ref[:, 2 * D:3 * D] = dv_ref[...].astype(BF16)

    col = lambda c: pl.BlockSpec((tm, D), lambda i: (i, c))
    row = pl.BlockSpec((tm, D), lambda i: (i, 0))
    return _pcall(
        body, grid=(S // tm,),
        in_specs=[col(0), col(1), row, row, row, pl.BlockSpec(g2.shape, lambda i: (0, 0))],
        out_specs=[pl.BlockSpec((tm, D3), lambda i: (i, 0)), pl.BlockSpec((2, LANES), lambda i: (0, 0))],
        out_shape=[_sds((S, D3), BF16), _sds((2, LANES), F32)],
        compiler_params=_cp("arbitrary"), name=name,
    )(qkv, qkv, dq, dk, dv, g2)


def _attn_consts():
    t = ATTN_BLOCK
    row = lax.broadcasted_iota(jnp.int32, (t, t), 0)
    col = lax.broadcasted_iota(jnp.int32, (t, t), 1)
    lane = lax.broadcasted_iota(jnp.int32, (1, LANES), 1)
    heads = (lane < HEAD_DIM, lane >= HEAD_DIM)
    return row, col, heads


def _split_dot(x, m):
    n = x.shape[0]
    hi = x.astype(BF16)
    lo = (x - hi.astype(F32)).astype(BF16)
    both = jnp.dot(jnp.concatenate([hi, lo], axis=0), m, preferred_element_type=F32)
    return both[:n] + both[n:]


def _log_keep(z):
    return -(jnp.maximum(z, 0.0) + jnp.log(1.0 + jnp.exp(-jnp.abs(z))))


def _stack_heads(a, heads):
    t = ATTN_BLOCK
    zero = jnp.zeros((t, LANES), a.dtype)
    return jnp.concatenate([jnp.where(h, a[s * t:(s + 1) * t], zero) for s in range(a.shape[0] // t) for h in heads], axis=0)


def _side_by_side(a):
    t = ATTN_BLOCK
    return jnp.concatenate([jnp.concatenate([a[2 * s * t:(2 * s + 1) * t], a[(2 * s + 1) * t:(2 * s + 2) * t]], axis=1)
                            for s in range(a.shape[0] // (2 * t))], axis=0)


def _grow(a, rows, cols):
    z = jnp.zeros((rows, cols), F32)
    return z if a is None else jnp.concatenate([z, a], axis=0)


def _attn_fwd(name, qs, kn, vb, exchange=None):
    S, D = qs.shape
    t = ATTN_BLOCK
    tq = ATTN_SUB * t

    def body(q_ref, k_ref, v_ref, o_ref):
        i = pl.program_id(1)
        row, col, heads = _attn_consts()
        after_m = (row > col).astype(BF16)
        causal = col < row
        q_all = _stack_heads(q_ref[...], heads)

        def blocks(specs, r, acc):
            n_rows = q_all.shape[0]
            offs = [pl.multiple_of(j * t, t) for j, _, _ in specs]
            zs = [lax.dot_general(q_all[lo:], k_ref[pl.ds(off, t), :], NT, preferred_element_type=F32)
                  for off, (_, lo, _) in zip(offs, specs)]
            lks = []
            for z, (_, _, mask) in zip(zs, specs):
                lk = _log_keep(z)
                lks.append(lk if mask is None else jnp.where(mask, lk, 0.0))
            cums = [_split_dot(lk, after_m) for lk in lks]
            ws = []
            for z, lk, cum, (_, lo, mask) in zip(zs, lks, cums, specs):
                rows = n_rows - lo
                r = _grow(r, rows - (0 if r is None else r.shape[0]), 1) if r is None or r.shape[0] < rows else r
                w = jnp.exp(z + lk + cum + r)
                ws.append((w if mask is None else jnp.where(mask, w, 0.0)).astype(BF16))
                r = r + jnp.sum(lk, axis=1, keepdims=True)
            acc = jnp.zeros((n_rows // 2, LANES), F32) if acc is None else acc
            for w, off, (_, lo, _) in zip(ws, offs, specs):
                part = jnp.dot(_side_by_side(w), _stack_heads(v_ref[pl.ds(off, t), :], heads), preferred_element_type=F32)
                acc = acc + (part if lo == 0 else _grow(part, lo // 2, LANES))
            return r, acc

        def head(n_more):
            specs = [(ATTN_SUB * i + s, 2 * s * t,
                      jnp.concatenate([causal, causal] + [jnp.ones_like(causal)] * (2 * (ATTN_SUB - 1 - s)), axis=0))
                     for s in reversed(range(ATTN_SUB))]
            specs += [(ATTN_SUB * i - 1 - b, 0, None) for b in range(n_more)]
            return blocks(specs, None, None)

        r, acc = lax.cond(ATTN_SUB * i >= ATTN_MORE, lambda: head(ATTN_MORE), lambda: head(0))

        def cond(c):
            return jnp.logical_and(c[0] >= 0, jnp.max(c[1]) > EXP_UNDERFLOW)

        def step(c):
            r, a = blocks([(c[0], 0, None)], c[1], c[2])
            return c[0] - 1, r, a

        first = jnp.where(ATTN_SUB * i >= ATTN_MORE, ATTN_SUB * i - 1 - ATTN_MORE, ATTN_SUB * i - 1)
        o_ref[...] = lax.while_loop(cond, step, (first, r, acc))[2]

    n_hp = D // LANES
    blk = pl.BlockSpec((tq, LANES), lambda hp, i: (i, hp))
    seq = pl.BlockSpec((S, LANES), lambda hp, i: (0, hp))
    n_i = S // tq
    body, in_specs, out_specs, out_shape, scratch, operands, aliases = _with_exchange(
        exchange, body, [blk, seq, seq], [blk], [_sds((S, D), F32)], [], [qs, kn, vb],
        lambda: jnp.logical_and(pl.program_id(0) == 0, pl.program_id(1) == 0),
        lambda: jnp.logical_and(pl.program_id(0) == n_hp - 1, pl.program_id(1) == n_i - 1))
    outs = _pcall(
        body, grid=(n_hp, n_i), in_specs=in_specs, out_specs=out_specs, out_shape=out_shape, scratch_shapes=scratch,
        input_output_aliases=aliases, compiler_params=_cp("arbitrary" if exchange else "parallel", "arbitrary"), name=name,
    )(*operands)
    return outs if exchange else outs[0]


def _attn_bwd(name, qs, kn, vb, o, do, exchange=None):
    S, D = qs.shape
    t = ATTN_BLOCK
    tq = ATTN_SUB * t

    def body(q_ref, k_ref, v_ref, o_ref, do_ref, dq_ref, dk_ref, dv_ref):
        i = pl.program_id(1)

        @pl.when(i == 0)
        def _():
            dk_ref[...] = jnp.zeros_like(dk_ref)
            dv_ref[...] = jnp.zeros_like(dv_ref)

        row, col, heads = _attn_consts()
        after_m = (row > col).astype(BF16)
        from_m = (row >= col).astype(BF16)
        causal = col < row
        q_all = _stack_heads(q_ref[...], heads)
        dob = do_ref[...].astype(BF16)
        do_all = _stack_heads(dob, heads)
        dsum_all = jnp.sum(_stack_heads(dob.astype(F32) * o_ref[...], heads), axis=1, keepdims=True)

        def blocks(specs, r, es, dq):
            n_rows = q_all.shape[0]
            offs = [pl.multiple_of(j * t, t) for j, _, _ in specs]
            masked = lambda x, mask: x if mask is None else jnp.where(mask, x, 0.0)
            top = lambda a, rows: a if a is not None and a.shape[0] == rows else _grow(a, rows - (0 if a is None else a.shape[0]), 1)
            zs = [lax.dot_general(q_all[lo:], k_ref[pl.ds(off, t), :], NT, preferred_element_type=F32)
                  for off, (_, lo, _) in zip(offs, specs)]
            gs = [lax.dot_general(do_all[lo:], v_ref[pl.ds(off, t), :], NT, preferred_element_type=F32)
                  for off, (_, lo, _) in zip(offs, specs)]
            lks = [masked(_log_keep(z), mask) for z, (_, _, mask) in zip(zs, specs)]
            cums = [_split_dot(lk, after_m) for lk in lks]
            ws, es_blk, sgs = [], [], []
            for z, g, lk, cum, (_, lo, mask) in zip(zs, gs, lks, cums, specs):
                r = top(r, n_rows - lo)
                ls = z + lk
                w = masked(jnp.exp(ls + cum + r), mask)
                ws.append(w.astype(BF16))
                es_blk.append(w * g)
                sgs.append(jnp.exp(ls))
                r = r + jnp.sum(lk, axis=1, keepdims=True)
            cum_es = [_split_dot(e, from_m) for e in es_blk]
            dzs = []
            for e, cum_e, sg, (_, lo, mask) in zip(es_blk, cum_es, sgs, specs):
                es = top(es, n_rows - lo)
                before = dsum_all[lo:] - (es + cum_e)
                dzs.append(masked(e - (e + before) * sg, mask).astype(BF16))
                es = es + jnp.sum(e, axis=1, keepdims=True)
            dq = jnp.zeros((n_rows // 2, LANES), F32) if dq is None else dq
            for dzb, w, off, (_, lo, _) in zip(dzs, ws, offs, specs):
                part = jnp.dot(_side_by_side(dzb), _stack_heads(k_ref[pl.ds(off, t), :], heads), preferred_element_type=F32)
                dq = dq + (part if lo == 0 else _grow(part, lo // 2, LANES))
                dk_ref[pl.ds(off, t), :] += lax.dot_general(dzb, q_all[lo:], TN, preferred_element_type=F32)
                dv_ref[pl.ds(off, t), :] += lax.dot_general(w, do_all[lo:], TN, preferred_element_type=F32)
            return r, es, dq

        def head(n_more):
            specs = [(ATTN_SUB * i + s, 2 * s * t,
                      jnp.concatenate([causal, causal] + [jnp.ones_like(causal)] * (2 * (ATTN_SUB - 1 - s)), axis=0))
                     for s in reversed(range(ATTN_SUB))]
            specs += [(ATTN_SUB * i - 1 - b, 0, None) for b in range(n_more)]
            return blocks(specs, None, None, None)

        r, es, dq = lax.cond(ATTN_SUB * i >= ATTN_MORE, lambda: head(ATTN_MORE), lambda: head(0))

        def cond(c):
            return jnp.logical_and(c[0] >= 0, jnp.max(c[1]) > EXP_UNDERFLOW)

        def step(c):
            r, es, a = blocks([(c[0], 0, None)], c[1], c[2], c[3])
            return c[0] - 1, r, es, a

        first = jnp.where(ATTN_SUB * i >= ATTN_MORE, ATTN_SUB * i - 1 - ATTN_MORE, ATTN_SUB * i - 1)
        dq_ref[...] = lax.while_loop(cond, step, (first, r, es, dq))[3]

    n_hp = D // LANES
    blk = pl.BlockSpec((tq, LANES), lambda hp, i: (i, hp))
    seq = pl.BlockSpec((S, LANES), lambda hp, i: (0, hp))
    n_i = S // tq
    body, in_specs, out_specs, out_shape, scratch, operands, aliases = _with_exchange(
        exchange, body, [blk, seq, seq, blk, blk], [blk, seq, seq], [_sds((S, D), F32)] * 3, [], [qs, kn, vb, o, do],
        lambda: jnp.logical_and(pl.program_id(0) == 0, pl.program_id(1) == 0),
        lambda: jnp.logical_and(pl.program_id(0) == n_hp - 1, pl.program_id(1) == n_i - 1))
    return _pcall(
        body, grid=(n_hp, n_i), in_specs=in_specs, out_specs=out_specs, out_shape=out_shape, scratch_shapes=scratch,
        input_output_aliases=aliases, compiler_params=_cp("arbitrary" if exchange else "parallel", "arbitrary"), name=name,
    )(*operands)


def _adamw(name, w, g, m, v):
    L, R, C = w.shape
    tr = _tile(R, 256, SUBLANES)
    c1 = 1.0 - ADAM_B1 ** ADAM_STEP
    c2 = 1.0 - ADAM_B2 ** ADAM_STEP

    def body(w_ref, g_ref, m_ref, v_ref, d_ref, mo_ref, vo_ref):
        gg = g_ref[...]
        mn = ADAM_B1 * m_ref[...] + (1.0 - ADAM_B1) * gg
        vn = ADAM_B2 * v_ref[...] + (1.0 - ADAM_B2) * (gg * gg)
        d_ref[...] = -ADAM_LR * ((mn / c1) / (jnp.sqrt(vn / c2) + ADAM_EPS) + ADAM_WD * w_ref[...])
        mo_ref[...] = mn
        vo_ref[...] = vn

    blk = pl.BlockSpec((None, tr, C), lambda l, i: (l, i, 0))
    return _pcall(
        body, grid=(L, R // tr), in_specs=[blk] * 4, out_specs=[blk] * 3, out_shape=[_sds(w.shape, F32)] * 3,
        compiler_params=_cp("parallel", "parallel"), name=name,
    )(w, g, m, v)


def _place():
    x, y, c = lax.axis_index("x"), lax.axis_index("y"), lax.axis_index("c")
    chips = [(1 - x, y), (x, 1 - y), (1 - x, 1 - y)]
    return x, y, c, chips


def _place_shard(name, w, j_idx):
    L, R, X = w.shape
    rh = R // 2
    tr = _tile(rh, 256, BF16_ROWS)

    def body(j_ref, w_ref, o_ref):
        o_ref[...] = w_ref[...].astype(BF16)

    return _pcall(
        body,
        grid_spec=pltpu.PrefetchScalarGridSpec(
            num_scalar_prefetch=1, grid=(L, 2, rh // tr),
            in_specs=[pl.BlockSpec((None, None, tr, X), lambda l, h, i, j_ref: (l, h, i, 0))],
            out_specs=pl.BlockSpec((None, None, None, tr, X), lambda l, h, i, j_ref: (l, j_ref[0], h, i, 0))),
        out_shape=_sds((L, N_CHIPS, 2, rh, X), BF16), compiler_params=_cp("parallel", "parallel", "parallel"), name=name,
    )(j_idx, w.reshape(L, 2, rh, X))


def _all_gather_weights(bufs, spans, small_ws):
    n_big, n_small = len(bufs), len(small_ws)
    n_in = n_big + n_small
    layers = [pl.ds(l0, n) for l0, n in spans]

    def body(*refs):
        ins, outs = refs[:n_in], refs[n_in:2 * n_in]
        send_sems, recv_sems, local_sems = refs[2 * n_in:]
        x, y, c, chips = _place()
        j_me = 2 * x + y
        j_of = [2 * cx + cy for cx, cy in chips]
        sibling = (x, y, 1 - c)

        def remote(src, dst, s, to):
            return pltpu.make_async_remote_copy(src_ref=src, dst_ref=dst, send_sem=send_sems.at[s], recv_sem=recv_sems.at[s],
                                                device_id=to, device_id_type=MESH)

        started = []
        for t in range(n_big, n_in):
            loc = pltpu.make_async_copy(ins[t], outs[t].at[:, j_me], local_sems.at[t - n_big])
            loc.start()
            started.append(loc)
        first = []
        for t in range(n_big):
            mine = outs[t].at[layers[t], j_me, c]
            for k in range(3):
                first.append(remote(mine, mine, 6 * t + k, (*chips[k], c)))
        for t in range(n_big, n_in):
            for k in range(3):
                first.append(remote(ins[t], outs[t].at[:, j_me], 6 * n_big + 3 * (t - n_big) + k, (*chips[k], c)))
        for cp in first:
            cp.start()
        passed = []
        for t in range(n_big):
            for k in range(3):
                landed = outs[t].at[layers[t], j_of[k], c]
                remote(landed, landed, 6 * t + k, (*chips[k], c)).wait_recv()
                fwd = remote(landed, landed, 6 * t + 3 + k, sibling)
                fwd.start()
                passed.append(fwd)
        for t in range(n_big):
            for k in range(3):
                other = outs[t].at[layers[t], j_of[k], 1 - c]
                remote(other, other, 6 * t + 3 + k, sibling).wait_recv()
        for t in range(n_big, n_in):
            for k in range(3):
                dst = outs[t].at[:, j_of[k]]
                remote(dst, dst, 6 * n_big + 3 * (t - n_big) + k, (*chips[k], c)).wait_recv()
        for cp in first + passed:
            cp.wait_send()
        for loc in started:
            loc.wait()

    out_shape = [_sds(b.shape, b.dtype) for b in bufs]
    out_shape += [_sds((w.shape[0], N_CHIPS) + w.shape[1:], w.dtype) for w in small_ws]
    n_sem = 6 * n_big + 3 * n_small
    outs = _pcall(
        body, in_specs=[ANY] * n_in, out_specs=[ANY] * n_in, out_shape=out_shape,
        input_output_aliases={t: t for t in range(n_big)},
        scratch_shapes=[pltpu.SemaphoreType.DMA((n_sem,)), pltpu.SemaphoreType.DMA((n_sem,)), pltpu.SemaphoreType.DMA((n_small,))],
        name="all_gather_weights",
    )(*bufs, *small_ws)
    return outs[:n_big], outs[n_big:]


class _Exchange:
    def __init__(self, operands, out_shapes, n_sems, copies, in_place=False):
        self.operands, self.out_shapes, self.n_sems, self.copies = list(operands), list(out_shapes), n_sems, copies
        self.aliases = {t: t for t in range(len(self.operands))} if in_place else {}

    @property
    def scratch(self):
        return [pltpu.SemaphoreType.DMA((self.n_sems,)), pltpu.SemaphoreType.DMA((self.n_sems,))]

    def split(self, refs):
        n_in, n_out = len(self.operands), len(self.out_shapes)
        return refs[:n_in], refs[n_in:n_in + n_out]

    def start(self, ins, outs, sems):
        for cp in self.copies(ins, outs, *sems):
            cp.start()

    def wait(self, ins, outs, sems):
        for cp in self.copies(ins, outs, *sems):
            cp.wait()


def _run_exchange(name, ex):
    n_in, n_out = len(ex.operands), len(ex.out_shapes)

    def body(*refs):
        ins, outs, sems = refs[:n_in], refs[n_in:n_in + n_out], refs[n_in + n_out:]
        ex.start(ins, outs, sems)
        ex.wait(ins, outs, sems)

    return _pcall(body, in_specs=[ANY] * n_in, out_specs=[ANY] * n_out, out_shape=ex.out_shapes, scratch_shapes=ex.scratch,
                  input_output_aliases=ex.aliases, name=name)(*ex.operands)


def _gather_chips_exchange(bufs, spans):
    def copies(ins, outs, send_sems, recv_sems):
        x, y, c, chips = _place()
        cps = []
        for t, (l0, n) in enumerate(spans):
            mine = outs[t].at[pl.ds(l0, n), 2 * x + y, c]
            cps += [pltpu.make_async_remote_copy(src_ref=mine, dst_ref=mine, send_sem=send_sems.at[3 * t + k],
                                                 recv_sem=recv_sems.at[3 * t + k], device_id=(cx, cy, c), device_id_type=MESH)
                    for k, (cx, cy) in enumerate(chips)]
        return cps

    return _Exchange(bufs, [_sds(b.shape, b.dtype) for b in bufs], 3 * len(bufs), copies, in_place=True)


def _gather_cores_exchange(bufs, spans):
    def copies(ins, outs, send_sems, recv_sems):
        x, y, c, chips = _place()
        cps = []
        for t, (l0, n) in enumerate(spans):
            for k, (cx, cy) in enumerate(chips):
                part = outs[t].at[pl.ds(l0, n), 2 * cx + cy, c]
                cps.append(pltpu.make_async_remote_copy(src_ref=part, dst_ref=part, send_sem=send_sems.at[3 * t + k],
                                                        recv_sem=recv_sems.at[3 * t + k], device_id=(x, y, 1 - c),
                                                        device_id_type=MESH))
        return cps

    return _Exchange(bufs, [_sds(b.shape, b.dtype) for b in bufs], 3 * len(bufs), copies, in_place=True)


def _core_halves_exchange(grads, spans):
    def copies(ins, outs, send_sems, recv_sems):
        x, y, c, _ = _place()
        return [pltpu.make_async_remote_copy(src_ref=ins[t].at[pl.ds(l0, n), :, 1 - c], dst_ref=outs[t],
                                             send_sem=send_sems.at[t], recv_sem=recv_sems.at[t], device_id=(x, y, 1 - c),
                                             device_id_type=MESH) for t, (l0, n) in enumerate(spans)]

    shapes = [_sds((n, g.shape[1], g.shape[3], g.shape[4]), F32) for g, (_, n) in zip(grads, spans)]
    return _Exchange(grads, shapes, len(grads), copies)


def _add_core_halves(name, g, a, c_idx, l0):
    _, nj, _, rh, X = g.shape
    L = a.shape[0]
    tr = _tile(rh, 256, BF16_ROWS)

    def body(c_ref, g_ref, a_ref, o_ref, ob_ref):
        s = g_ref[...] + a_ref[...]
        o_ref[...] = s
        ob_ref[...] = s.astype(BF16)

    blk = pl.BlockSpec((None, None, tr, X), lambda l, j, i, c_ref: (l, j, i, 0))
    return _pcall(
        body,
        grid_spec=pltpu.PrefetchScalarGridSpec(
            num_scalar_prefetch=1, grid=(L, nj, rh // tr),
            in_specs=[pl.BlockSpec((None, None, None, tr, X), lambda l, j, i, c_ref: (l + l0, j, c_ref[0], i, 0)), blk],
            out_specs=[blk, blk]),
        out_shape=[_sds((L, nj, rh, X), F32), _sds((L, nj, rh, X), BF16)],
        compiler_params=_cp("parallel", "parallel", "parallel"), name=name,
    )(c_idx, g, a)


def _chip_shards_exchange(parts):
    def copies(ins, outs, send_sems, recv_sems):
        x, y, c, chips = _place()
        return [pltpu.make_async_remote_copy(
            src_ref=ins[t].at[:, 2 * cx + cy], dst_ref=outs[t].at[k], send_sem=send_sems.at[3 * t + k],
            recv_sem=recv_sems.at[3 * t + k], device_id=(cx, cy, c), device_id_type=MESH)
            for t in range(len(parts)) for k, (cx, cy) in enumerate(chips)]

    shapes = [_sds((3, p.shape[0], p.shape[2], p.shape[3]), p.dtype) for p in parts]
    return _Exchange(parts, shapes, 3 * len(parts), copies)


def _add_chip_shards(name, p, b, jc_idx, l0, n_layers, buf):
    n, _, rh, X = p.shape
    tr = _tile(rh, 256, BF16_ROWS)

    def body(jc_ref, p_ref, b_ref, *rest):
        rest[-1][...] = ((p_ref[...] + b_ref[0].astype(F32)) + b_ref[1].astype(F32)) + b_ref[2].astype(F32)

    in_specs = [pl.BlockSpec((None, None, tr, X), lambda l, i, jc: (l, jc[0], i, 0)),
                pl.BlockSpec((3, None, tr, X), lambda l, i, jc: (0, l, i, 0))]
    operands = [jc_idx, p, b]
    if buf is not None:
        in_specs.append(ANY)
        operands.append(buf)
    return _pcall(
        body,
        grid_spec=pltpu.PrefetchScalarGridSpec(
            num_scalar_prefetch=1, grid=(n, rh // tr), in_specs=in_specs,
            out_specs=pl.BlockSpec((None, None, tr, X), lambda l, i, jc: (l + l0, jc[1], i, 0))),
        out_shape=_sds((n_layers, 2, rh, X), F32), input_output_aliases={3: 0} if buf is not None else {},
        compiler_params=_cp("parallel", "parallel"), name=name,
    )(*operands)


def _join_core_halves(bufs):
    n = len(bufs)

    def body(*refs):
        outs = refs[n:2 * n]
        send_sems, recv_sems = refs[2 * n:]
        x, y, c, _ = _place()
        cps = [pltpu.make_async_remote_copy(src_ref=outs[t].at[:, c], dst_ref=outs[t].at[:, c], send_sem=send_sems.at[t],
                                            recv_sem=recv_sems.at[t], device_id=(x, y, 1 - c), device_id_type=MESH)
               for t in range(n)]
        for cp in cps:
            cp.start()
        for t in range(n):
            pltpu.make_async_remote_copy(src_ref=outs[t].at[:, c], dst_ref=outs[t].at[:, 1 - c], send_sem=send_sems.at[t],
                                         recv_sem=recv_sems.at[t], device_id=(x, y, 1 - c), device_id_type=MESH).wait()

    outs = _pcall(
        body, in_specs=[ANY] * n, out_specs=[ANY] * n, out_shape=[_sds(b.shape, F32) for b in bufs],
        input_output_aliases={t: t for t in range(n)},
        scratch_shapes=[pltpu.SemaphoreType.DMA((n,)), pltpu.SemaphoreType.DMA((n,))],
        name="grad_join_core_halves",
    )(*bufs)
    return [o.reshape(o.shape[0], 2 * o.shape[2], o.shape[3]) for o in outs]


def _all_reduce_small(packed):
    R, C = packed.shape

    def body(x_ref, o_ref, slots, send_sems, recv_sems):
        x, y, c, _ = _place()
        me = 4 * x + 2 * y + c
        slots[me] = x_ref[...]
        cps = []
        for d in range(N_DEV):
            to = (d // 4, (d // 2) % 2, d % 2)
            cp = pltpu.make_async_remote_copy(src_ref=x_ref, dst_ref=slots.at[me], send_sem=send_sems.at[d],
                                              recv_sem=recv_sems.at[me], device_id=to, device_id_type=MESH)
            cps.append(cp)

            @pl.when(d != me)
            def _():
                cp.start()

        for d in range(N_DEV):
            @pl.when(d != me)
            def _():
                pltpu.make_async_remote_copy(src_ref=x_ref, dst_ref=slots.at[d], send_sem=send_sems.at[d],
                                             recv_sem=recv_sems.at[d], device_id=(x, y, c), device_id_type=MESH).wait_recv()
                cps[d].wait_send()

        acc = slots[0]
        for d in range(1, N_DEV):
            acc = acc + slots[d]
        o_ref[...] = acc

    vm = pl.BlockSpec(memory_space=pltpu.VMEM)
    return _pcall(
        body, in_specs=[vm], out_specs=vm, out_shape=_sds((R, C), F32),
        scratch_shapes=[pltpu.VMEM((N_DEV, R, C), F32), pltpu.SemaphoreType.DMA((N_DEV,)), pltpu.SemaphoreType.DMA((N_DEV,))],
        compiler_params=pltpu.CompilerParams(vmem_limit_bytes=VMEM_LIMIT_BYTES), name="all_reduce_small",
    )(packed)


PACK = SUBLANES * LANES


def _pack(arrays):
    flat = []
    for a in arrays:
        v = a.reshape(-1)
        flat.append(jnp.pad(v, (0, (-v.shape[0]) % PACK)))
    return jnp.concatenate(flat).reshape(-1, LANES)


def _unpack(packed, shapes):
    flat = packed.reshape(-1)
    out, pos = [], 0
    for s in shapes:
        n = 1
        for d in s:
            n *= d
        out.append(flat[pos:pos + n].reshape(s))
        pos += n + (-n) % PACK
    return out


def kernel(x, mix_norm_g, ffn_norm_g, conv_w_in, conv_a_dw_w, conv_a_dw_b, conv_a_ln_g, conv_a_ln_b, conv_b_dw_w, conv_w_out, attn_w_qkv, attn_q_g, attn_k_g, attn_w_o, ffn_w_up, ffn_dw_w, ffn_dw_b, ffn_w_down, loss_target, m_mix_norm_g, m_ffn_norm_g, m_conv_w_in, m_conv_a_dw_w, m_conv_a_dw_b, m_conv_a_ln_g, m_conv_a_ln_b, m_conv_b_dw_w, m_conv_w_out, m_attn_w_qkv, m_attn_q_g, m_attn_k_g, m_attn_w_o, m_ffn_w_up, m_ffn_dw_w, m_ffn_dw_b, m_ffn_w_down, v_mix_norm_g, v_ffn_norm_g, v_conv_w_in, v_conv_a_dw_w, v_conv_a_dw_b, v_conv_a_ln_g, v_conv_a_ln_b, v_conv_b_dw_w, v_conv_w_out, v_attn_w_qkv, v_attn_q_g, v_attn_k_g, v_attn_w_o, v_ffn_w_up, v_ffn_dw_w, v_ffn_dw_b, v_ffn_w_down):
    depth = mix_norm_g.shape[0]
    n_even, n_odd = conv_w_in.shape[0], attn_w_qkv.shape[0]
    S, D = x.shape[1], x.shape[2]
    dg = D // 2
    x0 = x.reshape(S, D)
    target = loss_target.reshape(S, D)
    j_me = 2 * lax.axis_index("x") + lax.axis_index("y")
    c_me = lax.axis_index("c")
    j_idx = j_me.astype(jnp.int32).reshape(1)
    c_idx = c_me.astype(jnp.int32).reshape(1)

    col_names = ["conv_w_in", "attn_w_qkv", "ffn_w_up"]
    row_names = ["conv_w_out", "attn_w_o", "ffn_w_down"]
    local = dict(conv_w_in=conv_w_in, attn_w_qkv=attn_w_qkv, ffn_w_up=ffn_w_up, conv_w_out=conv_w_out, attn_w_o=attn_w_o,
                 ffn_w_down=ffn_w_down)
    gbuf = {n: _place_shard(f"place_{n}", local[n], j_idx) for n in col_names + row_names}

    def weights_of(layer):
        mixer = ("conv_w_in", "conv_w_out") if layer % 2 == 0 else ("attn_w_qkv", "attn_w_o")
        return {mixer[0]: (layer // 2, 1), mixer[1]: (layer // 2, 1), "ffn_w_up": (layer, 1), "ffn_w_down": (layer, 1)}

    def w_col(n):
        return gbuf[n].reshape(gbuf[n].shape[0], N_CHIPS, -1, gbuf[n].shape[4])

    def w_row(n):
        return gbuf[n].reshape(gbuf[n].shape[0], -1, gbuf[n].shape[4])

    def carry(make_exchange, layer):
        if layer + 1 == depth:
            return None, []
        names = list(weights_of(layer + 1))
        return make_exchange([gbuf[n] for n in names], list(weights_of(layer + 1).values())), names

    first = weights_of(0)
    outs, (a_dw, b_dw, f_dw) = _all_gather_weights([gbuf[n] for n in first], list(first.values()),
                                                   [conv_a_dw_w, conv_b_dw_w, ffn_dw_w])
    gbuf.update(zip(first, outs))
    unshard = lambda a: jnp.moveaxis(a, 1, 2).reshape(a.shape[0], a.shape[2], N_CHIPS * a.shape[3])
    a_dw, b_dw, f_dw = unshard(a_dw), unshard(b_dw), unshard(f_dw)
    qk_gain = [jnp.stack([jnp.tile(attn_q_g[i], LANES // HEAD_DIM), jnp.tile(attn_k_g[i], LANES // HEAD_DIM)])
               for i in range(n_odd)]

    saved = []
    xc = x0
    for layer in range(depth):
        i = layer // 2
        tag = f"l{layer}"
        s = {"x_in": xc}
        h = _rms_fwd(f"rms_mix_fwd_{tag}", xc, mix_norm_g, layer)
        s["h"] = h
        ex, names = carry(_gather_chips_exchange, layer)
        if layer % 2 == 0:
            p = _mm_fwd(f"conv_in_fwd_{tag}", h, w_col("conv_w_in"), i, colshard=True)
            ab = _convmix_fwd(f"convmix_fwd_{tag}", p, a_dw, conv_a_dw_b, conv_a_ln_g, conv_a_ln_b, b_dw, i, ex)
            if ex:
                ab, *new = ab
                gbuf.update(zip(names, new))
            xm = _mm_fwd(f"conv_out_fwd_{tag}", ab, w_row("conv_w_out"), i, colshard=False, res=xc)
            s.update(p=p, ab=ab)
        else:
            qkv = _mm_fwd(f"attn_qkv_fwd_{tag}", h, w_col("attn_w_qkv"), i, colshard=True)
            qs, kn, vb = _qknorm_fwd(f"qknorm_fwd_{tag}", qkv, qk_gain[i])
            o = _attn_fwd(f"attn_fwd_{tag}", qs, kn, vb, ex)
            if ex:
                o, *new = o
                gbuf.update(zip(names, new))
            xm = _mm_fwd(f"attn_out_fwd_{tag}", o, w_row("attn_w_o"), i, colshard=False, res=xc)
            s.update(qkv=qkv, qs=qs, kn=kn, vb=vb, o=o)
        s["x_mid"] = xm
        h2 = _rms_fwd(f"rms_ffn_fwd_{tag}", xm, ffn_norm_g, layer)
        u2 = _mm_fwd(f"ffn_up_fwd_{tag}", h2, w_col("ffn_w_up"), layer, colshard=True, out_split=2)
        ex, names = carry(_gather_cores_exchange, layer)
        f = _ffn_mid_fwd(f"ffn_mid_fwd_{tag}", u2, f_dw, ffn_dw_b, layer, ex)
        if ex:
            f, *new = f
            gbuf.update(zip(names, new))
        xc = _mm_fwd(f"ffn_down_fwd_{tag}", f, w_row("ffn_w_down"), layer, colshard=False, res=xm)
        s.update(h2=h2, u2=u2, f=f)
        saved.append(s)

    dx, loss_tile = _loss_fwd_bwd("loss", xc, target)

    w_in, w_qkv, w_up = w_col("conv_w_in"), w_col("attn_w_qkv"), w_col("ffn_w_up")
    w_out, w_o, w_down = w_row("conv_w_out"), w_row("attn_w_o"), w_row("ffn_w_down")
    g_up = g_down = g_in = g_out = g_qkv = g_o = None
    big_names = col_names + row_names

    def halves_view(n, g):
        if n in col_names:
            return g.reshape(g.shape[0], N_CHIPS, 2, g.shape[2] // 2, g.shape[3])
        return g.reshape(g.shape[0], N_CHIPS, 2, g.shape[1] // (2 * N_CHIPS), g.shape[2])

    ffn_of_0 = {"ffn_w_up": (0, 1), "ffn_w_down": (0, 1)}
    mixer_of_0 = {"conv_w_in": (0, 1), "conv_w_out": (0, 1)}
    summed_parts = {n: [] for n in big_names}

    def stacks():
        return {"conv_w_in": g_in, "attn_w_qkv": g_qkv, "ffn_w_up": g_up, "conv_w_out": g_out, "attn_w_o": g_o,
                "ffn_w_down": g_down}

    def core_exchange(group):
        return _core_halves_exchange([halves_view(n, stacks()[n]) for n in group], list(group.values()))

    def chip_exchange(tag, arrived):
        sums, parts = [], []
        for group, from_sibling in arrived:
            for n, a in zip(group, from_sibling):
                f32_sum, bf16_sum = _add_core_halves(f"grad_add_core_{n}_{tag}_{group[n][0]}", halves_view(n, stacks()[n]), a,
                                                     c_idx, group[n][0])
                sums.append((n, group[n][0], f32_sum))
                parts.append(bf16_sum)
        return _chip_shards_exchange(parts), sums

    def record(sums, from_chips):
        for (n, l0, f32_sum), b in zip(sums, from_chips):
            summed_parts[n].append((l0, f32_sum, b))

    d_mix_g, d_ffn_g = [None] * depth, [None] * depth
    d_ffn_dw_w, d_ffn_dw_b = [None] * depth, [None] * depth
    d_a_dw_w, d_a_dw_b, d_a_ln_g, d_a_ln_b, d_b_dw_w = ([None] * n_even for _ in range(5))
    d_q_g, d_k_g = [None] * n_odd, [None] * n_odd
    for layer in reversed(range(depth)):
        i = layer // 2
        tag = f"l{layer}"
        s = saved[layer]
        df = _mm_dgrad(f"ffn_down_dgrad_{tag}", dx, w_down, layer, colshard=False)
        g_down = _mm_wgrad(f"ffn_down_wgrad_{tag}", s["f"], dx, layer, depth, g_down, colshard=False)
        above = weights_of(layer + 1) if layer + 1 < depth else None
        arrived = []
        du2, dww, dwb, *from_sibling = _ffn_mid_bwd(f"ffn_mid_bwd_{tag}", s["u2"], df, f_dw, ffn_dw_b, layer,
                                                    core_exchange(above) if above else None)
        if above:
            arrived.append((above, from_sibling))
        d_ffn_dw_w[layer] = jnp.moveaxis(dww, 0, 1).reshape(FFN_CONV_WIDTH, -1)
        d_ffn_dw_b[layer] = dwb.reshape(-1)
        dh2 = _mm_dgrad(f"ffn_up_dgrad_{tag}", du2, w_up, layer, colshard=True)
        g_up = _mm_wgrad(f"ffn_up_wgrad_{tag}", s["h2"], du2, layer, depth, g_up, colshard=True)
        dx, dg_, *from_sibling = _rms_bwd(f"rms_ffn_bwd_{tag}", s["x_mid"], ffn_norm_g, layer, dh2, dx,
                                          core_exchange(ffn_of_0) if layer == 0 else None)
        if layer == 0:
            arrived.append((ffn_of_0, from_sibling))
        d_ffn_g[layer] = dg_.reshape(-1)
        if layer % 2 == 0:
            dab = _mm_dgrad(f"conv_out_dgrad_{tag}", dx, w_out, i, colshard=False)
            g_out = _mm_wgrad(f"conv_out_wgrad_{tag}", s["ab"], dx, i, n_even, g_out, colshard=False)
            chip_ex, sums = chip_exchange(tag, arrived) if arrived else (None, [])
            dp, daw, dab_b, dlg, dlb, dbw, *from_chips = _convmix_bwd(
                f"convmix_bwd_{tag}", s["p"], dab, a_dw, conv_a_dw_b, conv_a_ln_g, conv_a_ln_b, b_dw, i, chip_ex)
            record(sums, from_chips)
            d_a_dw_w[i], d_a_dw_b[i], d_a_ln_g[i], d_a_ln_b[i], d_b_dw_w[i] = (
                daw, dab_b.reshape(-1), dlg.reshape(-1), dlb.reshape(-1), dbw)
            dh = _mm_dgrad(f"conv_in_dgrad_{tag}", dp, w_in, i, colshard=True)
            g_in = _mm_wgrad(f"conv_in_wgrad_{tag}", s["h"], dp, i, n_even, g_in, colshard=True)
        else:
            do = _mm_dgrad(f"attn_out_dgrad_{tag}", dx, w_o, i, colshard=False)
            g_o = _mm_wgrad(f"attn_out_wgrad_{tag}", s["o"], dx, i, n_odd, g_o, colshard=False)
            chip_ex, sums = chip_exchange(tag, arrived) if arrived else (None, [])
            dq, dk, dv, *from_chips = _attn_bwd(f"attn_bwd_{tag}", s["qs"], s["kn"], s["vb"], s["o"], do, chip_ex)
            record(sums, from_chips)
            dqkv, dgain = _qknorm_bwd(f"qknorm_bwd_{tag}", s["qkv"], dq, dk, dv, qk_gain[i])
            d_q_g[i] = dgain[0, :HEAD_DIM] + dgain[0, HEAD_DIM:]
            d_k_g[i] = dgain[1, :HEAD_DIM] + dgain[1, HEAD_DIM:]
            dh = _mm_dgrad(f"attn_qkv_dgrad_{tag}", dqkv, w_qkv, i, colshard=True)
            g_qkv = _mm_wgrad(f"attn_qkv_wgrad_{tag}", s["h"], dqkv, i, n_odd, g_qkv, colshard=True)
        dx, dg_ = _rms_bwd(f"rms_mix_bwd_{tag}", s["x_in"], mix_norm_g, layer, dh, dx)
        d_mix_g[layer] = dg_.reshape(-1)
    grad_x = dx.reshape(1, S, D)

    small = {
        "mix_norm_g": jnp.stack(d_mix_g), "ffn_norm_g": jnp.stack(d_ffn_g),
        "conv_a_dw_w": jnp.stack(d_a_dw_w), "conv_a_dw_b": jnp.stack(d_a_dw_b),
        "conv_a_ln_g": jnp.stack(d_a_ln_g), "conv_a_ln_b": jnp.stack(d_a_ln_b),
        "conv_b_dw_w": jnp.stack(d_b_dw_w), "attn_q_g": jnp.stack(d_q_g), "attn_k_g": jnp.stack(d_k_g),
        "ffn_dw_w": jnp.stack(d_ffn_dw_w), "ffn_dw_b": jnp.stack(d_ffn_dw_b),
    }
    small_names = list(small)
    summed = _all_reduce_small(_pack([loss_tile] + [small[n] for n in small_names]))
    parts = _unpack(summed, [loss_tile.shape] + [small[n].shape for n in small_names])
    loss = parts[0][0, 0]
    small_g = dict(zip(small_names, parts[1:]))
    for n in ("conv_a_dw_w", "conv_b_dw_w", "ffn_dw_w"):
        cs = small_g[n].shape[2] // N_CHIPS
        small_g[n] = lax.dynamic_slice_in_dim(small_g[n], j_me * cs, cs, axis=2)

    from_sibling = _run_exchange("grad_exchange_core_halves", core_exchange(mixer_of_0))
    chip_ex, sums = chip_exchange("last", [(mixer_of_0, from_sibling)])
    record(sums, _run_exchange("grad_exchange_chip_shards", chip_ex))
    jc_idx = jnp.concatenate([j_idx, c_idx])
    totals = {}
    for n in big_names:
        total = None
        for l0, p, b in summed_parts[n]:
            total = _add_chip_shards(f"grad_add_chips_{n}_{l0}", p, b, jc_idx, l0, stacks()[n].shape[0], total)
        totals[n] = total
    big_g = dict(zip(big_names, _join_core_halves([totals[n] for n in big_names])))

    weights = dict(mix_norm_g=mix_norm_g, ffn_norm_g=ffn_norm_g, conv_w_in=conv_w_in, conv_a_dw_w=conv_a_dw_w, conv_a_dw_b=conv_a_dw_b, conv_a_ln_g=conv_a_ln_g, conv_a_ln_b=conv_a_ln_b, conv_b_dw_w=conv_b_dw_w, conv_w_out=conv_w_out, attn_w_qkv=attn_w_qkv, attn_q_g=attn_q_g, attn_k_g=attn_k_g, attn_w_o=attn_w_o, ffn_w_up=ffn_w_up, ffn_dw_w=ffn_dw_w, ffn_dw_b=ffn_dw_b, ffn_w_down=ffn_w_down)
    m_in = dict(mix_norm_g=m_mix_norm_g, ffn_norm_g=m_ffn_norm_g, conv_w_in=m_conv_w_in, conv_a_dw_w=m_conv_a_dw_w, conv_a_dw_b=m_conv_a_dw_b, conv_a_ln_g=m_conv_a_ln_g, conv_a_ln_b=m_conv_a_ln_b, conv_b_dw_w=m_conv_b_dw_w, conv_w_out=m_conv_w_out, attn_w_qkv=m_attn_w_qkv, attn_q_g=m_attn_q_g, attn_k_g=m_attn_k_g, attn_w_o=m_attn_w_o, ffn_w_up=m_ffn_w_up, ffn_dw_w=m_ffn_dw_w, ffn_dw_b=m_ffn_dw_b, ffn_w_down=m_ffn_w_down)
    v_in = dict(mix_norm_g=v_mix_norm_g, ffn_norm_g=v_ffn_norm_g, conv_w_in=v_conv_w_in, conv_a_dw_w=v_conv_a_dw_w, conv_a_dw_b=v_conv_a_dw_b, conv_a_ln_g=v_conv_a_ln_g, conv_a_ln_b=v_conv_a_ln_b, conv_b_dw_w=v_conv_b_dw_w, conv_w_out=v_conv_w_out, attn_w_qkv=v_attn_w_qkv, attn_q_g=v_attn_q_g, attn_k_g=v_attn_k_g, attn_w_o=v_attn_w_o, ffn_w_up=v_ffn_w_up, ffn_dw_w=v_ffn_dw_w, ffn_dw_b=v_ffn_dw_b, ffn_w_down=v_ffn_w_down)
    order = list(weights)
    grads, delta, new_m, new_v = {}, {}, {}, {}
    for n in big_names:
        grads[n] = big_g[n]
        delta[n], new_m[n], new_v[n] = _adamw(f"adamw_{n}", weights[n], big_g[n], m_in[n], v_in[n])
    shapes = [weights[n].shape for n in small_names]
    packed = [_pack([d[n] for n in small_names]) for d in (weights, small_g, m_in, v_in)]
    upd = _adamw("adamw_small", *[p[None] for p in packed])
    for out, res in zip((delta, new_m, new_v), upd):
        out.update(zip(small_names, _unpack(res[0], shapes)))
    grads.update({n: small_g[n].reshape(weights[n].shape) for n in small_names})
    return (loss, grad_x, *[grads[n] for n in order], *[delta[n] for n in order], *[new_m[n] for n in order],
            *[new_v[n] for n in order])
```

```python
import jax
import jax.numpy as jnp
from jax import lax
from jax.experimental import pallas as pl
from jax.experimental.pallas import tpu as pltpu

F32 = jnp.float32
BF16 = jnp.bfloat16
EPS = 1e-6
CONV_A_WIDTH = 31
CONV_B_WIDTH = 3
FFN_CONV_WIDTH = 3
HEAD_DIM = 64
ADAM_LR = 0.001
ADAM_B1 = 0.9
ADAM_B2 = 0.999
ADAM_EPS = 1e-08
ADAM_WD = 0.01
ADAM_STEP = 10

LANES = 128
SUBLANES = 8
BF16_ROWS = 16
V7X_VMEM_BYTES = 64 * 1024 * 1024
VMEM_LIMIT_BYTES = V7X_VMEM_BYTES * 3 // 4
MM_VMEM_BUDGET = VMEM_LIMIT_BYTES * 4 // 5
MM_ROWS = 1024
N_CHIPS = 4
N_DEV = 8
HALO_A = 32
HALO_S = 8
ELT_ROWS = 64
ATTN_BLOCK = 128
ATTN_SUB = 2
ATTN_MORE = 2
EXP_UNDERFLOW = -104.0
MESH = pl.DeviceIdType.MESH
ANY = pl.BlockSpec(memory_space=pl.ANY)
NT = (((1,), (1,)), ((), ()))
NN = (((1,), (0,)), ((), ()))
TN = (((0,), (0,)), ((), ()))


def _pcall(body, **kw):
    return pl.pallas_call(body, **kw)


def _cp(*sem):
    return pltpu.CompilerParams(dimension_semantics=sem, vmem_limit_bytes=VMEM_LIMIT_BYTES)


def _sds(shape, dtype):
    return jax.ShapeDtypeStruct(tuple(shape), dtype)


def _tile(n, cap, align=LANES):
    if n <= cap:
        return n
    for t in range(cap - cap % align, 0, -align):
        if n % t == 0:
            return t
    return n


def _sig(x):
    return 0.5 * jnp.tanh(0.5 * x) + 0.5


def _rowsum(x):
    return jnp.sum(x, axis=0, keepdims=True)


def _fold(x):
    acc = x[0:SUBLANES]
    for r in range(SUBLANES, x.shape[0], SUBLANES):
        acc = acc + x[r:r + SUBLANES]
    return acc


def _with_exchange(ex, body, in_specs, out_specs, out_shape, scratch, operands, first, last):
    if ex is None:
        return body, in_specs, out_specs, out_shape, scratch, operands, {}
    n_in, n_out, n_scr = len(in_specs), len(out_specs), len(scratch)
    e_in, e_out = len(ex.operands), len(ex.out_shapes)

    def hosted(*refs):
        refs = list(refs)
        ins, refs = refs[:n_in], refs[n_in:]
        e_ins, refs = refs[:e_in], refs[e_in:]
        outs, refs = refs[:n_out], refs[n_out:]
        e_outs, refs = refs[:e_out], refs[e_out:]
        scr, sems = refs[:n_scr], refs[n_scr:]

        @pl.when(first())
        def _():
            ex.start(e_ins, e_outs, sems)

        body(*ins, *outs, *scr)

        @pl.when(last())
        def _():
            ex.wait(e_ins, e_outs, sems)

    return (hosted, in_specs + [ANY] * e_in, out_specs + [ANY] * e_out, out_shape + ex.out_shapes, scratch + ex.scratch,
            operands + ex.operands, {n_in + i: n_out + o for i, o in ex.aliases.items()})


def _mm_call(name, dn, operands, in_specs, out_shape, out_spec, grid, nk, acc_shape, has_res, has_alias):
    def body(*refs):
        a_ref, b_ref = refs[0], refs[1]
        pos = 2
        res_ref = refs[pos] if has_res else None
        pos += int(has_res) + int(has_alias)
        o_ref = refs[pos]
        acc_ref = refs[pos + 1] if nk > 1 else None
        p = lax.dot_general(a_ref[...].astype(BF16), b_ref[...].astype(BF16), dn, preferred_element_type=F32)

        def finish(v):
            if has_res:
                v = v + res_ref[...]
            o_ref[...] = v.astype(o_ref.dtype)

        if nk == 1:
            finish(p)
        else:
            k = pl.program_id(2)

            @pl.when(k == 0)
            def _():
                acc_ref[...] = p

            @pl.when(k > 0)
            def _():
                acc_ref[...] += p

            @pl.when(k == nk - 1)
            def _():
                finish(acc_ref[...])

    aliases = {len(operands) - 1: 0} if has_alias else {}
    return _pcall(
        body, grid=grid, in_specs=in_specs, out_specs=out_spec, out_shape=out_shape,
        scratch_shapes=[pltpu.VMEM(acc_shape, F32)] if nk > 1 else [],
        input_output_aliases=aliases, compiler_params=_cp("parallel", "parallel", "arbitrary"), name=name,
    )(*operands)


def _mm_fwd(name, a, w, l, *, colshard, res=None, out_split=1):
    M, K = a.shape
    tm = _tile(M, MM_ROWS, BF16_ROWS)
    if colshard:
        cs = w.shape[3]
        N, tn, tk = N_CHIPS * cs, cs, K
        b_spec = pl.BlockSpec((None, None, tk, tn), lambda j, i, k: (l, j, k, 0))
    else:
        N = w.shape[2]
        tn, tk = _tile(N, 1024), K
        if K > 1536:
            tm = _tile(M, MM_ROWS // 2, BF16_ROWS)
        b_spec = pl.BlockSpec((None, tk, tn), lambda j, i, k: (l, k, j))
    nk = K // tk
    in_specs = [pl.BlockSpec((tm, tk), lambda j, i, k: (i, k)), b_spec]
    operands = [a, w]
    if res is not None:
        in_specs.append(pl.BlockSpec((tm, tn), lambda j, i, k: (i, j)))
        operands.append(res)
    if out_split == 1:
        out_shape = _sds((M, N), F32)
        out_spec = pl.BlockSpec((tm, tn), lambda j, i, k: (i, j))
    else:
        per = N // tn // out_split
        out_shape = _sds((out_split, M, N // out_split), F32)
        out_spec = pl.BlockSpec((None, tm, tn), lambda j, i, k: (j // per, i, j % per))
    return _mm_call(name, NN, operands, in_specs, out_shape, out_spec, (N // tn, M // tm, nk), nk, (tm, tn),
                    res is not None, False)


def _mm_dgrad(name, g, w, l, *, colshard):
    split = g.ndim == 3
    M = g.shape[-2]
    tm = _tile(M, MM_ROWS, BF16_ROWS)
    if colshard:
        kw, cs = w.shape[2], w.shape[3]
        tm = _tile(M, MM_ROWS // 2, BF16_ROWS)
        per = N_CHIPS // g.shape[0] if split else N_CHIPS

        def body(a_ref, b_ref, o_ref):
            acc = None
            for j in range(N_CHIPS):
                cols = slice((j % per) * cs, (j % per + 1) * cs)
                a = a_ref[j // per, :, cols] if split else a_ref[:, cols]
                p = lax.dot_general(a.astype(BF16), b_ref[j], NT, preferred_element_type=F32)
                acc = p if acc is None else acc + p
            o_ref[...] = acc

        a_spec = (pl.BlockSpec((g.shape[0], tm, g.shape[2]), lambda i: (0, i, 0)) if split
                  else pl.BlockSpec((tm, N_CHIPS * cs), lambda i: (i, 0)))
        return _pcall(
            body, grid=(M // tm,),
            in_specs=[a_spec, pl.BlockSpec((None, N_CHIPS, kw, cs), lambda i: (l, 0, 0, 0))],
            out_specs=pl.BlockSpec((tm, kw), lambda i: (i, 0)), out_shape=_sds((M, kw), F32),
            compiler_params=_cp("parallel"), name=name,
        )(g, w)
    else:
        kw, ncon = w.shape[1], w.shape[2]
        tn, tk = _tile(kw, 1408), _tile(ncon, 1536)
        nk = ncon // tk
        b_spec = pl.BlockSpec((None, tn, tk), lambda j, i, k: (l, j, k))
    if split:
        per = nk // g.shape[0]
        a_spec = pl.BlockSpec((None, tm, tk), lambda j, i, k: (k // per, i, k % per))
    else:
        a_spec = pl.BlockSpec((tm, tk), lambda j, i, k: (i, k))
    out_shape = _sds((M, kw), F32)
    out_spec = pl.BlockSpec((tm, tn), lambda j, i, k: (i, j))
    return _mm_call(name, NT, [g, w], [a_spec, b_spec], out_shape, out_spec, (kw // tn, M // tm, nk), nk, (tm, tn),
                    False, False)


def _mm_wgrad(name, a, g, l, n_layers, buf, *, colshard):
    S, M = a.shape
    split = g.ndim == 3
    N = g.shape[-1] * (g.shape[0] if split else 1)
    tm = _tile(M, 1408)
    tn = N // N_CHIPS if colshard else _tile(N, 1024)
    per_row = 2 * (tm * a.dtype.itemsize + tn * g.dtype.itemsize)
    tk = _tile(S, max(BF16_ROWS, min(2048, (MM_VMEM_BUDGET - 3 * tm * tn * 4) // per_row)), BF16_ROWS)
    nk = S // tk
    if colshard:
        out_shape = _sds((n_layers, N_CHIPS, M, tn), F32)
        out_spec = pl.BlockSpec((None, None, tm, tn), lambda j, i, k: (l, j, i, 0))
    else:
        out_shape = _sds((n_layers, M, N), F32)
        out_spec = pl.BlockSpec((None, tm, tn), lambda j, i, k: (l, i, j))
    if split:
        per = N // tn // g.shape[0]
        b_spec = pl.BlockSpec((None, tk, tn), lambda j, i, k: (j // per, k, j % per))
    else:
        b_spec = pl.BlockSpec((tk, tn), lambda j, i, k: (k, j))
    in_specs = [pl.BlockSpec((tk, tm), lambda j, i, k: (k, i)), b_spec]
    operands = [a, g]
    if buf is not None:
        in_specs.append(ANY)
        operands.append(buf)
    return _mm_call(name, TN, operands, in_specs, out_shape, out_spec, (N // tn, M // tm, nk), nk, (tm, tn),
                    False, buf is not None)


def _rms_fwd(name, x, g, l):
    S, D = x.shape
    tm = _tile(S, 512, BF16_ROWS)

    def body(x_ref, g_ref, o_ref):
        xf = x_ref[...]
        r = lax.rsqrt(jnp.mean(xf * xf, axis=-1, keepdims=True) + EPS)
        o_ref[...] = (xf * r * g_ref[l:l + 1, :]).astype(BF16)

    return _pcall(
        body, grid=(S // tm,),
        in_specs=[pl.BlockSpec((tm, D), lambda i: (i, 0)), pl.BlockSpec(g.shape, lambda i: (0, 0))],
        out_specs=pl.BlockSpec((tm, D), lambda i: (i, 0)), out_shape=_sds((S, D), BF16),
        compiler_params=_cp("parallel"), name=name,
    )(x, g)


def _rms_bwd(name, x, g, l, dh, dres, exchange=None):
    S, D = x.shape
    tm = _tile(S, 512, SUBLANES)

    def body(x_ref, g_ref, dh_ref, dr_ref, dx_ref, dg_ref):
        xf = x_ref[...]
        r = lax.rsqrt(jnp.mean(xf * xf, axis=-1, keepdims=True) + EPS)
        xh = xf * r
        d = dh_ref[...]
        dxh = d * g_ref[l:l + 1, :]
        dx_ref[...] = dr_ref[...] + r * (dxh - xh * jnp.mean(dxh * xh, axis=-1, keepdims=True))

        @pl.when(pl.program_id(0) == 0)
        def _():
            dg_ref[...] = jnp.zeros_like(dg_ref)

        dg_ref[...] += _rowsum(d * xh)

    row = pl.BlockSpec((tm, D), lambda i: (i, 0))
    n_i = S // tm
    body, in_specs, out_specs, out_shape, scratch, operands, aliases = _with_exchange(
        exchange, body, [row, pl.BlockSpec(g.shape, lambda i: (0, 0)), row, row],
        [row, pl.BlockSpec((1, D), lambda i: (0, 0))], [_sds((S, D), F32), _sds((1, D), F32)], [], [x, g, dh, dres],
        lambda: pl.program_id(0) == 0, lambda: pl.program_id(0) == n_i - 1)
    return _pcall(
        body, grid=(n_i,), in_specs=in_specs, out_specs=out_specs, out_shape=out_shape, scratch_shapes=scratch,
        input_output_aliases=aliases, compiler_params=_cp("arbitrary"), name=name,
    )(*operands)


def _loss_fwd_bwd(name, y, t):
    S, D = y.shape
    tm = _tile(S, 512, SUBLANES)

    def body(y_ref, t_ref, dy_ref, l_ref):
        e = y_ref[...] - t_ref[...]
        dy_ref[...] = e * (1.0 / D)

        @pl.when(pl.program_id(0) == 0)
        def _():
            l_ref[...] = jnp.zeros_like(l_ref)

        l_ref[...] += 0.5 * jnp.sum(jnp.sum(e * e, axis=-1, keepdims=True) * (1.0 / D), axis=0, keepdims=True)

    row = pl.BlockSpec((tm, D), lambda i: (i, 0))
    return _pcall(
        body, grid=(S // tm,), in_specs=[row, row],
        out_specs=[row, pl.BlockSpec((SUBLANES, LANES), lambda i: (0, 0))],
        out_shape=[_sds((S, D), F32), _sds((SUBLANES, LANES), F32)],
        compiler_params=_cp("arbitrary"), name=name,
    )(y, t)


def _delayed_copies(us, n_rows):
    for s in range(1, SUBLANES):
        us[s, pl.ds(SUBLANES, n_rows - SUBLANES), :] = us[0, pl.ds(SUBLANES - s, n_rows - SUBLANES), :]


def _conv_a(aw_ref, ab_ref, l, us, row0, rows, dg):
    ka = CONV_A_WIDTH
    out = []
    for c0 in range(0, dg, LANES):
        lanes = slice(c0, c0 + LANES)
        acc = ab_ref[l:l + 1, lanes]
        for d in range(ka):
            a, s = divmod(d, SUBLANES)
            acc = acc + aw_ref[l, ka - 1 - d:ka - d, lanes] * us[s, pl.ds(row0 - SUBLANES * a, rows), lanes]
        out.append(acc)
    return jnp.concatenate(out, axis=1)


def _convmix_fwd(name, p, aw, ab, lg, lb, bw, l, exchange=None):
    S, W = p.shape
    dg = W // 5
    tm = _tile(S, 256, HALO_A)
    nb = tm // HALO_A
    ka, kb = CONV_A_WIDTH, CONV_B_WIDTH

    ext = HALO_A + tm
    rc = _tile(tm, ELT_ROWS, BF16_ROWS)

    def body(p_ref, ph_ref, aw_ref, ab_ref, lg_ref, lb_ref, bw_ref, o_ref, us, mext):
        first = pl.program_id(0) == 0
        ph = ph_ref[...]
        pc = p_ref[...]
        us[0, pl.ds(0, HALO_A), :] = jnp.where(first, 0.0, ph[:, 0:dg] * _sig(ph[:, dg:2 * dg]))
        us[0, pl.ds(HALO_A, tm), :] = pc[:, 0:dg] * _sig(pc[:, dg:2 * dg])
        mext[pl.ds(0, HALO_A), :] = jnp.where(first, 0.0, ph[:, 3 * dg:4 * dg] * ph[:, 4 * dg:5 * dg])
        mext[pl.ds(HALO_A, tm), :] = pc[:, 3 * dg:4 * dg] * pc[:, 4 * dg:5 * dg]
        _delayed_copies(us, ext)
        for r0 in range(0, tm, rc):
            rows = pl.ds(r0, rc)
            c = _conv_a(aw_ref, ab_ref, l, us, HALO_A + r0, rc, dg)
            xc = c - jnp.mean(c, axis=-1, keepdims=True)
            ln = xc * lax.rsqrt(jnp.mean(xc * xc, axis=-1, keepdims=True) + EPS) * lg_ref[l:l + 1, :] + lb_ref[l:l + 1, :]
            o_ref[rows, 0:dg] = (ln * _sig(ln)).astype(BF16)
            cb = bw_ref[l, 0:1, :] * mext[pl.ds(HALO_A - (kb - 1) + r0, rc), :]
            for k in range(1, kb):
                cb = cb + bw_ref[l, k:k + 1, :] * mext[pl.ds(HALO_A - (kb - 1) + k + r0, rc), :]
            o_ref[rows, dg:2 * dg] = (p_ref[rows, 2 * dg:3 * dg] * cb).astype(BF16)

    full = lambda a: pl.BlockSpec(a.shape, lambda i: (0,) * a.ndim)
    n_i = S // tm
    body, in_specs, out_specs, out_shape, scratch, operands, aliases = _with_exchange(
        exchange, body,
        [pl.BlockSpec((tm, W), lambda i: (i, 0)), pl.BlockSpec((HALO_A, W), lambda i: (jnp.maximum(i * nb - 1, 0), 0)),
         full(aw), full(ab), full(lg), full(lb), full(bw)],
        [pl.BlockSpec((tm, 2 * dg), lambda i: (i, 0))], [_sds((S, 2 * dg), BF16)],
        [pltpu.VMEM((SUBLANES, ext, dg), F32), pltpu.VMEM((ext, dg), F32)], [p, p, aw, ab, lg, lb, bw],
        lambda: pl.program_id(0) == 0, lambda: pl.program_id(0) == n_i - 1)
    outs = _pcall(
        body, grid=(n_i,), in_specs=in_specs, out_specs=out_specs, out_shape=out_shape, scratch_shapes=scratch,
        input_output_aliases=aliases, compiler_params=_cp("arbitrary" if exchange else "parallel"), name=name,
    )(*operands)
    return outs if exchange else outs[0]


def _convmix_bwd(name, p, dab, aw, ab, lg, lb, bw, l, exchange=None):
    S, W = p.shape
    dg = W // 5
    tm = _tile(S, 256, HALO_A)
    nb = tm // HALO_A
    n_i = S // tm
    ka, kb = CONV_A_WIDTH, CONV_B_WIDTH
    n = tm + HALO_A
    ext = HALO_A + n
    rc = _tile(tm, ELT_ROWS, BF16_ROWS)

    def body(p_ref, pp_ref, pn_ref, d_ref, dn_ref, aw_ref, ab_ref, lg_ref, lb_ref, bw_ref,
             dp_ref, daw_ref, dab_ref, dlg_ref, dlb_ref, dbw_ref, us, mext, dcs, dbext, accw):
        i = pl.program_id(0)
        first, last = i == 0, i == n_i - 1

        @pl.when(first)
        def _():
            for r in (daw_ref, dab_ref, dlg_ref, dlb_ref, dbw_ref):
                r[...] = jnp.zeros_like(r)

        accw[...] = jnp.zeros_like(accw)
        pp, pc, pn = pp_ref[...], p_ref[...], pn_ref[...]
        glu = lambda b: b[:, 0:dg] * _sig(b[:, dg:2 * dg])
        gch = lambda b: b[:, 3 * dg:4 * dg] * b[:, 4 * dg:5 * dg]
        us[0, pl.ds(0, HALO_A), :] = jnp.where(first, 0.0, glu(pp))
        us[0, pl.ds(HALO_A, tm), :] = glu(pc)
        us[0, pl.ds(HALO_A + tm, HALO_A), :] = glu(pn)
        mext[pl.ds(0, HALO_A), :] = jnp.where(first, 0.0, gch(pp))
        mext[pl.ds(HALO_A, tm), :] = gch(pc)
        mext[pl.ds(HALO_A + tm, HALO_A), :] = gch(pn)
        _delayed_copies(us, ext)
        chunks = [(r, rc) for r in range(0, tm, rc)] + [(tm, HALO_A)]
        g_ln = lg_ref[l:l + 1, :]
        zero8 = jnp.zeros((SUBLANES, dg), F32)

        acc_lg = acc_lb = acc_ab = zero8
        for r0, rows in chunks:
            c = _conv_a(aw_ref, ab_ref, l, us, HALO_A + r0, rows, dg)
            xc = c - jnp.mean(c, axis=-1, keepdims=True)
            rstd = lax.rsqrt(jnp.mean(xc * xc, axis=-1, keepdims=True) + EPS)
            chat = xc * rstd
            ln = chat * g_ln + lb_ref[l:l + 1, :]
            s = _sig(ln)
            da = d_ref[pl.ds(r0, rows), 0:dg] if r0 < tm else jnp.where(last, 0.0, dn_ref[:, 0:dg])
            dln = da * (s * (1.0 + ln * (1.0 - s)))
            dlnh = dln * g_ln
            dc = rstd * (dlnh - jnp.mean(dlnh, axis=-1, keepdims=True)
                         - chat * jnp.mean(dlnh * chat, axis=-1, keepdims=True))
            dcs[0, pl.ds(r0, rows), :] = dc
            if r0 < tm:
                acc_lg = acc_lg + _fold(dln * chat)
                acc_lb = acc_lb + _fold(dln)
                acc_ab = acc_ab + _fold(dc)
                for c0 in range(0, dg, LANES):
                    lanes = slice(c0, c0 + LANES)
                    for d in range(ka):
                        a, sh = divmod(d, SUBLANES)
                        k = ka - 1 - d
                        accw[pl.ds(SUBLANES * k, SUBLANES), lanes] += _fold(
                            dc[:, lanes] * us[sh, pl.ds(HALO_A + r0 - SUBLANES * a, rows), lanes])
        dlg_ref[...] += _rowsum(acc_lg)
        dlb_ref[...] += _rowsum(acc_lb)
        dab_ref[...] += _rowsum(acc_ab)
        for k in range(ka):
            daw_ref[k:k + 1, :] += _rowsum(accw[pl.ds(SUBLANES * k, SUBLANES), :])
        for s in range(1, SUBLANES):
            dcs[s, pl.ds(0, n - SUBLANES), :] = dcs[0, pl.ds(s, n - SUBLANES), :]
        for r0 in range(0, tm, rc):
            rows = pl.ds(r0, rc)
            parts = []
            for c0 in range(0, dg, LANES):
                lanes = slice(c0, c0 + LANES)
                acc = aw_ref[l, ka - 1:ka, lanes] * dcs[0, rows, lanes]
                for e in range(1, ka):
                    a, sh = divmod(e, SUBLANES)
                    acc = acc + aw_ref[l, ka - 1 - e:ka - e, lanes] * dcs[sh, pl.ds(r0 + SUBLANES * a, rc), lanes]
                parts.append(acc)
            du = jnp.concatenate(parts, axis=1)
            sg = _sig(p_ref[rows, dg:2 * dg])
            dp_ref[rows, 0:dg] = (du * sg).astype(BF16)
            dp_ref[rows, dg:2 * dg] = (du * p_ref[rows, 0:dg] * sg * (1.0 - sg)).astype(BF16)

        for r0, rows in chunks:
            if r0 < tm:
                dbext[pl.ds(r0, rows), :] = d_ref[pl.ds(r0, rows), dg:2 * dg] * p_ref[pl.ds(r0, rows), 2 * dg:3 * dg]
            else:
                dbext[pl.ds(r0, rows), :] = jnp.where(last, 0.0, dn_ref[:, dg:2 * dg] * pn[:, 2 * dg:3 * dg])
        acc_bw = [zero8] * kb
        for r0 in range(0, tm, rc):
            rows = pl.ds(r0, rc)
            m_k = [mext[pl.ds(HALO_A - (kb - 1) + k + r0, rc), :] for k in range(kb)]
            cb = bw_ref[l, 0:1, :] * m_k[0]
            dm = bw_ref[l, 0:1, :] * dbext[pl.ds(r0 + kb - 1, rc), :]
            for k in range(1, kb):
                cb = cb + bw_ref[l, k:k + 1, :] * m_k[k]
                dm = dm + bw_ref[l, k:k + 1, :] * dbext[pl.ds(r0 + kb - 1 - k, rc), :]
            dcb = dbext[rows, :]
            acc_bw = [acc_bw[k] + _fold(dcb * m_k[k]) for k in range(kb)]
            dp_ref[rows, 2 * dg:3 * dg] = (d_ref[rows, dg:2 * dg] * cb).astype(BF16)
            dp_ref[rows, 3 * dg:4 * dg] = (dm * p_ref[rows, 4 * dg:5 * dg]).astype(BF16)
            dp_ref[rows, 4 * dg:5 * dg] = (dm * p_ref[rows, 3 * dg:4 * dg]).astype(BF16)
        for k in range(kb):
            dbw_ref[k:k + 1, :] += _rowsum(acc_bw[k])

    full = lambda a: pl.BlockSpec(a.shape, lambda i: (0,) * a.ndim)
    prev = lambda i: (jnp.maximum(i * nb - 1, 0), 0)
    nxt = lambda i: (jnp.minimum((i + 1) * nb, S // HALO_A - 1), 0)
    acc = lambda r: pl.BlockSpec((r, dg), lambda i: (0, 0))
    body, in_specs, out_specs, out_shape, scratch, operands, aliases = _with_exchange(
        exchange, body,
        [pl.BlockSpec((tm, W), lambda i: (i, 0)), pl.BlockSpec((HALO_A, W), prev), pl.BlockSpec((HALO_A, W), nxt),
         pl.BlockSpec((tm, 2 * dg), lambda i: (i, 0)), pl.BlockSpec((HALO_A, 2 * dg), nxt),
         full(aw), full(ab), full(lg), full(lb), full(bw)],
        [pl.BlockSpec((tm, W), lambda i: (i, 0)), acc(ka), acc(1), acc(1), acc(1), acc(kb)],
        [_sds((S, W), BF16), _sds((ka, dg), F32), _sds((1, dg), F32), _sds((1, dg), F32), _sds((1, dg), F32),
         _sds((kb, dg), F32)],
        [pltpu.VMEM((SUBLANES, ext, dg), F32), pltpu.VMEM((ext, dg), F32), pltpu.VMEM((SUBLANES, n, dg), F32),
         pltpu.VMEM((n, dg), F32), pltpu.VMEM((SUBLANES * ka, dg), F32)],
        [p, p, p, dab, dab, aw, ab, lg, lb, bw],
        lambda: pl.program_id(0) == 0, lambda: pl.program_id(0) == n_i - 1)
    return _pcall(
        body, grid=(n_i,), in_specs=in_specs, out_specs=out_specs, out_shape=out_shape, scratch_shapes=scratch,
        input_output_aliases=aliases, compiler_params=_cp("arbitrary"), name=name,
    )(*operands)


def _ffn_mid_fwd(name, u2, dww, dwb, l, exchange=None):
    _, S, F = u2.shape
    tm = _tile(S, 256, BF16_ROWS)
    tc = _tile(F, 1408)
    n_f = F // tc
    nb = tm // HALO_S
    kf = FFN_CONV_WIDTH

    def body(u_ref, uh_ref, wg_ref, wv_ref, bg_ref, bv_ref, o_ref, ext):
        first = pl.program_id(1) == 0
        ext[:, pl.ds(0, HALO_S), :] = jnp.where(first, 0.0, uh_ref[...])
        ext[:, pl.ds(HALO_S, tm), :] = u_ref[...]
        rc = _tile(tm, ELT_ROWS, BF16_ROWS)

        def lane_chunk(ci, carry):
            lanes = pl.ds(pl.multiple_of(ci * LANES, LANES), LANES)
            taps = [[w_ref[k:k + 1, lanes] for k in range(kf)] for w_ref in (wg_ref, wv_ref)]
            bias = [b_ref[l:l + 1, lanes] for b_ref in (bg_ref, bv_ref)]
            for r0 in range(0, tm, rc):
                c = []
                for g in range(2):
                    acc = bias[g]
                    for k in range(kf):
                        acc = acc + taps[g][k] * ext[g, pl.ds(HALO_S - (kf - 1) + k + r0, rc), lanes]
                    c.append(acc)
                o_ref[pl.ds(r0, rc), lanes] = (c[0] * _sig(c[0]) * c[1]).astype(BF16)
            return carry

        lax.fori_loop(0, tc // LANES, lane_chunk, 0)

    n_l = dwb.shape[0]
    n_i = S // tm
    body, in_specs, out_specs, out_shape, scratch, operands, aliases = _with_exchange(
        exchange, body,
        [pl.BlockSpec((2, tm, tc), lambda j, i: (0, i, j)),
         pl.BlockSpec((2, HALO_S, tc), lambda j, i: (0, jnp.maximum(i * nb - 1, 0), j)),
         pl.BlockSpec((None, kf, tc), lambda j, i: (l, 0, j)),
         pl.BlockSpec((None, kf, tc), lambda j, i: (l, 0, j + n_f)),
         pl.BlockSpec((n_l, tc), lambda j, i: (0, j)),
         pl.BlockSpec((n_l, tc), lambda j, i: (0, j + n_f))],
        [pl.BlockSpec((tm, tc), lambda j, i: (i, j))], [_sds((S, F), BF16)],
        [pltpu.VMEM((2, HALO_S + tm, tc), F32)], [u2, u2, dww, dww, dwb, dwb],
        lambda: jnp.logical_and(pl.program_id(0) == 0, pl.program_id(1) == 0),
        lambda: jnp.logical_and(pl.program_id(0) == n_f - 1, pl.program_id(1) == n_i - 1))
    sem = "arbitrary" if exchange else "parallel"
    outs = _pcall(
        body, grid=(n_f, n_i), in_specs=in_specs, out_specs=out_specs, out_shape=out_shape, scratch_shapes=scratch,
        input_output_aliases=aliases, compiler_params=_cp(sem, sem), name=name,
    )(*operands)
    return outs if exchange else outs[0]


def _ffn_mid_bwd(name, u2, df, dww, dwb, l, exchange=None):
    _, S, F = u2.shape
    tm = _tile(S, 256, BF16_ROWS)
    tc = _tile(F, 1408)
    n_f = F // tc
    nb = tm // HALO_S
    n_i = S // tm
    kf = FFN_CONV_WIDTH
    n = tm + HALO_S

    def body(u_ref, up_ref, un_ref, df_ref, dfn_ref, wg_ref, wv_ref, bg_ref, bv_ref,
             du_ref, dw_ref, db_ref, uext, dcext):
        i = pl.program_id(1)
        first, last = i == 0, i == n_i - 1

        @pl.when(first)
        def _():
            dw_ref[...] = jnp.zeros_like(dw_ref)
            db_ref[...] = jnp.zeros_like(db_ref)

        uext[:, pl.ds(0, HALO_S), :] = jnp.where(first, 0.0, up_ref[...])
        uext[:, pl.ds(HALO_S, tm), :] = u_ref[...]
        uext[:, pl.ds(HALO_S + tm, HALO_S), :] = un_ref[...]
        rc = _tile(tm, ELT_ROWS, BF16_ROWS)

        def lane_chunk(ci, carry):
            lanes = pl.ds(pl.multiple_of(ci * LANES, LANES), LANES)
            taps = [[w_ref[k:k + 1, lanes] for k in range(kf)] for w_ref in (wg_ref, wv_ref)]
            bias = [b_ref[l:l + 1, lanes] for b_ref in (bg_ref, bv_ref)]
            acc_w = [[jnp.zeros((SUBLANES, LANES), F32) for _ in range(kf)] for _ in range(2)]
            acc_b = [jnp.zeros((SUBLANES, LANES), F32) for _ in range(2)]
            for r0, rows in [(r, rc) for r in range(0, tm, rc)] + [(tm, HALO_S)]:
                shifted = [[uext[g, pl.ds(HALO_S - (kf - 1) + k + r0, rows), lanes] for k in range(kf)] for g in range(2)]
                conv = []
                for g in range(2):
                    acc = bias[g]
                    for k in range(kf):
                        acc = acc + taps[g][k] * shifted[g][k]
                    conv.append(acc)
                cg, cv = conv
                s = _sig(cg)
                dfe = df_ref[pl.ds(r0, rows), lanes] if r0 < tm else jnp.where(last, 0.0, dfn_ref[:, lanes])
                dc = [dfe * cv * (s * (1.0 + cg * (1.0 - s))), dfe * (cg * s)]
                for g in range(2):
                    dcext[g, pl.ds(r0, rows), lanes] = dc[g]
                    if r0 < tm:
                        acc_b[g] = acc_b[g] + _fold(dc[g])
                        for k in range(kf):
                            acc_w[g][k] = acc_w[g][k] + _fold(dc[g] * shifted[g][k])
            for r0 in range(0, tm, rc):
                for g in range(2):
                    du = taps[g][0] * dcext[g, pl.ds(r0 + kf - 1, rc), lanes]
                    for k in range(1, kf):
                        du = du + taps[g][k] * dcext[g, pl.ds(r0 + kf - 1 - k, rc), lanes]
                    du_ref[g, pl.ds(r0, rc), lanes] = du.astype(BF16)
            for g in range(2):
                db_ref[g, :, lanes] += _rowsum(acc_b[g])
                for k in range(kf):
                    dw_ref[g, k:k + 1, lanes] += _rowsum(acc_w[g][k])
            return carry

        lax.fori_loop(0, tc // LANES, lane_chunk, 0)

    n_l = dwb.shape[0]
    prev = lambda j, i: (0, jnp.maximum(i * nb - 1, 0), j)
    nxt = lambda j, i: (0, jnp.minimum((i + 1) * nb, S // HALO_S - 1), j)
    body, in_specs, out_specs, out_shape, scratch, operands, aliases = _with_exchange(
        exchange, body,
        [pl.BlockSpec((2, tm, tc), lambda j, i: (0, i, j)),
         pl.BlockSpec((2, HALO_S, tc), prev), pl.BlockSpec((2, HALO_S, tc), nxt),
         pl.BlockSpec((tm, tc), lambda j, i: (i, j)),
         pl.BlockSpec((HALO_S, tc), lambda j, i: nxt(j, i)[1:]),
         pl.BlockSpec((None, kf, tc), lambda j, i: (l, 0, j)),
         pl.BlockSpec((None, kf, tc), lambda j, i: (l, 0, j + n_f)),
         pl.BlockSpec((n_l, tc), lambda j, i: (0, j)),
         pl.BlockSpec((n_l, tc), lambda j, i: (0, j + n_f))],
        [pl.BlockSpec((2, tm, tc), lambda j, i: (0, i, j)),
         pl.BlockSpec((2, kf, tc), lambda j, i: (0, 0, j)),
         pl.BlockSpec((2, 1, tc), lambda j, i: (0, 0, j))],
        [_sds((2, S, F), BF16), _sds((2, kf, F), F32), _sds((2, 1, F), F32)],
        [pltpu.VMEM((2, HALO_S + n, tc), F32), pltpu.VMEM((2, n, tc), F32)],
        [u2, u2, u2, df, df, dww, dww, dwb, dwb],
        lambda: jnp.logical_and(pl.program_id(0) == 0, pl.program_id(1) == 0),
        lambda: jnp.logical_and(pl.program_id(0) == n_f - 1, pl.program_id(1) == n_i - 1))
    return _pcall(
        body, grid=(n_f, n_i), in_specs=in_specs, out_specs=out_specs, out_shape=out_shape, scratch_shapes=scratch,
        input_output_aliases=aliases, compiler_params=_cp("arbitrary" if exchange else "parallel", "arbitrary"), name=name,
    )(*operands)


def _head_sum_matrix():
    r = lax.broadcasted_iota(jnp.int32, (LANES, LANES), 0) // HEAD_DIM
    c = lax.broadcasted_iota(jnp.int32, (LANES, LANES), 1) // HEAD_DIM
    return (r == c).astype(BF16)


def _head_mean(x, ones):
    return _split_dot(x, ones) * (1.0 / HEAD_DIM)


def _qknorm_fwd(name, qkv, g2):
    S, D3 = qkv.shape
    D = D3 // 3
    tm = _tile(S, 256, BF16_ROWS)
    scale = HEAD_DIM ** -0.5

    def body(q_ref, k_ref, v_ref, g_ref, qo_ref, ko_ref, vo_ref):
        ones = _head_sum_matrix()
        for cc in range(D // LANES):
            sl = slice(cc * LANES, (cc + 1) * LANES)
            for x_ref, o_ref, row, mult in ((q_ref, qo_ref, 0, scale), (k_ref, ko_ref, 1, 1.0)):
                x = x_ref[:, sl]
                r = lax.rsqrt(_head_mean(x * x, ones) + EPS)
                o_ref[:, sl] = ((x * r * g_ref[row:row + 1, :]).astype(BF16) * mult).astype(BF16)
        vo_ref[...] = v_ref[...].astype(BF16)

    col = lambda c: pl.BlockSpec((tm, D), lambda i: (i, c))
    out = pl.BlockSpec((tm, D), lambda i: (i, 0))
    return _pcall(
        body, grid=(S // tm,),
        in_specs=[col(0), col(1), col(2), pl.BlockSpec(g2.shape, lambda i: (0, 0))],
        out_specs=[out, out, out], out_shape=[_sds((S, D), BF16)] * 3,
        compiler_params=_cp("parallel"), name=name,
    )(qkv, qkv, qkv, g2)


def _qknorm_bwd(name, qkv, dq, dk, dv, g2):
    S, D3 = qkv.shape
    D = D3 // 3
    tm = _tile(S, 256, BF16_ROWS)
    scale = HEAD_DIM ** -0.5

    def body(q_ref, k_ref, dq_ref, dk_ref, dv_ref, g_ref, o_ref, dg_ref):
        @pl.when(pl.program_id(0) == 0)
        def _():
            dg_ref[...] = jnp.zeros_like(dg_ref)

        ones = _head_sum_matrix()
        for cc in range(D // LANES):
            sl = slice(cc * LANES, (cc + 1) * LANES)
            for x_ref, d_ref, row, mult, base in ((q_ref, dq_ref, 0, scale, 0), (k_ref, dk_ref, 1, 1.0, D)):
                x = x_ref[:, sl]
                r = lax.rsqrt(_head_mean(x * x, ones) + EPS)
                xh = x * r
                dn = d_ref[:, sl] * mult
                dxh = dn * g_ref[row:row + 1, :]
                dx = r * (dxh - xh * _head_mean(dxh * xh, ones))
                o_ref[:, base + cc * LANES:base + (cc + 1) * LANES] = dx.astype(BF16)
                dg_ref[row:row + 1, :] += _rowsum(dn * xh)
        o_ref[:, 2 * D:3 * D] = dv_ref[...].astype(BF16)

    col = lambda c: pl.BlockSpec((tm, D), lambda i: (i, c))
    row = pl.BlockSpec((tm, D), lambda i: (i, 0))
    return _pcall(
        body, grid=(S // tm,),
        in_specs=[col(0), col(1), row, row, row, pl.BlockSpec(g2.shape, lambda i: (0, 0))],
        out_specs=[pl.BlockSpec((tm, D3), lambda i: (i, 0)), pl.BlockSpec((2, LANES), lambda i: (0, 0))],
        out_shape=[_sds((S, D3), BF16), _sds((2, LANES), F32)],
        compiler_params=_cp("arbitrary"), name=name,
    )(qkv, qkv, dq, dk, dv, g2)


def _attn_consts():
    t = ATTN_BLOCK
    row = lax.broadcasted_iota(jnp.int32, (t, t), 0)
    col = lax.broadcasted_iota(jnp.int32, (t, t), 1)
    lane = lax.broadcasted_iota(jnp.int32, (1, LANES), 1)
    heads = (lane < HEAD_DIM, lane >= HEAD_DIM)
    return row, col, heads


def _split_dot(x, m):
    n = x.shape[0]
    hi = x.astype(BF16)
    lo = (x - hi.astype(F32)).astype(BF16)
    both = jnp.dot(jnp.concatenate([hi, lo], axis=0), m, preferred_element_type=F32)
    return both[:n] + both[n:]


def _log_keep(z):
    return -(jnp.maximum(z, 0.0) + jnp.log(1.0 + jnp.exp(-jnp.abs(z))))


def _stack_heads(a, heads):
    t = ATTN_BLOCK
    zero = jnp.zeros((t, LANES), a.dtype)
    return jnp.concatenate([jnp.where(h, a[s * t:(s + 1) * t], zero) for s in range(a.shape[0] // t) for h in heads], axis=0)


def _side_by_side(a):
    t = ATTN_BLOCK
    return jnp.concatenate([jnp.concatenate([a[2 * s * t:(2 * s + 1) * t], a[(2 * s + 1) * t:(2 * s + 2) * t]], axis=1)
                            for s in range(a.shape[0] // (2 * t))], axis=0)


def _grow(a, rows, cols):
    z = jnp.zeros((rows, cols), F32)
    return z if a is None else jnp.concatenate([z, a], axis=0)


def _attn_fwd(name, qs, kn, vb, exchange=None):
    S, D = qs.shape
    t = ATTN_BLOCK
    tq = ATTN_SUB * t

    def body(q_ref, k_ref, v_ref, o_ref):
        i = pl.program_id(1)
        row, col, heads = _attn_consts()
        after_m = (row > col).astype(BF16)
        causal = col < row
        q_all = _stack_heads(q_ref[...], heads)

        def blocks(specs, r, acc):
            n_rows = q_all.shape[0]
            offs = [pl.multiple_of(j * t, t) for j, _, _ in specs]
            zs = [lax.dot_general(q_all[lo:], k_ref[pl.ds(off, t), :], NT, preferred_element_type=F32)
                  for off, (_, lo, _) in zip(offs, specs)]
            lks = []
            for z, (_, _, mask) in zip(zs, specs):
                lk = _log_keep(z)
                lks.append(lk if mask is None else jnp.where(mask, lk, 0.0))
            cums = [_split_dot(lk, after_m) for lk in lks]
            ws = []
            for z, lk, cum, (_, lo, mask) in zip(zs, lks, cums, specs):
                rows = n_rows - lo
                r = _grow(r, rows - (0 if r is None else r.shape[0]), 1) if r is None or r.shape[0] < rows else r
                w = jnp.exp(z + lk + cum + r)
                ws.append((w if mask is None else jnp.where(mask, w, 0.0)).astype(BF16))
                r = r + jnp.sum(lk, axis=1, keepdims=True)
            acc = jnp.zeros((n_rows // 2, LANES), F32) if acc is None else acc
            for w, off, (_, lo, _) in zip(ws, offs, specs):
                part = jnp.dot(_side_by_side(w), _stack_heads(v_ref[pl.ds(off, t), :], heads), preferred_element_type=F32)
                acc = acc + (part if lo == 0 else _grow(part, lo // 2, LANES))
            return r, acc

        def head(n_more):
            specs = [(ATTN_SUB * i + s, 2 * s * t,
                      jnp.concatenate([causal, causal] + [jnp.ones_like(causal)] * (2 * (ATTN_SUB - 1 - s)), axis=0))
                     for s in reversed(range(ATTN_SUB))]
            specs += [(ATTN_SUB * i - 1 - b, 0, None) for b in range(n_more)]
            return blocks(specs, None, None)

        r, acc = lax.cond(ATTN_SUB * i >= ATTN_MORE, lambda: head(ATTN_MORE), lambda: head(0))

        def cond(c):
            return jnp.logical_and(c[0] >= 0, jnp.max(c[1]) > EXP_UNDERFLOW)

        def step(c):
            r, a = blocks([(c[0], 0, None)], c[1], c[2])
            return c[0] - 1, r, a

        first = jnp.where(ATTN_SUB * i >= ATTN_MORE, ATTN_SUB * i - 1 - ATTN_MORE, ATTN_SUB * i - 1)
        o_ref[...] = lax.while_loop(cond, step, (first, r, acc))[2]

    n_hp = D // LANES
    blk = pl.BlockSpec((tq, LANES), lambda hp, i: (i, hp))
    seq = pl.BlockSpec((S, LANES), lambda hp, i: (0, hp))
    n_i = S // tq
    body, in_specs, out_specs, out_shape, scratch, operands, aliases = _with_exchange(
        exchange, body, [blk, seq, seq], [blk], [_sds((S, D), F32)], [], [qs, kn, vb],
        lambda: jnp.logical_and(pl.program_id(0) == 0, pl.program_id(1) == 0),
        lambda: jnp.logical_and(pl.program_id(0) == n_hp - 1, pl.program_id(1) == n_i - 1))
    outs = _pcall(
        body, grid=(n_hp, n_i), in_specs=in_specs, out_specs=out_specs, out_shape=out_shape, scratch_shapes=scratch,
        input_output_aliases=aliases, compiler_params=_cp("arbitrary" if exchange else "parallel", "arbitrary"), name=name,
    )(*operands)
    return outs if exchange else outs[0]


def _attn_bwd(name, qs, kn, vb, o, do, exchange=None):
    S, D = qs.shape
    t = ATTN_BLOCK
    tq = ATTN_SUB * t

    def body(q_ref, k_ref, v_ref, o_ref, do_ref, dq_ref, dk_ref, dv_ref):
        i = pl.program_id(1)

        @pl.when(i == 0)
        def _():
            dk_ref[...] = jnp.zeros_like(dk_ref)
            dv_ref[...] = jnp.zeros_like(dv_ref)

        row, col, heads = _attn_consts()
        after_m = (row > col).astype(BF16)
        from_m = (row >= col).astype(BF16)
        causal = col < row
        q_all = _stack_heads(q_ref[...], heads)
        dob = do_ref[...].astype(BF16)
        do_all = _stack_heads(dob, heads)
        dsum_all = jnp.sum(_stack_heads(dob.astype(F32) * o_ref[...], heads), axis=1, keepdims=True)

        def blocks(specs, r, es, dq):
            n_rows = q_all.shape[0]
            offs = [pl.multiple_of(j * t, t) for j, _, _ in specs]
            masked = lambda x, mask: x if mask is None else jnp.where(mask, x, 0.0)
            top = lambda a, rows: a if a is not None and a.shape[0] == rows else _grow(a, rows - (0 if a is None else a.shape[0]), 1)
            zs = [lax.dot_general(q_all[lo:], k_ref[pl.ds(off, t), :], NT, preferred_element_type=F32)
                  for off, (_, lo, _) in zip(offs, specs)]
            gs = [lax.dot_general(do_all[lo:], v_ref[pl.ds(off, t), :], NT, preferred_element_type=F32)
                  for off, (_, lo, _) in zip(offs, specs)]
            lks = [masked(_log_keep(z), mask) for z, (_, _, mask) in zip(zs, specs)]
            cums = [_split_dot(lk, after_m) for lk in lks]
            ws, es_blk, sgs = [], [], []
            for z, g, lk, cum, (_, lo, mask) in zip(zs, gs, lks, cums, specs):
                r = top(r, n_rows - lo)
                ls = z + lk
                w = masked(jnp.exp(ls + cum + r), mask)
                ws.append(w.astype(BF16))
                es_blk.append(w * g)
                sgs.append(jnp.exp(ls))
                r = r + jnp.sum(lk, axis=1, keepdims=True)
            cum_es = [_split_dot(e, from_m) for e in es_blk]
            dzs = []
            for e, cum_e, sg, (_, lo, mask) in zip(es_blk, cum_es, sgs, specs):
                es = top(es, n_rows - lo)
                before = dsum_all[lo:] - (es + cum_e)
                dzs.append(masked(e - (e + before) * sg, mask).astype(BF16))
                es = es + jnp.sum(e, axis=1, keepdims=True)
            dq = jnp.zeros((n_rows // 2, LANES), F32) if dq is None else dq
            for dzb, w, off, (_, lo, _) in zip(dzs, ws, offs, specs):
                part = jnp.dot(_side_by_side(dzb), _stack_heads(k_ref[pl.ds(off, t), :], heads), preferred_element_type=F32)
                dq = dq + (part if lo == 0 else _grow(part, lo // 2, LANES))
                dk_ref[pl.ds(off, t), :] += lax.dot_general(dzb, q_all[lo:], TN, preferred_element_type=F32)
                dv_ref[pl.ds(off, t), :] += lax.dot_general(w, do_all[lo:], TN, preferred_element_type=F32)
            return r, es, dq

        def head(n_more):
            specs = [(ATTN_SUB * i + s, 2 * s * t,
                      jnp.concatenate([causal, causal] + [jnp.ones_like(causal)] * (2 * (ATTN_SUB - 1 - s)), axis=0))
                     for s in reversed(range(ATTN_SUB))]
            specs += [(ATTN_SUB * i - 1 - b, 0, None) for b in range(n_more)]
            return blocks(specs, None, None, None)

        r, es, dq = lax.cond(ATTN_SUB * i >= ATTN_MORE, lambda: head(ATTN_MORE), lambda: head(0))

        def cond(c):
            return jnp.logical_and(c[0] >= 0, jnp.max(c[1]) > EXP_UNDERFLOW)

        def step(c):
            r, es, a = blocks([(c[0], 0, None)], c[1], c[2], c[3])
            return c[0] - 1, r, es, a

        first = jnp.where(ATTN_SUB * i >= ATTN_MORE, ATTN_SUB * i - 1 - ATTN_MORE, ATTN_SUB * i - 1)
        dq_ref[...] = lax.while_loop(cond, step, (first, r, es, dq))[3]

    n_hp = D // LANES
    blk = pl.BlockSpec((tq, LANES), lambda hp, i: (i, hp))
    seq = pl.BlockSpec((S, LANES), lambda hp, i: (0, hp))
    n_i = S // tq
    body, in_specs, out_specs, out_shape, scratch, operands, aliases = _with_exchange(
        exchange, body, [blk, seq, seq, blk, blk], [blk, seq, seq], [_sds((S, D), F32)] * 3, [], [qs, kn, vb, o, do],
        lambda: jnp.logical_and(pl.program_id(0) == 0, pl.program_id(1) == 0),
        lambda: jnp.logical_and(pl.program_id(0) == n_hp - 1, pl.program_id(1) == n_i - 1))
    return _pcall(
        body, grid=(n_hp, n_i), in_specs=in_specs, out_specs=out_specs, out_shape=out_shape, scratch_shapes=scratch,
        input_output_aliases=aliases, compiler_params=_cp("arbitrary" if exchange else "parallel", "arbitrary"), name=name,
    )(*operands)


def _adamw(name, w, g, m, v):
    L, R, C = w.shape
    tr = _tile(R, 256, SUBLANES)
    c1 = 1.0 - ADAM_B1 ** ADAM_STEP
    c2 = 1.0 - ADAM_B2 ** ADAM_STEP

    def body(w_ref, g_ref, m_ref, v_ref, d_ref, mo_ref, vo_ref):
        gg = g_ref[...]
        mn = ADAM_B1 * m_ref[...] + (1.0 - ADAM_B1) * gg
        vn = ADAM_B2 * v_ref[...] + (1.0 - ADAM_B2) * (gg * gg)
        d_ref[...] = -ADAM_LR * ((mn / c1) / (jnp.sqrt(vn / c2) + ADAM_EPS) + ADAM_WD * w_ref[...])
        mo_ref[...] = mn
        vo_ref[...] = vn

    blk = pl.BlockSpec((None, tr, C), lambda l, i: (l, i, 0))
    return _pcall(
        body, grid=(L, R // tr), in_specs=[blk] * 4, out_specs=[blk] * 3, out_shape=[_sds(w.shape, F32)] * 3,
        compiler_params=_cp("parallel", "parallel"), name=name,
    )(w, g, m, v)


def _place():
    x, y, c = lax.axis_index("x"), lax.axis_index("y"), lax.axis_index("c")
    chips = [(1 - x, y), (x, 1 - y), (1 - x, 1 - y)]
    return x, y, c, chips


def _place_shard(name, w, j_idx):
    L, R, X = w.shape
    rh = R // 2
    tr = _tile(rh, 256, BF16_ROWS)

    def body(j_ref, w_ref, o_ref):
        o_ref[...] = w_ref[...].astype(BF16)

    return _pcall(
        body,
        grid_spec=pltpu.PrefetchScalarGridSpec(
            num_scalar_prefetch=1, grid=(L, 2, rh // tr),
            in_specs=[pl.BlockSpec((None, None, tr, X), lambda l, h, i, j_ref: (l, h, i, 0))],
            out_specs=pl.BlockSpec((None, None, None, tr, X), lambda l, h, i, j_ref: (l, j_ref[0], h, i, 0))),
        out_shape=_sds((L, N_CHIPS, 2, rh, X), BF16), compiler_params=_cp("parallel", "parallel", "parallel"), name=name,
    )(j_idx, w.reshape(L, 2, rh, X))


def _all_gather_weights(bufs, spans, small_ws):
    n_big, n_small = len(bufs), len(small_ws)
    n_in = n_big + n_small
    layers = [pl.ds(l0, n) for l0, n in spans]

    def body(*refs):
        ins, outs = refs[:n_in], refs[n_in:2 * n_in]
        send_sems, recv_sems, local_sems = refs[2 * n_in:]
        x, y, c, chips = _place()
        j_me = 2 * x + y
        j_of = [2 * cx + cy for cx, cy in chips]
        sibling = (x, y, 1 - c)

        def remote(src, dst, s, to):
            return pltpu.make_async_remote_copy(src_ref=src, dst_ref=dst, send_sem=send_sems.at[s], recv_sem=recv_sems.at[s],
                                                device_id=to, device_id_type=MESH)

        started = []
        for t in range(n_big, n_in):
            loc = pltpu.make_async_copy(ins[t], outs[t].at[:, j_me], local_sems.at[t - n_big])
            loc.start()
            started.append(loc)
        first = []
        for t in range(n_big):
            mine = outs[t].at[layers[t], j_me, c]
            for k in range(3):
                first.append(remote(mine, mine, 6 * t + k, (*chips[k], c)))
        for t in range(n_big, n_in):
            for k in range(3):
                first.append(remote(ins[t], outs[t].at[:, j_me], 6 * n_big + 3 * (t - n_big) + k, (*chips[k], c)))
        for cp in first:
            cp.start()
        passed = []
        for t in range(n_big):
            for k in range(3):
                landed = outs[t].at[layers[t], j_of[k], c]
                remote(landed, landed, 6 * t + k, (*chips[k], c)).wait_recv()
                fwd = remote(landed, landed, 6 * t + 3 + k, sibling)
                fwd.start()
                passed.append(fwd)
        for t in range(n_big):
            for k in range(3):
                other = outs[t].at[layers[t], j_of[k], 1 - c]
                remote(other, other, 6 * t + 3 + k, sibling).wait_recv()
        for t in range(n_big, n_in):
            for k in range(3):
                dst = outs[t].at[:, j_of[k]]
                remote(dst, dst, 6 * n_big + 3 * (t - n_big) + k, (*chips[k], c)).wait_recv()
        for cp in first + passed:
            cp.wait_send()
        for loc in started:
            loc.wait()

    out_shape = [_sds(b.shape, b.dtype) for b in bufs]
    out_shape += [_sds((w.shape[0], N_CHIPS) + w.shape[1:], w.dtype) for w in small_ws]
    n_sem = 6 * n_big + 3 * n_small
    outs = _pcall(
        body, in_specs=[ANY] * n_in, out_specs=[ANY] * n_in, out_shape=out_shape,
        input_output_aliases={t: t for t in range(n_big)},
        scratch_shapes=[pltpu.SemaphoreType.DMA((n_sem,)), pltpu.SemaphoreType.DMA((n_sem,)), pltpu.SemaphoreType.DMA((n_small,))],
        name="all_gather_weights",
    )(*bufs, *small_ws)
    return outs[:n_big], outs[n_big:]


class _Exchange:
    def __init__(self, operands, out_shapes, n_sems, copies, in_place=False):
        self.operands, self.out_shapes, self.n_sems, self.copies = list(operands), list(out_shapes), n_sems, copies
        self.aliases = {t: t for t in range(len(self.operands))} if in_place else {}

    @property
    def scratch(self):
        return [pltpu.SemaphoreType.DMA((self.n_sems,)), pltpu.SemaphoreType.DMA((self.n_sems,))]

    def split(self, refs):
        n_in, n_out = len(self.operands), len(self.out_shapes)
        return refs[:n_in], refs[n_in:n_in + n_out]

    def start(self, ins, outs, sems):
        for cp in self.copies(ins, outs, *sems):
            cp.start()

    def wait(self, ins, outs, sems):
        for cp in self.copies(ins, outs, *sems):
            cp.wait()


def _run_exchange(name, ex):
    n_in, n_out = len(ex.operands), len(ex.out_shapes)

    def body(*refs):
        ins, outs, sems = refs[:n_in], refs[n_in:n_in + n_out], refs[n_in + n_out:]
        ex.start(ins, outs, sems)
        ex.wait(ins, outs, sems)

    return _pcall(body, in_specs=[ANY] * n_in, out_specs=[ANY] * n_out, out_shape=ex.out_shapes, scratch_shapes=ex.scratch,
                  input_output_aliases=ex.aliases, name=name)(*ex.operands)


def _gather_chips_exchange(bufs, spans):
    def copies(ins, outs, send_sems, recv_sems):
        x, y, c, chips = _place()
        cps = []
        for t, (l0, n) in enumerate(spans):
            mine = outs[t].at[pl.ds(l0, n), 2 * x + y, c]
            cps += [pltpu.make_async_remote_copy(src_ref=mine, dst_ref=mine, send_sem=send_sems.at[3 * t + k],
                                                 recv_sem=recv_sems.at[3 * t + k], device_id=(cx, cy, c), device_id_type=MESH)
                    for k, (cx, cy) in enumerate(chips)]
        return cps

    return _Exchange(bufs, [_sds(b.shape, b.dtype) for b in bufs], 3 * len(bufs), copies, in_place=True)


def _gather_cores_exchange(bufs, spans):
    def copies(ins, outs, send_sems, recv_sems):
        x, y, c, chips = _place()
        cps = []
        for t, (l0, n) in enumerate(spans):
            for k, (cx, cy) in enumerate(chips):
                part = outs[t].at[pl.ds(l0, n), 2 * cx + cy, c]
                cps.append(pltpu.make_async_remote_copy(src_ref=part, dst_ref=part, send_sem=send_sems.at[3 * t + k],
                                                        recv_sem=recv_sems.at[3 * t + k], device_id=(x, y, 1 - c),
                                                        device_id_type=MESH))
        return cps

    return _Exchange(bufs, [_sds(b.shape, b.dtype) for b in bufs], 3 * len(bufs), copies, in_place=True)


def _core_halves_exchange(grads, spans):
    def copies(ins, outs, send_sems, recv_sems):
        x, y, c, _ = _place()
        return [pltpu.make_async_remote_copy(src_ref=ins[t].at[pl.ds(l0, n), :, 1 - c], dst_ref=outs[t],
                                             send_sem=send_sems.at[t], recv_sem=recv_sems.at[t], device_id=(x, y, 1 - c),
                                             device_id_type=MESH) for t, (l0, n) in enumerate(spans)]

    shapes = [_sds((n, g.shape[1], g.shape[3], g.shape[4]), F32) for g, (_, n) in zip(grads, spans)]
    return _Exchange(grads, shapes, len(grads), copies)


def _add_core_halves(name, g, a, c_idx, l0):
    _, nj, _, rh, X = g.shape
    L = a.shape[0]
    tr = _tile(rh, 256, BF16_ROWS)

    def body(c_ref, g_ref, a_ref, o_ref, ob_ref):
        s = g_ref[...] + a_ref[...]
        o_ref[...] = s
        ob_ref[...] = s.astype(BF16)

    blk = pl.BlockSpec((None, None, tr, X), lambda l, j, i, c_ref: (l, j, i, 0))
    return _pcall(
        body,
        grid_spec=pltpu.PrefetchScalarGridSpec(
            num_scalar_prefetch=1, grid=(L, nj, rh // tr),
            in_specs=[pl.BlockSpec((None, None, None, tr, X), lambda l, j, i, c_ref: (l + l0, j, c_ref[0], i, 0)), blk],
            out_specs=[blk, blk]),
        out_shape=[_sds((L, nj, rh, X), F32), _sds((L, nj, rh, X), BF16)],
        compiler_params=_cp("parallel", "parallel", "parallel"), name=name,
    )(c_idx, g, a)


def _chip_shards_exchange(parts):
    def copies(ins, outs, send_sems, recv_sems):
        x, y, c, chips = _place()
        return [pltpu.make_async_remote_copy(
            src_ref=ins[t].at[:, 2 * cx + cy], dst_ref=outs[t].at[k], send_sem=send_sems.at[3 * t + k],
            recv_sem=recv_sems.at[3 * t + k], device_id=(cx, cy, c), device_id_type=MESH)
            for t in range(len(parts)) for k, (cx, cy) in enumerate(chips)]

    shapes = [_sds((3, p.shape[0], p.shape[2], p.shape[3]), p.dtype) for p in parts]
    return _Exchange(parts, shapes, 3 * len(parts), copies)


def _add_chip_shards(name, p, b, jc_idx, l0, n_layers, buf):
    n, _, rh, X = p.shape
    tr = _tile(rh, 256, BF16_ROWS)

    def body(jc_ref, p_ref, b_ref, *rest):
        rest[-1][...] = ((p_ref[...] + b_ref[0].astype(F32)) + b_ref[1].astype(F32)) + b_ref[2].astype(F32)

    in_specs = [pl.BlockSpec((None, None, tr, X), lambda l, i, jc: (l, jc[0], i, 0)),
                pl.BlockSpec((3, None, tr, X), lambda l, i, jc: (0, l, i, 0))]
    operands = [jc_idx, p, b]
    if buf is not None:
        in_specs.append(ANY)
        operands.append(buf)
    return _pcall(
        body,
        grid_spec=pltpu.PrefetchScalarGridSpec(
            num_scalar_prefetch=1, grid=(n, rh // tr), in_specs=in_specs,
            out_specs=pl.BlockSpec((None, None, tr, X), lambda l, i, jc: (l + l0, jc[1], i, 0))),
        out_shape=_sds((n_layers, 2, rh, X), F32), input_output_aliases={3: 0} if buf is not None else {},
        compiler_params=_cp("parallel", "parallel"), name=name,
    )(*operands)


def _join_core_halves(bufs):
    n = len(bufs)

    def body(*refs):
        outs = refs[n:2 * n]
        send_sems, recv_sems = refs[2 * n:]
        x, y, c, _ = _place()
        cps = [pltpu.make_async_remote_copy(src_ref=outs[t].at[:, c], dst_ref=outs[t].at[:, c], send_sem=send_sems.at[t],
                                            recv_sem=recv_sems.at[t], device_id=(x, y, 1 - c), device_id_type=MESH)
               for t in range(n)]
        for cp in cps:
            cp.start()
        for t in range(n):
            pltpu.make_async_remote_copy(src_ref=outs[t].at[:, c], dst_ref=outs[t].at[:, 1 - c], send_sem=send_sems.at[t],
                                         recv_sem=recv_sems.at[t], device_id=(x, y, 1 - c), device_id_type=MESH).wait()

    outs = _pcall(
        body, in_specs=[ANY] * n, out_specs=[ANY] * n, out_shape=[_sds(b.shape, F32) for b in bufs],
        input_output_aliases={t: t for t in range(n)},
        scratch_shapes=[pltpu.SemaphoreType.DMA((n,)), pltpu.SemaphoreType.DMA((n,))],
        name="grad_join_core_halves",
    )(*bufs)
    return [o.reshape(o.shape[0], 2 * o.shape[2], o.shape[3]) for o in outs]


def _all_reduce_small(packed):
    R, C = packed.shape

    def body(x_ref, o_ref, slots, send_sems, recv_sems):
        x, y, c, _ = _place()
        me = 4 * x + 2 * y + c
        slots[me] = x_ref[...]
        cps = []
        for d in range(N_DEV):
            to = (d // 4, (d // 2) % 2, d % 2)
            cp = pltpu.make_async_remote_copy(src_ref=x_ref, dst_ref=slots.at[me], send_sem=send_sems.at[d],
                                              recv_sem=recv_sems.at[me], device_id=to, device_id_type=MESH)
            cps.append(cp)

            @pl.when(d != me)
            def _():
                cp.start()

        for d in range(N_DEV):
            @pl.when(d != me)
            def _():
                pltpu.make_async_remote_copy(src_ref=x_ref, dst_ref=slots.at[d], send_sem=send_sems.at[d],
                                             recv_sem=recv_sems.at[d], device_id=(x, y, c), device_id_type=MESH).wait_recv()
                cps[d].wait_send()

        acc = slots[0]
        for d in range(1, N_DEV):
            acc = acc + slots[d]
        o_ref[...] = acc

    vm = pl.BlockSpec(memory_space=pltpu.VMEM)
    return _pcall(
        body, in_specs=[vm], out_specs=vm, out_shape=_sds((R, C), F32),
        scratch_shapes=[pltpu.VMEM((N_DEV, R, C), F32), pltpu.SemaphoreType.DMA((N_DEV,)), pltpu.SemaphoreType.DMA((N_DEV,))],
        compiler_params=pltpu.CompilerParams(vmem_limit_bytes=VMEM_LIMIT_BYTES), name="all_reduce_small",
    )(packed)


PACK = SUBLANES * LANES


def _pack(arrays):
    flat = []
    for a in arrays:
        v = a.reshape(-1)
        flat.append(jnp.pad(v, (0, (-v.shape[0]) % PACK)))
    return jnp.concatenate(flat).reshape(-1, LANES)


def _unpack(packed, shapes):
    flat = packed.reshape(-1)
    out, pos = [], 0
    for s in shapes:
        n = 1
        for d in s:
            n *= d
        out.append(flat[pos:pos + n].reshape(s))
        pos += n + (-n) % PACK
    return out


def kernel(x, mix_norm_g, ffn_norm_g, conv_w_in, conv_a_dw_w, conv_a_dw_b, conv_a_ln_g, conv_a_ln_b, conv_b_dw_w, conv_w_out, attn_w_qkv, attn_q_g, attn_k_g, attn_w_o, ffn_w_up, ffn_dw_w, ffn_dw_b, ffn_w_down, loss_target, m_mix_norm_g, m_ffn_norm_g, m_conv_w_in, m_conv_a_dw_w, m_conv_a_dw_b, m_conv_a_ln_g, m_conv_a_ln_b, m_conv_b_dw_w, m_conv_w_out, m_attn_w_qkv, m_attn_q_g, m_attn_k_g, m_attn_w_o, m_ffn_w_up, m_ffn_dw_w, m_ffn_dw_b, m_ffn_w_down, v_mix_norm_g, v_ffn_norm_g, v_conv_w_in, v_conv_a_dw_w, v_conv_a_dw_b, v_conv_a_ln_g, v_conv_a_ln_b, v_conv_b_dw_w, v_conv_w_out, v_attn_w_qkv, v_attn_q_g, v_attn_k_g, v_attn_w_o, v_ffn_w_up, v_ffn_dw_w, v_ffn_dw_b, v_ffn_w_down):
    depth = mix_norm_g.shape[0]
    n_even, n_odd = conv_w_in.shape[0], attn_w_qkv.shape[0]
    S, D = x.shape[1], x.shape[2]
    dg = D // 2
    x0 = x.reshape(S, D)
    target = loss_target.reshape(S, D)
    j_me = 2 * lax.axis_index("x") + lax.axis_index("y")
    c_me = lax.axis_index("c")
    j_idx = j_me.astype(jnp.int32).reshape(1)
    c_idx = c_me.astype(jnp.int32).reshape(1)

    col_names = ["conv_w_in", "attn_w_qkv", "ffn_w_up"]
    row_names = ["conv_w_out", "attn_w_o", "ffn_w_down"]
    local = dict(conv_w_in=conv_w_in, attn_w_qkv=attn_w_qkv, ffn_w_up=ffn_w_up, conv_w_out=conv_w_out, attn_w_o=attn_w_o,
                 ffn_w_down=ffn_w_down)
    gbuf = {n: _place_shard(f"place_{n}", local[n], j_idx) for n in col_names + row_names}

    def weights_of(layer):
        mixer = ("conv_w_in", "conv_w_out") if layer % 2 == 0 else ("attn_w_qkv", "attn_w_o")
        return {mixer[0]: (layer // 2, 1), mixer[1]: (layer // 2, 1), "ffn_w_up": (layer, 1), "ffn_w_down": (layer, 1)}

    def w_col(n):
        return gbuf[n].reshape(gbuf[n].shape[0], N_CHIPS, -1, gbuf[n].shape[4])

    def w_row(n):
        return gbuf[n].reshape(gbuf[n].shape[0], -1, gbuf[n].shape[4])

    def carry(make_exchange, layer):
        if layer + 1 == depth:
            return None, []
        names = list(weights_of(layer + 1))
        return make_exchange([gbuf[n] for n in names], list(weights_of(layer + 1).values())), names

    first = weights_of(0)
    outs, (a_dw, b_dw, f_dw) = _all_gather_weights([gbuf[n] for n in first], list(first.values()),
                                                   [conv_a_dw_w, conv_b_dw_w, ffn_dw_w])
    gbuf.update(zip(first, outs))
    unshard = lambda a: jnp.moveaxis(a, 1, 2).reshape(a.shape[0], a.shape[2], N_CHIPS * a.shape[3])
    a_dw, b_dw, f_dw = unshard(a_dw), unshard(b_dw), unshard(f_dw)
    qk_gain = [jnp.stack([jnp.tile(attn_q_g[i], LANES // HEAD_DIM), jnp.tile(attn_k_g[i], LANES // HEAD_DIM)])
               for i in range(n_odd)]

    saved = []
    xc = x0
    for layer in range(depth):
        i = layer // 2
        tag = f"l{layer}"
        s = {"x_in": xc}
        h = _rms_fwd(f"rms_mix_fwd_{tag}", xc, mix_norm_g, layer)
        s["h"] = h
        ex, names = carry(_gather_chips_exchange, layer)
        if layer % 2 == 0:
            p = _mm_fwd(f"conv_in_fwd_{tag}", h, w_col("conv_w_in"), i, colshard=True)
            ab = _convmix_fwd(f"convmix_fwd_{tag}", p, a_dw, conv_a_dw_b, conv_a_ln_g, conv_a_ln_b, b_dw, i, ex)
            if ex:
                ab, *new = ab
                gbuf.update(zip(names, new))
            xm = _mm_fwd(f"conv_out_fwd_{tag}", ab, w_row("conv_w_out"), i, colshard=False, res=xc)
            s.update(p=p, ab=ab)
        else:
            qkv = _mm_fwd(f"attn_qkv_fwd_{tag}", h, w_col("attn_w_qkv"), i, colshard=True)
            qs, kn, vb = _qknorm_fwd(f"qknorm_fwd_{tag}", qkv, qk_gain[i])
            o = _attn_fwd(f"attn_fwd_{tag}", qs, kn, vb, ex)
            if ex:
                o, *new = o
                gbuf.update(zip(names, new))
            xm = _mm_fwd(f"attn_out_fwd_{tag}", o, w_row("attn_w_o"), i, colshard=False, res=xc)
            s.update(qkv=qkv, qs=qs, kn=kn, vb=vb, o=o)
        s["x_mid"] = xm
        h2 = _rms_fwd(f"rms_ffn_fwd_{tag}", xm, ffn_norm_g, layer)
        u2 = _mm_fwd(f"ffn_up_fwd_{tag}", h2, w_col("ffn_w_up"), layer, colshard=True, out_split=2)
        ex, names = carry(_gather_cores_exchange, layer)
        f = _ffn_mid_fwd(f"ffn_mid_fwd_{tag}", u2, f_dw, ffn_dw_b, layer, ex)
        if ex:
            f, *new = f
            gbuf.update(zip(names, new))
        xc = _mm_fwd(f"ffn_down_fwd_{tag}", f, w_row("ffn_w_down"), layer, colshard=False, res=xm)
        s.update(h2=h2, u2=u2, f=f)
        saved.append(s)

    dx, loss_tile = _loss_fwd_bwd("loss", xc, target)

    w_in, w_qkv, w_up = w_col("conv_w_in"), w_col("attn_w_qkv"), w_col("ffn_w_up")
    w_out, w_o, w_down = w_row("conv_w_out"), w_row("attn_w_o"), w_row("ffn_w_down")
    g_up = g_down = g_in = g_out = g_qkv = g_o = None
    big_names = col_names + row_names

    def halves_view(n, g):
        if n in col_names:
            return g.reshape(g.shape[0], N_CHIPS, 2, g.shape[2] // 2, g.shape[3])
        return g.reshape(g.shape[0], N_CHIPS, 2, g.shape[1] // (2 * N_CHIPS), g.shape[2])

    ffn_of_0 = {"ffn_w_up": (0, 1), "ffn_w_down": (0, 1)}
    mixer_of_0 = {"conv_w_in": (0, 1), "conv_w_out": (0, 1)}
    summed_parts = {n: [] for n in big_names}

    def stacks():
        return {"conv_w_in": g_in, "attn_w_qkv": g_qkv, "ffn_w_up": g_up, "conv_w_out": g_out, "attn_w_o": g_o,
                "ffn_w_down": g_down}

    def core_exchange(group):
        return _core_halves_exchange([halves_view(n, stacks()[n]) for n in group], list(group.values()))

    def chip_exchange(tag, arrived):
        sums, parts = [], []
        for group, from_sibling in arrived:
            for n, a in zip(group, from_sibling):
                f32_sum, bf16_sum = _add_core_halves(f"grad_add_core_{n}_{tag}_{group[n][0]}", halves_view(n, stacks()[n]), a,
                                                     c_idx, group[n][0])
                sums.append((n, group[n][0], f32_sum))
                parts.append(bf16_sum)
        return _chip_shards_exchange(parts), sums

    def record(sums, from_chips):
        for (n, l0, f32_sum), b in zip(sums, from_chips):
            summed_parts[n].append((l0, f32_sum, b))

    d_mix_g, d_ffn_g = [None] * depth, [None] * depth
    d_ffn_dw_w, d_ffn_dw_b = [None] * depth, [None] * depth
    d_a_dw_w, d_a_dw_b, d_a_ln_g, d_a_ln_b, d_b_dw_w = ([None] * n_even for _ in range(5))
    d_q_g, d_k_g = [None] * n_odd, [None] * n_odd
    for layer in reversed(range(depth)):
        i = layer // 2
        tag = f"l{layer}"
        s = saved[layer]
        df = _mm_dgrad(f"ffn_down_dgrad_{tag}", dx, w_down, layer, colshard=False)
        g_down = _mm_wgrad(f"ffn_down_wgrad_{tag}", s["f"], dx, layer, depth, g_down, colshard=False)
        above = weights_of(layer + 1) if layer + 1 < depth else None
        arrived = []
        du2, dww, dwb, *from_sibling = _ffn_mid_bwd(f"ffn_mid_bwd_{tag}", s["u2"], df, f_dw, ffn_dw_b, layer,
                                                    core_exchange(above) if above else None)
        if above:
            arrived.append((above, from_sibling))
        d_ffn_dw_w[layer] = jnp.moveaxis(dww, 0, 1).reshape(FFN_CONV_WIDTH, -1)
        d_ffn_dw_b[layer] = dwb.reshape(-1)
        dh2 = _mm_dgrad(f"ffn_up_dgrad_{tag}", du2, w_up, layer, colshard=True)
        g_up = _mm_wgrad(f"ffn_up_wgrad_{tag}", s["h2"], du2, layer, depth, g_up, colshard=True)
        dx, dg_, *from_sibling = _rms_bwd(f"rms_ffn_bwd_{tag}", s["x_mid"], ffn_norm_g, layer, dh2, dx,
                                          core_exchange(ffn_of_0) if layer == 0 else None)
        if layer == 0:
            arrived.append((ffn_of_0, from_sibling))
        d_ffn_g[layer] = dg_.reshape(-1)
        if layer % 2 == 0:
            dab = _mm_dgrad(f"conv_out_dgrad_{tag}", dx, w_out, i, colshard=False)
            g_out = _mm_wgrad(f"conv_out_wgrad_{tag}", s["ab"], dx, i, n_even, g_out, colshard=False)
            chip_ex, sums = chip_exchange(tag, arrived) if arrived else (None, [])
            dp, daw, dab_b, dlg, dlb, dbw, *from_chips = _convmix_bwd(
                f"convmix_bwd_{tag}", s["p"], dab, a_dw, conv_a_dw_b, conv_a_ln_g, conv_a_ln_b, b_dw, i, chip_ex)
            record(sums, from_chips)
            d_a_dw_w[i], d_a_dw_b[i], d_a_ln_g[i], d_a_ln_b[i], d_b_dw_w[i] = (
                daw, dab_b.reshape(-1), dlg.reshape(-1), dlb.reshape(-1), dbw)
            dh = _mm_dgrad(f"conv_in_dgrad_{tag}", dp, w_in, i, colshard=True)
            g_in = _mm_wgrad(f"conv_in_wgrad_{tag}", s["h"], dp, i, n_even, g_in, colshard=True)
        else:
            do = _mm_dgrad(f"attn_out_dgrad_{tag}", dx, w_o, i, colshard=False)
            g_o = _mm_wgrad(f"attn_out_wgrad_{tag}", s["o"], dx, i, n_odd, g_o, colshard=False)
            chip_ex, sums = chip_exchange(tag, arrived) if arrived else (None, [])
            dq, dk, dv, *from_chips = _attn_bwd(f"attn_bwd_{tag}", s["qs"], s["kn"], s["vb"], s["o"], do, chip_ex)
            record(sums, from_chips)
            dqkv, dgain = _qknorm_bwd(f"qknorm_bwd_{tag}", s["qkv"], dq, dk, dv, qk_gain[i])
            d_q_g[i] = dgain[0, :HEAD_DIM] + dgain[0, HEAD_DIM:]
            d_k_g[i] = dgain[1, :HEAD_DIM] + dgain[1, HEAD_DIM:]
            dh = _mm_dgrad(f"attn_qkv_dgrad_{tag}", dqkv, w_qkv, i, colshard=True)
            g_qkv = _mm_wgrad(f"attn_qkv_wgrad_{tag}", s["h"], dqkv, i, n_odd, g_qkv, colshard=True)
        dx, dg_ = _rms_bwd(f"rms_mix_bwd_{tag}", s["x_in"], mix_norm_g, layer, dh, dx)
        d_mix_g[layer] = dg_.reshape(-1)
    grad_x = dx.reshape(1, S, D)

    small = {
        "mix_norm_g": jnp.stack(d_mix_g), "ffn_norm_g": jnp.stack(d_ffn_g),
        "conv_a_dw_w": jnp.stack(d_a_dw_w), "conv_a_dw_b": jnp.stack(d_a_dw_b),
        "conv_a_ln_g": jnp.stack(d_a_ln_g), "conv_a_ln_b": jnp.stack(d_a_ln_b),
        "conv_b_dw_w": jnp.stack(d_b_dw_w), "attn_q_g": jnp.stack(d_q_g), "attn_k_g": jnp.stack(d_k_g),
        "ffn_dw_w": jnp.stack(d_ffn_dw_w), "ffn_dw_b": jnp.stack(d_ffn_dw_b),
    }
    small_names = list(small)
    summed = _all_reduce_small(_pack([loss_tile] + [small[n] for n in small_names]))
    parts = _unpack(summed, [loss_tile.shape] + [small[n].shape for n in small_names])
    loss = parts[0][0, 0]
    small_g = dict(zip(small_names, parts[1:]))
    for n in ("conv_a_dw_w", "conv_b_dw_w", "ffn_dw_w"):
        cs = small_g[n].shape[2] // N_CHIPS
        small_g[n] = lax.dynamic_slice_in_dim(small_g[n], j_me * cs, cs, axis=2)

    from_sibling = _run_exchange("grad_exchange_core_halves", core_exchange(mixer_of_0))
    chip_ex, sums = chip_exchange("last", [(mixer_of_0, from_sibling)])
    record(sums, _run_exchange("grad_exchange_chip_shards", chip_ex))
    jc_idx = jnp.concatenate([j_idx, c_idx])
    totals = {}
    for n in big_names:
        total = None
        for l0, p, b in summed_parts[n]:
            total = _add_chip_shards(f"grad_add_chips_{n}_{l0}", p, b, jc_idx, l0, stacks()[n].shape[0], total)
        totals[n] = total
    big_g = dict(zip(big_names, _join_core_halves([totals[n] for n in big_names])))

    weights = dict(mix_norm_g=mix_norm_g, ffn_norm_g=ffn_norm_g, conv_w_in=conv_w_in, conv_a_dw_w=conv_a_dw_w, conv_a_dw_b=conv_a_dw_b, conv_a_ln_g=conv_a_ln_g, conv_a_ln_b=conv_a_ln_b, conv_b_dw_w=conv_b_dw_w, conv_w_out=conv_w_out, attn_w_qkv=attn_w_qkv, attn_q_g=attn_q_g, attn_k_g=attn_k_g, attn_w_o=attn_w_o, ffn_w_up=ffn_w_up, ffn_dw_w=ffn_dw_w, ffn_dw_b=ffn_dw_b, ffn_w_down=ffn_w_down)
    m_in = dict(mix_norm_g=m_mix_norm_g, ffn_norm_g=m_ffn_norm_g, conv_w_in=m_conv_w_in, conv_a_dw_w=m_conv_a_dw_w, conv_a_dw_b=m_conv_a_dw_b, conv_a_ln_g=m_conv_a_ln_g, conv_a_ln_b=m_conv_a_ln_b, conv_b_dw_w=m_conv_b_dw_w, conv_w_out=m_conv_w_out, attn_w_qkv=m_attn_w_qkv, attn_q_g=m_attn_q_g, attn_k_g=m_attn_k_g, attn_w_o=m_attn_w_o, ffn_w_up=m_ffn_w_up, ffn_dw_w=m_ffn_dw_w, ffn_dw_b=m_ffn_dw_b, ffn_w_down=m_ffn_w_down)
    v_in = dict(mix_norm_g=v_mix_norm_g, ffn_norm_g=v_ffn_norm_g, conv_w_in=v_conv_w_in, conv_a_dw_w=v_conv_a_dw_w, conv_a_dw_b=v_conv_a_dw_b, conv_a_ln_g=v_conv_a_ln_g, conv_a_ln_b=v_conv_a_ln_b, conv_b_dw_w=v_conv_b_dw_w, conv_w_out=v_conv_w_out, attn_w_qkv=v_attn_w_qkv, attn_q_g=v_attn_q_g, attn_k_g=v_attn_k_g, attn_w_o=v_attn_w_o, ffn_w_up=v_ffn_w_up, ffn_dw_w=v_ffn_dw_w, ffn_dw_b=v_ffn_dw_b, ffn_w_down=v_ffn_w_down)
    order = list(weights)
    grads, delta, new_m, new_v = {}, {}, {}, {}
    for n in big_names:
        grads[n] = big_g[n]
        delta[n], new_m[n], new_v[n] = _adamw(f"adamw_{n}", weights[n], big_g[n], m_in[n], v_in[n])
    shapes = [weights[n].shape for n in small_names]
    packed = [_pack([d[n] for n in small_names]) for d in (weights, small_g, m_in, v_in)]
    upd = _adamw("adamw_small", *[p[None] for p in packed])
    for out, res in zip((delta, new_m, new_v), upd):
        out.update(zip(small_names, _unpack(res[0], shapes)))
    grads.update({n: small_g[n].reshape(weights[n].shape) for n in small_names})
    return (loss, grad_x, *[grads[n] for n in order], *[delta[n] for n in order], *[new_m[n] for n in order],
            *[new_v[n] for n in order])
```

```python
import jax
import jax.numpy as jnp
from jax import lax
from jax.experimental import pallas as pl
from jax.experimental.pallas import tpu as pltpu

F32 = jnp.float32
BF16 = jnp.bfloat16
EPS = 1e-6
CONV_A_WIDTH = 31
CONV_B_WIDTH = 3
FFN_CONV_WIDTH = 3
HEAD_DIM = 64
ADAM_LR = 0.001
ADAM_B1 = 0.9
ADAM_B2 = 0.999
ADAM_EPS = 1e-08
ADAM_WD = 0.01
ADAM_STEP = 10

LANES = 128
SUBLANES = 8
BF16_ROWS = 16
V7X_VMEM_BYTES = 64 * 1024 * 1024
VMEM_LIMIT_BYTES = V7X_VMEM_BYTES * 3 // 4
MM_VMEM_BUDGET = VMEM_LIMIT_BYTES * 4 // 5
MM_ROWS = 1024
N_CHIPS = 4
N_DEV = 8
HALO_A = 32
HALO_S = 8
ELT_ROWS = 64
ATTN_BLOCK = 128
ATTN_SUB = 2
ATTN_MORE = 2
EXP_UNDERFLOW = -104.0
MESH = pl.DeviceIdType.MESH
ANY = pl.BlockSpec(memory_space=pl.ANY)
NT = (((1,), (1,)), ((), ()))
NN = (((1,), (0,)), ((), ()))
TN = (((0,), (0,)), ((), ()))


def _pcall(body, **kw):
    return pl.pallas_call(body, **kw)


def _cp(*sem):
    return pltpu.CompilerParams(dimension_semantics=sem, vmem_limit_bytes=VMEM_LIMIT_BYTES)


def _sds(shape, dtype):
    return jax.ShapeDtypeStruct(tuple(shape), dtype)


def _tile(n, cap, align=LANES):
    if n <= cap:
        return n
    for t in range(cap - cap % align, 0, -align):
        if n % t == 0:
            return t
    return n


def _sig(x):
    return 0.5 * jnp.tanh(0.5 * x) + 0.5


def _rowsum(x):
    return jnp.sum(x, axis=0, keepdims=True)


def _fold(x):
    acc = x[0:SUBLANES]
    for r in range(SUBLANES, x.shape[0], SUBLANES):
        acc = acc + x[r:r + SUBLANES]
    return acc


def _with_exchange(ex, body, in_specs, out_specs, out_shape, scratch, operands, first, last):
    if ex is None:
        return body, in_specs, out_specs, out_shape, scratch, operands, {}
    n_in, n_out, n_scr = len(in_specs), len(out_specs), len(scratch)
    e_in, e_out = len(ex.operands), len(ex.out_shapes)

    def hosted(*refs):
        refs = list(refs)
        ins, refs = refs[:n_in], refs[n_in:]
        e_ins, refs = refs[:e_in], refs[e_in:]
        outs, refs = refs[:n_out], refs[n_out:]
        e_outs, refs = refs[:e_out], refs[e_out:]
        scr, sems = refs[:n_scr], refs[n_scr:]

        @pl.when(first())
        def _():
            ex.start(e_ins, e_outs, sems)

        body(*ins, *outs, *scr)

        @pl.when(last())
        def _():
            ex.wait(e_ins, e_outs, sems)

    return (hosted, in_specs + [ANY] * e_in, out_specs + [ANY] * e_out, out_shape + ex.out_shapes, scratch + ex.scratch,
            operands + ex.operands, {n_in + i: n_out + o for i, o in ex.aliases.items()})


def _mm_call(name, dn, operands, in_specs, out_shape, out_spec, grid, nk, acc_shape, has_res, has_alias):
    def body(*refs):
        a_ref, b_ref = refs[0], refs[1]
        pos = 2
        res_ref = refs[pos] if has_res else None
        pos += int(has_res) + int(has_alias)
        o_ref = refs[pos]
        acc_ref = refs[pos + 1] if nk > 1 else None
        p = lax.dot_general(a_ref[...].astype(BF16), b_ref[...].astype(BF16), dn, preferred_element_type=F32)

        def finish(v):
            if has_res:
                v = v + res_ref[...]
            o_ref[...] = v.astype(o_ref.dtype)

        if nk == 1:
            finish(p)
        else:
            k = pl.program_id(2)

            @pl.when(k == 0)
            def _():
                acc_ref[...] = p

            @pl.when(k > 0)
            def _():
                acc_ref[...] += p

            @pl.when(k == nk - 1)
            def _():
                finish(acc_ref[...])

    aliases = {len(operands) - 1: 0} if has_alias else {}
    return _pcall(
        body, grid=grid, in_specs=in_specs, out_specs=out_spec, out_shape=out_shape,
        scratch_shapes=[pltpu.VMEM(acc_shape, F32)] if nk > 1 else [],
        input_output_aliases=aliases, compiler_params=_cp("parallel", "parallel", "arbitrary"), name=name,
    )(*operands)


def _mm_fwd(name, a, w, l, *, colshard, res=None, out_split=1):
    M, K = a.shape
    tm = _tile(M, MM_ROWS, BF16_ROWS)
    if colshard and out_split == 1 and res is None:
        cs = w.shape[3]
        th = _tile(M, MM_ROWS // 2, BF16_ROWS)
        if 2 * (N_CHIPS * K * cs * 2 + th * N_CHIPS * cs * 4 + th * K * a.dtype.itemsize) <= MM_VMEM_BUDGET:
            def body(a_ref, b_ref, o_ref):
                av = a_ref[...].astype(BF16)
                for j in range(N_CHIPS):
                    o_ref[:, j * cs:(j + 1) * cs] = jnp.dot(av, b_ref[j], preferred_element_type=F32)

            return _pcall(
                body, grid=(M // th,),
                in_specs=[pl.BlockSpec((th, K), lambda i: (i, 0)), pl.BlockSpec((None, N_CHIPS, K, cs), lambda i: (l, 0, 0, 0))],
                out_specs=pl.BlockSpec((th, N_CHIPS * cs), lambda i: (i, 0)), out_shape=_sds((M, N_CHIPS * cs), F32),
                compiler_params=_cp("parallel"), name=name,
            )(a, w)
    if colshard:
        cs = w.shape[3]
        N, tn, tk = N_CHIPS * cs, cs, K
        b_spec = pl.BlockSpec((None, None, tk, tn), lambda j, i, k: (l, j, k, 0))
    else:
        N = w.shape[2]
        tn, tk = _tile(N, 1024), K
        if K > 1536:
            tm = _tile(M, MM_ROWS // 2, BF16_ROWS)
        b_spec = pl.BlockSpec((None, tk, tn), lambda j, i, k: (l, k, j))
    nk = K // tk
    in_specs = [pl.BlockSpec((tm, tk), lambda j, i, k: (i, k)), b_spec]
    operands = [a, w]
    if res is not None:
        in_specs.append(pl.BlockSpec((tm, tn), lambda j, i, k: (i, j)))
        operands.append(res)
    if out_split == 1:
        out_shape = _sds((M, N), F32)
        out_spec = pl.BlockSpec((tm, tn), lambda j, i, k: (i, j))
    else:
        per = N // tn // out_split
        out_shape = _sds((out_split, M, N // out_split), F32)
        out_spec = pl.BlockSpec((None, tm, tn), lambda j, i, k: (j // per, i, j % per))
    return _mm_call(name, NN, operands, in_specs, out_shape, out_spec, (N // tn, M // tm, nk), nk, (tm, tn),
                    res is not None, False)


def _mm_dgrad(name, g, w, l, *, colshard):
    split = g.ndim == 3
    M = g.shape[-2]
    tm = _tile(M, MM_ROWS, BF16_ROWS)
    if colshard:
        kw, cs = w.shape[2], w.shape[3]
        tm = _tile(M, MM_ROWS // 2, BF16_ROWS)
        per = N_CHIPS // g.shape[0] if split else N_CHIPS

        def body(a_ref, b_ref, o_ref):
            acc = None
            for j in range(N_CHIPS):
                cols = slice((j % per) * cs, (j % per + 1) * cs)
                a = a_ref[j // per, :, cols] if split else a_ref[:, cols]
                p = lax.dot_general(a.astype(BF16), b_ref[j], NT, preferred_element_type=F32)
                acc = p if acc is None else acc + p
            o_ref[...] = acc

        a_spec = (pl.BlockSpec((g.shape[0], tm, g.shape[2]), lambda i: (0, i, 0)) if split
                  else pl.BlockSpec((tm, N_CHIPS * cs), lambda i: (i, 0)))
        return _pcall(
            body, grid=(M // tm,),
            in_specs=[a_spec, pl.BlockSpec((None, N_CHIPS, kw, cs), lambda i: (l, 0, 0, 0))],
            out_specs=pl.BlockSpec((tm, kw), lambda i: (i, 0)), out_shape=_sds((M, kw), F32),
            compiler_params=_cp("parallel"), name=name,
        )(g, w)
    else:
        kw, ncon = w.shape[1], w.shape[2]
        tn, tk = _tile(kw, 1408), _tile(ncon, 1536)
        nk = ncon // tk
        th = _tile(M, MM_ROWS // 2, BF16_ROWS)
        if nk == 1 and 2 * (kw * ncon * w.dtype.itemsize + th * kw * 4 + th * ncon * g.dtype.itemsize) <= MM_VMEM_BUDGET:
            tm, tn = th, kw
        b_spec = pl.BlockSpec((None, tn, tk), lambda j, i, k: (l, j, k))
    if split:
        per = nk // g.shape[0]
        a_spec = pl.BlockSpec((None, tm, tk), lambda j, i, k: (k // per, i, k % per))
    else:
        a_spec = pl.BlockSpec((tm, tk), lambda j, i, k: (i, k))
    out_shape = _sds((M, kw), F32)
    out_spec = pl.BlockSpec((tm, tn), lambda j, i, k: (i, j))
    return _mm_call(name, NT, [g, w], [a_spec, b_spec], out_shape, out_spec, (kw // tn, M // tm, nk), nk, (tm, tn),
                    False, False)


def _mm_wgrad(name, a, g, l, n_layers, buf, *, colshard):
    S, M = a.shape
    split = g.ndim == 3
    N = g.shape[-1] * (g.shape[0] if split else 1)
    tm = _tile(M, 1408)
    tn = N // N_CHIPS if colshard else _tile(N, 1024)
    per_row = 2 * (tm * a.dtype.itemsize + tn * g.dtype.itemsize)
    tk = _tile(S, max(BF16_ROWS, min(2048, (MM_VMEM_BUDGET - 3 * tm * tn * 4) // per_row)), BF16_ROWS)
    nk = S // tk
    if colshard:
        out_shape = _sds((n_layers, N_CHIPS, M, tn), F32)
        out_spec = pl.BlockSpec((None, None, tm, tn), lambda j, i, k: (l, j, i, 0))
    else:
        out_shape = _sds((n_layers, M, N), F32)
        out_spec = pl.BlockSpec((None, tm, tn), lambda j, i, k: (l, i, j))
    if split:
        per = N // tn // g.shape[0]
        b_spec = pl.BlockSpec((None, tk, tn), lambda j, i, k: (j // per, k, j % per))
    else:
        b_spec = pl.BlockSpec((tk, tn), lambda j, i, k: (k, j))
    in_specs = [pl.BlockSpec((tk, tm), lambda j, i, k: (k, i)), b_spec]
    operands = [a, g]
    if buf is not None:
        in_specs.append(ANY)
        operands.append(buf)
    return _mm_call(name, TN, operands, in_specs, out_shape, out_spec, (N // tn, M // tm, nk), nk, (tm, tn),
                    False, buf is not None)


def _rms_fwd(name, x, g, l):
    S, D = x.shape
    tm = _tile(S, 512, BF16_ROWS)

    def body(x_ref, g_ref, o_ref):
        xf = x_ref[...]
        r = lax.rsqrt(jnp.mean(xf * xf, axis=-1, keepdims=True) + EPS)
        o_ref[...] = (xf * r * g_ref[l:l + 1, :]).astype(BF16)

    return _pcall(
        body, grid=(S // tm,),
        in_specs=[pl.BlockSpec((tm, D), lambda i: (i, 0)), pl.BlockSpec(g.shape, lambda i: (0, 0))],
        out_specs=pl.BlockSpec((tm, D), lambda i: (i, 0)), out_shape=_sds((S, D), BF16),
        compiler_params=_cp("parallel"), name=name,
    )(x, g)


def _rms_bwd(name, x, g, l, dh, dres, exchange=None):
    S, D = x.shape
    tm = _tile(S, 512, SUBLANES)

    def body(x_ref, g_ref, dh_ref, dr_ref, dx_ref, dg_ref):
        xf = x_ref[...]
        r = lax.rsqrt(jnp.mean(xf * xf, axis=-1, keepdims=True) + EPS)
        xh = xf * r
        d = dh_ref[...]
        dxh = d * g_ref[l:l + 1, :]
        dx_ref[...] = dr_ref[...] + r * (dxh - xh * jnp.mean(dxh * xh, axis=-1, keepdims=True))

        @pl.when(pl.program_id(0) == 0)
        def _():
            dg_ref[...] = jnp.zeros_like(dg_ref)

        dg_ref[...] += _rowsum(d * xh)

    row = pl.BlockSpec((tm, D), lambda i: (i, 0))
    n_i = S // tm
    body, in_specs, out_specs, out_shape, scratch, operands, aliases = _with_exchange(
        exchange, body, [row, pl.BlockSpec(g.shape, lambda i: (0, 0)), row, row],
        [row, pl.BlockSpec((1, D), lambda i: (0, 0))], [_sds((S, D), F32), _sds((1, D), F32)], [], [x, g, dh, dres],
        lambda: pl.program_id(0) == 0, lambda: pl.program_id(0) == n_i - 1)
    return _pcall(
        body, grid=(n_i,), in_specs=in_specs, out_specs=out_specs, out_shape=out_shape, scratch_shapes=scratch,
        input_output_aliases=aliases, compiler_params=_cp("arbitrary"), name=name,
    )(*operands)


def _loss_fwd_bwd(name, y, t):
    S, D = y.shape
    tm = _tile(S, 512, SUBLANES)

    def body(y_ref, t_ref, dy_ref, l_ref):
        e = y_ref[...] - t_ref[...]
        dy_ref[...] = e * (1.0 / D)

        @pl.when(pl.program_id(0) == 0)
        def _():
            l_ref[...] = jnp.zeros_like(l_ref)

        l_ref[...] += 0.5 * jnp.sum(jnp.sum(e * e, axis=-1, keepdims=True) * (1.0 / D), axis=0, keepdims=True)

    row = pl.BlockSpec((tm, D), lambda i: (i, 0))
    return _pcall(
        body, grid=(S // tm,), in_specs=[row, row],
        out_specs=[row, pl.BlockSpec((SUBLANES, LANES), lambda i: (0, 0))],
        out_shape=[_sds((S, D), F32), _sds((SUBLANES, LANES), F32)],
        compiler_params=_cp("arbitrary"), name=name,
    )(y, t)


def _delayed_copies(us, n_rows):
    for s in range(1, SUBLANES):
        us[s, pl.ds(SUBLANES, n_rows - SUBLANES), :] = us[0, pl.ds(SUBLANES - s, n_rows - SUBLANES), :]


def _conv_a(aw_ref, ab_ref, l, us, row0, rows, dg):
    ka = CONV_A_WIDTH
    out = []
    for c0 in range(0, dg, LANES):
        lanes = slice(c0, c0 + LANES)
        acc = ab_ref[l:l + 1, lanes]
        for d in range(ka):
            a, s = divmod(d, SUBLANES)
            acc = acc + aw_ref[l, ka - 1 - d:ka - d, lanes] * us[s, pl.ds(row0 - SUBLANES * a, rows), lanes]
        out.append(acc)
    return jnp.concatenate(out, axis=1)


def _convmix_fwd(name, p, aw, ab, lg, lb, bw, l, exchange=None):
    S, W = p.shape
    dg = W // 5
    tm = _tile(S, 256, HALO_A)
    nb = tm // HALO_A
    ka, kb = CONV_A_WIDTH, CONV_B_WIDTH

    ext = HALO_A + tm
    rc = _tile(tm, ELT_ROWS, BF16_ROWS)

    def body(p_ref, ph_ref, aw_ref, ab_ref, lg_ref, lb_ref, bw_ref, o_ref, us, mext):
        first = pl.program_id(0) == 0
        ph = ph_ref[...]
        pc = p_ref[...]
        us[0, pl.ds(0, HALO_A), :] = jnp.where(first, 0.0, ph[:, 0:dg] * _sig(ph[:, dg:2 * dg]))
        us[0, pl.ds(HALO_A, tm), :] = pc[:, 0:dg] * _sig(pc[:, dg:2 * dg])
        mext[pl.ds(0, HALO_A), :] = jnp.where(first, 0.0, ph[:, 3 * dg:4 * dg] * ph[:, 4 * dg:5 * dg])
        mext[pl.ds(HALO_A, tm), :] = pc[:, 3 * dg:4 * dg] * pc[:, 4 * dg:5 * dg]
        _delayed_copies(us, ext)
        for r0 in range(0, tm, rc):
            rows = pl.ds(r0, rc)
            c = _conv_a(aw_ref, ab_ref, l, us, HALO_A + r0, rc, dg)
            xc = c - jnp.mean(c, axis=-1, keepdims=True)
            ln = xc * lax.rsqrt(jnp.mean(xc * xc, axis=-1, keepdims=True) + EPS) * lg_ref[l:l + 1, :] + lb_ref[l:l + 1, :]
            o_ref[rows, 0:dg] = (ln * _sig(ln)).astype(BF16)
            cb = bw_ref[l, 0:1, :] * mext[pl.ds(HALO_A - (kb - 1) + r0, rc), :]
            for k in range(1, kb):
                cb = cb + bw_ref[l, k:k + 1, :] * mext[pl.ds(HALO_A - (kb - 1) + k + r0, rc), :]
            o_ref[rows, dg:2 * dg] = (p_ref[rows, 2 * dg:3 * dg] * cb).astype(BF16)

    full = lambda a: pl.BlockSpec(a.shape, lambda i: (0,) * a.ndim)
    n_i = S // tm
    body, in_specs, out_specs, out_shape, scratch, operands, aliases = _with_exchange(
        exchange, body,
        [pl.BlockSpec((tm, W), lambda i: (i, 0)), pl.BlockSpec((HALO_A, W), lambda i: (jnp.maximum(i * nb - 1, 0), 0)),
         full(aw), full(ab), full(lg), full(lb), full(bw)],
        [pl.BlockSpec((tm, 2 * dg), lambda i: (i, 0))], [_sds((S, 2 * dg), BF16)],
        [pltpu.VMEM((SUBLANES, ext, dg), F32), pltpu.VMEM((ext, dg), F32)], [p, p, aw, ab, lg, lb, bw],
        lambda: pl.program_id(0) == 0, lambda: pl.program_id(0) == n_i - 1)
    outs = _pcall(
        body, grid=(n_i,), in_specs=in_specs, out_specs=out_specs, out_shape=out_shape, scratch_shapes=scratch,
        input_output_aliases=aliases, compiler_params=_cp("arbitrary" if exchange else "parallel"), name=name,
    )(*operands)
    return outs if exchange else outs[0]


def _convmix_bwd(name, p, dab, aw, ab, lg, lb, bw, l, exchange=None):
    S, W = p.shape
    dg = W // 5
    tm = _tile(S, 256, HALO_A)
    nb = tm // HALO_A
    n_i = S // tm
    ka, kb = CONV_A_WIDTH, CONV_B_WIDTH
    n = tm + HALO_A
    ext = HALO_A + n
    rc = _tile(tm, ELT_ROWS, BF16_ROWS)

    def body(p_ref, pp_ref, pn_ref, d_ref, dn_ref, aw_ref, ab_ref, lg_ref, lb_ref, bw_ref,
             dp_ref, daw_ref, dab_ref, dlg_ref, dlb_ref, dbw_ref, us, mext, dcs, dbext, accw):
        i = pl.program_id(0)
        first, last = i == 0, i == n_i - 1

        @pl.when(first)
        def _():
            for r in (daw_ref, dab_ref, dlg_ref, dlb_ref, dbw_ref):
                r[...] = jnp.zeros_like(r)

        accw[...] = jnp.zeros_like(accw)
        pp, pc, pn = pp_ref[...], p_ref[...], pn_ref[...]
        glu = lambda b: b[:, 0:dg] * _sig(b[:, dg:2 * dg])
        gch = lambda b: b[:, 3 * dg:4 * dg] * b[:, 4 * dg:5 * dg]
        us[0, pl.ds(0, HALO_A), :] = jnp.where(first, 0.0, glu(pp))
        us[0, pl.ds(HALO_A, tm), :] = glu(pc)
        us[0, pl.ds(HALO_A + tm, HALO_A), :] = glu(pn)
        mext[pl.ds(0, HALO_A), :] = jnp.where(first, 0.0, gch(pp))
        mext[pl.ds(HALO_A, tm), :] = gch(pc)
        mext[pl.ds(HALO_A + tm, HALO_A), :] = gch(pn)
        _delayed_copies(us, ext)
        chunks = [(r, rc) for r in range(0, tm, rc)] + [(tm, HALO_A)]
        g_ln = lg_ref[l:l + 1, :]
        zero8 = jnp.zeros((SUBLANES, dg), F32)

        acc_lg = acc_lb = acc_ab = zero8
        for r0, rows in chunks:
            c = _conv_a(aw_ref, ab_ref, l, us, HALO_A + r0, rows, dg)
            xc = c - jnp.mean(c, axis=-1, keepdims=True)
            rstd = lax.rsqrt(jnp.mean(xc * xc, axis=-1, keepdims=True) + EPS)
            chat = xc * rstd
            ln = chat * g_ln + lb_ref[l:l + 1, :]
            s = _sig(ln)
            da = d_ref[pl.ds(r0, rows), 0:dg] if r0 < tm else jnp.where(last, 0.0, dn_ref[:, 0:dg])
            dln = da * (s * (1.0 + ln * (1.0 - s)))
            dlnh = dln * g_ln
            dc = rstd * (dlnh - jnp.mean(dlnh, axis=-1, keepdims=True)
                         - chat * jnp.mean(dlnh * chat, axis=-1, keepdims=True))
            dcs[0, pl.ds(r0, rows), :] = dc
            if r0 < tm:
                acc_lg = acc_lg + _fold(dln * chat)
                acc_lb = acc_lb + _fold(dln)
                acc_ab = acc_ab + _fold(dc)
                for c0 in range(0, dg, LANES):
                    lanes = slice(c0, c0 + LANES)
                    for d in range(ka):
                        a, sh = divmod(d, SUBLANES)
                        k = ka - 1 - d
                        accw[pl.ds(SUBLANES * k, SUBLANES), lanes] += _fold(
                            dc[:, lanes] * us[sh, pl.ds(HALO_A + r0 - SUBLANES * a, rows), lanes])
        dlg_ref[...] += _rowsum(acc_lg)
        dlb_ref[...] += _rowsum(acc_lb)
        dab_ref[...] += _rowsum(acc_ab)
        for k in range(ka):
            daw_ref[k:k + 1, :] += _rowsum(accw[pl.ds(SUBLANES * k, SUBLANES), :])
        for s in range(1, SUBLANES):
            dcs[s, pl.ds(0, n - SUBLANES), :] = dcs[0, pl.ds(s, n - SUBLANES), :]
        for r0 in range(0, tm, rc):
            rows = pl.ds(r0, rc)
            parts = []
            for c0 in range(0, dg, LANES):
                lanes = slice(c0, c0 + LANES)
                acc = aw_ref[l, ka - 1:ka, lanes] * dcs[0, rows, lanes]
                for e in range(1, ka):
                    a, sh = divmod(e, SUBLANES)
                    acc = acc + aw_ref[l, ka - 1 - e:ka - e, lanes] * dcs[sh, pl.ds(r0 + SUBLANES * a, rc), lanes]
                parts.append(acc)
            du = jnp.concatenate(parts, axis=1)
            sg = _sig(p_ref[rows, dg:2 * dg])
            dp_ref[rows, 0:dg] = (du * sg).astype(BF16)
            dp_ref[rows, dg:2 * dg] = (du * p_ref[rows, 0:dg] * sg * (1.0 - sg)).astype(BF16)

        for r0, rows in chunks:
            if r0 < tm:
                dbext[pl.ds(r0, rows), :] = d_ref[pl.ds(r0, rows), dg:2 * dg] * p_ref[pl.ds(r0, rows), 2 * dg:3 * dg]
            else:
                dbext[pl.ds(r0, rows), :] = jnp.where(last, 0.0, dn_ref[:, dg:2 * dg] * pn[:, 2 * dg:3 * dg])
        acc_bw = [zero8] * kb
        for r0 in range(0, tm, rc):
            rows = pl.ds(r0, rc)
            m_k = [mext[pl.ds(HALO_A - (kb - 1) + k + r0, rc), :] for k in range(kb)]
            cb = bw_ref[l, 0:1, :] * m_k[0]
            dm = bw_ref[l, 0:1, :] * dbext[pl.ds(r0 + kb - 1, rc), :]
            for k in range(1, kb):
                cb = cb + bw_ref[l, k:k + 1, :] * m_k[k]
                dm = dm + bw_ref[l, k:k + 1, :] * dbext[pl.ds(r0 + kb - 1 - k, rc), :]
            dcb = dbext[rows, :]
            acc_bw = [acc_bw[k] + _fold(dcb * m_k[k]) for k in range(kb)]
            dp_ref[rows, 2 * dg:3 * dg] = (d_ref[rows, dg:2 * dg] * cb).astype(BF16)
            dp_ref[rows, 3 * dg:4 * dg] = (dm * p_ref[rows, 4 * dg:5 * dg]).astype(BF16)
            dp_ref[rows, 4 * dg:5 * dg] = (dm * p_ref[rows, 3 * dg:4 * dg]).astype(BF16)
        for k in range(kb):
            dbw_ref[k:k + 1, :] += _rowsum(acc_bw[k])

    full = lambda a: pl.BlockSpec(a.shape, lambda i: (0,) * a.ndim)
    prev = lambda i: (jnp.maximum(i * nb - 1, 0), 0)
    nxt = lambda i: (jnp.minimum((i + 1) * nb, S // HALO_A - 1), 0)
    acc = lambda r: pl.BlockSpec((r, dg), lambda i: (0, 0))
    body, in_specs, out_specs, out_shape, scratch, operands, aliases = _with_exchange(
        exchange, body,
        [pl.BlockSpec((tm, W), lambda i: (i, 0)), pl.BlockSpec((HALO_A, W), prev), pl.BlockSpec((HALO_A, W), nxt),
         pl.BlockSpec((tm, 2 * dg), lambda i: (i, 0)), pl.BlockSpec((HALO_A, 2 * dg), nxt),
         full(aw), full(ab), full(lg), full(lb), full(bw)],
        [pl.BlockSpec((tm, W), lambda i: (i, 0)), acc(ka), acc(1), acc(1), acc(1), acc(kb)],
        [_sds((S, W), BF16), _sds((ka, dg), F32), _sds((1, dg), F32), _sds((1, dg), F32), _sds((1, dg), F32),
         _sds((kb, dg), F32)],
        [pltpu.VMEM((SUBLANES, ext, dg), F32), pltpu.VMEM((ext, dg), F32), pltpu.VMEM((SUBLANES, n, dg), F32),
         pltpu.VMEM((n, dg), F32), pltpu.VMEM((SUBLANES * ka, dg), F32)],
        [p, p, p, dab, dab, aw, ab, lg, lb, bw],
        lambda: pl.program_id(0) == 0, lambda: pl.program_id(0) == n_i - 1)
    return _pcall(
        body, grid=(n_i,), in_specs=in_specs, out_specs=out_specs, out_shape=out_shape, scratch_shapes=scratch,
        input_output_aliases=aliases, compiler_params=_cp("arbitrary"), name=name,
    )(*operands)


def _ffn_mid_fwd(name, u2, dww, dwb, l, exchange=None):
    _, S, F = u2.shape
    tm = _tile(S, 256, BF16_ROWS)
    tc = _tile(F, 1408)
    n_f = F // tc
    nb = tm // HALO_S
    kf = FFN_CONV_WIDTH

    def body(u_ref, uh_ref, wg_ref, wv_ref, bg_ref, bv_ref, o_ref, ext):
        first = pl.program_id(1) == 0
        ext[:, pl.ds(0, HALO_S), :] = jnp.where(first, 0.0, uh_ref[...])
        ext[:, pl.ds(HALO_S, tm), :] = u_ref[...]
        rc = _tile(tm, ELT_ROWS, BF16_ROWS)

        def lane_chunk(ci, carry):
            lanes = pl.ds(pl.multiple_of(ci * LANES, LANES), LANES)
            taps = [[w_ref[k:k + 1, lanes] for k in range(kf)] for w_ref in (wg_ref, wv_ref)]
            bias = [b_ref[l:l + 1, lanes] for b_ref in (bg_ref, bv_ref)]
            for r0 in range(0, tm, rc):
                c = []
                for g in range(2):
                    acc = bias[g]
                    for k in range(kf):
                        acc = acc + taps[g][k] * ext[g, pl.ds(HALO_S - (kf - 1) + k + r0, rc), lanes]
                    c.append(acc)
                o_ref[pl.ds(r0, rc), lanes] = (c[0] * _sig(c[0]) * c[1]).astype(BF16)
            return carry

        lax.fori_loop(0, tc // LANES, lane_chunk, 0)

    n_l = dwb.shape[0]
    n_i = S // tm
    body, in_specs, out_specs, out_shape, scratch, operands, aliases = _with_exchange(
        exchange, body,
        [pl.BlockSpec((2, tm, tc), lambda j, i: (0, i, j)),
         pl.BlockSpec((2, HALO_S, tc), lambda j, i: (0, jnp.maximum(i * nb - 1, 0), j)),
         pl.BlockSpec((None, kf, tc), lambda j, i: (l, 0, j)),
         pl.BlockSpec((None, kf, tc), lambda j, i: (l, 0, j + n_f)),
         pl.BlockSpec((n_l, tc), lambda j, i: (0, j)),
         pl.BlockSpec((n_l, tc), lambda j, i: (0, j + n_f))],
        [pl.BlockSpec((tm, tc), lambda j, i: (i, j))], [_sds((S, F), BF16)],
        [pltpu.VMEM((2, HALO_S + tm, tc), F32)], [u2, u2, dww, dww, dwb, dwb],
        lambda: jnp.logical_and(pl.program_id(0) == 0, pl.program_id(1) == 0),
        lambda: jnp.logical_and(pl.program_id(0) == n_f - 1, pl.program_id(1) == n_i - 1))
    sem = "arbitrary" if exchange else "parallel"
    outs = _pcall(
        body, grid=(n_f, n_i), in_specs=in_specs, out_specs=out_specs, out_shape=out_shape, scratch_shapes=scratch,
        input_output_aliases=aliases, compiler_params=_cp(sem, sem), name=name,
    )(*operands)
    return outs if exchange else outs[0]


def _ffn_mid_bwd(name, u2, df, dww, dwb, l, exchange=None):
    _, S, F = u2.shape
    tm = _tile(S, 256, BF16_ROWS)
    tc = _tile(F, 1408)
    n_f = F // tc
    nb = tm // HALO_S
    n_i = S // tm
    kf = FFN_CONV_WIDTH
    n = tm + HALO_S

    def body(u_ref, up_ref, un_ref, df_ref, dfn_ref, wg_ref, wv_ref, bg_ref, bv_ref,
             du_ref, dw_ref, db_ref, uext, dcext):
        i = pl.program_id(1)
        first, last = i == 0, i == n_i - 1

        @pl.when(first)
        def _():
            dw_ref[...] = jnp.zeros_like(dw_ref)
            db_ref[...] = jnp.zeros_like(db_ref)

        uext[:, pl.ds(0, HALO_S), :] = jnp.where(first, 0.0, up_ref[...])
        uext[:, pl.ds(HALO_S, tm), :] = u_ref[...]
        uext[:, pl.ds(HALO_S + tm, HALO_S), :] = un_ref[...]
        rc = _tile(tm, ELT_ROWS, BF16_ROWS)

        def lane_chunk(ci, carry):
            lanes = pl.ds(pl.multiple_of(ci * LANES, LANES), LANES)
            taps = [[w_ref[k:k + 1, lanes] for k in range(kf)] for w_ref in (wg_ref, wv_ref)]
            bias = [b_ref[l:l + 1, lanes] for b_ref in (bg_ref, bv_ref)]
            acc_w = [[jnp.zeros((SUBLANES, LANES), F32) for _ in range(kf)] for _ in range(2)]
            acc_b = [jnp.zeros((SUBLANES, LANES), F32) for _ in range(2)]
            for r0, rows in [(r, rc) for r in range(0, tm, rc)] + [(tm, HALO_S)]:
                shifted = [[uext[g, pl.ds(HALO_S - (kf - 1) + k + r0, rows), lanes] for k in range(kf)] for g in range(2)]
                conv = []
                for g in range(2):
                    acc = bias[g]
                    for k in range(kf):
                        acc = acc + taps[g][k] * shifted[g][k]
                    conv.append(acc)
                cg, cv = conv
                s = _sig(cg)
                dfe = df_ref[pl.ds(r0, rows), lanes] if r0 < tm else jnp.where(last, 0.0, dfn_ref[:, lanes])
                dc = [dfe * cv * (s * (1.0 + cg * (1.0 - s))), dfe * (cg * s)]
                for g in range(2):
                    dcext[g, pl.ds(r0, rows), lanes] = dc[g]
                    if r0 < tm:
                        acc_b[g] = acc_b[g] + _fold(dc[g])
                        for k in range(kf):
                            acc_w[g][k] = acc_w[g][k] + _fold(dc[g] * shifted[g][k])
            for r0 in range(0, tm, rc):
                for g in range(2):
                    du = taps[g][0] * dcext[g, pl.ds(r0 + kf - 1, rc), lanes]
                    for k in range(1, kf):
                        du = du + taps[g][k] * dcext[g, pl.ds(r0 + kf - 1 - k, rc), lanes]
                    du_ref[g, pl.ds(r0, rc), lanes] = du.astype(BF16)
            for g in range(2):
                db_ref[g, :, lanes] += _rowsum(acc_b[g])
                for k in range(kf):
                    dw_ref[g, k:k + 1, lanes] += _rowsum(acc_w[g][k])
            return carry

        lax.fori_loop(0, tc // LANES, lane_chunk, 0)

    n_l = dwb.shape[0]
    prev = lambda j, i: (0, jnp.maximum(i * nb - 1, 0), j)
    nxt = lambda j, i: (0, jnp.minimum((i + 1) * nb, S // HALO_S - 1), j)
    body, in_specs, out_specs, out_shape, scratch, operands, aliases = _with_exchange(
        exchange, body,
        [pl.BlockSpec((2, tm, tc), lambda j, i: (0, i, j)),
         pl.BlockSpec((2, HALO_S, tc), prev), pl.BlockSpec((2, HALO_S, tc), nxt),
         pl.BlockSpec((tm, tc), lambda j, i: (i, j)),
         pl.BlockSpec((HALO_S, tc), lambda j, i: nxt(j, i)[1:]),
         pl.BlockSpec((None, kf, tc), lambda j, i: (l, 0, j)),
         pl.BlockSpec((None, kf, tc), lambda j, i: (l, 0, j + n_f)),
         pl.BlockSpec((n_l, tc), lambda j, i: (0, j)),
         pl.BlockSpec((n_l, tc), lambda j, i: (0, j + n_f))],
        [pl.BlockSpec((2, tm, tc), lambda j, i: (0, i, j)),
         pl.BlockSpec((2, kf, tc), lambda j, i: (0, 0, j)),
         pl.BlockSpec((2, 1, tc), lambda j, i: (0, 0, j))],
        [_sds((2, S, F), BF16), _sds((2, kf, F), F32), _sds((2, 1, F), F32)],
        [pltpu.VMEM((2, HALO_S + n, tc), F32), pltpu.VMEM((2, n, tc), F32)],
        [u2, u2, u2, df, df, dww, dww, dwb, dwb],
        lambda: jnp.logical_and(pl.program_id(0) == 0, pl.program_id(1) == 0),
        lambda: jnp.logical_and(pl.program_id(0) == n_f - 1, pl.program_id(1) == n_i - 1))
    return _pcall(
        body, grid=(n_f, n_i), in_specs=in_specs, out_specs=out_specs, out_shape=out_shape, scratch_shapes=scratch,
        input_output_aliases=aliases, compiler_params=_cp("arbitrary" if exchange else "parallel", "arbitrary"), name=name,
    )(*operands)


def _head_sum_matrix():
    r = lax.broadcasted_iota(jnp.int32, (LANES, LANES), 0) // HEAD_DIM
    c = lax.broadcasted_iota(jnp.int32, (LANES, LANES), 1) // HEAD_DIM
    return (r == c).astype(BF16)


def _head_mean(x, ones):
    return _split_dot(x, ones) * (1.0 / HEAD_DIM)


def _qknorm_fwd(name, qkv, g2):
    S, D3 = qkv.shape
    D = D3 // 3
    tm = _tile(S, 256, BF16_ROWS)
    scale = HEAD_DIM ** -0.5

    def body(q_ref, k_ref, v_ref, g_ref, qo_ref, ko_ref, vo_ref):
        ones = _head_sum_matrix()
        for cc in range(D // LANES):
            sl = slice(cc * LANES, (cc + 1) * LANES)
            for x_ref, o_ref, row, mult in ((q_ref, qo_ref, 0, scale), (k_ref, ko_ref, 1, 1.0)):
                x = x_ref[:, sl]
                r = lax.rsqrt(_head_mean(x * x, ones) + EPS)
                o_ref[:, sl] = ((x * r * g_ref[row:row + 1, :]).astype(BF16) * mult).astype(BF16)
        vo_ref[...] = v_ref[...].astype(BF16)

    col = lambda c: pl.BlockSpec((tm, D), lambda i: (i, c))
    out = pl.BlockSpec((tm, D), lambda i: (i, 0))
    return _pcall(
        body, grid=(S // tm,),
        in_specs=[col(0), col(1), col(2), pl.BlockSpec(g2.shape, lambda i: (0, 0))],
        out_specs=[out, out, out], out_shape=[_sds((S, D), BF16)] * 3,
        compiler_params=_cp("parallel"), name=name,
    )(qkv, qkv, qkv, g2)


def _qknorm_bwd(name, qkv, dq, dk, dv, g2):
    S, D3 = qkv.shape
    D = D3 // 3
    tm = _tile(S, 256, BF16_ROWS)
    scale = HEAD_DIM ** -0.5

    def body(q_ref, k_ref, dq_ref, dk_ref, dv_ref, g_ref, o_ref, dg_ref):
        @pl.when(pl.program_id(0) == 0)
        def _():
            dg_ref[...] = jnp.zeros_like(dg_ref)

        ones = _head_sum_matrix()
        for cc in range(D // LANES):
            sl = slice(cc * LANES, (cc + 1) * LANES)
            for x_ref, d_ref, row, mult, base in ((q_ref, dq_ref, 0, scale, 0), (k_ref, dk_ref, 1, 1.0, D)):
                x = x_ref[:, sl]
                r = lax.rsqrt(_head_mean(x * x, ones) + EPS)
                xh = x * r
                dn = d_ref[:, sl] * mult
                dxh = dn * g_ref[row:row + 1, :]
                dx = r * (dxh - xh * _head_mean(dxh * xh, ones))
                o_ref[:, base + cc * LANES:base + (cc + 1) * LANES] = dx.astype(BF16)
                dg_ref[row:row + 1, :] += _rowsum(dn * xh)
        o_ref[:, 2 * D:3 * D] = dv_ref[...].astype(BF16)

    col = lambda c: pl.BlockSpec((tm, D), lambda i: (i, c))
    row = pl.BlockSpec((tm, D), lambda i: (i, 0))
    return _pcall(
        body, grid=(S // tm,),
        in_specs=[col(0), col(1), row, row, row, pl.BlockSpec(g2.shape, lambda i: (0, 0))],
        out_specs=[pl.BlockSpec((tm, D3), lambda i: (i, 0)), pl.BlockSpec((2, LANES), lambda i: (0, 0))],
        out_shape=[_sds((S, D3), BF16), _sds((2, LANES), F32)],
        compiler_params=_cp("arbitrary"), name=name,
    )(qkv, qkv, dq, dk, dv, g2)


def _attn_consts():
    t = ATTN_BLOCK
    row = lax.broadcasted_iota(jnp.int32, (t, t), 0)
    col = lax.broadcasted_iota(jnp.int32, (t, t), 1)
    lane = lax.broadcasted_iota(jnp.int32, (1, LANES), 1)
    heads = (lane < HEAD_DIM, lane >= HEAD_DIM)
    return row, col, heads


def _split_dot(x, m):
    n = x.shape[0]
    hi = x.astype(BF16)
    lo = (x - hi.astype(F32)).astype(BF16)
    both = jnp.dot(jnp.concatenate([hi, lo], axis=0), m, preferred_element_type=F32)
    return both[:n] + both[n:]


def _log_keep(z):
    return -(jnp.maximum(z, 0.0) + jnp.log(1.0 + jnp.exp(-jnp.abs(z))))


def _stack_heads(a, heads):
    t = ATTN_BLOCK
    zero = jnp.zeros((t, LANES), a.dtype)
    return jnp.concatenate([jnp.where(h, a[s * t:(s + 1) * t], zero) for s in range(a.shape[0] // t) for h in heads], axis=0)


def _side_by_side(a):
    t = ATTN_BLOCK
    return jnp.concatenate([jnp.concatenate([a[2 * s * t:(2 * s + 1) * t], a[(2 * s + 1) * t:(2 * s + 2) * t]], axis=1)
                            for s in range(a.shape[0] // (2 * t))], axis=0)


def _grow(a, rows, cols):
    z = jnp.zeros((rows, cols), F32)
    return z if a is None else jnp.concatenate([z, a], axis=0)


def _attn_fwd(name, qs, kn, vb, exchange=None):
    S, D = qs.shape
    t = ATTN_BLOCK
    tq = ATTN_SUB * t

    def body(q_ref, k_ref, v_ref, o_ref):
        i = pl.program_id(1)
        row, col, heads = _attn_consts()
        after_m = (row > col).astype(BF16)
        causal = col < row
        q_all = _stack_heads(q_ref[...], heads)

        def blocks(specs, r, acc):
            n_rows = q_all.shape[0]
            offs = [pl.multiple_of(j * t, t) for j, _, _ in specs]
            zs = [lax.dot_general(q_all[lo:], k_ref[pl.ds(off, t), :], NT, preferred_element_type=F32)
                  for off, (_, lo, _) in zip(offs, specs)]
            lks = []
            for z, (_, _, mask) in zip(zs, specs):
                lk = _log_keep(z)
                lks.append(lk if mask is None else jnp.where(mask, lk, 0.0))
            cums = [_split_dot(lk, after_m) for lk in lks]
            ws = []
            for z, lk, cum, (_, lo, mask) in zip(zs, lks, cums, specs):
                rows = n_rows - lo
                r = _grow(r, rows - (0 if r is None else r.shape[0]), 1) if r is None or r.shape[0] < rows else r
                w = jnp.exp(z + lk + cum + r)
                ws.append((w if mask is None else jnp.where(mask, w, 0.0)).astype(BF16))
                r = r + jnp.sum(lk, axis=1, keepdims=True)
            acc = jnp.zeros((n_rows // 2, LANES), F32) if acc is None else acc
            for w, off, (_, lo, _) in zip(ws, offs, specs):
                part = jnp.dot(_side_by_side(w), _stack_heads(v_ref[pl.ds(off, t), :], heads), preferred_element_type=F32)
                acc = acc + (part if lo == 0 else _grow(part, lo // 2, LANES))
            return r, acc

        def head(n_more):
            specs = [(ATTN_SUB * i + s, 2 * s * t,
                      jnp.concatenate([causal, causal] + [jnp.ones_like(causal)] * (2 * (ATTN_SUB - 1 - s)), axis=0))
                     for s in reversed(range(ATTN_SUB))]
            specs += [(ATTN_SUB * i - 1 - b, 0, None) for b in range(n_more)]
            return blocks(specs, None, None)

        r, acc = lax.cond(ATTN_SUB * i >= ATTN_MORE, lambda: head(ATTN_MORE), lambda: head(0))

        def cond(c):
            return jnp.logical_and(c[0] >= 0, jnp.max(c[1]) > EXP_UNDERFLOW)

        def step(c):
            r, a = blocks([(c[0], 0, None)], c[1], c[2])
            return c[0] - 1, r, a

        first = jnp.where(ATTN_SUB * i >= ATTN_MORE, ATTN_SUB * i - 1 - ATTN_MORE, ATTN_SUB * i - 1)
        o_ref[...] = lax.while_loop(cond, step, (first, r, acc))[2]

    n_hp = D // LANES
    blk = pl.BlockSpec((tq, LANES), lambda hp, i: (i, hp))
    seq = pl.BlockSpec((S, LANES), lambda hp, i: (0, hp))
    n_i = S // tq
    body, in_specs, out_specs, out_shape, scratch, operands, aliases = _with_exchange(
        exchange, body, [blk, seq, seq], [blk], [_sds((S, D), F32)], [], [qs, kn, vb],
        lambda: jnp.logical_and(pl.program_id(0) == 0, pl.program_id(1) == 0),
        lambda: jnp.logical_and(pl.program_id(0) == n_hp - 1, pl.program_id(1) == n_i - 1))
    outs = _pcall(
        body, grid=(n_hp, n_i), in_specs=in_specs, out_specs=out_specs, out_shape=out_shape, scratch_shapes=scratch,
        input_output_aliases=aliases, compiler_params=_cp("arbitrary" if exchange else "parallel", "arbitrary"), name=name,
    )(*operands)
    return outs if exchange else outs[0]


def _attn_bwd(name, qs, kn, vb, o, do, exchange=None):
    S, D = qs.shape
    t = ATTN_BLOCK
    tq = ATTN_SUB * t

    def body(q_ref, k_ref, v_ref, o_ref, do_ref, dq_ref, dk_ref, dv_ref):
        i = pl.program_id(1)

        @pl.when(i == 0)
        def _():
            dk_ref[...] = jnp.zeros_like(dk_ref)
            dv_ref[...] = jnp.zeros_like(dv_ref)

        row, col, heads = _attn_consts()
        after_m = (row > col).astype(BF16)
        from_m = (row >= col).astype(BF16)
        causal = col < row
        q_all = _stack_heads(q_ref[...], heads)
        dob = do_ref[...].astype(BF16)
        do_all = _stack_heads(dob, heads)
        dsum_all = jnp.sum(_stack_heads(dob.astype(F32) * o_ref[...], heads), axis=1, keepdims=True)

        def blocks(specs, r, es, dq):
            n_rows = q_all.shape[0]
            offs = [pl.multiple_of(j * t, t) for j, _, _ in specs]
            masked = lambda x, mask: x if mask is None else jnp.where(mask, x, 0.0)
            top = lambda a, rows: a if a is not None and a.shape[0] == rows else _grow(a, rows - (0 if a is None else a.shape[0]), 1)
            zs = [lax.dot_general(q_all[lo:], k_ref[pl.ds(off, t), :], NT, preferred_element_type=F32)
                  for off, (_, lo, _) in zip(offs, specs)]
            gs = [lax.dot_general(do_all[lo:], v_ref[pl.ds(off, t), :], NT, preferred_element_type=F32)
                  for off, (_, lo, _) in zip(offs, specs)]
            lks = [masked(_log_keep(z), mask) for z, (_, _, mask) in zip(zs, specs)]
            cums = [_split_dot(lk, after_m) for lk in lks]
            ws, es_blk, sgs = [], [], []
            for z, g, lk, cum, (_, lo, mask) in zip(zs, gs, lks, cums, specs):
                r = top(r, n_rows - lo)
                ls = z + lk
                w = masked(jnp.exp(ls + cum + r), mask)
                ws.append(w.astype(BF16))
                es_blk.append(w * g)
                sgs.append(jnp.exp(ls))
                r = r + jnp.sum(lk, axis=1, keepdims=True)
            cum_es = [_split_dot(e, from_m) for e in es_blk]
            dzs = []
            for e, cum_e, sg, (_, lo, mask) in zip(es_blk, cum_es, sgs, specs):
                es = top(es, n_rows - lo)
                before = dsum_all[lo:] - (es + cum_e)
                dzs.append(masked(e - (e + before) * sg, mask).astype(BF16))
                es = es + jnp.sum(e, axis=1, keepdims=True)
            dq = jnp.zeros((n_rows // 2, LANES), F32) if dq is None else dq
            for dzb, w, off, (_, lo, _) in zip(dzs, ws, offs, specs):
                part = jnp.dot(_side_by_side(dzb), _stack_heads(k_ref[pl.ds(off, t), :], heads), preferred_element_type=F32)
                dq = dq + (part if lo == 0 else _grow(part, lo // 2, LANES))
                dk_ref[pl.ds(off, t), :] += lax.dot_general(dzb, q_all[lo:], TN, preferred_element_type=F32)
                dv_ref[pl.ds(off, t), :] += lax.dot_general(w, do_all[lo:], TN, preferred_element_type=F32)
            return r, es, dq

        def head(n_more):
            specs = [(ATTN_SUB * i + s, 2 * s * t,
                      jnp.concatenate([causal, causal] + [jnp.ones_like(causal)] * (2 * (ATTN_SUB - 1 - s)), axis=0))
                     for s in reversed(range(ATTN_SUB))]
            specs += [(ATTN_SUB * i - 1 - b, 0, None) for b in range(n_more)]
            return blocks(specs, None, None, None)

        r, es, dq = lax.cond(ATTN_SUB * i >= ATTN_MORE, lambda: head(ATTN_MORE), lambda: head(0))

        def cond(c):
            return jnp.logical_and(c[0] >= 0, jnp.max(c[1]) > EXP_UNDERFLOW)

        def step(c):
            r, es, a = blocks([(c[0], 0, None)], c[1], c[2], c[3])
            return c[0] - 1, r, es, a

        first = jnp.where(ATTN_SUB * i >= ATTN_MORE, ATTN_SUB * i - 1 - ATTN_MORE, ATTN_SUB * i - 1)
        dq_ref[...] = lax.while_loop(cond, step, (first, r, es, dq))[3]

    n_hp = D // LANES
    blk = pl.BlockSpec((tq, LANES), lambda hp, i: (i, hp))
    seq = pl.BlockSpec((S, LANES), lambda hp, i: (0, hp))
    n_i = S // tq
    body, in_specs, out_specs, out_shape, scratch, operands, aliases = _with_exchange(
        exchange, body, [blk, seq, seq, blk, blk], [blk, seq, seq], [_sds((S, D), F32)] * 3, [], [qs, kn, vb, o, do],
        lambda: jnp.logical_and(pl.program_id(0) == 0, pl.program_id(1) == 0),
        lambda: jnp.logical_and(pl.program_id(0) == n_hp - 1, pl.program_id(1) == n_i - 1))
    return _pcall(
        body, grid=(n_hp, n_i), in_specs=in_specs, out_specs=out_specs, out_shape=out_shape, scratch_shapes=scratch,
        input_output_aliases=aliases, compiler_params=_cp("arbitrary" if exchange else "parallel", "arbitrary"), name=name,
    )(*operands)


def _adamw(name, w, g, m, v):
    L, R, C = w.shape
    tr = _tile(R, 256, SUBLANES)
    c1 = 1.0 - ADAM_B1 ** ADAM_STEP
    c2 = 1.0 - ADAM_B2 ** ADAM_STEP

    def body(w_ref, g_ref, m_ref, v_ref, d_ref, mo_ref, vo_ref):
        gg = g_ref[...]
        mn = ADAM_B1 * m_ref[...] + (1.0 - ADAM_B1) * gg
        vn = ADAM_B2 * v_ref[...] + (1.0 - ADAM_B2) * (gg * gg)
        d_ref[...] = -ADAM_LR * ((mn / c1) / (jnp.sqrt(vn / c2) + ADAM_EPS) + ADAM_WD * w_ref[...])
        mo_ref[...] = mn
        vo_ref[...] = vn

    blk = pl.BlockSpec((None, tr, C), lambda l, i: (l, i, 0))
    return _pcall(
        body, grid=(L, R // tr), in_specs=[blk] * 4, out_specs=[blk] * 3, out_shape=[_sds(w.shape, F32)] * 3,
        compiler_params=_cp("parallel", "parallel"), name=name,
    )(w, g, m, v)


def _place():
    x, y, c = lax.axis_index("x"), lax.axis_index("y"), lax.axis_index("c")
    chips = [(1 - x, y), (x, 1 - y), (1 - x, 1 - y)]
    return x, y, c, chips


def _place_shard(name, w, j_idx):
    L, R, X = w.shape
    rh = R // 2
    tr = _tile(rh, 256, BF16_ROWS)

    def body(j_ref, w_ref, o_ref):
        o_ref[...] = w_ref[...].astype(BF16)

    return _pcall(
        body,
        grid_spec=pltpu.PrefetchScalarGridSpec(
            num_scalar_prefetch=1, grid=(L, 2, rh // tr),
            in_specs=[pl.BlockSpec((None, None, tr, X), lambda l, h, i, j_ref: (l, h, i, 0))],
            out_specs=pl.BlockSpec((None, None, None, tr, X), lambda l, h, i, j_ref: (l, j_ref[0], h, i, 0))),
        out_shape=_sds((L, N_CHIPS, 2, rh, X), BF16), compiler_params=_cp("parallel", "parallel", "parallel"), name=name,
    )(j_idx, w.reshape(L, 2, rh, X))


def _all_gather_weights(bufs, spans, small_ws):
    n_big, n_small = len(bufs), len(small_ws)
    n_in = n_big + n_small
    layers = [pl.ds(l0, n) for l0, n in spans]

    def body(*refs):
        ins, outs = refs[:n_in], refs[n_in:2 * n_in]
        send_sems, recv_sems, local_sems = refs[2 * n_in:]
        x, y, c, chips = _place()
        j_me = 2 * x + y
        j_of = [2 * cx + cy for cx, cy in chips]
        sibling = (x, y, 1 - c)

        def remote(src, dst, s, to):
            return pltpu.make_async_remote_copy(src_ref=src, dst_ref=dst, send_sem=send_sems.at[s], recv_sem=recv_sems.at[s],
                                                device_id=to, device_id_type=MESH)

        started = []
        for t in range(n_big, n_in):
            loc = pltpu.make_async_copy(ins[t], outs[t].at[:, j_me], local_sems.at[t - n_big])
            loc.start()
            started.append(loc)
        first = []
        for t in range(n_big):
            mine = outs[t].at[layers[t], j_me, c]
            for k in range(3):
                first.append(remote(mine, mine, 6 * t + k, (*chips[k], c)))
        for t in range(n_big, n_in):
            for k in range(3):
                first.append(remote(ins[t], outs[t].at[:, j_me], 6 * n_big + 3 * (t - n_big) + k, (*chips[k], c)))
        for cp in first:
            cp.start()
        passed = []
        for t in range(n_big):
            for k in range(3):
                landed = outs[t].at[layers[t], j_of[k], c]
                remote(landed, landed, 6 * t + k, (*chips[k], c)).wait_recv()
                fwd = remote(landed, landed, 6 * t + 3 + k, sibling)
                fwd.start()
                passed.append(fwd)
        for t in range(n_big):
            for k in range(3):
                other = outs[t].at[layers[t], j_of[k], 1 - c]
                remote(other, other, 6 * t + 3 + k, sibling).wait_recv()
        for t in range(n_big, n_in):
            for k in range(3):
                dst = outs[t].at[:, j_of[k]]
                remote(dst, dst, 6 * n_big + 3 * (t - n_big) + k, (*chips[k], c)).wait_recv()
        for cp in first + passed:
            cp.wait_send()
        for loc in started:
            loc.wait()

    out_shape = [_sds(b.shape, b.dtype) for b in bufs]
    out_shape += [_sds((w.shape[0], N_CHIPS) + w.shape[1:], w.dtype) for w in small_ws]
    n_sem = 6 * n_big + 3 * n_small
    outs = _pcall(
        body, in_specs=[ANY] * n_in, out_specs=[ANY] * n_in, out_shape=out_shape,
        input_output_aliases={t: t for t in range(n_big)},
        scratch_shapes=[pltpu.SemaphoreType.DMA((n_sem,)), pltpu.SemaphoreType.DMA((n_sem,)), pltpu.SemaphoreType.DMA((n_small,))],
        name="all_gather_weights",
    )(*bufs, *small_ws)
    return outs[:n_big], outs[n_big:]


class _Exchange:
    def __init__(self, operands, out_shapes, n_sems, copies, in_place=False):
        self.operands, self.out_shapes, self.n_sems, self.copies = list(operands), list(out_shapes), n_sems, copies
        self.aliases = {t: t for t in range(len(self.operands))} if in_place else {}

    @property
    def scratch(self):
        return [pltpu.SemaphoreType.DMA((self.n_sems,)), pltpu.SemaphoreType.DMA((self.n_sems,))]

    def split(self, refs):
        n_in, n_out = len(self.operands), len(self.out_shapes)
        return refs[:n_in], refs[n_in:n_in + n_out]

    def start(self, ins, outs, sems):
        for cp in self.copies(ins, outs, *sems):
            cp.start()

    def wait(self, ins, outs, sems):
        for cp in self.copies(ins, outs, *sems):
            cp.wait()


def _run_exchange(name, ex):
    n_in, n_out = len(ex.operands), len(ex.out_shapes)

    def body(*refs):
        ins, outs, sems = refs[:n_in], refs[n_in:n_in + n_out], refs[n_in + n_out:]
        ex.start(ins, outs, sems)
        ex.wait(ins, outs, sems)

    return _pcall(body, in_specs=[ANY] * n_in, out_specs=[ANY] * n_out, out_shape=ex.out_shapes, scratch_shapes=ex.scratch,
                  input_output_aliases=ex.aliases, name=name)(*ex.operands)


def _gather_chips_exchange(bufs, spans):
    def copies(ins, outs, send_sems, recv_sems):
        x, y, c, chips = _place()
        cps = []
        for t, (l0, n) in enumerate(spans):
            mine = outs[t].at[pl.ds(l0, n), 2 * x + y, c]
            cps += [pltpu.make_async_remote_copy(src_ref=mine, dst_ref=mine, send_sem=send_sems.at[3 * t + k],
                                                 recv_sem=recv_sems.at[3 * t + k], device_id=(cx, cy, c), device_id_type=MESH)
                    for k, (cx, cy) in enumerate(chips)]
        return cps

    return _Exchange(bufs, [_sds(b.shape, b.dtype) for b in bufs], 3 * len(bufs), copies, in_place=True)


def _gather_cores_exchange(bufs, spans):
    def copies(ins, outs, send_sems, recv_sems):
        x, y, c, chips = _place()
        cps = []
        for t, (l0, n) in enumerate(spans):
            for k, (cx, cy) in enumerate(chips):
                part = outs[t].at[pl.ds(l0, n), 2 * cx + cy, c]
                cps.append(pltpu.make_async_remote_copy(src_ref=part, dst_ref=part, send_sem=send_sems.at[3 * t + k],
                                                        recv_sem=recv_sems.at[3 * t + k], device_id=(x, y, 1 - c),
                                                        device_id_type=MESH))
        return cps

    return _Exchange(bufs, [_sds(b.shape, b.dtype) for b in bufs], 3 * len(bufs), copies, in_place=True)


def _core_halves_exchange(grads, spans):
    def copies(ins, outs, send_sems, recv_sems):
        x, y, c, _ = _place()
        return [pltpu.make_async_remote_copy(src_ref=ins[t].at[pl.ds(l0, n), :, 1 - c], dst_ref=outs[t],
                                             send_sem=send_sems.at[t], recv_sem=recv_sems.at[t], device_id=(x, y, 1 - c),
                                             device_id_type=MESH) for t, (l0, n) in enumerate(spans)]

    shapes = [_sds((n, g.shape[1], g.shape[3], g.shape[4]), F32) for g, (_, n) in zip(grads, spans)]
    return _Exchange(grads, shapes, len(grads), copies)


def _add_core_halves(name, g, a, c_idx, l0):
    _, nj, _, rh, X = g.shape
    L = a.shape[0]
    tr = _tile(rh, 256, BF16_ROWS)

    def body(c_ref, g_ref, a_ref, o_ref, ob_ref):
        s = g_ref[...] + a_ref[...]
        o_ref[...] = s
        ob_ref[...] = s.astype(BF16)

    blk = pl.BlockSpec((None, None, tr, X), lambda l, j, i, c_ref: (l, j, i, 0))
    return _pcall(
        body,
        grid_spec=pltpu.PrefetchScalarGridSpec(
            num_scalar_prefetch=1, grid=(L, nj, rh // tr),
            in_specs=[pl.BlockSpec((None, None, None, tr, X), lambda l, j, i, c_ref: (l + l0, j, c_ref[0], i, 0)), blk],
            out_specs=[blk, blk]),
        out_shape=[_sds((L, nj, rh, X), F32), _sds((L, nj, rh, X), BF16)],
        compiler_params=_cp("parallel", "parallel", "parallel"), name=name,
    )(c_idx, g, a)


def _chip_shards_exchange(parts):
    def copies(ins, outs, send_sems, recv_sems):
        x, y, c, chips = _place()
        return [pltpu.make_async_remote_copy(
            src_ref=ins[t].at[:, 2 * cx + cy], dst_ref=outs[t].at[k], send_sem=send_sems.at[3 * t + k],
            recv_sem=recv_sems.at[3 * t + k], device_id=(cx, cy, c), device_id_type=MESH)
            for t in range(len(parts)) for k, (cx, cy) in enumerate(chips)]

    shapes = [_sds((3, p.shape[0], p.shape[2], p.shape[3]), p.dtype) for p in parts]
    return _Exchange(parts, shapes, 3 * len(parts), copies)


def _add_chip_shards(name, p, b, jc_idx, l0, n_layers, buf):
    n, _, rh, X = p.shape
    tr = _tile(rh, 256, BF16_ROWS)

    def body(jc_ref, p_ref, b_ref, *rest):
        rest[-1][...] = ((p_ref[...] + b_ref[0].astype(F32)) + b_ref[1].astype(F32)) + b_ref[2].astype(F32)

    in_specs = [pl.BlockSpec((None, None, tr, X), lambda l, i, jc: (l, jc[0], i, 0)),
                pl.BlockSpec((3, None, tr, X), lambda l, i, jc: (0, l, i, 0))]
    operands = [jc_idx, p, b]
    if buf is not None:
        in_specs.append(ANY)
        operands.append(buf)
    return _pcall(
        body,
        grid_spec=pltpu.PrefetchScalarGridSpec(
            num_scalar_prefetch=1, grid=(n, rh // tr), in_specs=in_specs,
            out_specs=pl.BlockSpec((None, None, tr, X), lambda l, i, jc: (l + l0, jc[1], i, 0))),
        out_shape=_sds((n_layers, 2, rh, X), F32), input_output_aliases={3: 0} if buf is not None else {},
        compiler_params=_cp("parallel", "parallel"), name=name,
    )(*operands)


def _join_core_halves(bufs):
    n = len(bufs)

    def body(*refs):
        outs = refs[n:2 * n]
        send_sems, recv_sems = refs[2 * n:]
        x, y, c, _ = _place()
        cps = [pltpu.make_async_remote_copy(src_ref=outs[t].at[:, c], dst_ref=outs[t].at[:, c], send_sem=send_sems.at[t],
                                            recv_sem=recv_sems.at[t], device_id=(x, y, 1 - c), device_id_type=MESH)
               for t in range(n)]
        for cp in cps:
            cp.start()
        for t in range(n):
            pltpu.make_async_remote_copy(src_ref=outs[t].at[:, c], dst_ref=outs[t].at[:, 1 - c], send_sem=send_sems.at[t],
                                         recv_sem=recv_sems.at[t], device_id=(x, y, 1 - c), device_id_type=MESH).wait()

    outs = _pcall(
        body, in_specs=[ANY] * n, out_specs=[ANY] * n, out_shape=[_sds(b.shape, F32) for b in bufs],
        input_output_aliases={t: t for t in range(n)},
        scratch_shapes=[pltpu.SemaphoreType.DMA((n,)), pltpu.SemaphoreType.DMA((n,))],
        name="grad_join_core_halves",
    )(*bufs)
    return [o.reshape(o.shape[0], 2 * o.shape[2], o.shape[3]) for o in outs]


def _all_reduce_small(packed):
    R, C = packed.shape

    def body(x_ref, o_ref, slots, send_sems, recv_sems):
        x, y, c, _ = _place()
        me = 4 * x + 2 * y + c
        slots[me] = x_ref[...]
        cps = []
        for d in range(N_DEV):
            to = (d // 4, (d // 2) % 2, d % 2)
            cp = pltpu.make_async_remote_copy(src_ref=x_ref, dst_ref=slots.at[me], send_sem=send_sems.at[d],
                                              recv_sem=recv_sems.at[me], device_id=to, device_id_type=MESH)
            cps.append(cp)

            @pl.when(d != me)
            def _():
                cp.start()

        for d in range(N_DEV):
            @pl.when(d != me)
            def _():
                pltpu.make_async_remote_copy(src_ref=x_ref, dst_ref=slots.at[d], send_sem=send_sems.at[d],
                                             recv_sem=recv_sems.at[d], device_id=(x, y, c), device_id_type=MESH).wait_recv()
                cps[d].wait_send()

        acc = slots[0]
        for d in range(1, N_DEV):
            acc = acc + slots[d]
        o_ref[...] = acc

    vm = pl.BlockSpec(memory_space=pltpu.VMEM)
    return _pcall(
        body, in_specs=[vm], out_specs=vm, out_shape=_sds((R, C), F32),
        scratch_shapes=[pltpu.VMEM((N_DEV, R, C), F32), pltpu.SemaphoreType.DMA((N_DEV,)), pltpu.SemaphoreType.DMA((N_DEV,))],
        compiler_params=pltpu.CompilerParams(vmem_limit_bytes=VMEM_LIMIT_BYTES), name="all_reduce_small",
    )(packed)


PACK = SUBLANES * LANES


def _pack(arrays):
    flat = []
    for a in arrays:
        v = a.reshape(-1)
        flat.append(jnp.pad(v, (0, (-v.shape[0]) % PACK)))
    return jnp.concatenate(flat).reshape(-1, LANES)


def _unpack(packed, shapes):
    flat = packed.reshape(-1)
    out, pos = [], 0
    for s in shapes:
        n = 1
        for d in s:
            n *= d
        out.append(flat[pos:pos + n].reshape(s))
        pos += n + (-n) % PACK
    return out


def kernel(x, mix_norm_g, ffn_norm_g, conv_w_in, conv_a_dw_w, conv_a_dw_b, conv_a_ln_g, conv_a_ln_b, conv_b_dw_w, conv_w_out, attn_w_qkv, attn_q_g, attn_k_g, attn_w_o, ffn_w_up, ffn_dw_w, ffn_dw_b, ffn_w_down, loss_target, m_mix_norm_g, m_ffn_norm_g, m_conv_w_in, m_conv_a_dw_w, m_conv_a_dw_b, m_conv_a_ln_g, m_conv_a_ln_b, m_conv_b_dw_w, m_conv_w_out, m_attn_w_qkv, m_attn_q_g, m_attn_k_g, m_attn_w_o, m_ffn_w_up, m_ffn_dw_w, m_ffn_dw_b, m_ffn_w_down, v_mix_norm_g, v_ffn_norm_g, v_conv_w_in, v_conv_a_dw_w, v_conv_a_dw_b, v_conv_a_ln_g, v_conv_a_ln_b, v_conv_b_dw_w, v_conv_w_out, v_attn_w_qkv, v_attn_q_g, v_attn_k_g, v_attn_w_o, v_ffn_w_up, v_ffn_dw_w, v_ffn_dw_b, v_ffn_w_down):
    depth = mix_norm_g.shape[0]
    n_even, n_odd = conv_w_in.shape[0], attn_w_qkv.shape[0]
    S, D = x.shape[1], x.shape[2]
    dg = D // 2
    x0 = x.reshape(S, D)
    target = loss_target.reshape(S, D)
    j_me = 2 * lax.axis_index("x") + lax.axis_index("y")
    c_me = lax.axis_index("c")
    j_idx = j_me.astype(jnp.int32).reshape(1)
    c_idx = c_me.astype(jnp.int32).reshape(1)

    col_names = ["conv_w_in", "attn_w_qkv", "ffn_w_up"]
    row_names = ["conv_w_out", "attn_w_o", "ffn_w_down"]
    local = dict(conv_w_in=conv_w_in, attn_w_qkv=attn_w_qkv, ffn_w_up=ffn_w_up, conv_w_out=conv_w_out, attn_w_o=attn_w_o,
                 ffn_w_down=ffn_w_down)
    gbuf = {n: _place_shard(f"place_{n}", local[n], j_idx) for n in col_names + row_names}

    def weights_of(layer):
        mixer = ("conv_w_in", "conv_w_out") if layer % 2 == 0 else ("attn_w_qkv", "attn_w_o")
        return {mixer[0]: (layer // 2, 1), mixer[1]: (layer // 2, 1), "ffn_w_up": (layer, 1), "ffn_w_down": (layer, 1)}

    def w_col(n):
        return gbuf[n].reshape(gbuf[n].shape[0], N_CHIPS, -1, gbuf[n].shape[4])

    def w_row(n):
        return gbuf[n].reshape(gbuf[n].shape[0], -1, gbuf[n].shape[4])

    def carry(make_exchange, layer):
        if layer + 1 == depth:
            return None, []
        names = list(weights_of(layer + 1))
        return make_exchange([gbuf[n] for n in names], list(weights_of(layer + 1).values())), names

    first = weights_of(0)
    outs, (a_dw, b_dw, f_dw) = _all_gather_weights([gbuf[n] for n in first], list(first.values()),
                                                   [conv_a_dw_w, conv_b_dw_w, ffn_dw_w])
    gbuf.update(zip(first, outs))
    unshard = lambda a: jnp.moveaxis(a, 1, 2).reshape(a.shape[0], a.shape[2], N_CHIPS * a.shape[3])
    a_dw, b_dw, f_dw = unshard(a_dw), unshard(b_dw), unshard(f_dw)
    qk_gain = [jnp.stack([jnp.tile(attn_q_g[i], LANES // HEAD_DIM), jnp.tile(attn_k_g[i], LANES // HEAD_DIM)])
               for i in range(n_odd)]

    saved = []
    xc = x0
    for layer in range(depth):
        i = layer // 2
        tag = f"l{layer}"
        s = {"x_in": xc}
        h = _rms_fwd(f"rms_mix_fwd_{tag}", xc, mix_norm_g, layer)
        s["h"] = h
        ex, names = carry(_gather_chips_exchange, layer)
        if layer % 2 == 0:
            p = _mm_fwd(f"conv_in_fwd_{tag}", h, w_col("conv_w_in"), i, colshard=True)
            ab = _convmix_fwd(f"convmix_fwd_{tag}", p, a_dw, conv_a_dw_b, conv_a_ln_g, conv_a_ln_b, b_dw, i, ex)
            if ex:
                ab, *new = ab
                gbuf.update(zip(names, new))
            xm = _mm_fwd(f"conv_out_fwd_{tag}", ab, w_row("conv_w_out"), i, colshard=False, res=xc)
            s.update(p=p, ab=ab)
        else:
            qkv = _mm_fwd(f"attn_qkv_fwd_{tag}", h, w_col("attn_w_qkv"), i, colshard=True)
            qs, kn, vb = _qknorm_fwd(f"qknorm_fwd_{tag}", qkv, qk_gain[i])
            o = _attn_fwd(f"attn_fwd_{tag}", qs, kn, vb, ex)
            if ex:
                o, *new = o
                gbuf.update(zip(names, new))
            xm = _mm_fwd(f"attn_out_fwd_{tag}", o, w_row("attn_w_o"), i, colshard=False, res=xc)
            s.update(qkv=qkv, qs=qs, kn=kn, vb=vb, o=o)
        s["x_mid"] = xm
        h2 = _rms_fwd(f"rms_ffn_fwd_{tag}", xm, ffn_norm_g, layer)
        u2 = _mm_fwd(f"ffn_up_fwd_{tag}", h2, w_col("ffn_w_up"), layer, colshard=True, out_split=2)
        ex, names = carry(_gather_cores_exchange, layer)
        f = _ffn_mid_fwd(f"ffn_mid_fwd_{tag}", u2, f_dw, ffn_dw_b, layer, ex)
        if ex:
            f, *new = f
            gbuf.update(zip(names, new))
        xc = _mm_fwd(f"ffn_down_fwd_{tag}", f, w_row("ffn_w_down"), layer, colshard=False, res=xm)
        s.update(h2=h2, u2=u2, f=f)
        saved.append(s)

    dx, loss_tile = _loss_fwd_bwd("loss", xc, target)

    w_in, w_qkv, w_up = w_col("conv_w_in"), w_col("attn_w_qkv"), w_col("ffn_w_up")
    w_out, w_o, w_down = w_row("conv_w_out"), w_row("attn_w_o"), w_row("ffn_w_down")
    g_up = g_down = g_in = g_out = g_qkv = g_o = None
    big_names = col_names + row_names

    def halves_view(n, g):
        if n in col_names:
            return g.reshape(g.shape[0], N_CHIPS, 2, g.shape[2] // 2, g.shape[3])
        return g.reshape(g.shape[0], N_CHIPS, 2, g.shape[1] // (2 * N_CHIPS), g.shape[2])

    ffn_of_0 = {"ffn_w_up": (0, 1), "ffn_w_down": (0, 1)}
    mixer_of_0 = {"conv_w_in": (0, 1), "conv_w_out": (0, 1)}
    summed_parts = {n: [] for n in big_names}

    def stacks():
        return {"conv_w_in": g_in, "attn_w_qkv": g_qkv, "ffn_w_up": g_up, "conv_w_out": g_out, "attn_w_o": g_o,
                "ffn_w_down": g_down}

    def core_exchange(group):
        return _core_halves_exchange([halves_view(n, stacks()[n]) for n in group], list(group.values()))

    def chip_exchange(tag, arrived):
        sums, parts = [], []
        for group, from_sibling in arrived:
            for n, a in zip(group, from_sibling):
                f32_sum, bf16_sum = _add_core_halves(f"grad_add_core_{n}_{tag}_{group[n][0]}", halves_view(n, stacks()[n]), a,
                                                     c_idx, group[n][0])
                sums.append((n, group[n][0], f32_sum))
                parts.append(bf16_sum)
        return _chip_shards_exchange(parts), sums

    def record(sums, from_chips):
        for (n, l0, f32_sum), b in zip(sums, from_chips):
            summed_parts[n].append((l0, f32_sum, b))

    d_mix_g, d_ffn_g = [None] * depth, [None] * depth
    d_ffn_dw_w, d_ffn_dw_b = [None] * depth, [None] * depth
    d_a_dw_w, d_a_dw_b, d_a_ln_g, d_a_ln_b, d_b_dw_w = ([None] * n_even for _ in range(5))
    d_q_g, d_k_g = [None] * n_odd, [None] * n_odd
    for layer in reversed(range(depth)):
        i = layer // 2
        tag = f"l{layer}"
        s = saved[layer]
        df = _mm_dgrad(f"ffn_down_dgrad_{tag}", dx, w_down, layer, colshard=False)
        g_down = _mm_wgrad(f"ffn_down_wgrad_{tag}", s["f"], dx, layer, depth, g_down, colshard=False)
        above = weights_of(layer + 1) if layer + 1 < depth else None
        arrived = []
        du2, dww, dwb, *from_sibling = _ffn_mid_bwd(f"ffn_mid_bwd_{tag}", s["u2"], df, f_dw, ffn_dw_b, layer,
                                                    core_exchange(above) if above else None)
        if above:
            arrived.append((above, from_sibling))
        d_ffn_dw_w[layer] = jnp.moveaxis(dww, 0, 1).reshape(FFN_CONV_WIDTH, -1)
        d_ffn_dw_b[layer] = dwb.reshape(-1)
        dh2 = _mm_dgrad(f"ffn_up_dgrad_{tag}", du2, w_up, layer, colshard=True)
        g_up = _mm_wgrad(f"ffn_up_wgrad_{tag}", s["h2"], du2, layer, depth, g_up, colshard=True)
        dx, dg_, *from_sibling = _rms_bwd(f"rms_ffn_bwd_{tag}", s["x_mid"], ffn_norm_g, layer, dh2, dx,
                                          core_exchange(ffn_of_0) if layer == 0 else None)
        if layer == 0:
            arrived.append((ffn_of_0, from_sibling))
        d_ffn_g[layer] = dg_.reshape(-1)
        if layer % 2 == 0:
            dab = _mm_dgrad(f"conv_out_dgrad_{tag}", dx, w_out, i, colshard=False)
            g_out = _mm_wgrad(f"conv_out_wgrad_{tag}", s["ab"], dx, i, n_even, g_out, colshard=False)
            chip_ex, sums = chip_exchange(tag, arrived) if arrived else (None, [])
            dp, daw, dab_b, dlg, dlb, dbw, *from_chips = _convmix_bwd(
                f"convmix_bwd_{tag}", s["p"], dab, a_dw, conv_a_dw_b, conv_a_ln_g, conv_a_ln_b, b_dw, i, chip_ex)
            record(sums, from_chips)
            d_a_dw_w[i], d_a_dw_b[i], d_a_ln_g[i], d_a_ln_b[i], d_b_dw_w[i] = (
                daw, dab_b.reshape(-1), dlg.reshape(-1), dlb.reshape(-1), dbw)
            dh = _mm_dgrad(f"conv_in_dgrad_{tag}", dp, w_in, i, colshard=True)
            g_in = _mm_wgrad(f"conv_in_wgrad_{tag}", s["h"], dp, i, n_even, g_in, colshard=True)
        else:
            do = _mm_dgrad(f"attn_out_dgrad_{tag}", dx, w_o, i, colshard=False)
            g_o = _mm_wgrad(f"attn_out_wgrad_{tag}", s["o"], dx, i, n_odd, g_o, colshard=False)
            chip_ex, sums = chip_exchange(tag, arrived) if arrived else (None, [])
            dq, dk, dv, *from_chips = _attn_bwd(f"attn_bwd_{tag}", s["qs"], s["kn"], s["vb"], s["o"], do, chip_ex)
            record(sums, from_chips)
            dqkv, dgain = _qknorm_bwd(f"qknorm_bwd_{tag}", s["qkv"], dq, dk, dv, qk_gain[i])
            d_q_g[i] = dgain[0, :HEAD_DIM] + dgain[0, HEAD_DIM:]
            d_k_g[i] = dgain[1, :HEAD_DIM] + dgain[1, HEAD_DIM:]
            dh = _mm_dgrad(f"attn_qkv_dgrad_{tag}", dqkv, w_qkv, i, colshard=True)
            g_qkv = _mm_wgrad(f"attn_qkv_wgrad_{tag}", s["h"], dqkv, i, n_odd, g_qkv, colshard=True)
        dx, dg_ = _rms_bwd(f"rms_mix_bwd_{tag}", s["x_in"], mix_norm_g, layer, dh, dx)
        d_mix_g[layer] = dg_.reshape(-1)
    grad_x = dx.reshape(1, S, D)

    small = {
        "mix_norm_g": jnp.stack(d_mix_g), "ffn_norm_g": jnp.stack(d_ffn_g),
        "conv_a_dw_w": jnp.stack(d_a_dw_w), "conv_a_dw_b": jnp.stack(d_a_dw_b),
        "conv_a_ln_g": jnp.stack(d_a_ln_g), "conv_a_ln_b": jnp.stack(d_a_ln_b),
        "conv_b_dw_w": jnp.stack(d_b_dw_w), "attn_q_g": jnp.stack(d_q_g), "attn_k_g": jnp.stack(d_k_g),
        "ffn_dw_w": jnp.stack(d_ffn_dw_w), "ffn_dw_b": jnp.stack(d_ffn_dw_b),
    }
    small_names = list(small)
    summed = _all_reduce_small(_pack([loss_tile] + [small[n] for n in small_names]))
    parts = _unpack(summed, [loss_tile.shape] + [small[n].shape for n in small_names])
    loss = parts[0][0, 0]
    small_g = dict(zip(small_names, parts[1:]))
    for n in ("conv_a_dw_w", "conv_b_dw_w", "ffn_dw_w"):
        cs = small_g[n].shape[2] // N_CHIPS
        small_g[n] = lax.dynamic_slice_in_dim(small_g[n], j_me * cs, cs, axis=2)

    from_sibling = _run_exchange("grad_exchange_core_halves", core_exchange(mixer_of_0))
    chip_ex, sums = chip_exchange("last", [(mixer_of_0, from_sibling)])
    record(sums, _run_exchange("grad_exchange_chip_shards", chip_ex))
    jc_idx = jnp.concatenate([j_idx, c_idx])
    totals = {}
    for n in big_names:
        total = None
        for l0, p, b in summed_parts[n]:
            total = _add_chip_shards(f"grad_add_chips_{n}_{l0}", p, b, jc_idx, l0, stacks()[n].shape[0], total)
        totals[n] = total
    big_g = dict(zip(big_names, _join_core_halves([totals[n] for n in big_names])))

    weights = dict(mix_norm_g=mix_norm_g, ffn_norm_g=ffn_norm_g, conv_w_in=conv_w_in, conv_a_dw_w=conv_a_dw_w, conv_a_dw_b=conv_a_dw_b, conv_a_ln_g=conv_a_ln_g, conv_a_ln_b=conv_a_ln_b, conv_b_dw_w=conv_b_dw_w, conv_w_out=conv_w_out, attn_w_qkv=attn_w_qkv, attn_q_g=attn_q_g, attn_k_g=attn_k_g, attn_w_o=attn_w_o, ffn_w_up=ffn_w_up, ffn_dw_w=ffn_dw_w, ffn_dw_b=ffn_dw_b, ffn_w_down=ffn_w_down)
    m_in = dict(mix_norm_g=m_mix_norm_g, ffn_norm_g=m_ffn_norm_g, conv_w_in=m_conv_w_in, conv_a_dw_w=m_conv_a_dw_w, conv_a_dw_b=m_conv_a_dw_b, conv_a_ln_g=m_conv_a_ln_g, conv_a_ln_b=m_conv_a_ln_b, conv_b_dw_w=m_conv_b_dw_w, conv_w_out=m_conv_w_out, attn_w_qkv=m_attn_w_qkv, attn_q_g=m_attn_q_g, attn_k_g=m_attn_k_g, attn_w_o=m_attn_w_o, ffn_w_up=m_ffn_w_up, ffn_dw_w=m_ffn_dw_w, ffn_dw_b=m_ffn_dw_b, ffn_w_down=m_ffn_w_down)
    v_in = dict(mix_norm_g=v_mix_norm_g, ffn_norm_g=v_ffn_norm_g, conv_w_in=v_conv_w_in, conv_a_dw_w=v_conv_a_dw_w, conv_a_dw_b=v_conv_a_dw_b, conv_a_ln_g=v_conv_a_ln_g, conv_a_ln_b=v_conv_a_ln_b, conv_b_dw_w=v_conv_b_dw_w, conv_w_out=v_conv_w_out, attn_w_qkv=v_attn_w_qkv, attn_q_g=v_attn_q_g, attn_k_g=v_attn_k_g, attn_w_o=v_attn_w_o, ffn_w_up=v_ffn_w_up, ffn_dw_w=v_ffn_dw_w, ffn_dw_b=v_ffn_dw_b, ffn_w_down=v_ffn_w_down)
    order = list(weights)
    grads, delta, new_m, new_v = {}, {}, {}, {}
    for n in big_names:
        grads[n] = big_g[n]
        delta[n], new_m[n], new_v[n] = _adamw(f"adamw_{n}", weights[n], big_g[n], m_in[n], v_in[n])
    shapes = [weights[n].shape for n in small_names]
    packed = [_pack([d[n] for n in small_names]) for d in (weights, small_g, m_in, v_in)]
    upd = _adamw("adamw_small", *[p[None] for p in packed])
    for out, res in zip((delta, new_m, new_v), upd):
        out.update(zip(small_names, _unpack(res[0], shapes)))
    grads.update({n: small_g[n].reshape(weights[n].shape) for n in small_names})
    return (loss, grad_x, *[grads[n] for n in order], *[delta[n] for n in order], *[new_m[n] for n in order],
            *[new_v[n] for n in order])
```

```python
import jax
import jax.numpy as jnp
from jax import lax
from jax.experimental import pallas as pl
from jax.experimental.pallas import tpu as pltpu

F32 = jnp.float32
BF16 = jnp.bfloat16
EPS = 1e-6
CONV_A_WIDTH = 31
CONV_B_WIDTH = 3
FFN_CONV_WIDTH = 3
HEAD_DIM = 64
ADAM_LR = 0.001
ADAM_B1 = 0.9
ADAM_B2 = 0.999
ADAM_EPS = 1e-08
ADAM_WD = 0.01
ADAM_STEP = 10

LANES = 128
SUBLANES = 8
BF16_ROWS = 16
V7X_VMEM_BYTES = 64 * 1024 * 1024
VMEM_LIMIT_BYTES = V7X_VMEM_BYTES * 3 // 4
MM_VMEM_BUDGET = VMEM_LIMIT_BYTES * 4 // 5
MM_ROWS = 1024
N_CHIPS = 4
N_DEV = 8
HALO_A = 32
HALO_S = 8
ELT_ROWS = 64
ATTN_BLOCK = 128
ATTN_SUB = 2
ATTN_MORE = 2
EXP_UNDERFLOW = -104.0
MESH = pl.DeviceIdType.MESH
ANY = pl.BlockSpec(memory_space=pl.ANY)
NT = (((1,), (1,)), ((), ()))
NN = (((1,), (0,)), ((), ()))
TN = (((0,), (0,)), ((), ()))


def _pcall(body, **kw):
    return pl.pallas_call(body, **kw)


def _cp(*sem):
    return pltpu.CompilerParams(dimension_semantics=sem, vmem_limit_bytes=VMEM_LIMIT_BYTES)


def _sds(shape, dtype):
    return jax.ShapeDtypeStruct(tuple(shape), dtype)


def _tile(n, cap, align=LANES):
    if n <= cap:
        return n
    for t in range(cap - cap % align, 0, -align):
        if n % t == 0:
            return t
    return n


def _sig(x):
    return 0.5 * jnp.tanh(0.5 * x) + 0.5


def _rowsum(x):
    return jnp.sum(x, axis=0, keepdims=True)


def _fold(x):
    acc = x[0:SUBLANES]
    for r in range(SUBLANES, x.shape[0], SUBLANES):
        acc = acc + x[r:r + SUBLANES]
    return acc


def _with_exchange(ex, body, in_specs, out_specs, out_shape, scratch, operands, first, last):
    if ex is None:
        return body, in_specs, out_specs, out_shape, scratch, operands, {}
    n_in, n_out, n_scr = len(in_specs), len(out_specs), len(scratch)
    e_in, e_out = len(ex.operands), len(ex.out_shapes)

    def hosted(*refs):
        refs = list(refs)
        ins, refs = refs[:n_in], refs[n_in:]
        e_ins, refs = refs[:e_in], refs[e_in:]
        outs, refs = refs[:n_out], refs[n_out:]
        e_outs, refs = refs[:e_out], refs[e_out:]
        scr, sems = refs[:n_scr], refs[n_scr:]

        @pl.when(first())
        def _():
            ex.start(e_ins, e_outs, sems)

        body(*ins, *outs, *scr)

        @pl.when(last())
        def _():
            ex.wait(e_ins, e_outs, sems)

    return (hosted, in_specs + [ANY] * e_in, out_specs + [ANY] * e_out, out_shape + ex.out_shapes, scratch + ex.scratch,
            operands + ex.operands, {n_in + i: n_out + o for i, o in ex.aliases.items()})


def _mm_call(name, dn, operands, in_specs, out_shape, out_spec, grid, nk, acc_shape, has_res, has_alias):
    def body(*refs):
        a_ref, b_ref = refs[0], refs[1]
        pos = 2
        res_ref = refs[pos] if has_res else None
        pos += int(has_res) + int(has_alias)
        o_ref = refs[pos]
        acc_ref = refs[pos + 1] if nk > 1 else None
        p = lax.dot_general(a_ref[...].astype(BF16), b_ref[...].astype(BF16), dn, preferred_element_type=F32)

        def finish(v):
            if has_res:
                v = v + res_ref[...]
            o_ref[...] = v.astype(o_ref.dtype)

        if nk == 1:
            finish(p)
        else:
            k = pl.program_id(2)

            @pl.when(k == 0)
            def _():
                acc_ref[...] = p

            @pl.when(k > 0)
            def _():
                acc_ref[...] += p

            @pl.when(k == nk - 1)
            def _():
                finish(acc_ref[...])

    aliases = {len(operands) - 1: 0} if has_alias else {}
    return _pcall(
        body, grid=grid, in_specs=in_specs, out_specs=out_spec, out_shape=out_shape,
        scratch_shapes=[pltpu.VMEM(acc_shape, F32)] if nk > 1 else [],
        input_output_aliases=aliases, compiler_params=_cp("parallel", "parallel", "arbitrary"), name=name,
    )(*operands)


def _mm_fwd(name, a, w, l, *, colshard, res=None, out_split=1):
    M, K = a.shape
    tm = _tile(M, MM_ROWS, BF16_ROWS)
    if colshard and out_split == 1 and res is None:
        cs = w.shape[3]
        th = _tile(M, MM_ROWS // 2, BF16_ROWS)
        if 2 * (N_CHIPS * K * cs * 2 + th * N_CHIPS * cs * 4 + th * K * a.dtype.itemsize) <= MM_VMEM_BUDGET:
            def body(a_ref, b_ref, o_ref):
                av = a_ref[...].astype(BF16)
                for j in range(N_CHIPS):
                    o_ref[:, j * cs:(j + 1) * cs] = jnp.dot(av, b_ref[j], preferred_element_type=F32)

            return _pcall(
                body, grid=(M // th,),
                in_specs=[pl.BlockSpec((th, K), lambda i: (i, 0)), pl.BlockSpec((None, N_CHIPS, K, cs), lambda i: (l, 0, 0, 0))],
                out_specs=pl.BlockSpec((th, N_CHIPS * cs), lambda i: (i, 0)), out_shape=_sds((M, N_CHIPS * cs), F32),
                compiler_params=_cp("parallel"), name=name,
            )(a, w)
    if colshard:
        cs = w.shape[3]
        N, tn, tk = N_CHIPS * cs, cs, K
        b_spec = pl.BlockSpec((None, None, tk, tn), lambda j, i, k: (l, j, k, 0))
    else:
        N = w.shape[2]
        tn, tk = _tile(N, 1024), K
        if K > 1536:
            tm = _tile(M, MM_ROWS // 2, BF16_ROWS)
        b_spec = pl.BlockSpec((None, tk, tn), lambda j, i, k: (l, k, j))
    nk = K // tk
    in_specs = [pl.BlockSpec((tm, tk), lambda j, i, k: (i, k)), b_spec]
    operands = [a, w]
    if res is not None:
        in_specs.append(pl.BlockSpec((tm, tn), lambda j, i, k: (i, j)))
        operands.append(res)
    if out_split == 1:
        out_shape = _sds((M, N), F32)
        out_spec = pl.BlockSpec((tm, tn), lambda j, i, k: (i, j))
    else:
        per = N // tn // out_split
        out_shape = _sds((out_split, M, N // out_split), F32)
        out_spec = pl.BlockSpec((None, tm, tn), lambda j, i, k: (j // per, i, j % per))
    return _mm_call(name, NN, operands, in_specs, out_shape, out_spec, (N // tn, M // tm, nk), nk, (tm, tn),
                    res is not None, False)


def _mm_dgrad(name, g, w, l, *, colshard):
    split = g.ndim == 3
    M = g.shape[-2]
    tm = _tile(M, MM_ROWS, BF16_ROWS)
    if colshard:
        kw, cs = w.shape[2], w.shape[3]
        tm = _tile(M, MM_ROWS // 2, BF16_ROWS)
        per = N_CHIPS // g.shape[0] if split else N_CHIPS

        def body(a_ref, b_ref, o_ref):
            acc = None
            for j in range(N_CHIPS):
                cols = slice((j % per) * cs, (j % per + 1) * cs)
                a = a_ref[j // per, :, cols] if split else a_ref[:, cols]
                p = lax.dot_general(a.astype(BF16), b_ref[j], NT, preferred_element_type=F32)
                acc = p if acc is None else acc + p
            o_ref[...] = acc

        a_spec = (pl.BlockSpec((g.shape[0], tm, g.shape[2]), lambda i: (0, i, 0)) if split
                  else pl.BlockSpec((tm, N_CHIPS * cs), lambda i: (i, 0)))
        return _pcall(
            body, grid=(M // tm,),
            in_specs=[a_spec, pl.BlockSpec((None, N_CHIPS, kw, cs), lambda i: (l, 0, 0, 0))],
            out_specs=pl.BlockSpec((tm, kw), lambda i: (i, 0)), out_shape=_sds((M, kw), F32),
            compiler_params=_cp("parallel"), name=name,
        )(g, w)
    else:
        kw, ncon = w.shape[1], w.shape[2]
        tn, tk = _tile(kw, 1408), _tile(ncon, 1536)
        nk = ncon // tk
        th = _tile(M, MM_ROWS // 2, BF16_ROWS)
        if nk == 1 and 2 * (kw * ncon * w.dtype.itemsize + th * kw * 4 + th * ncon * g.dtype.itemsize) <= MM_VMEM_BUDGET:
            tm, tn = th, kw
        b_spec = pl.BlockSpec((None, tn, tk), lambda j, i, k: (l, j, k))
    if split:
        per = nk // g.shape[0]
        a_spec = pl.BlockSpec((None, tm, tk), lambda j, i, k: (k // per, i, k % per))
    else:
        a_spec = pl.BlockSpec((tm, tk), lambda j, i, k: (i, k))
    out_shape = _sds((M, kw), F32)
    out_spec = pl.BlockSpec((tm, tn), lambda j, i, k: (i, j))
    return _mm_call(name, NT, [g, w], [a_spec, b_spec], out_shape, out_spec, (kw // tn, M // tm, nk), nk, (tm, tn),
                    False, False)


def _mm_wgrad(name, a, g, l, n_layers, buf, *, colshard):
    S, M = a.shape
    split = g.ndim == 3
    N = g.shape[-1] * (g.shape[0] if split else 1)
    tm = _tile(M, 1408)
    tn = N // N_CHIPS if colshard else _tile(N, 1024)
    per_row = 2 * (tm * a.dtype.itemsize + tn * g.dtype.itemsize)
    tk = _tile(S, max(BF16_ROWS, min(2048, (MM_VMEM_BUDGET - 3 * tm * tn * 4) // per_row)), BF16_ROWS)
    nk = S // tk
    if colshard:
        out_shape = _sds((n_layers, N_CHIPS, M, tn), F32)
        out_spec = pl.BlockSpec((None, None, tm, tn), lambda j, i, k: (l, j, i, 0))
    else:
        out_shape = _sds((n_layers, M, N), F32)
        out_spec = pl.BlockSpec((None, tm, tn), lambda j, i, k: (l, i, j))
    if split:
        per = N // tn // g.shape[0]
        b_spec = pl.BlockSpec((None, tk, tn), lambda j, i, k: (j // per, k, j % per))
    else:
        b_spec = pl.BlockSpec((tk, tn), lambda j, i, k: (k, j))
    in_specs = [pl.BlockSpec((tk, tm), lambda j, i, k: (k, i)), b_spec]
    operands = [a, g]
    if buf is not None:
        in_specs.append(ANY)
        operands.append(buf)
    return _mm_call(name, TN, operands, in_specs, out_shape, out_spec, (N // tn, M // tm, nk), nk, (tm, tn),
                    False, buf is not None)


def _rms_fwd(name, x, g, l, exchange=None):
    S, D = x.shape
    tm = _tile(S, 512, BF16_ROWS)
    n_i = S // tm

    def body(x_ref, g_ref, o_ref):
        xf = x_ref[...]
        r = lax.rsqrt(jnp.mean(xf * xf, axis=-1, keepdims=True) + EPS)
        o_ref[...] = (xf * r * g_ref[l:l + 1, :]).astype(BF16)

    body, in_specs, out_specs, out_shape, scratch, operands, aliases = _with_exchange(
        exchange, body, [pl.BlockSpec((tm, D), lambda i: (i, 0)), pl.BlockSpec(g.shape, lambda i: (0, 0))],
        [pl.BlockSpec((tm, D), lambda i: (i, 0))], [_sds((S, D), BF16)], [], [x, g],
        lambda: pl.program_id(0) == 0, lambda: pl.program_id(0) == n_i - 1)
    outs = _pcall(
        body, grid=(n_i,), in_specs=in_specs, out_specs=out_specs, out_shape=out_shape, scratch_shapes=scratch,
        input_output_aliases=aliases, compiler_params=_cp("arbitrary" if exchange else "parallel"), name=name,
    )(*operands)
    return outs if exchange else outs[0]


def _rms_bwd(name, x, g, l, dh, dres, exchange=None):
    S, D = x.shape
    tm = _tile(S, 512, SUBLANES)

    def body(x_ref, g_ref, dh_ref, dr_ref, dx_ref, dg_ref):
        xf = x_ref[...]
        r = lax.rsqrt(jnp.mean(xf * xf, axis=-1, keepdims=True) + EPS)
        xh = xf * r
        d = dh_ref[...]
        dxh = d * g_ref[l:l + 1, :]
        dx_ref[...] = dr_ref[...] + r * (dxh - xh * jnp.mean(dxh * xh, axis=-1, keepdims=True))

        @pl.when(pl.program_id(0) == 0)
        def _():
            dg_ref[...] = jnp.zeros_like(dg_ref)

        dg_ref[...] += _rowsum(d * xh)

    row = pl.BlockSpec((tm, D), lambda i: (i, 0))
    n_i = S // tm
    body, in_specs, out_specs, out_shape, scratch, operands, aliases = _with_exchange(
        exchange, body, [row, pl.BlockSpec(g.shape, lambda i: (0, 0)), row, row],
        [row, pl.BlockSpec((1, D), lambda i: (0, 0))], [_sds((S, D), F32), _sds((1, D), F32)], [], [x, g, dh, dres],
        lambda: pl.program_id(0) == 0, lambda: pl.program_id(0) == n_i - 1)
    return _pcall(
        body, grid=(n_i,), in_specs=in_specs, out_specs=out_specs, out_shape=out_shape, scratch_shapes=scratch,
        input_output_aliases=aliases, compiler_params=_cp("arbitrary"), name=name,
    )(*operands)


def _loss_fwd_bwd(name, y, t):
    S, D = y.shape
    tm = _tile(S, 512, SUBLANES)

    def body(y_ref, t_ref, dy_ref, l_ref):
        e = y_ref[...] - t_ref[...]
        dy_ref[...] = e * (1.0 / D)

        @pl.when(pl.program_id(0) == 0)
        def _():
            l_ref[...] = jnp.zeros_like(l_ref)

        l_ref[...] += 0.5 * jnp.sum(jnp.sum(e * e, axis=-1, keepdims=True) * (1.0 / D), axis=0, keepdims=True)

    row = pl.BlockSpec((tm, D), lambda i: (i, 0))
    return _pcall(
        body, grid=(S // tm,), in_specs=[row, row],
        out_specs=[row, pl.BlockSpec((SUBLANES, LANES), lambda i: (0, 0))],
        out_shape=[_sds((S, D), F32), _sds((SUBLANES, LANES), F32)],
        compiler_params=_cp("arbitrary"), name=name,
    )(y, t)


def _delayed_copies(us, n_rows):
    for s in range(1, SUBLANES):
        us[s, pl.ds(SUBLANES, n_rows - SUBLANES), :] = us[0, pl.ds(SUBLANES - s, n_rows - SUBLANES), :]


def _conv_a(aw_ref, ab_ref, l, us, row0, rows, dg):
    ka = CONV_A_WIDTH
    out = []
    for c0 in range(0, dg, LANES):
        lanes = slice(c0, c0 + LANES)
        acc = ab_ref[l:l + 1, lanes]
        for d in range(ka):
            a, s = divmod(d, SUBLANES)
            acc = acc + aw_ref[l, ka - 1 - d:ka - d, lanes] * us[s, pl.ds(row0 - SUBLANES * a, rows), lanes]
        out.append(acc)
    return jnp.concatenate(out, axis=1)


def _convmix_fwd(name, p, aw, ab, lg, lb, bw, l, exchange=None):
    S, W = p.shape
    dg = W // 5
    tm = _tile(S, 256, HALO_A)
    nb = tm // HALO_A
    ka, kb = CONV_A_WIDTH, CONV_B_WIDTH

    ext = HALO_A + tm
    rc = _tile(tm, ELT_ROWS, BF16_ROWS)

    def body(p_ref, ph_ref, aw_ref, ab_ref, lg_ref, lb_ref, bw_ref, o_ref, us, mext):
        first = pl.program_id(0) == 0
        ph = ph_ref[...]
        pc = p_ref[...]
        us[0, pl.ds(0, HALO_A), :] = jnp.where(first, 0.0, ph[:, 0:dg] * _sig(ph[:, dg:2 * dg]))
        us[0, pl.ds(HALO_A, tm), :] = pc[:, 0:dg] * _sig(pc[:, dg:2 * dg])
        mext[pl.ds(0, HALO_A), :] = jnp.where(first, 0.0, ph[:, 3 * dg:4 * dg] * ph[:, 4 * dg:5 * dg])
        mext[pl.ds(HALO_A, tm), :] = pc[:, 3 * dg:4 * dg] * pc[:, 4 * dg:5 * dg]
        _delayed_copies(us, ext)
        for r0 in range(0, tm, rc):
            rows = pl.ds(r0, rc)
            c = _conv_a(aw_ref, ab_ref, l, us, HALO_A + r0, rc, dg)
            xc = c - jnp.mean(c, axis=-1, keepdims=True)
            ln = xc * lax.rsqrt(jnp.mean(xc * xc, axis=-1, keepdims=True) + EPS) * lg_ref[l:l + 1, :] + lb_ref[l:l + 1, :]
            o_ref[rows, 0:dg] = (ln * _sig(ln)).astype(BF16)
            cb = bw_ref[l, 0:1, :] * mext[pl.ds(HALO_A - (kb - 1) + r0, rc), :]
            for k in range(1, kb):
                cb = cb + bw_ref[l, k:k + 1, :] * mext[pl.ds(HALO_A - (kb - 1) + k + r0, rc), :]
            o_ref[rows, dg:2 * dg] = (p_ref[rows, 2 * dg:3 * dg] * cb).astype(BF16)

    full = lambda a: pl.BlockSpec(a.shape, lambda i: (0,) * a.ndim)
    n_i = S // tm
    body, in_specs, out_specs, out_shape, scratch, operands, aliases = _with_exchange(
        exchange, body,
        [pl.BlockSpec((tm, W), lambda i: (i, 0)), pl.BlockSpec((HALO_A, W), lambda i: (jnp.maximum(i * nb - 1, 0), 0)),
         full(aw), full(ab), full(lg), full(lb), full(bw)],
        [pl.BlockSpec((tm, 2 * dg), lambda i: (i, 0))], [_sds((S, 2 * dg), BF16)],
        [pltpu.VMEM((SUBLANES, ext, dg), F32), pltpu.VMEM((ext, dg), F32)], [p, p, aw, ab, lg, lb, bw],
        lambda: pl.program_id(0) == 0, lambda: pl.program_id(0) == n_i - 1)
    outs = _pcall(
        body, grid=(n_i,), in_specs=in_specs, out_specs=out_specs, out_shape=out_shape, scratch_shapes=scratch,
        input_output_aliases=aliases, compiler_params=_cp("arbitrary" if exchange else "parallel"), name=name,
    )(*operands)
    return outs if exchange else outs[0]


def _convmix_bwd(name, p, dab, aw, ab, lg, lb, bw, l, exchange=None):
    S, W = p.shape
    dg = W // 5
    tm = _tile(S, 256, HALO_A)
    nb = tm // HALO_A
    n_i = S // tm
    ka, kb = CONV_A_WIDTH, CONV_B_WIDTH
    n = tm + HALO_A
    ext = HALO_A + n
    rc = _tile(tm, ELT_ROWS, BF16_ROWS)

    def body(p_ref, pp_ref, pn_ref, d_ref, dn_ref, aw_ref, ab_ref, lg_ref, lb_ref, bw_ref,
             dp_ref, daw_ref, dab_ref, dlg_ref, dlb_ref, dbw_ref, us, mext, dcs, dbext, accw):
        i = pl.program_id(0)
        first, last = i == 0, i == n_i - 1

        @pl.when(first)
        def _():
            for r in (daw_ref, dab_ref, dlg_ref, dlb_ref, dbw_ref):
                r[...] = jnp.zeros_like(r)

        accw[...] = jnp.zeros_like(accw)
        pp, pc, pn = pp_ref[...], p_ref[...], pn_ref[...]
        glu = lambda b: b[:, 0:dg] * _sig(b[:, dg:2 * dg])
        gch = lambda b: b[:, 3 * dg:4 * dg] * b[:, 4 * dg:5 * dg]
        us[0, pl.ds(0, HALO_A), :] = jnp.where(first, 0.0, glu(pp))
        us[0, pl.ds(HALO_A, tm), :] = glu(pc)
        us[0, pl.ds(HALO_A + tm, HALO_A), :] = glu(pn)
        mext[pl.ds(0, HALO_A), :] = jnp.where(first, 0.0, gch(pp))
        mext[pl.ds(HALO_A, tm), :] = gch(pc)
        mext[pl.ds(HALO_A + tm, HALO_A), :] = gch(pn)
        _delayed_copies(us, ext)
        chunks = [(r, rc) for r in range(0, tm, rc)] + [(tm, HALO_A)]
        g_ln = lg_ref[l:l + 1, :]
        zero8 = jnp.zeros((SUBLANES, dg), F32)

        acc_lg = acc_lb = acc_ab = zero8
        for r0, rows in chunks:
            c = _conv_a(aw_ref, ab_ref, l, us, HALO_A + r0, rows, dg)
            xc = c - jnp.mean(c, axis=-1, keepdims=True)
            rstd = lax.rsqrt(jnp.mean(xc * xc, axis=-1, keepdims=True) + EPS)
            chat = xc * rstd
            ln = chat * g_ln + lb_ref[l:l + 1, :]
            s = _sig(ln)
            da = d_ref[pl.ds(r0, rows), 0:dg] if r0 < tm else jnp.where(last, 0.0, dn_ref[:, 0:dg])
            dln = da * (s * (1.0 + ln * (1.0 - s)))
            dlnh = dln * g_ln
            dc = rstd * (dlnh - jnp.mean(dlnh, axis=-1, keepdims=True)
                         - chat * jnp.mean(dlnh * chat, axis=-1, keepdims=True))
            dcs[0, pl.ds(r0, rows), :] = dc
            if r0 < tm:
                acc_lg = acc_lg + _fold(dln * chat)
                acc_lb = acc_lb + _fold(dln)
                acc_ab = acc_ab + _fold(dc)
                for c0 in range(0, dg, LANES):
                    lanes = slice(c0, c0 + LANES)
                    for d in range(ka):
                        a, sh = divmod(d, SUBLANES)
                        k = ka - 1 - d
                        accw[pl.ds(SUBLANES * k, SUBLANES), lanes] += _fold(
                            dc[:, lanes] * us[sh, pl.ds(HALO_A + r0 - SUBLANES * a, rows), lanes])
        dlg_ref[...] += _rowsum(acc_lg)
        dlb_ref[...] += _rowsum(acc_lb)
        dab_ref[...] += _rowsum(acc_ab)
        for k in range(ka):
            daw_ref[k:k + 1, :] += _rowsum(accw[pl.ds(SUBLANES * k, SUBLANES), :])
        for s in range(1, SUBLANES):
            dcs[s, pl.ds(0, n - SUBLANES), :] = dcs[0, pl.ds(s, n - SUBLANES), :]
        for r0 in range(0, tm, rc):
            rows = pl.ds(r0, rc)
            parts = []
            for c0 in range(0, dg, LANES):
                lanes = slice(c0, c0 + LANES)
                acc = aw_ref[l, ka - 1:ka, lanes] * dcs[0, rows, lanes]
                for e in range(1, ka):
                    a, sh = divmod(e, SUBLANES)
                    acc = acc + aw_ref[l, ka - 1 - e:ka - e, lanes] * dcs[sh, pl.ds(r0 + SUBLANES * a, rc), lanes]
                parts.append(acc)
            du = jnp.concatenate(parts, axis=1)
            sg = _sig(p_ref[rows, dg:2 * dg])
            dp_ref[rows, 0:dg] = (du * sg).astype(BF16)
            dp_ref[rows, dg:2 * dg] = (du * p_ref[rows, 0:dg] * sg * (1.0 - sg)).astype(BF16)

        for r0, rows in chunks:
            if r0 < tm:
                dbext[pl.ds(r0, rows), :] = d_ref[pl.ds(r0, rows), dg:2 * dg] * p_ref[pl.ds(r0, rows), 2 * dg:3 * dg]
            else:
                dbext[pl.ds(r0, rows), :] = jnp.where(last, 0.0, dn_ref[:, dg:2 * dg] * pn[:, 2 * dg:3 * dg])
        acc_bw = [zero8] * kb
        for r0 in range(0, tm, rc):
            rows = pl.ds(r0, rc)
            m_k = [mext[pl.ds(HALO_A - (kb - 1) + k + r0, rc), :] for k in range(kb)]
            cb = bw_ref[l, 0:1, :] * m_k[0]
            dm = bw_ref[l, 0:1, :] * dbext[pl.ds(r0 + kb - 1, rc), :]
            for k in range(1, kb):
                cb = cb + bw_ref[l, k:k + 1, :] * m_k[k]
                dm = dm + bw_ref[l, k:k + 1, :] * dbext[pl.ds(r0 + kb - 1 - k, rc), :]
            dcb = dbext[rows, :]
            acc_bw = [acc_bw[k] + _fold(dcb * m_k[k]) for k in range(kb)]
            dp_ref[rows, 2 * dg:3 * dg] = (d_ref[rows, dg:2 * dg] * cb).astype(BF16)
            dp_ref[rows, 3 * dg:4 * dg] = (dm * p_ref[rows, 4 * dg:5 * dg]).astype(BF16)
            dp_ref[rows, 4 * dg:5 * dg] = (dm * p_ref[rows, 3 * dg:4 * dg]).astype(BF16)
        for k in range(kb):
            dbw_ref[k:k + 1, :] += _rowsum(acc_bw[k])

    full = lambda a: pl.BlockSpec(a.shape, lambda i: (0,) * a.ndim)
    prev = lambda i: (jnp.maximum(i * nb - 1, 0), 0)
    nxt = lambda i: (jnp.minimum((i + 1) * nb, S // HALO_A - 1), 0)
    acc = lambda r: pl.BlockSpec((r, dg), lambda i: (0, 0))
    body, in_specs, out_specs, out_shape, scratch, operands, aliases = _with_exchange(
        exchange, body,
        [pl.BlockSpec((tm, W), lambda i: (i, 0)), pl.BlockSpec((HALO_A, W), prev), pl.BlockSpec((HALO_A, W), nxt),
         pl.BlockSpec((tm, 2 * dg), lambda i: (i, 0)), pl.BlockSpec((HALO_A, 2 * dg), nxt),
         full(aw), full(ab), full(lg), full(lb), full(bw)],
        [pl.BlockSpec((tm, W), lambda i: (i, 0)), acc(ka), acc(1), acc(1), acc(1), acc(kb)],
        [_sds((S, W), BF16), _sds((ka, dg), F32), _sds((1, dg), F32), _sds((1, dg), F32), _sds((1, dg), F32),
         _sds((kb, dg), F32)],
        [pltpu.VMEM((SUBLANES, ext, dg), F32), pltpu.VMEM((ext, dg), F32), pltpu.VMEM((SUBLANES, n, dg), F32),
         pltpu.VMEM((n, dg), F32), pltpu.VMEM((SUBLANES * ka, dg), F32)],
        [p, p, p, dab, dab, aw, ab, lg, lb, bw],
        lambda: pl.program_id(0) == 0, lambda: pl.program_id(0) == n_i - 1)
    return _pcall(
        body, grid=(n_i,), in_specs=in_specs, out_specs=out_specs, out_shape=out_shape, scratch_shapes=scratch,
        input_output_aliases=aliases, compiler_params=_cp("arbitrary"), name=name,
    )(*operands)


def _ffn_mid_fwd(name, u2, dww, dwb, l, exchange=None):
    _, S, F = u2.shape
    tm = _tile(S, 256, BF16_ROWS)
    tc = _tile(F, 1408)
    n_f = F // tc
    nb = tm // HALO_S
    kf = FFN_CONV_WIDTH

    def body(u_ref, uh_ref, wg_ref, wv_ref, bg_ref, bv_ref, o_ref, ext):
        first = pl.program_id(1) == 0
        ext[:, pl.ds(0, HALO_S), :] = jnp.where(first, 0.0, uh_ref[...])
        ext[:, pl.ds(HALO_S, tm), :] = u_ref[...]
        rc = _tile(tm, ELT_ROWS, BF16_ROWS)

        def lane_chunk(ci, carry):
            lanes = pl.ds(pl.multiple_of(ci * LANES, LANES), LANES)
            taps = [[w_ref[k:k + 1, lanes] for k in range(kf)] for w_ref in (wg_ref, wv_ref)]
            bias = [b_ref[l:l + 1, lanes] for b_ref in (bg_ref, bv_ref)]
            for r0 in range(0, tm, rc):
                c = []
                for g in range(2):
                    acc = bias[g]
                    for k in range(kf):
                        acc = acc + taps[g][k] * ext[g, pl.ds(HALO_S - (kf - 1) + k + r0, rc), lanes]
                    c.append(acc)
                o_ref[pl.ds(r0, rc), lanes] = (c[0] * _sig(c[0]) * c[1]).astype(BF16)
            return carry

        lax.fori_loop(0, tc // LANES, lane_chunk, 0)

    n_l = dwb.shape[0]
    n_i = S // tm
    body, in_specs, out_specs, out_shape, scratch, operands, aliases = _with_exchange(
        exchange, body,
        [pl.BlockSpec((2, tm, tc), lambda j, i: (0, i, j)),
         pl.BlockSpec((2, HALO_S, tc), lambda j, i: (0, jnp.maximum(i * nb - 1, 0), j)),
         pl.BlockSpec((None, kf, tc), lambda j, i: (l, 0, j)),
         pl.BlockSpec((None, kf, tc), lambda j, i: (l, 0, j + n_f)),
         pl.BlockSpec((n_l, tc), lambda j, i: (0, j)),
         pl.BlockSpec((n_l, tc), lambda j, i: (0, j + n_f))],
        [pl.BlockSpec((tm, tc), lambda j, i: (i, j))], [_sds((S, F), BF16)],
        [pltpu.VMEM((2, HALO_S + tm, tc), F32)], [u2, u2, dww, dww, dwb, dwb],
        lambda: jnp.logical_and(pl.program_id(0) == 0, pl.program_id(1) == 0),
        lambda: jnp.logical_and(pl.program_id(0) == n_f - 1, pl.program_id(1) == n_i - 1))
    sem = "arbitrary" if exchange else "parallel"
    outs = _pcall(
        body, grid=(n_f, n_i), in_specs=in_specs, out_specs=out_specs, out_shape=out_shape, scratch_shapes=scratch,
        input_output_aliases=aliases, compiler_params=_cp(sem, sem), name=name,
    )(*operands)
    return outs if exchange else outs[0]


def _ffn_mid_bwd(name, u2, df, dww, dwb, l, exchange=None):
    _, S, F = u2.shape
    tm = _tile(S, 256, BF16_ROWS)
    tc = _tile(F, 1408)
    n_f = F // tc
    nb = tm // HALO_S
    n_i = S // tm
    kf = FFN_CONV_WIDTH
    n = tm + HALO_S

    def body(u_ref, up_ref, un_ref, df_ref, dfn_ref, wg_ref, wv_ref, bg_ref, bv_ref,
             du_ref, dw_ref, db_ref, uext, dcext):
        i = pl.program_id(1)
        first, last = i == 0, i == n_i - 1

        @pl.when(first)
        def _():
            dw_ref[...] = jnp.zeros_like(dw_ref)
            db_ref[...] = jnp.zeros_like(db_ref)

        uext[:, pl.ds(0, HALO_S), :] = jnp.where(first, 0.0, up_ref[...])
        uext[:, pl.ds(HALO_S, tm), :] = u_ref[...]
        uext[:, pl.ds(HALO_S + tm, HALO_S), :] = un_ref[...]
        rc = _tile(tm, ELT_ROWS, BF16_ROWS)

        def lane_chunk(ci, carry):
            lanes = pl.ds(pl.multiple_of(ci * LANES, LANES), LANES)
            taps = [[w_ref[k:k + 1, lanes] for k in range(kf)] for w_ref in (wg_ref, wv_ref)]
            bias = [b_ref[l:l + 1, lanes] for b_ref in (bg_ref, bv_ref)]
            acc_w = [[jnp.zeros((SUBLANES, LANES), F32) for _ in range(kf)] for _ in range(2)]
            acc_b = [jnp.zeros((SUBLANES, LANES), F32) for _ in range(2)]
            for r0, rows in [(r, rc) for r in range(0, tm, rc)] + [(tm, HALO_S)]:
                shifted = [[uext[g, pl.ds(HALO_S - (kf - 1) + k + r0, rows), lanes] for k in range(kf)] for g in range(2)]
                conv = []
                for g in range(2):
                    acc = bias[g]
                    for k in range(kf):
                        acc = acc + taps[g][k] * shifted[g][k]
                    conv.append(acc)
                cg, cv = conv
                s = _sig(cg)
                dfe = df_ref[pl.ds(r0, rows), lanes] if r0 < tm else jnp.where(last, 0.0, dfn_ref[:, lanes])
                dc = [dfe * cv * (s * (1.0 + cg * (1.0 - s))), dfe * (cg * s)]
                for g in range(2):
                    dcext[g, pl.ds(r0, rows), lanes] = dc[g]
                    if r0 < tm:
                        acc_b[g] = acc_b[g] + _fold(dc[g])
                        for k in range(kf):
                            acc_w[g][k] = acc_w[g][k] + _fold(dc[g] * shifted[g][k])
            for r0 in range(0, tm, rc):
                for g in range(2):
                    du = taps[g][0] * dcext[g, pl.ds(r0 + kf - 1, rc), lanes]
                    for k in range(1, kf):
                        du = du + taps[g][k] * dcext[g, pl.ds(r0 + kf - 1 - k, rc), lanes]
                    du_ref[g, pl.ds(r0, rc), lanes] = du.astype(BF16)
            for g in range(2):
                db_ref[g, :, lanes] += _rowsum(acc_b[g])
                for k in range(kf):
                    dw_ref[g, k:k + 1, lanes] += _rowsum(acc_w[g][k])
            return carry

        lax.fori_loop(0, tc // LANES, lane_chunk, 0)

    n_l = dwb.shape[0]
    prev = lambda j, i: (0, jnp.maximum(i * nb - 1, 0), j)
    nxt = lambda j, i: (0, jnp.minimum((i + 1) * nb, S // HALO_S - 1), j)
    body, in_specs, out_specs, out_shape, scratch, operands, aliases = _with_exchange(
        exchange, body,
        [pl.BlockSpec((2, tm, tc), lambda j, i: (0, i, j)),
         pl.BlockSpec((2, HALO_S, tc), prev), pl.BlockSpec((2, HALO_S, tc), nxt),
         pl.BlockSpec((tm, tc), lambda j, i: (i, j)),
         pl.BlockSpec((HALO_S, tc), lambda j, i: nxt(j, i)[1:]),
         pl.BlockSpec((None, kf, tc), lambda j, i: (l, 0, j)),
         pl.BlockSpec((None, kf, tc), lambda j, i: (l, 0, j + n_f)),
         pl.BlockSpec((n_l, tc), lambda j, i: (0, j)),
         pl.BlockSpec((n_l, tc), lambda j, i: (0, j + n_f))],
        [pl.BlockSpec((2, tm, tc), lambda j, i: (0, i, j)),
         pl.BlockSpec((2, kf, tc), lambda j, i: (0, 0, j)),
         pl.BlockSpec((2, 1, tc), lambda j, i: (0, 0, j))],
        [_sds((2, S, F), BF16), _sds((2, kf, F), F32), _sds((2, 1, F), F32)],
        [pltpu.VMEM((2, HALO_S + n, tc), F32), pltpu.VMEM((2, n, tc), F32)],
        [u2, u2, u2, df, df, dww, dww, dwb, dwb],
        lambda: jnp.logical_and(pl.program_id(0) == 0, pl.program_id(1) == 0),
        lambda: jnp.logical_and(pl.program_id(0) == n_f - 1, pl.program_id(1) == n_i - 1))
    return _pcall(
        body, grid=(n_f, n_i), in_specs=in_specs, out_specs=out_specs, out_shape=out_shape, scratch_shapes=scratch,
        input_output_aliases=aliases, compiler_params=_cp("arbitrary" if exchange else "parallel", "arbitrary"), name=name,
    )(*operands)


def _head_sum_matrix():
    r = lax.broadcasted_iota(jnp.int32, (LANES, LANES), 0) // HEAD_DIM
    c = lax.broadcasted_iota(jnp.int32, (LANES, LANES), 1) // HEAD_DIM
    return (r == c).astype(BF16)


def _head_mean(x, ones):
    return _split_dot(x, ones) * (1.0 / HEAD_DIM)


def _qknorm_fwd(name, qkv, g2):
    S, D3 = qkv.shape
    D = D3 // 3
    tm = _tile(S, 256, BF16_ROWS)
    scale = HEAD_DIM ** -0.5

    def body(q_ref, k_ref, v_ref, g_ref, qo_ref, ko_ref, vo_ref):
        ones = _head_sum_matrix()
        for cc in range(D // LANES):
            sl = slice(cc * LANES, (cc + 1) * LANES)
            for x_ref, o_ref, row, mult in ((q_ref, qo_ref, 0, scale), (k_ref, ko_ref, 1, 1.0)):
                x = x_ref[:, sl]
                r = lax.rsqrt(_head_mean(x * x, ones) + EPS)
                o_ref[:, sl] = ((x * r * g_ref[row:row + 1, :]).astype(BF16) * mult).astype(BF16)
        vo_ref[...] = v_ref[...].astype(BF16)

    col = lambda c: pl.BlockSpec((tm, D), lambda i: (i, c))
    out = pl.BlockSpec((tm, D), lambda i: (i, 0))
    return _pcall(
        body, grid=(S // tm,),
        in_specs=[col(0), col(1), col(2), pl.BlockSpec(g2.shape, lambda i: (0, 0))],
        out_specs=[out, out, out], out_shape=[_sds((S, D), BF16)] * 3,
        compiler_params=_cp("parallel"), name=name,
    )(qkv, qkv, qkv, g2)


def _qknorm_bwd(name, qkv, dq, dk, dv, g2):
    S, D3 = qkv.shape
    D = D3 // 3
    tm = _tile(S, 256, BF16_ROWS)
    scale = HEAD_DIM ** -0.5

    def body(q_ref, k_ref, dq_ref, dk_ref, dv_ref, g_ref, o_ref, dg_ref):
        @pl.when(pl.program_id(0) == 0)
        def _():
            dg_ref[...] = jnp.zeros_like(dg_ref)

        ones = _head_sum_matrix()
        for cc in range(D // LANES):
            sl = slice(cc * LANES, (cc + 1) * LANES)
            for x_ref, d_ref, row, mult, base in ((q_ref, dq_ref, 0, scale, 0), (k_ref, dk_ref, 1, 1.0, D)):
                x = x_ref[:, sl]
                r = lax.rsqrt(_head_mean(x * x, ones) + EPS)
                xh = x * r
                dn = d_ref[:, sl] * mult
                dxh = dn * g_ref[row:row + 1, :]
                dx = r * (dxh - xh * _head_mean(dxh * xh, ones))
                o_ref[:, base + cc * LANES:base + (cc + 1) * LANES] = dx.astype(BF16)
                dg_ref[row:row + 1, :] += _rowsum(dn * xh)
        o_ref[:, 2 * D:3 * D] = dv_ref[...].astype(BF16)

    col = lambda c: pl.BlockSpec((tm, D), lambda i: (i, c))
    row = pl.BlockSpec((tm, D), lambda i: (i, 0))
    return _pcall(
        body, grid=(S // tm,),
        in_specs=[col(0), col(1), row, row, row, pl.BlockSpec(g2.shape, lambda i: (0, 0))],
        out_specs=[pl.BlockSpec((tm, D3), lambda i: (i, 0)), pl.BlockSpec((2, LANES), lambda i: (0, 0))],
        out_shape=[_sds((S, D3), BF16), _sds((2, LANES), F32)],
        compiler_params=_cp("arbitrary"), name=name,
    )(qkv, qkv, dq, dk, dv, g2)


def _attn_consts():
    t = ATTN_BLOCK
    row = lax.broadcasted_iota(jnp.int32, (t, t), 0)
    col = lax.broadcasted_iota(jnp.int32, (t, t), 1)
    lane = lax.broadcasted_iota(jnp.int32, (1, LANES), 1)
    heads = (lane < HEAD_DIM, lane >= HEAD_DIM)
    return row, col, heads


def _split_dot(x, m):
    n = x.shape[0]
    hi = x.astype(BF16)
    lo = (x - hi.astype(F32)).astype(BF16)
    both = jnp.dot(jnp.concatenate([hi, lo], axis=0), m, preferred_element_type=F32)
    return both[:n] + both[n:]


def _log_keep(z):
    return -(jnp.maximum(z, 0.0) + jnp.log(1.0 + jnp.exp(-jnp.abs(z))))


def _stack_heads(a, heads):
    t = ATTN_BLOCK
    zero = jnp.zeros((t, LANES), a.dtype)
    return jnp.concatenate([jnp.where(h, a[s * t:(s + 1) * t], zero) for s in range(a.shape[0] // t) for h in heads], axis=0)


def _side_by_side(a):
    t = ATTN_BLOCK
    return jnp.concatenate([jnp.concatenate([a[2 * s * t:(2 * s + 1) * t], a[(2 * s + 1) * t:(2 * s + 2) * t]], axis=1)
                            for s in range(a.shape[0] // (2 * t))], axis=0)


def _grow(a, rows, cols):
    z = jnp.zeros((rows, cols), F32)
    return z if a is None else jnp.concatenate([z, a], axis=0)


def _attn_fwd(name, qs, kn, vb, exchange=None):
    S, D = qs.shape
    t = ATTN_BLOCK
    tq = ATTN_SUB * t

    def body(q_ref, k_ref, v_ref, o_ref):
        i = pl.program_id(1)
        row, col, heads = _attn_consts()
        after_m = (row > col).astype(BF16)
        causal = col < row
        q_all = _stack_heads(q_ref[...], heads)

        def blocks(specs, r, acc):
            n_rows = q_all.shape[0]
            offs = [pl.multiple_of(j * t, t) for j, _, _ in specs]
            zs = [lax.dot_general(q_all[lo:], k_ref[pl.ds(off, t), :], NT, preferred_element_type=F32)
                  for off, (_, lo, _) in zip(offs, specs)]
            lks = []
            for z, (_, _, mask) in zip(zs, specs):
                lk = _log_keep(z)
                lks.append(lk if mask is None else jnp.where(mask, lk, 0.0))
            cums = [_split_dot(lk, after_m) for lk in lks]
            ws = []
            for z, lk, cum, (_, lo, mask) in zip(zs, lks, cums, specs):
                rows = n_rows - lo
                r = _grow(r, rows - (0 if r is None else r.shape[0]), 1) if r is None or r.shape[0] < rows else r
                w = jnp.exp(z + lk + cum + r)
                ws.append((w if mask is None else jnp.where(mask, w, 0.0)).astype(BF16))
                r = r + jnp.sum(lk, axis=1, keepdims=True)
            acc = jnp.zeros((n_rows // 2, LANES), F32) if acc is None else acc
            for w, off, (_, lo, _) in zip(ws, offs, specs):
                part = jnp.dot(_side_by_side(w), _stack_heads(v_ref[pl.ds(off, t), :], heads), preferred_element_type=F32)
                acc = acc + (part if lo == 0 else _grow(part, lo // 2, LANES))
            return r, acc

        def head(n_more):
            specs = [(ATTN_SUB * i + s, 2 * s * t,
                      jnp.concatenate([causal, causal] + [jnp.ones_like(causal)] * (2 * (ATTN_SUB - 1 - s)), axis=0))
                     for s in reversed(range(ATTN_SUB))]
            specs += [(ATTN_SUB * i - 1 - b, 0, None) for b in range(n_more)]
            return blocks(specs, None, None)

        r, acc = lax.cond(ATTN_SUB * i >= ATTN_MORE, lambda: head(ATTN_MORE), lambda: head(0))

        def cond(c):
            return jnp.logical_and(c[0] >= 0, jnp.max(c[1]) > EXP_UNDERFLOW)

        def step(c):
            r, a = blocks([(c[0], 0, None)], c[1], c[2])
            return c[0] - 1, r, a

        first = jnp.where(ATTN_SUB * i >= ATTN_MORE, ATTN_SUB * i - 1 - ATTN_MORE, ATTN_SUB * i - 1)
        o_ref[...] = lax.while_loop(cond, step, (first, r, acc))[2]

    n_hp = D // LANES
    blk = pl.BlockSpec((tq, LANES), lambda hp, i: (i, hp))
    seq = pl.BlockSpec((S, LANES), lambda hp, i: (0, hp))
    n_i = S // tq
    body, in_specs, out_specs, out_shape, scratch, operands, aliases = _with_exchange(
        exchange, body, [blk, seq, seq], [blk], [_sds((S, D), F32)], [], [qs, kn, vb],
        lambda: jnp.logical_and(pl.program_id(0) == 0, pl.program_id(1) == 0),
        lambda: jnp.logical_and(pl.program_id(0) == n_hp - 1, pl.program_id(1) == n_i - 1))
    outs = _pcall(
        body, grid=(n_hp, n_i), in_specs=in_specs, out_specs=out_specs, out_shape=out_shape, scratch_shapes=scratch,
        input_output_aliases=aliases, compiler_params=_cp("arbitrary" if exchange else "parallel", "arbitrary"), name=name,
    )(*operands)
    return outs if exchange else outs[0]


def _attn_bwd(name, qs, kn, vb, o, do, exchange=None):
    S, D = qs.shape
    t = ATTN_BLOCK
    tq = ATTN_SUB * t

    def body(q_ref, k_ref, v_ref, o_ref, do_ref, dq_ref, dk_ref, dv_ref):
        i = pl.program_id(1)

        @pl.when(i == 0)
        def _():
            dk_ref[...] = jnp.zeros_like(dk_ref)
            dv_ref[...] = jnp.zeros_like(dv_ref)

        row, col, heads = _attn_consts()
        after_m = (row > col).astype(BF16)
        from_m = (row >= col).astype(BF16)
        causal = col < row
        q_all = _stack_heads(q_ref[...], heads)
        dob = do_ref[...].astype(BF16)
        do_all = _stack_heads(dob, heads)
        dsum_all = jnp.sum(_stack_heads(dob.astype(F32) * o_ref[...], heads), axis=1, keepdims=True)

        def blocks(specs, r, es, dq):
            n_rows = q_all.shape[0]
            offs = [pl.multiple_of(j * t, t) for j, _, _ in specs]
            masked = lambda x, mask: x if mask is None else jnp.where(mask, x, 0.0)
            top = lambda a, rows: a if a is not None and a.shape[0] == rows else _grow(a, rows - (0 if a is None else a.shape[0]), 1)
            zs = [lax.dot_general(q_all[lo:], k_ref[pl.ds(off, t), :], NT, preferred_element_type=F32)
                  for off, (_, lo, _) in zip(offs, specs)]
            gs = [lax.dot_general(do_all[lo:], v_ref[pl.ds(off, t), :], NT, preferred_element_type=F32)
                  for off, (_, lo, _) in zip(offs, specs)]
            lks = [masked(_log_keep(z), mask) for z, (_, _, mask) in zip(zs, specs)]
            cums = [_split_dot(lk, after_m) for lk in lks]
            ws, es_blk, sgs = [], [], []
            for z, g, lk, cum, (_, lo, mask) in zip(zs, gs, lks, cums, specs):
                r = top(r, n_rows - lo)
                ls = z + lk
                w = masked(jnp.exp(ls + cum + r), mask)
                ws.append(w.astype(BF16))
                es_blk.append(w * g)
                sgs.append(jnp.exp(ls))
                r = r + jnp.sum(lk, axis=1, keepdims=True)
            cum_es = [_split_dot(e, from_m) for e in es_blk]
            dzs = []
            for e, cum_e, sg, (_, lo, mask) in zip(es_blk, cum_es, sgs, specs):
                es = top(es, n_rows - lo)
                before = dsum_all[lo:] - (es + cum_e)
                dzs.append(masked(e - (e + before) * sg, mask).astype(BF16))
                es = es + jnp.sum(e, axis=1, keepdims=True)
            dq = jnp.zeros((n_rows // 2, LANES), F32) if dq is None else dq
            for dzb, w, off, (_, lo, _) in zip(dzs, ws, offs, specs):
                part = jnp.dot(_side_by_side(dzb), _stack_heads(k_ref[pl.ds(off, t), :], heads), preferred_element_type=F32)
                dq = dq + (part if lo == 0 else _grow(part, lo // 2, LANES))
                dk_ref[pl.ds(off, t), :] += lax.dot_general(dzb, q_all[lo:], TN, preferred_element_type=F32)
                dv_ref[pl.ds(off, t), :] += lax.dot_general(w, do_all[lo:], TN, preferred_element_type=F32)
            return r, es, dq

        def head(n_more):
            specs = [(ATTN_SUB * i + s, 2 * s * t,
                      jnp.concatenate([causal, causal] + [jnp.ones_like(causal)] * (2 * (ATTN_SUB - 1 - s)), axis=0))
                     for s in reversed(range(ATTN_SUB))]
            specs += [(ATTN_SUB * i - 1 - b, 0, None) for b in range(n_more)]
            return blocks(specs, None, None, None)

        r, es, dq = lax.cond(ATTN_SUB * i >= ATTN_MORE, lambda: head(ATTN_MORE), lambda: head(0))

        def cond(c):
            return jnp.logical_and(c[0] >= 0, jnp.max(c[1]) > EXP_UNDERFLOW)

        def step(c):
            r, es, a = blocks([(c[0], 0, None)], c[1], c[2], c[3])
            return c[0] - 1, r, es, a

        first = jnp.where(ATTN_SUB * i >= ATTN_MORE, ATTN_SUB * i - 1 - ATTN_MORE, ATTN_SUB * i - 1)
        dq_ref[...] = lax.while_loop(cond, step, (first, r, es, dq))[3]

    n_hp = D // LANES
    blk = pl.BlockSpec((tq, LANES), lambda hp, i: (i, hp))
    seq = pl.BlockSpec((S, LANES), lambda hp, i: (0, hp))
    n_i = S // tq
    body, in_specs, out_specs, out_shape, scratch, operands, aliases = _with_exchange(
        exchange, body, [blk, seq, seq, blk, blk], [blk, seq, seq], [_sds((S, D), F32)] * 3, [], [qs, kn, vb, o, do],
        lambda: jnp.logical_and(pl.program_id(0) == 0, pl.program_id(1) == 0),
        lambda: jnp.logical_and(pl.program_id(0) == n_hp - 1, pl.program_id(1) == n_i - 1))
    return _pcall(
        body, grid=(n_hp, n_i), in_specs=in_specs, out_specs=out_specs, out_shape=out_shape, scratch_shapes=scratch,
        input_output_aliases=aliases, compiler_params=_cp("arbitrary" if exchange else "parallel", "arbitrary"), name=name,
    )(*operands)


def _adamw(name, w, g, m, v):
    L, R, C = w.shape
    tr = _tile(R, 256, SUBLANES)
    c1 = 1.0 - ADAM_B1 ** ADAM_STEP
    c2 = 1.0 - ADAM_B2 ** ADAM_STEP

    def body(w_ref, g_ref, m_ref, v_ref, d_ref, mo_ref, vo_ref):
        gg = g_ref[...]
        mn = ADAM_B1 * m_ref[...] + (1.0 - ADAM_B1) * gg
        vn = ADAM_B2 * v_ref[...] + (1.0 - ADAM_B2) * (gg * gg)
        d_ref[...] = -ADAM_LR * ((mn / c1) / (jnp.sqrt(vn / c2) + ADAM_EPS) + ADAM_WD * w_ref[...])
        mo_ref[...] = mn
        vo_ref[...] = vn

    blk = pl.BlockSpec((None, tr, C), lambda l, i: (l, i, 0))
    return _pcall(
        body, grid=(L, R // tr), in_specs=[blk] * 4, out_specs=[blk] * 3, out_shape=[_sds(w.shape, F32)] * 3,
        compiler_params=_cp("parallel", "parallel"), name=name,
    )(w, g, m, v)


def _place():
    x, y, c = lax.axis_index("x"), lax.axis_index("y"), lax.axis_index("c")
    chips = [(1 - x, y), (x, 1 - y), (1 - x, 1 - y)]
    return x, y, c, chips


def _place_shard(name, w, j_idx):
    L, R, X = w.shape
    rh = R // 2
    tr = _tile(rh, 256, BF16_ROWS)

    def body(j_ref, w_ref, o_ref):
        o_ref[...] = w_ref[...].astype(BF16)

    return _pcall(
        body,
        grid_spec=pltpu.PrefetchScalarGridSpec(
            num_scalar_prefetch=1, grid=(L, 2, rh // tr),
            in_specs=[pl.BlockSpec((None, None, tr, X), lambda l, h, i, j_ref: (l, h, i, 0))],
            out_specs=pl.BlockSpec((None, None, None, tr, X), lambda l, h, i, j_ref: (l, j_ref[0], h, i, 0))),
        out_shape=_sds((L, N_CHIPS, 2, rh, X), BF16), compiler_params=_cp("parallel", "parallel", "parallel"), name=name,
    )(j_idx, w.reshape(L, 2, rh, X))


def _all_gather_weights(bufs, spans, small_ws):
    n_big, n_small = len(bufs), len(small_ws)
    n_in = n_big + n_small
    layers = [pl.ds(l0, n) for l0, n in spans]

    def body(*refs):
        ins, outs = refs[:n_in], refs[n_in:2 * n_in]
        send_sems, recv_sems, local_sems = refs[2 * n_in:]
        x, y, c, chips = _place()
        j_me = 2 * x + y
        j_of = [2 * cx + cy for cx, cy in chips]
        sibling = (x, y, 1 - c)

        def remote(src, dst, s, to):
            return pltpu.make_async_remote_copy(src_ref=src, dst_ref=dst, send_sem=send_sems.at[s], recv_sem=recv_sems.at[s],
                                                device_id=to, device_id_type=MESH)

        started = []
        for t in range(n_big, n_in):
            loc = pltpu.make_async_copy(ins[t], outs[t].at[:, j_me], local_sems.at[t - n_big])
            loc.start()
            started.append(loc)
        first = []
        for t in range(n_big):
            mine = outs[t].at[layers[t], j_me, c]
            for k in range(3):
                first.append(remote(mine, mine, 6 * t + k, (*chips[k], c)))
        for t in range(n_big, n_in):
            for k in range(3):
                first.append(remote(ins[t], outs[t].at[:, j_me], 6 * n_big + 3 * (t - n_big) + k, (*chips[k], c)))
        for cp in first:
            cp.start()
        passed = []
        for t in range(n_big):
            for k in range(3):
                landed = outs[t].at[layers[t], j_of[k], c]
                remote(landed, landed, 6 * t + k, (*chips[k], c)).wait_recv()
                fwd = remote(landed, landed, 6 * t + 3 + k, sibling)
                fwd.start()
                passed.append(fwd)
        for t in range(n_big):
            for k in range(3):
                other = outs[t].at[layers[t], j_of[k], 1 - c]
                remote(other, other, 6 * t + 3 + k, sibling).wait_recv()
        for t in range(n_big, n_in):
            for k in range(3):
                dst = outs[t].at[:, j_of[k]]
                remote(dst, dst, 6 * n_big + 3 * (t - n_big) + k, (*chips[k], c)).wait_recv()
        for cp in first + passed:
            cp.wait_send()
        for loc in started:
            loc.wait()

    out_shape = [_sds(b.shape, b.dtype) for b in bufs]
    out_shape += [_sds((w.shape[0], N_CHIPS) + w.shape[1:], w.dtype) for w in small_ws]
    n_sem = 6 * n_big + 3 * n_small
    outs = _pcall(
        body, in_specs=[ANY] * n_in, out_specs=[ANY] * n_in, out_shape=out_shape,
        input_output_aliases={t: t for t in range(n_big)},
        scratch_shapes=[pltpu.SemaphoreType.DMA((n_sem,)), pltpu.SemaphoreType.DMA((n_sem,)), pltpu.SemaphoreType.DMA((n_small,))],
        name="all_gather_weights",
    )(*bufs, *small_ws)
    return outs[:n_big], outs[n_big:]


class _Exchange:
    def __init__(self, operands, out_shapes, n_sems, copies, in_place=False):
        self.operands, self.out_shapes, self.n_sems, self.copies = list(operands), list(out_shapes), n_sems, copies
        self.aliases = {t: t for t in range(len(self.operands))} if in_place else {}

    @property
    def scratch(self):
        return [pltpu.SemaphoreType.DMA((self.n_sems,)), pltpu.SemaphoreType.DMA((self.n_sems,))]

    def split(self, refs):
        n_in, n_out = len(self.operands), len(self.out_shapes)
        return refs[:n_in], refs[n_in:n_in + n_out]

    def start(self, ins, outs, sems):
        for cp in self.copies(ins, outs, *sems):
            cp.start()

    def wait(self, ins, outs, sems):
        for cp in self.copies(ins, outs, *sems):
            cp.wait()


def _run_exchange(name, ex):
    n_in, n_out = len(ex.operands), len(ex.out_shapes)

    def body(*refs):
        ins, outs, sems = refs[:n_in], refs[n_in:n_in + n_out], refs[n_in + n_out:]
        ex.start(ins, outs, sems)
        ex.wait(ins, outs, sems)

    return _pcall(body, in_specs=[ANY] * n_in, out_specs=[ANY] * n_out, out_shape=ex.out_shapes, scratch_shapes=ex.scratch,
                  input_output_aliases=ex.aliases, name=name)(*ex.operands)


def _gather_chips_exchange(bufs, spans):
    def copies(ins, outs, send_sems, recv_sems):
        x, y, c, chips = _place()
        cps = []
        for t, (l0, n) in enumerate(spans):
            mine = outs[t].at[pl.ds(l0, n), 2 * x + y, c]
            cps += [pltpu.make_async_remote_copy(src_ref=mine, dst_ref=mine, send_sem=send_sems.at[3 * t + k],
                                                 recv_sem=recv_sems.at[3 * t + k], device_id=(cx, cy, c), device_id_type=MESH)
                    for k, (cx, cy) in enumerate(chips)]
        return cps

    return _Exchange(bufs, [_sds(b.shape, b.dtype) for b in bufs], 3 * len(bufs), copies, in_place=True)


def _gather_cores_exchange(bufs, spans):
    def copies(ins, outs, send_sems, recv_sems):
        x, y, c, chips = _place()
        cps = []
        for t, (l0, n) in enumerate(spans):
            for k, (cx, cy) in enumerate(chips):
                part = outs[t].at[pl.ds(l0, n), 2 * cx + cy, c]
                cps.append(pltpu.make_async_remote_copy(src_ref=part, dst_ref=part, send_sem=send_sems.at[3 * t + k],
                                                        recv_sem=recv_sems.at[3 * t + k], device_id=(x, y, 1 - c),
                                                        device_id_type=MESH))
        return cps

    return _Exchange(bufs, [_sds(b.shape, b.dtype) for b in bufs], 3 * len(bufs), copies, in_place=True)


def _core_halves_exchange(grads, spans):
    def copies(ins, outs, send_sems, recv_sems):
        x, y, c, _ = _place()
        return [pltpu.make_async_remote_copy(src_ref=ins[t].at[pl.ds(l0, n), :, 1 - c], dst_ref=outs[t],
                                             send_sem=send_sems.at[t], recv_sem=recv_sems.at[t], device_id=(x, y, 1 - c),
                                             device_id_type=MESH) for t, (l0, n) in enumerate(spans)]

    shapes = [_sds((n, g.shape[1], g.shape[3], g.shape[4]), F32) for g, (_, n) in zip(grads, spans)]
    return _Exchange(grads, shapes, len(grads), copies)


def _add_core_halves(name, g, a, c_idx, l0):
    _, nj, _, rh, X = g.shape
    L = a.shape[0]
    tr = _tile(rh, 256, BF16_ROWS)

    def body(c_ref, g_ref, a_ref, o_ref, ob_ref):
        s = g_ref[...] + a_ref[...]
        o_ref[...] = s
        ob_ref[...] = s.astype(BF16)

    blk = pl.BlockSpec((None, None, tr, X), lambda l, j, i, c_ref: (l, j, i, 0))
    return _pcall(
        body,
        grid_spec=pltpu.PrefetchScalarGridSpec(
            num_scalar_prefetch=1, grid=(L, nj, rh // tr),
            in_specs=[pl.BlockSpec((None, None, None, tr, X), lambda l, j, i, c_ref: (l + l0, j, c_ref[0], i, 0)), blk],
            out_specs=[blk, blk]),
        out_shape=[_sds((L, nj, rh, X), F32), _sds((L, nj, rh, X), BF16)],
        compiler_params=_cp("parallel", "parallel", "parallel"), name=name,
    )(c_idx, g, a)


def _chip_shards_exchange(parts):
    def copies(ins, outs, send_sems, recv_sems):
        x, y, c, chips = _place()
        return [pltpu.make_async_remote_copy(
            src_ref=ins[t].at[:, 2 * cx + cy], dst_ref=outs[t].at[k], send_sem=send_sems.at[3 * t + k],
            recv_sem=recv_sems.at[3 * t + k], device_id=(cx, cy, c), device_id_type=MESH)
            for t in range(len(parts)) for k, (cx, cy) in enumerate(chips)]

    shapes = [_sds((3, p.shape[0], p.shape[2], p.shape[3]), p.dtype) for p in parts]
    return _Exchange(parts, shapes, 3 * len(parts), copies)


def _add_chip_shards(name, p, b, jc_idx, l0, n_layers, buf):
    n, _, rh, X = p.shape
    tr = _tile(rh, 256, BF16_ROWS)

    def body(jc_ref, p_ref, b_ref, *rest):
        rest[-1][...] = ((p_ref[...] + b_ref[0].astype(F32)) + b_ref[1].astype(F32)) + b_ref[2].astype(F32)

    in_specs = [pl.BlockSpec((None, None, tr, X), lambda l, i, jc: (l, jc[0], i, 0)),
                pl.BlockSpec((3, None, tr, X), lambda l, i, jc: (0, l, i, 0))]
    operands = [jc_idx, p, b]
    if buf is not None:
        in_specs.append(ANY)
        operands.append(buf)
    return _pcall(
        body,
        grid_spec=pltpu.PrefetchScalarGridSpec(
            num_scalar_prefetch=1, grid=(n, rh // tr), in_specs=in_specs,
            out_specs=pl.BlockSpec((None, None, tr, X), lambda l, i, jc: (l + l0, jc[1], i, 0))),
        out_shape=_sds((n_layers, 2, rh, X), F32), input_output_aliases={3: 0} if buf is not None else {},
        compiler_params=_cp("parallel", "parallel"), name=name,
    )(*operands)


def _join_core_halves(bufs):
    n = len(bufs)

    def body(*refs):
        outs = refs[n:2 * n]
        send_sems, recv_sems = refs[2 * n:]
        x, y, c, _ = _place()
        cps = [pltpu.make_async_remote_copy(src_ref=outs[t].at[:, c], dst_ref=outs[t].at[:, c], send_sem=send_sems.at[t],
                                            recv_sem=recv_sems.at[t], device_id=(x, y, 1 - c), device_id_type=MESH)
               for t in range(n)]
        for cp in cps:
            cp.start()
        for t in range(n):
            pltpu.make_async_remote_copy(src_ref=outs[t].at[:, c], dst_ref=outs[t].at[:, 1 - c], send_sem=send_sems.at[t],
                                         recv_sem=recv_sems.at[t], device_id=(x, y, 1 - c), device_id_type=MESH).wait()

    outs = _pcall(
        body, in_specs=[ANY] * n, out_specs=[ANY] * n, out_shape=[_sds(b.shape, F32) for b in bufs],
        input_output_aliases={t: t for t in range(n)},
        scratch_shapes=[pltpu.SemaphoreType.DMA((n,)), pltpu.SemaphoreType.DMA((n,))],
        name="grad_join_core_halves",
    )(*bufs)
    return [o.reshape(o.shape[0], 2 * o.shape[2], o.shape[3]) for o in outs]


def _all_reduce_small(packed):
    R, C = packed.shape

    def body(x_ref, o_ref, slots, send_sems, recv_sems):
        x, y, c, _ = _place()
        me = 4 * x + 2 * y + c
        slots[me] = x_ref[...]
        cps = []
        for d in range(N_DEV):
            to = (d // 4, (d // 2) % 2, d % 2)
            cp = pltpu.make_async_remote_copy(src_ref=x_ref, dst_ref=slots.at[me], send_sem=send_sems.at[d],
                                              recv_sem=recv_sems.at[me], device_id=to, device_id_type=MESH)
            cps.append(cp)

            @pl.when(d != me)
            def _():
                cp.start()

        for d in range(N_DEV):
            @pl.when(d != me)
            def _():
                pltpu.make_async_remote_copy(src_ref=x_ref, dst_ref=slots.at[d], send_sem=send_sems.at[d],
                                             recv_sem=recv_sems.at[d], device_id=(x, y, c), device_id_type=MESH).wait_recv()
                cps[d].wait_send()

        acc = slots[0]
        for d in range(1, N_DEV):
            acc = acc + slots[d]
        o_ref[...] = acc

    vm = pl.BlockSpec(memory_space=pltpu.VMEM)
    return _pcall(
        body, in_specs=[vm], out_specs=vm, out_shape=_sds((R, C), F32),
        scratch_shapes=[pltpu.VMEM((N_DEV, R, C), F32), pltpu.SemaphoreType.DMA((N_DEV,)), pltpu.SemaphoreType.DMA((N_DEV,))],
        compiler_params=pltpu.CompilerParams(vmem_limit_bytes=VMEM_LIMIT_BYTES), name="all_reduce_small",
    )(packed)


PACK = SUBLANES * LANES


def _pack(arrays):
    flat = []
    for a in arrays:
        v = a.reshape(-1)
        flat.append(jnp.pad(v, (0, (-v.shape[0]) % PACK)))
    return jnp.concatenate(flat).reshape(-1, LANES)


def _unpack(packed, shapes):
    flat = packed.reshape(-1)
    out, pos = [], 0
    for s in shapes:
        n = 1
        for d in s:
            n *= d
        out.append(flat[pos:pos + n].reshape(s))
        pos += n + (-n) % PACK
    return out


def kernel(x, mix_norm_g, ffn_norm_g, conv_w_in, conv_a_dw_w, conv_a_dw_b, conv_a_ln_g, conv_a_ln_b, conv_b_dw_w, conv_w_out, attn_w_qkv, attn_q_g, attn_k_g, attn_w_o, ffn_w_up, ffn_dw_w, ffn_dw_b, ffn_w_down, loss_target, m_mix_norm_g, m_ffn_norm_g, m_conv_w_in, m_conv_a_dw_w, m_conv_a_dw_b, m_conv_a_ln_g, m_conv_a_ln_b, m_conv_b_dw_w, m_conv_w_out, m_attn_w_qkv, m_attn_q_g, m_attn_k_g, m_attn_w_o, m_ffn_w_up, m_ffn_dw_w, m_ffn_dw_b, m_ffn_w_down, v_mix_norm_g, v_ffn_norm_g, v_conv_w_in, v_conv_a_dw_w, v_conv_a_dw_b, v_conv_a_ln_g, v_conv_a_ln_b, v_conv_b_dw_w, v_conv_w_out, v_attn_w_qkv, v_attn_q_g, v_attn_k_g, v_attn_w_o, v_ffn_w_up, v_ffn_dw_w, v_ffn_dw_b, v_ffn_w_down):
    depth = mix_norm_g.shape[0]
    n_even, n_odd = conv_w_in.shape[0], attn_w_qkv.shape[0]
    S, D = x.shape[1], x.shape[2]
    dg = D // 2
    x0 = x.reshape(S, D)
    target = loss_target.reshape(S, D)
    j_me = 2 * lax.axis_index("x") + lax.axis_index("y")
    c_me = lax.axis_index("c")
    j_idx = j_me.astype(jnp.int32).reshape(1)
    c_idx = c_me.astype(jnp.int32).reshape(1)

    col_names = ["conv_w_in", "attn_w_qkv", "ffn_w_up"]
    row_names = ["conv_w_out", "attn_w_o", "ffn_w_down"]
    local = dict(conv_w_in=conv_w_in, attn_w_qkv=attn_w_qkv, ffn_w_up=ffn_w_up, conv_w_out=conv_w_out, attn_w_o=attn_w_o,
                 ffn_w_down=ffn_w_down)
    gbuf = {n: _place_shard(f"place_{n}", local[n], j_idx) for n in col_names + row_names}

    def weights_of(layer):
        mixer = ("conv_w_in", "conv_w_out") if layer % 2 == 0 else ("attn_w_qkv", "attn_w_o")
        return {mixer[0]: (layer // 2, 1), mixer[1]: (layer // 2, 1), "ffn_w_up": (layer, 1), "ffn_w_down": (layer, 1)}

    def w_col(n):
        return gbuf[n].reshape(gbuf[n].shape[0], N_CHIPS, -1, gbuf[n].shape[4])

    def w_row(n):
        return gbuf[n].reshape(gbuf[n].shape[0], -1, gbuf[n].shape[4])

    def carried(kernel_out, make_exchange, group):
        if not group:
            return kernel_out(None)
        out, *new = kernel_out(make_exchange([gbuf[n] for n in group], list(group.values())))
        gbuf.update(zip(group, new))
        return out

    first = {"conv_w_in": (0, 1), "conv_w_out": (0, 1)}
    ffn_first = {"ffn_w_up": (0, 1), "ffn_w_down": (0, 1)}
    outs, (a_dw, b_dw, f_dw) = _all_gather_weights([gbuf[n] for n in first], list(first.values()),
                                                   [conv_a_dw_w, conv_b_dw_w, ffn_dw_w])
    gbuf.update(zip(first, outs))
    unshard = lambda a: jnp.moveaxis(a, 1, 2).reshape(a.shape[0], a.shape[2], N_CHIPS * a.shape[3])
    a_dw, b_dw, f_dw = unshard(a_dw), unshard(b_dw), unshard(f_dw)
    qk_gain = [jnp.stack([jnp.tile(attn_q_g[i], LANES // HEAD_DIM), jnp.tile(attn_k_g[i], LANES // HEAD_DIM)])
               for i in range(n_odd)]

    saved = []
    xc = x0
    for layer in range(depth):
        i = layer // 2
        tag = f"l{layer}"
        s = {"x_in": xc}
        here = weights_of(layer) if layer else None
        h = carried(lambda ex: _rms_fwd(f"rms_mix_fwd_{tag}", xc, mix_norm_g, layer, ex), _gather_cores_exchange, here)
        s["h"] = h
        if layer % 2 == 0:
            p = _mm_fwd(f"conv_in_fwd_{tag}", h, w_col("conv_w_in"), i, colshard=True)
            ab = carried(lambda ex: _convmix_fwd(f"convmix_fwd_{tag}", p, a_dw, conv_a_dw_b, conv_a_ln_g, conv_a_ln_b, b_dw,
                                                 i, ex), _gather_chips_exchange, None if layer else ffn_first)
            xm = _mm_fwd(f"conv_out_fwd_{tag}", ab, w_row("conv_w_out"), i, colshard=False, res=xc)
            s.update(p=p, ab=ab)
        else:
            qkv = _mm_fwd(f"attn_qkv_fwd_{tag}", h, w_col("attn_w_qkv"), i, colshard=True)
            qs, kn, vb = _qknorm_fwd(f"qknorm_fwd_{tag}", qkv, qk_gain[i])
            o = _attn_fwd(f"attn_fwd_{tag}", qs, kn, vb)
            xm = _mm_fwd(f"attn_out_fwd_{tag}", o, w_row("attn_w_o"), i, colshard=False, res=xc)
            s.update(qkv=qkv, qs=qs, kn=kn, vb=vb, o=o)
        s["x_mid"] = xm
        h2 = carried(lambda ex: _rms_fwd(f"rms_ffn_fwd_{tag}", xm, ffn_norm_g, layer, ex), _gather_cores_exchange,
                     None if layer else ffn_first)
        u2 = _mm_fwd(f"ffn_up_fwd_{tag}", h2, w_col("ffn_w_up"), layer, colshard=True, out_split=2)
        f = carried(lambda ex: _ffn_mid_fwd(f"ffn_mid_fwd_{tag}", u2, f_dw, ffn_dw_b, layer, ex), _gather_chips_exchange,
                    weights_of(layer + 1) if layer + 1 < depth else None)
        xc = _mm_fwd(f"ffn_down_fwd_{tag}", f, w_row("ffn_w_down"), layer, colshard=False, res=xm)
        s.update(h2=h2, u2=u2, f=f)
        saved.append(s)

    dx, loss_tile = _loss_fwd_bwd("loss", xc, target)

    w_in, w_qkv, w_up = w_col("conv_w_in"), w_col("attn_w_qkv"), w_col("ffn_w_up")
    w_out, w_o, w_down = w_row("conv_w_out"), w_row("attn_w_o"), w_row("ffn_w_down")
    g_up = g_down = g_in = g_out = g_qkv = g_o = None
    big_names = col_names + row_names

    def halves_view(n, g):
        if n in col_names:
            return g.reshape(g.shape[0], N_CHIPS, 2, g.shape[2] // 2, g.shape[3])
        return g.reshape(g.shape[0], N_CHIPS, 2, g.shape[1] // (2 * N_CHIPS), g.shape[2])

    ffn_of_0 = {"ffn_w_up": (0, 1), "ffn_w_down": (0, 1)}
    mixer_of_0 = {"conv_w_in": (0, 1), "conv_w_out": (0, 1)}
    summed_parts = {n: [] for n in big_names}

    def stacks():
        return {"conv_w_in": g_in, "attn_w_qkv": g_qkv, "ffn_w_up": g_up, "conv_w_out": g_out, "attn_w_o": g_o,
                "ffn_w_down": g_down}

    def core_exchange(group):
        return _core_halves_exchange([halves_view(n, stacks()[n]) for n in group], list(group.values()))

    def chip_exchange(tag, arrived):
        sums, parts = [], []
        for group, from_sibling in arrived:
            for n, a in zip(group, from_sibling):
                f32_sum, bf16_sum = _add_core_halves(f"grad_add_core_{n}_{tag}_{group[n][0]}", halves_view(n, stacks()[n]), a,
                                                     c_idx, group[n][0])
                sums.append((n, group[n][0], f32_sum))
                parts.append(bf16_sum)
        return _chip_shards_exchange(parts), sums

    def record(sums, from_chips):
        for (n, l0, f32_sum), b in zip(sums, from_chips):
            summed_parts[n].append((l0, f32_sum, b))

    d_mix_g, d_ffn_g = [None] * depth, [None] * depth
    d_ffn_dw_w, d_ffn_dw_b = [None] * depth, [None] * depth
    d_a_dw_w, d_a_dw_b, d_a_ln_g, d_a_ln_b, d_b_dw_w = ([None] * n_even for _ in range(5))
    d_q_g, d_k_g = [None] * n_odd, [None] * n_odd
    for layer in reversed(range(depth)):
        i = layer // 2
        tag = f"l{layer}"
        s = saved[layer]
        df = _mm_dgrad(f"ffn_down_dgrad_{tag}", dx, w_down, layer, colshard=False)
        g_down = _mm_wgrad(f"ffn_down_wgrad_{tag}", s["f"], dx, layer, depth, g_down, colshard=False)
        above = weights_of(layer + 1) if layer + 1 < depth else None
        arrived = []
        du2, dww, dwb, *from_sibling = _ffn_mid_bwd(f"ffn_mid_bwd_{tag}", s["u2"], df, f_dw, ffn_dw_b, layer,
                                                    core_exchange(above) if above else None)
        if above:
            arrived.append((above, from_sibling))
        d_ffn_dw_w[layer] = jnp.moveaxis(dww, 0, 1).reshape(FFN_CONV_WIDTH, -1)
        d_ffn_dw_b[layer] = dwb.reshape(-1)
        dh2 = _mm_dgrad(f"ffn_up_dgrad_{tag}", du2, w_up, layer, colshard=True)
        g_up = _mm_wgrad(f"ffn_up_wgrad_{tag}", s["h2"], du2, layer, depth, g_up, colshard=True)
        dx, dg_, *from_sibling = _rms_bwd(f"rms_ffn_bwd_{tag}", s["x_mid"], ffn_norm_g, layer, dh2, dx,
                                          core_exchange(ffn_of_0) if layer == 0 else None)
        if layer == 0:
            arrived.append((ffn_of_0, from_sibling))
        d_ffn_g[layer] = dg_.reshape(-1)
        if layer % 2 == 0:
            dab = _mm_dgrad(f"conv_out_dgrad_{tag}", dx, w_out, i, colshard=False)
            g_out = _mm_wgrad(f"conv_out_wgrad_{tag}", s["ab"], dx, i, n_even, g_out, colshard=False)
            chip_ex, sums = chip_exchange(tag, arrived) if arrived else (None, [])
            dp, daw, dab_b, dlg, dlb, dbw, *from_chips = _convmix_bwd(
                f"convmix_bwd_{tag}", s["p"], dab, a_dw, conv_a_dw_b, conv_a_ln_g, conv_a_ln_b, b_dw, i, chip_ex)
            record(sums, from_chips)
            d_a_dw_w[i], d_a_dw_b[i], d_a_ln_g[i], d_a_ln_b[i], d_b_dw_w[i] = (
                daw, dab_b.reshape(-1), dlg.reshape(-1), dlb.reshape(-1), dbw)
            dh = _mm_dgrad(f"conv_in_dgrad_{tag}", dp, w_in, i, colshard=True)
            g_in = _mm_wgrad(f"conv_in_wgrad_{tag}", s["h"], dp, i, n_even, g_in, colshard=True)
        else:
            do = _mm_dgrad(f"attn_out_dgrad_{tag}", dx, w_o, i, colshard=False)
            g_o = _mm_wgrad(f"attn_out_wgrad_{tag}", s["o"], dx, i, n_odd, g_o, colshard=False)
            chip_ex, sums = chip_exchange(tag, arrived) if arrived else (None, [])
            dq, dk, dv, *from_chips = _attn_bwd(f"attn_bwd_{tag}", s["qs"], s["kn"], s["vb"], s["o"], do, chip_ex)
            record(sums, from_chips)
            dqkv, dgain = _qknorm_bwd(f"qknorm_bwd_{tag}", s["qkv"], dq, dk, dv, qk_gain[i])
            d_q_g[i] = dgain[0, :HEAD_DIM] + dgain[0, HEAD_DIM:]
            d_k_g[i] = dgain[1, :HEAD_DIM] + dgain[1, HEAD_DIM:]
            dh = _mm_dgrad(f"attn_qkv_dgrad_{tag}", dqkv, w_qkv, i, colshard=True)
            g_qkv = _mm_wgrad(f"attn_qkv_wgrad_{tag}", s["h"], dqkv, i, n_odd, g_qkv, colshard=True)
        dx, dg_ = _rms_bwd(f"rms_mix_bwd_{tag}", s["x_in"], mix_norm_g, layer, dh, dx)
        d_mix_g[layer] = dg_.reshape(-1)
    grad_x = dx.reshape(1, S, D)

    small = {
        "mix_norm_g": jnp.stack(d_mix_g), "ffn_norm_g": jnp.stack(d_ffn_g),
        "conv_a_dw_w": jnp.stack(d_a_dw_w), "conv_a_dw_b": jnp.stack(d_a_dw_b),
        "conv_a_ln_g": jnp.stack(d_a_ln_g), "conv_a_ln_b": jnp.stack(d_a_ln_b),
        "conv_b_dw_w": jnp.stack(d_b_dw_w), "attn_q_g": jnp.stack(d_q_g), "attn_k_g": jnp.stack(d_k_g),
        "ffn_dw_w": jnp.stack(d_ffn_dw_w), "ffn_dw_b": jnp.stack(d_ffn_dw_b),
    }
    small_names = list(small)
    summed = _all_reduce_small(_pack([loss_tile] + [small[n] for n in small_names]))
    parts = _unpack(summed, [loss_tile.shape] + [small[n].shape for n in small_names])
    loss = parts[0][0, 0]
    small_g = dict(zip(small_names, parts[1:]))
    for n in ("conv_a_dw_w", "conv_b_dw_w", "ffn_dw_w"):
        cs = small_g[n].shape[2] // N_CHIPS
        small_g[n] = lax.dynamic_slice_in_dim(small_g[n], j_me * cs, cs, axis=2)

    from_sibling = _run_exchange("grad_exchange_core_halves", core_exchange(mixer_of_0))
    chip_ex, sums = chip_exchange("last", [(mixer_of_0, from_sibling)])
    record(sums, _run_exchange("grad_exchange_chip_shards", chip_ex))
    jc_idx = jnp.concatenate([j_idx, c_idx])
    totals = {}
    for n in big_names:
        total = None
        for l0, p, b in summed_parts[n]:
            total = _add_chip_shards(f"grad_add_chips_{n}_{l0}", p, b, jc_idx, l0, stacks()[n].shape[0], total)
        totals[n] = total
    big_g = dict(zip(big_names, _join_core_halves([totals[n] for n in big_names])))

    weights = dict(mix_norm_g=mix_norm_g, ffn_norm_g=ffn_norm_g, conv_w_in=conv_w_in, conv_a_dw_w=conv_a_dw_w, conv_a_dw_b=conv_a_dw_b, conv_a_ln_g=conv_a_ln_g, conv_a_ln_b=conv_a_ln_b, conv_b_dw_w=conv_b_dw_w, conv_w_out=conv_w_out, attn_w_qkv=attn_w_qkv, attn_q_g=attn_q_g, attn_k_g=attn_k_g, attn_w_o=attn_w_o, ffn_w_up=ffn_w_up, ffn_dw_w=ffn_dw_w, ffn_dw_b=ffn_dw_b, ffn_w_down=ffn_w_down)
    m_in = dict(mix_norm_g=m_mix_norm_g, ffn_norm_g=m_ffn_norm_g, conv_w_in=m_conv_w_in, conv_a_dw_w=m_conv_a_dw_w, conv_a_dw_b=m_conv_a_dw_b, conv_a_ln_g=m_conv_a_ln_g, conv_a_ln_b=m_conv_a_ln_b, conv_b_dw_w=m_conv_b_dw_w, conv_w_out=m_conv_w_out, attn_w_qkv=m_attn_w_qkv, attn_q_g=m_attn_q_g, attn_k_g=m_attn_k_g, attn_w_o=m_attn_w_o, ffn_w_up=m_ffn_w_up, ffn_dw_w=m_ffn_dw_w, ffn_dw_b=m_ffn_dw_b, ffn_w_down=m_ffn_w_down)
    v_in = dict(mix_norm_g=v_mix_norm_g, ffn_norm_g=v_ffn_norm_g, conv_w_in=v_conv_w_in, conv_a_dw_w=v_conv_a_dw_w, conv_a_dw_b=v_conv_a_dw_b, conv_a_ln_g=v_conv_a_ln_g, conv_a_ln_b=v_conv_a_ln_b, conv_b_dw_w=v_conv_b_dw_w, conv_w_out=v_conv_w_out, attn_w_qkv=v_attn_w_qkv, attn_q_g=v_attn_q_g, attn_k_g=v_attn_k_g, attn_w_o=v_attn_w_o, ffn_w_up=v_ffn_w_up, ffn_dw_w=v_ffn_dw_w, ffn_dw_b=v_ffn_dw_b, ffn_w_down=v_ffn_w_down)
    order = list(weights)
    grads, delta, new_m, new_v = {}, {}, {}, {}
    for n in big_names:
        grads[n] = big_g[n]
        delta[n], new_m[n], new_v[n] = _adamw(f"adamw_{n}", weights[n], big_g[n], m_in[n], v_in[n])
    shapes = [weights[n].shape for n in small_names]
    packed = [_pack([d[n] for n in small_names]) for d in (weights, small_g, m_in, v_in)]
    upd = _adamw("adamw_small", *[p[None] for p in packed])
    for out, res in zip((delta, new_m, new_v), upd):
        out.update(zip(small_names, _unpack(res[0], shapes)))
    grads.update({n: small_g[n].reshape(weights[n].shape) for n in small_names})
    return (loss, grad_x, *[grads[n] for n in order], *[delta[n] for n in order], *[new_m[n] for n in order],
            *[new_v[n] for n in order])
```

```python
import jax
import jax.numpy as jnp
from jax import lax
from jax.experimental import pallas as pl
from jax.experimental.pallas import tpu as pltpu

F32 = jnp.float32
BF16 = jnp.bfloat16
EPS = 1e-6
CONV_A_WIDTH = 31
CONV_B_WIDTH = 3
FFN_CONV_WIDTH = 3
HEAD_DIM = 64
ADAM_LR = 0.001
ADAM_B1 = 0.9
ADAM_B2 = 0.999
ADAM_EPS = 1e-08
ADAM_WD = 0.01
ADAM_STEP = 10

LANES = 128
SUBLANES = 8
BF16_ROWS = 16
V7X_VMEM_BYTES = 64 * 1024 * 1024
VMEM_LIMIT_BYTES = V7X_VMEM_BYTES * 3 // 4
MM_VMEM_BUDGET = VMEM_LIMIT_BYTES * 4 // 5
MM_ROWS = 1024
N_CHIPS = 4
N_DEV = 8
HALO_A = 32
HALO_S = 8
ELT_ROWS = 64
FFN_MID_ROWS = 512
ATTN_BLOCK = 128
ATTN_SUB = 2
ATTN_MORE = 2
EXP_UNDERFLOW = -104.0
MESH = pl.DeviceIdType.MESH
ANY = pl.BlockSpec(memory_space=pl.ANY)
NT = (((1,), (1,)), ((), ()))
NN = (((1,), (0,)), ((), ()))
TN = (((0,), (0,)), ((), ()))


def _pcall(body, **kw):
    return pl.pallas_call(body, **kw)


def _cp(*sem):
    return pltpu.CompilerParams(dimension_semantics=sem, vmem_limit_bytes=VMEM_LIMIT_BYTES)


def _sds(shape, dtype):
    return jax.ShapeDtypeStruct(tuple(shape), dtype)


def _tile(n, cap, align=LANES):
    if n <= cap:
        return n
    for t in range(cap - cap % align, 0, -align):
        if n % t == 0:
            return t
    return n


def _sig(x):
    return 0.5 * jnp.tanh(0.5 * x) + 0.5


def _rowsum(x):
    return jnp.sum(x, axis=0, keepdims=True)


def _fold(x):
    acc = x[0:SUBLANES]
    for r in range(SUBLANES, x.shape[0], SUBLANES):
        acc = acc + x[r:r + SUBLANES]
    return acc


def _with_exchange(ex, body, in_specs, out_specs, out_shape, scratch, operands, first, last):
    if ex is None:
        return body, in_specs, out_specs, out_shape, scratch, operands, {}
    n_in, n_out, n_scr = len(in_specs), len(out_specs), len(scratch)
    e_in, e_out = len(ex.operands), len(ex.out_shapes)

    def hosted(*refs):
        refs = list(refs)
        ins, refs = refs[:n_in], refs[n_in:]
        e_ins, refs = refs[:e_in], refs[e_in:]
        outs, refs = refs[:n_out], refs[n_out:]
        e_outs, refs = refs[:e_out], refs[e_out:]
        scr, sems = refs[:n_scr], refs[n_scr:]

        @pl.when(first())
        def _():
            ex.start(e_ins, e_outs, sems)

        body(*ins, *outs, *scr)

        @pl.when(last())
        def _():
            ex.wait(e_ins, e_outs, sems)

    return (hosted, in_specs + [ANY] * e_in, out_specs + [ANY] * e_out, out_shape + ex.out_shapes, scratch + ex.scratch,
            operands + ex.operands, {n_in + i: n_out + o for i, o in ex.aliases.items()})


def _mm_call(name, dn, operands, in_specs, out_shape, out_spec, grid, nk, acc_shape, has_res, has_alias):
    def body(*refs):
        a_ref, b_ref = refs[0], refs[1]
        pos = 2
        res_ref = refs[pos] if has_res else None
        pos += int(has_res) + int(has_alias)
        o_ref = refs[pos]
        acc_ref = refs[pos + 1] if nk > 1 else None
        p = lax.dot_general(a_ref[...].astype(BF16), b_ref[...].astype(BF16), dn, preferred_element_type=F32)

        def finish(v):
            if has_res:
                v = v + res_ref[...]
            o_ref[...] = v.astype(o_ref.dtype)

        if nk == 1:
            finish(p)
        else:
            k = pl.program_id(2)

            @pl.when(k == 0)
            def _():
                acc_ref[...] = p

            @pl.when(k > 0)
            def _():
                acc_ref[...] += p

            @pl.when(k == nk - 1)
            def _():
                finish(acc_ref[...])

    aliases = {len(operands) - 1: 0} if has_alias else {}
    return _pcall(
        body, grid=grid, in_specs=in_specs, out_specs=out_spec, out_shape=out_shape,
        scratch_shapes=[pltpu.VMEM(acc_shape, F32)] if nk > 1 else [],
        input_output_aliases=aliases, compiler_params=_cp("parallel", "parallel", "arbitrary"), name=name,
    )(*operands)


def _mm_fwd(name, a, w, l, *, colshard, res=None, out_split=1):
    M, K = a.shape
    tm = _tile(M, MM_ROWS, BF16_ROWS)
    if colshard and out_split == 1 and res is None:
        cs = w.shape[3]
        th = _tile(M, MM_ROWS // 2, BF16_ROWS)
        if 2 * (N_CHIPS * K * cs * 2 + th * N_CHIPS * cs * 4 + th * K * a.dtype.itemsize) <= MM_VMEM_BUDGET:
            def body(a_ref, b_ref, o_ref):
                av = a_ref[...].astype(BF16)
                for j in range(N_CHIPS):
                    o_ref[:, j * cs:(j + 1) * cs] = jnp.dot(av, b_ref[j], preferred_element_type=F32)

            return _pcall(
                body, grid=(M // th,),
                in_specs=[pl.BlockSpec((th, K), lambda i: (i, 0)), pl.BlockSpec((None, N_CHIPS, K, cs), lambda i: (l, 0, 0, 0))],
                out_specs=pl.BlockSpec((th, N_CHIPS * cs), lambda i: (i, 0)), out_shape=_sds((M, N_CHIPS * cs), F32),
                compiler_params=_cp("parallel"), name=name,
            )(a, w)
    if colshard:
        cs = w.shape[3]
        N, tn, tk = N_CHIPS * cs, cs, K
        b_spec = pl.BlockSpec((None, None, tk, tn), lambda j, i, k: (l, j, k, 0))
    else:
        N = w.shape[2]
        tn, tk = _tile(N, 1024), K
        if K > 1536:
            tm = _tile(M, MM_ROWS // 2, BF16_ROWS)
        b_spec = pl.BlockSpec((None, tk, tn), lambda j, i, k: (l, k, j))
    nk = K // tk
    in_specs = [pl.BlockSpec((tm, tk), lambda j, i, k: (i, k)), b_spec]
    operands = [a, w]
    if res is not None:
        in_specs.append(pl.BlockSpec((tm, tn), lambda j, i, k: (i, j)))
        operands.append(res)
    if out_split == 1:
        out_shape = _sds((M, N), F32)
        out_spec = pl.BlockSpec((tm, tn), lambda j, i, k: (i, j))
    else:
        per = N // tn // out_split
        out_shape = _sds((out_split, M, N // out_split), F32)
        out_spec = pl.BlockSpec((None, tm, tn), lambda j, i, k: (j // per, i, j % per))
    return _mm_call(name, NN, operands, in_specs, out_shape, out_spec, (N // tn, M // tm, nk), nk, (tm, tn),
                    res is not None, False)


def _mm_dgrad(name, g, w, l, *, colshard):
    split = g.ndim == 3
    M = g.shape[-2]
    tm = _tile(M, MM_ROWS, BF16_ROWS)
    if colshard:
        kw, cs = w.shape[2], w.shape[3]
        tm = _tile(M, MM_ROWS // 2, BF16_ROWS)
        per = N_CHIPS // g.shape[0] if split else N_CHIPS

        def body(a_ref, b_ref, o_ref):
            acc = None
            for j in range(N_CHIPS):
                cols = slice((j % per) * cs, (j % per + 1) * cs)
                a = a_ref[j // per, :, cols] if split else a_ref[:, cols]
                p = lax.dot_general(a.astype(BF16), b_ref[j], NT, preferred_element_type=F32)
                acc = p if acc is None else acc + p
            o_ref[...] = acc

        a_spec = (pl.BlockSpec((g.shape[0], tm, g.shape[2]), lambda i: (0, i, 0)) if split
                  else pl.BlockSpec((tm, N_CHIPS * cs), lambda i: (i, 0)))
        return _pcall(
            body, grid=(M // tm,),
            in_specs=[a_spec, pl.BlockSpec((None, N_CHIPS, kw, cs), lambda i: (l, 0, 0, 0))],
            out_specs=pl.BlockSpec((tm, kw), lambda i: (i, 0)), out_shape=_sds((M, kw), F32),
            compiler_params=_cp("parallel"), name=name,
        )(g, w)
    else:
        kw, ncon = w.shape[1], w.shape[2]
        tn, tk = _tile(kw, 1408), _tile(ncon, 1536)
        nk = ncon // tk
        th = _tile(M, MM_ROWS // 2, BF16_ROWS)
        if nk == 1 and 2 * (kw * ncon * w.dtype.itemsize + th * kw * 4 + th * ncon * g.dtype.itemsize) <= MM_VMEM_BUDGET:
            tm, tn = th, kw
        b_spec = pl.BlockSpec((None, tn, tk), lambda j, i, k: (l, j, k))
    if split:
        per = nk // g.shape[0]
        a_spec = pl.BlockSpec((None, tm, tk), lambda j, i, k: (k // per, i, k % per))
    else:
        a_spec = pl.BlockSpec((tm, tk), lambda j, i, k: (i, k))
    out_shape = _sds((M, kw), F32)
    out_spec = pl.BlockSpec((tm, tn), lambda j, i, k: (i, j))
    return _mm_call(name, NT, [g, w], [a_spec, b_spec], out_shape, out_spec, (kw // tn, M // tm, nk), nk, (tm, tn),
                    False, False)


def _mm_wgrad(name, a, g, l, n_layers, buf, *, colshard):
    S, M = a.shape
    split = g.ndim == 3
    N = g.shape[-1] * (g.shape[0] if split else 1)
    tm = _tile(M, 1408)
    tn = N // N_CHIPS if colshard else _tile(N, 1024)
    per_row = 2 * (tm * a.dtype.itemsize + tn * g.dtype.itemsize)
    tk = _tile(S, max(BF16_ROWS, min(2048, (MM_VMEM_BUDGET - 3 * tm * tn * 4) // per_row)), BF16_ROWS)
    nk = S // tk
    if colshard:
        out_shape = _sds((n_layers, N_CHIPS, M, tn), F32)
        out_spec = pl.BlockSpec((None, None, tm, tn), lambda j, i, k: (l, j, i, 0))
    else:
        out_shape = _sds((n_layers, M, N), F32)
        out_spec = pl.BlockSpec((None, tm, tn), lambda j, i, k: (l, i, j))
    if split:
        per = N // tn // g.shape[0]
        b_spec = pl.BlockSpec((None, tk, tn), lambda j, i, k: (j // per, k, j % per))
    else:
        b_spec = pl.BlockSpec((tk, tn), lambda j, i, k: (k, j))
    in_specs = [pl.BlockSpec((tk, tm), lambda j, i, k: (k, i)), b_spec]
    operands = [a, g]
    if buf is not None:
        in_specs.append(ANY)
        operands.append(buf)
    return _mm_call(name, TN, operands, in_specs, out_shape, out_spec, (N // tn, M // tm, nk), nk, (tm, tn),
                    False, buf is not None)


def _rms_fwd(name, x, g, l, exchange=None):
    S, D = x.shape
    tm = _tile(S, 512, BF16_ROWS)
    n_i = S // tm

    def body(x_ref, g_ref, o_ref):
        xf = x_ref[...]
        r = lax.rsqrt(jnp.mean(xf * xf, axis=-1, keepdims=True) + EPS)
        o_ref[...] = (xf * r * g_ref[l:l + 1, :]).astype(BF16)

    body, in_specs, out_specs, out_shape, scratch, operands, aliases = _with_exchange(
        exchange, body, [pl.BlockSpec((tm, D), lambda i: (i, 0)), pl.BlockSpec(g.shape, lambda i: (0, 0))],
        [pl.BlockSpec((tm, D), lambda i: (i, 0))], [_sds((S, D), BF16)], [], [x, g],
        lambda: pl.program_id(0) == 0, lambda: pl.program_id(0) == n_i - 1)
    outs = _pcall(
        body, grid=(n_i,), in_specs=in_specs, out_specs=out_specs, out_shape=out_shape, scratch_shapes=scratch,
        input_output_aliases=aliases, compiler_params=_cp("arbitrary" if exchange else "parallel"), name=name,
    )(*operands)
    return outs if exchange else outs[0]


def _rms_bwd(name, x, g, l, dh, dres, exchange=None):
    S, D = x.shape
    tm = _tile(S, 512, SUBLANES)

    def body(x_ref, g_ref, dh_ref, dr_ref, dx_ref, dg_ref):
        xf = x_ref[...]
        r = lax.rsqrt(jnp.mean(xf * xf, axis=-1, keepdims=True) + EPS)
        xh = xf * r
        d = dh_ref[...]
        dxh = d * g_ref[l:l + 1, :]
        dx_ref[...] = dr_ref[...] + r * (dxh - xh * jnp.mean(dxh * xh, axis=-1, keepdims=True))

        @pl.when(pl.program_id(0) == 0)
        def _():
            dg_ref[...] = jnp.zeros_like(dg_ref)

        dg_ref[...] += _rowsum(d * xh)

    row = pl.BlockSpec((tm, D), lambda i: (i, 0))
    n_i = S // tm
    body, in_specs, out_specs, out_shape, scratch, operands, aliases = _with_exchange(
        exchange, body, [row, pl.BlockSpec(g.shape, lambda i: (0, 0)), row, row],
        [row, pl.BlockSpec((1, D), lambda i: (0, 0))], [_sds((S, D), F32), _sds((1, D), F32)], [], [x, g, dh, dres],
        lambda: pl.program_id(0) == 0, lambda: pl.program_id(0) == n_i - 1)
    return _pcall(
        body, grid=(n_i,), in_specs=in_specs, out_specs=out_specs, out_shape=out_shape, scratch_shapes=scratch,
        input_output_aliases=aliases, compiler_params=_cp("arbitrary"), name=name,
    )(*operands)


def _loss_fwd_bwd(name, y, t):
    S, D = y.shape
    tm = _tile(S, 512, SUBLANES)

    def body(y_ref, t_ref, dy_ref, l_ref):
        e = y_ref[...] - t_ref[...]
        dy_ref[...] = e * (1.0 / D)

        @pl.when(pl.program_id(0) == 0)
        def _():
            l_ref[...] = jnp.zeros_like(l_ref)

        l_ref[...] += 0.5 * jnp.sum(jnp.sum(e * e, axis=-1, keepdims=True) * (1.0 / D), axis=0, keepdims=True)

    row = pl.BlockSpec((tm, D), lambda i: (i, 0))
    return _pcall(
        body, grid=(S // tm,), in_specs=[row, row],
        out_specs=[row, pl.BlockSpec((SUBLANES, LANES), lambda i: (0, 0))],
        out_shape=[_sds((S, D), F32), _sds((SUBLANES, LANES), F32)],
        compiler_params=_cp("arbitrary"), name=name,
    )(y, t)


def _delayed_copies(us, n_rows):
    for s in range(1, SUBLANES):
        us[s, pl.ds(SUBLANES, n_rows - SUBLANES), :] = us[0, pl.ds(SUBLANES - s, n_rows - SUBLANES), :]


def _conv_a(aw_ref, ab_ref, l, us, row0, rows, dg):
    ka = CONV_A_WIDTH
    out = []
    for c0 in range(0, dg, LANES):
        lanes = slice(c0, c0 + LANES)
        acc = ab_ref[l:l + 1, lanes]
        for d in range(ka):
            a, s = divmod(d, SUBLANES)
            acc = acc + aw_ref[l, ka - 1 - d:ka - d, lanes] * us[s, pl.ds(row0 - SUBLANES * a, rows), lanes]
        out.append(acc)
    return jnp.concatenate(out, axis=1)


def _convmix_fwd(name, p, aw, ab, lg, lb, bw, l, exchange=None):
    S, W = p.shape
    dg = W // 5
    tm = _tile(S, 256, HALO_A)
    nb = tm // HALO_A
    ka, kb = CONV_A_WIDTH, CONV_B_WIDTH

    ext = HALO_A + tm
    rc = _tile(tm, ELT_ROWS, BF16_ROWS)

    def body(p_ref, ph_ref, aw_ref, ab_ref, lg_ref, lb_ref, bw_ref, o_ref, us, mext):
        first = pl.program_id(0) == 0
        ph = ph_ref[...]
        pc = p_ref[...]
        us[0, pl.ds(0, HALO_A), :] = jnp.where(first, 0.0, ph[:, 0:dg] * _sig(ph[:, dg:2 * dg]))
        us[0, pl.ds(HALO_A, tm), :] = pc[:, 0:dg] * _sig(pc[:, dg:2 * dg])
        mext[pl.ds(0, HALO_A), :] = jnp.where(first, 0.0, ph[:, 3 * dg:4 * dg] * ph[:, 4 * dg:5 * dg])
        mext[pl.ds(HALO_A, tm), :] = pc[:, 3 * dg:4 * dg] * pc[:, 4 * dg:5 * dg]
        _delayed_copies(us, ext)
        for r0 in range(0, tm, rc):
            rows = pl.ds(r0, rc)
            c = _conv_a(aw_ref, ab_ref, l, us, HALO_A + r0, rc, dg)
            xc = c - jnp.mean(c, axis=-1, keepdims=True)
            ln = xc * lax.rsqrt(jnp.mean(xc * xc, axis=-1, keepdims=True) + EPS) * lg_ref[l:l + 1, :] + lb_ref[l:l + 1, :]
            o_ref[rows, 0:dg] = (ln * _sig(ln)).astype(BF16)
            cb = bw_ref[l, 0:1, :] * mext[pl.ds(HALO_A - (kb - 1) + r0, rc), :]
            for k in range(1, kb):
                cb = cb + bw_ref[l, k:k + 1, :] * mext[pl.ds(HALO_A - (kb - 1) + k + r0, rc), :]
            o_ref[rows, dg:2 * dg] = (p_ref[rows, 2 * dg:3 * dg] * cb).astype(BF16)

    full = lambda a: pl.BlockSpec(a.shape, lambda i: (0,) * a.ndim)
    n_i = S // tm
    body, in_specs, out_specs, out_shape, scratch, operands, aliases = _with_exchange(
        exchange, body,
        [pl.BlockSpec((tm, W), lambda i: (i, 0)), pl.BlockSpec((HALO_A, W), lambda i: (jnp.maximum(i * nb - 1, 0), 0)),
         full(aw), full(ab), full(lg), full(lb), full(bw)],
        [pl.BlockSpec((tm, 2 * dg), lambda i: (i, 0))], [_sds((S, 2 * dg), BF16)],
        [pltpu.VMEM((SUBLANES, ext, dg), F32), pltpu.VMEM((ext, dg), F32)], [p, p, aw, ab, lg, lb, bw],
        lambda: pl.program_id(0) == 0, lambda: pl.program_id(0) == n_i - 1)
    outs = _pcall(
        body, grid=(n_i,), in_specs=in_specs, out_specs=out_specs, out_shape=out_shape, scratch_shapes=scratch,
        input_output_aliases=aliases, compiler_params=_cp("arbitrary" if exchange else "parallel"), name=name,
    )(*operands)
    return outs if exchange else outs[0]


def _convmix_bwd(name, p, dab, aw, ab, lg, lb, bw, l, exchange=None):
    S, W = p.shape
    dg = W // 5
    tm = _tile(S, 256, HALO_A)
    nb = tm // HALO_A
    n_i = S // tm
    ka, kb = CONV_A_WIDTH, CONV_B_WIDTH
    n = tm + HALO_A
    ext = HALO_A + n
    rc = _tile(tm, ELT_ROWS, BF16_ROWS)

    def body(p_ref, pp_ref, pn_ref, d_ref, dn_ref, aw_ref, ab_ref, lg_ref, lb_ref, bw_ref,
             dp_ref, daw_ref, dab_ref, dlg_ref, dlb_ref, dbw_ref, us, mext, dcs, dbext, accw):
        i = pl.program_id(0)
        first, last = i == 0, i == n_i - 1

        @pl.when(first)
        def _():
            for r in (daw_ref, dab_ref, dlg_ref, dlb_ref, dbw_ref):
                r[...] = jnp.zeros_like(r)

        accw[...] = jnp.zeros_like(accw)
        pp, pc, pn = pp_ref[...], p_ref[...], pn_ref[...]
        glu = lambda b: b[:, 0:dg] * _sig(b[:, dg:2 * dg])
        gch = lambda b: b[:, 3 * dg:4 * dg] * b[:, 4 * dg:5 * dg]
        us[0, pl.ds(0, HALO_A), :] = jnp.where(first, 0.0, glu(pp))
        us[0, pl.ds(HALO_A, tm), :] = glu(pc)
        us[0, pl.ds(HALO_A + tm, HALO_A), :] = glu(pn)
        mext[pl.ds(0, HALO_A), :] = jnp.where(first, 0.0, gch(pp))
        mext[pl.ds(HALO_A, tm), :] = gch(pc)
        mext[pl.ds(HALO_A + tm, HALO_A), :] = gch(pn)
        _delayed_copies(us, ext)
        chunks = [(r, rc) for r in range(0, tm, rc)] + [(tm, HALO_A)]
        g_ln = lg_ref[l:l + 1, :]
        zero8 = jnp.zeros((SUBLANES, dg), F32)

        acc_lg = acc_lb = acc_ab = zero8
        for r0, rows in chunks:
            c = _conv_a(aw_ref, ab_ref, l, us, HALO_A + r0, rows, dg)
            xc = c - jnp.mean(c, axis=-1, keepdims=True)
            rstd = lax.rsqrt(jnp.mean(xc * xc, axis=-1, keepdims=True) + EPS)
            chat = xc * rstd
            ln = chat * g_ln + lb_ref[l:l + 1, :]
            s = _sig(ln)
            da = d_ref[pl.ds(r0, rows), 0:dg] if r0 < tm else jnp.where(last, 0.0, dn_ref[:, 0:dg])
            dln = da * (s * (1.0 + ln * (1.0 - s)))
            dlnh = dln * g_ln
            dc = rstd * (dlnh - jnp.mean(dlnh, axis=-1, keepdims=True)
                         - chat * jnp.mean(dlnh * chat, axis=-1, keepdims=True))
            dcs[0, pl.ds(r0, rows), :] = dc
            if r0 < tm:
                acc_lg = acc_lg + _fold(dln * chat)
                acc_lb = acc_lb + _fold(dln)
                acc_ab = acc_ab + _fold(dc)
                for c0 in range(0, dg, LANES):
                    lanes = slice(c0, c0 + LANES)
                    for d in range(ka):
                        a, sh = divmod(d, SUBLANES)
                        k = ka - 1 - d
                        accw[pl.ds(SUBLANES * k, SUBLANES), lanes] += _fold(
                            dc[:, lanes] * us[sh, pl.ds(HALO_A + r0 - SUBLANES * a, rows), lanes])
        dlg_ref[...] += _rowsum(acc_lg)
        dlb_ref[...] += _rowsum(acc_lb)
        dab_ref[...] += _rowsum(acc_ab)
        for k in range(ka):
            daw_ref[k:k + 1, :] += _rowsum(accw[pl.ds(SUBLANES * k, SUBLANES), :])
        for s in range(1, SUBLANES):
            dcs[s, pl.ds(0, n - SUBLANES), :] = dcs[0, pl.ds(s, n - SUBLANES), :]
        for r0 in range(0, tm, rc):
            rows = pl.ds(r0, rc)
            parts = []
            for c0 in range(0, dg, LANES):
                lanes = slice(c0, c0 + LANES)
                acc = aw_ref[l, ka - 1:ka, lanes] * dcs[0, rows, lanes]
                for e in range(1, ka):
                    a, sh = divmod(e, SUBLANES)
                    acc = acc + aw_ref[l, ka - 1 - e:ka - e, lanes] * dcs[sh, pl.ds(r0 + SUBLANES * a, rc), lanes]
                parts.append(acc)
            du = jnp.concatenate(parts, axis=1)
            sg = _sig(p_ref[rows, dg:2 * dg])
            dp_ref[rows, 0:dg] = (du * sg).astype(BF16)
            dp_ref[rows, dg:2 * dg] = (du * p_ref[rows, 0:dg] * sg * (1.0 - sg)).astype(BF16)

        for r0, rows in chunks:
            if r0 < tm:
                dbext[pl.ds(r0, rows), :] = d_ref[pl.ds(r0, rows), dg:2 * dg] * p_ref[pl.ds(r0, rows), 2 * dg:3 * dg]
            else:
                dbext[pl.ds(r0, rows), :] = jnp.where(last, 0.0, dn_ref[:, dg:2 * dg] * pn[:, 2 * dg:3 * dg])
        acc_bw = [zero8] * kb
        for r0 in range(0, tm, rc):
            rows = pl.ds(r0, rc)
            m_k = [mext[pl.ds(HALO_A - (kb - 1) + k + r0, rc), :] for k in range(kb)]
            cb = bw_ref[l, 0:1, :] * m_k[0]
            dm = bw_ref[l, 0:1, :] * dbext[pl.ds(r0 + kb - 1, rc), :]
            for k in range(1, kb):
                cb = cb + bw_ref[l, k:k + 1, :] * m_k[k]
                dm = dm + bw_ref[l, k:k + 1, :] * dbext[pl.ds(r0 + kb - 1 - k, rc), :]
            dcb = dbext[rows, :]
            acc_bw = [acc_bw[k] + _fold(dcb * m_k[k]) for k in range(kb)]
            dp_ref[rows, 2 * dg:3 * dg] = (d_ref[rows, dg:2 * dg] * cb).astype(BF16)
            dp_ref[rows, 3 * dg:4 * dg] = (dm * p_ref[rows, 4 * dg:5 * dg]).astype(BF16)
            dp_ref[rows, 4 * dg:5 * dg] = (dm * p_ref[rows, 3 * dg:4 * dg]).astype(BF16)
        for k in range(kb):
            dbw_ref[k:k + 1, :] += _rowsum(acc_bw[k])

    full = lambda a: pl.BlockSpec(a.shape, lambda i: (0,) * a.ndim)
    prev = lambda i: (jnp.maximum(i * nb - 1, 0), 0)
    nxt = lambda i: (jnp.minimum((i + 1) * nb, S // HALO_A - 1), 0)
    acc = lambda r: pl.BlockSpec((r, dg), lambda i: (0, 0))
    body, in_specs, out_specs, out_shape, scratch, operands, aliases = _with_exchange(
        exchange, body,
        [pl.BlockSpec((tm, W), lambda i: (i, 0)), pl.BlockSpec((HALO_A, W), prev), pl.BlockSpec((HALO_A, W), nxt),
         pl.BlockSpec((tm, 2 * dg), lambda i: (i, 0)), pl.BlockSpec((HALO_A, 2 * dg), nxt),
         full(aw), full(ab), full(lg), full(lb), full(bw)],
        [pl.BlockSpec((tm, W), lambda i: (i, 0)), acc(ka), acc(1), acc(1), acc(1), acc(kb)],
        [_sds((S, W), BF16), _sds((ka, dg), F32), _sds((1, dg), F32), _sds((1, dg), F32), _sds((1, dg), F32),
         _sds((kb, dg), F32)],
        [pltpu.VMEM((SUBLANES, ext, dg), F32), pltpu.VMEM((ext, dg), F32), pltpu.VMEM((SUBLANES, n, dg), F32),
         pltpu.VMEM((n, dg), F32), pltpu.VMEM((SUBLANES * ka, dg), F32)],
        [p, p, p, dab, dab, aw, ab, lg, lb, bw],
        lambda: pl.program_id(0) == 0, lambda: pl.program_id(0) == n_i - 1)
    return _pcall(
        body, grid=(n_i,), in_specs=in_specs, out_specs=out_specs, out_shape=out_shape, scratch_shapes=scratch,
        input_output_aliases=aliases, compiler_params=_cp("arbitrary"), name=name,
    )(*operands)


def _ffn_mid_fwd(name, u2, dww, dwb, l, exchange=None):
    _, S, F = u2.shape
    tm = _tile(S, FFN_MID_ROWS, BF16_ROWS)
    tc = _tile(F, 1408)
    n_f = F // tc
    nb = tm // HALO_S
    kf = FFN_CONV_WIDTH

    def body(u_ref, uh_ref, wg_ref, wv_ref, bg_ref, bv_ref, o_ref, ext):
        first = pl.program_id(1) == 0
        ext[:, pl.ds(0, HALO_S), :] = jnp.where(first, 0.0, uh_ref[...])
        ext[:, pl.ds(HALO_S, tm), :] = u_ref[...]
        rc = _tile(tm, ELT_ROWS, BF16_ROWS)

        def lane_chunk(ci, carry):
            lanes = pl.ds(pl.multiple_of(ci * LANES, LANES), LANES)
            taps = [[w_ref[k:k + 1, lanes] for k in range(kf)] for w_ref in (wg_ref, wv_ref)]
            bias = [b_ref[l:l + 1, lanes] for b_ref in (bg_ref, bv_ref)]
            for r0 in range(0, tm, rc):
                c = []
                for g in range(2):
                    acc = bias[g]
                    for k in range(kf):
                        acc = acc + taps[g][k] * ext[g, pl.ds(HALO_S - (kf - 1) + k + r0, rc), lanes]
                    c.append(acc)
                o_ref[pl.ds(r0, rc), lanes] = (c[0] * _sig(c[0]) * c[1]).astype(BF16)
            return carry

        lax.fori_loop(0, tc // LANES, lane_chunk, 0)

    n_l = dwb.shape[0]
    n_i = S // tm
    body, in_specs, out_specs, out_shape, scratch, operands, aliases = _with_exchange(
        exchange, body,
        [pl.BlockSpec((2, tm, tc), lambda j, i: (0, i, j)),
         pl.BlockSpec((2, HALO_S, tc), lambda j, i: (0, jnp.maximum(i * nb - 1, 0), j)),
         pl.BlockSpec((None, kf, tc), lambda j, i: (l, 0, j)),
         pl.BlockSpec((None, kf, tc), lambda j, i: (l, 0, j + n_f)),
         pl.BlockSpec((n_l, tc), lambda j, i: (0, j)),
         pl.BlockSpec((n_l, tc), lambda j, i: (0, j + n_f))],
        [pl.BlockSpec((tm, tc), lambda j, i: (i, j))], [_sds((S, F), BF16)],
        [pltpu.VMEM((2, HALO_S + tm, tc), F32)], [u2, u2, dww, dww, dwb, dwb],
        lambda: jnp.logical_and(pl.program_id(0) == 0, pl.program_id(1) == 0),
        lambda: jnp.logical_and(pl.program_id(0) == n_f - 1, pl.program_id(1) == n_i - 1))
    sem = "arbitrary" if exchange else "parallel"
    outs = _pcall(
        body, grid=(n_f, n_i), in_specs=in_specs, out_specs=out_specs, out_shape=out_shape, scratch_shapes=scratch,
        input_output_aliases=aliases, compiler_params=_cp(sem, sem), name=name,
    )(*operands)
    return outs if exchange else outs[0]


def _ffn_mid_bwd(name, u2, df, dww, dwb, l, exchange=None):
    _, S, F = u2.shape
    tm = _tile(S, FFN_MID_ROWS, BF16_ROWS)
    tc = _tile(F, 1408)
    n_f = F // tc
    nb = tm // HALO_S
    n_i = S // tm
    kf = FFN_CONV_WIDTH
    n = tm + HALO_S

    def body(u_ref, up_ref, un_ref, df_ref, dfn_ref, wg_ref, wv_ref, bg_ref, bv_ref,
             du_ref, dw_ref, db_ref, uext, dcext):
        i = pl.program_id(1)
        first, last = i == 0, i == n_i - 1

        @pl.when(first)
        def _():
            dw_ref[...] = jnp.zeros_like(dw_ref)
            db_ref[...] = jnp.zeros_like(db_ref)

        uext[:, pl.ds(0, HALO_S), :] = jnp.where(first, 0.0, up_ref[...])
        uext[:, pl.ds(HALO_S, tm), :] = u_ref[...]
        uext[:, pl.ds(HALO_S + tm, HALO_S), :] = un_ref[...]
        rc = _tile(tm, ELT_ROWS, BF16_ROWS)

        def lane_chunk(ci, carry):
            lanes = pl.ds(pl.multiple_of(ci * LANES, LANES), LANES)
            taps = [[w_ref[k:k + 1, lanes] for k in range(kf)] for w_ref in (wg_ref, wv_ref)]
            bias = [b_ref[l:l + 1, lanes] for b_ref in (bg_ref, bv_ref)]
            acc_w = [[jnp.zeros((SUBLANES, LANES), F32) for _ in range(kf)] for _ in range(2)]
            acc_b = [jnp.zeros((SUBLANES, LANES), F32) for _ in range(2)]
            for r0, rows in [(r, rc) for r in range(0, tm, rc)] + [(tm, HALO_S)]:
                shifted = [[uext[g, pl.ds(HALO_S - (kf - 1) + k + r0, rows), lanes] for k in range(kf)] for g in range(2)]
                conv = []
                for g in range(2):
                    acc = bias[g]
                    for k in range(kf):
                        acc = acc + taps[g][k] * shifted[g][k]
                    conv.append(acc)
                cg, cv = conv
                s = _sig(cg)
                dfe = df_ref[pl.ds(r0, rows), lanes] if r0 < tm else jnp.where(last, 0.0, dfn_ref[:, lanes])
                dc = [dfe * cv * (s * (1.0 + cg * (1.0 - s))), dfe * (cg * s)]
                for g in range(2):
                    dcext[g, pl.ds(r0, rows), lanes] = dc[g]
                    if r0 < tm:
                        acc_b[g] = acc_b[g] + _fold(dc[g])
                        for k in range(kf):
                            acc_w[g][k] = acc_w[g][k] + _fold(dc[g] * shifted[g][k])
            for r0 in range(0, tm, rc):
                for g in range(2):
                    du = taps[g][0] * dcext[g, pl.ds(r0 + kf - 1, rc), lanes]
                    for k in range(1, kf):
                        du = du + taps[g][k] * dcext[g, pl.ds(r0 + kf - 1 - k, rc), lanes]
                    du_ref[g, pl.ds(r0, rc), lanes] = du.astype(BF16)
            for g in range(2):
                db_ref[g, :, lanes] += _rowsum(acc_b[g])
                for k in range(kf):
                    dw_ref[g, k:k + 1, lanes] += _rowsum(acc_w[g][k])
            return carry

        lax.fori_loop(0, tc // LANES, lane_chunk, 0)

    n_l = dwb.shape[0]
    prev = lambda j, i: (0, jnp.maximum(i * nb - 1, 0), j)
    nxt = lambda j, i: (0, jnp.minimum((i + 1) * nb, S // HALO_S - 1), j)
    body, in_specs, out_specs, out_shape, scratch, operands, aliases = _with_exchange(
        exchange, body,
        [pl.BlockSpec((2, tm, tc), lambda j, i: (0, i, j)),
         pl.BlockSpec((2, HALO_S, tc), prev), pl.BlockSpec((2, HALO_S, tc), nxt),
         pl.BlockSpec((tm, tc), lambda j, i: (i, j)),
         pl.BlockSpec((HALO_S, tc), lambda j, i: nxt(j, i)[1:]),
         pl.BlockSpec((None, kf, tc), lambda j, i: (l, 0, j)),
         pl.BlockSpec((None, kf, tc), lambda j, i: (l, 0, j + n_f)),
         pl.BlockSpec((n_l, tc), lambda j, i: (0, j)),
         pl.BlockSpec((n_l, tc), lambda j, i: (0, j + n_f))],
        [pl.BlockSpec((2, tm, tc), lambda j, i: (0, i, j)),
         pl.BlockSpec((2, kf, tc), lambda j, i: (0, 0, j)),
         pl.BlockSpec((2, 1, tc), lambda j, i: (0, 0, j))],
        [_sds((2, S, F), BF16), _sds((2, kf, F), F32), _sds((2, 1, F), F32)],
        [pltpu.VMEM((2, HALO_S + n, tc), F32), pltpu.VMEM((2, n, tc), F32)],
        [u2, u2, u2, df, df, dww, dww, dwb, dwb],
        lambda: jnp.logical_and(pl.program_id(0) == 0, pl.program_id(1) == 0),
        lambda: jnp.logical_and(pl.program_id(0) == n_f - 1, pl.program_id(1) == n_i - 1))
    return _pcall(
        body, grid=(n_f, n_i), in_specs=in_specs, out_specs=out_specs, out_shape=out_shape, scratch_shapes=scratch,
        input_output_aliases=aliases, compiler_params=_cp("arbitrary" if exchange else "parallel", "arbitrary"), name=name,
    )(*operands)


def _head_sum_matrix():
    r = lax.broadcasted_iota(jnp.int32, (LANES, LANES), 0) // HEAD_DIM
    c = lax.broadcasted_iota(jnp.int32, (LANES, LANES), 1) // HEAD_DIM
    return (r == c).astype(BF16)


def _head_mean(x, ones):
    return _split_dot(x, ones) * (1.0 / HEAD_DIM)


def _qknorm_fwd(name, qkv, g2):
    S, D3 = qkv.shape
    D = D3 // 3
    tm = _tile(S, 256, BF16_ROWS)
    scale = HEAD_DIM ** -0.5

    def body(q_ref, k_ref, v_ref, g_ref, qo_ref, ko_ref, vo_ref):
        ones = _head_sum_matrix()
        for cc in range(D // LANES):
            sl = slice(cc * LANES, (cc + 1) * LANES)
            for x_ref, o_ref, row, mult in ((q_ref, qo_ref, 0, scale), (k_ref, ko_ref, 1, 1.0)):
                x = x_ref[:, sl]
                r = lax.rsqrt(_head_mean(x * x, ones) + EPS)
                o_ref[:, sl] = ((x * r * g_ref[row:row + 1, :]).astype(BF16) * mult).astype(BF16)
        vo_ref[...] = v_ref[...].astype(BF16)

    col = lambda c: pl.BlockSpec((tm, D), lambda i: (i, c))
    out = pl.BlockSpec((tm, D), lambda i: (i, 0))
    return _pcall(
        body, grid=(S // tm,),
        in_specs=[col(0), col(1), col(2), pl.BlockSpec(g2.shape, lambda i: (0, 0))],
        out_specs=[out, out, out], out_shape=[_sds((S, D), BF16)] * 3,
        compiler_params=_cp("parallel"), name=name,
    )(qkv, qkv, qkv, g2)


def _qknorm_bwd(name, qkv, dq, dk, dv, g2):
    S, D3 = qkv.shape
    D = D3 // 3
    tm = _tile(S, 256, BF16_ROWS)
    scale = HEAD_DIM ** -0.5

    def body(q_ref, k_ref, dq_ref, dk_ref, dv_ref, g_ref, o_ref, dg_ref):
        @pl.when(pl.program_id(0) == 0)
        def _():
            dg_ref[...] = jnp.zeros_like(dg_ref)

        ones = _head_sum_matrix()
        for cc in range(D // LANES):
            sl = slice(cc * LANES, (cc + 1) * LANES)
            for x_ref, d_ref, row, mult, base in ((q_ref, dq_ref, 0, scale, 0), (k_ref, dk_ref, 1, 1.0, D)):
                x = x_ref[:, sl]
                r = lax.rsqrt(_head_mean(x * x, ones) + EPS)
                xh = x * r
                dn = d_ref[:, sl] * mult
                dxh = dn * g_ref[row:row + 1, :]
                dx = r * (dxh - xh * _head_mean(dxh * xh, ones))
                o_ref[:, base + cc * LANES:base + (cc + 1) * LANES] = dx.astype(BF16)
                dg_ref[row:row + 1, :] += _rowsum(dn * xh)
        o_ref[:, 2 * D:3 * D] = dv_ref[...].astype(BF16)

    col = lambda c: pl.BlockSpec((tm, D), lambda i: (i, c))
    row = pl.BlockSpec((tm, D), lambda i: (i, 0))
    return _pcall(
        body, grid=(S // tm,),
        in_specs=[col(0), col(1), row, row, row, pl.BlockSpec(g2.shape, lambda i: (0, 0))],
        out_specs=[pl.BlockSpec((tm, D3), lambda i: (i, 0)), pl.BlockSpec((2, LANES), lambda i: (0, 0))],
        out_shape=[_sds((S, D3), BF16), _sds((2, LANES), F32)],
        compiler_params=_cp("arbitrary"), name=name,
    )(qkv, qkv, dq, dk, dv, g2)


def _attn_consts():
    t = ATTN_BLOCK
    row = lax.broadcasted_iota(jnp.int32, (t, t), 0)
    col = lax.broadcasted_iota(jnp.int32, (t, t), 1)
    lane = lax.broadcasted_iota(jnp.int32, (1, LANES), 1)
    heads = (lane < HEAD_DIM, lane >= HEAD_DIM)
    return row, col, heads


def _split_dot(x, m):
    n = x.shape[0]
    hi = x.astype(BF16)
    lo = (x - hi.astype(F32)).astype(BF16)
    both = jnp.dot(jnp.concatenate([hi, lo], axis=0), m, preferred_element_type=F32)
    return both[:n] + both[n:]


def _log_keep(z):
    return -(jnp.maximum(z, 0.0) + jnp.log(1.0 + jnp.exp(-jnp.abs(z))))


def _stack_heads(a, heads):
    t = ATTN_BLOCK
    zero = jnp.zeros((t, LANES), a.dtype)
    return jnp.concatenate([jnp.where(h, a[s * t:(s + 1) * t], zero) for s in range(a.shape[0] // t) for h in heads], axis=0)


def _side_by_side(a):
    t = ATTN_BLOCK
    return jnp.concatenate([jnp.concatenate([a[2 * s * t:(2 * s + 1) * t], a[(2 * s + 1) * t:(2 * s + 2) * t]], axis=1)
                            for s in range(a.shape[0] // (2 * t))], axis=0)


def _grow(a, rows, cols):
    z = jnp.zeros((rows, cols), F32)
    return z if a is None else jnp.concatenate([z, a], axis=0)


def _attn_fwd(name, qs, kn, vb, exchange=None):
    S, D = qs.shape
    t = ATTN_BLOCK
    tq = ATTN_SUB * t

    def body(q_ref, k_ref, v_ref, o_ref):
        i = pl.program_id(1)
        row, col, heads = _attn_consts()
        after_m = (row > col).astype(BF16)
        causal = col < row
        q_all = _stack_heads(q_ref[...], heads)

        def blocks(specs, r, acc):
            n_rows = q_all.shape[0]
            offs = [pl.multiple_of(j * t, t) for j, _, _ in specs]
            zs = [lax.dot_general(q_all[lo:], k_ref[pl.ds(off, t), :], NT, preferred_element_type=F32)
                  for off, (_, lo, _) in zip(offs, specs)]
            lks = []
            for z, (_, _, mask) in zip(zs, specs):
                lk = _log_keep(z)
                lks.append(lk if mask is None else jnp.where(mask, lk, 0.0))
            cums = [_split_dot(lk, after_m) for lk in lks]
            ws = []
            for z, lk, cum, (_, lo, mask) in zip(zs, lks, cums, specs):
                rows = n_rows - lo
                r = _grow(r, rows - (0 if r is None else r.shape[0]), 1) if r is None or r.shape[0] < rows else r
                w = jnp.exp(z + lk + cum + r)
                ws.append((w if mask is None else jnp.where(mask, w, 0.0)).astype(BF16))
                r = r + jnp.sum(lk, axis=1, keepdims=True)
            acc = jnp.zeros((n_rows // 2, LANES), F32) if acc is None else acc
            for w, off, (_, lo, _) in zip(ws, offs, specs):
                part = jnp.dot(_side_by_side(w), _stack_heads(v_ref[pl.ds(off, t), :], heads), preferred_element_type=F32)
                acc = acc + (part if lo == 0 else _grow(part, lo // 2, LANES))
            return r, acc

        def head(n_more):
            specs = [(ATTN_SUB * i + s, 2 * s * t,
                      jnp.concatenate([causal, causal] + [jnp.ones_like(causal)] * (2 * (ATTN_SUB - 1 - s)), axis=0))
                     for s in reversed(range(ATTN_SUB))]
            specs += [(ATTN_SUB * i - 1 - b, 0, None) for b in range(n_more)]
            return blocks(specs, None, None)

        r, acc = lax.cond(ATTN_SUB * i >= ATTN_MORE, lambda: head(ATTN_MORE), lambda: head(0))

        def cond(c):
            return jnp.logical_and(c[0] >= 0, jnp.max(c[1]) > EXP_UNDERFLOW)

        def step(c):
            r, a = blocks([(c[0], 0, None)], c[1], c[2])
            return c[0] - 1, r, a

        first = jnp.where(ATTN_SUB * i >= ATTN_MORE, ATTN_SUB * i - 1 - ATTN_MORE, ATTN_SUB * i - 1)
        o_ref[...] = lax.while_loop(cond, step, (first, r, acc))[2]

    n_hp = D // LANES
    blk = pl.BlockSpec((tq, LANES), lambda hp, i: (i, hp))
    seq = pl.BlockSpec((S, LANES), lambda hp, i: (0, hp))
    n_i = S // tq
    body, in_specs, out_specs, out_shape, scratch, operands, aliases = _with_exchange(
        exchange, body, [blk, seq, seq], [blk], [_sds((S, D), F32)], [], [qs, kn, vb],
        lambda: jnp.logical_and(pl.program_id(0) == 0, pl.program_id(1) == 0),
        lambda: jnp.logical_and(pl.program_id(0) == n_hp - 1, pl.program_id(1) == n_i - 1))
    outs = _pcall(
        body, grid=(n_hp, n_i), in_specs=in_specs, out_specs=out_specs, out_shape=out_shape, scratch_shapes=scratch,
        input_output_aliases=aliases, compiler_params=_cp("arbitrary" if exchange else "parallel", "arbitrary"), name=name,
    )(*operands)
    return outs if exchange else outs[0]


def _attn_bwd(name, qs, kn, vb, o, do, exchange=None):
    S, D = qs.shape
    t = ATTN_BLOCK
    tq = ATTN_SUB * t

    def body(q_ref, k_ref, v_ref, o_ref, do_ref, dq_ref, dk_ref, dv_ref):
        i = pl.program_id(1)

        @pl.when(i == 0)
        def _():
            dk_ref[...] = jnp.zeros_like(dk_ref)
            dv_ref[...] = jnp.zeros_like(dv_ref)

        row, col, heads = _attn_consts()
        after_m = (row > col).astype(BF16)
        from_m = (row >= col).astype(BF16)
        causal = col < row
        q_all = _stack_heads(q_ref[...], heads)
        dob = do_ref[...].astype(BF16)
        do_all = _stack_heads(dob, heads)
        dsum_all = jnp.sum(_stack_heads(dob.astype(F32) * o_ref[...], heads), axis=1, keepdims=True)

        def blocks(specs, r, es, dq):
            n_rows = q_all.shape[0]
            offs = [pl.multiple_of(j * t, t) for j, _, _ in specs]
            masked = lambda x, mask: x if mask is None else jnp.where(mask, x, 0.0)
            top = lambda a, rows: a if a is not None and a.shape[0] == rows else _grow(a, rows - (0 if a is None else a.shape[0]), 1)
            zs = [lax.dot_general(q_all[lo:], k_ref[pl.ds(off, t), :], NT, preferred_element_type=F32)
                  for off, (_, lo, _) in zip(offs, specs)]
            gs = [lax.dot_general(do_all[lo:], v_ref[pl.ds(off, t), :], NT, preferred_element_type=F32)
                  for off, (_, lo, _) in zip(offs, specs)]
            lks = [masked(_log_keep(z), mask) for z, (_, _, mask) in zip(zs, specs)]
            cums = [_split_dot(lk, after_m) for lk in lks]
            ws, es_blk, sgs = [], [], []
            for z, g, lk, cum, (_, lo, mask) in zip(zs, gs, lks, cums, specs):
                r = top(r, n_rows - lo)
                ls = z + lk
                w = masked(jnp.exp(ls + cum + r), mask)
                ws.append(w.astype(BF16))
                es_blk.append(w * g)
                sgs.append(jnp.exp(ls))
                r = r + jnp.sum(lk, axis=1, keepdims=True)
            cum_es = [_split_dot(e, from_m) for e in es_blk]
            dzs = []
            for e, cum_e, sg, (_, lo, mask) in zip(es_blk, cum_es, sgs, specs):
                es = top(es, n_rows - lo)
                before = dsum_all[lo:] - (es + cum_e)
                dzs.append(masked(e - (e + before) * sg, mask).astype(BF16))
                es = es + jnp.sum(e, axis=1, keepdims=True)
            dq = jnp.zeros((n_rows // 2, LANES), F32) if dq is None else dq
            for dzb, w, off, (_, lo, _) in zip(dzs, ws, offs, specs):
                part = jnp.dot(_side_by_side(dzb), _stack_heads(k_ref[pl.ds(off, t), :], heads), preferred_element_type=F32)
                dq = dq + (part if lo == 0 else _grow(part, lo // 2, LANES))
                dk_ref[pl.ds(off, t), :] += lax.dot_general(dzb, q_all[lo:], TN, preferred_element_type=F32)
                dv_ref[pl.ds(off, t), :] += lax.dot_general(w, do_all[lo:], TN, preferred_element_type=F32)
            return r, es, dq

        def head(n_more):
            specs = [(ATTN_SUB * i + s, 2 * s * t,
                      jnp.concatenate([causal, causal] + [jnp.ones_like(causal)] * (2 * (ATTN_SUB - 1 - s)), axis=0))
                     for s in reversed(range(ATTN_SUB))]
            specs += [(ATTN_SUB * i - 1 - b, 0, None) for b in range(n_more)]
            return blocks(specs, None, None, None)

        r, es, dq = lax.cond(ATTN_SUB * i >= ATTN_MORE, lambda: head(ATTN_MORE), lambda: head(0))

        def cond(c):
            return jnp.logical_and(c[0] >= 0, jnp.max(c[1]) > EXP_UNDERFLOW)

        def step(c):
            r, es, a = blocks([(c[0], 0, None)], c[1], c[2], c[3])
            return c[0] - 1, r, es, a

        first = jnp.where(ATTN_SUB * i >= ATTN_MORE, ATTN_SUB * i - 1 - ATTN_MORE, ATTN_SUB * i - 1)
        dq_ref[...] = lax.while_loop(cond, step, (first, r, es, dq))[3]

    n_hp = D // LANES
    blk = pl.BlockSpec((tq, LANES), lambda hp, i: (i, hp))
    seq = pl.BlockSpec((S, LANES), lambda hp, i: (0, hp))
    n_i = S // tq
    body, in_specs, out_specs, out_shape, scratch, operands, aliases = _with_exchange(
        exchange, body, [blk, seq, seq, blk, blk], [blk, seq, seq], [_sds((S, D), F32)] * 3, [], [qs, kn, vb, o, do],
        lambda: jnp.logical_and(pl.program_id(0) == 0, pl.program_id(1) == 0),
        lambda: jnp.logical_and(pl.program_id(0) == n_hp - 1, pl.program_id(1) == n_i - 1))
    return _pcall(
        body, grid=(n_hp, n_i), in_specs=in_specs, out_specs=out_specs, out_shape=out_shape, scratch_shapes=scratch,
        input_output_aliases=aliases, compiler_params=_cp("arbitrary" if exchange else "parallel", "arbitrary"), name=name,
    )(*operands)


def _adamw(name, w, g, m, v):
    L, R, C = w.shape
    tr = _tile(R, 256, SUBLANES)
    c1 = 1.0 - ADAM_B1 ** ADAM_STEP
    c2 = 1.0 - ADAM_B2 ** ADAM_STEP

    def body(w_ref, g_ref, m_ref, v_ref, d_ref, mo_ref, vo_ref):
        gg = g_ref[...]
        mn = ADAM_B1 * m_ref[...] + (1.0 - ADAM_B1) * gg
        vn = ADAM_B2 * v_ref[...] + (1.0 - ADAM_B2) * (gg * gg)
        d_ref[...] = -ADAM_LR * ((mn / c1) / (jnp.sqrt(vn / c2) + ADAM_EPS) + ADAM_WD * w_ref[...])
        mo_ref[...] = mn
        vo_ref[...] = vn

    blk = pl.BlockSpec((None, tr, C), lambda l, i: (l, i, 0))
    return _pcall(
        body, grid=(L, R // tr), in_specs=[blk] * 4, out_specs=[blk] * 3, out_shape=[_sds(w.shape, F32)] * 3,
        compiler_params=_cp("parallel", "parallel"), name=name,
    )(w, g, m, v)


def _place():
    x, y, c = lax.axis_index("x"), lax.axis_index("y"), lax.axis_index("c")
    chips = [(1 - x, y), (x, 1 - y), (1 - x, 1 - y)]
    return x, y, c, chips


def _place_shard(name, w, j_idx):
    L, R, X = w.shape
    rh = R // 2
    tr = _tile(rh, 256, BF16_ROWS)

    def body(j_ref, w_ref, o_ref):
        o_ref[...] = w_ref[...].astype(BF16)

    return _pcall(
        body,
        grid_spec=pltpu.PrefetchScalarGridSpec(
            num_scalar_prefetch=1, grid=(L, 2, rh // tr),
            in_specs=[pl.BlockSpec((None, None, tr, X), lambda l, h, i, j_ref: (l, h, i, 0))],
            out_specs=pl.BlockSpec((None, None, None, tr, X), lambda l, h, i, j_ref: (l, j_ref[0], h, i, 0))),
        out_shape=_sds((L, N_CHIPS, 2, rh, X), BF16), compiler_params=_cp("parallel", "parallel", "parallel"), name=name,
    )(j_idx, w.reshape(L, 2, rh, X))


def _all_gather_weights(bufs, spans, small_ws):
    n_big, n_small = len(bufs), len(small_ws)
    n_in = n_big + n_small
    layers = [pl.ds(l0, n) for l0, n in spans]

    def body(*refs):
        ins, outs = refs[:n_in], refs[n_in:2 * n_in]
        send_sems, recv_sems, local_sems = refs[2 * n_in:]
        x, y, c, chips = _place()
        j_me = 2 * x + y
        j_of = [2 * cx + cy for cx, cy in chips]
        sibling = (x, y, 1 - c)

        def remote(src, dst, s, to):
            return pltpu.make_async_remote_copy(src_ref=src, dst_ref=dst, send_sem=send_sems.at[s], recv_sem=recv_sems.at[s],
                                                device_id=to, device_id_type=MESH)

        started = []
        for t in range(n_big, n_in):
            loc = pltpu.make_async_copy(ins[t], outs[t].at[:, j_me], local_sems.at[t - n_big])
            loc.start()
            started.append(loc)
        first = []
        for t in range(n_big):
            mine = outs[t].at[layers[t], j_me, c]
            for k in range(3):
                first.append(remote(mine, mine, 6 * t + k, (*chips[k], c)))
        for t in range(n_big, n_in):
            for k in range(3):
                first.append(remote(ins[t], outs[t].at[:, j_me], 6 * n_big + 3 * (t - n_big) + k, (*chips[k], c)))
        for cp in first:
            cp.start()
        passed = []
        for t in range(n_big):
            for k in range(3):
                landed = outs[t].at[layers[t], j_of[k], c]
                remote(landed, landed, 6 * t + k, (*chips[k], c)).wait_recv()
                fwd = remote(landed, landed, 6 * t + 3 + k, sibling)
                fwd.start()
                passed.append(fwd)
        for t in range(n_big):
            for k in range(3):
                other = outs[t].at[layers[t], j_of[k], 1 - c]
                remote(other, other, 6 * t + 3 + k, sibling).wait_recv()
        for t in range(n_big, n_in):
            for k in range(3):
                dst = outs[t].at[:, j_of[k]]
                remote(dst, dst, 6 * n_big + 3 * (t - n_big) + k, (*chips[k], c)).wait_recv()
        for cp in first + passed:
            cp.wait_send()
        for loc in started:
            loc.wait()

    out_shape = [_sds(b.shape, b.dtype) for b in bufs]
    out_shape += [_sds((w.shape[0], N_CHIPS) + w.shape[1:], w.dtype) for w in small_ws]
    n_sem = 6 * n_big + 3 * n_small
    outs = _pcall(
        body, in_specs=[ANY] * n_in, out_specs=[ANY] * n_in, out_shape=out_shape,
        input_output_aliases={t: t for t in range(n_big)},
        scratch_shapes=[pltpu.SemaphoreType.DMA((n_sem,)), pltpu.SemaphoreType.DMA((n_sem,)), pltpu.SemaphoreType.DMA((n_small,))],
        name="all_gather_weights",
    )(*bufs, *small_ws)
    return outs[:n_big], outs[n_big:]


class _Exchange:
    def __init__(self, operands, out_shapes, n_sems, copies, in_place=False):
        self.operands, self.out_shapes, self.n_sems, self.copies = list(operands), list(out_shapes), n_sems, copies
        self.aliases = {t: t for t in range(len(self.operands))} if in_place else {}

    @property
    def scratch(self):
        return [pltpu.SemaphoreType.DMA((self.n_sems,)), pltpu.SemaphoreType.DMA((self.n_sems,))]

    def split(self, refs):
        n_in, n_out = len(self.operands), len(self.out_shapes)
        return refs[:n_in], refs[n_in:n_in + n_out]

    def start(self, ins, outs, sems):
        for cp in self.copies(ins, outs, *sems):
            cp.start()

    def wait(self, ins, outs, sems):
        for cp in self.copies(ins, outs, *sems):
            cp.wait()


def _run_exchange(name, ex):
    n_in, n_out = len(ex.operands), len(ex.out_shapes)

    def body(*refs):
        ins, outs, sems = refs[:n_in], refs[n_in:n_in + n_out], refs[n_in + n_out:]
        ex.start(ins, outs, sems)
        ex.wait(ins, outs, sems)

    return _pcall(body, in_specs=[ANY] * n_in, out_specs=[ANY] * n_out, out_shape=ex.out_shapes, scratch_shapes=ex.scratch,
                  input_output_aliases=ex.aliases, name=name)(*ex.operands)


def _gather_chips_exchange(bufs, spans):
    def copies(ins, outs, send_sems, recv_sems):
        x, y, c, chips = _place()
        cps = []
        for t, (l0, n) in enumerate(spans):
            mine = outs[t].at[pl.ds(l0, n), 2 * x + y, c]
            cps += [pltpu.make_async_remote_copy(src_ref=mine, dst_ref=mine, send_sem=send_sems.at[3 * t + k],
                                                 recv_sem=recv_sems.at[3 * t + k], device_id=(cx, cy, c), device_id_type=MESH)
                    for k, (cx, cy) in enumerate(chips)]
        return cps

    return _Exchange(bufs, [_sds(b.shape, b.dtype) for b in bufs], 3 * len(bufs), copies, in_place=True)


def _gather_cores_exchange(bufs, spans):
    def copies(ins, outs, send_sems, recv_sems):
        x, y, c, chips = _place()
        cps = []
        for t, (l0, n) in enumerate(spans):
            for k, (cx, cy) in enumerate(chips):
                part = outs[t].at[pl.ds(l0, n), 2 * cx + cy, c]
                cps.append(pltpu.make_async_remote_copy(src_ref=part, dst_ref=part, send_sem=send_sems.at[3 * t + k],
                                                        recv_sem=recv_sems.at[3 * t + k], device_id=(x, y, 1 - c),
                                                        device_id_type=MESH))
        return cps

    return _Exchange(bufs, [_sds(b.shape, b.dtype) for b in bufs], 3 * len(bufs), copies, in_place=True)


def _core_halves_exchange(grads, spans):
    def copies(ins, outs, send_sems, recv_sems):
        x, y, c, _ = _place()
        return [pltpu.make_async_remote_copy(src_ref=ins[t].at[pl.ds(l0, n), :, 1 - c], dst_ref=outs[t],
                                             send_sem=send_sems.at[t], recv_sem=recv_sems.at[t], device_id=(x, y, 1 - c),
                                             device_id_type=MESH) for t, (l0, n) in enumerate(spans)]

    shapes = [_sds((n, g.shape[1], g.shape[3], g.shape[4]), F32) for g, (_, n) in zip(grads, spans)]
    return _Exchange(grads, shapes, len(grads), copies)


def _add_core_halves(name, g, a, c_idx, l0):
    _, nj, _, rh, X = g.shape
    L = a.shape[0]
    tr = _tile(rh, 256, BF16_ROWS)

    def body(c_ref, g_ref, a_ref, o_ref, ob_ref):
        s = g_ref[...] + a_ref[...]
        o_ref[...] = s
        ob_ref[...] = s.astype(BF16)

    blk = pl.BlockSpec((None, None, tr, X), lambda l, j, i, c_ref: (l, j, i, 0))
    return _pcall(
        body,
        grid_spec=pltpu.PrefetchScalarGridSpec(
            num_scalar_prefetch=1, grid=(L, nj, rh // tr),
            in_specs=[pl.BlockSpec((None, None, None, tr, X), lambda l, j, i, c_ref: (l + l0, j, c_ref[0], i, 0)), blk],
            out_specs=[blk, blk]),
        out_shape=[_sds((L, nj, rh, X), F32), _sds((L, nj, rh, X), BF16)],
        compiler_params=_cp("parallel", "parallel", "parallel"), name=name,
    )(c_idx, g, a)


def _chip_shards_exchange(parts):
    def copies(ins, outs, send_sems, recv_sems):
        x, y, c, chips = _place()
        return [pltpu.make_async_remote_copy(
            src_ref=ins[t].at[:, 2 * cx + cy], dst_ref=outs[t].at[k], send_sem=send_sems.at[3 * t + k],
            recv_sem=recv_sems.at[3 * t + k], device_id=(cx, cy, c), device_id_type=MESH)
            for t in range(len(parts)) for k, (cx, cy) in enumerate(chips)]

    shapes = [_sds((3, p.shape[0], p.shape[2], p.shape[3]), p.dtype) for p in parts]
    return _Exchange(parts, shapes, 3 * len(parts), copies)


def _add_chip_shards(name, p, b, jc_idx, l0, n_layers, buf):
    n, _, rh, X = p.shape
    tr = _tile(rh, 256, BF16_ROWS)

    def body(jc_ref, p_ref, b_ref, *rest):
        rest[-1][...] = ((p_ref[...] + b_ref[0].astype(F32)) + b_ref[1].astype(F32)) + b_ref[2].astype(F32)

    in_specs = [pl.BlockSpec((None, None, tr, X), lambda l, i, jc: (l, jc[0], i, 0)),
                pl.BlockSpec((3, None, tr, X), lambda l, i, jc: (0, l, i, 0))]
    operands = [jc_idx, p, b]
    if buf is not None:
        in_specs.append(ANY)
        operands.append(buf)
    return _pcall(
        body,
        grid_spec=pltpu.PrefetchScalarGridSpec(
            num_scalar_prefetch=1, grid=(n, rh // tr), in_specs=in_specs,
            out_specs=pl.BlockSpec((None, None, tr, X), lambda l, i, jc: (l + l0, jc[1], i, 0))),
        out_shape=_sds((n_layers, 2, rh, X), F32), input_output_aliases={3: 0} if buf is not None else {},
        compiler_params=_cp("parallel", "parallel"), name=name,
    )(*operands)


def _join_core_halves(bufs):
    n = len(bufs)

    def body(*refs):
        outs = refs[n:2 * n]
        send_sems, recv_sems = refs[2 * n:]
        x, y, c, _ = _place()
        cps = [pltpu.make_async_remote_copy(src_ref=outs[t].at[:, c], dst_ref=outs[t].at[:, c], send_sem=send_sems.at[t],
                                            recv_sem=recv_sems.at[t], device_id=(x, y, 1 - c), device_id_type=MESH)
               for t in range(n)]
        for cp in cps:
            cp.start()
        for t in range(n):
            pltpu.make_async_remote_copy(src_ref=outs[t].at[:, c], dst_ref=outs[t].at[:, 1 - c], send_sem=send_sems.at[t],
                                         recv_sem=recv_sems.at[t], device_id=(x, y, 1 - c), device_id_type=MESH).wait()

    outs = _pcall(
        body, in_specs=[ANY] * n, out_specs=[ANY] * n, out_shape=[_sds(b.shape, F32) for b in bufs],
        input_output_aliases={t: t for t in range(n)},
        scratch_shapes=[pltpu.SemaphoreType.DMA((n,)), pltpu.SemaphoreType.DMA((n,))],
        name="grad_join_core_halves",
    )(*bufs)
    return [o.reshape(o.shape[0], 2 * o.shape[2], o.shape[3]) for o in outs]


def _all_reduce_small(packed):
    R, C = packed.shape

    def body(x_ref, o_ref, slots, send_sems, recv_sems):
        x, y, c, _ = _place()
        me = 4 * x + 2 * y + c
        slots[me] = x_ref[...]
        cps = []
        for d in range(N_DEV):
            to = (d // 4, (d // 2) % 2, d % 2)
            cp = pltpu.make_async_remote_copy(src_ref=x_ref, dst_ref=slots.at[me], send_sem=send_sems.at[d],
                                              recv_sem=recv_sems.at[me], device_id=to, device_id_type=MESH)
            cps.append(cp)

            @pl.when(d != me)
            def _():
                cp.start()

        for d in range(N_DEV):
            @pl.when(d != me)
            def _():
                pltpu.make_async_remote_copy(src_ref=x_ref, dst_ref=slots.at[d], send_sem=send_sems.at[d],
                                             recv_sem=recv_sems.at[d], device_id=(x, y, c), device_id_type=MESH).wait_recv()
                cps[d].wait_send()

        acc = slots[0]
        for d in range(1, N_DEV):
            acc = acc + slots[d]
        o_ref[...] = acc

    vm = pl.BlockSpec(memory_space=pltpu.VMEM)
    return _pcall(
        body, in_specs=[vm], out_specs=vm, out_shape=_sds((R, C), F32),
        scratch_shapes=[pltpu.VMEM((N_DEV, R, C), F32), pltpu.SemaphoreType.DMA((N_DEV,)), pltpu.SemaphoreType.DMA((N_DEV,))],
        compiler_params=pltpu.CompilerParams(vmem_limit_bytes=VMEM_LIMIT_BYTES), name="all_reduce_small",
    )(packed)


PACK = SUBLANES * LANES


def _pack(arrays):
    flat = []
    for a in arrays:
        v = a.reshape(-1)
        flat.append(jnp.pad(v, (0, (-v.shape[0]) % PACK)))
    return jnp.concatenate(flat).reshape(-1, LANES)


def _unpack(packed, shapes):
    flat = packed.reshape(-1)
    out, pos = [], 0
    for s in shapes:
        n = 1
        for d in s:
            n *= d
        out.append(flat[pos:pos + n].reshape(s))
        pos += n + (-n) % PACK
    return out


def kernel(x, mix_norm_g, ffn_norm_g, conv_w_in, conv_a_dw_w, conv_a_dw_b, conv_a_ln_g, conv_a_ln_b, conv_b_dw_w, conv_w_out, attn_w_qkv, attn_q_g, attn_k_g, attn_w_o, ffn_w_up, ffn_dw_w, ffn_dw_b, ffn_w_down, loss_target, m_mix_norm_g, m_ffn_norm_g, m_conv_w_in, m_conv_a_dw_w, m_conv_a_dw_b, m_conv_a_ln_g, m_conv_a_ln_b, m_conv_b_dw_w, m_conv_w_out, m_attn_w_qkv, m_attn_q_g, m_attn_k_g, m_attn_w_o, m_ffn_w_up, m_ffn_dw_w, m_ffn_dw_b, m_ffn_w_down, v_mix_norm_g, v_ffn_norm_g, v_conv_w_in, v_conv_a_dw_w, v_conv_a_dw_b, v_conv_a_ln_g, v_conv_a_ln_b, v_conv_b_dw_w, v_conv_w_out, v_attn_w_qkv, v_attn_q_g, v_attn_k_g, v_attn_w_o, v_ffn_w_up, v_ffn_dw_w, v_ffn_dw_b, v_ffn_w_down):
    depth = mix_norm_g.shape[0]
    n_even, n_odd = conv_w_in.shape[0], attn_w_qkv.shape[0]
    S, D = x.shape[1], x.shape[2]
    dg = D // 2
    x0 = x.reshape(S, D)
    target = loss_target.reshape(S, D)
    j_me = 2 * lax.axis_index("x") + lax.axis_index("y")
    c_me = lax.axis_index("c")
    j_idx = j_me.astype(jnp.int32).reshape(1)
    c_idx = c_me.astype(jnp.int32).reshape(1)

    col_names = ["conv_w_in", "attn_w_qkv", "ffn_w_up"]
    row_names = ["conv_w_out", "attn_w_o", "ffn_w_down"]
    local = dict(conv_w_in=conv_w_in, attn_w_qkv=attn_w_qkv, ffn_w_up=ffn_w_up, conv_w_out=conv_w_out, attn_w_o=attn_w_o,
                 ffn_w_down=ffn_w_down)
    gbuf = {n: _place_shard(f"place_{n}", local[n], j_idx) for n in col_names + row_names}

    def weights_of(layer):
        mixer = ("conv_w_in", "conv_w_out") if layer % 2 == 0 else ("attn_w_qkv", "attn_w_o")
        return {mixer[0]: (layer // 2, 1), mixer[1]: (layer // 2, 1), "ffn_w_up": (layer, 1), "ffn_w_down": (layer, 1)}

    def w_col(n):
        return gbuf[n].reshape(gbuf[n].shape[0], N_CHIPS, -1, gbuf[n].shape[4])

    def w_row(n):
        return gbuf[n].reshape(gbuf[n].shape[0], -1, gbuf[n].shape[4])

    def carried(kernel_out, make_exchange, group):
        if not group:
            return kernel_out(None)
        out, *new = kernel_out(make_exchange([gbuf[n] for n in group], list(group.values())))
        gbuf.update(zip(group, new))
        return out

    first = {"conv_w_in": (0, 1), "conv_w_out": (0, 1)}
    ffn_first = {"ffn_w_up": (0, 1), "ffn_w_down": (0, 1)}
    outs, (a_dw, b_dw, f_dw) = _all_gather_weights([gbuf[n] for n in first], list(first.values()),
                                                   [conv_a_dw_w, conv_b_dw_w, ffn_dw_w])
    gbuf.update(zip(first, outs))
    unshard = lambda a: jnp.moveaxis(a, 1, 2).reshape(a.shape[0], a.shape[2], N_CHIPS * a.shape[3])
    a_dw, b_dw, f_dw = unshard(a_dw), unshard(b_dw), unshard(f_dw)
    qk_gain = [jnp.stack([jnp.tile(attn_q_g[i], LANES // HEAD_DIM), jnp.tile(attn_k_g[i], LANES // HEAD_DIM)])
               for i in range(n_odd)]

    saved = []
    xc = x0
    for layer in range(depth):
        i = layer // 2
        tag = f"l{layer}"
        s = {"x_in": xc}
        here = weights_of(layer) if layer else None
        h = carried(lambda ex: _rms_fwd(f"rms_mix_fwd_{tag}", xc, mix_norm_g, layer, ex), _gather_cores_exchange, here)
        s["h"] = h
        if layer % 2 == 0:
            p = _mm_fwd(f"conv_in_fwd_{tag}", h, w_col("conv_w_in"), i, colshard=True)
            ab = carried(lambda ex: _convmix_fwd(f"convmix_fwd_{tag}", p, a_dw, conv_a_dw_b, conv_a_ln_g, conv_a_ln_b, b_dw,
                                                 i, ex), _gather_chips_exchange, None if layer else ffn_first)
            xm = _mm_fwd(f"conv_out_fwd_{tag}", ab, w_row("conv_w_out"), i, colshard=False, res=xc)
            s.update(p=p, ab=ab)
        else:
            qkv = _mm_fwd(f"attn_qkv_fwd_{tag}", h, w_col("attn_w_qkv"), i, colshard=True)
            qs, kn, vb = _qknorm_fwd(f"qknorm_fwd_{tag}", qkv, qk_gain[i])
            o = _attn_fwd(f"attn_fwd_{tag}", qs, kn, vb)
            xm = _mm_fwd(f"attn_out_fwd_{tag}", o, w_row("attn_w_o"), i, colshard=False, res=xc)
            s.update(qkv=qkv, qs=qs, kn=kn, vb=vb, o=o)
        s["x_mid"] = xm
        h2 = carried(lambda ex: _rms_fwd(f"rms_ffn_fwd_{tag}", xm, ffn_norm_g, layer, ex), _gather_cores_exchange,
                     None if layer else ffn_first)
        u2 = _mm_fwd(f"ffn_up_fwd_{tag}", h2, w_col("ffn_w_up"), layer, colshard=True, out_split=2)
        f = carried(lambda ex: _ffn_mid_fwd(f"ffn_mid_fwd_{tag}", u2, f_dw, ffn_dw_b, layer, ex), _gather_chips_exchange,
                    weights_of(layer + 1) if layer + 1 < depth else None)
        xc = _mm_fwd(f"ffn_down_fwd_{tag}", f, w_row("ffn_w_down"), layer, colshard=False, res=xm)
        s.update(h2=h2, u2=u2, f=f)
        saved.append(s)

    dx, loss_tile = _loss_fwd_bwd("loss", xc, target)

    w_in, w_qkv, w_up = w_col("conv_w_in"), w_col("attn_w_qkv"), w_col("ffn_w_up")
    w_out, w_o, w_down = w_row("conv_w_out"), w_row("attn_w_o"), w_row("ffn_w_down")
    g_up = g_down = g_in = g_out = g_qkv = g_o = None
    big_names = col_names + row_names

    def halves_view(n, g):
        if n in col_names:
            return g.reshape(g.shape[0], N_CHIPS, 2, g.shape[2] // 2, g.shape[3])
        return g.reshape(g.shape[0], N_CHIPS, 2, g.shape[1] // (2 * N_CHIPS), g.shape[2])

    ffn_of_0 = {"ffn_w_up": (0, 1), "ffn_w_down": (0, 1)}
    mixer_of_0 = {"conv_w_in": (0, 1), "conv_w_out": (0, 1)}
    summed_parts = {n: [] for n in big_names}

    def stacks():
        return {"conv_w_in": g_in, "attn_w_qkv": g_qkv, "ffn_w_up": g_up, "conv_w_out": g_out, "attn_w_o": g_o,
                "ffn_w_down": g_down}

    def core_exchange(group):
        return _core_halves_exchange([halves_view(n, stacks()[n]) for n in group], list(group.values()))

    def chip_exchange(tag, arrived):
        sums, parts = [], []
        for group, from_sibling in arrived:
            for n, a in zip(group, from_sibling):
                f32_sum, bf16_sum = _add_core_halves(f"grad_add_core_{n}_{tag}_{group[n][0]}", halves_view(n, stacks()[n]), a,
                                                     c_idx, group[n][0])
                sums.append((n, group[n][0], f32_sum))
                parts.append(bf16_sum)
        return _chip_shards_exchange(parts), sums

    def record(sums, from_chips):
        for (n, l0, f32_sum), b in zip(sums, from_chips):
            summed_parts[n].append((l0, f32_sum, b))

    d_mix_g, d_ffn_g = [None] * depth, [None] * depth
    d_ffn_dw_w, d_ffn_dw_b = [None] * depth, [None] * depth
    d_a_dw_w, d_a_dw_b, d_a_ln_g, d_a_ln_b, d_b_dw_w = ([None] * n_even for _ in range(5))
    d_q_g, d_k_g = [None] * n_odd, [None] * n_odd
    for layer in reversed(range(depth)):
        i = layer // 2
        tag = f"l{layer}"
        s = saved[layer]
        df = _mm_dgrad(f"ffn_down_dgrad_{tag}", dx, w_down, layer, colshard=False)
        g_down = _mm_wgrad(f"ffn_down_wgrad_{tag}", s["f"], dx, layer, depth, g_down, colshard=False)
        above = weights_of(layer + 1) if layer + 1 < depth else None
        arrived = []
        du2, dww, dwb, *from_sibling = _ffn_mid_bwd(f"ffn_mid_bwd_{tag}", s["u2"], df, f_dw, ffn_dw_b, layer,
                                                    core_exchange(above) if above else None)
        if above:
            arrived.append((above, from_sibling))
        d_ffn_dw_w[layer] = jnp.moveaxis(dww, 0, 1).reshape(FFN_CONV_WIDTH, -1)
        d_ffn_dw_b[layer] = dwb.reshape(-1)
        dh2 = _mm_dgrad(f"ffn_up_dgrad_{tag}", du2, w_up, layer, colshard=True)
        g_up = _mm_wgrad(f"ffn_up_wgrad_{tag}", s["h2"], du2, layer, depth, g_up, colshard=True)
        dx, dg_, *from_sibling = _rms_bwd(f"rms_ffn_bwd_{tag}", s["x_mid"], ffn_norm_g, layer, dh2, dx,
                                          core_exchange(ffn_of_0) if layer == 0 else None)
        if layer == 0:
            arrived.append((ffn_of_0, from_sibling))
        d_ffn_g[layer] = dg_.reshape(-1)
        if layer % 2 == 0:
            dab = _mm_dgrad(f"conv_out_dgrad_{tag}", dx, w_out, i, colshard=False)
            g_out = _mm_wgrad(f"conv_out_wgrad_{tag}", s["ab"], dx, i, n_even, g_out, colshard=False)
            chip_ex, sums = chip_exchange(tag, arrived) if arrived else (None, [])
            dp, daw, dab_b, dlg, dlb, dbw, *from_chips = _convmix_bwd(
                f"convmix_bwd_{tag}", s["p"], dab, a_dw, conv_a_dw_b, conv_a_ln_g, conv_a_ln_b, b_dw, i, chip_ex)
            record(sums, from_chips)
            d_a_dw_w[i], d_a_dw_b[i], d_a_ln_g[i], d_a_ln_b[i], d_b_dw_w[i] = (
                daw, dab_b.reshape(-1), dlg.reshape(-1), dlb.reshape(-1), dbw)
            dh = _mm_dgrad(f"conv_in_dgrad_{tag}", dp, w_in, i, colshard=True)
            g_in = _mm_wgrad(f"conv_in_wgrad_{tag}", s["h"], dp, i, n_even, g_in, colshard=True)
        else:
            do = _mm_dgrad(f"attn_out_dgrad_{tag}", dx, w_o, i, colshard=False)
            g_o = _mm_wgrad(f"attn_out_wgrad_{tag}", s["o"], dx, i, n_odd, g_o, colshard=False)
            chip_ex, sums = chip_exchange(tag, arrived) if arrived else (None, [])
            dq, dk, dv, *from_chips = _attn_bwd(f"attn_bwd_{tag}", s["qs"], s["kn"], s["vb"], s["o"], do, chip_ex)
            record(sums, from_chips)
            dqkv, dgain = _qknorm_bwd(f"qknorm_bwd_{tag}", s["qkv"], dq, dk, dv, qk_gain[i])
            d_q_g[i] = dgain[0, :HEAD_DIM] + dgain[0, HEAD_DIM:]
            d_k_g[i] = dgain[1, :HEAD_DIM] + dgain[1, HEAD_DIM:]
            dh = _mm_dgrad(f"attn_qkv_dgrad_{tag}", dqkv, w_qkv, i, colshard=True)
            g_qkv = _mm_wgrad(f"attn_qkv_wgrad_{tag}", s["h"], dqkv, i, n_odd, g_qkv, colshard=True)
        dx, dg_ = _rms_bwd(f"rms_mix_bwd_{tag}", s["x_in"], mix_norm_g, layer, dh, dx)
        d_mix_g[layer] = dg_.reshape(-1)
    grad_x = dx.reshape(1, S, D)

    small = {
        "mix_norm_g": jnp.stack(d_mix_g), "ffn_norm_g": jnp.stack(d_ffn_g),
        "conv_a_dw_w": jnp.stack(d_a_dw_w), "conv_a_dw_b": jnp.stack(d_a_dw_b),
        "conv_a_ln_g": jnp.stack(d_a_ln_g), "conv_a_ln_b": jnp.stack(d_a_ln_b),
        "conv_b_dw_w": jnp.stack(d_b_dw_w), "attn_q_g": jnp.stack(d_q_g), "attn_k_g": jnp.stack(d_k_g),
        "ffn_dw_w": jnp.stack(d_ffn_dw_w), "ffn_dw_b": jnp.stack(d_ffn_dw_b),
    }
    small_names = list(small)
    summed = _all_reduce_small(_pack([loss_tile] + [small[n] for n in small_names]))
    parts = _unpack(summed, [loss_tile.shape] + [small[n].shape for n in small_names])
    loss = parts[0][0, 0]
    small_g = dict(zip(small_names, parts[1:]))
    for n in ("conv_a_dw_w", "conv_b_dw_w", "ffn_dw_w"):
        cs = small_g[n].shape[2] // N_CHIPS
        small_g[n] = lax.dynamic_slice_in_dim(small_g[n], j_me * cs, cs, axis=2)

    from_sibling = _run_exchange("grad_exchange_core_halves", core_exchange(mixer_of_0))
    chip_ex, sums = chip_exchange("last", [(mixer_of_0, from_sibling)])
    record(sums, _run_exchange("grad_exchange_chip_shards", chip_ex))
    jc_idx = jnp.concatenate([j_idx, c_idx])
    totals = {}
    for n in big_names:
        total = None
        for l0, p, b in summed_parts[n]:
            total = _add_chip_shards(f"grad_add_chips_{n}_{l0}", p, b, jc_idx, l0, stacks()[n].shape[0], total)
        totals[n] = total
    big_g = dict(zip(big_names, _join_core_halves([totals[n] for n in big_names])))

    weights = dict(mix_norm_g=mix_norm_g, ffn_norm_g=ffn_norm_g, conv_w_in=conv_w_in, conv_a_dw_w=conv_a_dw_w, conv_a_dw_b=conv_a_dw_b, conv_a_ln_g=conv_a_ln_g, conv_a_ln_b=conv_a_ln_b, conv_b_dw_w=conv_b_dw_w, conv_w_out=conv_w_out, attn_w_qkv=attn_w_qkv, attn_q_g=attn_q_g, attn_k_g=attn_k_g, attn_w_o=attn_w_o, ffn_w_up=ffn_w_up, ffn_dw_w=ffn_dw_w, ffn_dw_b=ffn_dw_b, ffn_w_down=ffn_w_down)
    m_in = dict(mix_norm_g=m_mix_norm_g, ffn_norm_g=m_ffn_norm_g, conv_w_in=m_conv_w_in, conv_a_dw_w=m_conv_a_dw_w, conv_a_dw_b=m_conv_a_dw_b, conv_a_ln_g=m_conv_a_ln_g, conv_a_ln_b=m_conv_a_ln_b, conv_b_dw_w=m_conv_b_dw_w, conv_w_out=m_conv_w_out, attn_w_qkv=m_attn_w_qkv, attn_q_g=m_attn_q_g, attn_k_g=m_attn_k_g, attn_w_o=m_attn_w_o, ffn_w_up=m_ffn_w_up, ffn_dw_w=m_ffn_dw_w, ffn_dw_b=m_ffn_dw_b, ffn_w_down=m_ffn_w_down)
    v_in = dict(mix_norm_g=v_mix_norm_g, ffn_norm_g=v_ffn_norm_g, conv_w_in=v_conv_w_in, conv_a_dw_w=v_conv_a_dw_w, conv_a_dw_b=v_conv_a_dw_b, conv_a_ln_g=v_conv_a_ln_g, conv_a_ln_b=v_conv_a_ln_b, conv_b_dw_w=v_conv_b_dw_w, conv_w_out=v_conv_w_out, attn_w_qkv=v_attn_w_qkv, attn_q_g=v_attn_q_g, attn_k_g=v_attn_k_g, attn_w_o=v_attn_w_o, ffn_w_up=v_ffn_w_up, ffn_dw_w=v_ffn_dw_w, ffn_dw_b=v_ffn_dw_b, ffn_w_down=v_ffn_w_down)
    order = list(weights)
    grads, delta, new_m, new_v = {}, {}, {}, {}
    for n in big_names:
        grads[n] = big_g[n]
        delta[n], new_m[n], new_v[n] = _adamw(f"adamw_{n}", weights[n], big_g[n], m_in[n], v_in[n])
    shapes = [weights[n].shape for n in small_names]
    packed = [_pack([d[n] for n in small_names]) for d in (weights, small_g, m_in, v_in)]
    upd = _adamw("adamw_small", *[p[None] for p in packed])
    for out, res in zip((delta, new_m, new_v), upd):
        out.update(zip(small_names, _unpack(res[0], shapes)))
    grads.update({n: small_g[n].reshape(weights[n].shape) for n in small_names})
    return (loss, grad_x, *[grads[n] for n in order], *[delta[n] for n in order], *[new_m[n] for n in order],
            *[new_v[n] for n in order])
```

```python
import jax
import jax.numpy as jnp
from jax import lax
from jax.experimental import pallas as pl
from jax.experimental.pallas import tpu as pltpu

F32 = jnp.float32
BF16 = jnp.bfloat16
EPS = 1e-6
CONV_A_WIDTH = 31
CONV_B_WIDTH = 3
FFN_CONV_WIDTH = 3
HEAD_DIM = 64
ADAM_LR = 0.001
ADAM_B1 = 0.9
ADAM_B2 = 0.999
ADAM_EPS = 1e-08
ADAM_WD = 0.01
ADAM_STEP = 10

LANES = 128
SUBLANES = 8
BF16_ROWS = 16
V7X_VMEM_BYTES = 64 * 1024 * 1024
VMEM_LIMIT_BYTES = V7X_VMEM_BYTES * 3 // 4
MM_VMEM_BUDGET = VMEM_LIMIT_BYTES * 4 // 5
MM_ROWS = 1024
N_CHIPS = 4
N_DEV = 8
HALO_A = 32
HALO_S = 8
ELT_ROWS = 64
FFN_MID_ROWS = 512
NORM_ROWS = 1024
ATTN_BLOCK = 128
ATTN_SUB = 2
ATTN_MORE = 2
EXP_UNDERFLOW = -104.0
MESH = pl.DeviceIdType.MESH
ANY = pl.BlockSpec(memory_space=pl.ANY)
NT = (((1,), (1,)), ((), ()))
NN = (((1,), (0,)), ((), ()))
TN = (((0,), (0,)), ((), ()))


def _pcall(body, **kw):
    return pl.pallas_call(body, **kw)


def _cp(*sem):
    return pltpu.CompilerParams(dimension_semantics=sem, vmem_limit_bytes=VMEM_LIMIT_BYTES)


def _sds(shape, dtype):
    return jax.ShapeDtypeStruct(tuple(shape), dtype)


def _tile(n, cap, align=LANES):
    if n <= cap:
        return n
    for t in range(cap - cap % align, 0, -align):
        if n % t == 0:
            return t
    return n


def _sig(x):
    return 0.5 * jnp.tanh(0.5 * x) + 0.5


def _rowsum(x):
    return jnp.sum(x, axis=0, keepdims=True)


def _fold(x):
    acc = x[0:SUBLANES]
    for r in range(SUBLANES, x.shape[0], SUBLANES):
        acc = acc + x[r:r + SUBLANES]
    return acc


def _with_exchange(ex, body, in_specs, out_specs, out_shape, scratch, operands, first, last):
    if ex is None:
        return body, in_specs, out_specs, out_shape, scratch, operands, {}
    n_in, n_out, n_scr = len(in_specs), len(out_specs), len(scratch)
    e_in, e_out = len(ex.operands), len(ex.out_shapes)

    def hosted(*refs):
        refs = list(refs)
        ins, refs = refs[:n_in], refs[n_in:]
        e_ins, refs = refs[:e_in], refs[e_in:]
        outs, refs = refs[:n_out], refs[n_out:]
        e_outs, refs = refs[:e_out], refs[e_out:]
        scr, sems = refs[:n_scr], refs[n_scr:]

        @pl.when(first())
        def _():
            ex.start(e_ins, e_outs, sems)

        body(*ins, *outs, *scr)

        @pl.when(last())
        def _():
            ex.wait(e_ins, e_outs, sems)

    return (hosted, in_specs + [ANY] * e_in, out_specs + [ANY] * e_out, out_shape + ex.out_shapes, scratch + ex.scratch,
            operands + ex.operands, {n_in + i: n_out + o for i, o in ex.aliases.items()})


def _mm_call(name, dn, operands, in_specs, out_shape, out_spec, grid, nk, acc_shape, has_res, has_alias):
    def body(*refs):
        a_ref, b_ref = refs[0], refs[1]
        pos = 2
        res_ref = refs[pos] if has_res else None
        pos += int(has_res) + int(has_alias)
        o_ref = refs[pos]
        acc_ref = refs[pos + 1] if nk > 1 else None
        p = lax.dot_general(a_ref[...].astype(BF16), b_ref[...].astype(BF16), dn, preferred_element_type=F32)

        def finish(v):
            if has_res:
                v = v + res_ref[...]
            o_ref[...] = v.astype(o_ref.dtype)

        if nk == 1:
            finish(p)
        else:
            k = pl.program_id(2)

            @pl.when(k == 0)
            def _():
                acc_ref[...] = p

            @pl.when(k > 0)
            def _():
                acc_ref[...] += p

            @pl.when(k == nk - 1)
            def _():
                finish(acc_ref[...])

    aliases = {len(operands) - 1: 0} if has_alias else {}
    return _pcall(
        body, grid=grid, in_specs=in_specs, out_specs=out_spec, out_shape=out_shape,
        scratch_shapes=[pltpu.VMEM(acc_shape, F32)] if nk > 1 else [],
        input_output_aliases=aliases, compiler_params=_cp("parallel", "parallel", "arbitrary"), name=name,
    )(*operands)


def _mm_fwd(name, a, w, l, *, colshard, res=None, out_split=1):
    M, K = a.shape
    tm = _tile(M, MM_ROWS, BF16_ROWS)
    if colshard and out_split == 1 and res is None:
        cs = w.shape[3]
        th = _tile(M, MM_ROWS // 2, BF16_ROWS)
        if 2 * (N_CHIPS * K * cs * 2 + th * N_CHIPS * cs * 4 + th * K * a.dtype.itemsize) <= MM_VMEM_BUDGET:
            def body(a_ref, b_ref, o_ref):
                av = a_ref[...].astype(BF16)
                for j in range(N_CHIPS):
                    o_ref[:, j * cs:(j + 1) * cs] = jnp.dot(av, b_ref[j], preferred_element_type=F32)

            return _pcall(
                body, grid=(M // th,),
                in_specs=[pl.BlockSpec((th, K), lambda i: (i, 0)), pl.BlockSpec((None, N_CHIPS, K, cs), lambda i: (l, 0, 0, 0))],
                out_specs=pl.BlockSpec((th, N_CHIPS * cs), lambda i: (i, 0)), out_shape=_sds((M, N_CHIPS * cs), F32),
                compiler_params=_cp("parallel"), name=name,
            )(a, w)
    if colshard:
        cs = w.shape[3]
        N, tn, tk = N_CHIPS * cs, cs, K
        b_spec = pl.BlockSpec((None, None, tk, tn), lambda j, i, k: (l, j, k, 0))
    else:
        N = w.shape[2]
        tn, tk = _tile(N, 1024), K
        if K > 1536:
            tm = _tile(M, MM_ROWS // 2, BF16_ROWS)
        b_spec = pl.BlockSpec((None, tk, tn), lambda j, i, k: (l, k, j))
    nk = K // tk
    in_specs = [pl.BlockSpec((tm, tk), lambda j, i, k: (i, k)), b_spec]
    operands = [a, w]
    if res is not None:
        in_specs.append(pl.BlockSpec((tm, tn), lambda j, i, k: (i, j)))
        operands.append(res)
    if out_split == 1:
        out_shape = _sds((M, N), F32)
        out_spec = pl.BlockSpec((tm, tn), lambda j, i, k: (i, j))
    else:
        per = N // tn // out_split
        out_shape = _sds((out_split, M, N // out_split), F32)
        out_spec = pl.BlockSpec((None, tm, tn), lambda j, i, k: (j // per, i, j % per))
    return _mm_call(name, NN, operands, in_specs, out_shape, out_spec, (N // tn, M // tm, nk), nk, (tm, tn),
                    res is not None, False)


def _mm_dgrad(name, g, w, l, *, colshard):
    split = g.ndim == 3
    M = g.shape[-2]
    tm = _tile(M, MM_ROWS, BF16_ROWS)
    if colshard:
        kw, cs = w.shape[2], w.shape[3]
        tm = _tile(M, MM_ROWS // 2, BF16_ROWS)
        per = N_CHIPS // g.shape[0] if split else N_CHIPS

        def body(a_ref, b_ref, o_ref):
            acc = None
            for j in range(N_CHIPS):
                cols = slice((j % per) * cs, (j % per + 1) * cs)
                a = a_ref[j // per, :, cols] if split else a_ref[:, cols]
                p = lax.dot_general(a.astype(BF16), b_ref[j], NT, preferred_element_type=F32)
                acc = p if acc is None else acc + p
            o_ref[...] = acc

        a_spec = (pl.BlockSpec((g.shape[0], tm, g.shape[2]), lambda i: (0, i, 0)) if split
                  else pl.BlockSpec((tm, N_CHIPS * cs), lambda i: (i, 0)))
        return _pcall(
            body, grid=(M // tm,),
            in_specs=[a_spec, pl.BlockSpec((None, N_CHIPS, kw, cs), lambda i: (l, 0, 0, 0))],
            out_specs=pl.BlockSpec((tm, kw), lambda i: (i, 0)), out_shape=_sds((M, kw), F32),
            compiler_params=_cp("parallel"), name=name,
        )(g, w)
    else:
        kw, ncon = w.shape[1], w.shape[2]
        tn, tk = _tile(kw, 1408), _tile(ncon, 1536)
        nk = ncon // tk
        th = _tile(M, MM_ROWS // 2, BF16_ROWS)
        if nk == 1 and 2 * (kw * ncon * w.dtype.itemsize + th * kw * 4 + th * ncon * g.dtype.itemsize) <= MM_VMEM_BUDGET:
            tm, tn = th, kw
        b_spec = pl.BlockSpec((None, tn, tk), lambda j, i, k: (l, j, k))
    if split:
        per = nk // g.shape[0]
        a_spec = pl.BlockSpec((None, tm, tk), lambda j, i, k: (k // per, i, k % per))
    else:
        a_spec = pl.BlockSpec((tm, tk), lambda j, i, k: (i, k))
    out_shape = _sds((M, kw), F32)
    out_spec = pl.BlockSpec((tm, tn), lambda j, i, k: (i, j))
    return _mm_call(name, NT, [g, w], [a_spec, b_spec], out_shape, out_spec, (kw // tn, M // tm, nk), nk, (tm, tn),
                    False, False)


def _mm_wgrad(name, a, g, l, n_layers, buf, *, colshard):
    S, M = a.shape
    split = g.ndim == 3
    N = g.shape[-1] * (g.shape[0] if split else 1)
    tm = _tile(M, 1408)
    tn = N // N_CHIPS if colshard else _tile(N, 1024)
    per_row = 2 * (tm * a.dtype.itemsize + tn * g.dtype.itemsize)
    tk = _tile(S, max(BF16_ROWS, min(2048, (MM_VMEM_BUDGET - 3 * tm * tn * 4) // per_row)), BF16_ROWS)
    nk = S // tk
    if colshard:
        out_shape = _sds((n_layers, N_CHIPS, M, tn), F32)
        out_spec = pl.BlockSpec((None, None, tm, tn), lambda j, i, k: (l, j, i, 0))
    else:
        out_shape = _sds((n_layers, M, N), F32)
        out_spec = pl.BlockSpec((None, tm, tn), lambda j, i, k: (l, i, j))
    if split:
        per = N // tn // g.shape[0]
        b_spec = pl.BlockSpec((None, tk, tn), lambda j, i, k: (j // per, k, j % per))
    else:
        b_spec = pl.BlockSpec((tk, tn), lambda j, i, k: (k, j))
    in_specs = [pl.BlockSpec((tk, tm), lambda j, i, k: (k, i)), b_spec]
    operands = [a, g]
    if buf is not None:
        in_specs.append(ANY)
        operands.append(buf)
    return _mm_call(name, TN, operands, in_specs, out_shape, out_spec, (N // tn, M // tm, nk), nk, (tm, tn),
                    False, buf is not None)


def _rms_fwd(name, x, g, l, exchange=None):
    S, D = x.shape
    tm = _tile(S, NORM_ROWS, BF16_ROWS)
    n_i = S // tm

    def body(x_ref, g_ref, o_ref):
        xf = x_ref[...]
        r = lax.rsqrt(jnp.mean(xf * xf, axis=-1, keepdims=True) + EPS)
        o_ref[...] = (xf * r * g_ref[l:l + 1, :]).astype(BF16)

    body, in_specs, out_specs, out_shape, scratch, operands, aliases = _with_exchange(
        exchange, body, [pl.BlockSpec((tm, D), lambda i: (i, 0)), pl.BlockSpec(g.shape, lambda i: (0, 0))],
        [pl.BlockSpec((tm, D), lambda i: (i, 0))], [_sds((S, D), BF16)], [], [x, g],
        lambda: pl.program_id(0) == 0, lambda: pl.program_id(0) == n_i - 1)
    outs = _pcall(
        body, grid=(n_i,), in_specs=in_specs, out_specs=out_specs, out_shape=out_shape, scratch_shapes=scratch,
        input_output_aliases=aliases, compiler_params=_cp("arbitrary" if exchange else "parallel"), name=name,
    )(*operands)
    return outs if exchange else outs[0]


def _rms_bwd(name, x, g, l, dh, dres, exchange=None):
    S, D = x.shape
    tm = _tile(S, NORM_ROWS, SUBLANES)

    def body(x_ref, g_ref, dh_ref, dr_ref, dx_ref, dg_ref):
        xf = x_ref[...]
        r = lax.rsqrt(jnp.mean(xf * xf, axis=-1, keepdims=True) + EPS)
        xh = xf * r
        d = dh_ref[...]
        dxh = d * g_ref[l:l + 1, :]
        dx_ref[...] = dr_ref[...] + r * (dxh - xh * jnp.mean(dxh * xh, axis=-1, keepdims=True))

        @pl.when(pl.program_id(0) == 0)
        def _():
            dg_ref[...] = jnp.zeros_like(dg_ref)

        dg_ref[...] += _rowsum(d * xh)

    row = pl.BlockSpec((tm, D), lambda i: (i, 0))
    n_i = S // tm
    body, in_specs, out_specs, out_shape, scratch, operands, aliases = _with_exchange(
        exchange, body, [row, pl.BlockSpec(g.shape, lambda i: (0, 0)), row, row],
        [row, pl.BlockSpec((1, D), lambda i: (0, 0))], [_sds((S, D), F32), _sds((1, D), F32)], [], [x, g, dh, dres],
        lambda: pl.program_id(0) == 0, lambda: pl.program_id(0) == n_i - 1)
    return _pcall(
        body, grid=(n_i,), in_specs=in_specs, out_specs=out_specs, out_shape=out_shape, scratch_shapes=scratch,
        input_output_aliases=aliases, compiler_params=_cp("arbitrary"), name=name,
    )(*operands)


def _loss_fwd_bwd(name, y, t):
    S, D = y.shape
    tm = _tile(S, NORM_ROWS, SUBLANES)

    def body(y_ref, t_ref, dy_ref, l_ref):
        e = y_ref[...] - t_ref[...]
        dy_ref[...] = e * (1.0 / D)

        @pl.when(pl.program_id(0) == 0)
        def _():
            l_ref[...] = jnp.zeros_like(l_ref)

        l_ref[...] += 0.5 * jnp.sum(jnp.sum(e * e, axis=-1, keepdims=True) * (1.0 / D), axis=0, keepdims=True)

    row = pl.BlockSpec((tm, D), lambda i: (i, 0))
    return _pcall(
        body, grid=(S // tm,), in_specs=[row, row],
        out_specs=[row, pl.BlockSpec((SUBLANES, LANES), lambda i: (0, 0))],
        out_shape=[_sds((S, D), F32), _sds((SUBLANES, LANES), F32)],
        compiler_params=_cp("arbitrary"), name=name,
    )(y, t)


def _delayed_copies(us, n_rows):
    for s in range(1, SUBLANES):
        us[s, pl.ds(SUBLANES, n_rows - SUBLANES), :] = us[0, pl.ds(SUBLANES - s, n_rows - SUBLANES), :]


def _conv_a(aw_ref, ab_ref, l, us, row0, rows, dg):
    ka = CONV_A_WIDTH
    out = []
    for c0 in range(0, dg, LANES):
        lanes = slice(c0, c0 + LANES)
        acc = ab_ref[l:l + 1, lanes]
        for d in range(ka):
            a, s = divmod(d, SUBLANES)
            acc = acc + aw_ref[l, ka - 1 - d:ka - d, lanes] * us[s, pl.ds(row0 - SUBLANES * a, rows), lanes]
        out.append(acc)
    return jnp.concatenate(out, axis=1)


def _convmix_fwd(name, p, aw, ab, lg, lb, bw, l, exchange=None):
    S, W = p.shape
    dg = W // 5
    tm = _tile(S, 256, HALO_A)
    nb = tm // HALO_A
    ka, kb = CONV_A_WIDTH, CONV_B_WIDTH

    ext = HALO_A + tm
    rc = _tile(tm, ELT_ROWS, BF16_ROWS)

    def body(p_ref, ph_ref, aw_ref, ab_ref, lg_ref, lb_ref, bw_ref, o_ref, us, mext):
        first = pl.program_id(0) == 0
        ph = ph_ref[...]
        pc = p_ref[...]
        us[0, pl.ds(0, HALO_A), :] = jnp.where(first, 0.0, ph[:, 0:dg] * _sig(ph[:, dg:2 * dg]))
        us[0, pl.ds(HALO_A, tm), :] = pc[:, 0:dg] * _sig(pc[:, dg:2 * dg])
        mext[pl.ds(0, HALO_A), :] = jnp.where(first, 0.0, ph[:, 3 * dg:4 * dg] * ph[:, 4 * dg:5 * dg])
        mext[pl.ds(HALO_A, tm), :] = pc[:, 3 * dg:4 * dg] * pc[:, 4 * dg:5 * dg]
        _delayed_copies(us, ext)
        for r0 in range(0, tm, rc):
            rows = pl.ds(r0, rc)
            c = _conv_a(aw_ref, ab_ref, l, us, HALO_A + r0, rc, dg)
            xc = c - jnp.mean(c, axis=-1, keepdims=True)
            ln = xc * lax.rsqrt(jnp.mean(xc * xc, axis=-1, keepdims=True) + EPS) * lg_ref[l:l + 1, :] + lb_ref[l:l + 1, :]
            o_ref[rows, 0:dg] = (ln * _sig(ln)).astype(BF16)
            cb = bw_ref[l, 0:1, :] * mext[pl.ds(HALO_A - (kb - 1) + r0, rc), :]
            for k in range(1, kb):
                cb = cb + bw_ref[l, k:k + 1, :] * mext[pl.ds(HALO_A - (kb - 1) + k + r0, rc), :]
            o_ref[rows, dg:2 * dg] = (p_ref[rows, 2 * dg:3 * dg] * cb).astype(BF16)

    full = lambda a: pl.BlockSpec(a.shape, lambda i: (0,) * a.ndim)
    n_i = S // tm
    body, in_specs, out_specs, out_shape, scratch, operands, aliases = _with_exchange(
        exchange, body,
        [pl.BlockSpec((tm, W), lambda i: (i, 0)), pl.BlockSpec((HALO_A, W), lambda i: (jnp.maximum(i * nb - 1, 0), 0)),
         full(aw), full(ab), full(lg), full(lb), full(bw)],
        [pl.BlockSpec((tm, 2 * dg), lambda i: (i, 0))], [_sds((S, 2 * dg), BF16)],
        [pltpu.VMEM((SUBLANES, ext, dg), F32), pltpu.VMEM((ext, dg), F32)], [p, p, aw, ab, lg, lb, bw],
        lambda: pl.program_id(0) == 0, lambda: pl.program_id(0) == n_i - 1)
    outs = _pcall(
        body, grid=(n_i,), in_specs=in_specs, out_specs=out_specs, out_shape=out_shape, scratch_shapes=scratch,
        input_output_aliases=aliases, compiler_params=_cp("arbitrary" if exchange else "parallel"), name=name,
    )(*operands)
    return outs if exchange else outs[0]


def _convmix_bwd(name, p, dab, aw, ab, lg, lb, bw, l, exchange=None):
    S, W = p.shape
    dg = W // 5
    tm = _tile(S, 256, HALO_A)
    nb = tm // HALO_A
    n_i = S // tm
    ka, kb = CONV_A_WIDTH, CONV_B_WIDTH
    n = tm + HALO_A
    ext = HALO_A + n
    rc = _tile(tm, ELT_ROWS, BF16_ROWS)

    def body(p_ref, pp_ref, pn_ref, d_ref, dn_ref, aw_ref, ab_ref, lg_ref, lb_ref, bw_ref,
             dp_ref, daw_ref, dab_ref, dlg_ref, dlb_ref, dbw_ref, us, mext, dcs, dbext, accw):
        i = pl.program_id(0)
        first, last = i == 0, i == n_i - 1

        @pl.when(first)
        def _():
            for r in (daw_ref, dab_ref, dlg_ref, dlb_ref, dbw_ref):
                r[...] = jnp.zeros_like(r)

        accw[...] = jnp.zeros_like(accw)
        pp, pc, pn = pp_ref[...], p_ref[...], pn_ref[...]
        glu = lambda b: b[:, 0:dg] * _sig(b[:, dg:2 * dg])
        gch = lambda b: b[:, 3 * dg:4 * dg] * b[:, 4 * dg:5 * dg]
        us[0, pl.ds(0, HALO_A), :] = jnp.where(first, 0.0, glu(pp))
        us[0, pl.ds(HALO_A, tm), :] = glu(pc)
        us[0, pl.ds(HALO_A + tm, HALO_A), :] = glu(pn)
        mext[pl.ds(0, HALO_A), :] = jnp.where(first, 0.0, gch(pp))
        mext[pl.ds(HALO_A, tm), :] = gch(pc)
        mext[pl.ds(HALO_A + tm, HALO_A), :] = gch(pn)
        _delayed_copies(us, ext)
        chunks = [(r, rc) for r in range(0, tm, rc)] + [(tm, HALO_A)]
        g_ln = lg_ref[l:l + 1, :]
        zero8 = jnp.zeros((SUBLANES, dg), F32)

        acc_lg = acc_lb = acc_ab = zero8
        for r0, rows in chunks:
            c = _conv_a(aw_ref, ab_ref, l, us, HALO_A + r0, rows, dg)
            xc = c - jnp.mean(c, axis=-1, keepdims=True)
            rstd = lax.rsqrt(jnp.mean(xc * xc, axis=-1, keepdims=True) + EPS)
            chat = xc * rstd
            ln = chat * g_ln + lb_ref[l:l + 1, :]
            s = _sig(ln)
            da = d_ref[pl.ds(r0, rows), 0:dg] if r0 < tm else jnp.where(last, 0.0, dn_ref[:, 0:dg])
            dln = da * (s * (1.0 + ln * (1.0 - s)))
            dlnh = dln * g_ln
            dc = rstd * (dlnh - jnp.mean(dlnh, axis=-1, keepdims=True)
                         - chat * jnp.mean(dlnh * chat, axis=-1, keepdims=True))
            dcs[0, pl.ds(r0, rows), :] = dc
            if r0 < tm:
                acc_lg = acc_lg + _fold(dln * chat)
                acc_lb = acc_lb + _fold(dln)
                acc_ab = acc_ab + _fold(dc)
                for c0 in range(0, dg, LANES):
                    lanes = slice(c0, c0 + LANES)
                    for d in range(ka):
                        a, sh = divmod(d, SUBLANES)
                        k = ka - 1 - d
                        accw[pl.ds(SUBLANES * k, SUBLANES), lanes] += _fold(
                            dc[:, lanes] * us[sh, pl.ds(HALO_A + r0 - SUBLANES * a, rows), lanes])
        dlg_ref[...] += _rowsum(acc_lg)
        dlb_ref[...] += _rowsum(acc_lb)
        dab_ref[...] += _rowsum(acc_ab)
        for k in range(ka):
            daw_ref[k:k + 1, :] += _rowsum(accw[pl.ds(SUBLANES * k, SUBLANES), :])
        for s in range(1, SUBLANES):
            dcs[s, pl.ds(0, n - SUBLANES), :] = dcs[0, pl.ds(s, n - SUBLANES), :]
        for r0 in range(0, tm, rc):
            rows = pl.ds(r0, rc)
            parts = []
            for c0 in range(0, dg, LANES):
                lanes = slice(c0, c0 + LANES)
                acc = aw_ref[l, ka - 1:ka, lanes] * dcs[0, rows, lanes]
                for e in range(1, ka):
                    a, sh = divmod(e, SUBLANES)
                    acc = acc + aw_ref[l, ka - 1 - e:ka - e, lanes] * dcs[sh, pl.ds(r0 + SUBLANES * a, rc), lanes]
                parts.append(acc)
            du = jnp.concatenate(parts, axis=1)
            sg = _sig(p_ref[rows, dg:2 * dg])
            dp_ref[rows, 0:dg] = (du * sg).astype(BF16)
            dp_ref[rows, dg:2 * dg] = (du * p_ref[rows, 0:dg] * sg * (1.0 - sg)).astype(BF16)

        for r0, rows in chunks:
            if r0 < tm:
                dbext[pl.ds(r0, rows), :] = d_ref[pl.ds(r0, rows), dg:2 * dg] * p_ref[pl.ds(r0, rows), 2 * dg:3 * dg]
            else:
                dbext[pl.ds(r0, rows), :] = jnp.where(last, 0.0, dn_ref[:, dg:2 * dg] * pn[:, 2 * dg:3 * dg])
        acc_bw = [zero8] * kb
        for r0 in range(0, tm, rc):
            rows = pl.ds(r0, rc)
            m_k = [mext[pl.ds(HALO_A - (kb - 1) + k + r0, rc), :] for k in range(kb)]
            cb = bw_ref[l, 0:1, :] * m_k[0]
            dm = bw_ref[l, 0:1, :] * dbext[pl.ds(r0 + kb - 1, rc), :]
            for k in range(1, kb):
                cb = cb + bw_ref[l, k:k + 1, :] * m_k[k]
                dm = dm + bw_ref[l, k:k + 1, :] * dbext[pl.ds(r0 + kb - 1 - k, rc), :]
            dcb = dbext[rows, :]
            acc_bw = [acc_bw[k] + _fold(dcb * m_k[k]) for k in range(kb)]
            dp_ref[rows, 2 * dg:3 * dg] = (d_ref[rows, dg:2 * dg] * cb).astype(BF16)
            dp_ref[rows, 3 * dg:4 * dg] = (dm * p_ref[rows, 4 * dg:5 * dg]).astype(BF16)
            dp_ref[rows, 4 * dg:5 * dg] = (dm * p_ref[rows, 3 * dg:4 * dg]).astype(BF16)
        for k in range(kb):
            dbw_ref[k:k + 1, :] += _rowsum(acc_bw[k])

    full = lambda a: pl.BlockSpec(a.shape, lambda i: (0,) * a.ndim)
    prev = lambda i: (jnp.maximum(i * nb - 1, 0), 0)
    nxt = lambda i: (jnp.minimum((i + 1) * nb, S // HALO_A - 1), 0)
    acc = lambda r: pl.BlockSpec((r, dg), lambda i: (0, 0))
    body, in_specs, out_specs, out_shape, scratch, operands, aliases = _with_exchange(
        exchange, body,
        [pl.BlockSpec((tm, W), lambda i: (i, 0)), pl.BlockSpec((HALO_A, W), prev), pl.BlockSpec((HALO_A, W), nxt),
         pl.BlockSpec((tm, 2 * dg), lambda i: (i, 0)), pl.BlockSpec((HALO_A, 2 * dg), nxt),
         full(aw), full(ab), full(lg), full(lb), full(bw)],
        [pl.BlockSpec((tm, W), lambda i: (i, 0)), acc(ka), acc(1), acc(1), acc(1), acc(kb)],
        [_sds((S, W), BF16), _sds((ka, dg), F32), _sds((1, dg), F32), _sds((1, dg), F32), _sds((1, dg), F32),
         _sds((kb, dg), F32)],
        [pltpu.VMEM((SUBLANES, ext, dg), F32), pltpu.VMEM((ext, dg), F32), pltpu.VMEM((SUBLANES, n, dg), F32),
         pltpu.VMEM((n, dg), F32), pltpu.VMEM((SUBLANES * ka, dg), F32)],
        [p, p, p, dab, dab, aw, ab, lg, lb, bw],
        lambda: pl.program_id(0) == 0, lambda: pl.program_id(0) == n_i - 1)
    return _pcall(
        body, grid=(n_i,), in_specs=in_specs, out_specs=out_specs, out_shape=out_shape, scratch_shapes=scratch,
        input_output_aliases=aliases, compiler_params=_cp("arbitrary"), name=name,
    )(*operands)


def _ffn_mid_fwd(name, u2, dww, dwb, l, exchange=None):
    _, S, F = u2.shape
    tm = _tile(S, FFN_MID_ROWS, BF16_ROWS)
    tc = _tile(F, 1408)
    n_f = F // tc
    nb = tm // HALO_S
    kf = FFN_CONV_WIDTH

    def body(u_ref, uh_ref, wg_ref, wv_ref, bg_ref, bv_ref, o_ref, ext):
        first = pl.program_id(1) == 0
        ext[:, pl.ds(0, HALO_S), :] = jnp.where(first, 0.0, uh_ref[...])
        ext[:, pl.ds(HALO_S, tm), :] = u_ref[...]
        rc = _tile(tm, ELT_ROWS, BF16_ROWS)

        def lane_chunk(ci, carry):
            lanes = pl.ds(pl.multiple_of(ci * LANES, LANES), LANES)
            taps = [[w_ref[k:k + 1, lanes] for k in range(kf)] for w_ref in (wg_ref, wv_ref)]
            bias = [b_ref[l:l + 1, lanes] for b_ref in (bg_ref, bv_ref)]
            for r0 in range(0, tm, rc):
                c = []
                for g in range(2):
                    acc = bias[g]
                    for k in range(kf):
                        acc = acc + taps[g][k] * ext[g, pl.ds(HALO_S - (kf - 1) + k + r0, rc), lanes]
                    c.append(acc)
                o_ref[pl.ds(r0, rc), lanes] = (c[0] * _sig(c[0]) * c[1]).astype(BF16)
            return carry

        lax.fori_loop(0, tc // LANES, lane_chunk, 0)

    n_l = dwb.shape[0]
    n_i = S // tm
    body, in_specs, out_specs, out_shape, scratch, operands, aliases = _with_exchange(
        exchange, body,
        [pl.BlockSpec((2, tm, tc), lambda j, i: (0, i, j)),
         pl.BlockSpec((2, HALO_S, tc), lambda j, i: (0, jnp.maximum(i * nb - 1, 0), j)),
         pl.BlockSpec((None, kf, tc), lambda j, i: (l, 0, j)),
         pl.BlockSpec((None, kf, tc), lambda j, i: (l, 0, j + n_f)),
         pl.BlockSpec((n_l, tc), lambda j, i: (0, j)),
         pl.BlockSpec((n_l, tc), lambda j, i: (0, j + n_f))],
        [pl.BlockSpec((tm, tc), lambda j, i: (i, j))], [_sds((S, F), BF16)],
        [pltpu.VMEM((2, HALO_S + tm, tc), F32)], [u2, u2, dww, dww, dwb, dwb],
        lambda: jnp.logical_and(pl.program_id(0) == 0, pl.program_id(1) == 0),
        lambda: jnp.logical_and(pl.program_id(0) == n_f - 1, pl.program_id(1) == n_i - 1))
    sem = "arbitrary" if exchange else "parallel"
    outs = _pcall(
        body, grid=(n_f, n_i), in_specs=in_specs, out_specs=out_specs, out_shape=out_shape, scratch_shapes=scratch,
        input_output_aliases=aliases, compiler_params=_cp(sem, sem), name=name,
    )(*operands)
    return outs if exchange else outs[0]


def _ffn_mid_bwd(name, u2, df, dww, dwb, l, exchange=None):
    _, S, F = u2.shape
    tm = _tile(S, FFN_MID_ROWS, BF16_ROWS)
    tc = _tile(F, 1408)
    n_f = F // tc
    nb = tm // HALO_S
    n_i = S // tm
    kf = FFN_CONV_WIDTH
    n = tm + HALO_S

    def body(u_ref, up_ref, un_ref, df_ref, dfn_ref, wg_ref, wv_ref, bg_ref, bv_ref,
             du_ref, dw_ref, db_ref, uext, dcext):
        i = pl.program_id(1)
        first, last = i == 0, i == n_i - 1

        @pl.when(first)
        def _():
            dw_ref[...] = jnp.zeros_like(dw_ref)
            db_ref[...] = jnp.zeros_like(db_ref)

        uext[:, pl.ds(0, HALO_S), :] = jnp.where(first, 0.0, up_ref[...])
        uext[:, pl.ds(HALO_S, tm), :] = u_ref[...]
        uext[:, pl.ds(HALO_S + tm, HALO_S), :] = un_ref[...]
        rc = _tile(tm, ELT_ROWS, BF16_ROWS)

        def lane_chunk(ci, carry):
            lanes = pl.ds(pl.multiple_of(ci * LANES, LANES), LANES)
            taps = [[w_ref[k:k + 1, lanes] for k in range(kf)] for w_ref in (wg_ref, wv_ref)]
            bias = [b_ref[l:l + 1, lanes] for b_ref in (bg_ref, bv_ref)]
            acc_w = [[jnp.zeros((SUBLANES, LANES), F32) for _ in range(kf)] for _ in range(2)]
            acc_b = [jnp.zeros((SUBLANES, LANES), F32) for _ in range(2)]
            for r0, rows in [(r, rc) for r in range(0, tm, rc)] + [(tm, HALO_S)]:
                shifted = [[uext[g, pl.ds(HALO_S - (kf - 1) + k + r0, rows), lanes] for k in range(kf)] for g in range(2)]
                conv = []
                for g in range(2):
                    acc = bias[g]
                    for k in range(kf):
                        acc = acc + taps[g][k] * shifted[g][k]
                    conv.append(acc)
                cg, cv = conv
                s = _sig(cg)
                dfe = df_ref[pl.ds(r0, rows), lanes] if r0 < tm else jnp.where(last, 0.0, dfn_ref[:, lanes])
                dc = [dfe * cv * (s * (1.0 + cg * (1.0 - s))), dfe * (cg * s)]
                for g in range(2):
                    dcext[g, pl.ds(r0, rows), lanes] = dc[g]
                    if r0 < tm:
                        acc_b[g] = acc_b[g] + _fold(dc[g])
                        for k in range(kf):
                            acc_w[g][k] = acc_w[g][k] + _fold(dc[g] * shifted[g][k])
            for r0 in range(0, tm, rc):
                for g in range(2):
                    du = taps[g][0] * dcext[g, pl.ds(r0 + kf - 1, rc), lanes]
                    for k in range(1, kf):
                        du = du + taps[g][k] * dcext[g, pl.ds(r0 + kf - 1 - k, rc), lanes]
                    du_ref[g, pl.ds(r0, rc), lanes] = du.astype(BF16)
            for g in range(2):
                db_ref[g, :, lanes] += _rowsum(acc_b[g])
                for k in range(kf):
                    dw_ref[g, k:k + 1, lanes] += _rowsum(acc_w[g][k])
            return carry

        lax.fori_loop(0, tc // LANES, lane_chunk, 0)

    n_l = dwb.shape[0]
    prev = lambda j, i: (0, jnp.maximum(i * nb - 1, 0), j)
    nxt = lambda j, i: (0, jnp.minimum((i + 1) * nb, S // HALO_S - 1), j)
    body, in_specs, out_specs, out_shape, scratch, operands, aliases = _with_exchange(
        exchange, body,
        [pl.BlockSpec((2, tm, tc), lambda j, i: (0, i, j)),
         pl.BlockSpec((2, HALO_S, tc), prev), pl.BlockSpec((2, HALO_S, tc), nxt),
         pl.BlockSpec((tm, tc), lambda j, i: (i, j)),
         pl.BlockSpec((HALO_S, tc), lambda j, i: nxt(j, i)[1:]),
         pl.BlockSpec((None, kf, tc), lambda j, i: (l, 0, j)),
         pl.BlockSpec((None, kf, tc), lambda j, i: (l, 0, j + n_f)),
         pl.BlockSpec((n_l, tc), lambda j, i: (0, j)),
         pl.BlockSpec((n_l, tc), lambda j, i: (0, j + n_f))],
        [pl.BlockSpec((2, tm, tc), lambda j, i: (0, i, j)),
         pl.BlockSpec((2, kf, tc), lambda j, i: (0, 0, j)),
         pl.BlockSpec((2, 1, tc), lambda j, i: (0, 0, j))],
        [_sds((2, S, F), BF16), _sds((2, kf, F), F32), _sds((2, 1, F), F32)],
        [pltpu.VMEM((2, HALO_S + n, tc), F32), pltpu.VMEM((2, n, tc), F32)],
        [u2, u2, u2, df, df, dww, dww, dwb, dwb],
        lambda: jnp.logical_and(pl.program_id(0) == 0, pl.program_id(1) == 0),
        lambda: jnp.logical_and(pl.program_id(0) == n_f - 1, pl.program_id(1) == n_i - 1))
    return _pcall(
        body, grid=(n_f, n_i), in_specs=in_specs, out_specs=out_specs, out_shape=out_shape, scratch_shapes=scratch,
        input_output_aliases=aliases, compiler_params=_cp("arbitrary" if exchange else "parallel", "arbitrary"), name=name,
    )(*operands)


def _head_sum_matrix():
    r = lax.broadcasted_iota(jnp.int32, (LANES, LANES), 0) // HEAD_DIM
    c = lax.broadcasted_iota(jnp.int32, (LANES, LANES), 1) // HEAD_DIM
    return (r == c).astype(BF16)


def _head_mean(x, ones):
    return _split_dot(x, ones) * (1.0 / HEAD_DIM)


def _qknorm_fwd(name, qkv, g2):
    S, D3 = qkv.shape
    D = D3 // 3
    tm = _tile(S, NORM_ROWS // 2, BF16_ROWS)
    scale = HEAD_DIM ** -0.5

    def body(q_ref, k_ref, v_ref, g_ref, qo_ref, ko_ref, vo_ref):
        ones = _head_sum_matrix()
        for cc in range(D // LANES):
            sl = slice(cc * LANES, (cc + 1) * LANES)
            for x_ref, o_ref, row, mult in ((q_ref, qo_ref, 0, scale), (k_ref, ko_ref, 1, 1.0)):
                x = x_ref[:, sl]
                r = lax.rsqrt(_head_mean(x * x, ones) + EPS)
                o_ref[:, sl] = ((x * r * g_ref[row:row + 1, :]).astype(BF16) * mult).astype(BF16)
        vo_ref[...] = v_ref[...].astype(BF16)

    col = lambda c: pl.BlockSpec((tm, D), lambda i: (i, c))
    out = pl.BlockSpec((tm, D), lambda i: (i, 0))
    return _pcall(
        body, grid=(S // tm,),
        in_specs=[col(0), col(1), col(2), pl.BlockSpec(g2.shape, lambda i: (0, 0))],
        out_specs=[out, out, out], out_shape=[_sds((S, D), BF16)] * 3,
        compiler_params=_cp("parallel"), name=name,
    )(qkv, qkv, qkv, g2)


def _qknorm_bwd(name, qkv, dq, dk, dv, g2):
    S, D3 = qkv.shape
    D = D3 // 3
    tm = _tile(S, NORM_ROWS // 2, BF16_ROWS)
    scale = HEAD_DIM ** -0.5

    def body(q_ref, k_ref, dq_ref, dk_ref, dv_ref, g_ref, o_ref, dg_ref):
        @pl.when(pl.program_id(0) == 0)
        def _():
            dg_ref[...] = jnp.zeros_like(dg_ref)

        ones = _head_sum_matrix()
        for cc in range(D // LANES):
            sl = slice(cc * LANES, (cc + 1) * LANES)
            for x_ref, d_ref, row, mult, base in ((q_ref, dq_ref, 0, scale, 0), (k_ref, dk_ref, 1, 1.0, D)):
                x = x_ref[:, sl]
                r = lax.rsqrt(_head_mean(x * x, ones) + EPS)
                xh = x * r
                dn = d_ref[:, sl] * mult
                dxh = dn * g_ref[row:row + 1, :]
                dx = r * (dxh - xh * _head_mean(dxh * xh, ones))
                o_ref[:, base + cc * LANES:base + (cc + 1) * LANES] = dx.astype(BF16)
                dg_ref[row:row + 1, :] += _rowsum(dn * xh)
        o_ref[:, 2 * D:3 * D] = dv_ref[...].astype(BF16)

    col = lambda c: pl.BlockSpec((tm, D), lambda i: (i, c))
    row = pl.BlockSpec((tm, D), lambda i: (i, 0))
    return _pcall(
        body, grid=(S // tm,),
        in_specs=[col(0), col(1), row, row, row, pl.BlockSpec(g2.shape, lambda i: (0, 0))],
        out_specs=[pl.BlockSpec((tm, D3), lambda i: (i, 0)), pl.BlockSpec((2, LANES), lambda i: (0, 0))],
        out_shape=[_sds((S, D3), BF16), _sds((2, LANES), F32)],
        compiler_params=_cp("arbitrary"), name=name,
    )(qkv, qkv, dq, dk, dv, g2)


def _attn_consts():
    t = ATTN_BLOCK
    row = lax.broadcasted_iota(jnp.int32, (t, t), 0)
    col = lax.broadcasted_iota(jnp.int32, (t, t), 1)
    lane = lax.broadcasted_iota(jnp.int32, (1, LANES), 1)
    heads = (lane < HEAD_DIM, lane >= HEAD_DIM)
    return row, col, heads


def _split_dot(x, m):
    n = x.shape[0]
    hi = x.astype(BF16)
    lo = (x - hi.astype(F32)).astype(BF16)
    both = jnp.dot(jnp.concatenate([hi, lo], axis=0), m, preferred_element_type=F32)
    return both[:n] + both[n:]


def _log_keep(z):
    return -(jnp.maximum(z, 0.0) + jnp.log(1.0 + jnp.exp(-jnp.abs(z))))


def _stack_heads(a, heads):
    t = ATTN_BLOCK
    zero = jnp.zeros((t, LANES), a.dtype)
    return jnp.concatenate([jnp.where(h, a[s * t:(s + 1) * t], zero) for s in range(a.shape[0] // t) for h in heads], axis=0)


def _side_by_side(a):
    t = ATTN_BLOCK
    return jnp.concatenate([jnp.concatenate([a[2 * s * t:(2 * s + 1) * t], a[(2 * s + 1) * t:(2 * s + 2) * t]], axis=1)
                            for s in range(a.shape[0] // (2 * t))], axis=0)


def _grow(a, rows, cols):
    z = jnp.zeros((rows, cols), F32)
    return z if a is None else jnp.concatenate([z, a], axis=0)


def _attn_fwd(name, qs, kn, vb, exchange=None):
    S, D = qs.shape
    t = ATTN_BLOCK
    tq = ATTN_SUB * t

    def body(q_ref, k_ref, v_ref, o_ref):
        i = pl.program_id(1)
        row, col, heads = _attn_consts()
        after_m = (row > col).astype(BF16)
        causal = col < row
        q_all = _stack_heads(q_ref[...], heads)

        def blocks(specs, r, acc):
            n_rows = q_all.shape[0]
            offs = [pl.multiple_of(j * t, t) for j, _, _ in specs]
            zs = [lax.dot_general(q_all[lo:], k_ref[pl.ds(off, t), :], NT, preferred_element_type=F32)
                  for off, (_, lo, _) in zip(offs, specs)]
            lks = []
            for z, (_, _, mask) in zip(zs, specs):
                lk = _log_keep(z)
                lks.append(lk if mask is None else jnp.where(mask, lk, 0.0))
            cums = [_split_dot(lk, after_m) for lk in lks]
            ws = []
            for z, lk, cum, (_, lo, mask) in zip(zs, lks, cums, specs):
                rows = n_rows - lo
                r = _grow(r, rows - (0 if r is None else r.shape[0]), 1) if r is None or r.shape[0] < rows else r
                w = jnp.exp(z + lk + cum + r)
                ws.append((w if mask is None else jnp.where(mask, w, 0.0)).astype(BF16))
                r = r + jnp.sum(lk, axis=1, keepdims=True)
            acc = jnp.zeros((n_rows // 2, LANES), F32) if acc is None else acc
            for w, off, (_, lo, _) in zip(ws, offs, specs):
                part = jnp.dot(_side_by_side(w), _stack_heads(v_ref[pl.ds(off, t), :], heads), preferred_element_type=F32)
                acc = acc + (part if lo == 0 else _grow(part, lo // 2, LANES))
            return r, acc

        def head(n_more):
            specs = [(ATTN_SUB * i + s, 2 * s * t,
                      jnp.concatenate([causal, causal] + [jnp.ones_like(causal)] * (2 * (ATTN_SUB - 1 - s)), axis=0))
                     for s in reversed(range(ATTN_SUB))]
            specs += [(ATTN_SUB * i - 1 - b, 0, None) for b in range(n_more)]
            return blocks(specs, None, None)

        r, acc = lax.cond(ATTN_SUB * i >= ATTN_MORE, lambda: head(ATTN_MORE), lambda: head(0))

        def cond(c):
            return jnp.logical_and(c[0] >= 0, jnp.max(c[1]) > EXP_UNDERFLOW)

        def step(c):
            r, a = blocks([(c[0], 0, None)], c[1], c[2])
            return c[0] - 1, r, a

        first = jnp.where(ATTN_SUB * i >= ATTN_MORE, ATTN_SUB * i - 1 - ATTN_MORE, ATTN_SUB * i - 1)
        o_ref[...] = lax.while_loop(cond, step, (first, r, acc))[2]

    n_hp = D // LANES
    blk = pl.BlockSpec((tq, LANES), lambda hp, i: (i, hp))
    seq = pl.BlockSpec((S, LANES), lambda hp, i: (0, hp))
    n_i = S // tq
    body, in_specs, out_specs, out_shape, scratch, operands, aliases = _with_exchange(
        exchange, body, [blk, seq, seq], [blk], [_sds((S, D), F32)], [], [qs, kn, vb],
        lambda: jnp.logical_and(pl.program_id(0) == 0, pl.program_id(1) == 0),
        lambda: jnp.logical_and(pl.program_id(0) == n_hp - 1, pl.program_id(1) == n_i - 1))
    outs = _pcall(
        body, grid=(n_hp, n_i), in_specs=in_specs, out_specs=out_specs, out_shape=out_shape, scratch_shapes=scratch,
        input_output_aliases=aliases, compiler_params=_cp("arbitrary" if exchange else "parallel", "arbitrary"), name=name,
    )(*operands)
    return outs if exchange else outs[0]


def _attn_bwd(name, qs, kn, vb, o, do, exchange=None):
    S, D = qs.shape
    t = ATTN_BLOCK
    tq = ATTN_SUB * t

    def body(q_ref, k_ref, v_ref, o_ref, do_ref, dq_ref, dk_ref, dv_ref):
        i = pl.program_id(1)

        @pl.when(i == 0)
        def _():
            dk_ref[...] = jnp.zeros_like(dk_ref)
            dv_ref[...] = jnp.zeros_like(dv_ref)

        row, col, heads = _attn_consts()
        after_m = (row > col).astype(BF16)
        from_m = (row >= col).astype(BF16)
        causal = col < row
        q_all = _stack_heads(q_ref[...], heads)
        dob = do_ref[...].astype(BF16)
        do_all = _stack_heads(dob, heads)
        dsum_all = jnp.sum(_stack_heads(dob.astype(F32) * o_ref[...], heads), axis=1, keepdims=True)

        def blocks(specs, r, es, dq):
            n_rows = q_all.shape[0]
            offs = [pl.multiple_of(j * t, t) for j, _, _ in specs]
            masked = lambda x, mask: x if mask is None else jnp.where(mask, x, 0.0)
            top = lambda a, rows: a if a is not None and a.shape[0] == rows else _grow(a, rows - (0 if a is None else a.shape[0]), 1)
            zs = [lax.dot_general(q_all[lo:], k_ref[pl.ds(off, t), :], NT, preferred_element_type=F32)
                  for off, (_, lo, _) in zip(offs, specs)]
            gs = [lax.dot_general(do_all[lo:], v_ref[pl.ds(off, t), :], NT, preferred_element_type=F32)
                  for off, (_, lo, _) in zip(offs, specs)]
            lks = [masked(_log_keep(z), mask) for z, (_, _, mask) in zip(zs, specs)]
            cums = [_split_dot(lk, after_m) for lk in lks]
            ws, es_blk, sgs = [], [], []
            for z, g, lk, cum, (_, lo, mask) in zip(zs, gs, lks, cums, specs):
                r = top(r, n_rows - lo)
                ls = z + lk
                w = masked(jnp.exp(ls + cum + r), mask)
                ws.append(w.astype(BF16))
                es_blk.append(w * g)
                sgs.append(jnp.exp(ls))
                r = r + jnp.sum(lk, axis=1, keepdims=True)
            cum_es = [_split_dot(e, from_m) for e in es_blk]
            dzs = []
            for e, cum_e, sg, (_, lo, mask) in zip(es_blk, cum_es, sgs, specs):
                es = top(es, n_rows - lo)
                before = dsum_all[lo:] - (es + cum_e)
                dzs.append(masked(e - (e + before) * sg, mask).astype(BF16))
                es = es + jnp.sum(e, axis=1, keepdims=True)
            dq = jnp.zeros((n_rows // 2, LANES), F32) if dq is None else dq
            for dzb, w, off, (_, lo, _) in zip(dzs, ws, offs, specs):
                part = jnp.dot(_side_by_side(dzb), _stack_heads(k_ref[pl.ds(off, t), :], heads), preferred_element_type=F32)
                dq = dq + (part if lo == 0 else _grow(part, lo // 2, LANES))
                dk_ref[pl.ds(off, t), :] += lax.dot_general(dzb, q_all[lo:], TN, preferred_element_type=F32)
                dv_ref[pl.ds(off, t), :] += lax.dot_general(w, do_all[lo:], TN, preferred_element_type=F32)
            return r, es, dq

        def head(n_more):
            specs = [(ATTN_SUB * i + s, 2 * s * t,
                      jnp.concatenate([causal, causal] + [jnp.ones_like(causal)] * (2 * (ATTN_SUB - 1 - s)), axis=0))
                     for s in reversed(range(ATTN_SUB))]
            specs += [(ATTN_SUB * i - 1 - b, 0, None) for b in range(n_more)]
            return blocks(specs, None, None, None)

        r, es, dq = lax.cond(ATTN_SUB * i >= ATTN_MORE, lambda: head(ATTN_MORE), lambda: head(0))

        def cond(c):
            return jnp.logical_and(c[0] >= 0, jnp.max(c[1]) > EXP_UNDERFLOW)

        def step(c):
            r, es, a = blocks([(c[0], 0, None)], c[1], c[2], c[3])
            return c[0] - 1, r, es, a

        first = jnp.where(ATTN_SUB * i >= ATTN_MORE, ATTN_SUB * i - 1 - ATTN_MORE, ATTN_SUB * i - 1)
        dq_ref[...] = lax.while_loop(cond, step, (first, r, es, dq))[3]

    n_hp = D // LANES
    blk = pl.BlockSpec((tq, LANES), lambda hp, i: (i, hp))
    seq = pl.BlockSpec((S, LANES), lambda hp, i: (0, hp))
    n_i = S // tq
    body, in_specs, out_specs, out_shape, scratch, operands, aliases = _with_exchange(
        exchange, body, [blk, seq, seq, blk, blk], [blk, seq, seq], [_sds((S, D), F32)] * 3, [], [qs, kn, vb, o, do],
        lambda: jnp.logical_and(pl.program_id(0) == 0, pl.program_id(1) == 0),
        lambda: jnp.logical_and(pl.program_id(0) == n_hp - 1, pl.program_id(1) == n_i - 1))
    return _pcall(
        body, grid=(n_hp, n_i), in_specs=in_specs, out_specs=out_specs, out_shape=out_shape, scratch_shapes=scratch,
        input_output_aliases=aliases, compiler_params=_cp("arbitrary" if exchange else "parallel", "arbitrary"), name=name,
    )(*operands)


def _adamw(name, w, g, m, v):
    L, R, C = w.shape
    tr = _tile(R, 256, SUBLANES)
    c1 = 1.0 - ADAM_B1 ** ADAM_STEP
    c2 = 1.0 - ADAM_B2 ** ADAM_STEP

    def body(w_ref, g_ref, m_ref, v_ref, d_ref, mo_ref, vo_ref):
        gg = g_ref[...]
        mn = ADAM_B1 * m_ref[...] + (1.0 - ADAM_B1) * gg
        vn = ADAM_B2 * v_ref[...] + (1.0 - ADAM_B2) * (gg * gg)
        d_ref[...] = -ADAM_LR * ((mn / c1) / (jnp.sqrt(vn / c2) + ADAM_EPS) + ADAM_WD * w_ref[...])
        mo_ref[...] = mn
        vo_ref[...] = vn

    blk = pl.BlockSpec((None, tr, C), lambda l, i: (l, i, 0))
    return _pcall(
        body, grid=(L, R // tr), in_specs=[blk] * 4, out_specs=[blk] * 3, out_shape=[_sds(w.shape, F32)] * 3,
        compiler_params=_cp("parallel", "parallel"), name=name,
    )(w, g, m, v)


def _place():
    x, y, c = lax.axis_index("x"), lax.axis_index("y"), lax.axis_index("c")
    chips = [(1 - x, y), (x, 1 - y), (1 - x, 1 - y)]
    return x, y, c, chips


def _place_shard(name, w, j_idx):
    L, R, X = w.shape
    rh = R // 2
    tr = _tile(rh, 256, BF16_ROWS)

    def body(j_ref, w_ref, o_ref):
        o_ref[...] = w_ref[...].astype(BF16)

    return _pcall(
        body,
        grid_spec=pltpu.PrefetchScalarGridSpec(
            num_scalar_prefetch=1, grid=(L, 2, rh // tr),
            in_specs=[pl.BlockSpec((None, None, tr, X), lambda l, h, i, j_ref: (l, h, i, 0))],
            out_specs=pl.BlockSpec((None, None, None, tr, X), lambda l, h, i, j_ref: (l, j_ref[0], h, i, 0))),
        out_shape=_sds((L, N_CHIPS, 2, rh, X), BF16), compiler_params=_cp("parallel", "parallel", "parallel"), name=name,
    )(j_idx, w.reshape(L, 2, rh, X))


def _all_gather_weights(bufs, spans, small_ws):
    n_big, n_small = len(bufs), len(small_ws)
    n_in = n_big + n_small
    layers = [pl.ds(l0, n) for l0, n in spans]

    def body(*refs):
        ins, outs = refs[:n_in], refs[n_in:2 * n_in]
        send_sems, recv_sems, local_sems = refs[2 * n_in:]
        x, y, c, chips = _place()
        j_me = 2 * x + y
        j_of = [2 * cx + cy for cx, cy in chips]
        sibling = (x, y, 1 - c)

        def remote(src, dst, s, to):
            return pltpu.make_async_remote_copy(src_ref=src, dst_ref=dst, send_sem=send_sems.at[s], recv_sem=recv_sems.at[s],
                                                device_id=to, device_id_type=MESH)

        started = []
        for t in range(n_big, n_in):
            loc = pltpu.make_async_copy(ins[t], outs[t].at[:, j_me], local_sems.at[t - n_big])
            loc.start()
            started.append(loc)
        first = []
        for t in range(n_big):
            mine = outs[t].at[layers[t], j_me, c]
            for k in range(3):
                first.append(remote(mine, mine, 6 * t + k, (*chips[k], c)))
        for t in range(n_big, n_in):
            for k in range(3):
                first.append(remote(ins[t], outs[t].at[:, j_me], 6 * n_big + 3 * (t - n_big) + k, (*chips[k], c)))
        for cp in first:
            cp.start()
        passed = []
        for t in range(n_big):
            for k in range(3):
                landed = outs[t].at[layers[t], j_of[k], c]
                remote(landed, landed, 6 * t + k, (*chips[k], c)).wait_recv()
                fwd = remote(landed, landed, 6 * t + 3 + k, sibling)
                fwd.start()
                passed.append(fwd)
        for t in range(n_big):
            for k in range(3):
                other = outs[t].at[layers[t], j_of[k], 1 - c]
                remote(other, other, 6 * t + 3 + k, sibling).wait_recv()
        for t in range(n_big, n_in):
            for k in range(3):
                dst = outs[t].at[:, j_of[k]]
                remote(dst, dst, 6 * n_big + 3 * (t - n_big) + k, (*chips[k], c)).wait_recv()
        for cp in first + passed:
            cp.wait_send()
        for loc in started:
            loc.wait()

    out_shape = [_sds(b.shape, b.dtype) for b in bufs]
    out_shape += [_sds((w.shape[0], N_CHIPS) + w.shape[1:], w.dtype) for w in small_ws]
    n_sem = 6 * n_big + 3 * n_small
    outs = _pcall(
        body, in_specs=[ANY] * n_in, out_specs=[ANY] * n_in, out_shape=out_shape,
        input_output_aliases={t: t for t in range(n_big)},
        scratch_shapes=[pltpu.SemaphoreType.DMA((n_sem,)), pltpu.SemaphoreType.DMA((n_sem,)), pltpu.SemaphoreType.DMA((n_small,))],
        name="all_gather_weights",
    )(*bufs, *small_ws)
    return outs[:n_big], outs[n_big:]


class _Exchange:
    def __init__(self, operands, out_shapes, n_sems, copies, in_place=False):
        self.operands, self.out_shapes, self.n_sems, self.copies = list(operands), list(out_shapes), n_sems, copies
        self.aliases = {t: t for t in range(len(self.operands))} if in_place else {}

    @property
    def scratch(self):
        return [pltpu.SemaphoreType.DMA((self.n_sems,)), pltpu.SemaphoreType.DMA((self.n_sems,))]

    def split(self, refs):
        n_in, n_out = len(self.operands), len(self.out_shapes)
        return refs[:n_in], refs[n_in:n_in + n_out]

    def start(self, ins, outs, sems):
        for cp in self.copies(ins, outs, *sems):
            cp.start()

    def wait(self, ins, outs, sems):
        for cp in self.copies(ins, outs, *sems):
            cp.wait()


def _run_exchange(name, ex):
    n_in, n_out = len(ex.operands), len(ex.out_shapes)

    def body(*refs):
        ins, outs, sems = refs[:n_in], refs[n_in:n_in + n_out], refs[n_in + n_out:]
        ex.start(ins, outs, sems)
        ex.wait(ins, outs, sems)

    return _pcall(body, in_specs=[ANY] * n_in, out_specs=[ANY] * n_out, out_shape=ex.out_shapes, scratch_shapes=ex.scratch,
                  input_output_aliases=ex.aliases, name=name)(*ex.operands)


def _gather_chips_exchange(bufs, spans):
    def copies(ins, outs, send_sems, recv_sems):
        x, y, c, chips = _place()
        cps = []
        for t, (l0, n) in enumerate(spans):
            mine = outs[t].at[pl.ds(l0, n), 2 * x + y, c]
            cps += [pltpu.make_async_remote_copy(src_ref=mine, dst_ref=mine, send_sem=send_sems.at[3 * t + k],
                                                 recv_sem=recv_sems.at[3 * t + k], device_id=(cx, cy, c), device_id_type=MESH)
                    for k, (cx, cy) in enumerate(chips)]
        return cps

    return _Exchange(bufs, [_sds(b.shape, b.dtype) for b in bufs], 3 * len(bufs), copies, in_place=True)


def _gather_cores_exchange(bufs, spans):
    def copies(ins, outs, send_sems, recv_sems):
        x, y, c, chips = _place()
        cps = []
        for t, (l0, n) in enumerate(spans):
            for k, (cx, cy) in enumerate(chips):
                part = outs[t].at[pl.ds(l0, n), 2 * cx + cy, c]
                cps.append(pltpu.make_async_remote_copy(src_ref=part, dst_ref=part, send_sem=send_sems.at[3 * t + k],
                                                        recv_sem=recv_sems.at[3 * t + k], device_id=(x, y, 1 - c),
                                                        device_id_type=MESH))
        return cps

    return _Exchange(bufs, [_sds(b.shape, b.dtype) for b in bufs], 3 * len(bufs), copies, in_place=True)


def _core_halves_exchange(grads, spans):
    def copies(ins, outs, send_sems, recv_sems):
        x, y, c, _ = _place()
        return [pltpu.make_async_remote_copy(src_ref=ins[t].at[pl.ds(l0, n), :, 1 - c], dst_ref=outs[t],
                                             send_sem=send_sems.at[t], recv_sem=recv_sems.at[t], device_id=(x, y, 1 - c),
                                             device_id_type=MESH) for t, (l0, n) in enumerate(spans)]

    shapes = [_sds((n, g.shape[1], g.shape[3], g.shape[4]), F32) for g, (_, n) in zip(grads, spans)]
    return _Exchange(grads, shapes, len(grads), copies)


def _add_core_halves(name, g, a, c_idx, l0):
    _, nj, _, rh, X = g.shape
    L = a.shape[0]
    tr = _tile(rh, 256, BF16_ROWS)

    def body(c_ref, g_ref, a_ref, o_ref, ob_ref):
        s = g_ref[...] + a_ref[...]
        o_ref[...] = s
        ob_ref[...] = s.astype(BF16)

    blk = pl.BlockSpec((None, None, tr, X), lambda l, j, i, c_ref: (l, j, i, 0))
    return _pcall(
        body,
        grid_spec=pltpu.PrefetchScalarGridSpec(
            num_scalar_prefetch=1, grid=(L, nj, rh // tr),
            in_specs=[pl.BlockSpec((None, None, None, tr, X), lambda l, j, i, c_ref: (l + l0, j, c_ref[0], i, 0)), blk],
            out_specs=[blk, blk]),
        out_shape=[_sds((L, nj, rh, X), F32), _sds((L, nj, rh, X), BF16)],
        compiler_params=_cp("parallel", "parallel", "parallel"), name=name,
    )(c_idx, g, a)


def _chip_shards_exchange(parts):
    def copies(ins, outs, send_sems, recv_sems):
        x, y, c, chips = _place()
        return [pltpu.make_async_remote_copy(
            src_ref=ins[t].at[:, 2 * cx + cy], dst_ref=outs[t].at[k], send_sem=send_sems.at[3 * t + k],
            recv_sem=recv_sems.at[3 * t + k], device_id=(cx, cy, c), device_id_type=MESH)
            for t in range(len(parts)) for k, (cx, cy) in enumerate(chips)]

    shapes = [_sds((3, p.shape[0], p.shape[2], p.shape[3]), p.dtype) for p in parts]
    return _Exchange(parts, shapes, 3 * len(parts), copies)


def _add_chip_shards(name, p, b, jc_idx, l0, n_layers, buf):
    n, _, rh, X = p.shape
    tr = _tile(rh, 256, BF16_ROWS)

    def body(jc_ref, p_ref, b_ref, *rest):
        rest[-1][...] = ((p_ref[...] + b_ref[0].astype(F32)) + b_ref[1].astype(F32)) + b_ref[2].astype(F32)

    in_specs = [pl.BlockSpec((None, None, tr, X), lambda l, i, jc: (l, jc[0], i, 0)),
                pl.BlockSpec((3, None, tr, X), lambda l, i, jc: (0, l, i, 0))]
    operands = [jc_idx, p, b]
    if buf is not None:
        in_specs.append(ANY)
        operands.append(buf)
    return _pcall(
        body,
        grid_spec=pltpu.PrefetchScalarGridSpec(
            num_scalar_prefetch=1, grid=(n, rh // tr), in_specs=in_specs,
            out_specs=pl.BlockSpec((None, None, tr, X), lambda l, i, jc: (l + l0, jc[1], i, 0))),
        out_shape=_sds((n_layers, 2, rh, X), F32), input_output_aliases={3: 0} if buf is not None else {},
        compiler_params=_cp("parallel", "parallel"), name=name,
    )(*operands)


def _join_core_halves(bufs):
    n = len(bufs)

    def body(*refs):
        outs = refs[n:2 * n]
        send_sems, recv_sems = refs[2 * n:]
        x, y, c, _ = _place()
        cps = [pltpu.make_async_remote_copy(src_ref=outs[t].at[:, c], dst_ref=outs[t].at[:, c], send_sem=send_sems.at[t],
                                            recv_sem=recv_sems.at[t], device_id=(x, y, 1 - c), device_id_type=MESH)
               for t in range(n)]
        for cp in cps:
            cp.start()
        for t in range(n):
            pltpu.make_async_remote_copy(src_ref=outs[t].at[:, c], dst_ref=outs[t].at[:, 1 - c], send_sem=send_sems.at[t],
                                         recv_sem=recv_sems.at[t], device_id=(x, y, 1 - c), device_id_type=MESH).wait()

    outs = _pcall(
        body, in_specs=[ANY] * n, out_specs=[ANY] * n, out_shape=[_sds(b.shape, F32) for b in bufs],
        input_output_aliases={t: t for t in range(n)},
        scratch_shapes=[pltpu.SemaphoreType.DMA((n,)), pltpu.SemaphoreType.DMA((n,))],
        name="grad_join_core_halves",
    )(*bufs)
    return [o.reshape(o.shape[0], 2 * o.shape[2], o.shape[3]) for o in outs]


def _all_reduce_small(packed):
    R, C = packed.shape

    def body(x_ref, o_ref, slots, send_sems, recv_sems):
        x, y, c, _ = _place()
        me = 4 * x + 2 * y + c
        slots[me] = x_ref[...]
        cps = []
        for d in range(N_DEV):
            to = (d // 4, (d // 2) % 2, d % 2)
            cp = pltpu.make_async_remote_copy(src_ref=x_ref, dst_ref=slots.at[me], send_sem=send_sems.at[d],
                                              recv_sem=recv_sems.at[me], device_id=to, device_id_type=MESH)
            cps.append(cp)

            @pl.when(d != me)
            def _():
                cp.start()

        for d in range(N_DEV):
            @pl.when(d != me)
            def _():
                pltpu.make_async_remote_copy(src_ref=x_ref, dst_ref=slots.at[d], send_sem=send_sems.at[d],
                                             recv_sem=recv_sems.at[d], device_id=(x, y, c), device_id_type=MESH).wait_recv()
                cps[d].wait_send()

        acc = slots[0]
        for d in range(1, N_DEV):
            acc = acc + slots[d]
        o_ref[...] = acc

    vm = pl.BlockSpec(memory_space=pltpu.VMEM)
    return _pcall(
        body, in_specs=[vm], out_specs=vm, out_shape=_sds((R, C), F32),
        scratch_shapes=[pltpu.VMEM((N_DEV, R, C), F32), pltpu.SemaphoreType.DMA((N_DEV,)), pltpu.SemaphoreType.DMA((N_DEV,))],
        compiler_params=pltpu.CompilerParams(vmem_limit_bytes=VMEM_LIMIT_BYTES), name="all_reduce_small",
    )(packed)


PACK = SUBLANES * LANES


def _pack(arrays):
    flat = []
    for a in arrays:
        v = a.reshape(-1)
        flat.append(jnp.pad(v, (0, (-v.shape[0]) % PACK)))
    return jnp.concatenate(flat).reshape(-1, LANES)


def _unpack(packed, shapes):
    flat = packed.reshape(-1)
    out, pos = [], 0
    for s in shapes:
        n = 1
        for d in s:
            n *= d
        out.append(flat[pos:pos + n].reshape(s))
        pos += n + (-n) % PACK
    return out


def kernel(x, mix_norm_g, ffn_norm_g, conv_w_in, conv_a_dw_w, conv_a_dw_b, conv_a_ln_g, conv_a_ln_b, conv_b_dw_w, conv_w_out, attn_w_qkv, attn_q_g, attn_k_g, attn_w_o, ffn_w_up, ffn_dw_w, ffn_dw_b, ffn_w_down, loss_target, m_mix_norm_g, m_ffn_norm_g, m_conv_w_in, m_conv_a_dw_w, m_conv_a_dw_b, m_conv_a_ln_g, m_conv_a_ln_b, m_conv_b_dw_w, m_conv_w_out, m_attn_w_qkv, m_attn_q_g, m_attn_k_g, m_attn_w_o, m_ffn_w_up, m_ffn_dw_w, m_ffn_dw_b, m_ffn_w_down, v_mix_norm_g, v_ffn_norm_g, v_conv_w_in, v_conv_a_dw_w, v_conv_a_dw_b, v_conv_a_ln_g, v_conv_a_ln_b, v_conv_b_dw_w, v_conv_w_out, v_attn_w_qkv, v_attn_q_g, v_attn_k_g, v_attn_w_o, v_ffn_w_up, v_ffn_dw_w, v_ffn_dw_b, v_ffn_w_down):
    depth = mix_norm_g.shape[0]
    n_even, n_odd = conv_w_in.shape[0], attn_w_qkv.shape[0]
    S, D = x.shape[1], x.shape[2]
    dg = D // 2
    x0 = x.reshape(S, D)
    target = loss_target.reshape(S, D)
    j_me = 2 * lax.axis_index("x") + lax.axis_index("y")
    c_me = lax.axis_index("c")
    j_idx = j_me.astype(jnp.int32).reshape(1)
    c_idx = c_me.astype(jnp.int32).reshape(1)

    col_names = ["conv_w_in", "attn_w_qkv", "ffn_w_up"]
    row_names = ["conv_w_out", "attn_w_o", "ffn_w_down"]
    local = dict(conv_w_in=conv_w_in, attn_w_qkv=attn_w_qkv, ffn_w_up=ffn_w_up, conv_w_out=conv_w_out, attn_w_o=attn_w_o,
                 ffn_w_down=ffn_w_down)
    gbuf = {n: _place_shard(f"place_{n}", local[n], j_idx) for n in col_names + row_names}

    def weights_of(layer):
        mixer = ("conv_w_in", "conv_w_out") if layer % 2 == 0 else ("attn_w_qkv", "attn_w_o")
        return {mixer[0]: (layer // 2, 1), mixer[1]: (layer // 2, 1), "ffn_w_up": (layer, 1), "ffn_w_down": (layer, 1)}

    def w_col(n):
        return gbuf[n].reshape(gbuf[n].shape[0], N_CHIPS, -1, gbuf[n].shape[4])

    def w_row(n):
        return gbuf[n].reshape(gbuf[n].shape[0], -1, gbuf[n].shape[4])

    def carried(kernel_out, make_exchange, group):
        if not group:
            return kernel_out(None)
        out, *new = kernel_out(make_exchange([gbuf[n] for n in group], list(group.values())))
        gbuf.update(zip(group, new))
        return out

    first = {"conv_w_in": (0, 1), "conv_w_out": (0, 1)}
    ffn_first = {"ffn_w_up": (0, 1), "ffn_w_down": (0, 1)}
    outs, (a_dw, b_dw, f_dw) = _all_gather_weights([gbuf[n] for n in first], list(first.values()),
                                                   [conv_a_dw_w, conv_b_dw_w, ffn_dw_w])
    gbuf.update(zip(first, outs))
    unshard = lambda a: jnp.moveaxis(a, 1, 2).reshape(a.shape[0], a.shape[2], N_CHIPS * a.shape[3])
    a_dw, b_dw, f_dw = unshard(a_dw), unshard(b_dw), unshard(f_dw)
    qk_gain = [jnp.stack([jnp.tile(attn_q_g[i], LANES // HEAD_DIM), jnp.tile(attn_k_g[i], LANES // HEAD_DIM)])
               for i in range(n_odd)]

    saved = []
    xc = x0
    for layer in range(depth):
        i = layer // 2
        tag = f"l{layer}"
        s = {"x_in": xc}
        here = weights_of(layer) if layer else None
        h = carried(lambda ex: _rms_fwd(f"rms_mix_fwd_{tag}", xc, mix_norm_g, layer, ex), _gather_cores_exchange, here)
        s["h"] = h
        if layer % 2 == 0:
            p = _mm_fwd(f"conv_in_fwd_{tag}", h, w_col("conv_w_in"), i, colshard=True)
            ab = carried(lambda ex: _convmix_fwd(f"convmix_fwd_{tag}", p, a_dw, conv_a_dw_b, conv_a_ln_g, conv_a_ln_b, b_dw,
                                                 i, ex), _gather_chips_exchange, None if layer else ffn_first)
            xm = _mm_fwd(f"conv_out_fwd_{tag}", ab, w_row("conv_w_out"), i, colshard=False, res=xc)
            s.update(p=p, ab=ab)
        else:
            qkv = _mm_fwd(f"attn_qkv_fwd_{tag}", h, w_col("attn_w_qkv"), i, colshard=True)
            qs, kn, vb = _qknorm_fwd(f"qknorm_fwd_{tag}", qkv, qk_gain[i])
            o = _attn_fwd(f"attn_fwd_{tag}", qs, kn, vb)
            xm = _mm_fwd(f"attn_out_fwd_{tag}", o, w_row("attn_w_o"), i, colshard=False, res=xc)
            s.update(qkv=qkv, qs=qs, kn=kn, vb=vb, o=o)
        s["x_mid"] = xm
        h2 = carried(lambda ex: _rms_fwd(f"rms_ffn_fwd_{tag}", xm, ffn_norm_g, layer, ex), _gather_cores_exchange,
                     None if layer else ffn_first)
        u2 = _mm_fwd(f"ffn_up_fwd_{tag}", h2, w_col("ffn_w_up"), layer, colshard=True, out_split=2)
        f = carried(lambda ex: _ffn_mid_fwd(f"ffn_mid_fwd_{tag}", u2, f_dw, ffn_dw_b, layer, ex), _gather_chips_exchange,
                    weights_of(layer + 1) if layer + 1 < depth else None)
        xc = _mm_fwd(f"ffn_down_fwd_{tag}", f, w_row("ffn_w_down"), layer, colshard=False, res=xm)
        s.update(h2=h2, u2=u2, f=f)
        saved.append(s)

    dx, loss_tile = _loss_fwd_bwd("loss", xc, target)

    w_in, w_qkv, w_up = w_col("conv_w_in"), w_col("attn_w_qkv"), w_col("ffn_w_up")
    w_out, w_o, w_down = w_row("conv_w_out"), w_row("attn_w_o"), w_row("ffn_w_down")
    g_up = g_down = g_in = g_out = g_qkv = g_o = None
    big_names = col_names + row_names

    def halves_view(n, g):
        if n in col_names:
            return g.reshape(g.shape[0], N_CHIPS, 2, g.shape[2] // 2, g.shape[3])
        return g.reshape(g.shape[0], N_CHIPS, 2, g.shape[1] // (2 * N_CHIPS), g.shape[2])

    ffn_of_0 = {"ffn_w_up": (0, 1), "ffn_w_down": (0, 1)}
    mixer_of_0 = {"conv_w_in": (0, 1), "conv_w_out": (0, 1)}
    summed_parts = {n: [] for n in big_names}

    def stacks():
        return {"conv_w_in": g_in, "attn_w_qkv": g_qkv, "ffn_w_up": g_up, "conv_w_out": g_out, "attn_w_o": g_o,
                "ffn_w_down": g_down}

    def core_exchange(group):
        return _core_halves_exchange([halves_view(n, stacks()[n]) for n in group], list(group.values()))

    def chip_exchange(tag, arrived):
        sums, parts = [], []
        for group, from_sibling in arrived:
            for n, a in zip(group, from_sibling):
                f32_sum, bf16_sum = _add_core_halves(f"grad_add_core_{n}_{tag}_{group[n][0]}", halves_view(n, stacks()[n]), a,
                                                     c_idx, group[n][0])
                sums.append((n, group[n][0], f32_sum))
                parts.append(bf16_sum)
        return _chip_shards_exchange(parts), sums

    def record(sums, from_chips):
        for (n, l0, f32_sum), b in zip(sums, from_chips):
            summed_parts[n].append((l0, f32_sum, b))

    d_mix_g, d_ffn_g = [None] * depth, [None] * depth
    d_ffn_dw_w, d_ffn_dw_b = [None] * depth, [None] * depth
    d_a_dw_w, d_a_dw_b, d_a_ln_g, d_a_ln_b, d_b_dw_w = ([None] * n_even for _ in range(5))
    d_q_g, d_k_g = [None] * n_odd, [None] * n_odd
    for layer in reversed(range(depth)):
        i = layer // 2
        tag = f"l{layer}"
        s = saved[layer]
        df = _mm_dgrad(f"ffn_down_dgrad_{tag}", dx, w_down, layer, colshard=False)
        g_down = _mm_wgrad(f"ffn_down_wgrad_{tag}", s["f"], dx, layer, depth, g_down, colshard=False)
        above = weights_of(layer + 1) if layer + 1 < depth else None
        arrived = []
        du2, dww, dwb, *from_sibling = _ffn_mid_bwd(f"ffn_mid_bwd_{tag}", s["u2"], df, f_dw, ffn_dw_b, layer,
                                                    core_exchange(above) if above else None)
        if above:
            arrived.append((above, from_sibling))
        d_ffn_dw_w[layer] = jnp.moveaxis(dww, 0, 1).reshape(FFN_CONV_WIDTH, -1)
        d_ffn_dw_b[layer] = dwb.reshape(-1)
        dh2 = _mm_dgrad(f"ffn_up_dgrad_{tag}", du2, w_up, layer, colshard=True)
        g_up = _mm_wgrad(f"ffn_up_wgrad_{tag}", s["h2"], du2, layer, depth, g_up, colshard=True)
        dx, dg_, *from_sibling = _rms_bwd(f"rms_ffn_bwd_{tag}", s["x_mid"], ffn_norm_g, layer, dh2, dx,
                                          core_exchange(ffn_of_0) if layer == 0 else None)
        if layer == 0:
            arrived.append((ffn_of_0, from_sibling))
        d_ffn_g[layer] = dg_.reshape(-1)
        if layer % 2 == 0:
            dab = _mm_dgrad(f"conv_out_dgrad_{tag}", dx, w_out, i, colshard=False)
            g_out = _mm_wgrad(f"conv_out_wgrad_{tag}", s["ab"], dx, i, n_even, g_out, colshard=False)
            chip_ex, sums = chip_exchange(tag, arrived) if arrived else (None, [])
            dp, daw, dab_b, dlg, dlb, dbw, *from_chips = _convmix_bwd(
                f"convmix_bwd_{tag}", s["p"], dab, a_dw, conv_a_dw_b, conv_a_ln_g, conv_a_ln_b, b_dw, i, chip_ex)
            record(sums, from_chips)
            d_a_dw_w[i], d_a_dw_b[i], d_a_ln_g[i], d_a_ln_b[i], d_b_dw_w[i] = (
                daw, dab_b.reshape(-1), dlg.reshape(-1), dlb.reshape(-1), dbw)
            dh = _mm_dgrad(f"conv_in_dgrad_{tag}", dp, w_in, i, colshard=True)
            g_in = _mm_wgrad(f"conv_in_wgrad_{tag}", s["h"], dp, i, n_even, g_in, colshard=True)
        else:
            do = _mm_dgrad(f"attn_out_dgrad_{tag}", dx, w_o, i, colshard=False)
            g_o = _mm_wgrad(f"attn_out_wgrad_{tag}", s["o"], dx, i, n_odd, g_o, colshard=False)
            chip_ex, sums = chip_exchange(tag, arrived) if arrived else (None, [])
            dq, dk, dv, *from_chips = _attn_bwd(f"attn_bwd_{tag}", s["qs"], s["kn"], s["vb"], s["o"], do, chip_ex)
            record(sums, from_chips)
            dqkv, dgain = _qknorm_bwd(f"qknorm_bwd_{tag}", s["qkv"], dq, dk, dv, qk_gain[i])
            d_q_g[i] = dgain[0, :HEAD_DIM] + dgain[0, HEAD_DIM:]
            d_k_g[i] = dgain[1, :HEAD_DIM] + dgain[1, HEAD_DIM:]
            dh = _mm_dgrad(f"attn_qkv_dgrad_{tag}", dqkv, w_qkv, i, colshard=True)
            g_qkv = _mm_wgrad(f"attn_qkv_wgrad_{tag}", s["h"], dqkv, i, n_odd, g_qkv, colshard=True)
        dx, dg_ = _rms_bwd(f"rms_mix_bwd_{tag}", s["x_in"], mix_norm_g, layer, dh, dx)
        d_mix_g[layer] = dg_.reshape(-1)
    grad_x = dx.reshape(1, S, D)

    small = {
        "mix_norm_g": jnp.stack(d_mix_g), "ffn_norm_g": jnp.stack(d_ffn_g),
        "conv_a_dw_w": jnp.stack(d_a_dw_w), "conv_a_dw_b": jnp.stack(d_a_dw_b),
        "conv_a_ln_g": jnp.stack(d_a_ln_g), "conv_a_ln_b": jnp.stack(d_a_ln_b),
        "conv_b_dw_w": jnp.stack(d_b_dw_w), "attn_q_g": jnp.stack(d_q_g), "attn_k_g": jnp.stack(d_k_g),
        "ffn_dw_w": jnp.stack(d_ffn_dw_w), "ffn_dw_b": jnp.stack(d_ffn_dw_b),
    }
    small_names = list(small)
    summed = _all_reduce_small(_pack([loss_tile] + [small[n] for n in small_names]))
    parts = _unpack(summed, [loss_tile.shape] + [small[n].shape for n in small_names])
    loss = parts[0][0, 0]
    small_g = dict(zip(small_names, parts[1:]))
    for n in ("conv_a_dw_w", "conv_b_dw_w", "ffn_dw_w"):
        cs = small_g[n].shape[2] // N_CHIPS
        small_g[n] = lax.dynamic_slice_in_dim(small_g[n], j_me * cs, cs, axis=2)

    from_sibling = _run_exchange("grad_exchange_core_halves", core_exchange(mixer_of_0))
    chip_ex, sums = chip_exchange("last", [(mixer_of_0, from_sibling)])
    record(sums, _run_exchange("grad_exchange_chip_shards", chip_ex))
    jc_idx = jnp.concatenate([j_idx, c_idx])
    totals = {}
    for n in big_names:
        total = None
        for l0, p, b in summed_parts[n]:
            total = _add_chip_shards(f"grad_add_chips_{n}_{l0}", p, b, jc_idx, l0, stacks()[n].shape[0], total)
        totals[n] = total
    big_g = dict(zip(big_names, _join_core_halves([totals[n] for n in big_names])))

    weights = dict(mix_norm_g=mix_norm_g, ffn_norm_g=ffn_norm_g, conv_w_in=conv_w_in, conv_a_dw_w=conv_a_dw_w, conv_a_dw_b=conv_a_dw_b, conv_a_ln_g=conv_a_ln_g, conv_a_ln_b=conv_a_ln_b, conv_b_dw_w=conv_b_dw_w, conv_w_out=conv_w_out, attn_w_qkv=attn_w_qkv, attn_q_g=attn_q_g, attn_k_g=attn_k_g, attn_w_o=attn_w_o, ffn_w_up=ffn_w_up, ffn_dw_w=ffn_dw_w, ffn_dw_b=ffn_dw_b, ffn_w_down=ffn_w_down)
    m_in = dict(mix_norm_g=m_mix_norm_g, ffn_norm_g=m_ffn_norm_g, conv_w_in=m_conv_w_in, conv_a_dw_w=m_conv_a_dw_w, conv_a_dw_b=m_conv_a_dw_b, conv_a_ln_g=m_conv_a_ln_g, conv_a_ln_b=m_conv_a_ln_b, conv_b_dw_w=m_conv_b_dw_w, conv_w_out=m_conv_w_out, attn_w_qkv=m_attn_w_qkv, attn_q_g=m_attn_q_g, attn_k_g=m_attn_k_g, attn_w_o=m_attn_w_o, ffn_w_up=m_ffn_w_up, ffn_dw_w=m_ffn_dw_w, ffn_dw_b=m_ffn_dw_b, ffn_w_down=m_ffn_w_down)
    v_in = dict(mix_norm_g=v_mix_norm_g, ffn_norm_g=v_ffn_norm_g, conv_w_in=v_conv_w_in, conv_a_dw_w=v_conv_a_dw_w, conv_a_dw_b=v_conv_a_dw_b, conv_a_ln_g=v_conv_a_ln_g, conv_a_ln_b=v_conv_a_ln_b, conv_b_dw_w=v_conv_b_dw_w, conv_w_out=v_conv_w_out, attn_w_qkv=v_attn_w_qkv, attn_q_g=v_attn_q_g, attn_k_g=v_attn_k_g, attn_w_o=v_attn_w_o, ffn_w_up=v_ffn_w_up, ffn_dw_w=v_ffn_dw_w, ffn_dw_b=v_ffn_dw_b, ffn_w_down=v_ffn_w_down)
    order = list(weights)
    grads, delta, new_m, new_v = {}, {}, {}, {}
    for n in big_names:
        grads[n] = big_g[n]
        delta[n], new_m[n], new_v[n] = _adamw(f"adamw_{n}", weights[n], big_g[n], m_in[n], v_in[n])
    shapes = [weights[n].shape for n in small_names]
    packed = [_pack([d[n] for n in small_names]) for d in (weights, small_g, m_in, v_in)]
    upd = _adamw("adamw_small", *[p[None] for p in packed])
    for out, res in zip((delta, new_m, new_v), upd):
        out.update(zip(small_names, _unpack(res[0], shapes)))
    grads.update({n: small_g[n].reshape(weights[n].shape) for n in small_names})
    return (loss, grad_x, *[grads[n] for n in order], *[delta[n] for n in order], *[new_m[n] for n in order],
            *[new_v[n] for n in order])
```

```python
import jax
import jax.numpy as jnp
from jax import lax
from jax.experimental import pallas as pl
from jax.experimental.pallas import tpu as pltpu

F32 = jnp.float32
BF16 = jnp.bfloat16
EPS = 1e-6
CONV_A_WIDTH = 31
CONV_B_WIDTH = 3
FFN_CONV_WIDTH = 3
HEAD_DIM = 64
ADAM_LR = 0.001
ADAM_B1 = 0.9
ADAM_B2 = 0.999
ADAM_EPS = 1e-08
ADAM_WD = 0.01
ADAM_STEP = 10

LANES = 128
SUBLANES = 8
BF16_ROWS = 16
V7X_VMEM_BYTES = 64 * 1024 * 1024
VMEM_LIMIT_BYTES = V7X_VMEM_BYTES * 3 // 4
MM_VMEM_BUDGET = VMEM_LIMIT_BYTES * 4 // 5
MM_ROWS = 1024
N_CHIPS = 4
N_DEV = 8
HALO_A = 32
HALO_S = 8
ELT_ROWS = 64
FFN_MID_ROWS = 512
NORM_ROWS = 1024
ATTN_BLOCK = 128
ATTN_SUB = 2
ATTN_MORE = 2
ATTN_TILES = 2
EXP_UNDERFLOW = -104.0
MESH = pl.DeviceIdType.MESH
ANY = pl.BlockSpec(memory_space=pl.ANY)
NT = (((1,), (1,)), ((), ()))
NN = (((1,), (0,)), ((), ()))
TN = (((0,), (0,)), ((), ()))


def _pcall(body, **kw):
    return pl.pallas_call(body, **kw)


def _cp(*sem):
    return pltpu.CompilerParams(dimension_semantics=sem, vmem_limit_bytes=VMEM_LIMIT_BYTES)


def _sds(shape, dtype):
    return jax.ShapeDtypeStruct(tuple(shape), dtype)


def _tile(n, cap, align=LANES):
    if n <= cap:
        return n
    for t in range(cap - cap % align, 0, -align):
        if n % t == 0:
            return t
    return n


def _sig(x):
    return 0.5 * jnp.tanh(0.5 * x) + 0.5


def _rowsum(x):
    return jnp.sum(x, axis=0, keepdims=True)


def _fold(x):
    acc = x[0:SUBLANES]
    for r in range(SUBLANES, x.shape[0], SUBLANES):
        acc = acc + x[r:r + SUBLANES]
    return acc


def _with_exchange(ex, body, in_specs, out_specs, out_shape, scratch, operands, first, last):
    if ex is None:
        return body, in_specs, out_specs, out_shape, scratch, operands, {}
    n_in, n_out, n_scr = len(in_specs), len(out_specs), len(scratch)
    e_in, e_out = len(ex.operands), len(ex.out_shapes)

    def hosted(*refs):
        refs = list(refs)
        ins, refs = refs[:n_in], refs[n_in:]
        e_ins, refs = refs[:e_in], refs[e_in:]
        outs, refs = refs[:n_out], refs[n_out:]
        e_outs, refs = refs[:e_out], refs[e_out:]
        scr, sems = refs[:n_scr], refs[n_scr:]

        @pl.when(first())
        def _():
            ex.start(e_ins, e_outs, sems)

        body(*ins, *outs, *scr)

        @pl.when(last())
        def _():
            ex.wait(e_ins, e_outs, sems)

    return (hosted, in_specs + [ANY] * e_in, out_specs + [ANY] * e_out, out_shape + ex.out_shapes, scratch + ex.scratch,
            operands + ex.operands, {n_in + i: n_out + o for i, o in ex.aliases.items()})


def _mm_call(name, dn, operands, in_specs, out_shape, out_spec, grid, nk, acc_shape, has_res, has_alias):
    def body(*refs):
        a_ref, b_ref = refs[0], refs[1]
        pos = 2
        res_ref = refs[pos] if has_res else None
        pos += int(has_res) + int(has_alias)
        o_ref = refs[pos]
        acc_ref = refs[pos + 1] if nk > 1 else None
        p = lax.dot_general(a_ref[...].astype(BF16), b_ref[...].astype(BF16), dn, preferred_element_type=F32)

        def finish(v):
            if has_res:
                v = v + res_ref[...]
            o_ref[...] = v.astype(o_ref.dtype)

        if nk == 1:
            finish(p)
        else:
            k = pl.program_id(2)

            @pl.when(k == 0)
            def _():
                acc_ref[...] = p

            @pl.when(k > 0)
            def _():
                acc_ref[...] += p

            @pl.when(k == nk - 1)
            def _():
                finish(acc_ref[...])

    aliases = {len(operands) - 1: 0} if has_alias else {}
    return _pcall(
        body, grid=grid, in_specs=in_specs, out_specs=out_spec, out_shape=out_shape,
        scratch_shapes=[pltpu.VMEM(acc_shape, F32)] if nk > 1 else [],
        input_output_aliases=aliases, compiler_params=_cp("parallel", "parallel", "arbitrary"), name=name,
    )(*operands)


def _mm_fwd(name, a, w, l, *, colshard, res=None, out_split=1):
    M, K = a.shape
    tm = _tile(M, MM_ROWS, BF16_ROWS)
    if colshard and out_split == 1 and res is None:
        cs = w.shape[3]
        th = _tile(M, MM_ROWS // 2, BF16_ROWS)
        if 2 * (N_CHIPS * K * cs * 2 + th * N_CHIPS * cs * 4 + th * K * a.dtype.itemsize) <= MM_VMEM_BUDGET:
            def body(a_ref, b_ref, o_ref):
                av = a_ref[...].astype(BF16)
                for j in range(N_CHIPS):
                    o_ref[:, j * cs:(j + 1) * cs] = jnp.dot(av, b_ref[j], preferred_element_type=F32)

            return _pcall(
                body, grid=(M // th,),
                in_specs=[pl.BlockSpec((th, K), lambda i: (i, 0)), pl.BlockSpec((None, N_CHIPS, K, cs), lambda i: (l, 0, 0, 0))],
                out_specs=pl.BlockSpec((th, N_CHIPS * cs), lambda i: (i, 0)), out_shape=_sds((M, N_CHIPS * cs), F32),
                compiler_params=_cp("parallel"), name=name,
            )(a, w)
    if colshard:
        cs = w.shape[3]
        N, tn, tk = N_CHIPS * cs, cs, K
        b_spec = pl.BlockSpec((None, None, tk, tn), lambda j, i, k: (l, j, k, 0))
    else:
        N = w.shape[2]
        tn, tk = _tile(N, 1024), K
        if K > 1536:
            tm = _tile(M, MM_ROWS // 2, BF16_ROWS)
        b_spec = pl.BlockSpec((None, tk, tn), lambda j, i, k: (l, k, j))
    nk = K // tk
    in_specs = [pl.BlockSpec((tm, tk), lambda j, i, k: (i, k)), b_spec]
    operands = [a, w]
    if res is not None:
        in_specs.append(pl.BlockSpec((tm, tn), lambda j, i, k: (i, j)))
        operands.append(res)
    if out_split == 1:
        out_shape = _sds((M, N), F32)
        out_spec = pl.BlockSpec((tm, tn), lambda j, i, k: (i, j))
    else:
        per = N // tn // out_split
        out_shape = _sds((out_split, M, N // out_split), F32)
        out_spec = pl.BlockSpec((None, tm, tn), lambda j, i, k: (j // per, i, j % per))
    return _mm_call(name, NN, operands, in_specs, out_shape, out_spec, (N // tn, M // tm, nk), nk, (tm, tn),
                    res is not None, False)


def _mm_dgrad(name, g, w, l, *, colshard):
    split = g.ndim == 3
    M = g.shape[-2]
    tm = _tile(M, MM_ROWS, BF16_ROWS)
    if colshard:
        kw, cs = w.shape[2], w.shape[3]
        tm = _tile(M, MM_ROWS // 2, BF16_ROWS)
        per = N_CHIPS // g.shape[0] if split else N_CHIPS

        def body(a_ref, b_ref, o_ref):
            acc = None
            for j in range(N_CHIPS):
                cols = slice((j % per) * cs, (j % per + 1) * cs)
                a = a_ref[j // per, :, cols] if split else a_ref[:, cols]
                p = lax.dot_general(a.astype(BF16), b_ref[j], NT, preferred_element_type=F32)
                acc = p if acc is None else acc + p
            o_ref[...] = acc

        a_spec = (pl.BlockSpec((g.shape[0], tm, g.shape[2]), lambda i: (0, i, 0)) if split
                  else pl.BlockSpec((tm, N_CHIPS * cs), lambda i: (i, 0)))
        return _pcall(
            body, grid=(M // tm,),
            in_specs=[a_spec, pl.BlockSpec((None, N_CHIPS, kw, cs), lambda i: (l, 0, 0, 0))],
            out_specs=pl.BlockSpec((tm, kw), lambda i: (i, 0)), out_shape=_sds((M, kw), F32),
            compiler_params=_cp("parallel"), name=name,
        )(g, w)
    else:
        kw, ncon = w.shape[1], w.shape[2]
        tn, tk = _tile(kw, 1408), _tile(ncon, 1536)
        nk = ncon // tk
        th = _tile(M, MM_ROWS // 2, BF16_ROWS)
        if nk == 1 and 2 * (kw * ncon * w.dtype.itemsize + th * kw * 4 + th * ncon * g.dtype.itemsize) <= MM_VMEM_BUDGET:
            tm, tn = th, kw
        b_spec = pl.BlockSpec((None, tn, tk), lambda j, i, k: (l, j, k))
    if split:
        per = nk // g.shape[0]
        a_spec = pl.BlockSpec((None, tm, tk), lambda j, i, k: (k // per, i, k % per))
    else:
        a_spec = pl.BlockSpec((tm, tk), lambda j, i, k: (i, k))
    out_shape = _sds((M, kw), F32)
    out_spec = pl.BlockSpec((tm, tn), lambda j, i, k: (i, j))
    return _mm_call(name, NT, [g, w], [a_spec, b_spec], out_shape, out_spec, (kw // tn, M // tm, nk), nk, (tm, tn),
                    False, False)


def _mm_wgrad(name, a, g, l, n_layers, buf, *, colshard):
    S, M = a.shape
    split = g.ndim == 3
    N = g.shape[-1] * (g.shape[0] if split else 1)
    tm = _tile(M, 1408)
    tn = N // N_CHIPS if colshard else _tile(N, 1024)
    per_row = 2 * (tm * a.dtype.itemsize + tn * g.dtype.itemsize)
    tk = _tile(S, max(BF16_ROWS, min(2048, (MM_VMEM_BUDGET - 3 * tm * tn * 4) // per_row)), BF16_ROWS)
    nk = S // tk
    if colshard:
        out_shape = _sds((n_layers, N_CHIPS, M, tn), F32)
        out_spec = pl.BlockSpec((None, None, tm, tn), lambda j, i, k: (l, j, i, 0))
    else:
        out_shape = _sds((n_layers, M, N), F32)
        out_spec = pl.BlockSpec((None, tm, tn), lambda j, i, k: (l, i, j))
    if split:
        per = N // tn // g.shape[0]
        b_spec = pl.BlockSpec((None, tk, tn), lambda j, i, k: (j // per, k, j % per))
    else:
        b_spec = pl.BlockSpec((tk, tn), lambda j, i, k: (k, j))
    in_specs = [pl.BlockSpec((tk, tm), lambda j, i, k: (k, i)), b_spec]
    operands = [a, g]
    if buf is not None:
        in_specs.append(ANY)
        operands.append(buf)
    return _mm_call(name, TN, operands, in_specs, out_shape, out_spec, (N // tn, M // tm, nk), nk, (tm, tn),
                    False, buf is not None)


def _rms_fwd(name, x, g, l, exchange=None):
    S, D = x.shape
    tm = _tile(S, NORM_ROWS, BF16_ROWS)
    n_i = S // tm

    def body(x_ref, g_ref, o_ref):
        xf = x_ref[...]
        r = lax.rsqrt(jnp.mean(xf * xf, axis=-1, keepdims=True) + EPS)
        o_ref[...] = (xf * r * g_ref[l:l + 1, :]).astype(BF16)

    body, in_specs, out_specs, out_shape, scratch, operands, aliases = _with_exchange(
        exchange, body, [pl.BlockSpec((tm, D), lambda i: (i, 0)), pl.BlockSpec(g.shape, lambda i: (0, 0))],
        [pl.BlockSpec((tm, D), lambda i: (i, 0))], [_sds((S, D), BF16)], [], [x, g],
        lambda: pl.program_id(0) == 0, lambda: pl.program_id(0) == n_i - 1)
    outs = _pcall(
        body, grid=(n_i,), in_specs=in_specs, out_specs=out_specs, out_shape=out_shape, scratch_shapes=scratch,
        input_output_aliases=aliases, compiler_params=_cp("arbitrary" if exchange else "parallel"), name=name,
    )(*operands)
    return outs if exchange else outs[0]


def _rms_bwd(name, x, g, l, dh, dres, exchange=None):
    S, D = x.shape
    tm = _tile(S, NORM_ROWS, SUBLANES)

    def body(x_ref, g_ref, dh_ref, dr_ref, dx_ref, dg_ref):
        xf = x_ref[...]
        r = lax.rsqrt(jnp.mean(xf * xf, axis=-1, keepdims=True) + EPS)
        xh = xf * r
        d = dh_ref[...]
        dxh = d * g_ref[l:l + 1, :]
        dx_ref[...] = dr_ref[...] + r * (dxh - xh * jnp.mean(dxh * xh, axis=-1, keepdims=True))

        @pl.when(pl.program_id(0) == 0)
        def _():
            dg_ref[...] = jnp.zeros_like(dg_ref)

        dg_ref[...] += _rowsum(d * xh)

    row = pl.BlockSpec((tm, D), lambda i: (i, 0))
    n_i = S // tm
    body, in_specs, out_specs, out_shape, scratch, operands, aliases = _with_exchange(
        exchange, body, [row, pl.BlockSpec(g.shape, lambda i: (0, 0)), row, row],
        [row, pl.BlockSpec((1, D), lambda i: (0, 0))], [_sds((S, D), F32), _sds((1, D), F32)], [], [x, g, dh, dres],
        lambda: pl.program_id(0) == 0, lambda: pl.program_id(0) == n_i - 1)
    return _pcall(
        body, grid=(n_i,), in_specs=in_specs, out_specs=out_specs, out_shape=out_shape, scratch_shapes=scratch,
        input_output_aliases=aliases, compiler_params=_cp("arbitrary"), name=name,
    )(*operands)


def _loss_fwd_bwd(name, y, t):
    S, D = y.shape
    tm = _tile(S, NORM_ROWS, SUBLANES)

    def body(y_ref, t_ref, dy_ref, l_ref):
        e = y_ref[...] - t_ref[...]
        dy_ref[...] = e * (1.0 / D)

        @pl.when(pl.program_id(0) == 0)
        def _():
            l_ref[...] = jnp.zeros_like(l_ref)

        l_ref[...] += 0.5 * jnp.sum(jnp.sum(e * e, axis=-1, keepdims=True) * (1.0 / D), axis=0, keepdims=True)

    row = pl.BlockSpec((tm, D), lambda i: (i, 0))
    return _pcall(
        body, grid=(S // tm,), in_specs=[row, row],
        out_specs=[row, pl.BlockSpec((SUBLANES, LANES), lambda i: (0, 0))],
        out_shape=[_sds((S, D), F32), _sds((SUBLANES, LANES), F32)],
        compiler_params=_cp("arbitrary"), name=name,
    )(y, t)


def _delayed_copies(us, n_rows):
    for s in range(1, SUBLANES):
        us[s, pl.ds(SUBLANES, n_rows - SUBLANES), :] = us[0, pl.ds(SUBLANES - s, n_rows - SUBLANES), :]


def _conv_a(aw_ref, ab_ref, l, us, row0, rows, dg):
    ka = CONV_A_WIDTH
    out = []
    for c0 in range(0, dg, LANES):
        lanes = slice(c0, c0 + LANES)
        acc = ab_ref[l:l + 1, lanes]
        for d in range(ka):
            a, s = divmod(d, SUBLANES)
            acc = acc + aw_ref[l, ka - 1 - d:ka - d, lanes] * us[s, pl.ds(row0 - SUBLANES * a, rows), lanes]
        out.append(acc)
    return jnp.concatenate(out, axis=1)


def _convmix_fwd(name, p, aw, ab, lg, lb, bw, l, exchange=None):
    S, W = p.shape
    dg = W // 5
    tm = _tile(S, 256, HALO_A)
    nb = tm // HALO_A
    ka, kb = CONV_A_WIDTH, CONV_B_WIDTH

    ext = HALO_A + tm
    rc = _tile(tm, ELT_ROWS, BF16_ROWS)

    def body(p_ref, ph_ref, aw_ref, ab_ref, lg_ref, lb_ref, bw_ref, o_ref, us, mext):
        first = pl.program_id(0) == 0
        ph = ph_ref[...]
        pc = p_ref[...]
        us[0, pl.ds(0, HALO_A), :] = jnp.where(first, 0.0, ph[:, 0:dg] * _sig(ph[:, dg:2 * dg]))
        us[0, pl.ds(HALO_A, tm), :] = pc[:, 0:dg] * _sig(pc[:, dg:2 * dg])
        mext[pl.ds(0, HALO_A), :] = jnp.where(first, 0.0, ph[:, 3 * dg:4 * dg] * ph[:, 4 * dg:5 * dg])
        mext[pl.ds(HALO_A, tm), :] = pc[:, 3 * dg:4 * dg] * pc[:, 4 * dg:5 * dg]
        _delayed_copies(us, ext)
        for r0 in range(0, tm, rc):
            rows = pl.ds(r0, rc)
            c = _conv_a(aw_ref, ab_ref, l, us, HALO_A + r0, rc, dg)
            xc = c - jnp.mean(c, axis=-1, keepdims=True)
            ln = xc * lax.rsqrt(jnp.mean(xc * xc, axis=-1, keepdims=True) + EPS) * lg_ref[l:l + 1, :] + lb_ref[l:l + 1, :]
            o_ref[rows, 0:dg] = (ln * _sig(ln)).astype(BF16)
            cb = bw_ref[l, 0:1, :] * mext[pl.ds(HALO_A - (kb - 1) + r0, rc), :]
            for k in range(1, kb):
                cb = cb + bw_ref[l, k:k + 1, :] * mext[pl.ds(HALO_A - (kb - 1) + k + r0, rc), :]
            o_ref[rows, dg:2 * dg] = (p_ref[rows, 2 * dg:3 * dg] * cb).astype(BF16)

    full = lambda a: pl.BlockSpec(a.shape, lambda i: (0,) * a.ndim)
    n_i = S // tm
    body, in_specs, out_specs, out_shape, scratch, operands, aliases = _with_exchange(
        exchange, body,
        [pl.BlockSpec((tm, W), lambda i: (i, 0)), pl.BlockSpec((HALO_A, W), lambda i: (jnp.maximum(i * nb - 1, 0), 0)),
         full(aw), full(ab), full(lg), full(lb), full(bw)],
        [pl.BlockSpec((tm, 2 * dg), lambda i: (i, 0))], [_sds((S, 2 * dg), BF16)],
        [pltpu.VMEM((SUBLANES, ext, dg), F32), pltpu.VMEM((ext, dg), F32)], [p, p, aw, ab, lg, lb, bw],
        lambda: pl.program_id(0) == 0, lambda: pl.program_id(0) == n_i - 1)
    outs = _pcall(
        body, grid=(n_i,), in_specs=in_specs, out_specs=out_specs, out_shape=out_shape, scratch_shapes=scratch,
        input_output_aliases=aliases, compiler_params=_cp("arbitrary" if exchange else "parallel"), name=name,
    )(*operands)
    return outs if exchange else outs[0]


def _convmix_bwd(name, p, dab, aw, ab, lg, lb, bw, l, exchange=None):
    S, W = p.shape
    dg = W // 5
    tm = _tile(S, 256, HALO_A)
    nb = tm // HALO_A
    n_i = S // tm
    ka, kb = CONV_A_WIDTH, CONV_B_WIDTH
    n = tm + HALO_A
    ext = HALO_A + n
    rc = _tile(tm, ELT_ROWS, BF16_ROWS)

    def body(p_ref, pp_ref, pn_ref, d_ref, dn_ref, aw_ref, ab_ref, lg_ref, lb_ref, bw_ref,
             dp_ref, daw_ref, dab_ref, dlg_ref, dlb_ref, dbw_ref, us, mext, dcs, dbext, accw):
        i = pl.program_id(0)
        first, last = i == 0, i == n_i - 1

        @pl.when(first)
        def _():
            for r in (daw_ref, dab_ref, dlg_ref, dlb_ref, dbw_ref):
                r[...] = jnp.zeros_like(r)

        accw[...] = jnp.zeros_like(accw)
        pp, pc, pn = pp_ref[...], p_ref[...], pn_ref[...]
        glu = lambda b: b[:, 0:dg] * _sig(b[:, dg:2 * dg])
        gch = lambda b: b[:, 3 * dg:4 * dg] * b[:, 4 * dg:5 * dg]
        us[0, pl.ds(0, HALO_A), :] = jnp.where(first, 0.0, glu(pp))
        us[0, pl.ds(HALO_A, tm), :] = glu(pc)
        us[0, pl.ds(HALO_A + tm, HALO_A), :] = glu(pn)
        mext[pl.ds(0, HALO_A), :] = jnp.where(first, 0.0, gch(pp))
        mext[pl.ds(HALO_A, tm), :] = gch(pc)
        mext[pl.ds(HALO_A + tm, HALO_A), :] = gch(pn)
        _delayed_copies(us, ext)
        chunks = [(r, rc) for r in range(0, tm, rc)] + [(tm, HALO_A)]
        g_ln = lg_ref[l:l + 1, :]
        zero8 = jnp.zeros((SUBLANES, dg), F32)

        acc_lg = acc_lb = acc_ab = zero8
        for r0, rows in chunks:
            c = _conv_a(aw_ref, ab_ref, l, us, HALO_A + r0, rows, dg)
            xc = c - jnp.mean(c, axis=-1, keepdims=True)
            rstd = lax.rsqrt(jnp.mean(xc * xc, axis=-1, keepdims=True) + EPS)
            chat = xc * rstd
            ln = chat * g_ln + lb_ref[l:l + 1, :]
            s = _sig(ln)
            da = d_ref[pl.ds(r0, rows), 0:dg] if r0 < tm else jnp.where(last, 0.0, dn_ref[:, 0:dg])
            dln = da * (s * (1.0 + ln * (1.0 - s)))
            dlnh = dln * g_ln
            dc = rstd * (dlnh - jnp.mean(dlnh, axis=-1, keepdims=True)
                         - chat * jnp.mean(dlnh * chat, axis=-1, keepdims=True))
            dcs[0, pl.ds(r0, rows), :] = dc
            if r0 < tm:
                acc_lg = acc_lg + _fold(dln * chat)
                acc_lb = acc_lb + _fold(dln)
                acc_ab = acc_ab + _fold(dc)
                for c0 in range(0, dg, LANES):
                    lanes = slice(c0, c0 + LANES)
                    for d in range(ka):
                        a, sh = divmod(d, SUBLANES)
                        k = ka - 1 - d
                        accw[pl.ds(SUBLANES * k, SUBLANES), lanes] += _fold(
                            dc[:, lanes] * us[sh, pl.ds(HALO_A + r0 - SUBLANES * a, rows), lanes])
        dlg_ref[...] += _rowsum(acc_lg)
        dlb_ref[...] += _rowsum(acc_lb)
        dab_ref[...] += _rowsum(acc_ab)
        for k in range(ka):
            daw_ref[k:k + 1, :] += _rowsum(accw[pl.ds(SUBLANES * k, SUBLANES), :])
        for s in range(1, SUBLANES):
            dcs[s, pl.ds(0, n - SUBLANES), :] = dcs[0, pl.ds(s, n - SUBLANES), :]
        for r0 in range(0, tm, rc):
            rows = pl.ds(r0, rc)
            parts = []
            for c0 in range(0, dg, LANES):
                lanes = slice(c0, c0 + LANES)
                acc = aw_ref[l, ka - 1:ka, lanes] * dcs[0, rows, lanes]
                for e in range(1, ka):
                    a, sh = divmod(e, SUBLANES)
                    acc = acc + aw_ref[l, ka - 1 - e:ka - e, lanes] * dcs[sh, pl.ds(r0 + SUBLANES * a, rc), lanes]
                parts.append(acc)
            du = jnp.concatenate(parts, axis=1)
            sg = _sig(p_ref[rows, dg:2 * dg])
            dp_ref[rows, 0:dg] = (du * sg).astype(BF16)
            dp_ref[rows, dg:2 * dg] = (du * p_ref[rows, 0:dg] * sg * (1.0 - sg)).astype(BF16)

        for r0, rows in chunks:
            if r0 < tm:
                dbext[pl.ds(r0, rows), :] = d_ref[pl.ds(r0, rows), dg:2 * dg] * p_ref[pl.ds(r0, rows), 2 * dg:3 * dg]
            else:
                dbext[pl.ds(r0, rows), :] = jnp.where(last, 0.0, dn_ref[:, dg:2 * dg] * pn[:, 2 * dg:3 * dg])
        acc_bw = [zero8] * kb
        for r0 in range(0, tm, rc):
            rows = pl.ds(r0, rc)
            m_k = [mext[pl.ds(HALO_A - (kb - 1) + k + r0, rc), :] for k in range(kb)]
            cb = bw_ref[l, 0:1, :] * m_k[0]
            dm = bw_ref[l, 0:1, :] * dbext[pl.ds(r0 + kb - 1, rc), :]
            for k in range(1, kb):
                cb = cb + bw_ref[l, k:k + 1, :] * m_k[k]
                dm = dm + bw_ref[l, k:k + 1, :] * dbext[pl.ds(r0 + kb - 1 - k, rc), :]
            dcb = dbext[rows, :]
            acc_bw = [acc_bw[k] + _fold(dcb * m_k[k]) for k in range(kb)]
            dp_ref[rows, 2 * dg:3 * dg] = (d_ref[rows, dg:2 * dg] * cb).astype(BF16)
            dp_ref[rows, 3 * dg:4 * dg] = (dm * p_ref[rows, 4 * dg:5 * dg]).astype(BF16)
            dp_ref[rows, 4 * dg:5 * dg] = (dm * p_ref[rows, 3 * dg:4 * dg]).astype(BF16)
        for k in range(kb):
            dbw_ref[k:k + 1, :] += _rowsum(acc_bw[k])

    full = lambda a: pl.BlockSpec(a.shape, lambda i: (0,) * a.ndim)
    prev = lambda i: (jnp.maximum(i * nb - 1, 0), 0)
    nxt = lambda i: (jnp.minimum((i + 1) * nb, S // HALO_A - 1), 0)
    acc = lambda r: pl.BlockSpec((r, dg), lambda i: (0, 0))
    body, in_specs, out_specs, out_shape, scratch, operands, aliases = _with_exchange(
        exchange, body,
        [pl.BlockSpec((tm, W), lambda i: (i, 0)), pl.BlockSpec((HALO_A, W), prev), pl.BlockSpec((HALO_A, W), nxt),
         pl.BlockSpec((tm, 2 * dg), lambda i: (i, 0)), pl.BlockSpec((HALO_A, 2 * dg), nxt),
         full(aw), full(ab), full(lg), full(lb), full(bw)],
        [pl.BlockSpec((tm, W), lambda i: (i, 0)), acc(ka), acc(1), acc(1), acc(1), acc(kb)],
        [_sds((S, W), BF16), _sds((ka, dg), F32), _sds((1, dg), F32), _sds((1, dg), F32), _sds((1, dg), F32),
         _sds((kb, dg), F32)],
        [pltpu.VMEM((SUBLANES, ext, dg), F32), pltpu.VMEM((ext, dg), F32), pltpu.VMEM((SUBLANES, n, dg), F32),
         pltpu.VMEM((n, dg), F32), pltpu.VMEM((SUBLANES * ka, dg), F32)],
        [p, p, p, dab, dab, aw, ab, lg, lb, bw],
        lambda: pl.program_id(0) == 0, lambda: pl.program_id(0) == n_i - 1)
    return _pcall(
        body, grid=(n_i,), in_specs=in_specs, out_specs=out_specs, out_shape=out_shape, scratch_shapes=scratch,
        input_output_aliases=aliases, compiler_params=_cp("arbitrary"), name=name,
    )(*operands)


def _ffn_mid_fwd(name, u2, dww, dwb, l, exchange=None):
    _, S, F = u2.shape
    tm = _tile(S, FFN_MID_ROWS, BF16_ROWS)
    tc = _tile(F, 1408)
    n_f = F // tc
    nb = tm // HALO_S
    kf = FFN_CONV_WIDTH

    def body(u_ref, uh_ref, wg_ref, wv_ref, bg_ref, bv_ref, o_ref, ext):
        first = pl.program_id(1) == 0
        ext[:, pl.ds(0, HALO_S), :] = jnp.where(first, 0.0, uh_ref[...])
        ext[:, pl.ds(HALO_S, tm), :] = u_ref[...]
        rc = _tile(tm, ELT_ROWS, BF16_ROWS)

        def lane_chunk(ci, carry):
            lanes = pl.ds(pl.multiple_of(ci * LANES, LANES), LANES)
            taps = [[w_ref[k:k + 1, lanes] for k in range(kf)] for w_ref in (wg_ref, wv_ref)]
            bias = [b_ref[l:l + 1, lanes] for b_ref in (bg_ref, bv_ref)]
            for r0 in range(0, tm, rc):
                c = []
                for g in range(2):
                    acc = bias[g]
                    for k in range(kf):
                        acc = acc + taps[g][k] * ext[g, pl.ds(HALO_S - (kf - 1) + k + r0, rc), lanes]
                    c.append(acc)
                o_ref[pl.ds(r0, rc), lanes] = (c[0] * _sig(c[0]) * c[1]).astype(BF16)
            return carry

        lax.fori_loop(0, tc // LANES, lane_chunk, 0)

    n_l = dwb.shape[0]
    n_i = S // tm
    body, in_specs, out_specs, out_shape, scratch, operands, aliases = _with_exchange(
        exchange, body,
        [pl.BlockSpec((2, tm, tc), lambda j, i: (0, i, j)),
         pl.BlockSpec((2, HALO_S, tc), lambda j, i: (0, jnp.maximum(i * nb - 1, 0), j)),
         pl.BlockSpec((None, kf, tc), lambda j, i: (l, 0, j)),
         pl.BlockSpec((None, kf, tc), lambda j, i: (l, 0, j + n_f)),
         pl.BlockSpec((n_l, tc), lambda j, i: (0, j)),
         pl.BlockSpec((n_l, tc), lambda j, i: (0, j + n_f))],
        [pl.BlockSpec((tm, tc), lambda j, i: (i, j))], [_sds((S, F), BF16)],
        [pltpu.VMEM((2, HALO_S + tm, tc), F32)], [u2, u2, dww, dww, dwb, dwb],
        lambda: jnp.logical_and(pl.program_id(0) == 0, pl.program_id(1) == 0),
        lambda: jnp.logical_and(pl.program_id(0) == n_f - 1, pl.program_id(1) == n_i - 1))
    sem = "arbitrary" if exchange else "parallel"
    outs = _pcall(
        body, grid=(n_f, n_i), in_specs=in_specs, out_specs=out_specs, out_shape=out_shape, scratch_shapes=scratch,
        input_output_aliases=aliases, compiler_params=_cp(sem, sem), name=name,
    )(*operands)
    return outs if exchange else outs[0]


def _ffn_mid_bwd(name, u2, df, dww, dwb, l, exchange=None):
    _, S, F = u2.shape
    tm = _tile(S, FFN_MID_ROWS, BF16_ROWS)
    tc = _tile(F, 1408)
    n_f = F // tc
    nb = tm // HALO_S
    n_i = S // tm
    kf = FFN_CONV_WIDTH
    n = tm + HALO_S

    def body(u_ref, up_ref, un_ref, df_ref, dfn_ref, wg_ref, wv_ref, bg_ref, bv_ref,
             du_ref, dw_ref, db_ref, uext, dcext):
        i = pl.program_id(1)
        first, last = i == 0, i == n_i - 1

        @pl.when(first)
        def _():
            dw_ref[...] = jnp.zeros_like(dw_ref)
            db_ref[...] = jnp.zeros_like(db_ref)

        uext[:, pl.ds(0, HALO_S), :] = jnp.where(first, 0.0, up_ref[...])
        uext[:, pl.ds(HALO_S, tm), :] = u_ref[...]
        uext[:, pl.ds(HALO_S + tm, HALO_S), :] = un_ref[...]
        rc = _tile(tm, ELT_ROWS, BF16_ROWS)

        def lane_chunk(ci, carry):
            lanes = pl.ds(pl.multiple_of(ci * LANES, LANES), LANES)
            taps = [[w_ref[k:k + 1, lanes] for k in range(kf)] for w_ref in (wg_ref, wv_ref)]
            bias = [b_ref[l:l + 1, lanes] for b_ref in (bg_ref, bv_ref)]
            acc_w = [[jnp.zeros((SUBLANES, LANES), F32) for _ in range(kf)] for _ in range(2)]
            acc_b = [jnp.zeros((SUBLANES, LANES), F32) for _ in range(2)]
            for r0, rows in [(r, rc) for r in range(0, tm, rc)] + [(tm, HALO_S)]:
                shifted = [[uext[g, pl.ds(HALO_S - (kf - 1) + k + r0, rows), lanes] for k in range(kf)] for g in range(2)]
                conv = []
                for g in range(2):
                    acc = bias[g]
                    for k in range(kf):
                        acc = acc + taps[g][k] * shifted[g][k]
                    conv.append(acc)
                cg, cv = conv
                s = _sig(cg)
                dfe = df_ref[pl.ds(r0, rows), lanes] if r0 < tm else jnp.where(last, 0.0, dfn_ref[:, lanes])
                dc = [dfe * cv * (s * (1.0 + cg * (1.0 - s))), dfe * (cg * s)]
                for g in range(2):
                    dcext[g, pl.ds(r0, rows), lanes] = dc[g]
                    if r0 < tm:
                        acc_b[g] = acc_b[g] + _fold(dc[g])
                        for k in range(kf):
                            acc_w[g][k] = acc_w[g][k] + _fold(dc[g] * shifted[g][k])
            for r0 in range(0, tm, rc):
                for g in range(2):
                    du = taps[g][0] * dcext[g, pl.ds(r0 + kf - 1, rc), lanes]
                    for k in range(1, kf):
                        du = du + taps[g][k] * dcext[g, pl.ds(r0 + kf - 1 - k, rc), lanes]
                    du_ref[g, pl.ds(r0, rc), lanes] = du.astype(BF16)
            for g in range(2):
                db_ref[g, :, lanes] += _rowsum(acc_b[g])
                for k in range(kf):
                    dw_ref[g, k:k + 1, lanes] += _rowsum(acc_w[g][k])
            return carry

        lax.fori_loop(0, tc // LANES, lane_chunk, 0)

    n_l = dwb.shape[0]
    prev = lambda j, i: (0, jnp.maximum(i * nb - 1, 0), j)
    nxt = lambda j, i: (0, jnp.minimum((i + 1) * nb, S // HALO_S - 1), j)
    body, in_specs, out_specs, out_shape, scratch, operands, aliases = _with_exchange(
        exchange, body,
        [pl.BlockSpec((2, tm, tc), lambda j, i: (0, i, j)),
         pl.BlockSpec((2, HALO_S, tc), prev), pl.BlockSpec((2, HALO_S, tc), nxt),
         pl.BlockSpec((tm, tc), lambda j, i: (i, j)),
         pl.BlockSpec((HALO_S, tc), lambda j, i: nxt(j, i)[1:]),
         pl.BlockSpec((None, kf, tc), lambda j, i: (l, 0, j)),
         pl.BlockSpec((None, kf, tc), lambda j, i: (l, 0, j + n_f)),
         pl.BlockSpec((n_l, tc), lambda j, i: (0, j)),
         pl.BlockSpec((n_l, tc), lambda j, i: (0, j + n_f))],
        [pl.BlockSpec((2, tm, tc), lambda j, i: (0, i, j)),
         pl.BlockSpec((2, kf, tc), lambda j, i: (0, 0, j)),
         pl.BlockSpec((2, 1, tc), lambda j, i: (0, 0, j))],
        [_sds((2, S, F), BF16), _sds((2, kf, F), F32), _sds((2, 1, F), F32)],
        [pltpu.VMEM((2, HALO_S + n, tc), F32), pltpu.VMEM((2, n, tc), F32)],
        [u2, u2, u2, df, df, dww, dww, dwb, dwb],
        lambda: jnp.logical_and(pl.program_id(0) == 0, pl.program_id(1) == 0),
        lambda: jnp.logical_and(pl.program_id(0) == n_f - 1, pl.program_id(1) == n_i - 1))
    return _pcall(
        body, grid=(n_f, n_i), in_specs=in_specs, out_specs=out_specs, out_shape=out_shape, scratch_shapes=scratch,
        input_output_aliases=aliases, compiler_params=_cp("arbitrary" if exchange else "parallel", "arbitrary"), name=name,
    )(*operands)


def _head_sum_matrix():
    r = lax.broadcasted_iota(jnp.int32, (LANES, LANES), 0) // HEAD_DIM
    c = lax.broadcasted_iota(jnp.int32, (LANES, LANES), 1) // HEAD_DIM
    return (r == c).astype(BF16)


def _head_mean(x, ones):
    return _split_dot(x, ones) * (1.0 / HEAD_DIM)


def _qknorm_fwd(name, qkv, g2):
    S, D3 = qkv.shape
    D = D3 // 3
    tm = _tile(S, NORM_ROWS // 2, BF16_ROWS)
    scale = HEAD_DIM ** -0.5

    def body(q_ref, k_ref, v_ref, g_ref, qo_ref, ko_ref, vo_ref):
        ones = _head_sum_matrix()
        for cc in range(D // LANES):
            sl = slice(cc * LANES, (cc + 1) * LANES)
            for x_ref, o_ref, row, mult in ((q_ref, qo_ref, 0, scale), (k_ref, ko_ref, 1, 1.0)):
                x = x_ref[:, sl]
                r = lax.rsqrt(_head_mean(x * x, ones) + EPS)
                o_ref[:, sl] = ((x * r * g_ref[row:row + 1, :]).astype(BF16) * mult).astype(BF16)
        vo_ref[...] = v_ref[...].astype(BF16)

    col = lambda c: pl.BlockSpec((tm, D), lambda i: (i, c))
    out = pl.BlockSpec((tm, D), lambda i: (i, 0))
    return _pcall(
        body, grid=(S // tm,),
        in_specs=[col(0), col(1), col(2), pl.BlockSpec(g2.shape, lambda i: (0, 0))],
        out_specs=[out, out, out], out_shape=[_sds((S, D), BF16)] * 3,
        compiler_params=_cp("parallel"), name=name,
    )(qkv, qkv, qkv, g2)


def _qknorm_bwd(name, qkv, dq, dk, dv, g2):
    S, D3 = qkv.shape
    D = D3 // 3
    tm = _tile(S, NORM_ROWS // 2, BF16_ROWS)
    scale = HEAD_DIM ** -0.5

    def body(q_ref, k_ref, dq_ref, dk_ref, dv_ref, g_ref, o_ref, dg_ref):
        @pl.when(pl.program_id(0) == 0)
        def _():
            dg_ref[...] = jnp.zeros_like(dg_ref)

        ones = _head_sum_matrix()
        for cc in range(D // LANES):
            sl = slice(cc * LANES, (cc + 1) * LANES)
            for x_ref, d_ref, row, mult, base in ((q_ref, dq_ref, 0, scale, 0), (k_ref, dk_ref, 1, 1.0, D)):
                x = x_ref[:, sl]
                r = lax.rsqrt(_head_mean(x * x, ones) + EPS)
                xh = x * r
                dn = d_ref[:, sl] * mult
                dxh = dn * g_ref[row:row + 1, :]
                dx = r * (dxh - xh * _head_mean(dxh * xh, ones))
                o_ref[:, base + cc * LANES:base + (cc + 1) * LANES] = dx.astype(BF16)
                dg_ref[row:row + 1, :] += _rowsum(dn * xh)
        o_ref[:, 2 * D:3 * D] = dv_ref[...].astype(BF16)

    col = lambda c: pl.BlockSpec((tm, D), lambda i: (i, c))
    row = pl.BlockSpec((tm, D), lambda i: (i, 0))
    return _pcall(
        body, grid=(S // tm,),
        in_specs=[col(0), col(1), row, row, row, pl.BlockSpec(g2.shape, lambda i: (0, 0))],
        out_specs=[pl.BlockSpec((tm, D3), lambda i: (i, 0)), pl.BlockSpec((2, LANES), lambda i: (0, 0))],
        out_shape=[_sds((S, D3), BF16), _sds((2, LANES), F32)],
        compiler_params=_cp("arbitrary"), name=name,
    )(qkv, qkv, dq, dk, dv, g2)


def _attn_consts():
    t = ATTN_BLOCK
    row = lax.broadcasted_iota(jnp.int32, (t, t), 0)
    col = lax.broadcasted_iota(jnp.int32, (t, t), 1)
    lane = lax.broadcasted_iota(jnp.int32, (1, LANES), 1)
    heads = (lane < HEAD_DIM, lane >= HEAD_DIM)
    return row, col, heads


def _split_dot(x, m):
    n = x.shape[0]
    hi = x.astype(BF16)
    lo = (x - hi.astype(F32)).astype(BF16)
    both = jnp.dot(jnp.concatenate([hi, lo], axis=0), m, preferred_element_type=F32)
    return both[:n] + both[n:]


def _log_keep(z):
    return -(jnp.maximum(z, 0.0) + jnp.log(1.0 + jnp.exp(-jnp.abs(z))))


def _stack_heads(a, heads):
    t = ATTN_BLOCK
    zero = jnp.zeros((t, LANES), a.dtype)
    return jnp.concatenate([jnp.where(h, a[s * t:(s + 1) * t], zero) for s in range(a.shape[0] // t) for h in heads], axis=0)


def _side_by_side(a):
    t = ATTN_BLOCK
    return jnp.concatenate([jnp.concatenate([a[2 * s * t:(2 * s + 1) * t], a[(2 * s + 1) * t:(2 * s + 2) * t]], axis=1)
                            for s in range(a.shape[0] // (2 * t))], axis=0)


def _grow(a, rows, cols):
    z = jnp.zeros((rows, cols), F32)
    return z if a is None else jnp.concatenate([z, a], axis=0)


def _attn_fwd(name, qs, kn, vb, exchange=None):
    S, D = qs.shape
    t = ATTN_BLOCK
    tq = ATTN_SUB * t

    def body(q_ref, k_ref, v_ref, o_ref):
        for part in range(ATTN_TILES):
            rows = pl.ds(part * tq, tq)
            tile(ATTN_TILES * pl.program_id(1) + part, q_ref.at[rows, :], k_ref, v_ref, o_ref.at[rows, :])

    def tile(i, q_ref, k_ref, v_ref, o_ref):
        row, col, heads = _attn_consts()
        after_m = (row > col).astype(BF16)
        causal = col < row
        q_all = _stack_heads(q_ref[...], heads)

        def blocks(specs, r, acc):
            n_rows = q_all.shape[0]
            offs = [pl.multiple_of(j * t, t) for j, _, _ in specs]
            zs = [lax.dot_general(q_all[lo:], k_ref[pl.ds(off, t), :], NT, preferred_element_type=F32)
                  for off, (_, lo, _) in zip(offs, specs)]
            lks = []
            for z, (_, _, mask) in zip(zs, specs):
                lk = _log_keep(z)
                lks.append(lk if mask is None else jnp.where(mask, lk, 0.0))
            cums = [_split_dot(lk, after_m) for lk in lks]
            ws = []
            for z, lk, cum, (_, lo, mask) in zip(zs, lks, cums, specs):
                rows = n_rows - lo
                r = _grow(r, rows - (0 if r is None else r.shape[0]), 1) if r is None or r.shape[0] < rows else r
                w = jnp.exp(z + lk + cum + r)
                ws.append((w if mask is None else jnp.where(mask, w, 0.0)).astype(BF16))
                r = r + jnp.sum(lk, axis=1, keepdims=True)
            acc = jnp.zeros((n_rows // 2, LANES), F32) if acc is None else acc
            for w, off, (_, lo, _) in zip(ws, offs, specs):
                part = jnp.dot(_side_by_side(w), _stack_heads(v_ref[pl.ds(off, t), :], heads), preferred_element_type=F32)
                acc = acc + (part if lo == 0 else _grow(part, lo // 2, LANES))
            return r, acc

        def head(n_more):
            specs = [(ATTN_SUB * i + s, 2 * s * t,
                      jnp.concatenate([causal, causal] + [jnp.ones_like(causal)] * (2 * (ATTN_SUB - 1 - s)), axis=0))
                     for s in reversed(range(ATTN_SUB))]
            specs += [(ATTN_SUB * i - 1 - b, 0, None) for b in range(n_more)]
            return blocks(specs, None, None)

        r, acc = lax.cond(ATTN_SUB * i >= ATTN_MORE, lambda: head(ATTN_MORE), lambda: head(0))

        def cond(c):
            return jnp.logical_and(c[0] >= 0, jnp.max(c[1]) > EXP_UNDERFLOW)

        def step(c):
            r, a = blocks([(c[0], 0, None)], c[1], c[2])
            return c[0] - 1, r, a

        first = jnp.where(ATTN_SUB * i >= ATTN_MORE, ATTN_SUB * i - 1 - ATTN_MORE, ATTN_SUB * i - 1)
        o_ref[...] = lax.while_loop(cond, step, (first, r, acc))[2]

    n_hp = D // LANES
    blk = pl.BlockSpec((ATTN_TILES * tq, LANES), lambda hp, i: (i, hp))
    seq = pl.BlockSpec((S, LANES), lambda hp, i: (0, hp))
    n_i = S // (ATTN_TILES * tq)
    body, in_specs, out_specs, out_shape, scratch, operands, aliases = _with_exchange(
        exchange, body, [blk, seq, seq], [blk], [_sds((S, D), F32)], [], [qs, kn, vb],
        lambda: jnp.logical_and(pl.program_id(0) == 0, pl.program_id(1) == 0),
        lambda: jnp.logical_and(pl.program_id(0) == n_hp - 1, pl.program_id(1) == n_i - 1))
    outs = _pcall(
        body, grid=(n_hp, n_i), in_specs=in_specs, out_specs=out_specs, out_shape=out_shape, scratch_shapes=scratch,
        input_output_aliases=aliases, compiler_params=_cp("arbitrary" if exchange else "parallel", "arbitrary"), name=name,
    )(*operands)
    return outs if exchange else outs[0]


def _attn_bwd(name, qs, kn, vb, o, do, exchange=None):
    S, D = qs.shape
    t = ATTN_BLOCK
    tq = ATTN_SUB * t

    def body(q_ref, k_ref, v_ref, o_ref, do_ref, dq_ref, dk_ref, dv_ref):
        for part in range(ATTN_TILES):
            rows = pl.ds(part * tq, tq)
            tile(ATTN_TILES * pl.program_id(1) + part, q_ref.at[rows, :], k_ref, v_ref, o_ref.at[rows, :],
                 do_ref.at[rows, :], dq_ref.at[rows, :], dk_ref, dv_ref)

    def tile(i, q_ref, k_ref, v_ref, o_ref, do_ref, dq_ref, dk_ref, dv_ref):

        @pl.when(i == 0)
        def _():
            dk_ref[...] = jnp.zeros_like(dk_ref)
            dv_ref[...] = jnp.zeros_like(dv_ref)

        row, col, heads = _attn_consts()
        after_m = (row > col).astype(BF16)
        from_m = (row >= col).astype(BF16)
        causal = col < row
        q_all = _stack_heads(q_ref[...], heads)
        dob = do_ref[...].astype(BF16)
        do_all = _stack_heads(dob, heads)
        dsum_all = jnp.sum(_stack_heads(dob.astype(F32) * o_ref[...], heads), axis=1, keepdims=True)

        def blocks(specs, r, es, dq):
            n_rows = q_all.shape[0]
            offs = [pl.multiple_of(j * t, t) for j, _, _ in specs]
            masked = lambda x, mask: x if mask is None else jnp.where(mask, x, 0.0)
            top = lambda a, rows: a if a is not None and a.shape[0] == rows else _grow(a, rows - (0 if a is None else a.shape[0]), 1)
            zs = [lax.dot_general(q_all[lo:], k_ref[pl.ds(off, t), :], NT, preferred_element_type=F32)
                  for off, (_, lo, _) in zip(offs, specs)]
            gs = [lax.dot_general(do_all[lo:], v_ref[pl.ds(off, t), :], NT, preferred_element_type=F32)
                  for off, (_, lo, _) in zip(offs, specs)]
            lks = [masked(_log_keep(z), mask) for z, (_, _, mask) in zip(zs, specs)]
            cums = [_split_dot(lk, after_m) for lk in lks]
            ws, es_blk, sgs = [], [], []
            for z, g, lk, cum, (_, lo, mask) in zip(zs, gs, lks, cums, specs):
                r = top(r, n_rows - lo)
                ls = z + lk
                w = masked(jnp.exp(ls + cum + r), mask)
                ws.append(w.astype(BF16))
                es_blk.append(w * g)
                sgs.append(jnp.exp(ls))
                r = r + jnp.sum(lk, axis=1, keepdims=True)
            cum_es = [_split_dot(e, from_m) for e in es_blk]
            dzs = []
            for e, cum_e, sg, (_, lo, mask) in zip(es_blk, cum_es, sgs, specs):
                es = top(es, n_rows - lo)
                before = dsum_all[lo:] - (es + cum_e)
                dzs.append(masked(e - (e + before) * sg, mask).astype(BF16))
                es = es + jnp.sum(e, axis=1, keepdims=True)
            dq = jnp.zeros((n_rows // 2, LANES), F32) if dq is None else dq
            for dzb, w, off, (_, lo, _) in zip(dzs, ws, offs, specs):
                part = jnp.dot(_side_by_side(dzb), _stack_heads(k_ref[pl.ds(off, t), :], heads), preferred_element_type=F32)
                dq = dq + (part if lo == 0 else _grow(part, lo // 2, LANES))
                dk_ref[pl.ds(off, t), :] += lax.dot_general(dzb, q_all[lo:], TN, preferred_element_type=F32)
                dv_ref[pl.ds(off, t), :] += lax.dot_general(w, do_all[lo:], TN, preferred_element_type=F32)
            return r, es, dq

        def head(n_more):
            specs = [(ATTN_SUB * i + s, 2 * s * t,
                      jnp.concatenate([causal, causal] + [jnp.ones_like(causal)] * (2 * (ATTN_SUB - 1 - s)), axis=0))
                     for s in reversed(range(ATTN_SUB))]
            specs += [(ATTN_SUB * i - 1 - b, 0, None) for b in range(n_more)]
            return blocks(specs, None, None, None)

        r, es, dq = lax.cond(ATTN_SUB * i >= ATTN_MORE, lambda: head(ATTN_MORE), lambda: head(0))

        def cond(c):
            return jnp.logical_and(c[0] >= 0, jnp.max(c[1]) > EXP_UNDERFLOW)

        def step(c):
            r, es, a = blocks([(c[0], 0, None)], c[1], c[2], c[3])
            return c[0] - 1, r, es, a

        first = jnp.where(ATTN_SUB * i >= ATTN_MORE, ATTN_SUB * i - 1 - ATTN_MORE, ATTN_SUB * i - 1)
        dq_ref[...] = lax.while_loop(cond, step, (first, r, es, dq))[3]

    n_hp = D // LANES
    blk = pl.BlockSpec((ATTN_TILES * tq, LANES), lambda hp, i: (i, hp))
    seq = pl.BlockSpec((S, LANES), lambda hp, i: (0, hp))
    n_i = S // (ATTN_TILES * tq)
    body, in_specs, out_specs, out_shape, scratch, operands, aliases = _with_exchange(
        exchange, body, [blk, seq, seq, blk, blk], [blk, seq, seq], [_sds((S, D), F32)] * 3, [], [qs, kn, vb, o, do],
        lambda: jnp.logical_and(pl.program_id(0) == 0, pl.program_id(1) == 0),
        lambda: jnp.logical_and(pl.program_id(0) == n_hp - 1, pl.program_id(1) == n_i - 1))
    return _pcall(
        body, grid=(n_hp, n_i), in_specs=in_specs, out_specs=out_specs, out_shape=out_shape, scratch_shapes=scratch,
        input_output_aliases=aliases, compiler_params=_cp("arbitrary" if exchange else "parallel", "arbitrary"), name=name,
    )(*operands)


def _adamw(name, w, g, m, v):
    L, R, C = w.shape
    tr = _tile(R, 256, SUBLANES)
    c1 = 1.0 - ADAM_B1 ** ADAM_STEP
    c2 = 1.0 - ADAM_B2 ** ADAM_STEP

    def body(w_ref, g_ref, m_ref, v_ref, d_ref, mo_ref, vo_ref):
        gg = g_ref[...]
        mn = ADAM_B1 * m_ref[...] + (1.0 - ADAM_B1) * gg
        vn = ADAM_B2 * v_ref[...] + (1.0 - ADAM_B2) * (gg * gg)
        d_ref[...] = -ADAM_LR * ((mn / c1) / (jnp.sqrt(vn / c2) + ADAM_EPS) + ADAM_WD * w_ref[...])
        mo_ref[...] = mn
        vo_ref[...] = vn

    blk = pl.BlockSpec((None, tr, C), lambda l, i: (l, i, 0))
    return _pcall(
        body, grid=(L, R // tr), in_specs=[blk] * 4, out_specs=[blk] * 3, out_shape=[_sds(w.shape, F32)] * 3,
        compiler_params=_cp("parallel", "parallel"), name=name,
    )(w, g, m, v)


def _place():
    x, y, c = lax.axis_index("x"), lax.axis_index("y"), lax.axis_index("c")
    chips = [(1 - x, y), (x, 1 - y), (1 - x, 1 - y)]
    return x, y, c, chips


def _place_shard(name, w, j_idx):
    L, R, X = w.shape
    rh = R // 2
    tr = _tile(rh, 256, BF16_ROWS)

    def body(j_ref, w_ref, o_ref):
        o_ref[...] = w_ref[...].astype(BF16)

    return _pcall(
        body,
        grid_spec=pltpu.PrefetchScalarGridSpec(
            num_scalar_prefetch=1, grid=(L, 2, rh // tr),
            in_specs=[pl.BlockSpec((None, None, tr, X), lambda l, h, i, j_ref: (l, h, i, 0))],
            out_specs=pl.BlockSpec((None, None, None, tr, X), lambda l, h, i, j_ref: (l, j_ref[0], h, i, 0))),
        out_shape=_sds((L, N_CHIPS, 2, rh, X), BF16), compiler_params=_cp("parallel", "parallel", "parallel"), name=name,
    )(j_idx, w.reshape(L, 2, rh, X))


def _all_gather_weights(bufs, spans, small_ws):
    n_big, n_small = len(bufs), len(small_ws)
    n_in = n_big + n_small
    layers = [pl.ds(l0, n) for l0, n in spans]

    def body(*refs):
        ins, outs = refs[:n_in], refs[n_in:2 * n_in]
        send_sems, recv_sems, local_sems = refs[2 * n_in:]
        x, y, c, chips = _place()
        j_me = 2 * x + y
        j_of = [2 * cx + cy for cx, cy in chips]
        sibling = (x, y, 1 - c)

        def remote(src, dst, s, to):
            return pltpu.make_async_remote_copy(src_ref=src, dst_ref=dst, send_sem=send_sems.at[s], recv_sem=recv_sems.at[s],
                                                device_id=to, device_id_type=MESH)

        started = []
        for t in range(n_big, n_in):
            loc = pltpu.make_async_copy(ins[t], outs[t].at[:, j_me], local_sems.at[t - n_big])
            loc.start()
            started.append(loc)
        first = []
        for t in range(n_big):
            mine = outs[t].at[layers[t], j_me, c]
            for k in range(3):
                first.append(remote(mine, mine, 6 * t + k, (*chips[k], c)))
        for t in range(n_big, n_in):
            for k in range(3):
                first.append(remote(ins[t], outs[t].at[:, j_me], 6 * n_big + 3 * (t - n_big) + k, (*chips[k], c)))
        for cp in first:
            cp.start()
        passed = []
        for t in range(n_big):
            for k in range(3):
                landed = outs[t].at[layers[t], j_of[k], c]
                remote(landed, landed, 6 * t + k, (*chips[k], c)).wait_recv()
                fwd = remote(landed, landed, 6 * t + 3 + k, sibling)
                fwd.start()
                passed.append(fwd)
        for t in range(n_big):
            for k in range(3):
                other = outs[t].at[layers[t], j_of[k], 1 - c]
                remote(other, other, 6 * t + 3 + k, sibling).wait_recv()
        for t in range(n_big, n_in):
            for k in range(3):
                dst = outs[t].at[:, j_of[k]]
                remote(dst, dst, 6 * n_big + 3 * (t - n_big) + k, (*chips[k], c)).wait_recv()
        for cp in first + passed:
            cp.wait_send()
        for loc in started:
            loc.wait()

    out_shape = [_sds(b.shape, b.dtype) for b in bufs]
    out_shape += [_sds((w.shape[0], N_CHIPS) + w.shape[1:], w.dtype) for w in small_ws]
    n_sem = 6 * n_big + 3 * n_small
    outs = _pcall(
        body, in_specs=[ANY] * n_in, out_specs=[ANY] * n_in, out_shape=out_shape,
        input_output_aliases={t: t for t in range(n_big)},
        scratch_shapes=[pltpu.SemaphoreType.DMA((n_sem,)), pltpu.SemaphoreType.DMA((n_sem,)), pltpu.SemaphoreType.DMA((n_small,))],
        name="all_gather_weights",
    )(*bufs, *small_ws)
    return outs[:n_big], outs[n_big:]


class _Exchange:
    def __init__(self, operands, out_shapes, n_sems, copies, in_place=False):
        self.operands, self.out_shapes, self.n_sems, self.copies = list(operands), list(out_shapes), n_sems, copies
        self.aliases = {t: t for t in range(len(self.operands))} if in_place else {}

    @property
    def scratch(self):
        return [pltpu.SemaphoreType.DMA((self.n_sems,)), pltpu.SemaphoreType.DMA((self.n_sems,))]

    def split(self, refs):
        n_in, n_out = len(self.operands), len(self.out_shapes)
        return refs[:n_in], refs[n_in:n_in + n_out]

    def start(self, ins, outs, sems):
        for cp in self.copies(ins, outs, *sems):
            cp.start()

    def wait(self, ins, outs, sems):
        for cp in self.copies(ins, outs, *sems):
            cp.wait()


def _run_exchange(name, ex):
    n_in, n_out = len(ex.operands), len(ex.out_shapes)

    def body(*refs):
        ins, outs, sems = refs[:n_in], refs[n_in:n_in + n_out], refs[n_in + n_out:]
        ex.start(ins, outs, sems)
        ex.wait(ins, outs, sems)

    return _pcall(body, in_specs=[ANY] * n_in, out_specs=[ANY] * n_out, out_shape=ex.out_shapes, scratch_shapes=ex.scratch,
                  input_output_aliases=ex.aliases, name=name)(*ex.operands)


def _gather_chips_exchange(bufs, spans):
    def copies(ins, outs, send_sems, recv_sems):
        x, y, c, chips = _place()
        cps = []
        for t, (l0, n) in enumerate(spans):
            mine = outs[t].at[pl.ds(l0, n), 2 * x + y, c]
            cps += [pltpu.make_async_remote_copy(src_ref=mine, dst_ref=mine, send_sem=send_sems.at[3 * t + k],
                                                 recv_sem=recv_sems.at[3 * t + k], device_id=(cx, cy, c), device_id_type=MESH)
                    for k, (cx, cy) in enumerate(chips)]
        return cps

    return _Exchange(bufs, [_sds(b.shape, b.dtype) for b in bufs], 3 * len(bufs), copies, in_place=True)


def _gather_cores_exchange(bufs, spans):
    def copies(ins, outs, send_sems, recv_sems):
        x, y, c, chips = _place()
        cps = []
        for t, (l0, n) in enumerate(spans):
            for k, (cx, cy) in enumerate(chips):
                part = outs[t].at[pl.ds(l0, n), 2 * cx + cy, c]
                cps.append(pltpu.make_async_remote_copy(src_ref=part, dst_ref=part, send_sem=send_sems.at[3 * t + k],
                                                        recv_sem=recv_sems.at[3 * t + k], device_id=(x, y, 1 - c),
                                                        device_id_type=MESH))
        return cps

    return _Exchange(bufs, [_sds(b.shape, b.dtype) for b in bufs], 3 * len(bufs), copies, in_place=True)


def _core_halves_exchange(grads, spans):
    def copies(ins, outs, send_sems, recv_sems):
        x, y, c, _ = _place()
        return [pltpu.make_async_remote_copy(src_ref=ins[t].at[pl.ds(l0, n), :, 1 - c], dst_ref=outs[t],
                                             send_sem=send_sems.at[t], recv_sem=recv_sems.at[t], device_id=(x, y, 1 - c),
                                             device_id_type=MESH) for t, (l0, n) in enumerate(spans)]

    shapes = [_sds((n, g.shape[1], g.shape[3], g.shape[4]), F32) for g, (_, n) in zip(grads, spans)]
    return _Exchange(grads, shapes, len(grads), copies)


def _add_core_halves(name, g, a, c_idx, l0):
    _, nj, _, rh, X = g.shape
    L = a.shape[0]
    tr = _tile(rh, 256, BF16_ROWS)

    def body(c_ref, g_ref, a_ref, o_ref, ob_ref):
        s = g_ref[...] + a_ref[...]
        o_ref[...] = s
        ob_ref[...] = s.astype(BF16)

    blk = pl.BlockSpec((None, None, tr, X), lambda l, j, i, c_ref: (l, j, i, 0))
    return _pcall(
        body,
        grid_spec=pltpu.PrefetchScalarGridSpec(
            num_scalar_prefetch=1, grid=(L, nj, rh // tr),
            in_specs=[pl.BlockSpec((None, None, None, tr, X), lambda l, j, i, c_ref: (l + l0, j, c_ref[0], i, 0)), blk],
            out_specs=[blk, blk]),
        out_shape=[_sds((L, nj, rh, X), F32), _sds((L, nj, rh, X), BF16)],
        compiler_params=_cp("parallel", "parallel", "parallel"), name=name,
    )(c_idx, g, a)


def _chip_shards_exchange(parts):
    def copies(ins, outs, send_sems, recv_sems):
        x, y, c, chips = _place()
        return [pltpu.make_async_remote_copy(
            src_ref=ins[t].at[:, 2 * cx + cy], dst_ref=outs[t].at[k], send_sem=send_sems.at[3 * t + k],
            recv_sem=recv_sems.at[3 * t + k], device_id=(cx, cy, c), device_id_type=MESH)
            for t in range(len(parts)) for k, (cx, cy) in enumerate(chips)]

    shapes = [_sds((3, p.shape[0], p.shape[2], p.shape[3]), p.dtype) for p in parts]
    return _Exchange(parts, shapes, 3 * len(parts), copies)


def _add_chip_shards(name, p, b, jc_idx, l0, n_layers, buf):
    n, _, rh, X = p.shape
    tr = _tile(rh, 256, BF16_ROWS)

    def body(jc_ref, p_ref, b_ref, *rest):
        rest[-1][...] = ((p_ref[...] + b_ref[0].astype(F32)) + b_ref[1].astype(F32)) + b_ref[2].astype(F32)

    in_specs = [pl.BlockSpec((None, None, tr, X), lambda l, i, jc: (l, jc[0], i, 0)),
                pl.BlockSpec((3, None, tr, X), lambda l, i, jc: (0, l, i, 0))]
    operands = [jc_idx, p, b]
    if buf is not None:
        in_specs.append(ANY)
        operands.append(buf)
    return _pcall(
        body,
        grid_spec=pltpu.PrefetchScalarGridSpec(
            num_scalar_prefetch=1, grid=(n, rh // tr), in_specs=in_specs,
            out_specs=pl.BlockSpec((None, None, tr, X), lambda l, i, jc: (l + l0, jc[1], i, 0))),
        out_shape=_sds((n_layers, 2, rh, X), F32), input_output_aliases={3: 0} if buf is not None else {},
        compiler_params=_cp("parallel", "parallel"), name=name,
    )(*operands)


def _join_core_halves(bufs):
    n = len(bufs)

    def body(*refs):
        outs = refs[n:2 * n]
        send_sems, recv_sems = refs[2 * n:]
        x, y, c, _ = _place()
        cps = [pltpu.make_async_remote_copy(src_ref=outs[t].at[:, c], dst_ref=outs[t].at[:, c], send_sem=send_sems.at[t],
                                            recv_sem=recv_sems.at[t], device_id=(x, y, 1 - c), device_id_type=MESH)
               for t in range(n)]
        for cp in cps:
            cp.start()
        for t in range(n):
            pltpu.make_async_remote_copy(src_ref=outs[t].at[:, c], dst_ref=outs[t].at[:, 1 - c], send_sem=send_sems.at[t],
                                         recv_sem=recv_sems.at[t], device_id=(x, y, 1 - c), device_id_type=MESH).wait()

    outs = _pcall(
        body, in_specs=[ANY] * n, out_specs=[ANY] * n, out_shape=[_sds(b.shape, F32) for b in bufs],
        input_output_aliases={t: t for t in range(n)},
        scratch_shapes=[pltpu.SemaphoreType.DMA((n,)), pltpu.SemaphoreType.DMA((n,))],
        name="grad_join_core_halves",
    )(*bufs)
    return [o.reshape(o.shape[0], 2 * o.shape[2], o.shape[3]) for o in outs]


def _all_reduce_small(packed):
    R, C = packed.shape

    def body(x_ref, o_ref, slots, send_sems, recv_sems):
        x, y, c, _ = _place()
        me = 4 * x + 2 * y + c
        slots[me] = x_ref[...]
        cps = []
        for d in range(N_DEV):
            to = (d // 4, (d // 2) % 2, d % 2)
            cp = pltpu.make_async_remote_copy(src_ref=x_ref, dst_ref=slots.at[me], send_sem=send_sems.at[d],
                                              recv_sem=recv_sems.at[me], device_id=to, device_id_type=MESH)
            cps.append(cp)

            @pl.when(d != me)
            def _():
                cp.start()

        for d in range(N_DEV):
            @pl.when(d != me)
            def _():
                pltpu.make_async_remote_copy(src_ref=x_ref, dst_ref=slots.at[d], send_sem=send_sems.at[d],
                                             recv_sem=recv_sems.at[d], device_id=(x, y, c), device_id_type=MESH).wait_recv()
                cps[d].wait_send()

        acc = slots[0]
        for d in range(1, N_DEV):
            acc = acc + slots[d]
        o_ref[...] = acc

    vm = pl.BlockSpec(memory_space=pltpu.VMEM)
    return _pcall(
        body, in_specs=[vm], out_specs=vm, out_shape=_sds((R, C), F32),
        scratch_shapes=[pltpu.VMEM((N_DEV, R, C), F32), pltpu.SemaphoreType.DMA((N_DEV,)), pltpu.SemaphoreType.DMA((N_DEV,))],
        compiler_params=pltpu.CompilerParams(vmem_limit_bytes=VMEM_LIMIT_BYTES), name="all_reduce_small",
    )(packed)


PACK = SUBLANES * LANES


def _pack(arrays):
    flat = []
    for a in arrays:
        v = a.reshape(-1)
        flat.append(jnp.pad(v, (0, (-v.shape[0]) % PACK)))
    return jnp.concatenate(flat).reshape(-1, LANES)


def _unpack(packed, shapes):
    flat = packed.reshape(-1)
    out, pos = [], 0
    for s in shapes:
        n = 1
        for d in s:
            n *= d
        out.append(flat[pos:pos + n].reshape(s))
        pos += n + (-n) % PACK
    return out


def kernel(x, mix_norm_g, ffn_norm_g, conv_w_in, conv_a_dw_w, conv_a_dw_b, conv_a_ln_g, conv_a_ln_b, conv_b_dw_w, conv_w_out, attn_w_qkv, attn_q_g, attn_k_g, attn_w_o, ffn_w_up, ffn_dw_w, ffn_dw_b, ffn_w_down, loss_target, m_mix_norm_g, m_ffn_norm_g, m_conv_w_in, m_conv_a_dw_w, m_conv_a_dw_b, m_conv_a_ln_g, m_conv_a_ln_b, m_conv_b_dw_w, m_conv_w_out, m_attn_w_qkv, m_attn_q_g, m_attn_k_g, m_attn_w_o, m_ffn_w_up, m_ffn_dw_w, m_ffn_dw_b, m_ffn_w_down, v_mix_norm_g, v_ffn_norm_g, v_conv_w_in, v_conv_a_dw_w, v_conv_a_dw_b, v_conv_a_ln_g, v_conv_a_ln_b, v_conv_b_dw_w, v_conv_w_out, v_attn_w_qkv, v_attn_q_g, v_attn_k_g, v_attn_w_o, v_ffn_w_up, v_ffn_dw_w, v_ffn_dw_b, v_ffn_w_down):
    depth = mix_norm_g.shape[0]
    n_even, n_odd = conv_w_in.shape[0], attn_w_qkv.shape[0]
    S, D = x.shape[1], x.shape[2]
    dg = D // 2
    x0 = x.reshape(S, D)
    target = loss_target.reshape(S, D)
    j_me = 2 * lax.axis_index("x") + lax.axis_index("y")
    c_me = lax.axis_index("c")
    j_idx = j_me.astype(jnp.int32).reshape(1)
    c_idx = c_me.astype(jnp.int32).reshape(1)

    col_names = ["conv_w_in", "attn_w_qkv", "ffn_w_up"]
    row_names = ["conv_w_out", "attn_w_o", "ffn_w_down"]
    local = dict(conv_w_in=conv_w_in, attn_w_qkv=attn_w_qkv, ffn_w_up=ffn_w_up, conv_w_out=conv_w_out, attn_w_o=attn_w_o,
                 ffn_w_down=ffn_w_down)
    gbuf = {n: _place_shard(f"place_{n}", local[n], j_idx) for n in col_names + row_names}

    def weights_of(layer):
        mixer = ("conv_w_in", "conv_w_out") if layer % 2 == 0 else ("attn_w_qkv", "attn_w_o")
        return {mixer[0]: (layer // 2, 1), mixer[1]: (layer // 2, 1), "ffn_w_up": (layer, 1), "ffn_w_down": (layer, 1)}

    def w_col(n):
        return gbuf[n].reshape(gbuf[n].shape[0], N_CHIPS, -1, gbuf[n].shape[4])

    def w_row(n):
        return gbuf[n].reshape(gbuf[n].shape[0], -1, gbuf[n].shape[4])

    def carried(kernel_out, make_exchange, group):
        if not group:
            return kernel_out(None)
        out, *new = kernel_out(make_exchange([gbuf[n] for n in group], list(group.values())))
        gbuf.update(zip(group, new))
        return out

    first = {"conv_w_in": (0, 1), "conv_w_out": (0, 1)}
    ffn_first = {"ffn_w_up": (0, 1), "ffn_w_down": (0, 1)}
    outs, (a_dw, b_dw, f_dw) = _all_gather_weights([gbuf[n] for n in first], list(first.values()),
                                                   [conv_a_dw_w, conv_b_dw_w, ffn_dw_w])
    gbuf.update(zip(first, outs))
    unshard = lambda a: jnp.moveaxis(a, 1, 2).reshape(a.shape[0], a.shape[2], N_CHIPS * a.shape[3])
    a_dw, b_dw, f_dw = unshard(a_dw), unshard(b_dw), unshard(f_dw)
    qk_gain = [jnp.stack([jnp.tile(attn_q_g[i], LANES // HEAD_DIM), jnp.tile(attn_k_g[i], LANES // HEAD_DIM)])
               for i in range(n_odd)]

    saved = []
    xc = x0
    for layer in range(depth):
        i = layer // 2
        tag = f"l{layer}"
        s = {"x_in": xc}
        here = weights_of(layer) if layer else None
        h = carried(lambda ex: _rms_fwd(f"rms_mix_fwd_{tag}", xc, mix_norm_g, layer, ex), _gather_cores_exchange, here)
        s["h"] = h
        if layer % 2 == 0:
            p = _mm_fwd(f"conv_in_fwd_{tag}", h, w_col("conv_w_in"), i, colshard=True)
            ab = carried(lambda ex: _convmix_fwd(f"convmix_fwd_{tag}", p, a_dw, conv_a_dw_b, conv_a_ln_g, conv_a_ln_b, b_dw,
                                                 i, ex), _gather_chips_exchange, None if layer else ffn_first)
            xm = _mm_fwd(f"conv_out_fwd_{tag}", ab, w_row("conv_w_out"), i, colshard=False, res=xc)
            s.update(p=p, ab=ab)
        else:
            qkv = _mm_fwd(f"attn_qkv_fwd_{tag}", h, w_col("attn_w_qkv"), i, colshard=True)
            qs, kn, vb = _qknorm_fwd(f"qknorm_fwd_{tag}", qkv, qk_gain[i])
            o = _attn_fwd(f"attn_fwd_{tag}", qs, kn, vb)
            xm = _mm_fwd(f"attn_out_fwd_{tag}", o, w_row("attn_w_o"), i, colshard=False, res=xc)
            s.update(qkv=qkv, qs=qs, kn=kn, vb=vb, o=o)
        s["x_mid"] = xm
        h2 = carried(lambda ex: _rms_fwd(f"rms_ffn_fwd_{tag}", xm, ffn_norm_g, layer, ex), _gather_cores_exchange,
                     None if layer else ffn_first)
        u2 = _mm_fwd(f"ffn_up_fwd_{tag}", h2, w_col("ffn_w_up"), layer, colshard=True, out_split=2)
        f = carried(lambda ex: _ffn_mid_fwd(f"ffn_mid_fwd_{tag}", u2, f_dw, ffn_dw_b, layer, ex), _gather_chips_exchange,
                    weights_of(layer + 1) if layer + 1 < depth else None)
        xc = _mm_fwd(f"ffn_down_fwd_{tag}", f, w_row("ffn_w_down"), layer, colshard=False, res=xm)
        s.update(h2=h2, u2=u2, f=f)
        saved.append(s)

    dx, loss_tile = _loss_fwd_bwd("loss", xc, target)

    w_in, w_qkv, w_up = w_col("conv_w_in"), w_col("attn_w_qkv"), w_col("ffn_w_up")
    w_out, w_o, w_down = w_row("conv_w_out"), w_row("attn_w_o"), w_row("ffn_w_down")
    g_up = g_down = g_in = g_out = g_qkv = g_o = None
    big_names = col_names + row_names

    def halves_view(n, g):
        if n in col_names:
            return g.reshape(g.shape[0], N_CHIPS, 2, g.shape[2] // 2, g.shape[3])
        return g.reshape(g.shape[0], N_CHIPS, 2, g.shape[1] // (2 * N_CHIPS), g.shape[2])

    ffn_of_0 = {"ffn_w_up": (0, 1), "ffn_w_down": (0, 1)}
    mixer_of_0 = {"conv_w_in": (0, 1), "conv_w_out": (0, 1)}
    summed_parts = {n: [] for n in big_names}

    def stacks():
        return {"conv_w_in": g_in, "attn_w_qkv": g_qkv, "ffn_w_up": g_up, "conv_w_out": g_out, "attn_w_o": g_o,
                "ffn_w_down": g_down}

    def core_exchange(group):
        return _core_halves_exchange([halves_view(n, stacks()[n]) for n in group], list(group.values()))

    def chip_exchange(tag, arrived):
        sums, parts = [], []
        for group, from_sibling in arrived:
            for n, a in zip(group, from_sibling):
                f32_sum, bf16_sum = _add_core_halves(f"grad_add_core_{n}_{tag}_{group[n][0]}", halves_view(n, stacks()[n]), a,
                                                     c_idx, group[n][0])
                sums.append((n, group[n][0], f32_sum))
                parts.append(bf16_sum)
        return _chip_shards_exchange(parts), sums

    def record(sums, from_chips):
        for (n, l0, f32_sum), b in zip(sums, from_chips):
            summed_parts[n].append((l0, f32_sum, b))

    d_mix_g, d_ffn_g = [None] * depth, [None] * depth
    d_ffn_dw_w, d_ffn_dw_b = [None] * depth, [None] * depth
    d_a_dw_w, d_a_dw_b, d_a_ln_g, d_a_ln_b, d_b_dw_w = ([None] * n_even for _ in range(5))
    d_q_g, d_k_g = [None] * n_odd, [None] * n_odd
    for layer in reversed(range(depth)):
        i = layer // 2
        tag = f"l{layer}"
        s = saved[layer]
        df = _mm_dgrad(f"ffn_down_dgrad_{tag}", dx, w_down, layer, colshard=False)
        g_down = _mm_wgrad(f"ffn_down_wgrad_{tag}", s["f"], dx, layer, depth, g_down, colshard=False)
        above = weights_of(layer + 1) if layer + 1 < depth else None
        arrived = []
        du2, dww, dwb, *from_sibling = _ffn_mid_bwd(f"ffn_mid_bwd_{tag}", s["u2"], df, f_dw, ffn_dw_b, layer,
                                                    core_exchange(above) if above else None)
        if above:
            arrived.append((above, from_sibling))
        d_ffn_dw_w[layer] = jnp.moveaxis(dww, 0, 1).reshape(FFN_CONV_WIDTH, -1)
        d_ffn_dw_b[layer] = dwb.reshape(-1)
        dh2 = _mm_dgrad(f"ffn_up_dgrad_{tag}", du2, w_up, layer, colshard=True)
        g_up = _mm_wgrad(f"ffn_up_wgrad_{tag}", s["h2"], du2, layer, depth, g_up, colshard=True)
        dx, dg_, *from_sibling = _rms_bwd(f"rms_ffn_bwd_{tag}", s["x_mid"], ffn_norm_g, layer, dh2, dx,
                                          core_exchange(ffn_of_0) if layer == 0 else None)
        if layer == 0:
            arrived.append((ffn_of_0, from_sibling))
        d_ffn_g[layer] = dg_.reshape(-1)
        if layer % 2 == 0:
            dab = _mm_dgrad(f"conv_out_dgrad_{tag}", dx, w_out, i, colshard=False)
            g_out = _mm_wgrad(f"conv_out_wgrad_{tag}", s["ab"], dx, i, n_even, g_out, colshard=False)
            chip_ex, sums = chip_exchange(tag, arrived) if arrived else (None, [])
            dp, daw, dab_b, dlg, dlb, dbw, *from_chips = _convmix_bwd(
                f"convmix_bwd_{tag}", s["p"], dab, a_dw, conv_a_dw_b, conv_a_ln_g, conv_a_ln_b, b_dw, i, chip_ex)
            record(sums, from_chips)
            d_a_dw_w[i], d_a_dw_b[i], d_a_ln_g[i], d_a_ln_b[i], d_b_dw_w[i] = (
                daw, dab_b.reshape(-1), dlg.reshape(-1), dlb.reshape(-1), dbw)
            dh = _mm_dgrad(f"conv_in_dgrad_{tag}", dp, w_in, i, colshard=True)
            g_in = _mm_wgrad(f"conv_in_wgrad_{tag}", s["h"], dp, i, n_even, g_in, colshard=True)
        else:
            do = _mm_dgrad(f"attn_out_dgrad_{tag}", dx, w_o, i, colshard=False)
            g_o = _mm_wgrad(f"attn_out_wgrad_{tag}", s["o"], dx, i, n_odd, g_o, colshard=False)
            chip_ex, sums = chip_exchange(tag, arrived) if arrived else (None, [])
            dq, dk, dv, *from_chips = _attn_bwd(f"attn_bwd_{tag}", s["qs"], s["kn"], s["vb"], s["o"], do, chip_ex)
            record(sums, from_chips)
            dqkv, dgain = _qknorm_bwd(f"qknorm_bwd_{tag}", s["qkv"], dq, dk, dv, qk_gain[i])
            d_q_g[i] = dgain[0, :HEAD_DIM] + dgain[0, HEAD_DIM:]
            d_k_g[i] = dgain[1, :HEAD_DIM] + dgain[1, HEAD_DIM:]
            dh = _mm_dgrad(f"attn_qkv_dgrad_{tag}", dqkv, w_qkv, i, colshard=True)
            g_qkv = _mm_wgrad(f"attn_qkv_wgrad_{tag}", s["h"], dqkv, i, n_odd, g_qkv, colshard=True)
        dx, dg_ = _rms_bwd(f"rms_mix_bwd_{tag}", s["x_in"], mix_norm_g, layer, dh, dx)
        d_mix_g[layer] = dg_.reshape(-1)
    grad_x = dx.reshape(1, S, D)

    small = {
        "mix_norm_g": jnp.stack(d_mix_g), "ffn_norm_g": jnp.stack(d_ffn_g),
        "conv_a_dw_w": jnp.stack(d_a_dw_w), "conv_a_dw_b": jnp.stack(d_a_dw_b),
        "conv_a_ln_g": jnp.stack(d_a_ln_g), "conv_a_ln_b": jnp.stack(d_a_ln_b),
        "conv_b_dw_w": jnp.stack(d_b_dw_w), "attn_q_g": jnp.stack(d_q_g), "attn_k_g": jnp.stack(d_k_g),
        "ffn_dw_w": jnp.stack(d_ffn_dw_w), "ffn_dw_b": jnp.stack(d_ffn_dw_b),
    }
    small_names = list(small)
    summed = _all_reduce_small(_pack([loss_tile] + [small[n] for n in small_names]))
    parts = _unpack(summed, [loss_tile.shape] + [small[n].shape for n in small_names])
    loss = parts[0][0, 0]
    small_g = dict(zip(small_names, parts[1:]))
    for n in ("conv_a_dw_w", "conv_b_dw_w", "ffn_dw_w"):
        cs = small_g[n].shape[2] // N_CHIPS
        small_g[n] = lax.dynamic_slice_in_dim(small_g[n], j_me * cs, cs, axis=2)

    from_sibling = _run_exchange("grad_exchange_core_halves", core_exchange(mixer_of_0))
    chip_ex, sums = chip_exchange("last", [(mixer_of_0, from_sibling)])
    record(sums, _run_exchange("grad_exchange_chip_shards", chip_ex))
    jc_idx = jnp.concatenate([j_idx, c_idx])
    totals = {}
    for n in big_names:
        total = None
        for l0, p, b in summed_parts[n]:
            total = _add_chip_shards(f"grad_add_chips_{n}_{l0}", p, b, jc_idx, l0, stacks()[n].shape[0], total)
        totals[n] = total
    big_g = dict(zip(big_names, _join_core_halves([totals[n] for n in big_names])))

    weights = dict(mix_norm_g=mix_norm_g, ffn_norm_g=ffn_norm_g, conv_w_in=conv_w_in, conv_a_dw_w=conv_a_dw_w, conv_a_dw_b=conv_a_dw_b, conv_a_ln_g=conv_a_ln_g, conv_a_ln_b=conv_a_ln_b, conv_b_dw_w=conv_b_dw_w, conv_w_out=conv_w_out, attn_w_qkv=attn_w_qkv, attn_q_g=attn_q_g, attn_k_g=attn_k_g, attn_w_o=attn_w_o, ffn_w_up=ffn_w_up, ffn_dw_w=ffn_dw_w, ffn_dw_b=ffn_dw_b, ffn_w_down=ffn_w_down)
    m_in = dict(mix_norm_g=m_mix_norm_g, ffn_norm_g=m_ffn_norm_g, conv_w_in=m_conv_w_in, conv_a_dw_w=m_conv_a_dw_w, conv_a_dw_b=m_conv_a_dw_b, conv_a_ln_g=m_conv_a_ln_g, conv_a_ln_b=m_conv_a_ln_b, conv_b_dw_w=m_conv_b_dw_w, conv_w_out=m_conv_w_out, attn_w_qkv=m_attn_w_qkv, attn_q_g=m_attn_q_g, attn_k_g=m_attn_k_g, attn_w_o=m_attn_w_o, ffn_w_up=m_ffn_w_up, ffn_dw_w=m_ffn_dw_w, ffn_dw_b=m_ffn_dw_b, ffn_w_down=m_ffn_w_down)
    v_in = dict(mix_norm_g=v_mix_norm_g, ffn_norm_g=v_ffn_norm_g, conv_w_in=v_conv_w_in, conv_a_dw_w=v_conv_a_dw_w, conv_a_dw_b=v_conv_a_dw_b, conv_a_ln_g=v_conv_a_ln_g, conv_a_ln_b=v_conv_a_ln_b, conv_b_dw_w=v_conv_b_dw_w, conv_w_out=v_conv_w_out, attn_w_qkv=v_attn_w_qkv, attn_q_g=v_attn_q_g, attn_k_g=v_attn_k_g, attn_w_o=v_attn_w_o, ffn_w_up=v_ffn_w_up, ffn_dw_w=v_ffn_dw_w, ffn_dw_b=v_ffn_dw_b, ffn_w_down=v_ffn_w_down)
    order = list(weights)
    grads, delta, new_m, new_v = {}, {}, {}, {}
    for n in big_names:
        grads[n] = big_g[n]
        delta[n], new_m[n], new_v[n] = _adamw(f"adamw_{n}", weights[n], big_g[n], m_in[n], v_in[n])
    shapes = [weights[n].shape for n in small_names]
    packed = [_pack([d[n] for n in small_names]) for d in (weights, small_g, m_in, v_in)]
    upd = _adamw("adamw_small", *[p[None] for p in packed])
    for out, res in zip((delta, new_m, new_v), upd):
        out.update(zip(small_names, _unpack(res[0], shapes)))
    grads.update({n: small_g[n].reshape(weights[n].shape) for n in small_names})
    return (loss, grad_x, *[grads[n] for n in order], *[delta[n] for n in order], *[new_m[n] for n in order],
            *[new_v[n] for n in order])
```

```python
import jax
import jax.numpy as jnp
from jax import lax
from jax.experimental import pallas as pl
from jax.experimental.pallas import tpu as pltpu

F32 = jnp.float32
BF16 = jnp.bfloat16
EPS = 1e-6
CONV_A_WIDTH = 31
CONV_B_WIDTH = 3
FFN_CONV_WIDTH = 3
HEAD_DIM = 64
ADAM_LR = 0.001
ADAM_B1 = 0.9
ADAM_B2 = 0.999
ADAM_EPS = 1e-08
ADAM_WD = 0.01
ADAM_STEP = 10

LANES = 128
SUBLANES = 8
BF16_ROWS = 16
V7X_VMEM_BYTES = 64 * 1024 * 1024
VMEM_LIMIT_BYTES = V7X_VMEM_BYTES * 3 // 4
MM_VMEM_BUDGET = VMEM_LIMIT_BYTES * 4 // 5
MM_ROWS = 1024
N_CHIPS = 4
N_DEV = 8
HALO_A = 32
HALO_S = 8
ELT_ROWS = 64
FFN_MID_ROWS = 512
NORM_ROWS = 1024
ATTN_BLOCK = 128
ATTN_SUB = 2
ATTN_MORE = 2
ATTN_TILES = 4
EXP_UNDERFLOW = -104.0
MESH = pl.DeviceIdType.MESH
ANY = pl.BlockSpec(memory_space=pl.ANY)
NT = (((1,), (1,)), ((), ()))
NN = (((1,), (0,)), ((), ()))
TN = (((0,), (0,)), ((), ()))


def _pcall(body, **kw):
    return pl.pallas_call(body, **kw)


def _cp(*sem):
    return pltpu.CompilerParams(dimension_semantics=sem, vmem_limit_bytes=VMEM_LIMIT_BYTES)


def _sds(shape, dtype):
    return jax.ShapeDtypeStruct(tuple(shape), dtype)


def _tile(n, cap, align=LANES):
    if n <= cap:
        return n
    for t in range(cap - cap % align, 0, -align):
        if n % t == 0:
            return t
    return n


def _sig(x):
    return 0.5 * jnp.tanh(0.5 * x) + 0.5


def _rowsum(x):
    return jnp.sum(x, axis=0, keepdims=True)


def _fold(x):
    acc = x[0:SUBLANES]
    for r in range(SUBLANES, x.shape[0], SUBLANES):
        acc = acc + x[r:r + SUBLANES]
    return acc


def _with_exchange(ex, body, in_specs, out_specs, out_shape, scratch, operands, first, last):
    if ex is None:
        return body, in_specs, out_specs, out_shape, scratch, operands, {}
    n_in, n_out, n_scr = len(in_specs), len(out_specs), len(scratch)
    e_in, e_out = len(ex.operands), len(ex.out_shapes)

    def hosted(*refs):
        refs = list(refs)
        ins, refs = refs[:n_in], refs[n_in:]
        e_ins, refs = refs[:e_in], refs[e_in:]
        outs, refs = refs[:n_out], refs[n_out:]
        e_outs, refs = refs[:e_out], refs[e_out:]
        scr, sems = refs[:n_scr], refs[n_scr:]

        @pl.when(first())
        def _():
            ex.start(e_ins, e_outs, sems)

        body(*ins, *outs, *scr)

        @pl.when(last())
        def _():
            ex.wait(e_ins, e_outs, sems)

    return (hosted, in_specs + [ANY] * e_in, out_specs + [ANY] * e_out, out_shape + ex.out_shapes, scratch + ex.scratch,
            operands + ex.operands, {n_in + i: n_out + o for i, o in ex.aliases.items()})


def _mm_call(name, dn, operands, in_specs, out_shape, out_spec, grid, nk, acc_shape, has_res, has_alias):
    def body(*refs):
        a_ref, b_ref = refs[0], refs[1]
        pos = 2
        res_ref = refs[pos] if has_res else None
        pos += int(has_res) + int(has_alias)
        o_ref = refs[pos]
        acc_ref = refs[pos + 1] if nk > 1 else None
        p = lax.dot_general(a_ref[...].astype(BF16), b_ref[...].astype(BF16), dn, preferred_element_type=F32)

        def finish(v):
            if has_res:
                v = v + res_ref[...]
            o_ref[...] = v.astype(o_ref.dtype)

        if nk == 1:
            finish(p)
        else:
            k = pl.program_id(2)

            @pl.when(k == 0)
            def _():
                acc_ref[...] = p

            @pl.when(k > 0)
            def _():
                acc_ref[...] += p

            @pl.when(k == nk - 1)
            def _():
                finish(acc_ref[...])

    aliases = {len(operands) - 1: 0} if has_alias else {}
    return _pcall(
        body, grid=grid, in_specs=in_specs, out_specs=out_spec, out_shape=out_shape,
        scratch_shapes=[pltpu.VMEM(acc_shape, F32)] if nk > 1 else [],
        input_output_aliases=aliases, compiler_params=_cp("parallel", "parallel", "arbitrary"), name=name,
    )(*operands)


def _mm_fwd(name, a, w, l, *, colshard, res=None, out_split=1):
    M, K = a.shape
    tm = _tile(M, MM_ROWS, BF16_ROWS)
    if colshard and out_split == 1 and res is None:
        cs = w.shape[3]
        th = _tile(M, MM_ROWS // 2, BF16_ROWS)
        if 2 * (N_CHIPS * K * cs * 2 + th * N_CHIPS * cs * 4 + th * K * a.dtype.itemsize) <= MM_VMEM_BUDGET:
            def body(a_ref, b_ref, o_ref):
                av = a_ref[...].astype(BF16)
                for j in range(N_CHIPS):
                    o_ref[:, j * cs:(j + 1) * cs] = jnp.dot(av, b_ref[j], preferred_element_type=F32)

            return _pcall(
                body, grid=(M // th,),
                in_specs=[pl.BlockSpec((th, K), lambda i: (i, 0)), pl.BlockSpec((None, N_CHIPS, K, cs), lambda i: (l, 0, 0, 0))],
                out_specs=pl.BlockSpec((th, N_CHIPS * cs), lambda i: (i, 0)), out_shape=_sds((M, N_CHIPS * cs), F32),
                compiler_params=_cp("parallel"), name=name,
            )(a, w)
    if colshard:
        cs = w.shape[3]
        N, tn, tk = N_CHIPS * cs, cs, K
        b_spec = pl.BlockSpec((None, None, tk, tn), lambda j, i, k: (l, j, k, 0))
    else:
        N = w.shape[2]
        tn, tk = _tile(N, 1024), K
        if K > 1536:
            tm = _tile(M, MM_ROWS // 2, BF16_ROWS)
        b_spec = pl.BlockSpec((None, tk, tn), lambda j, i, k: (l, k, j))
    nk = K // tk
    in_specs = [pl.BlockSpec((tm, tk), lambda j, i, k: (i, k)), b_spec]
    operands = [a, w]
    if res is not None:
        in_specs.append(pl.BlockSpec((tm, tn), lambda j, i, k: (i, j)))
        operands.append(res)
    if out_split == 1:
        out_shape = _sds((M, N), F32)
        out_spec = pl.BlockSpec((tm, tn), lambda j, i, k: (i, j))
    else:
        per = N // tn // out_split
        out_shape = _sds((out_split, M, N // out_split), F32)
        out_spec = pl.BlockSpec((None, tm, tn), lambda j, i, k: (j // per, i, j % per))
    return _mm_call(name, NN, operands, in_specs, out_shape, out_spec, (N // tn, M // tm, nk), nk, (tm, tn),
                    res is not None, False)


def _mm_dgrad(name, g, w, l, *, colshard):
    split = g.ndim == 3
    M = g.shape[-2]
    tm = _tile(M, MM_ROWS, BF16_ROWS)
    if colshard:
        kw, cs = w.shape[2], w.shape[3]
        tm = _tile(M, MM_ROWS // 2, BF16_ROWS)
        per = N_CHIPS // g.shape[0] if split else N_CHIPS

        def body(a_ref, b_ref, o_ref):
            acc = None
            for j in range(N_CHIPS):
                cols = slice((j % per) * cs, (j % per + 1) * cs)
                a = a_ref[j // per, :, cols] if split else a_ref[:, cols]
                p = lax.dot_general(a.astype(BF16), b_ref[j], NT, preferred_element_type=F32)
                acc = p if acc is None else acc + p
            o_ref[...] = acc

        a_spec = (pl.BlockSpec((g.shape[0], tm, g.shape[2]), lambda i: (0, i, 0)) if split
                  else pl.BlockSpec((tm, N_CHIPS * cs), lambda i: (i, 0)))
        return _pcall(
            body, grid=(M // tm,),
            in_specs=[a_spec, pl.BlockSpec((None, N_CHIPS, kw, cs), lambda i: (l, 0, 0, 0))],
            out_specs=pl.BlockSpec((tm, kw), lambda i: (i, 0)), out_shape=_sds((M, kw), F32),
            compiler_params=_cp("parallel"), name=name,
        )(g, w)
    else:
        kw, ncon = w.shape[1], w.shape[2]
        tn, tk = _tile(kw, 1408), _tile(ncon, 1536)
        nk = ncon // tk
        th = _tile(M, MM_ROWS // 2, BF16_ROWS)
        if nk == 1 and 2 * (kw * ncon * w.dtype.itemsize + th * kw * 4 + th * ncon * g.dtype.itemsize) <= MM_VMEM_BUDGET:
            tm, tn = th, kw
        b_spec = pl.BlockSpec((None, tn, tk), lambda j, i, k: (l, j, k))
    if split:
        per = nk // g.shape[0]
        a_spec = pl.BlockSpec((None, tm, tk), lambda j, i, k: (k // per, i, k % per))
    else:
        a_spec = pl.BlockSpec((tm, tk), lambda j, i, k: (i, k))
    out_shape = _sds((M, kw), F32)
    out_spec = pl.BlockSpec((tm, tn), lambda j, i, k: (i, j))
    return _mm_call(name, NT, [g, w], [a_spec, b_spec], out_shape, out_spec, (kw // tn, M // tm, nk), nk, (tm, tn),
                    False, False)


def _mm_wgrad(name, a, g, l, n_layers, buf, *, colshard):
    S, M = a.shape
    split = g.ndim == 3
    N = g.shape[-1] * (g.shape[0] if split else 1)
    tm = _tile(M, 1408)
    tn = N // N_CHIPS if colshard else _tile(N, 1024)
    per_row = 2 * (tm * a.dtype.itemsize + tn * g.dtype.itemsize)
    tk = _tile(S, max(BF16_ROWS, min(2048, (MM_VMEM_BUDGET - 3 * tm * tn * 4) // per_row)), BF16_ROWS)
    nk = S // tk
    if colshard:
        out_shape = _sds((n_layers, N_CHIPS, M, tn), F32)
        out_spec = pl.BlockSpec((None, None, tm, tn), lambda j, i, k: (l, j, i, 0))
    else:
        out_shape = _sds((n_layers, M, N), F32)
        out_spec = pl.BlockSpec((None, tm, tn), lambda j, i, k: (l, i, j))
    if split:
        per = N // tn // g.shape[0]
        b_spec = pl.BlockSpec((None, tk, tn), lambda j, i, k: (j // per, k, j % per))
    else:
        b_spec = pl.BlockSpec((tk, tn), lambda j, i, k: (k, j))
    in_specs = [pl.BlockSpec((tk, tm), lambda j, i, k: (k, i)), b_spec]
    operands = [a, g]
    if buf is not None:
        in_specs.append(ANY)
        operands.append(buf)
    return _mm_call(name, TN, operands, in_specs, out_shape, out_spec, (N // tn, M // tm, nk), nk, (tm, tn),
                    False, buf is not None)


def _rms_fwd(name, x, g, l, exchange=None):
    S, D = x.shape
    tm = _tile(S, NORM_ROWS, BF16_ROWS)
    n_i = S // tm

    def body(x_ref, g_ref, o_ref):
        xf = x_ref[...]
        r = lax.rsqrt(jnp.mean(xf * xf, axis=-1, keepdims=True) + EPS)
        o_ref[...] = (xf * r * g_ref[l:l + 1, :]).astype(BF16)

    body, in_specs, out_specs, out_shape, scratch, operands, aliases = _with_exchange(
        exchange, body, [pl.BlockSpec((tm, D), lambda i: (i, 0)), pl.BlockSpec(g.shape, lambda i: (0, 0))],
        [pl.BlockSpec((tm, D), lambda i: (i, 0))], [_sds((S, D), BF16)], [], [x, g],
        lambda: pl.program_id(0) == 0, lambda: pl.program_id(0) == n_i - 1)
    outs = _pcall(
        body, grid=(n_i,), in_specs=in_specs, out_specs=out_specs, out_shape=out_shape, scratch_shapes=scratch,
        input_output_aliases=aliases, compiler_params=_cp("arbitrary" if exchange else "parallel"), name=name,
    )(*operands)
    return outs if exchange else outs[0]


def _rms_bwd(name, x, g, l, dh, dres, exchange=None):
    S, D = x.shape
    tm = _tile(S, NORM_ROWS, SUBLANES)

    def body(x_ref, g_ref, dh_ref, dr_ref, dx_ref, dg_ref):
        xf = x_ref[...]
        r = lax.rsqrt(jnp.mean(xf * xf, axis=-1, keepdims=True) + EPS)
        xh = xf * r
        d = dh_ref[...]
        dxh = d * g_ref[l:l + 1, :]
        dx_ref[...] = dr_ref[...] + r * (dxh - xh * jnp.mean(dxh * xh, axis=-1, keepdims=True))

        @pl.when(pl.program_id(0) == 0)
        def _():
            dg_ref[...] = jnp.zeros_like(dg_ref)

        dg_ref[...] += _rowsum(d * xh)

    row = pl.BlockSpec((tm, D), lambda i: (i, 0))
    n_i = S // tm
    body, in_specs, out_specs, out_shape, scratch, operands, aliases = _with_exchange(
        exchange, body, [row, pl.BlockSpec(g.shape, lambda i: (0, 0)), row, row],
        [row, pl.BlockSpec((1, D), lambda i: (0, 0))], [_sds((S, D), F32), _sds((1, D), F32)], [], [x, g, dh, dres],
        lambda: pl.program_id(0) == 0, lambda: pl.program_id(0) == n_i - 1)
    return _pcall(
        body, grid=(n_i,), in_specs=in_specs, out_specs=out_specs, out_shape=out_shape, scratch_shapes=scratch,
        input_output_aliases=aliases, compiler_params=_cp("arbitrary"), name=name,
    )(*operands)


def _loss_fwd_bwd(name, y, t):
    S, D = y.shape
    tm = _tile(S, NORM_ROWS, SUBLANES)

    def body(y_ref, t_ref, dy_ref, l_ref):
        e = y_ref[...] - t_ref[...]
        dy_ref[...] = e * (1.0 / D)

        @pl.when(pl.program_id(0) == 0)
        def _():
            l_ref[...] = jnp.zeros_like(l_ref)

        l_ref[...] += 0.5 * jnp.sum(jnp.sum(e * e, axis=-1, keepdims=True) * (1.0 / D), axis=0, keepdims=True)

    row = pl.BlockSpec((tm, D), lambda i: (i, 0))
    return _pcall(
        body, grid=(S // tm,), in_specs=[row, row],
        out_specs=[row, pl.BlockSpec((SUBLANES, LANES), lambda i: (0, 0))],
        out_shape=[_sds((S, D), F32), _sds((SUBLANES, LANES), F32)],
        compiler_params=_cp("arbitrary"), name=name,
    )(y, t)


def _delayed_copies(us, n_rows):
    for s in range(1, SUBLANES):
        us[s, pl.ds(SUBLANES, n_rows - SUBLANES), :] = us[0, pl.ds(SUBLANES - s, n_rows - SUBLANES), :]


def _conv_a(aw_ref, ab_ref, l, us, row0, rows, dg):
    ka = CONV_A_WIDTH
    out = []
    for c0 in range(0, dg, LANES):
        lanes = slice(c0, c0 + LANES)
        acc = ab_ref[l:l + 1, lanes]
        for d in range(ka):
            a, s = divmod(d, SUBLANES)
            acc = acc + aw_ref[l, ka - 1 - d:ka - d, lanes] * us[s, pl.ds(row0 - SUBLANES * a, rows), lanes]
        out.append(acc)
    return jnp.concatenate(out, axis=1)


def _convmix_fwd(name, p, aw, ab, lg, lb, bw, l, exchange=None):
    S, W = p.shape
    dg = W // 5
    tm = _tile(S, 256, HALO_A)
    nb = tm // HALO_A
    ka, kb = CONV_A_WIDTH, CONV_B_WIDTH

    ext = HALO_A + tm
    rc = _tile(tm, ELT_ROWS, BF16_ROWS)

    def body(p_ref, ph_ref, aw_ref, ab_ref, lg_ref, lb_ref, bw_ref, o_ref, us, mext):
        first = pl.program_id(0) == 0
        ph = ph_ref[...]
        pc = p_ref[...]
        us[0, pl.ds(0, HALO_A), :] = jnp.where(first, 0.0, ph[:, 0:dg] * _sig(ph[:, dg:2 * dg]))
        us[0, pl.ds(HALO_A, tm), :] = pc[:, 0:dg] * _sig(pc[:, dg:2 * dg])
        mext[pl.ds(0, HALO_A), :] = jnp.where(first, 0.0, ph[:, 3 * dg:4 * dg] * ph[:, 4 * dg:5 * dg])
        mext[pl.ds(HALO_A, tm), :] = pc[:, 3 * dg:4 * dg] * pc[:, 4 * dg:5 * dg]
        _delayed_copies(us, ext)
        for r0 in range(0, tm, rc):
            rows = pl.ds(r0, rc)
            c = _conv_a(aw_ref, ab_ref, l, us, HALO_A + r0, rc, dg)
            xc = c - jnp.mean(c, axis=-1, keepdims=True)
            ln = xc * lax.rsqrt(jnp.mean(xc * xc, axis=-1, keepdims=True) + EPS) * lg_ref[l:l + 1, :] + lb_ref[l:l + 1, :]
            o_ref[rows, 0:dg] = (ln * _sig(ln)).astype(BF16)
            cb = bw_ref[l, 0:1, :] * mext[pl.ds(HALO_A - (kb - 1) + r0, rc), :]
            for k in range(1, kb):
                cb = cb + bw_ref[l, k:k + 1, :] * mext[pl.ds(HALO_A - (kb - 1) + k + r0, rc), :]
            o_ref[rows, dg:2 * dg] = (p_ref[rows, 2 * dg:3 * dg] * cb).astype(BF16)

    full = lambda a: pl.BlockSpec(a.shape, lambda i: (0,) * a.ndim)
    n_i = S // tm
    body, in_specs, out_specs, out_shape, scratch, operands, aliases = _with_exchange(
        exchange, body,
        [pl.BlockSpec((tm, W), lambda i: (i, 0)), pl.BlockSpec((HALO_A, W), lambda i: (jnp.maximum(i * nb - 1, 0), 0)),
         full(aw), full(ab), full(lg), full(lb), full(bw)],
        [pl.BlockSpec((tm, 2 * dg), lambda i: (i, 0))], [_sds((S, 2 * dg), BF16)],
        [pltpu.VMEM((SUBLANES, ext, dg), F32), pltpu.VMEM((ext, dg), F32)], [p, p, aw, ab, lg, lb, bw],
        lambda: pl.program_id(0) == 0, lambda: pl.program_id(0) == n_i - 1)
    outs = _pcall(
        body, grid=(n_i,), in_specs=in_specs, out_specs=out_specs, out_shape=out_shape, scratch_shapes=scratch,
        input_output_aliases=aliases, compiler_params=_cp("arbitrary" if exchange else "parallel"), name=name,
    )(*operands)
    return outs if exchange else outs[0]


def _convmix_bwd(name, p, dab, aw, ab, lg, lb, bw, l, exchange=None):
    S, W = p.shape
    dg = W // 5
    tm = _tile(S, 256, HALO_A)
    nb = tm // HALO_A
    n_i = S // tm
    ka, kb = CONV_A_WIDTH, CONV_B_WIDTH
    n = tm + HALO_A
    ext = HALO_A + n
    rc = _tile(tm, ELT_ROWS, BF16_ROWS)

    def body(p_ref, pp_ref, pn_ref, d_ref, dn_ref, aw_ref, ab_ref, lg_ref, lb_ref, bw_ref,
             dp_ref, daw_ref, dab_ref, dlg_ref, dlb_ref, dbw_ref, us, mext, dcs, dbext, accw):
        i = pl.program_id(0)
        first, last = i == 0, i == n_i - 1

        @pl.when(first)
        def _():
            for r in (daw_ref, dab_ref, dlg_ref, dlb_ref, dbw_ref):
                r[...] = jnp.zeros_like(r)

        accw[...] = jnp.zeros_like(accw)
        pp, pc, pn = pp_ref[...], p_ref[...], pn_ref[...]
        glu = lambda b: b[:, 0:dg] * _sig(b[:, dg:2 * dg])
        gch = lambda b: b[:, 3 * dg:4 * dg] * b[:, 4 * dg:5 * dg]
        us[0, pl.ds(0, HALO_A), :] = jnp.where(first, 0.0, glu(pp))
        us[0, pl.ds(HALO_A, tm), :] = glu(pc)
        us[0, pl.ds(HALO_A + tm, HALO_A), :] = glu(pn)
        mext[pl.ds(0, HALO_A), :] = jnp.where(first, 0.0, gch(pp))
        mext[pl.ds(HALO_A, tm), :] = gch(pc)
        mext[pl.ds(HALO_A + tm, HALO_A), :] = gch(pn)
        _delayed_copies(us, ext)
        chunks = [(r, rc) for r in range(0, tm, rc)] + [(tm, HALO_A)]
        g_ln = lg_ref[l:l + 1, :]
        zero8 = jnp.zeros((SUBLANES, dg), F32)

        acc_lg = acc_lb = acc_ab = zero8
        for r0, rows in chunks:
            c = _conv_a(aw_ref, ab_ref, l, us, HALO_A + r0, rows, dg)
            xc = c - jnp.mean(c, axis=-1, keepdims=True)
            rstd = lax.rsqrt(jnp.mean(xc * xc, axis=-1, keepdims=True) + EPS)
            chat = xc * rstd
            ln = chat * g_ln + lb_ref[l:l + 1, :]
            s = _sig(ln)
            da = d_ref[pl.ds(r0, rows), 0:dg] if r0 < tm else jnp.where(last, 0.0, dn_ref[:, 0:dg])
            dln = da * (s * (1.0 + ln * (1.0 - s)))
            dlnh = dln * g_ln
            dc = rstd * (dlnh - jnp.mean(dlnh, axis=-1, keepdims=True)
                         - chat * jnp.mean(dlnh * chat, axis=-1, keepdims=True))
            dcs[0, pl.ds(r0, rows), :] = dc
            if r0 < tm:
                acc_lg = acc_lg + _fold(dln * chat)
                acc_lb = acc_lb + _fold(dln)
                acc_ab = acc_ab + _fold(dc)
                for c0 in range(0, dg, LANES):
                    lanes = slice(c0, c0 + LANES)
                    for d in range(ka):
                        a, sh = divmod(d, SUBLANES)
                        k = ka - 1 - d
                        accw[pl.ds(SUBLANES * k, SUBLANES), lanes] += _fold(
                            dc[:, lanes] * us[sh, pl.ds(HALO_A + r0 - SUBLANES * a, rows), lanes])
        dlg_ref[...] += _rowsum(acc_lg)
        dlb_ref[...] += _rowsum(acc_lb)
        dab_ref[...] += _rowsum(acc_ab)
        for k in range(ka):
            daw_ref[k:k + 1, :] += _rowsum(accw[pl.ds(SUBLANES * k, SUBLANES), :])
        for s in range(1, SUBLANES):
            dcs[s, pl.ds(0, n - SUBLANES), :] = dcs[0, pl.ds(s, n - SUBLANES), :]
        for r0 in range(0, tm, rc):
            rows = pl.ds(r0, rc)
            parts = []
            for c0 in range(0, dg, LANES):
                lanes = slice(c0, c0 + LANES)
                acc = aw_ref[l, ka - 1:ka, lanes] * dcs[0, rows, lanes]
                for e in range(1, ka):
                    a, sh = divmod(e, SUBLANES)
                    acc = acc + aw_ref[l, ka - 1 - e:ka - e, lanes] * dcs[sh, pl.ds(r0 + SUBLANES * a, rc), lanes]
                parts.append(acc)
            du = jnp.concatenate(parts, axis=1)
            sg = _sig(p_ref[rows, dg:2 * dg])
            dp_ref[rows, 0:dg] = (du * sg).astype(BF16)
            dp_ref[rows, dg:2 * dg] = (du * p_ref[rows, 0:dg] * sg * (1.0 - sg)).astype(BF16)

        for r0, rows in chunks:
            if r0 < tm:
                dbext[pl.ds(r0, rows), :] = d_ref[pl.ds(r0, rows), dg:2 * dg] * p_ref[pl.ds(r0, rows), 2 * dg:3 * dg]
            else:
                dbext[pl.ds(r0, rows), :] = jnp.where(last, 0.0, dn_ref[:, dg:2 * dg] * pn[:, 2 * dg:3 * dg])
        acc_bw = [zero8] * kb
        for r0 in range(0, tm, rc):
            rows = pl.ds(r0, rc)
            m_k = [mext[pl.ds(HALO_A - (kb - 1) + k + r0, rc), :] for k in range(kb)]
            cb = bw_ref[l, 0:1, :] * m_k[0]
            dm = bw_ref[l, 0:1, :] * dbext[pl.ds(r0 + kb - 1, rc), :]
            for k in range(1, kb):
                cb = cb + bw_ref[l, k:k + 1, :] * m_k[k]
                dm = dm + bw_ref[l, k:k + 1, :] * dbext[pl.ds(r0 + kb - 1 - k, rc), :]
            dcb = dbext[rows, :]
            acc_bw = [acc_bw[k] + _fold(dcb * m_k[k]) for k in range(kb)]
            dp_ref[rows, 2 * dg:3 * dg] = (d_ref[rows, dg:2 * dg] * cb).astype(BF16)
            dp_ref[rows, 3 * dg:4 * dg] = (dm * p_ref[rows, 4 * dg:5 * dg]).astype(BF16)
            dp_ref[rows, 4 * dg:5 * dg] = (dm * p_ref[rows, 3 * dg:4 * dg]).astype(BF16)
        for k in range(kb):
            dbw_ref[k:k + 1, :] += _rowsum(acc_bw[k])

    full = lambda a: pl.BlockSpec(a.shape, lambda i: (0,) * a.ndim)
    prev = lambda i: (jnp.maximum(i * nb - 1, 0), 0)
    nxt = lambda i: (jnp.minimum((i + 1) * nb, S // HALO_A - 1), 0)
    acc = lambda r: pl.BlockSpec((r, dg), lambda i: (0, 0))
    body, in_specs, out_specs, out_shape, scratch, operands, aliases = _with_exchange(
        exchange, body,
        [pl.BlockSpec((tm, W), lambda i: (i, 0)), pl.BlockSpec((HALO_A, W), prev), pl.BlockSpec((HALO_A, W), nxt),
         pl.BlockSpec((tm, 2 * dg), lambda i: (i, 0)), pl.BlockSpec((HALO_A, 2 * dg), nxt),
         full(aw), full(ab), full(lg), full(lb), full(bw)],
        [pl.BlockSpec((tm, W), lambda i: (i, 0)), acc(ka), acc(1), acc(1), acc(1), acc(kb)],
        [_sds((S, W), BF16), _sds((ka, dg), F32), _sds((1, dg), F32), _sds((1, dg), F32), _sds((1, dg), F32),
         _sds((kb, dg), F32)],
        [pltpu.VMEM((SUBLANES, ext, dg), F32), pltpu.VMEM((ext, dg), F32), pltpu.VMEM((SUBLANES, n, dg), F32),
         pltpu.VMEM((n, dg), F32), pltpu.VMEM((SUBLANES * ka, dg), F32)],
        [p, p, p, dab, dab, aw, ab, lg, lb, bw],
        lambda: pl.program_id(0) == 0, lambda: pl.program_id(0) == n_i - 1)
    return _pcall(
        body, grid=(n_i,), in_specs=in_specs, out_specs=out_specs, out_shape=out_shape, scratch_shapes=scratch,
        input_output_aliases=aliases, compiler_params=_cp("arbitrary"), name=name,
    )(*operands)


def _ffn_mid_fwd(name, u2, dww, dwb, l, exchange=None):
    _, S, F = u2.shape
    tm = _tile(S, FFN_MID_ROWS, BF16_ROWS)
    tc = _tile(F, 1408)
    n_f = F // tc
    nb = tm // HALO_S
    kf = FFN_CONV_WIDTH

    def body(u_ref, uh_ref, wg_ref, wv_ref, bg_ref, bv_ref, o_ref, ext):
        first = pl.program_id(1) == 0
        ext[:, pl.ds(0, HALO_S), :] = jnp.where(first, 0.0, uh_ref[...])
        ext[:, pl.ds(HALO_S, tm), :] = u_ref[...]
        rc = _tile(tm, ELT_ROWS, BF16_ROWS)

        def lane_chunk(ci, carry):
            lanes = pl.ds(pl.multiple_of(ci * LANES, LANES), LANES)
            taps = [[w_ref[k:k + 1, lanes] for k in range(kf)] for w_ref in (wg_ref, wv_ref)]
            bias = [b_ref[l:l + 1, lanes] for b_ref in (bg_ref, bv_ref)]
            for r0 in range(0, tm, rc):
                c = []
                for g in range(2):
                    acc = bias[g]
                    for k in range(kf):
                        acc = acc + taps[g][k] * ext[g, pl.ds(HALO_S - (kf - 1) + k + r0, rc), lanes]
                    c.append(acc)
                o_ref[pl.ds(r0, rc), lanes] = (c[0] * _sig(c[0]) * c[1]).astype(BF16)
            return carry

        lax.fori_loop(0, tc // LANES, lane_chunk, 0)

    n_l = dwb.shape[0]
    n_i = S // tm
    body, in_specs, out_specs, out_shape, scratch, operands, aliases = _with_exchange(
        exchange, body,
        [pl.BlockSpec((2, tm, tc), lambda j, i: (0, i, j)),
         pl.BlockSpec((2, HALO_S, tc), lambda j, i: (0, jnp.maximum(i * nb - 1, 0), j)),
         pl.BlockSpec((None, kf, tc), lambda j, i: (l, 0, j)),
         pl.BlockSpec((None, kf, tc), lambda j, i: (l, 0, j + n_f)),
         pl.BlockSpec((n_l, tc), lambda j, i: (0, j)),
         pl.BlockSpec((n_l, tc), lambda j, i: (0, j + n_f))],
        [pl.BlockSpec((tm, tc), lambda j, i: (i, j))], [_sds((S, F), BF16)],
        [pltpu.VMEM((2, HALO_S + tm, tc), F32)], [u2, u2, dww, dww, dwb, dwb],
        lambda: jnp.logical_and(pl.program_id(0) == 0, pl.program_id(1) == 0),
        lambda: jnp.logical_and(pl.program_id(0) == n_f - 1, pl.program_id(1) == n_i - 1))
    sem = "arbitrary" if exchange else "parallel"
    outs = _pcall(
        body, grid=(n_f, n_i), in_specs=in_specs, out_specs=out_specs, out_shape=out_shape, scratch_shapes=scratch,
        input_output_aliases=aliases, compiler_params=_cp(sem, sem), name=name,
    )(*operands)
    return outs if exchange else outs[0]


def _ffn_mid_bwd(name, u2, df, dww, dwb, l, exchange=None):
    _, S, F = u2.shape
    tm = _tile(S, FFN_MID_ROWS, BF16_ROWS)
    tc = _tile(F, 1408)
    n_f = F // tc
    nb = tm // HALO_S
    n_i = S // tm
    kf = FFN_CONV_WIDTH
    n = tm + HALO_S

    def body(u_ref, up_ref, un_ref, df_ref, dfn_ref, wg_ref, wv_ref, bg_ref, bv_ref,
             du_ref, dw_ref, db_ref, uext, dcext):
        i = pl.program_id(1)
        first, last = i == 0, i == n_i - 1

        @pl.when(first)
        def _():
            dw_ref[...] = jnp.zeros_like(dw_ref)
            db_ref[...] = jnp.zeros_like(db_ref)

        uext[:, pl.ds(0, HALO_S), :] = jnp.where(first, 0.0, up_ref[...])
        uext[:, pl.ds(HALO_S, tm), :] = u_ref[...]
        uext[:, pl.ds(HALO_S + tm, HALO_S), :] = un_ref[...]
        rc = _tile(tm, ELT_ROWS, BF16_ROWS)

        def lane_chunk(ci, carry):
            lanes = pl.ds(pl.multiple_of(ci * LANES, LANES), LANES)
            taps = [[w_ref[k:k + 1, lanes] for k in range(kf)] for w_ref in (wg_ref, wv_ref)]
            bias = [b_ref[l:l + 1, lanes] for b_ref in (bg_ref, bv_ref)]
            acc_w = [[jnp.zeros((SUBLANES, LANES), F32) for _ in range(kf)] for _ in range(2)]
            acc_b = [jnp.zeros((SUBLANES, LANES), F32) for _ in range(2)]
            for r0, rows in [(r, rc) for r in range(0, tm, rc)] + [(tm, HALO_S)]:
                shifted = [[uext[g, pl.ds(HALO_S - (kf - 1) + k + r0, rows), lanes] for k in range(kf)] for g in range(2)]
                conv = []
                for g in range(2):
                    acc = bias[g]
                    for k in range(kf):
                        acc = acc + taps[g][k] * shifted[g][k]
                    conv.append(acc)
                cg, cv = conv
                s = _sig(cg)
                dfe = df_ref[pl.ds(r0, rows), lanes] if r0 < tm else jnp.where(last, 0.0, dfn_ref[:, lanes])
                dc = [dfe * cv * (s * (1.0 + cg * (1.0 - s))), dfe * (cg * s)]
                for g in range(2):
                    dcext[g, pl.ds(r0, rows), lanes] = dc[g]
                    if r0 < tm:
                        acc_b[g] = acc_b[g] + _fold(dc[g])
                        for k in range(kf):
                            acc_w[g][k] = acc_w[g][k] + _fold(dc[g] * shifted[g][k])
            for r0 in range(0, tm, rc):
                for g in range(2):
                    du = taps[g][0] * dcext[g, pl.ds(r0 + kf - 1, rc), lanes]
                    for k in range(1, kf):
                        du = du + taps[g][k] * dcext[g, pl.ds(r0 + kf - 1 - k, rc), lanes]
                    du_ref[g, pl.ds(r0, rc), lanes] = du.astype(BF16)
            for g in range(2):
                db_ref[g, :, lanes] += _rowsum(acc_b[g])
                for k in range(kf):
                    dw_ref[g, k:k + 1, lanes] += _rowsum(acc_w[g][k])
            return carry

        lax.fori_loop(0, tc // LANES, lane_chunk, 0)

    n_l = dwb.shape[0]
    prev = lambda j, i: (0, jnp.maximum(i * nb - 1, 0), j)
    nxt = lambda j, i: (0, jnp.minimum((i + 1) * nb, S // HALO_S - 1), j)
    body, in_specs, out_specs, out_shape, scratch, operands, aliases = _with_exchange(
        exchange, body,
        [pl.BlockSpec((2, tm, tc), lambda j, i: (0, i, j)),
         pl.BlockSpec((2, HALO_S, tc), prev), pl.BlockSpec((2, HALO_S, tc), nxt),
         pl.BlockSpec((tm, tc), lambda j, i: (i, j)),
         pl.BlockSpec((HALO_S, tc), lambda j, i: nxt(j, i)[1:]),
         pl.BlockSpec((None, kf, tc), lambda j, i: (l, 0, j)),
         pl.BlockSpec((None, kf, tc), lambda j, i: (l, 0, j + n_f)),
         pl.BlockSpec((n_l, tc), lambda j, i: (0, j)),
         pl.BlockSpec((n_l, tc), lambda j, i: (0, j + n_f))],
        [pl.BlockSpec((2, tm, tc), lambda j, i: (0, i, j)),
         pl.BlockSpec((2, kf, tc), lambda j, i: (0, 0, j)),
         pl.BlockSpec((2, 1, tc), lambda j, i: (0, 0, j))],
        [_sds((2, S, F), BF16), _sds((2, kf, F), F32), _sds((2, 1, F), F32)],
        [pltpu.VMEM((2, HALO_S + n, tc), F32), pltpu.VMEM((2, n, tc), F32)],
        [u2, u2, u2, df, df, dww, dww, dwb, dwb],
        lambda: jnp.logical_and(pl.program_id(0) == 0, pl.program_id(1) == 0),
        lambda: jnp.logical_and(pl.program_id(0) == n_f - 1, pl.program_id(1) == n_i - 1))
    return _pcall(
        body, grid=(n_f, n_i), in_specs=in_specs, out_specs=out_specs, out_shape=out_shape, scratch_shapes=scratch,
        input_output_aliases=aliases, compiler_params=_cp("arbitrary" if exchange else "parallel", "arbitrary"), name=name,
    )(*operands)


def _head_sum_matrix():
    r = lax.broadcasted_iota(jnp.int32, (LANES, LANES), 0) // HEAD_DIM
    c = lax.broadcasted_iota(jnp.int32, (LANES, LANES), 1) // HEAD_DIM
    return (r == c).astype(BF16)


def _head_mean(x, ones):
    return _split_dot(x, ones) * (1.0 / HEAD_DIM)


def _qknorm_fwd(name, qkv, g2):
    S, D3 = qkv.shape
    D = D3 // 3
    tm = _tile(S, NORM_ROWS // 2, BF16_ROWS)
    scale = HEAD_DIM ** -0.5

    def body(q_ref, k_ref, v_ref, g_ref, qo_ref, ko_ref, vo_ref):
        ones = _head_sum_matrix()
        for cc in range(D // LANES):
            sl = slice(cc * LANES, (cc + 1) * LANES)
            for x_ref, o_ref, row, mult in ((q_ref, qo_ref, 0, scale), (k_ref, ko_ref, 1, 1.0)):
                x = x_ref[:, sl]
                r = lax.rsqrt(_head_mean(x * x, ones) + EPS)
                o_ref[:, sl] = ((x * r * g_ref[row:row + 1, :]).astype(BF16) * mult).astype(BF16)
        vo_ref[...] = v_ref[...].astype(BF16)

    col = lambda c: pl.BlockSpec((tm, D), lambda i: (i, c))
    out = pl.BlockSpec((tm, D), lambda i: (i, 0))
    return _pcall(
        body, grid=(S // tm,),
        in_specs=[col(0), col(1), col(2), pl.BlockSpec(g2.shape, lambda i: (0, 0))],
        out_specs=[out, out, out], out_shape=[_sds((S, D), BF16)] * 3,
        compiler_params=_cp("parallel"), name=name,
    )(qkv, qkv, qkv, g2)


def _qknorm_bwd(name, qkv, dq, dk, dv, g2):
    S, D3 = qkv.shape
    D = D3 // 3
    tm = _tile(S, NORM_ROWS // 2, BF16_ROWS)
    scale = HEAD_DIM ** -0.5

    def body(q_ref, k_ref, dq_ref, dk_ref, dv_ref, g_ref, o_ref, dg_ref):
        @pl.when(pl.program_id(0) == 0)
        def _():
            dg_ref[...] = jnp.zeros_like(dg_ref)

        ones = _head_sum_matrix()
        for cc in range(D // LANES):
            sl = slice(cc * LANES, (cc + 1) * LANES)
            for x_ref, d_ref, row, mult, base in ((q_ref, dq_ref, 0, scale, 0), (k_ref, dk_ref, 1, 1.0, D)):
                x = x_ref[:, sl]
                r = lax.rsqrt(_head_mean(x * x, ones) + EPS)
                xh = x * r
                dn = d_ref[:, sl] * mult
                dxh = dn * g_ref[row:row + 1, :]
                dx = r * (dxh - xh * _head_mean(dxh * xh, ones))
                o_ref[:, base + cc * LANES:base + (cc + 1) * LANES] = dx.astype(BF16)
                dg_ref[row:row + 1, :] += _rowsum(dn * xh)
        o_ref[:, 2 * D:3 * D] = dv_ref[...].astype(BF16)

    col = lambda c: pl.BlockSpec((tm, D), lambda i: (i, c))
    row = pl.BlockSpec((tm, D), lambda i: (i, 0))
    return _pcall(
        body, grid=(S // tm,),
        in_specs=[col(0), col(1), row, row, row, pl.BlockSpec(g2.shape, lambda i: (0, 0))],
        out_specs=[pl.BlockSpec((tm, D3), lambda i: (i, 0)), pl.BlockSpec((2, LANES), lambda i: (0, 0))],
        out_shape=[_sds((S, D3), BF16), _sds((2, LANES), F32)],
        compiler_params=_cp("arbitrary"), name=name,
    )(qkv, qkv, dq, dk, dv, g2)


def _attn_consts():
    t = ATTN_BLOCK
    row = lax.broadcasted_iota(jnp.int32, (t, t), 0)
    col = lax.broadcasted_iota(jnp.int32, (t, t), 1)
    lane = lax.broadcasted_iota(jnp.int32, (1, LANES), 1)
    heads = (lane < HEAD_DIM, lane >= HEAD_DIM)
    return row, col, heads


def _split_dot(x, m):
    n = x.shape[0]
    hi = x.astype(BF16)
    lo = (x - hi.astype(F32)).astype(BF16)
    both = jnp.dot(jnp.concatenate([hi, lo], axis=0), m, preferred_element_type=F32)
    return both[:n] + both[n:]


def _log_keep(z):
    return -(jnp.maximum(z, 0.0) + jnp.log(1.0 + jnp.exp(-jnp.abs(z))))


def _stack_heads(a, heads):
    t = ATTN_BLOCK
    zero = jnp.zeros((t, LANES), a.dtype)
    return jnp.concatenate([jnp.where(h, a[s * t:(s + 1) * t], zero) for s in range(a.shape[0] // t) for h in heads], axis=0)


def _side_by_side(a):
    t = ATTN_BLOCK
    return jnp.concatenate([jnp.concatenate([a[2 * s * t:(2 * s + 1) * t], a[(2 * s + 1) * t:(2 * s + 2) * t]], axis=1)
                            for s in range(a.shape[0] // (2 * t))], axis=0)


def _grow(a, rows, cols):
    z = jnp.zeros((rows, cols), F32)
    return z if a is None else jnp.concatenate([z, a], axis=0)


def _attn_fwd(name, qs, kn, vb, exchange=None):
    S, D = qs.shape
    t = ATTN_BLOCK
    tq = ATTN_SUB * t

    def body(q_ref, k_ref, v_ref, o_ref):
        for part in range(ATTN_TILES):
            rows = pl.ds(part * tq, tq)
            tile(ATTN_TILES * pl.program_id(1) + part, q_ref.at[rows, :], k_ref, v_ref, o_ref.at[rows, :])

    def tile(i, q_ref, k_ref, v_ref, o_ref):
        row, col, heads = _attn_consts()
        after_m = (row > col).astype(BF16)
        causal = col < row
        q_all = _stack_heads(q_ref[...], heads)

        def blocks(specs, r, acc):
            n_rows = q_all.shape[0]
            offs = [pl.multiple_of(j * t, t) for j, _, _ in specs]
            zs = [lax.dot_general(q_all[lo:], k_ref[pl.ds(off, t), :], NT, preferred_element_type=F32)
                  for off, (_, lo, _) in zip(offs, specs)]
            lks = []
            for z, (_, _, mask) in zip(zs, specs):
                lk = _log_keep(z)
                lks.append(lk if mask is None else jnp.where(mask, lk, 0.0))
            cums = [_split_dot(lk, after_m) for lk in lks]
            ws = []
            for z, lk, cum, (_, lo, mask) in zip(zs, lks, cums, specs):
                rows = n_rows - lo
                r = _grow(r, rows - (0 if r is None else r.shape[0]), 1) if r is None or r.shape[0] < rows else r
                w = jnp.exp(z + lk + cum + r)
                ws.append((w if mask is None else jnp.where(mask, w, 0.0)).astype(BF16))
                r = r + jnp.sum(lk, axis=1, keepdims=True)
            acc = jnp.zeros((n_rows // 2, LANES), F32) if acc is None else acc
            for w, off, (_, lo, _) in zip(ws, offs, specs):
                part = jnp.dot(_side_by_side(w), _stack_heads(v_ref[pl.ds(off, t), :], heads), preferred_element_type=F32)
                acc = acc + (part if lo == 0 else _grow(part, lo // 2, LANES))
            return r, acc

        def head(n_more):
            specs = [(ATTN_SUB * i + s, 2 * s * t,
                      jnp.concatenate([causal, causal] + [jnp.ones_like(causal)] * (2 * (ATTN_SUB - 1 - s)), axis=0))
                     for s in reversed(range(ATTN_SUB))]
            specs += [(ATTN_SUB * i - 1 - b, 0, None) for b in range(n_more)]
            return blocks(specs, None, None)

        r, acc = lax.cond(ATTN_SUB * i >= ATTN_MORE, lambda: head(ATTN_MORE), lambda: head(0))

        def cond(c):
            return jnp.logical_and(c[0] >= 0, jnp.max(c[1]) > EXP_UNDERFLOW)

        def step(c):
            r, a = blocks([(c[0], 0, None)], c[1], c[2])
            return c[0] - 1, r, a

        first = jnp.where(ATTN_SUB * i >= ATTN_MORE, ATTN_SUB * i - 1 - ATTN_MORE, ATTN_SUB * i - 1)
        o_ref[...] = lax.while_loop(cond, step, (first, r, acc))[2]

    n_hp = D // LANES
    blk = pl.BlockSpec((ATTN_TILES * tq, LANES), lambda hp, i: (i, hp))
    seq = pl.BlockSpec((S, LANES), lambda hp, i: (0, hp))
    n_i = S // (ATTN_TILES * tq)
    body, in_specs, out_specs, out_shape, scratch, operands, aliases = _with_exchange(
        exchange, body, [blk, seq, seq], [blk], [_sds((S, D), F32)], [], [qs, kn, vb],
        lambda: jnp.logical_and(pl.program_id(0) == 0, pl.program_id(1) == 0),
        lambda: jnp.logical_and(pl.program_id(0) == n_hp - 1, pl.program_id(1) == n_i - 1))
    outs = _pcall(
        body, grid=(n_hp, n_i), in_specs=in_specs, out_specs=out_specs, out_shape=out_shape, scratch_shapes=scratch,
        input_output_aliases=aliases, compiler_params=_cp("arbitrary" if exchange else "parallel", "arbitrary"), name=name,
    )(*operands)
    return outs if exchange else outs[0]


def _attn_bwd(name, qs, kn, vb, o, do, exchange=None):
    S, D = qs.shape
    t = ATTN_BLOCK
    tq = ATTN_SUB * t

    def body(q_ref, k_ref, v_ref, o_ref, do_ref, dq_ref, dk_ref, dv_ref):
        for part in range(ATTN_TILES):
            rows = pl.ds(part * tq, tq)
            tile(ATTN_TILES * pl.program_id(1) + part, q_ref.at[rows, :], k_ref, v_ref, o_ref.at[rows, :],
                 do_ref.at[rows, :], dq_ref.at[rows, :], dk_ref, dv_ref)

    def tile(i, q_ref, k_ref, v_ref, o_ref, do_ref, dq_ref, dk_ref, dv_ref):

        @pl.when(i == 0)
        def _():
            dk_ref[...] = jnp.zeros_like(dk_ref)
            dv_ref[...] = jnp.zeros_like(dv_ref)

        row, col, heads = _attn_consts()
        after_m = (row > col).astype(BF16)
        from_m = (row >= col).astype(BF16)
        causal = col < row
        q_all = _stack_heads(q_ref[...], heads)
        dob = do_ref[...].astype(BF16)
        do_all = _stack_heads(dob, heads)
        dsum_all = jnp.sum(_stack_heads(dob.astype(F32) * o_ref[...], heads), axis=1, keepdims=True)

        def blocks(specs, r, es, dq):
            n_rows = q_all.shape[0]
            offs = [pl.multiple_of(j * t, t) for j, _, _ in specs]
            masked = lambda x, mask: x if mask is None else jnp.where(mask, x, 0.0)
            top = lambda a, rows: a if a is not None and a.shape[0] == rows else _grow(a, rows - (0 if a is None else a.shape[0]), 1)
            zs = [lax.dot_general(q_all[lo:], k_ref[pl.ds(off, t), :], NT, preferred_element_type=F32)
                  for off, (_, lo, _) in zip(offs, specs)]
            gs = [lax.dot_general(do_all[lo:], v_ref[pl.ds(off, t), :], NT, preferred_element_type=F32)
                  for off, (_, lo, _) in zip(offs, specs)]
            lks = [masked(_log_keep(z), mask) for z, (_, _, mask) in zip(zs, specs)]
            cums = [_split_dot(lk, after_m) for lk in lks]
            ws, es_blk, sgs = [], [], []
            for z, g, lk, cum, (_, lo, mask) in zip(zs, gs, lks, cums, specs):
                r = top(r, n_rows - lo)
                ls = z + lk
                w = masked(jnp.exp(ls + cum + r), mask)
                ws.append(w.astype(BF16))
                es_blk.append(w * g)
                sgs.append(jnp.exp(ls))
                r = r + jnp.sum(lk, axis=1, keepdims=True)
            cum_es = [_split_dot(e, from_m) for e in es_blk]
            dzs = []
            for e, cum_e, sg, (_, lo, mask) in zip(es_blk, cum_es, sgs, specs):
                es = top(es, n_rows - lo)
                before = dsum_all[lo:] - (es + cum_e)
                dzs.append(masked(e - (e + before) * sg, mask).astype(BF16))
                es = es + jnp.sum(e, axis=1, keepdims=True)
            dq = jnp.zeros((n_rows // 2, LANES), F32) if dq is None else dq
            for dzb, w, off, (_, lo, _) in zip(dzs, ws, offs, specs):
                part = jnp.dot(_side_by_side(dzb), _stack_heads(k_ref[pl.ds(off, t), :], heads), preferred_element_type=F32)
                dq = dq + (part if lo == 0 else _grow(part, lo // 2, LANES))
                dk_ref[pl.ds(off, t), :] += lax.dot_general(dzb, q_all[lo:], TN, preferred_element_type=F32)
                dv_ref[pl.ds(off, t), :] += lax.dot_general(w, do_all[lo:], TN, preferred_element_type=F32)
            return r, es, dq

        def head(n_more):
            specs = [(ATTN_SUB * i + s, 2 * s * t,
                      jnp.concatenate([causal, causal] + [jnp.ones_like(causal)] * (2 * (ATTN_SUB - 1 - s)), axis=0))
                     for s in reversed(range(ATTN_SUB))]
            specs += [(ATTN_SUB * i - 1 - b, 0, None) for b in range(n_more)]
            return blocks(specs, None, None, None)

        r, es, dq = lax.cond(ATTN_SUB * i >= ATTN_MORE, lambda: head(ATTN_MORE), lambda: head(0))

        def cond(c):
            return jnp.logical_and(c[0] >= 0, jnp.max(c[1]) > EXP_UNDERFLOW)

        def step(c):
            r, es, a = blocks([(c[0], 0, None)], c[1], c[2], c[3])
            return c[0] - 1, r, es, a

        first = jnp.where(ATTN_SUB * i >= ATTN_MORE, ATTN_SUB * i - 1 - ATTN_MORE, ATTN_SUB * i - 1)
        dq_ref[...] = lax.while_loop(cond, step, (first, r, es, dq))[3]

    n_hp = D // LANES
    blk = pl.BlockSpec((ATTN_TILES * tq, LANES), lambda hp, i: (i, hp))
    seq = pl.BlockSpec((S, LANES), lambda hp, i: (0, hp))
    n_i = S // (ATTN_TILES * tq)
    body, in_specs, out_specs, out_shape, scratch, operands, aliases = _with_exchange(
        exchange, body, [blk, seq, seq, blk, blk], [blk, seq, seq], [_sds((S, D), F32)] * 3, [], [qs, kn, vb, o, do],
        lambda: jnp.logical_and(pl.program_id(0) == 0, pl.program_id(1) == 0),
        lambda: jnp.logical_and(pl.program_id(0) == n_hp - 1, pl.program_id(1) == n_i - 1))
    return _pcall(
        body, grid=(n_hp, n_i), in_specs=in_specs, out_specs=out_specs, out_shape=out_shape, scratch_shapes=scratch,
        input_output_aliases=aliases, compiler_params=_cp("arbitrary" if exchange else "parallel", "arbitrary"), name=name,
    )(*operands)


def _adamw(name, w, g, m, v):
    L, R, C = w.shape
    tr = _tile(R, 256, SUBLANES)
    c1 = 1.0 - ADAM_B1 ** ADAM_STEP
    c2 = 1.0 - ADAM_B2 ** ADAM_STEP

    def body(w_ref, g_ref, m_ref, v_ref, d_ref, mo_ref, vo_ref):
        gg = g_ref[...]
        mn = ADAM_B1 * m_ref[...] + (1.0 - ADAM_B1) * gg
        vn = ADAM_B2 * v_ref[...] + (1.0 - ADAM_B2) * (gg * gg)
        d_ref[...] = -ADAM_LR * ((mn / c1) / (jnp.sqrt(vn / c2) + ADAM_EPS) + ADAM_WD * w_ref[...])
        mo_ref[...] = mn
        vo_ref[...] = vn

    blk = pl.BlockSpec((None, tr, C), lambda l, i: (l, i, 0))
    return _pcall(
        body, grid=(L, R // tr), in_specs=[blk] * 4, out_specs=[blk] * 3, out_shape=[_sds(w.shape, F32)] * 3,
        compiler_params=_cp("parallel", "parallel"), name=name,
    )(w, g, m, v)


def _place():
    x, y, c = lax.axis_index("x"), lax.axis_index("y"), lax.axis_index("c")
    chips = [(1 - x, y), (x, 1 - y), (1 - x, 1 - y)]
    return x, y, c, chips


def _place_shard(name, w, j_idx):
    L, R, X = w.shape
    rh = R // 2
    tr = _tile(rh, 256, BF16_ROWS)

    def body(j_ref, w_ref, o_ref):
        o_ref[...] = w_ref[...].astype(BF16)

    return _pcall(
        body,
        grid_spec=pltpu.PrefetchScalarGridSpec(
            num_scalar_prefetch=1, grid=(L, 2, rh // tr),
            in_specs=[pl.BlockSpec((None, None, tr, X), lambda l, h, i, j_ref: (l, h, i, 0))],
            out_specs=pl.BlockSpec((None, None, None, tr, X), lambda l, h, i, j_ref: (l, j_ref[0], h, i, 0))),
        out_shape=_sds((L, N_CHIPS, 2, rh, X), BF16), compiler_params=_cp("parallel", "parallel", "parallel"), name=name,
    )(j_idx, w.reshape(L, 2, rh, X))


def _all_gather_weights(bufs, spans, small_ws):
    n_big, n_small = len(bufs), len(small_ws)
    n_in = n_big + n_small
    layers = [pl.ds(l0, n) for l0, n in spans]

    def body(*refs):
        ins, outs = refs[:n_in], refs[n_in:2 * n_in]
        send_sems, recv_sems, local_sems = refs[2 * n_in:]
        x, y, c, chips = _place()
        j_me = 2 * x + y
        j_of = [2 * cx + cy for cx, cy in chips]
        sibling = (x, y, 1 - c)

        def remote(src, dst, s, to):
            return pltpu.make_async_remote_copy(src_ref=src, dst_ref=dst, send_sem=send_sems.at[s], recv_sem=recv_sems.at[s],
                                                device_id=to, device_id_type=MESH)

        started = []
        for t in range(n_big, n_in):
            loc = pltpu.make_async_copy(ins[t], outs[t].at[:, j_me], local_sems.at[t - n_big])
            loc.start()
            started.append(loc)
        first = []
        for t in range(n_big):
            mine = outs[t].at[layers[t], j_me, c]
            for k in range(3):
                first.append(remote(mine, mine, 6 * t + k, (*chips[k], c)))
        for t in range(n_big, n_in):
            for k in range(3):
                first.append(remote(ins[t], outs[t].at[:, j_me], 6 * n_big + 3 * (t - n_big) + k, (*chips[k], c)))
        for cp in first:
            cp.start()
        passed = []
        for t in range(n_big):
            for k in range(3):
                landed = outs[t].at[layers[t], j_of[k], c]
                remote(landed, landed, 6 * t + k, (*chips[k], c)).wait_recv()
                fwd = remote(landed, landed, 6 * t + 3 + k, sibling)
                fwd.start()
                passed.append(fwd)
        for t in range(n_big):
            for k in range(3):
                other = outs[t].at[layers[t], j_of[k], 1 - c]
                remote(other, other, 6 * t + 3 + k, sibling).wait_recv()
        for t in range(n_big, n_in):
            for k in range(3):
                dst = outs[t].at[:, j_of[k]]
                remote(dst, dst, 6 * n_big + 3 * (t - n_big) + k, (*chips[k], c)).wait_recv()
        for cp in first + passed:
            cp.wait_send()
        for loc in started:
            loc.wait()

    out_shape = [_sds(b.shape, b.dtype) for b in bufs]
    out_shape += [_sds((w.shape[0], N_CHIPS) + w.shape[1:], w.dtype) for w in small_ws]
    n_sem = 6 * n_big + 3 * n_small
    outs = _pcall(
        body, in_specs=[ANY] * n_in, out_specs=[ANY] * n_in, out_shape=out_shape,
        input_output_aliases={t: t for t in range(n_big)},
        scratch_shapes=[pltpu.SemaphoreType.DMA((n_sem,)), pltpu.SemaphoreType.DMA((n_sem,)), pltpu.SemaphoreType.DMA((n_small,))],
        name="all_gather_weights",
    )(*bufs, *small_ws)
    return outs[:n_big], outs[n_big:]


class _Exchange:
    def __init__(self, operands, out_shapes, n_sems, copies, in_place=False):
        self.operands, self.out_shapes, self.n_sems, self.copies = list(operands), list(out_shapes), n_sems, copies
        self.aliases = {t: t for t in range(len(self.operands))} if in_place else {}

    @property
    def scratch(self):
        return [pltpu.SemaphoreType.DMA((self.n_sems,)), pltpu.SemaphoreType.DMA((self.n_sems,))]

    def split(self, refs):
        n_in, n_out = len(self.operands), len(self.out_shapes)
        return refs[:n_in], refs[n_in:n_in + n_out]

    def start(self, ins, outs, sems):
        for cp in self.copies(ins, outs, *sems):
            cp.start()

    def wait(self, ins, outs, sems):
        for cp in self.copies(ins, outs, *sems):
            cp.wait()


def _run_exchange(name, ex):
    n_in, n_out = len(ex.operands), len(ex.out_shapes)

    def body(*refs):
        ins, outs, sems = refs[:n_in], refs[n_in:n_in + n_out], refs[n_in + n_out:]
        ex.start(ins, outs, sems)
        ex.wait(ins, outs, sems)

    return _pcall(body, in_specs=[ANY] * n_in, out_specs=[ANY] * n_out, out_shape=ex.out_shapes, scratch_shapes=ex.scratch,
                  input_output_aliases=ex.aliases, name=name)(*ex.operands)


def _gather_chips_exchange(bufs, spans):
    def copies(ins, outs, send_sems, recv_sems):
        x, y, c, chips = _place()
        cps = []
        for t, (l0, n) in enumerate(spans):
            mine = outs[t].at[pl.ds(l0, n), 2 * x + y, c]
            cps += [pltpu.make_async_remote_copy(src_ref=mine, dst_ref=mine, send_sem=send_sems.at[3 * t + k],
                                                 recv_sem=recv_sems.at[3 * t + k], device_id=(cx, cy, c), device_id_type=MESH)
                    for k, (cx, cy) in enumerate(chips)]
        return cps

    return _Exchange(bufs, [_sds(b.shape, b.dtype) for b in bufs], 3 * len(bufs), copies, in_place=True)


def _gather_cores_exchange(bufs, spans):
    def copies(ins, outs, send_sems, recv_sems):
        x, y, c, chips = _place()
        cps = []
        for t, (l0, n) in enumerate(spans):
            for k, (cx, cy) in enumerate(chips):
                part = outs[t].at[pl.ds(l0, n), 2 * cx + cy, c]
                cps.append(pltpu.make_async_remote_copy(src_ref=part, dst_ref=part, send_sem=send_sems.at[3 * t + k],
                                                        recv_sem=recv_sems.at[3 * t + k], device_id=(x, y, 1 - c),
                                                        device_id_type=MESH))
        return cps

    return _Exchange(bufs, [_sds(b.shape, b.dtype) for b in bufs], 3 * len(bufs), copies, in_place=True)


def _core_halves_exchange(grads, spans):
    def copies(ins, outs, send_sems, recv_sems):
        x, y, c, _ = _place()
        return [pltpu.make_async_remote_copy(src_ref=ins[t].at[pl.ds(l0, n), :, 1 - c], dst_ref=outs[t],
                                             send_sem=send_sems.at[t], recv_sem=recv_sems.at[t], device_id=(x, y, 1 - c),
                                             device_id_type=MESH) for t, (l0, n) in enumerate(spans)]

    shapes = [_sds((n, g.shape[1], g.shape[3], g.shape[4]), F32) for g, (_, n) in zip(grads, spans)]
    return _Exchange(grads, shapes, len(grads), copies)


def _add_core_halves(name, g, a, c_idx, l0):
    _, nj, _, rh, X = g.shape
    L = a.shape[0]
    tr = _tile(rh, 256, BF16_ROWS)

    def body(c_ref, g_ref, a_ref, o_ref, ob_ref):
        s = g_ref[...] + a_ref[...]
        o_ref[...] = s
        ob_ref[...] = s.astype(BF16)

    blk = pl.BlockSpec((None, None, tr, X), lambda l, j, i, c_ref: (l, j, i, 0))
    return _pcall(
        body,
        grid_spec=pltpu.PrefetchScalarGridSpec(
            num_scalar_prefetch=1, grid=(L, nj, rh // tr),
            in_specs=[pl.BlockSpec((None, None, None, tr, X), lambda l, j, i, c_ref: (l + l0, j, c_ref[0], i, 0)), blk],
            out_specs=[blk, blk]),
        out_shape=[_sds((L, nj, rh, X), F32), _sds((L, nj, rh, X), BF16)],
        compiler_params=_cp("parallel", "parallel", "parallel"), name=name,
    )(c_idx, g, a)


def _chip_shards_exchange(parts):
    def copies(ins, outs, send_sems, recv_sems):
        x, y, c, chips = _place()
        return [pltpu.make_async_remote_copy(
            src_ref=ins[t].at[:, 2 * cx + cy], dst_ref=outs[t].at[k], send_sem=send_sems.at[3 * t + k],
            recv_sem=recv_sems.at[3 * t + k], device_id=(cx, cy, c), device_id_type=MESH)
            for t in range(len(parts)) for k, (cx, cy) in enumerate(chips)]

    shapes = [_sds((3, p.shape[0], p.shape[2], p.shape[3]), p.dtype) for p in parts]
    return _Exchange(parts, shapes, 3 * len(parts), copies)


def _add_chip_shards(name, p, b, jc_idx, l0, n_layers, buf):
    n, _, rh, X = p.shape
    tr = _tile(rh, 256, BF16_ROWS)

    def body(jc_ref, p_ref, b_ref, *rest):
        rest[-1][...] = ((p_ref[...] + b_ref[0].astype(F32)) + b_ref[1].astype(F32)) + b_ref[2].astype(F32)

    in_specs = [pl.BlockSpec((None, None, tr, X), lambda l, i, jc: (l, jc[0], i, 0)),
                pl.BlockSpec((3, None, tr, X), lambda l, i, jc: (0, l, i, 0))]
    operands = [jc_idx, p, b]
    if buf is not None:
        in_specs.append(ANY)
        operands.append(buf)
    return _pcall(
        body,
        grid_spec=pltpu.PrefetchScalarGridSpec(
            num_scalar_prefetch=1, grid=(n, rh // tr), in_specs=in_specs,
            out_specs=pl.BlockSpec((None, None, tr, X), lambda l, i, jc: (l + l0, jc[1], i, 0))),
        out_shape=_sds((n_layers, 2, rh, X), F32), input_output_aliases={3: 0} if buf is not None else {},
        compiler_params=_cp("parallel", "parallel"), name=name,
    )(*operands)


def _join_core_halves(bufs):
    n = len(bufs)

    def body(*refs):
        outs = refs[n:2 * n]
        send_sems, recv_sems = refs[2 * n:]
        x, y, c, _ = _place()
        cps = [pltpu.make_async_remote_copy(src_ref=outs[t].at[:, c], dst_ref=outs[t].at[:, c], send_sem=send_sems.at[t],
                                            recv_sem=recv_sems.at[t], device_id=(x, y, 1 - c), device_id_type=MESH)
               for t in range(n)]
        for cp in cps:
            cp.start()
        for t in range(n):
            pltpu.make_async_remote_copy(src_ref=outs[t].at[:, c], dst_ref=outs[t].at[:, 1 - c], send_sem=send_sems.at[t],
                                         recv_sem=recv_sems.at[t], device_id=(x, y, 1 - c), device_id_type=MESH).wait()

    outs = _pcall(
        body, in_specs=[ANY] * n, out_specs=[ANY] * n, out_shape=[_sds(b.shape, F32) for b in bufs],
        input_output_aliases={t: t for t in range(n)},
        scratch_shapes=[pltpu.SemaphoreType.DMA((n,)), pltpu.SemaphoreType.DMA((n,))],
        name="grad_join_core_halves",
    )(*bufs)
    return [o.reshape(o.shape[0], 2 * o.shape[2], o.shape[3]) for o in outs]


def _all_reduce_small(packed):
    R, C = packed.shape

    def body(x_ref, o_ref, slots, send_sems, recv_sems):
        x, y, c, _ = _place()
        me = 4 * x + 2 * y + c
        slots[me] = x_ref[...]
        cps = []
        for d in range(N_DEV):
            to = (d // 4, (d // 2) % 2, d % 2)
            cp = pltpu.make_async_remote_copy(src_ref=x_ref, dst_ref=slots.at[me], send_sem=send_sems.at[d],
                                              recv_sem=recv_sems.at[me], device_id=to, device_id_type=MESH)
            cps.append(cp)

            @pl.when(d != me)
            def _():
                cp.start()

        for d in range(N_DEV):
            @pl.when(d != me)
            def _():
                pltpu.make_async_remote_copy(src_ref=x_ref, dst_ref=slots.at[d], send_sem=send_sems.at[d],
                                             recv_sem=recv_sems.at[d], device_id=(x, y, c), device_id_type=MESH).wait_recv()
                cps[d].wait_send()

        acc = slots[0]
        for d in range(1, N_DEV):
            acc = acc + slots[d]
        o_ref[...] = acc

    vm = pl.BlockSpec(memory_space=pltpu.VMEM)
    return _pcall(
        body, in_specs=[vm], out_specs=vm, out_shape=_sds((R, C), F32),
        scratch_shapes=[pltpu.VMEM((N_DEV, R, C), F32), pltpu.SemaphoreType.DMA((N_DEV,)), pltpu.SemaphoreType.DMA((N_DEV,))],
        compiler_params=pltpu.CompilerParams(vmem_limit_bytes=VMEM_LIMIT_BYTES), name="all_reduce_small",
    )(packed)


PACK = SUBLANES * LANES


def _pack(arrays):
    flat = []
    for a in arrays:
        v = a.reshape(-1)
        flat.append(jnp.pad(v, (0, (-v.shape[0]) % PACK)))
    return jnp.concatenate(flat).reshape(-1, LANES)


def _unpack(packed, shapes):
    flat = packed.reshape(-1)
    out, pos = [], 0
    for s in shapes:
        n = 1
        for d in s:
            n *= d
        out.append(flat[pos:pos + n].reshape(s))
        pos += n + (-n) % PACK
    return out


def kernel(x, mix_norm_g, ffn_norm_g, conv_w_in, conv_a_dw_w, conv_a_dw_b, conv_a_ln_g, conv_a_ln_b, conv_b_dw_w, conv_w_out, attn_w_qkv, attn_q_g, attn_k_g, attn_w_o, ffn_w_up, ffn_dw_w, ffn_dw_b, ffn_w_down, loss_target, m_mix_norm_g, m_ffn_norm_g, m_conv_w_in, m_conv_a_dw_w, m_conv_a_dw_b, m_conv_a_ln_g, m_conv_a_ln_b, m_conv_b_dw_w, m_conv_w_out, m_attn_w_qkv, m_attn_q_g, m_attn_k_g, m_attn_w_o, m_ffn_w_up, m_ffn_dw_w, m_ffn_dw_b, m_ffn_w_down, v_mix_norm_g, v_ffn_norm_g, v_conv_w_in, v_conv_a_dw_w, v_conv_a_dw_b, v_conv_a_ln_g, v_conv_a_ln_b, v_conv_b_dw_w, v_conv_w_out, v_attn_w_qkv, v_attn_q_g, v_attn_k_g, v_attn_w_o, v_ffn_w_up, v_ffn_dw_w, v_ffn_dw_b, v_ffn_w_down):
    depth = mix_norm_g.shape[0]
    n_even, n_odd = conv_w_in.shape[0], attn_w_qkv.shape[0]
    S, D = x.shape[1], x.shape[2]
    dg = D // 2
    x0 = x.reshape(S, D)
    target = loss_target.reshape(S, D)
    j_me = 2 * lax.axis_index("x") + lax.axis_index("y")
    c_me = lax.axis_index("c")
    j_idx = j_me.astype(jnp.int32).reshape(1)
    c_idx = c_me.astype(jnp.int32).reshape(1)

    col_names = ["conv_w_in", "attn_w_qkv", "ffn_w_up"]
    row_names = ["conv_w_out", "attn_w_o", "ffn_w_down"]
    local = dict(conv_w_in=conv_w_in, attn_w_qkv=attn_w_qkv, ffn_w_up=ffn_w_up, conv_w_out=conv_w_out, attn_w_o=attn_w_o,
                 ffn_w_down=ffn_w_down)
    gbuf = {n: _place_shard(f"place_{n}", local[n], j_idx) for n in col_names + row_names}

    def weights_of(layer):
        mixer = ("conv_w_in", "conv_w_out") if layer % 2 == 0 else ("attn_w_qkv", "attn_w_o")
        return {mixer[0]: (layer // 2, 1), mixer[1]: (layer // 2, 1), "ffn_w_up": (layer, 1), "ffn_w_down": (layer, 1)}

    def w_col(n):
        return gbuf[n].reshape(gbuf[n].shape[0], N_CHIPS, -1, gbuf[n].shape[4])

    def w_row(n):
        return gbuf[n].reshape(gbuf[n].shape[0], -1, gbuf[n].shape[4])

    def carried(kernel_out, make_exchange, group):
        if not group:
            return kernel_out(None)
        out, *new = kernel_out(make_exchange([gbuf[n] for n in group], list(group.values())))
        gbuf.update(zip(group, new))
        return out

    first = {"conv_w_in": (0, 1), "conv_w_out": (0, 1)}
    ffn_first = {"ffn_w_up": (0, 1), "ffn_w_down": (0, 1)}
    outs, (a_dw, b_dw, f_dw) = _all_gather_weights([gbuf[n] for n in first], list(first.values()),
                                                   [conv_a_dw_w, conv_b_dw_w, ffn_dw_w])
    gbuf.update(zip(first, outs))
    unshard = lambda a: jnp.moveaxis(a, 1, 2).reshape(a.shape[0], a.shape[2], N_CHIPS * a.shape[3])
    a_dw, b_dw, f_dw = unshard(a_dw), unshard(b_dw), unshard(f_dw)
    qk_gain = [jnp.stack([jnp.tile(attn_q_g[i], LANES // HEAD_DIM), jnp.tile(attn_k_g[i], LANES // HEAD_DIM)])
               for i in range(n_odd)]

    saved = []
    xc = x0
    for layer in range(depth):
        i = layer // 2
        tag = f"l{layer}"
        s = {"x_in": xc}
        here = weights_of(layer) if layer else None
        h = carried(lambda ex: _rms_fwd(f"rms_mix_fwd_{tag}", xc, mix_norm_g, layer, ex), _gather_cores_exchange, here)
        s["h"] = h
        if layer % 2 == 0:
            p = _mm_fwd(f"conv_in_fwd_{tag}", h, w_col("conv_w_in"), i, colshard=True)
            ab = carried(lambda ex: _convmix_fwd(f"convmix_fwd_{tag}", p, a_dw, conv_a_dw_b, conv_a_ln_g, conv_a_ln_b, b_dw,
                                                 i, ex), _gather_chips_exchange, None if layer else ffn_first)
            xm = _mm_fwd(f"conv_out_fwd_{tag}", ab, w_row("conv_w_out"), i, colshard=False, res=xc)
            s.update(p=p, ab=ab)
        else:
            qkv = _mm_fwd(f"attn_qkv_fwd_{tag}", h, w_col("attn_w_qkv"), i, colshard=True)
            qs, kn, vb = _qknorm_fwd(f"qknorm_fwd_{tag}", qkv, qk_gain[i])
            o = _attn_fwd(f"attn_fwd_{tag}", qs, kn, vb)
            xm = _mm_fwd(f"attn_out_fwd_{tag}", o, w_row("attn_w_o"), i, colshard=False, res=xc)
            s.update(qkv=qkv, qs=qs, kn=kn, vb=vb, o=o)
        s["x_mid"] = xm
        h2 = carried(lambda ex: _rms_fwd(f"rms_ffn_fwd_{tag}", xm, ffn_norm_g, layer, ex), _gather_cores_exchange,
                     None if layer else ffn_first)
        u2 = _mm_fwd(f"ffn_up_fwd_{tag}", h2, w_col("ffn_w_up"), layer, colshard=True, out_split=2)
        f = carried(lambda ex: _ffn_mid_fwd(f"ffn_mid_fwd_{tag}", u2, f_dw, ffn_dw_b, layer, ex), _gather_chips_exchange,
                    weights_of(layer + 1) if layer + 1 < depth else None)
        xc = _mm_fwd(f"ffn_down_fwd_{tag}", f, w_row("ffn_w_down"), layer, colshard=False, res=xm)
        s.update(h2=h2, u2=u2, f=f)
        saved.append(s)

    dx, loss_tile = _loss_fwd_bwd("loss", xc, target)

    w_in, w_qkv, w_up = w_col("conv_w_in"), w_col("attn_w_qkv"), w_col("ffn_w_up")
    w_out, w_o, w_down = w_row("conv_w_out"), w_row("attn_w_o"), w_row("ffn_w_down")
    g_up = g_down = g_in = g_out = g_qkv = g_o = None
    big_names = col_names + row_names

    def halves_view(n, g):
        if n in col_names:
            return g.reshape(g.shape[0], N_CHIPS, 2, g.shape[2] // 2, g.shape[3])
        return g.reshape(g.shape[0], N_CHIPS, 2, g.shape[1] // (2 * N_CHIPS), g.shape[2])

    ffn_of_0 = {"ffn_w_up": (0, 1), "ffn_w_down": (0, 1)}
    mixer_of_0 = {"conv_w_in": (0, 1), "conv_w_out": (0, 1)}
    summed_parts = {n: [] for n in big_names}

    def stacks():
        return {"conv_w_in": g_in, "attn_w_qkv": g_qkv, "ffn_w_up": g_up, "conv_w_out": g_out, "attn_w_o": g_o,
                "ffn_w_down": g_down}

    def core_exchange(group):
        return _core_halves_exchange([halves_view(n, stacks()[n]) for n in group], list(group.values()))

    def chip_exchange(tag, arrived):
        sums, parts = [], []
        for group, from_sibling in arrived:
            for n, a in zip(group, from_sibling):
                f32_sum, bf16_sum = _add_core_halves(f"grad_add_core_{n}_{tag}_{group[n][0]}", halves_view(n, stacks()[n]), a,
                                                     c_idx, group[n][0])
                sums.append((n, group[n][0], f32_sum))
                parts.append(bf16_sum)
        return _chip_shards_exchange(parts), sums

    def record(sums, from_chips):
        for (n, l0, f32_sum), b in zip(sums, from_chips):
            summed_parts[n].append((l0, f32_sum, b))

    d_mix_g, d_ffn_g = [None] * depth, [None] * depth
    d_ffn_dw_w, d_ffn_dw_b = [None] * depth, [None] * depth
    d_a_dw_w, d_a_dw_b, d_a_ln_g, d_a_ln_b, d_b_dw_w = ([None] * n_even for _ in range(5))
    d_q_g, d_k_g = [None] * n_odd, [None] * n_odd
    for layer in reversed(range(depth)):
        i = layer // 2
        tag = f"l{layer}"
        s = saved[layer]
        df = _mm_dgrad(f"ffn_down_dgrad_{tag}", dx, w_down, layer, colshard=False)
        g_down = _mm_wgrad(f"ffn_down_wgrad_{tag}", s["f"], dx, layer, depth, g_down, colshard=False)
        above = weights_of(layer + 1) if layer + 1 < depth else None
        arrived = []
        du2, dww, dwb, *from_sibling = _ffn_mid_bwd(f"ffn_mid_bwd_{tag}", s["u2"], df, f_dw, ffn_dw_b, layer,
                                                    core_exchange(above) if above else None)
        if above:
            arrived.append((above, from_sibling))
        d_ffn_dw_w[layer] = jnp.moveaxis(dww, 0, 1).reshape(FFN_CONV_WIDTH, -1)
        d_ffn_dw_b[layer] = dwb.reshape(-1)
        dh2 = _mm_dgrad(f"ffn_up_dgrad_{tag}", du2, w_up, layer, colshard=True)
        g_up = _mm_wgrad(f"ffn_up_wgrad_{tag}", s["h2"], du2, layer, depth, g_up, colshard=True)
        dx, dg_, *from_sibling = _rms_bwd(f"rms_ffn_bwd_{tag}", s["x_mid"], ffn_norm_g, layer, dh2, dx,
                                          core_exchange(ffn_of_0) if layer == 0 else None)
        if layer == 0:
            arrived.append((ffn_of_0, from_sibling))
        d_ffn_g[layer] = dg_.reshape(-1)
        if layer % 2 == 0:
            dab = _mm_dgrad(f"conv_out_dgrad_{tag}", dx, w_out, i, colshard=False)
            g_out = _mm_wgrad(f"conv_out_wgrad_{tag}", s["ab"], dx, i, n_even, g_out, colshard=False)
            chip_ex, sums = chip_exchange(tag, arrived) if arrived else (None, [])
            dp, daw, dab_b, dlg, dlb, dbw, *from_chips = _convmix_bwd(
                f"convmix_bwd_{tag}", s["p"], dab, a_dw, conv_a_dw_b, conv_a_ln_g, conv_a_ln_b, b_dw, i, chip_ex)
            record(sums, from_chips)
            d_a_dw_w[i], d_a_dw_b[i], d_a_ln_g[i], d_a_ln_b[i], d_b_dw_w[i] = (
                daw, dab_b.reshape(-1), dlg.reshape(-1), dlb.reshape(-1), dbw)
            dh = _mm_dgrad(f"conv_in_dgrad_{tag}", dp, w_in, i, colshard=True)
            g_in = _mm_wgrad(f"conv_in_wgrad_{tag}", s["h"], dp, i, n_even, g_in, colshard=True)
        else:
            do = _mm_dgrad(f"attn_out_dgrad_{tag}", dx, w_o, i, colshard=False)
            g_o = _mm_wgrad(f"attn_out_wgrad_{tag}", s["o"], dx, i, n_odd, g_o, colshard=False)
            chip_ex, sums = chip_exchange(tag, arrived) if arrived else (None, [])
            dq, dk, dv, *from_chips = _attn_bwd(f"attn_bwd_{tag}", s["qs"], s["kn"], s["vb"], s["o"], do, chip_ex)
            record(sums, from_chips)
            dqkv, dgain = _qknorm_bwd(f"qknorm_bwd_{tag}", s["qkv"], dq, dk, dv, qk_gain[i])
            d_q_g[i] = dgain[0, :HEAD_DIM] + dgain[0, HEAD_DIM:]
            d_k_g[i] = dgain[1, :HEAD_DIM] + dgain[1, HEAD_DIM:]
            dh = _mm_dgrad(f"attn_qkv_dgrad_{tag}", dqkv, w_qkv, i, colshard=True)
            g_qkv = _mm_wgrad(f"attn_qkv_wgrad_{tag}", s["h"], dqkv, i, n_odd, g_qkv, colshard=True)
        dx, dg_ = _rms_bwd(f"rms_mix_bwd_{tag}", s["x_in"], mix_norm_g, layer, dh, dx)
        d_mix_g[layer] = dg_.reshape(-1)
    grad_x = dx.reshape(1, S, D)

    small = {
        "mix_norm_g": jnp.stack(d_mix_g), "ffn_norm_g": jnp.stack(d_ffn_g),
        "conv_a_dw_w": jnp.stack(d_a_dw_w), "conv_a_dw_b": jnp.stack(d_a_dw_b),
        "conv_a_ln_g": jnp.stack(d_a_ln_g), "conv_a_ln_b": jnp.stack(d_a_ln_b),
        "conv_b_dw_w": jnp.stack(d_b_dw_w), "attn_q_g": jnp.stack(d_q_g), "attn_k_g": jnp.stack(d_k_g),
        "ffn_dw_w": jnp.stack(d_ffn_dw_w), "ffn_dw_b": jnp.stack(d_ffn_dw_b),
    }
    small_names = list(small)
    summed = _all_reduce_small(_pack([loss_tile] + [small[n] for n in small_names]))
    parts = _unpack(summed, [loss_tile.shape] + [small[n].shape for n in small_names])
    loss = parts[0][0, 0]
    small_g = dict(zip(small_names, parts[1:]))
    for n in ("conv_a_dw_w", "conv_b_dw_w", "ffn_dw_w"):
        cs = small_g[n].shape[2] // N_CHIPS
        small_g[n] = lax.dynamic_slice_in_dim(small_g[n], j_me * cs, cs, axis=2)

    from_sibling = _run_exchange("grad_exchange_core_halves", core_exchange(mixer_of_0))
    chip_ex, sums = chip_exchange("last", [(mixer_of_0, from_sibling)])
    record(sums, _run_exchange("grad_exchange_chip_shards", chip_ex))
    jc_idx = jnp.concatenate([j_idx, c_idx])
    totals = {}
    for n in big_names:
        total = None
        for l0, p, b in summed_parts[n]:
            total = _add_chip_shards(f"grad_add_chips_{n}_{l0}", p, b, jc_idx, l0, stacks()[n].shape[0], total)
        totals[n] = total
    big_g = dict(zip(big_names, _join_core_halves([totals[n] for n in big_names])))

    weights = dict(mix_norm_g=mix_norm_g, ffn_norm_g=ffn_norm_g, conv_w_in=conv_w_in, conv_a_dw_w=conv_a_dw_w, conv_a_dw_b=conv_a_dw_b, conv_a_ln_g=conv_a_ln_g, conv_a_ln_b=conv_a_ln_b, conv_b_dw_w=conv_b_dw_w, conv_w_out=conv_w_out, attn_w_qkv=attn_w_qkv, attn_q_g=attn_q_g, attn_k_g=attn_k_g, attn_w_o=attn_w_o, ffn_w_up=ffn_w_up, ffn_dw_w=ffn_dw_w, ffn_dw_b=ffn_dw_b, ffn_w_down=ffn_w_down)
    m_in = dict(mix_norm_g=m_mix_norm_g, ffn_norm_g=m_ffn_norm_g, conv_w_in=m_conv_w_in, conv_a_dw_w=m_conv_a_dw_w, conv_a_dw_b=m_conv_a_dw_b, conv_a_ln_g=m_conv_a_ln_g, conv_a_ln_b=m_conv_a_ln_b, conv_b_dw_w=m_conv_b_dw_w, conv_w_out=m_conv_w_out, attn_w_qkv=m_attn_w_qkv, attn_q_g=m_attn_q_g, attn_k_g=m_attn_k_g, attn_w_o=m_attn_w_o, ffn_w_up=m_ffn_w_up, ffn_dw_w=m_ffn_dw_w, ffn_dw_b=m_ffn_dw_b, ffn_w_down=m_ffn_w_down)
    v_in = dict(mix_norm_g=v_mix_norm_g, ffn_norm_g=v_ffn_norm_g, conv_w_in=v_conv_w_in, conv_a_dw_w=v_conv_a_dw_w, conv_a_dw_b=v_conv_a_dw_b, conv_a_ln_g=v_conv_a_ln_g, conv_a_ln_b=v_conv_a_ln_b, conv_b_dw_w=v_conv_b_dw_w, conv_w_out=v_conv_w_out, attn_w_qkv=v_attn_w_qkv, attn_q_g=v_attn_q_g, attn_k_g=v_attn_k_g, attn_w_o=v_attn_w_o, ffn_w_up=v_ffn_w_up, ffn_dw_w=v_ffn_dw_w, ffn_dw_b=v_ffn_dw_b, ffn_w_down=v_ffn_w_down)
    order = list(weights)
    grads, delta, new_m, new_v = {}, {}, {}, {}
    for n in big_names:
        grads[n] = big_g[n]
        delta[n], new_m[n], new_v[n] = _adamw(f"adamw_{n}", weights[n], big_g[n], m_in[n], v_in[n])
    shapes = [weights[n].shape for n in small_names]
    packed = [_pack([d[n] for n in small_names]) for d in (weights, small_g, m_in, v_in)]
    upd = _adamw("adamw_small", *[p[None] for p in packed])
    for out, res in zip((delta, new_m, new_v), upd):
        out.update(zip(small_names, _unpack(res[0], shapes)))
    grads.update({n: small_g[n].reshape(weights[n].shape) for n in small_names})
    return (loss, grad_x, *[grads[n] for n in order], *[delta[n] for n in order], *[new_m[n] for n in order],
            *[new_v[n] for n in order])
```
